```python
import math
import jax, jax.numpy as jnp
from jax import lax
import numpy as np

D_MODEL = 1024
BATCH = 4
SEQ = 4096
DEPTH = 2

N_META = 16
EPS = 1e-6
ATT_HEADS = 4
ATT_QK_DIM = 64
ATT_V_DIM = 2 * ATT_QK_DIM
ATT_QK_COLS = 2 * ATT_HEADS * ATT_QK_DIM
ATT_WIDTH = ATT_HEADS * ATT_V_DIM
Q_BLOCK = 128
REL_BUCKETS = 32
REL_MAX_DIST = 128
CONV_WIDTH = 512
CONV_K = 3
HGRN_HEADS = 4
HGRN_DK = 128
HGRN_DV = 128
HGRN_WIDTH = HGRN_HEADS * HGRN_DV
HGRN_CHUNK = 64
N_BRANCH = 3
BRANCH_WIDTH = 512
IN_SPLITS = (ATT_QK_COLS, ATT_QK_COLS, ATT_WIDTH,
             CONV_WIDTH, CONV_WIDTH, CONV_WIDTH,
             HGRN_HEADS * HGRN_DK, HGRN_HEADS * HGRN_DK, HGRN_WIDTH, HGRN_WIDTH,
             N_BRANCH * D_MODEL)
IN_COLS = sum(IN_SPLITS)
D_FF_DENSE = 2816
N_EXPERTS = 8
TOP_K = 2
D_FF_EXPERT = 3584
N_DENSE = (DEPTH + 1) // 2
N_MOE = DEPTH // 2

kernel_name = "hybrid_diffattn_shortconv_hgrn2_moe"


def rms_norm(x, gain):
    xf = x.astype(jnp.float32)
    y = xf * lax.rsqrt(jnp.mean(xf * xf, axis=-1, keepdims=True) + EPS)
    return (y * gain.astype(jnp.float32)).astype(x.dtype)


def rel_bucket(q_pos, k_pos):
    n = jnp.maximum(q_pos[:, None] - k_pos[None, :], 0)
    max_exact = REL_BUCKETS // 2
    nf = jnp.maximum(n, 1).astype(jnp.float32)
    large = max_exact + (jnp.log(nf / max_exact) / math.log(REL_MAX_DIST / max_exact)
                         * (REL_BUCKETS - max_exact)).astype(jnp.int32)
    large = jnp.minimum(large, REL_BUCKETS - 1)
    return jnp.where(n < max_exact, n, large)


def diff_attn_block(q, q_pos, k, v, k_pos, rel_bias, lam):
    bias = jnp.moveaxis(rel_bias[rel_bucket(q_pos, k_pos)], -1, 0).astype(jnp.float32)
    logits = jnp.einsum('bmhqd,bmhkd->bmhqk', q, k, preferred_element_type=jnp.float32)
    logits = logits * (ATT_QK_DIM ** -0.5) + bias
    causal = k_pos[None, :] <= q_pos[:, None]
    probs = jax.nn.softmax(jnp.where(causal, logits, -jnp.inf), axis=-1)
    attn = probs[:, 0] - lam * probs[:, 1]
    return jnp.einsum('bhqk,bhkd->bhqd', attn.astype(v.dtype), v)


def diff_attention(q, k, v, q_gain, k_gain, lam_params, sub_gain, rel_bias, layer):
    bsz, t_len, _ = q.shape
    n_real = t_len - N_META
    n_blk = n_real // Q_BLOCK
    q = rms_norm(q.reshape(bsz, t_len, 2, ATT_HEADS, ATT_QK_DIM), q_gain).transpose(0, 2, 3, 1, 4)
    k = rms_norm(k.reshape(bsz, t_len, 2, ATT_HEADS, ATT_QK_DIM), k_gain).transpose(0, 2, 3, 1, 4)
    v = v.reshape(bsz, t_len, ATT_HEADS, ATT_V_DIM).transpose(0, 2, 1, 3)
    lam_init = 0.8 - 0.6 * math.exp(-0.3 * layer)
    lp = lam_params.astype(jnp.float32)
    lam = jnp.exp(jnp.sum(lp[0] * lp[1])) - jnp.exp(jnp.sum(lp[2] * lp[3])) + lam_init
    pos = jnp.arange(t_len)
    o_meta = diff_attn_block(q[:, :, :, :N_META], pos[:N_META], k[:, :, :, :N_META],
                             v[:, :, :N_META], pos[:N_META], rel_bias, lam)
    q_blocks = jnp.moveaxis(q[:, :, :, N_META:].reshape(bsz, 2, ATT_HEADS, n_blk, Q_BLOCK, ATT_QK_DIM), 3, 0)
    qpos_blocks = pos[N_META:].reshape(n_blk, Q_BLOCK)
    o_real = lax.map(lambda a: diff_attn_block(a[0], a[1], k, v, pos, rel_bias, lam), (q_blocks, qpos_blocks))
    o_real = jnp.moveaxis(o_real, 0, 2).reshape(bsz, ATT_HEADS, n_real, ATT_V_DIM)
    o = jnp.concatenate([o_meta, o_real], axis=2)
    o = rms_norm(o, sub_gain) * (1.0 - lam_init)
    return o.transpose(0, 2, 1, 3).reshape(bsz, t_len, ATT_WIDTH)


def short_conv(b, c, h, conv_w):
    z = c * h
    y = lax.conv_general_dilated(z, conv_w[:, None, :].astype(z.dtype), window_strides=(1,),
                                 padding=[(CONV_K - 1, 0)], dimension_numbers=('NWC', 'WIO', 'NWC'),
                                 feature_group_count=CONV_WIDTH)
    return b * y


def gla_chunk(state, q, k, v, log_f):
    c_len = q.shape[2]
    g_cum = jnp.cumsum(log_f, axis=2)
    o_inter = jnp.einsum('bhcd,bhde->bhce', q * jnp.exp(g_cum), state)
    diff = g_cum[:, :, :, None, :] - g_cum[:, :, None, :, :]
    mask = jnp.tril(jnp.ones((c_len, c_len), dtype=bool))[:, :, None]
    decay = jnp.exp(jnp.where(mask, diff, -jnp.inf))
    scores = jnp.einsum('bhtd,bhsd,bhtsd->bhts', q, k, decay)
    o_intra = jnp.einsum('bhts,bhse->bhte', scores, v)
    g_last = g_cum[:, :, -1, :]
    new_state = jnp.exp(g_last)[..., None] * state + jnp.einsum(
        'bhsd,bhse->bhde', k * jnp.exp(g_last[:, :, None, :] - g_cum), v)
    return new_state, o_inter + o_intra


def hgrn2(q, f_pre, i, g, lb, out_gain):
    bsz, t_len, _ = q.shape
    out_dtype = q.dtype
    n_real = t_len - N_META
    n_chunk = n_real // HGRN_CHUNK
    f32 = jnp.float32
    qh = jax.nn.silu(q.astype(f32)).reshape(bsz, t_len, HGRN_HEADS, HGRN_DK)
    z = f_pre.astype(f32).reshape(bsz, t_len, HGRN_HEADS, HGRN_DK)
    lb = lb.astype(f32).reshape(HGRN_HEADS, HGRN_DK)
    log_f = jnp.logaddexp(jnp.log(lb), jnp.log1p(-lb) + jax.nn.log_sigmoid(z))
    kh = (1.0 - lb) * jax.nn.sigmoid(-z)
    vh = i.astype(f32).reshape(bsz, t_len, HGRN_HEADS, HGRN_DV)
    qh, kh, vh, log_f = (a.transpose(0, 2, 1, 3) for a in (qh, kh, vh, log_f))
    state0 = jnp.zeros((bsz, HGRN_HEADS, HGRN_DK, HGRN_DV), f32)
    state, o_meta = gla_chunk(state0, qh[:, :, :N_META], kh[:, :, :N_META],
                              vh[:, :, :N_META], log_f[:, :, :N_META])

    def to_chunks(a):
        return jnp.moveaxis(a[:, :, N_META:].reshape(bsz, HGRN_HEADS, n_chunk, HGRN_CHUNK, a.shape[-1]), 2, 0)

    _, o_real = lax.scan(lambda s, xs: gla_chunk(s, *xs), state,
                         (to_chunks(qh), to_chunks(kh), to_chunks(vh), to_chunks(log_f)))
    o_real = jnp.moveaxis(o_real, 0, 2).reshape(bsz, HGRN_HEADS, n_real, HGRN_DV)
    o = jnp.concatenate([o_meta, o_real], axis=2).transpose(0, 2, 1, 3)
    gate = g.astype(f32).reshape(bsz, t_len, HGRN_HEADS, HGRN_DV)
    o = rms_norm(o, out_gain) * jax.nn.silu(gate)
    return o.reshape(bsz, t_len, HGRN_WIDTH).astype(out_dtype)


def swiglu(h, w_gate, w_up, w_down):
    return (jax.nn.silu(h @ w_gate) * (h @ w_up)) @ w_down


def moe_swiglu(h, router_w, w_gate, w_up, w_down):
    logits = jnp.einsum('btd,de->bte', h, router_w).astype(jnp.float32)
    top_logits, top_idx = lax.top_k(logits, TOP_K)
    top_w = jax.nn.softmax(top_logits, axis=-1)
    combine = jnp.sum(jax.nn.one_hot(top_idx, N_EXPERTS, dtype=jnp.float32) * top_w[..., None], axis=-2)
    combine = combine.astype(h.dtype)
    out = jnp.zeros_like(h)
    for e in range(N_EXPERTS):
        out = out + combine[..., e:e + 1] * swiglu(h, w_gate[e], w_up[e], w_down[e])
    return out


def setup_inputs(seed: int = 0) -> dict:
    key = jax.random.key(seed)
    ks = iter(jax.random.split(key, 32))

    def nrm(shape, scale):
        return scale * jax.random.normal(next(ks), shape, jnp.float32)

    def gain(shape):
        return 1.0 + nrm(shape, 0.05)

    return {
        "x": nrm((BATCH, SEQ, D_MODEL), 1.0),
        "meta_tokens": nrm((N_META, D_MODEL), 1.0),
        "norm1_gain": gain((DEPTH, D_MODEL)),
        "norm2_gain": gain((DEPTH, D_MODEL)),
        "w_in": nrm((DEPTH, D_MODEL, IN_COLS), D_MODEL ** -0.5),
        "q_norm_gain": gain((DEPTH, ATT_QK_DIM)),
        "k_norm_gain": gain((DEPTH, ATT_QK_DIM)),
        "diff_lambda": nrm((DEPTH, 4, ATT_QK_DIM), 0.1),
        "attn_sub_gain": gain((DEPTH, ATT_V_DIM)),
        "rel_bias": nrm((REL_BUCKETS, ATT_HEADS), 0.5),
        "conv_w": nrm((DEPTH, CONV_K, CONV_WIDTH), CONV_K ** -0.5),
        "hgrn_lb_logits": nrm((DEPTH, HGRN_HEADS * HGRN_DK), 1.0),
        "hgrn_out_gain": gain((DEPTH, HGRN_DV)),
        "w_branch": nrm((DEPTH, N_BRANCH, BRANCH_WIDTH, D_MODEL), BRANCH_WIDTH ** -0.5),
        "w_out": nrm((DEPTH, D_MODEL, D_MODEL), D_MODEL ** -0.5),
        "ffn_w_gate": nrm((N_DENSE, D_MODEL, D_FF_DENSE), D_MODEL ** -0.5),
        "ffn_w_up": nrm((N_DENSE, D_MODEL, D_FF_DENSE), D_MODEL ** -0.5),
        "ffn_w_down": nrm((N_DENSE, D_FF_DENSE, D_MODEL), D_FF_DENSE ** -0.5),
        "router_w": nrm((N_MOE, D_MODEL, N_EXPERTS), D_MODEL ** -0.5),
        "moe_w_gate": nrm((N_MOE, N_EXPERTS, D_MODEL, D_FF_EXPERT), D_MODEL ** -0.5),
        "moe_w_up": nrm((N_MOE, N_EXPERTS, D_MODEL, D_FF_EXPERT), D_MODEL ** -0.5),
        "moe_w_down": nrm((N_MOE, N_EXPERTS, D_FF_EXPERT, D_MODEL), D_FF_EXPERT ** -0.5),
    }


def reference(x, meta_tokens, norm1_gain, norm2_gain, w_in, q_norm_gain, k_norm_gain,
              diff_lambda, attn_sub_gain, rel_bias, conv_w, hgrn_lb_logits, hgrn_out_gain,
              w_branch, w_out, ffn_w_gate, ffn_w_up, ffn_w_down, router_w,
              moe_w_gate, moe_w_up, moe_w_down):
    bsz = x.shape[0]
    meta = jnp.broadcast_to(meta_tokens.astype(x.dtype)[None], (bsz, N_META, D_MODEL))
    hs = jnp.concatenate([meta, x], axis=1)
    t_len = hs.shape[1]
    lb_all = jnp.cumsum(jax.nn.softmax(hgrn_lb_logits.astype(jnp.float32), axis=0), axis=0)
    lb_all = lb_all - lb_all[0]
    split_at = [int(s) for s in np.cumsum(IN_SPLITS)[:-1]]
    for layer in range(DEPTH):
        hn = rms_norm(hs, norm1_gain[layer])
        proj = jnp.einsum('btd,dn->btn', hn, w_in[layer])
        (a_q, a_k, a_v, c_b, c_c, c_h, r_q, r_f, r_i, r_g, gates) = jnp.split(proj, split_at, axis=-1)
        u_att = diff_attention(a_q, a_k, a_v, q_norm_gain[layer], k_norm_gain[layer],
                               diff_lambda[layer], attn_sub_gain[layer], rel_bias, layer)
        u_conv = short_conv(c_b, c_c, c_h, conv_w[layer])
        u_hgrn = hgrn2(r_q, r_f, r_i, r_g, lb_all[layer], hgrn_out_gain[layer])
        branches = jnp.stack([u_att, u_conv, u_hgrn], axis=2)
        up = jnp.einsum('btnw,nwd->btnd', branches, w_branch[layer])
        gate = jax.nn.sigmoid(gates.reshape(bsz, t_len, N_BRANCH, D_MODEL))
        mixed = jnp.sum(gate * up, axis=2)
        hs = hs + jnp.einsum('btd,de->bte', mixed, w_out[layer])
        hn2 = rms_norm(hs, norm2_gain[layer])
        if layer % 2 == 0:
            j = layer // 2
            ffn = swiglu(hn2, ffn_w_gate[j], ffn_w_up[j], ffn_w_down[j])
        else:
            j = layer // 2
            ffn = moe_swiglu(hn2, router_w[j], moe_w_gate[j], moe_w_up[j], moe_w_down[j])
        hs = hs + ffn
    return hs[:, N_META:]
```

```python
import functools
import math

import numpy as np
import jax
import jax.numpy as jnp
from jax import lax
from jax.experimental import pallas as pl
from jax.experimental.pallas import tpu as pltpu

F32 = jnp.float32
BF16 = jnp.bfloat16

D_MODEL = 1024
N_META = 16
EPS = 1e-6
ATT_HEADS = 4
ATT_QK_DIM = 64
ATT_V_DIM = 128
REL_BUCKETS = 32
REL_MAX_DIST = 128
CONV_K = 3
HGRN_HEADS = 4
HGRN_D = 128
N_EXPERTS = 8
IN_COLS = 8192

T_PAD = 128
PAD0 = T_PAD - N_META
ATT_TILE = 384
HGRN_CHUNK = 128
HGRN_SUB = 16
COL_TILE = 512
MASK = -1e30
VMEM_LIMIT = 56 * 1024 * 1024

BLK_Q, BLK_K, BLK_V = 0, 4, 8
BLK_RQ, BLK_RF, BLK_RI, BLK_RG = 24, 28, 32, 36
BLK_CB, BLK_CC, BLK_CH = 3, 4, 5
BLK_GATE = 5


def _row_tile(rows, target):
    n = rows // 128
    best = 1
    for d in range(1, n + 1):
        if n % d == 0 and d * 128 <= target:
            best = d
    return best * 128


def _params(sem, vmem=VMEM_LIMIT):
    return pltpu.CompilerParams(dimension_semantics=sem, vmem_limit_bytes=vmem)


def _inproj_kernel(x_ref, g_ref, w_ref, qkg_ref, gm_ref, o_ref, xn_ref):
    j = pl.program_id(1)

    @pl.when(j == 0)
    def _():
        x = x_ref[...]
        ms = jnp.mean(x * x, axis=-1, keepdims=True)
        xn_ref[...] = (x * lax.rsqrt(ms + EPS) * g_ref[...]).astype(BF16)

    acc = jnp.dot(xn_ref[...], w_ref[...], preferred_element_type=F32)

    @pl.when(j < 2)
    def _():
        sq = acc * acc
        hi = sq.astype(BF16)
        lo = (sq - hi.astype(F32)).astype(BF16)
        ms = (jnp.dot(hi, gm_ref[...], preferred_element_type=F32)
              + jnp.dot(lo, gm_ref[...], preferred_element_type=F32))
        o_ref[...] = (acc * lax.rsqrt(ms + EPS) * qkg_ref[...]).astype(BF16)

    @pl.when(j >= 2)
    def _():
        o_ref[...] = acc.astype(BF16)


def _inproj(hs, gain, w_bf, qk_gain):
    rows = hs.shape[0]
    tm = _row_tile(rows, 1536)
    n_col = IN_COLS // COL_TILE
    grp = np.arange(COL_TILE) // ATT_QK_DIM
    gm = jnp.asarray((grp[:, None] == grp[None, :]).astype(np.float32) / ATT_QK_DIM, BF16)
    return pl.pallas_call(
        _inproj_kernel,
        out_shape=jax.ShapeDtypeStruct((rows, IN_COLS), BF16),
        grid=(rows // tm, n_col),
        in_specs=[
            pl.BlockSpec((tm, D_MODEL), lambda i, j: (i, 0)),
            pl.BlockSpec((1, D_MODEL), lambda i, j: (0, 0)),
            pl.BlockSpec((D_MODEL, COL_TILE), lambda i, j: (0, j)),
            pl.BlockSpec((None, 1, COL_TILE), lambda i, j: (jnp.minimum(j, 1), 0, 0)),
            pl.BlockSpec((COL_TILE, COL_TILE), lambda i, j: (0, 0)),
        ],
        out_specs=pl.BlockSpec((tm, COL_TILE), lambda i, j: (i, j)),
        scratch_shapes=[pltpu.VMEM((tm, D_MODEL), BF16)],
        compiler_params=_params(("parallel", "arbitrary")),
        name="inproj",
    )(hs, gain.reshape(1, D_MODEL), w_bf, qk_gain, gm)


def _rel_bucket_table(n_max):
    n = np.arange(n_max, dtype=np.int64)
    max_exact = REL_BUCKETS // 2
    nf = np.maximum(n, 1).astype(np.float32)
    large = max_exact + (np.log(nf / np.float32(max_exact)) / np.float32(math.log(REL_MAX_DIST / max_exact))
                         * np.float32(REL_BUCKETS - max_exact)).astype(np.int32)
    large = np.minimum(large, REL_BUCKETS - 1)
    return np.where(n < max_exact, n, large).astype(np.int32)


def _attn_bias_tables(rel_bias, t):
    bucket = _rel_bucket_table(2 * t)
    assert np.all(bucket[t + 1:] == REL_BUCKETS - 1)
    r = np.arange(t)[:, None]
    c = np.arange(t)[None, :]
    far = rel_bias[REL_BUCKETS - 1].astype(F32)
    rb = rel_bias.astype(F32) - far[None, :]
    n0 = r - c
    diag = jnp.where(jnp.asarray(n0 >= 0)[..., None], rb[bucket[np.maximum(n0, 0)]], MASK)
    near = rb[bucket[t + r - c]]
    zero = jnp.zeros_like(near)
    keymask = jnp.asarray(np.broadcast_to(c < PAD0, (t, t)))[..., None]
    first = [jnp.where(keymask, MASK, a) for a in (diag, near, zero)]
    tab = jnp.stack([diag, near, zero] + first, axis=0)
    return jnp.moveaxis(tab, -1, 0)


def _attn_kernel(lam_ref, q_ref, k_ref, v_ref, bt_ref, sg_ref, o_ref, *, t, out_scale):
    i = pl.program_id(2)
    q = q_ref[...]
    lane = lax.broadcasted_iota(jnp.int32, (1, 2 * ATT_QK_DIM), 1)
    zero = jnp.zeros_like(q)
    q_maps = (jnp.where(lane < ATT_QK_DIM, q, zero), jnp.where(lane >= ATT_QK_DIM, q, zero))

    def body(j, carry):
        r0 = pl.multiple_of(j * t, t)
        ks = k_ref[pl.ds(r0, t), :]
        vs = v_ref[pl.ds(r0, t), :]
        kind = jnp.minimum(i - j, 2) + jnp.where(j == 0, 3, 0)
        bias = bt_ref[kind]
        out = []
        for qm, (m, l, a) in zip(q_maps, carry):
            s = lax.dot_general(qm, ks, (((1,), (1,)), ((), ())), preferred_element_type=F32) + bias
            mn = jnp.maximum(m, jnp.max(s, axis=-1, keepdims=True))
            p = jnp.exp(s - mn)
            alpha = jnp.exp(m - mn)
            l = alpha * l + jnp.sum(p, axis=-1, keepdims=True)
            a = alpha * a + jnp.dot(p.astype(BF16), vs, preferred_element_type=F32)
            out.append((mn, l, a))
        return tuple(out)

    def init():
        return (jnp.full((t, 1), MASK, F32), jnp.zeros((t, 1), F32), jnp.zeros((t, ATT_V_DIM), F32))

    (_, l1, a1), (_, l2, a2) = lax.fori_loop(0, i + 1, body, (init(), init()))
    o = a1 / l1 - lam_ref[0] * (a2 / l2)
    ms = jnp.mean(o * o, axis=-1, keepdims=True)
    y = o * lax.rsqrt(ms + EPS) * (sg_ref[...] * out_scale)
    row = i * t + lax.broadcasted_iota(jnp.int32, (t, 1), 0)
    o_ref[...] = jnp.where(row >= PAD0, y, 0.0).astype(BF16)


def _diff_attention(proj3, lam, sub_gain, btab, out_scale):
    bsz, tp, _ = proj3.shape
    t = ATT_TILE
    kern = functools.partial(_attn_kernel, t=t, out_scale=out_scale)
    return pl.pallas_call(
        kern,
        out_shape=jax.ShapeDtypeStruct((bsz, tp, ATT_HEADS * ATT_V_DIM), BF16),
        grid=(bsz, ATT_HEADS, tp // t),
        in_specs=[
            pl.BlockSpec(memory_space=pltpu.SMEM),
            pl.BlockSpec((None, t, 128), lambda b, h, i: (b, i, BLK_Q + h)),
            pl.BlockSpec((None, tp, 128), lambda b, h, i: (b, 0, BLK_K + h)),
            pl.BlockSpec((None, tp, 128), lambda b, h, i: (b, 0, BLK_V + h)),
            pl.BlockSpec((None, 6, t, t), lambda b, h, i: (h, 0, 0, 0)),
            pl.BlockSpec((1, ATT_V_DIM), lambda b, h, i: (0, 0)),
        ],
        out_specs=pl.BlockSpec((None, t, 128), lambda b, h, i: (b, i, h)),
        compiler_params=_params(("parallel", "parallel", "arbitrary")),
        name="diff_attn",
    )(lam, proj3, proj3, proj3, btab, sub_gain.reshape(1, ATT_V_DIM))


def _split3(x):
    h1 = x.astype(BF16)
    r1 = x - h1.astype(F32)
    h2 = r1.astype(BF16)
    h3 = (r1 - h2.astype(F32)).astype(BF16)
    return h1, h2, h3


def _hgrn_kernel(q_ref, f_ref, i_ref, g_ref, la_ref, l1m_ref, oml_ref, og_ref, o_ref):
    c_len, sub = HGRN_CHUNK, HGRN_SUB
    n_chunk = q_ref.shape[0] // c_len
    la, l1m, oml, og = la_ref[...], l1m_ref[...], oml_ref[...], og_ref[...]
    rr = lax.broadcasted_iota(jnp.int32, (c_len, c_len), 0)
    cc = lax.broadcasted_iota(jnp.int32, (c_len, c_len), 1)
    tri = jnp.where(cc <= rr, 1.0, 0.0).astype(BF16)
    sub_row = lax.broadcasted_iota(jnp.int32, (sub, 1), 0)
    sub_lane = lax.broadcasted_iota(jnp.int32, (sub, c_len), 1)
    nt = (((1,), (1,)), ((), ()))

    def chunk(c, st):
        r0 = pl.multiple_of(c * c_len, c_len)
        rows = pl.ds(r0, c_len)
        z = f_ref[rows, :].astype(F32)
        qh = q_ref[rows, :].astype(F32)
        qh = qh * jax.nn.sigmoid(qh)
        v = i_ref[rows, :]
        gate = g_ref[rows, :].astype(F32)
        log_sig = jnp.minimum(z, 0.0) - jnp.log1p(jnp.exp(-jnp.abs(z)))
        bb = l1m + log_sig
        log_f = jnp.maximum(la, bb) + jnp.log1p(jnp.exp(-jnp.abs(la - bb)))
        valid = (r0 + lax.broadcasted_iota(jnp.int32, (c_len, 1), 0)) >= PAD0
        kh = jnp.where(valid, oml * jax.nn.sigmoid(-z), 0.0)
        g = sum(jnp.dot(tri, part, preferred_element_type=F32) for part in _split3(log_f))
        o_inter = lax.dot_general((qh * jnp.exp(g)).astype(BF16), st.astype(BF16), nt,
                                  preferred_element_type=F32)
        a_rows = []
        for a in range(c_len // sub):
            lo = a * sub
            ga = g[lo:lo + sub, :]
            qa = qh[lo:lo + sub, :]
            ka = kh[lo:lo + sub, :]
            if a == 0:
                a_blk = jnp.zeros((sub, c_len), F32)
            else:
                gs = g[lo - 1:lo, :]
                qd = (qa * jnp.exp(ga - gs)).astype(BF16)
                kd = (kh * jnp.exp(jnp.minimum(gs - g, 0.0))).astype(BF16)
                a_blk = lax.dot_general(qd, kd, nt, preferred_element_type=F32)
                a_blk = jnp.where(sub_lane < lo, a_blk, 0.0)
            for s in range(sub):
                w = qa * jnp.exp(ga - ga[s:s + 1, :]) * ka[s:s + 1, :]
                col = jnp.sum(jnp.where(sub_row >= s, w, 0.0), axis=-1, keepdims=True)
                a_blk = jnp.where(sub_lane == lo + s, col, a_blk)
            a_rows.append(a_blk)
        a_full = jnp.concatenate(a_rows, axis=0).astype(BF16)
        o = o_inter + jnp.dot(a_full, v, preferred_element_type=F32)
        g_last = g[c_len - 1:c_len, :]
        kd = (kh * jnp.exp(g_last - g)).astype(BF16)
        st = st * jnp.exp(g_last) + lax.dot_general(v, kd, (((0,), (0,)), ((), ())),
                                                    preferred_element_type=F32)
        ms = jnp.mean(o * o, axis=-1, keepdims=True)
        y = o * lax.rsqrt(ms + EPS) * og * (gate * jax.nn.sigmoid(gate))
        o_ref[rows, :] = y.astype(BF16)
        return st

    lax.fori_loop(0, n_chunk, chunk, jnp.zeros((HGRN_D, HGRN_D), F32))


def _hgrn(proj3, log_lb, log1m_lb, onem_lb, out_gain):
    bsz, tp, _ = proj3.shape
    seq = lambda blk: pl.BlockSpec((None, tp, 128), lambda b, h: (b, 0, blk + h))
    chan = pl.BlockSpec((None, 1, HGRN_D), lambda b, h: (h, 0, 0))
    return pl.pallas_call(
        _hgrn_kernel,
        out_shape=jax.ShapeDtypeStruct((bsz, tp, HGRN_HEADS * HGRN_D), BF16),
        grid=(bsz, HGRN_HEADS),
        in_specs=[seq(BLK_RQ), seq(BLK_RF), seq(BLK_RI), seq(BLK_RG), chan, chan, chan,
                  pl.BlockSpec((1, HGRN_D), lambda b, h: (0, 0))],
        out_specs=pl.BlockSpec((None, tp, 128), lambda b, h: (b, 0, h)),
        compiler_params=_params(("parallel", "parallel")),
        name="hgrn2",
    )(proj3, proj3, proj3, proj3,
      log_lb.reshape(HGRN_HEADS, 1, HGRN_D), log1m_lb.reshape(HGRN_HEADS, 1, HGRN_D),
      onem_lb.reshape(HGRN_HEADS, 1, HGRN_D), out_gain.reshape(1, HGRN_D))


def _merge_kernel(hs_ref, ua_ref, ur_ref, cb_ref, cc_ref, ch_ref, pc_ref, ph_ref,
                  g0_ref, g1_ref, g2_ref, cw_ref, wb_ref, wo_ref, o_ref, *, tm, tiles_per_seq):
    i = pl.program_id(0)
    row = (i % tiles_per_seq) * tm + lax.broadcasted_iota(jnp.int32, (tm, 1), 0)
    valid = row >= PAD0
    z = jnp.where(valid, cc_ref[...].astype(F32) * ch_ref[...].astype(F32), 0.0)
    halo_row = (i % tiles_per_seq) * tm - 8 + lax.broadcasted_iota(jnp.int32, (8, 1), 0)
    zp = jnp.where(halo_row >= PAD0, pc_ref[...].astype(F32) * ph_ref[...].astype(F32), 0.0)
    zz = jnp.concatenate([zp, z], axis=0)
    cw = cw_ref[...]
    y = (cw[2:3, :] * z + cw[1:2, :] * zz[7:7 + tm, :] + cw[0:1, :] * zz[6:6 + tm, :])
    u_conv = jnp.where(valid, cb_ref[...].astype(F32) * y, 0.0).astype(BF16)
    mixed = jnp.zeros((tm, D_MODEL), F32)
    for n, (u, g_ref) in enumerate(((ua_ref[...], g0_ref), (u_conv, g1_ref), (ur_ref[...], g2_ref))):
        up = jnp.dot(u, wb_ref[n], preferred_element_type=F32)
        mixed = mixed + jax.nn.sigmoid(g_ref[...].astype(F32)) * up
    o_ref[...] = hs_ref[...] + jnp.dot(mixed.astype(BF16), wo_ref[...], preferred_element_type=F32)


def _merge(hs, u_att, u_hgrn, proj, conv_w, wb_bf, wo_bf, tp):
    rows = hs.shape[0]
    tm = _row_tile(tp, 384)
    kern = functools.partial(_merge_kernel, tm=tm, tiles_per_seq=tp // tm)
    row_blk = lambda w, blk: pl.BlockSpec((tm, w), lambda i: (i, blk))
    halo = lambda blk: pl.BlockSpec((8, 512), lambda i: (jnp.maximum(i * (tm // 8) - 1, 0), blk))
    const = lambda shape: pl.BlockSpec(shape, lambda i: (0,) * len(shape))
    return pl.pallas_call(
        kern,
        out_shape=jax.ShapeDtypeStruct((rows, D_MODEL), F32),
        grid=(rows // tm,),
        in_specs=[row_blk(D_MODEL, 0), row_blk(512, 0), row_blk(512, 0),
                  row_blk(512, BLK_CB), row_blk(512, BLK_CC), row_blk(512, BLK_CH),
                  halo(BLK_CC), halo(BLK_CH),
                  row_blk(1024, BLK_GATE), row_blk(1024, BLK_GATE + 1), row_blk(1024, BLK_GATE + 2),
                  const((CONV_K, 512)), const((3, 512, D_MODEL)), const((D_MODEL, D_MODEL))],
        out_specs=row_blk(D_MODEL, 0),
        compiler_params=_params(("parallel",)),
        name="merge",
    )(hs, u_att, u_hgrn, proj, proj, proj, proj, proj, proj, proj, proj, conv_w, wb_bf, wo_bf)


def _ffn_kernel(hs_ref, g_ref, wg_ref, wu_ref, wd_ref, o_ref):
    x = hs_ref[...]
    ms = jnp.mean(x * x, axis=-1, keepdims=True)
    hn = (x * lax.rsqrt(ms + EPS) * g_ref[...]).astype(BF16)
    a = jnp.dot(hn, wg_ref[...], preferred_element_type=F32)
    u = jnp.dot(hn, wu_ref[...], preferred_element_type=F32)
    h = (a * jax.nn.sigmoid(a) * u).astype(BF16)
    o_ref[...] = x + jnp.dot(h, wd_ref[...], preferred_element_type=F32)


def _ffn(hs, gain, wg_bf, wu_bf, wd_bf):
    rows = hs.shape[0]
    d_ff = wg_bf.shape[1]
    tm = _row_tile(rows, 384)
    const = lambda shape: pl.BlockSpec(shape, lambda i: (0, 0))
    return pl.pallas_call(
        _ffn_kernel,
        out_shape=jax.ShapeDtypeStruct((rows, D_MODEL), F32),
        grid=(rows // tm,),
        in_specs=[pl.BlockSpec((tm, D_MODEL), lambda i: (i, 0)), const((1, D_MODEL)),
                  const((D_MODEL, d_ff)), const((D_MODEL, d_ff)), const((d_ff, D_MODEL))],
        out_specs=pl.BlockSpec((tm, D_MODEL), lambda i: (i, 0)),
        compiler_params=_params(("parallel",)),
        name="ffn_dense",
    )(hs, gain.reshape(1, D_MODEL), wg_bf, wu_bf, wd_bf)


def _router_kernel(hs_ref, g_ref, rw_ref, idx_ref, wt_ref):
    x = hs_ref[...]
    ms = jnp.mean(x * x, axis=-1, keepdims=True)
    hn = x * lax.rsqrt(ms + EPS) * g_ref[...]
    logits = jnp.dot(hn, rw_ref[...], preferred_element_type=F32, precision=lax.Precision.HIGHEST)
    lane = lax.broadcasted_iota(jnp.int32, logits.shape, 1)
    lane_f = lane.astype(F32)
    logits = jnp.where(lane < N_EXPERTS, logits, -jnp.inf)
    m1 = jnp.max(logits, axis=-1, keepdims=True)
    i1 = jnp.min(jnp.where(logits == m1, lane_f, 128.0), axis=-1, keepdims=True)
    rest = jnp.where(lane_f == i1, -jnp.inf, logits)
    m2 = jnp.max(rest, axis=-1, keepdims=True)
    i2 = jnp.min(jnp.where(rest == m2, lane_f, 128.0), axis=-1, keepdims=True)
    e2 = jnp.exp(m2 - m1)
    w1 = 1.0 / (1.0 + e2)
    w2 = e2 / (1.0 + e2)
    idx_ref[...] = jnp.where(lane == 0, i1, jnp.where(lane == 1, i2, 0.0)).astype(jnp.int32)
    wt_ref[...] = jnp.where(lane == 0, w1, jnp.where(lane == 1, w2, 0.0))


def _router(hs, gain, router_w):
    rows = hs.shape[0]
    tm = _row_tile(rows, 384)
    rw = jnp.zeros((D_MODEL, 128), F32).at[:, :N_EXPERTS].set(router_w.astype(F32))
    blk = pl.BlockSpec((tm, 128), lambda i: (i, 0))
    return pl.pallas_call(
        _router_kernel,
        out_shape=(jax.ShapeDtypeStruct((rows, 128), jnp.int32), jax.ShapeDtypeStruct((rows, 128), F32)),
        grid=(rows // tm,),
        in_specs=[pl.BlockSpec((tm, D_MODEL), lambda i: (i, 0)),
                  pl.BlockSpec((1, D_MODEL), lambda i: (0, 0)),
                  pl.BlockSpec((D_MODEL, 128), lambda i: (0, 0))],
        out_specs=(blk, blk),
        compiler_params=_params(("parallel",)),
        name="moe_router",
    )(hs, gain.reshape(1, D_MODEL), rw)


def _gather_kernel(tok_ref, src_ref, o_ref, sem, *, tg):
    base = pl.program_id(0) * tg

    def copy(r):
        return pltpu.make_async_copy(src_ref.at[pl.ds(tok_ref[base + r], 1), :],
                                     o_ref.at[pl.ds(r, 1), :], sem)

    def start(r, c):
        copy(r).start()
        return c

    def wait(r, c):
        copy(r).wait()
        return c

    lax.fori_loop(0, tg, start, 0)
    lax.fori_loop(0, tg, wait, 0)


def _gather_rows(src, tok_of_slot, tg):
    slots = tok_of_slot.shape[0]
    width = src.shape[1]
    return pl.pallas_call(
        functools.partial(_gather_kernel, tg=tg),
        out_shape=jax.ShapeDtypeStruct((slots, width), src.dtype),
        grid_spec=pltpu.PrefetchScalarGridSpec(
            num_scalar_prefetch=1,
            grid=(slots // tg,),
            in_specs=[pl.BlockSpec(memory_space=pl.ANY)],
            out_specs=pl.BlockSpec((tg, width), lambda i, tok: (i, 0)),
            scratch_shapes=[pltpu.SemaphoreType.DMA],
        ),
        compiler_params=_params(("arbitrary",)),
        name="moe_gather",
    )(tok_of_slot, src)


def _expert_kernel(te_ref, nt_ref, x_ref, g_ref, wg_ref, wu_ref, wd_ref, o_ref, hn_ref):
    t = pl.program_id(0)
    f = pl.program_id(1)

    @pl.when(t < nt_ref[0])
    def _():
        @pl.when(f == 0)
        def _():
            x = x_ref[...]
            ms = jnp.mean(x * x, axis=-1, keepdims=True)
            hn_ref[...] = (x * lax.rsqrt(ms + EPS) * g_ref[...]).astype(BF16)
            o_ref[...] = jnp.zeros_like(o_ref)

        hn = hn_ref[...]
        a = jnp.dot(hn, wg_ref[...], preferred_element_type=F32)
        u = jnp.dot(hn, wu_ref[...], preferred_element_type=F32)
        h = (a * jax.nn.sigmoid(a) * u).astype(BF16)
        o_ref[...] += jnp.dot(h, wd_ref[...], preferred_element_type=F32)

    @pl.when(jnp.logical_and(t >= nt_ref[0], f == 0))
    def _():
        o_ref[...] = jnp.zeros_like(o_ref)


def _experts(xs, gain, tile_expert, n_tiles_used, wg_bf, wu_bf, wd_bf, tg):
    slots = xs.shape[0]
    d_ff = wg_bf.shape[2]
    n_f = 2
    tf = d_ff // n_f
    return pl.pallas_call(
        _expert_kernel,
        out_shape=jax.ShapeDtypeStruct((slots, D_MODEL), F32),
        grid_spec=pltpu.PrefetchScalarGridSpec(
            num_scalar_prefetch=2,
            grid=(slots // tg, n_f),
            in_specs=[
                pl.BlockSpec((tg, D_MODEL), lambda t, f, te, nt: (t, 0)),
                pl.BlockSpec((1, D_MODEL), lambda t, f, te, nt: (0, 0)),
                pl.BlockSpec((None, D_MODEL, tf), lambda t, f, te, nt: (te[t], 0, f)),
                pl.BlockSpec((None, D_MODEL, tf), lambda t, f, te, nt: (te[t], 0, f)),
                pl.BlockSpec((None, tf, D_MODEL), lambda t, f, te, nt: (te[t], f, 0)),
            ],
            out_specs=pl.BlockSpec((tg, D_MODEL), lambda t, f, te, nt: (t, 0)),
            scratch_shapes=[pltpu.VMEM((tg, D_MODEL), BF16)],
        ),
        compiler_params=_params(("arbitrary", "arbitrary")),
        name="moe_experts",
    )(tile_expert, n_tiles_used, xs, gain.reshape(1, D_MODEL), wg_bf, wu_bf, wd_bf)


def _combine_kernel(s1_ref, s2_ref, hs_ref, wt_ref, y_ref, o_ref, buf_ref, sem, *, tc, tiles_per_seq):
    base = (pl.program_id(0) * tiles_per_seq + pl.program_id(1)) * tc

    def copy(r, k):
        slot = (s1_ref, s2_ref)[k][base + r]
        return pltpu.make_async_copy(y_ref.at[pl.ds(slot, 1), :], buf_ref.at[k, pl.ds(r, 1), :], sem)

    def start(r, c):
        copy(r, 0).start()
        copy(r, 1).start()
        return c

    def wait(r, c):
        copy(r, 0).wait()
        copy(r, 1).wait()
        return c

    lax.fori_loop(0, tc, start, 0)
    lax.fori_loop(0, tc, wait, 0)
    wt = wt_ref[...]
    o_ref[...] = hs_ref[...] + wt[:, 0:1] * buf_ref[0] + wt[:, 1:2] * buf_ref[1]


def _combine(hs3, wts3, y, slot1, slot2, seq):
    bsz = hs3.shape[0]
    tc = 128
    tiles_per_seq = seq // tc
    off = T_PAD // tc
    kern = functools.partial(_combine_kernel, tc=tc, tiles_per_seq=tiles_per_seq)
    return pl.pallas_call(
        kern,
        out_shape=jax.ShapeDtypeStruct((bsz, seq, D_MODEL), F32),
        grid_spec=pltpu.PrefetchScalarGridSpec(
            num_scalar_prefetch=2,
            grid=(bsz, tiles_per_seq),
            in_specs=[
                pl.BlockSpec((None, tc, D_MODEL), lambda b, i, s1, s2: (b, i + off, 0)),
                pl.BlockSpec((None, tc, 128), lambda b, i, s1, s2: (b, i + off, 0)),
                pl.BlockSpec(memory_space=pl.ANY),
            ],
            out_specs=pl.BlockSpec((None, tc, D_MODEL), lambda b, i, s1, s2: (b, i, 0)),
            scratch_shapes=[pltpu.VMEM((2, tc, D_MODEL), F32), pltpu.SemaphoreType.DMA],
        ),
        compiler_params=_params(("arbitrary", "arbitrary")),
        name="moe_combine",
    )(slot1, slot2, hs3, wts3, y)


def _moe(hs, gain, router_w, wg_bf, wu_bf, wd_bf, bsz, tp):
    seq = tp - T_PAD
    n_tok = bsz * seq
    tg = 512
    idx, wts = _router(hs, gain, router_w)
    idx3 = idx.reshape(bsz, tp, 128)
    e_of = idx3[:, T_PAD:, :2].reshape(n_tok * 2)
    onehot = (e_of[:, None] == jnp.arange(N_EXPERTS)[None, :]).astype(jnp.int32)
    rank = jnp.sum((jnp.cumsum(onehot, axis=0) - onehot) * onehot, axis=1)
    counts = jnp.sum(onehot, axis=0)
    padded = ((counts + tg - 1) // tg) * tg
    ends = jnp.cumsum(padded)
    starts = ends - padded
    slot = (starts[e_of] + rank).astype(jnp.int32)
    n_slots = 2 * n_tok + N_EXPERTS * tg
    tok = jnp.arange(n_tok, dtype=jnp.int32)
    src_row = (tok // seq) * tp + T_PAD + tok % seq
    tok_of_slot = jnp.zeros((n_slots,), jnp.int32).at[slot].set(jnp.repeat(src_row, 2))
    tile_start = jnp.arange(n_slots // tg, dtype=jnp.int32) * tg
    tile_expert = jnp.minimum(jnp.searchsorted(ends, tile_start, side="right"), N_EXPERTS - 1).astype(jnp.int32)
    n_tiles_used = (ends[-1:] // tg).astype(jnp.int32)

    xs = _gather_rows(hs, tok_of_slot, tg)
    y = _experts(xs, gain, tile_expert, n_tiles_used, wg_bf, wu_bf, wd_bf, tg)
    slot2 = slot.reshape(n_tok, 2)
    return _combine(hs.reshape(bsz, tp, D_MODEL), wts.reshape(bsz, tp, 128), y,
                    slot2[:, 0], slot2[:, 1], seq)


def _permute_qk_cols(w):
    return w.reshape(D_MODEL, 2, ATT_HEADS, ATT_QK_DIM).transpose(0, 2, 1, 3).reshape(D_MODEL, 512)


def kernel(x, meta_tokens, norm1_gain, norm2_gain, w_in, q_norm_gain, k_norm_gain, diff_lambda,
           attn_sub_gain, rel_bias, conv_w, hgrn_lb_logits, hgrn_out_gain, w_branch, w_out,
           ffn_w_gate, ffn_w_up, ffn_w_down, router_w, moe_w_gate, moe_w_up, moe_w_down):
    bsz, seq, _ = x.shape
    depth = w_in.shape[0]
    tp = T_PAD + seq
    assert tp % ATT_TILE == 0 and depth == 2

    head = jnp.concatenate([jnp.zeros((PAD0, D_MODEL), x.dtype), meta_tokens.astype(x.dtype)], axis=0)
    hs = jnp.concatenate([jnp.broadcast_to(head[None], (bsz, T_PAD, D_MODEL)), x], axis=1)
    hs = hs.reshape(bsz * tp, D_MODEL)

    lb_all = jnp.cumsum(jax.nn.softmax(hgrn_lb_logits.astype(F32), axis=0), axis=0)
    lb_all = lb_all - lb_all[0]
    btab = _attn_bias_tables(rel_bias, ATT_TILE)

    out = None
    for layer in range(depth):
        w = w_in[layer]
        w_bf = jnp.concatenate([_permute_qk_cols(w[:, :512]), _permute_qk_cols(w[:, 512:1024]), w[:, 1024:]],
                               axis=1).astype(BF16)
        qk_gain = jnp.stack([jnp.tile(q_norm_gain[layer].astype(F32), 8) * (ATT_QK_DIM ** -0.5),
                             jnp.tile(k_norm_gain[layer].astype(F32), 8)]).reshape(2, 1, COL_TILE)
        proj = _inproj(hs, norm1_gain[layer], w_bf, qk_gain)
        proj3 = proj.reshape(bsz, tp, IN_COLS)

        lam_init = 0.8 - 0.6 * math.exp(-0.3 * layer)
        lp = diff_lambda[layer].astype(F32)
        lam = jnp.exp(jnp.sum(lp[0] * lp[1])) - jnp.exp(jnp.sum(lp[2] * lp[3])) + lam_init
        u_att = _diff_attention(proj3, lam.reshape(1), attn_sub_gain[layer].astype(F32), btab,
                                1.0 - lam_init)

        lb = lb_all[layer]
        u_hgrn = _hgrn(proj3, jnp.log(lb), jnp.log1p(-lb), 1.0 - lb, hgrn_out_gain[layer].astype(F32))

        hs = _merge(hs, u_att.reshape(bsz * tp, 512), u_hgrn.reshape(bsz * tp, 512), proj,
                    conv_w[layer].astype(F32), w_branch[layer].astype(BF16), w_out[layer].astype(BF16), tp)

        j = layer // 2
        if layer % 2 == 0:
            hs = _ffn(hs, norm2_gain[layer], ffn_w_gate[j].astype(BF16), ffn_w_up[j].astype(BF16),
                      ffn_w_down[j].astype(BF16))
        else:
            out = _moe(hs, norm2_gain[layer], router_w[j], moe_w_gate[j].astype(BF16),
                       moe_w_up[j].astype(BF16), moe_w_down[j].astype(BF16), bsz, tp)
    return out
```

```python
import functools
import math

import numpy as np
import jax
import jax.numpy as jnp
from jax import lax
from jax.experimental import pallas as pl
from jax.experimental.pallas import tpu as pltpu

F32 = jnp.float32
BF16 = jnp.bfloat16

D_MODEL = 1024
N_META = 16
EPS = 1e-6
ATT_HEADS = 4
ATT_QK_DIM = 64
ATT_V_DIM = 128
REL_BUCKETS = 32
REL_MAX_DIST = 128
CONV_K = 3
HGRN_HEADS = 4
HGRN_D = 128
N_EXPERTS = 8
IN_COLS = 8192

T_PAD = 128
PAD0 = T_PAD - N_META
ATT_TILE = 384
LOG2E = math.log2(math.e)
HGRN_CHUNK = 128
HGRN_SUB = 16
COL_TILE = 512
MASK = -1e30
VMEM_LIMIT = 56 * 1024 * 1024

BLK_Q, BLK_K, BLK_V = 0, 4, 8
BLK_RQ, BLK_RF, BLK_RI, BLK_RG = 24, 28, 32, 36
BLK_CB, BLK_CC, BLK_CH = 3, 4, 5
BLK_GATE = 5


def _row_tile(rows, target):
    n = rows // 128
    best = 1
    for d in range(1, n + 1):
        if n % d == 0 and d * 128 <= target:
            best = d
    return best * 128


def _params(sem, vmem=VMEM_LIMIT):
    return pltpu.CompilerParams(dimension_semantics=sem, vmem_limit_bytes=vmem)


def _inproj_kernel(x_ref, g_ref, w_ref, qkg_ref, gm_ref, o_ref, xn_ref):
    j = pl.program_id(1)

    @pl.when(j == 0)
    def _():
        x = x_ref[...]
        ms = jnp.mean(x * x, axis=-1, keepdims=True)
        xn_ref[...] = (x * lax.rsqrt(ms + EPS) * g_ref[...]).astype(BF16)

    acc = jnp.dot(xn_ref[...], w_ref[...], preferred_element_type=F32)

    @pl.when(j < 2)
    def _():
        sq = acc * acc
        hi = sq.astype(BF16)
        lo = (sq - hi.astype(F32)).astype(BF16)
        ms = (jnp.dot(hi, gm_ref[...], preferred_element_type=F32)
              + jnp.dot(lo, gm_ref[...], preferred_element_type=F32))
        o_ref[...] = (acc * lax.rsqrt(ms + EPS) * qkg_ref[...]).astype(BF16)

    @pl.when(j >= 2)
    def _():
        o_ref[...] = acc.astype(BF16)


def _inproj(hs, gain, w_bf, qk_gain):
    rows = hs.shape[0]
    tm = _row_tile(rows, 1536)
    n_col = IN_COLS // COL_TILE
    grp = np.arange(COL_TILE) // ATT_QK_DIM
    gm = jnp.asarray((grp[:, None] == grp[None, :]).astype(np.float32) / ATT_QK_DIM, BF16)
    return pl.pallas_call(
        _inproj_kernel,
        out_shape=jax.ShapeDtypeStruct((rows, IN_COLS), BF16),
        grid=(rows // tm, n_col),
        in_specs=[
            pl.BlockSpec((tm, D_MODEL), lambda i, j: (i, 0)),
            pl.BlockSpec((1, D_MODEL), lambda i, j: (0, 0)),
            pl.BlockSpec((D_MODEL, COL_TILE), lambda i, j: (0, j)),
            pl.BlockSpec((None, 1, COL_TILE), lambda i, j: (jnp.minimum(j, 1), 0, 0)),
            pl.BlockSpec((COL_TILE, COL_TILE), lambda i, j: (0, 0)),
        ],
        out_specs=pl.BlockSpec((tm, COL_TILE), lambda i, j: (i, j)),
        scratch_shapes=[pltpu.VMEM((tm, D_MODEL), BF16)],
        compiler_params=_params(("parallel", "arbitrary")),
        name="inproj",
    )(hs, gain.reshape(1, D_MODEL), w_bf, qk_gain, gm)


def _rel_bucket_table(n_max):
    n = np.arange(n_max, dtype=np.int64)
    max_exact = REL_BUCKETS // 2
    nf = np.maximum(n, 1).astype(np.float32)
    large = max_exact + (np.log(nf / np.float32(max_exact)) / np.float32(math.log(REL_MAX_DIST / max_exact))
                         * np.float32(REL_BUCKETS - max_exact)).astype(np.int32)
    large = np.minimum(large, REL_BUCKETS - 1)
    return np.where(n < max_exact, n, large).astype(np.int32)


def _attn_bias_tables(rel_bias, t):
    bucket = _rel_bucket_table(2 * t)
    assert np.all(bucket[t + 1:] == REL_BUCKETS - 1) and np.all(np.diff(bucket) >= 0)
    first_dist = tuple(int(np.searchsorted(bucket, b, side="left")) for b in range(REL_BUCKETS))
    return pl.pallas_call(
        functools.partial(_bias_kernel, t=t, first_dist=first_dist),
        out_shape=jax.ShapeDtypeStruct((ATT_HEADS, 6, t, t), F32),
        grid=(ATT_HEADS,),
        in_specs=[pl.BlockSpec(memory_space=pltpu.SMEM)],
        out_specs=pl.BlockSpec((None, 6, t, t), lambda h: (h, 0, 0, 0)),
        compiler_params=_params(("parallel",)),
        name="attn_bias",
    )(rel_bias.astype(F32))


def _bias_kernel(rb_ref, o_ref, *, t, first_dist):
    h = pl.program_id(0)
    r = lax.broadcasted_iota(jnp.int32, (t, t), 0)
    c = lax.broadcasted_iota(jnp.int32, (t, t), 1)
    far = rb_ref[REL_BUCKETS - 1, h]

    def table(n):
        val = jnp.full((t, t), rb_ref[0, h] - far, F32)
        for b in range(1, REL_BUCKETS):
            val = jnp.where(n >= first_dist[b], rb_ref[b, h] - far, val)
        return val * LOG2E

    n0 = r - c
    diag = jnp.where(n0 >= 0, table(n0), MASK)
    near = table(n0 + t)
    zero = jnp.zeros((t, t), F32)
    for kind, tab in enumerate((diag, near, zero)):
        o_ref[kind] = tab
        o_ref[kind + 3] = jnp.where(c < PAD0, MASK, tab)


def _attn_kernel(lam_ref, q_ref, k_ref, v_ref, bt_ref, sg_ref, o_ref,
                 m_ref, acc_ref, *, t, out_scale):
    i = pl.program_id(2)
    q = q_ref[...]
    lane = lax.broadcasted_iota(jnp.int32, (1, 2 * ATT_QK_DIM), 1)
    zero = jnp.zeros_like(q)
    q_maps = (jnp.where(lane < ATT_QK_DIM, q, zero), jnp.where(lane >= ATT_QK_DIM, q, zero))
    m_ref[...] = jnp.full(m_ref.shape, MASK, F32)
    acc_ref[...] = jnp.zeros(acc_ref.shape, F32)
    nt = (((1,), (1,)), ((), ()))
    ones = jnp.ones((t, ATT_V_DIM), BF16)

    def tile(j, bias_kind):
        r0 = pl.multiple_of(j * t, t)
        ks = k_ref[pl.ds(r0, t), :]
        vs = jnp.concatenate([v_ref[pl.ds(r0, t), :], ones], axis=1)
        for mi in range(2):
            s = lax.dot_general(q_maps[mi], ks, nt, preferred_element_type=F32)
            if bias_kind is not None:
                s = s + bt_ref[bias_kind]
            m_old = m_ref[mi]
            m_new = jnp.maximum(m_old, jnp.max(s, axis=-1, keepdims=True))
            m_ref[mi] = m_new
            p = jnp.exp2(s - m_new).astype(BF16)
            acc_ref[mi] = jnp.exp2(m_old - m_new) * acc_ref[mi] + jnp.dot(p, vs, preferred_element_type=F32)

    tile(0, jnp.minimum(i, 2) + 3)

    def far(j, c):
        tile(j, None)
        return c

    lax.fori_loop(1, i - 1, far, 0)

    @pl.when(i >= 2)
    def _():
        tile(i - 1, 1)

    @pl.when(i >= 1)
    def _():
        tile(i, 0)

    a1, a2 = acc_ref[0], acc_ref[1]
    o = (a1[:, :ATT_V_DIM] / a1[:, ATT_V_DIM:]
         - lam_ref[0] * (a2[:, :ATT_V_DIM] / a2[:, ATT_V_DIM:]))
    ms = jnp.mean(o * o, axis=-1, keepdims=True)
    y = o * lax.rsqrt(ms + EPS) * (sg_ref[...] * out_scale)
    row = i * t + lax.broadcasted_iota(jnp.int32, (t, 1), 0)
    o_ref[...] = jnp.where(row >= PAD0, y, 0.0).astype(BF16)


def _diff_attention(proj3, lam, sub_gain, btab, out_scale):
    bsz, tp, _ = proj3.shape
    t = ATT_TILE
    kern = functools.partial(_attn_kernel, t=t, out_scale=out_scale)
    return pl.pallas_call(
        kern,
        out_shape=jax.ShapeDtypeStruct((bsz, tp, ATT_HEADS * ATT_V_DIM), BF16),
        grid=(bsz, ATT_HEADS, tp // t),
        in_specs=[
            pl.BlockSpec(memory_space=pltpu.SMEM),
            pl.BlockSpec((None, t, 128), lambda b, h, i: (b, i, BLK_Q + h)),
            pl.BlockSpec((None, tp, 128), lambda b, h, i: (b, 0, BLK_K + h)),
            pl.BlockSpec((None, tp, 128), lambda b, h, i: (b, 0, BLK_V + h)),
            pl.BlockSpec((None, 6, t, t), lambda b, h, i: (h, 0, 0, 0)),
            pl.BlockSpec((1, ATT_V_DIM), lambda b, h, i: (0, 0)),
        ],
        out_specs=pl.BlockSpec((None, t, 128), lambda b, h, i: (b, i, h)),
        scratch_shapes=[pltpu.VMEM((2, t, 1), F32), pltpu.VMEM((2, t, 2 * ATT_V_DIM), F32)],
        compiler_params=_params(("parallel", "parallel", "arbitrary")),
        name="diff_attn",
    )(lam, proj3, proj3, proj3, btab, sub_gain.reshape(1, ATT_V_DIM))


def _split3(x):
    h1 = x.astype(BF16)
    r1 = x - h1.astype(F32)
    h2 = r1.astype(BF16)
    h3 = (r1 - h2.astype(F32)).astype(BF16)
    return h1, h2, h3


def _hgrn_kernel(q_ref, f_ref, i_ref, g_ref, la_ref, l1m_ref, oml_ref, og_ref, o_ref):
    c_len, sub = HGRN_CHUNK, HGRN_SUB
    n_chunk = q_ref.shape[0] // c_len
    la, l1m, oml, og = la_ref[...], l1m_ref[...], oml_ref[...], og_ref[...]
    rr = lax.broadcasted_iota(jnp.int32, (c_len, c_len), 0)
    cc = lax.broadcasted_iota(jnp.int32, (c_len, c_len), 1)
    tri = jnp.where(cc <= rr, 1.0, 0.0).astype(BF16)
    sub_row = lax.broadcasted_iota(jnp.int32, (sub, 1), 0)
    sub_lane = lax.broadcasted_iota(jnp.int32, (sub, c_len), 1)
    nt = (((1,), (1,)), ((), ()))

    def chunk(c, st):
        r0 = pl.multiple_of(c * c_len, c_len)
        rows = pl.ds(r0, c_len)
        z = f_ref[rows, :].astype(F32)
        qh = q_ref[rows, :].astype(F32)
        qh = qh * jax.nn.sigmoid(qh)
        v = i_ref[rows, :]
        gate = g_ref[rows, :].astype(F32)
        log_sig = jnp.minimum(z, 0.0) - jnp.log1p(jnp.exp(-jnp.abs(z)))
        bb = l1m + log_sig
        log_f = jnp.maximum(la, bb) + jnp.log1p(jnp.exp(-jnp.abs(la - bb)))
        valid = (r0 + lax.broadcasted_iota(jnp.int32, (c_len, 1), 0)) >= PAD0
        kh = jnp.where(valid, oml * jax.nn.sigmoid(-z), 0.0)
        g = sum(jnp.dot(tri, part, preferred_element_type=F32) for part in _split3(log_f))
        o_inter = lax.dot_general((qh * jnp.exp(g)).astype(BF16), st.astype(BF16), nt,
                                  preferred_element_type=F32)
        a_rows = []
        for a in range(c_len // sub):
            lo = a * sub
            ga = g[lo:lo + sub, :]
            qa = qh[lo:lo + sub, :]
            ka = kh[lo:lo + sub, :]
            if a == 0:
                a_blk = jnp.zeros((sub, c_len), F32)
            else:
                gs = g[lo - 1:lo, :]
                qd = (qa * jnp.exp(ga - gs)).astype(BF16)
                kd = (kh * jnp.exp(jnp.minimum(gs - g, 0.0))).astype(BF16)
                a_blk = lax.dot_general(qd, kd, nt, preferred_element_type=F32)
                a_blk = jnp.where(sub_lane < lo, a_blk, 0.0)
            for s in range(sub):
                w = qa * jnp.exp(ga - ga[s:s + 1, :]) * ka[s:s + 1, :]
                col = jnp.sum(jnp.where(sub_row >= s, w, 0.0), axis=-1, keepdims=True)
                a_blk = jnp.where(sub_lane == lo + s, col, a_blk)
            a_rows.append(a_blk)
        a_full = jnp.concatenate(a_rows, axis=0).astype(BF16)
        o = o_inter + jnp.dot(a_full, v, preferred_element_type=F32)
        g_last = g[c_len - 1:c_len, :]
        kd = (kh * jnp.exp(g_last - g)).astype(BF16)
        st = st * jnp.exp(g_last) + lax.dot_general(v, kd, (((0,), (0,)), ((), ())),
                                                    preferred_element_type=F32)
        ms = jnp.mean(o * o, axis=-1, keepdims=True)
        y = o * lax.rsqrt(ms + EPS) * og * (gate * jax.nn.sigmoid(gate))
        o_ref[rows, :] = y.astype(BF16)
        return st

    lax.fori_loop(0, n_chunk, chunk, jnp.zeros((HGRN_D, HGRN_D), F32))


def _hgrn(proj3, log_lb, log1m_lb, onem_lb, out_gain):
    bsz, tp, _ = proj3.shape
    seq = lambda blk: pl.BlockSpec((None, tp, 128), lambda b, h: (b, 0, blk + h))
    chan = pl.BlockSpec((None, 1, HGRN_D), lambda b, h: (h, 0, 0))
    return pl.pallas_call(
        _hgrn_kernel,
        out_shape=jax.ShapeDtypeStruct((bsz, tp, HGRN_HEADS * HGRN_D), BF16),
        grid=(bsz, HGRN_HEADS),
        in_specs=[seq(BLK_RQ), seq(BLK_RF), seq(BLK_RI), seq(BLK_RG), chan, chan, chan,
                  pl.BlockSpec((1, HGRN_D), lambda b, h: (0, 0))],
        out_specs=pl.BlockSpec((None, tp, 128), lambda b, h: (b, 0, h)),
        compiler_params=_params(("parallel", "parallel")),
        name="hgrn2",
    )(proj3, proj3, proj3, proj3,
      log_lb.reshape(HGRN_HEADS, 1, HGRN_D), log1m_lb.reshape(HGRN_HEADS, 1, HGRN_D),
      onem_lb.reshape(HGRN_HEADS, 1, HGRN_D), out_gain.reshape(1, HGRN_D))


def _merge_kernel(hs_ref, ua_ref, ur_ref, cb_ref, cc_ref, ch_ref, pc_ref, ph_ref,
                  g0_ref, g1_ref, g2_ref, cw_ref, wb_ref, wo_ref, o_ref, *, tm, tiles_per_seq):
    i = pl.program_id(0)
    row = (i % tiles_per_seq) * tm + lax.broadcasted_iota(jnp.int32, (tm, 1), 0)
    valid = row >= PAD0
    z = jnp.where(valid, cc_ref[...].astype(F32) * ch_ref[...].astype(F32), 0.0)
    halo_row = (i % tiles_per_seq) * tm - 8 + lax.broadcasted_iota(jnp.int32, (8, 1), 0)
    zp = jnp.where(halo_row >= PAD0, pc_ref[...].astype(F32) * ph_ref[...].astype(F32), 0.0)
    zz = jnp.concatenate([zp, z], axis=0)
    cw = cw_ref[...]
    y = (cw[2:3, :] * z + cw[1:2, :] * zz[7:7 + tm, :] + cw[0:1, :] * zz[6:6 + tm, :])
    u_conv = jnp.where(valid, cb_ref[...].astype(F32) * y, 0.0).astype(BF16)
    mixed = jnp.zeros((tm, D_MODEL), F32)
    for n, (u, g_ref) in enumerate(((ua_ref[...], g0_ref), (u_conv, g1_ref), (ur_ref[...], g2_ref))):
        up = jnp.dot(u, wb_ref[n], preferred_element_type=F32)
        mixed = mixed + jax.nn.sigmoid(g_ref[...].astype(F32)) * up
    o_ref[...] = hs_ref[...] + jnp.dot(mixed.astype(BF16), wo_ref[...], preferred_element_type=F32)


def _merge(hs, u_att, u_hgrn, proj, conv_w, wb_bf, wo_bf, tp):
    rows = hs.shape[0]
    tm = _row_tile(tp, 384)
    kern = functools.partial(_merge_kernel, tm=tm, tiles_per_seq=tp // tm)
    row_blk = lambda w, blk: pl.BlockSpec((tm, w), lambda i: (i, blk))
    halo = lambda blk: pl.BlockSpec((8, 512), lambda i: (jnp.maximum(i * (tm // 8) - 1, 0), blk))
    const = lambda shape: pl.BlockSpec(shape, lambda i: (0,) * len(shape))
    return pl.pallas_call(
        kern,
        out_shape=jax.ShapeDtypeStruct((rows, D_MODEL), F32),
        grid=(rows // tm,),
        in_specs=[row_blk(D_MODEL, 0), row_blk(512, 0), row_blk(512, 0),
                  row_blk(512, BLK_CB), row_blk(512, BLK_CC), row_blk(512, BLK_CH),
                  halo(BLK_CC), halo(BLK_CH),
                  row_blk(1024, BLK_GATE), row_blk(1024, BLK_GATE + 1), row_blk(1024, BLK_GATE + 2),
                  const((CONV_K, 512)), const((3, 512, D_MODEL)), const((D_MODEL, D_MODEL))],
        out_specs=row_blk(D_MODEL, 0),
        compiler_params=_params(("parallel",)),
        name="merge",
    )(hs, u_att, u_hgrn, proj, proj, proj, proj, proj, proj, proj, proj, conv_w, wb_bf, wo_bf)


def _ffn_kernel(hs_ref, g_ref, wg_ref, wu_ref, wd_ref, o_ref):
    x = hs_ref[...]
    ms = jnp.mean(x * x, axis=-1, keepdims=True)
    hn = (x * lax.rsqrt(ms + EPS) * g_ref[...]).astype(BF16)
    a = jnp.dot(hn, wg_ref[...], preferred_element_type=F32)
    u = jnp.dot(hn, wu_ref[...], preferred_element_type=F32)
    h = (a * jax.nn.sigmoid(a) * u).astype(BF16)
    o_ref[...] = x + jnp.dot(h, wd_ref[...], preferred_element_type=F32)


def _ffn(hs, gain, wg_bf, wu_bf, wd_bf):
    rows = hs.shape[0]
    d_ff = wg_bf.shape[1]
    tm = _row_tile(rows, 384)
    const = lambda shape: pl.BlockSpec(shape, lambda i: (0, 0))
    return pl.pallas_call(
        _ffn_kernel,
        out_shape=jax.ShapeDtypeStruct((rows, D_MODEL), F32),
        grid=(rows // tm,),
        in_specs=[pl.BlockSpec((tm, D_MODEL), lambda i: (i, 0)), const((1, D_MODEL)),
                  const((D_MODEL, d_ff)), const((D_MODEL, d_ff)), const((d_ff, D_MODEL))],
        out_specs=pl.BlockSpec((tm, D_MODEL), lambda i: (i, 0)),
        compiler_params=_params(("parallel",)),
        name="ffn_dense",
    )(hs, gain.reshape(1, D_MODEL), wg_bf, wu_bf, wd_bf)


def _router_kernel(hs_ref, g_ref, rw_ref, idx_ref, wt_ref):
    x = hs_ref[...]
    ms = jnp.mean(x * x, axis=-1, keepdims=True)
    hn = x * lax.rsqrt(ms + EPS) * g_ref[...]
    logits = jnp.dot(hn, rw_ref[...], preferred_element_type=F32, precision=lax.Precision.HIGHEST)
    lane = lax.broadcasted_iota(jnp.int32, logits.shape, 1)
    lane_f = lane.astype(F32)
    logits = jnp.where(lane < N_EXPERTS, logits, -jnp.inf)
    m1 = jnp.max(logits, axis=-1, keepdims=True)
    i1 = jnp.min(jnp.where(logits == m1, lane_f, 128.0), axis=-1, keepdims=True)
    rest = jnp.where(lane_f == i1, -jnp.inf, logits)
    m2 = jnp.max(rest, axis=-1, keepdims=True)
    i2 = jnp.min(jnp.where(rest == m2, lane_f, 128.0), axis=-1, keepdims=True)
    e2 = jnp.exp(m2 - m1)
    w1 = 1.0 / (1.0 + e2)
    w2 = e2 / (1.0 + e2)
    idx_ref[...] = jnp.where(lane == 0, i1, jnp.where(lane == 1, i2, 0.0)).astype(jnp.int32)
    wt_ref[...] = jnp.where(lane == 0, w1, jnp.where(lane == 1, w2, 0.0))


def _router(hs, gain, router_w):
    rows = hs.shape[0]
    tm = _row_tile(rows, 384)
    rw = jnp.zeros((D_MODEL, 128), F32).at[:, :N_EXPERTS].set(router_w.astype(F32))
    blk = pl.BlockSpec((tm, 128), lambda i: (i, 0))
    return pl.pallas_call(
        _router_kernel,
        out_shape=(jax.ShapeDtypeStruct((rows, 128), jnp.int32), jax.ShapeDtypeStruct((rows, 128), F32)),
        grid=(rows // tm,),
        in_specs=[pl.BlockSpec((tm, D_MODEL), lambda i: (i, 0)),
                  pl.BlockSpec((1, D_MODEL), lambda i: (0, 0)),
                  pl.BlockSpec((D_MODEL, 128), lambda i: (0, 0))],
        out_specs=(blk, blk),
        compiler_params=_params(("parallel",)),
        name="moe_router",
    )(hs, gain.reshape(1, D_MODEL), rw)


def _gather_kernel(tok_ref, src_ref, o_ref, sem, *, tg):
    base = pl.program_id(0) * tg

    def copy(r):
        return pltpu.make_async_copy(src_ref.at[pl.ds(tok_ref[base + r], 1), :],
                                     o_ref.at[pl.ds(r, 1), :], sem)

    def start(r, c):
        copy(r).start()
        return c

    def wait(r, c):
        copy(r).wait()
        return c

    lax.fori_loop(0, tg, start, 0)
    lax.fori_loop(0, tg, wait, 0)


def _gather_rows(src, tok_of_slot, tg):
    slots = tok_of_slot.shape[0]
    width = src.shape[1]
    return pl.pallas_call(
        functools.partial(_gather_kernel, tg=tg),
        out_shape=jax.ShapeDtypeStruct((slots, width), src.dtype),
        grid_spec=pltpu.PrefetchScalarGridSpec(
            num_scalar_prefetch=1,
            grid=(slots // tg,),
            in_specs=[pl.BlockSpec(memory_space=pl.ANY)],
            out_specs=pl.BlockSpec((tg, width), lambda i, tok: (i, 0)),
            scratch_shapes=[pltpu.SemaphoreType.DMA],
        ),
        compiler_params=_params(("arbitrary",)),
        name="moe_gather",
    )(tok_of_slot, src)


def _expert_kernel(te_ref, nt_ref, x_ref, g_ref, wg_ref, wu_ref, wd_ref, o_ref, hn_ref):
    t = pl.program_id(0)
    f = pl.program_id(1)

    @pl.when(t < nt_ref[0])
    def _():
        @pl.when(f == 0)
        def _():
            x = x_ref[...]
            ms = jnp.mean(x * x, axis=-1, keepdims=True)
            hn_ref[...] = (x * lax.rsqrt(ms + EPS) * g_ref[...]).astype(BF16)
            o_ref[...] = jnp.zeros_like(o_ref)

        hn = hn_ref[...]
        a = jnp.dot(hn, wg_ref[...], preferred_element_type=F32)
        u = jnp.dot(hn, wu_ref[...], preferred_element_type=F32)
        h = (a * jax.nn.sigmoid(a) * u).astype(BF16)
        o_ref[...] += jnp.dot(h, wd_ref[...], preferred_element_type=F32)

    @pl.when(jnp.logical_and(t >= nt_ref[0], f == 0))
    def _():
        o_ref[...] = jnp.zeros_like(o_ref)


def _experts(xs, gain, tile_expert, n_tiles_used, wg_bf, wu_bf, wd_bf, tg):
    slots = xs.shape[0]
    d_ff = wg_bf.shape[2]
    n_f = 2
    tf = d_ff // n_f
    return pl.pallas_call(
        _expert_kernel,
        out_shape=jax.ShapeDtypeStruct((slots, D_MODEL), F32),
        grid_spec=pltpu.PrefetchScalarGridSpec(
            num_scalar_prefetch=2,
            grid=(slots // tg, n_f),
            in_specs=[
                pl.BlockSpec((tg, D_MODEL), lambda t, f, te, nt: (t, 0)),
                pl.BlockSpec((1, D_MODEL), lambda t, f, te, nt: (0, 0)),
                pl.BlockSpec((None, D_MODEL, tf), lambda t, f, te, nt: (te[t], 0, f)),
                pl.BlockSpec((None, D_MODEL, tf), lambda t, f, te, nt: (te[t], 0, f)),
                pl.BlockSpec((None, tf, D_MODEL), lambda t, f, te, nt: (te[t], f, 0)),
            ],
            out_specs=pl.BlockSpec((tg, D_MODEL), lambda t, f, te, nt: (t, 0)),
            scratch_shapes=[pltpu.VMEM((tg, D_MODEL), BF16)],
        ),
        compiler_params=_params(("arbitrary", "arbitrary")),
        name="moe_experts",
    )(tile_expert, n_tiles_used, xs, gain.reshape(1, D_MODEL), wg_bf, wu_bf, wd_bf)


def _combine_kernel(s1_ref, s2_ref, hs_ref, wt_ref, y_ref, o_ref, buf_ref, sem, *, tc, tiles_per_seq):
    base = (pl.program_id(0) * tiles_per_seq + pl.program_id(1)) * tc

    def copy(r, k):
        slot = (s1_ref, s2_ref)[k][base + r]
        return pltpu.make_async_copy(y_ref.at[pl.ds(slot, 1), :], buf_ref.at[k, pl.ds(r, 1), :], sem)

    def start(r, c):
        copy(r, 0).start()
        copy(r, 1).start()
        return c

    def wait(r, c):
        copy(r, 0).wait()
        copy(r, 1).wait()
        return c

    lax.fori_loop(0, tc, start, 0)
    lax.fori_loop(0, tc, wait, 0)
    wt = wt_ref[...]
    o_ref[...] = hs_ref[...] + wt[:, 0:1] * buf_ref[0] + wt[:, 1:2] * buf_ref[1]


def _combine(hs3, wts3, y, slot1, slot2, seq):
    bsz = hs3.shape[0]
    tc = 128
    tiles_per_seq = seq // tc
    off = T_PAD // tc
    kern = functools.partial(_combine_kernel, tc=tc, tiles_per_seq=tiles_per_seq)
    return pl.pallas_call(
        kern,
        out_shape=jax.ShapeDtypeStruct((bsz, seq, D_MODEL), F32),
        grid_spec=pltpu.PrefetchScalarGridSpec(
            num_scalar_prefetch=2,
            grid=(bsz, tiles_per_seq),
            in_specs=[
                pl.BlockSpec((None, tc, D_MODEL), lambda b, i, s1, s2: (b, i + off, 0)),
                pl.BlockSpec((None, tc, 128), lambda b, i, s1, s2: (b, i + off, 0)),
                pl.BlockSpec(memory_space=pl.ANY),
            ],
            out_specs=pl.BlockSpec((None, tc, D_MODEL), lambda b, i, s1, s2: (b, i, 0)),
            scratch_shapes=[pltpu.VMEM((2, tc, D_MODEL), F32), pltpu.SemaphoreType.DMA],
        ),
        compiler_params=_params(("arbitrary", "arbitrary")),
        name="moe_combine",
    )(slot1, slot2, hs3, wts3, y)


def _moe(hs, gain, router_w, wg_bf, wu_bf, wd_bf, bsz, tp):
    seq = tp - T_PAD
    n_tok = bsz * seq
    tg = 512
    idx, wts = _router(hs, gain, router_w)
    idx3 = idx.reshape(bsz, tp, 128)
    e_of = idx3[:, T_PAD:, :2].reshape(n_tok * 2)
    onehot = (e_of[:, None] == jnp.arange(N_EXPERTS)[None, :]).astype(jnp.int32)
    rank = jnp.sum((jnp.cumsum(onehot, axis=0) - onehot) * onehot, axis=1)
    counts = jnp.sum(onehot, axis=0)
    padded = ((counts + tg - 1) // tg) * tg
    ends = jnp.cumsum(padded)
    starts = ends - padded
    slot = (starts[e_of] + rank).astype(jnp.int32)
    n_slots = 2 * n_tok + N_EXPERTS * tg
    tok = jnp.arange(n_tok, dtype=jnp.int32)
    src_row = (tok // seq) * tp + T_PAD + tok % seq
    tok_of_slot = jnp.zeros((n_slots,), jnp.int32).at[slot].set(jnp.repeat(src_row, 2))
    tile_start = jnp.arange(n_slots // tg, dtype=jnp.int32) * tg
    tile_expert = jnp.minimum(jnp.searchsorted(ends, tile_start, side="right"), N_EXPERTS - 1).astype(jnp.int32)
    n_tiles_used = (ends[-1:] // tg).astype(jnp.int32)

    xs = _gather_rows(hs, tok_of_slot, tg)
    y = _experts(xs, gain, tile_expert, n_tiles_used, wg_bf, wu_bf, wd_bf, tg)
    slot2 = slot.reshape(n_tok, 2)
    return _combine(hs.reshape(bsz, tp, D_MODEL), wts.reshape(bsz, tp, 128), y,
                    slot2[:, 0], slot2[:, 1], seq)


def _permute_qk_cols(w):
    return w.reshape(D_MODEL, 2, ATT_HEADS, ATT_QK_DIM).transpose(0, 2, 1, 3).reshape(D_MODEL, 512)


def kernel(x, meta_tokens, norm1_gain, norm2_gain, w_in, q_norm_gain, k_norm_gain, diff_lambda,
           attn_sub_gain, rel_bias, conv_w, hgrn_lb_logits, hgrn_out_gain, w_branch, w_out,
           ffn_w_gate, ffn_w_up, ffn_w_down, router_w, moe_w_gate, moe_w_up, moe_w_down):
    bsz, seq, _ = x.shape
    depth = w_in.shape[0]
    tp = T_PAD + seq
    assert tp % ATT_TILE == 0 and depth == 2

    head = jnp.concatenate([jnp.zeros((PAD0, D_MODEL), x.dtype), meta_tokens.astype(x.dtype)], axis=0)
    hs = jnp.concatenate([jnp.broadcast_to(head[None], (bsz, T_PAD, D_MODEL)), x], axis=1)
    hs = hs.reshape(bsz * tp, D_MODEL)

    lb_all = jnp.cumsum(jax.nn.softmax(hgrn_lb_logits.astype(F32), axis=0), axis=0)
    lb_all = lb_all - lb_all[0]
    btab = _attn_bias_tables(rel_bias, ATT_TILE)

    out = None
    for layer in range(depth):
        w = w_in[layer]
        w_bf = jnp.concatenate([_permute_qk_cols(w[:, :512]), _permute_qk_cols(w[:, 512:1024]), w[:, 1024:]],
                               axis=1).astype(BF16)
        qk_gain = jnp.stack([jnp.tile(q_norm_gain[layer].astype(F32), 8) * (ATT_QK_DIM ** -0.5 * LOG2E),
                             jnp.tile(k_norm_gain[layer].astype(F32), 8)]).reshape(2, 1, COL_TILE)
        proj = _inproj(hs, norm1_gain[layer], w_bf, qk_gain)
        proj3 = proj.reshape(bsz, tp, IN_COLS)

        lam_init = 0.8 - 0.6 * math.exp(-0.3 * layer)
        lp = diff_lambda[layer].astype(F32)
        lam = jnp.exp(jnp.sum(lp[0] * lp[1])) - jnp.exp(jnp.sum(lp[2] * lp[3])) + lam_init
        u_att = _diff_attention(proj3, lam.reshape(1), attn_sub_gain[layer].astype(F32), btab,
                                1.0 - lam_init)

        lb = lb_all[layer]
        u_hgrn = _hgrn(proj3, jnp.log(lb), jnp.log1p(-lb), 1.0 - lb, hgrn_out_gain[layer].astype(F32))

        hs = _merge(hs, u_att.reshape(bsz * tp, 512), u_hgrn.reshape(bsz * tp, 512), proj,
                    conv_w[layer].astype(F32), w_branch[layer].astype(BF16), w_out[layer].astype(BF16), tp)

        j = layer // 2
        if layer % 2 == 0:
            hs = _ffn(hs, norm2_gain[layer], ffn_w_gate[j].astype(BF16), ffn_w_up[j].astype(BF16),
                      ffn_w_down[j].astype(BF16))
        else:
            out = _moe(hs, norm2_gain[layer], router_w[j], moe_w_gate[j].astype(BF16),
                       moe_w_up[j].astype(BF16), moe_w_down[j].astype(BF16), bsz, tp)
    return out
```

```python
import functools
import math

import numpy as np
import jax
import jax.numpy as jnp
from jax import lax
from jax.experimental import pallas as pl
from jax.experimental.pallas import tpu as pltpu

F32 = jnp.float32
BF16 = jnp.bfloat16

D_MODEL = 1024
N_META = 16
EPS = 1e-6
ATT_HEADS = 4
ATT_QK_DIM = 64
ATT_V_DIM = 128
REL_BUCKETS = 32
REL_MAX_DIST = 128
CONV_K = 3
HGRN_HEADS = 4
HGRN_D = 128
N_EXPERTS = 8
IN_COLS = 8192

T_PAD = 128
PAD0 = T_PAD - N_META
ATT_TILE = 384
LOG2E = math.log2(math.e)
HGRN_CHUNK = 128
HGRN_SUB = 16
COL_TILE = 512
GROUP_MEAN_WIDTH = 256
MASK = -1e30
VMEM_LIMIT = 56 * 1024 * 1024

BLK_Q, BLK_K, BLK_V = 0, 4, 8
BLK_RQ, BLK_RF, BLK_RI, BLK_RG = 24, 28, 32, 36
BLK_CB, BLK_CC, BLK_CH = 3, 4, 5
BLK_GATE = 5


def _row_tile(rows, target):
    n = rows // 128
    best = 1
    for d in range(1, n + 1):
        if n % d == 0 and d * 128 <= target:
            best = d
    return best * 128


def _params(sem, vmem=VMEM_LIMIT):
    return pltpu.CompilerParams(dimension_semantics=sem, vmem_limit_bytes=vmem)


def _inproj_kernel(x_ref, g_ref, w_ref, qkg_ref, gm_ref, o_ref, xn_ref):
    j = pl.program_id(1)

    @pl.when(j == 0)
    def _():
        x = x_ref[...]
        ms = jnp.mean(x * x, axis=-1, keepdims=True)
        xn_ref[...] = (x * lax.rsqrt(ms + EPS) * g_ref[...]).astype(BF16)

    acc = jnp.dot(xn_ref[...], w_ref[...], preferred_element_type=F32)

    @pl.when(j < 2)
    def _():
        sq = acc * acc
        hi = sq.astype(BF16)
        lo = (sq - hi.astype(F32)).astype(BF16)
        gm = gm_ref[...]
        width = gm.shape[0]
        ms = jnp.concatenate(
            [jnp.dot(hi[:, c:c + width], gm, preferred_element_type=F32)
             + jnp.dot(lo[:, c:c + width], gm, preferred_element_type=F32)
             for c in range(0, COL_TILE, width)], axis=1)
        o_ref[...] = (acc * lax.rsqrt(ms + EPS) * qkg_ref[...]).astype(BF16)

    @pl.when(j >= 2)
    def _():
        o_ref[...] = acc.astype(BF16)


def _inproj(hs, gain, w_bf, qk_gain):
    rows = hs.shape[0]
    tm = _row_tile(rows, 1536)
    n_col = IN_COLS // COL_TILE
    grp = np.arange(GROUP_MEAN_WIDTH) // ATT_QK_DIM
    gm = jnp.asarray((grp[:, None] == grp[None, :]).astype(np.float32) / ATT_QK_DIM, BF16)
    return pl.pallas_call(
        _inproj_kernel,
        out_shape=jax.ShapeDtypeStruct((rows, IN_COLS), BF16),
        grid=(rows // tm, n_col),
        in_specs=[
            pl.BlockSpec((tm, D_MODEL), lambda i, j: (i, 0)),
            pl.BlockSpec((1, D_MODEL), lambda i, j: (0, 0)),
            pl.BlockSpec((D_MODEL, COL_TILE), lambda i, j: (0, j)),
            pl.BlockSpec((None, 1, COL_TILE), lambda i, j: (jnp.minimum(j, 1), 0, 0)),
            pl.BlockSpec((GROUP_MEAN_WIDTH, GROUP_MEAN_WIDTH), lambda i, j: (0, 0)),
        ],
        out_specs=pl.BlockSpec((tm, COL_TILE), lambda i, j: (i, j)),
        scratch_shapes=[pltpu.VMEM((tm, D_MODEL), BF16)],
        compiler_params=_params(("parallel", "arbitrary")),
        name="inproj",
    )(hs, gain.reshape(1, D_MODEL), w_bf, qk_gain, gm)


def _rel_bucket_table(n_max):
    n = np.arange(n_max, dtype=np.int64)
    max_exact = REL_BUCKETS // 2
    nf = np.maximum(n, 1).astype(np.float32)
    large = max_exact + (np.log(nf / np.float32(max_exact)) / np.float32(math.log(REL_MAX_DIST / max_exact))
                         * np.float32(REL_BUCKETS - max_exact)).astype(np.int32)
    large = np.minimum(large, REL_BUCKETS - 1)
    return np.where(n < max_exact, n, large).astype(np.int32)


def _attn_bias_tables(rel_bias, t):
    bucket = _rel_bucket_table(2 * t)
    assert np.all(bucket[t + 1:] == REL_BUCKETS - 1) and np.all(np.diff(bucket) >= 0)
    first_dist = tuple(int(np.searchsorted(bucket, b, side="left")) for b in range(REL_BUCKETS))
    return pl.pallas_call(
        functools.partial(_bias_kernel, t=t, first_dist=first_dist),
        out_shape=jax.ShapeDtypeStruct((ATT_HEADS, 6, t, t), F32),
        grid=(ATT_HEADS,),
        in_specs=[pl.BlockSpec(memory_space=pltpu.SMEM)],
        out_specs=pl.BlockSpec((None, 6, t, t), lambda h: (h, 0, 0, 0)),
        compiler_params=_params(("parallel",)),
        name="attn_bias",
    )(rel_bias.astype(F32))


def _bias_kernel(rb_ref, o_ref, *, t, first_dist):
    h = pl.program_id(0)
    r = lax.broadcasted_iota(jnp.int32, (t, t), 0)
    c = lax.broadcasted_iota(jnp.int32, (t, t), 1)
    far = rb_ref[REL_BUCKETS - 1, h]

    def table(n):
        val = jnp.full((t, t), rb_ref[0, h] - far, F32)
        for b in range(1, REL_BUCKETS):
            val = jnp.where(n >= first_dist[b], rb_ref[b, h] - far, val)
        return val * LOG2E

    n0 = r - c
    diag = jnp.where(n0 >= 0, table(n0), MASK)
    near = table(n0 + t)
    zero = jnp.zeros((t, t), F32)
    for kind, tab in enumerate((diag, near, zero)):
        o_ref[kind] = tab
        o_ref[kind + 3] = jnp.where(c < PAD0, MASK, tab)


def _attn_kernel(lam_ref, q_ref, k_ref, v_ref, bt_ref, sg_ref, o_ref,
                 m_ref, acc_ref, *, t, out_scale):
    i = pl.program_id(2)
    q = q_ref[...]
    lane = lax.broadcasted_iota(jnp.int32, (1, 2 * ATT_QK_DIM), 1)
    zero = jnp.zeros_like(q)
    q_maps = (jnp.where(lane < ATT_QK_DIM, q, zero), jnp.where(lane >= ATT_QK_DIM, q, zero))
    m_ref[...] = jnp.full(m_ref.shape, MASK, F32)
    acc_ref[...] = jnp.zeros(acc_ref.shape, F32)
    nt = (((1,), (1,)), ((), ()))
    ones = jnp.ones((t, ATT_V_DIM), BF16)

    def tile(j, bias_kind):
        r0 = pl.multiple_of(j * t, t)
        ks = k_ref[pl.ds(r0, t), :]
        vs = jnp.concatenate([v_ref[pl.ds(r0, t), :], ones], axis=1)
        for mi in range(2):
            s = lax.dot_general(q_maps[mi], ks, nt, preferred_element_type=F32)
            if bias_kind is not None:
                s = s + bt_ref[bias_kind]
            m_old = m_ref[mi]
            m_new = jnp.maximum(m_old, jnp.max(s, axis=-1, keepdims=True))
            m_ref[mi] = m_new
            p = jnp.exp2(s - m_new).astype(BF16)
            acc_ref[mi] = jnp.exp2(m_old - m_new) * acc_ref[mi] + jnp.dot(p, vs, preferred_element_type=F32)

    tile(0, jnp.minimum(i, 2) + 3)

    def far(j, c):
        tile(j, None)
        return c

    lax.fori_loop(1, i - 1, far, 0)

    @pl.when(i >= 2)
    def _():
        tile(i - 1, 1)

    @pl.when(i >= 1)
    def _():
        tile(i, 0)

    a1, a2 = acc_ref[0], acc_ref[1]
    o = (a1[:, :ATT_V_DIM] / a1[:, ATT_V_DIM:]
         - lam_ref[0] * (a2[:, :ATT_V_DIM] / a2[:, ATT_V_DIM:]))
    ms = jnp.mean(o * o, axis=-1, keepdims=True)
    y = o * lax.rsqrt(ms + EPS) * (sg_ref[...] * out_scale)
    row = i * t + lax.broadcasted_iota(jnp.int32, (t, 1), 0)
    o_ref[...] = jnp.where(row >= PAD0, y, 0.0).astype(BF16)


def _diff_attention(proj3, lam, sub_gain, btab, out_scale):
    bsz, tp, _ = proj3.shape
    t = ATT_TILE
    kern = functools.partial(_attn_kernel, t=t, out_scale=out_scale)
    return pl.pallas_call(
        kern,
        out_shape=jax.ShapeDtypeStruct((bsz, tp, ATT_HEADS * ATT_V_DIM), BF16),
        grid=(bsz, ATT_HEADS, tp // t),
        in_specs=[
            pl.BlockSpec(memory_space=pltpu.SMEM),
            pl.BlockSpec((None, t, 128), lambda b, h, i: (b, i, BLK_Q + h)),
            pl.BlockSpec((None, tp, 128), lambda b, h, i: (b, 0, BLK_K + h)),
            pl.BlockSpec((None, tp, 128), lambda b, h, i: (b, 0, BLK_V + h)),
            pl.BlockSpec((None, 6, t, t), lambda b, h, i: (h, 0, 0, 0)),
            pl.BlockSpec((1, ATT_V_DIM), lambda b, h, i: (0, 0)),
        ],
        out_specs=pl.BlockSpec((None, t, 128), lambda b, h, i: (b, i, h)),
        scratch_shapes=[pltpu.VMEM((2, t, 1), F32), pltpu.VMEM((2, t, 2 * ATT_V_DIM), F32)],
        compiler_params=_params(("parallel", "parallel", "arbitrary")),
        name="diff_attn",
    )(lam, proj3, proj3, proj3, btab, sub_gain.reshape(1, ATT_V_DIM))


def _split3(x):
    h1 = x.astype(BF16)
    r1 = x - h1.astype(F32)
    h2 = r1.astype(BF16)
    h3 = (r1 - h2.astype(F32)).astype(BF16)
    return h1, h2, h3


def _hgrn_kernel(q_ref, f_ref, i_ref, g_ref, la_ref, l1m_ref, og_ref, o_ref):
    c_len, sub = HGRN_CHUNK, HGRN_SUB
    half = sub // 2
    n_chunk = q_ref.shape[0] // c_len
    la, l1m, og = la_ref[...], l1m_ref[...], og_ref[...]
    rr = lax.broadcasted_iota(jnp.int32, (c_len, c_len), 0)
    cc = lax.broadcasted_iota(jnp.int32, (c_len, c_len), 1)
    tri = jnp.where(cc <= rr, 1.0, 0.0).astype(BF16)
    row8 = lax.broadcasted_iota(jnp.int32, (half, 1), 0)
    lane8 = lax.broadcasted_iota(jnp.int32, (half, c_len), 1)
    nt = (((1,), (1,)), ((), ()))

    def chunk(c, st):
        r0 = pl.multiple_of(c * c_len, c_len)
        rows = pl.ds(r0, c_len)
        z = f_ref[rows, :].astype(F32)
        qh = q_ref[rows, :].astype(F32)
        qh = qh * jax.nn.sigmoid(qh)
        v = i_ref[rows, :]
        gate = g_ref[rows, :].astype(F32)
        sp = jnp.log(1.0 + jnp.exp(-jnp.abs(z)))
        bb = l1m + jnp.minimum(z, 0.0) - sp
        log_f = jnp.maximum(la, bb) + jnp.log(1.0 + jnp.exp(-jnp.abs(la - bb)))
        valid = (r0 + lax.broadcasted_iota(jnp.int32, (c_len, 1), 0)) >= PAD0
        log_k = jnp.where(valid, l1m + jnp.minimum(-z, 0.0) - sp, -jnp.inf)
        g = sum(jnp.dot(tri, part, preferred_element_type=F32) for part in _split3(log_f))
        ck = log_k - g
        o_inter = lax.dot_general((qh * jnp.exp(g)).astype(BF16), st.astype(BF16), nt,
                                  preferred_element_type=F32)
        a_rows = []
        for a in range(c_len // sub):
            lo = a * sub
            ga = (g[lo:lo + half, :], g[lo + half:lo + sub, :])
            qa = (qh[lo:lo + half, :], qh[lo + half:lo + sub, :])
            if a == 0:
                blk = [jnp.zeros((half, c_len), F32)] * 2
            else:
                gs = g[lo - 1:lo, :]
                qd = (qh[lo:lo + sub, :] * jnp.exp(g[lo:lo + sub, :] - gs)).astype(BF16)
                kd = jnp.exp(jnp.minimum(gs - g[:lo, :], 0.0) + log_k[:lo, :]).astype(BF16)
                kd = jnp.concatenate([kd, jnp.zeros((c_len - lo, HGRN_D), BF16)], axis=0)
                a_off = lax.dot_general(qd, kd, nt, preferred_element_type=F32)
                blk = [a_off[:half, :], a_off[half:, :]]
            for s in range(sub):
                crow = ck[lo + s:lo + s + 1, :]
                for hh in range(s // half, 2):
                    col = jnp.sum(qa[hh] * jnp.exp(ga[hh] + crow), axis=-1, keepdims=True)
                    blk[hh] = jnp.where(lane8 == lo + s, col, blk[hh])
            for hh in range(2):
                a_rows.append(jnp.where(lane8 <= lo + hh * half + row8, blk[hh], 0.0))
        a_full = jnp.concatenate(a_rows, axis=0).astype(BF16)
        o = o_inter + jnp.dot(a_full, v, preferred_element_type=F32)
        g_last = g[c_len - 1:c_len, :]
        kd = jnp.exp(g_last - g + log_k).astype(BF16)
        st = st * jnp.exp(g_last) + lax.dot_general(v, kd, (((0,), (0,)), ((), ())),
                                                    preferred_element_type=F32)
        ms = jnp.mean(o * o, axis=-1, keepdims=True)
        y = o * lax.rsqrt(ms + EPS) * og * (gate * jax.nn.sigmoid(gate))
        o_ref[rows, :] = y.astype(BF16)
        return st

    lax.fori_loop(0, n_chunk, chunk, jnp.zeros((HGRN_D, HGRN_D), F32), unroll=3)


def _hgrn(proj3, log_lb, log1m_lb, out_gain):
    bsz, tp, _ = proj3.shape
    seq = lambda blk: pl.BlockSpec((None, tp, 128), lambda b, h: (b, 0, blk + h))
    chan = pl.BlockSpec((None, 1, HGRN_D), lambda b, h: (h, 0, 0))
    return pl.pallas_call(
        _hgrn_kernel,
        out_shape=jax.ShapeDtypeStruct((bsz, tp, HGRN_HEADS * HGRN_D), BF16),
        grid=(bsz, HGRN_HEADS),
        in_specs=[seq(BLK_RQ), seq(BLK_RF), seq(BLK_RI), seq(BLK_RG), chan, chan,
                  pl.BlockSpec((1, HGRN_D), lambda b, h: (0, 0))],
        out_specs=pl.BlockSpec((None, tp, 128), lambda b, h: (b, 0, h)),
        compiler_params=_params(("parallel", "parallel")),
        name="hgrn2",
    )(proj3, proj3, proj3, proj3,
      log_lb.reshape(HGRN_HEADS, 1, HGRN_D), log1m_lb.reshape(HGRN_HEADS, 1, HGRN_D),
      out_gain.reshape(1, HGRN_D))


def _merge_kernel(hs_ref, ua_ref, ur_ref, cb_ref, cc_ref, ch_ref, pc_ref, ph_ref,
                  g0_ref, g1_ref, g2_ref, cw_ref, wb_ref, wo_ref, o_ref, *, tm, tiles_per_seq):
    i = pl.program_id(0)
    row = (i % tiles_per_seq) * tm + lax.broadcasted_iota(jnp.int32, (tm, 1), 0)
    valid = row >= PAD0
    z = jnp.where(valid, cc_ref[...].astype(F32) * ch_ref[...].astype(F32), 0.0)
    halo_row = (i % tiles_per_seq) * tm - 8 + lax.broadcasted_iota(jnp.int32, (8, 1), 0)
    zp = jnp.where(halo_row >= PAD0, pc_ref[...].astype(F32) * ph_ref[...].astype(F32), 0.0)
    zz = jnp.concatenate([zp, z], axis=0)
    cw = cw_ref[...]
    y = (cw[2:3, :] * z + cw[1:2, :] * zz[7:7 + tm, :] + cw[0:1, :] * zz[6:6 + tm, :])
    u_conv = jnp.where(valid, cb_ref[...].astype(F32) * y, 0.0).astype(BF16)
    mixed = jnp.zeros((tm, D_MODEL), F32)
    for n, (u, g_ref) in enumerate(((ua_ref[...], g0_ref), (u_conv, g1_ref), (ur_ref[...], g2_ref))):
        up = jnp.dot(u, wb_ref[n], preferred_element_type=F32)
        mixed = mixed + jax.nn.sigmoid(g_ref[...].astype(F32)) * up
    o_ref[...] = hs_ref[...] + jnp.dot(mixed.astype(BF16), wo_ref[...], preferred_element_type=F32)


def _merge(hs, u_att, u_hgrn, proj, conv_w, wb_bf, wo_bf, tp):
    rows = hs.shape[0]
    tm = _row_tile(tp, 384)
    kern = functools.partial(_merge_kernel, tm=tm, tiles_per_seq=tp // tm)
    row_blk = lambda w, blk: pl.BlockSpec((tm, w), lambda i: (i, blk))
    halo = lambda blk: pl.BlockSpec((8, 512), lambda i: (jnp.maximum(i * (tm // 8) - 1, 0), blk))
    const = lambda shape: pl.BlockSpec(shape, lambda i: (0,) * len(shape))
    return pl.pallas_call(
        kern,
        out_shape=jax.ShapeDtypeStruct((rows, D_MODEL), F32),
        grid=(rows // tm,),
        in_specs=[row_blk(D_MODEL, 0), row_blk(512, 0), row_blk(512, 0),
                  row_blk(512, BLK_CB), row_blk(512, BLK_CC), row_blk(512, BLK_CH),
                  halo(BLK_CC), halo(BLK_CH),
                  row_blk(1024, BLK_GATE), row_blk(1024, BLK_GATE + 1), row_blk(1024, BLK_GATE + 2),
                  const((CONV_K, 512)), const((3, 512, D_MODEL)), const((D_MODEL, D_MODEL))],
        out_specs=row_blk(D_MODEL, 0),
        compiler_params=_params(("parallel",)),
        name="merge",
    )(hs, u_att, u_hgrn, proj, proj, proj, proj, proj, proj, proj, proj, conv_w, wb_bf, wo_bf)


def _ffn_kernel(hs_ref, g_ref, wg_ref, wu_ref, wd_ref, o_ref):
    x = hs_ref[...]
    ms = jnp.mean(x * x, axis=-1, keepdims=True)
    hn = (x * lax.rsqrt(ms + EPS) * g_ref[...]).astype(BF16)
    a = jnp.dot(hn, wg_ref[...], preferred_element_type=F32)
    u = jnp.dot(hn, wu_ref[...], preferred_element_type=F32)
    h = (a * jax.nn.sigmoid(a) * u).astype(BF16)
    o_ref[...] = x + jnp.dot(h, wd_ref[...], preferred_element_type=F32)


def _ffn(hs, gain, wg_bf, wu_bf, wd_bf):
    rows = hs.shape[0]
    d_ff = wg_bf.shape[1]
    tm = _row_tile(rows, 384)
    const = lambda shape: pl.BlockSpec(shape, lambda i: (0, 0))
    return pl.pallas_call(
        _ffn_kernel,
        out_shape=jax.ShapeDtypeStruct((rows, D_MODEL), F32),
        grid=(rows // tm,),
        in_specs=[pl.BlockSpec((tm, D_MODEL), lambda i: (i, 0)), const((1, D_MODEL)),
                  const((D_MODEL, d_ff)), const((D_MODEL, d_ff)), const((d_ff, D_MODEL))],
        out_specs=pl.BlockSpec((tm, D_MODEL), lambda i: (i, 0)),
        compiler_params=_params(("parallel",)),
        name="ffn_dense",
    )(hs, gain.reshape(1, D_MODEL), wg_bf, wu_bf, wd_bf)


def _router_kernel(hs_ref, g_ref, rw_ref, idx_ref, wt_ref):
    x = hs_ref[...]
    ms = jnp.mean(x * x, axis=-1, keepdims=True)
    hn = x * lax.rsqrt(ms + EPS) * g_ref[...]
    logits = jnp.dot(hn, rw_ref[...], preferred_element_type=F32, precision=lax.Precision.HIGHEST)
    lane = lax.broadcasted_iota(jnp.int32, logits.shape, 1)
    lane_f = lane.astype(F32)
    logits = jnp.where(lane < N_EXPERTS, logits, -jnp.inf)
    m1 = jnp.max(logits, axis=-1, keepdims=True)
    i1 = jnp.min(jnp.where(logits == m1, lane_f, 128.0), axis=-1, keepdims=True)
    rest = jnp.where(lane_f == i1, -jnp.inf, logits)
    m2 = jnp.max(rest, axis=-1, keepdims=True)
    i2 = jnp.min(jnp.where(rest == m2, lane_f, 128.0), axis=-1, keepdims=True)
    e2 = jnp.exp(m2 - m1)
    w1 = 1.0 / (1.0 + e2)
    w2 = e2 / (1.0 + e2)
    idx_ref[...] = jnp.where(lane == 0, i1, jnp.where(lane == 1, i2, 0.0)).astype(jnp.int32)
    wt_ref[...] = jnp.where(lane == 0, w1, jnp.where(lane == 1, w2, 0.0))


def _router(hs, gain, router_w):
    rows = hs.shape[0]
    tm = _row_tile(rows, 384)
    rw = jnp.zeros((D_MODEL, 128), F32).at[:, :N_EXPERTS].set(router_w.astype(F32))
    blk = pl.BlockSpec((tm, 128), lambda i: (i, 0))
    return pl.pallas_call(
        _router_kernel,
        out_shape=(jax.ShapeDtypeStruct((rows, 128), jnp.int32), jax.ShapeDtypeStruct((rows, 128), F32)),
        grid=(rows // tm,),
        in_specs=[pl.BlockSpec((tm, D_MODEL), lambda i: (i, 0)),
                  pl.BlockSpec((1, D_MODEL), lambda i: (0, 0)),
                  pl.BlockSpec((D_MODEL, 128), lambda i: (0, 0))],
        out_specs=(blk, blk),
        compiler_params=_params(("parallel",)),
        name="moe_router",
    )(hs, gain.reshape(1, D_MODEL), rw)


def _gather_kernel(tok_ref, src_ref, o_ref, sem, *, tg):
    base = pl.program_id(0) * tg

    def copy(r):
        return pltpu.make_async_copy(src_ref.at[pl.ds(tok_ref[base + r], 1), :],
                                     o_ref.at[pl.ds(r, 1), :], sem)

    def start(r, c):
        copy(r).start()
        return c

    def wait(r, c):
        copy(r).wait()
        return c

    lax.fori_loop(0, tg, start, 0)
    lax.fori_loop(0, tg, wait, 0)


def _gather_rows(src, tok_of_slot, tg):
    slots = tok_of_slot.shape[0]
    width = src.shape[1]
    return pl.pallas_call(
        functools.partial(_gather_kernel, tg=tg),
        out_shape=jax.ShapeDtypeStruct((slots, width), src.dtype),
        grid_spec=pltpu.PrefetchScalarGridSpec(
            num_scalar_prefetch=1,
            grid=(slots // tg,),
            in_specs=[pl.BlockSpec(memory_space=pl.ANY)],
            out_specs=pl.BlockSpec((tg, width), lambda i, tok: (i, 0)),
            scratch_shapes=[pltpu.SemaphoreType.DMA],
        ),
        compiler_params=_params(("arbitrary",)),
        name="moe_gather",
    )(tok_of_slot, src)


def _expert_kernel(te_ref, nt_ref, x_ref, g_ref, wg_ref, wu_ref, wd_ref, o_ref, hn_ref):
    t = pl.program_id(0)
    f = pl.program_id(1)

    @pl.when(t < nt_ref[0])
    def _():
        @pl.when(f == 0)
        def _():
            x = x_ref[...]
            ms = jnp.mean(x * x, axis=-1, keepdims=True)
            hn_ref[...] = (x * lax.rsqrt(ms + EPS) * g_ref[...]).astype(BF16)
            o_ref[...] = jnp.zeros_like(o_ref)

        hn = hn_ref[...]
        a = jnp.dot(hn, wg_ref[...], preferred_element_type=F32)
        u = jnp.dot(hn, wu_ref[...], preferred_element_type=F32)
        h = (a * jax.nn.sigmoid(a) * u).astype(BF16)
        o_ref[...] += jnp.dot(h, wd_ref[...], preferred_element_type=F32)

    @pl.when(jnp.logical_and(t >= nt_ref[0], f == 0))
    def _():
        o_ref[...] = jnp.zeros_like(o_ref)


def _experts(xs, gain, tile_expert, n_tiles_used, wg_bf, wu_bf, wd_bf, tg):
    slots = xs.shape[0]
    d_ff = wg_bf.shape[2]
    n_f = 2
    tf = d_ff // n_f
    return pl.pallas_call(
        _expert_kernel,
        out_shape=jax.ShapeDtypeStruct((slots, D_MODEL), F32),
        grid_spec=pltpu.PrefetchScalarGridSpec(
            num_scalar_prefetch=2,
            grid=(slots // tg, n_f),
            in_specs=[
                pl.BlockSpec((tg, D_MODEL), lambda t, f, te, nt: (t, 0)),
                pl.BlockSpec((1, D_MODEL), lambda t, f, te, nt: (0, 0)),
                pl.BlockSpec((None, D_MODEL, tf), lambda t, f, te, nt: (te[t], 0, f)),
                pl.BlockSpec((None, D_MODEL, tf), lambda t, f, te, nt: (te[t], 0, f)),
                pl.BlockSpec((None, tf, D_MODEL), lambda t, f, te, nt: (te[t], f, 0)),
            ],
            out_specs=pl.BlockSpec((tg, D_MODEL), lambda t, f, te, nt: (t, 0)),
            scratch_shapes=[pltpu.VMEM((tg, D_MODEL), BF16)],
        ),
        compiler_params=_params(("arbitrary", "arbitrary")),
        name="moe_experts",
    )(tile_expert, n_tiles_used, xs, gain.reshape(1, D_MODEL), wg_bf, wu_bf, wd_bf)


def _combine_kernel(s1_ref, s2_ref, hs_ref, wt_ref, y_ref, o_ref, buf_ref, sem, *, tc, tiles_per_seq):
    base = (pl.program_id(0) * tiles_per_seq + pl.program_id(1)) * tc

    def copy(r, k):
        slot = (s1_ref, s2_ref)[k][base + r]
        return pltpu.make_async_copy(y_ref.at[pl.ds(slot, 1), :], buf_ref.at[k, pl.ds(r, 1), :], sem)

    def start(r, c):
        copy(r, 0).start()
        copy(r, 1).start()
        return c

    def wait(r, c):
        copy(r, 0).wait()
        copy(r, 1).wait()
        return c

    lax.fori_loop(0, tc, start, 0)
    lax.fori_loop(0, tc, wait, 0)
    wt = wt_ref[...]
    o_ref[...] = hs_ref[...] + wt[:, 0:1] * buf_ref[0] + wt[:, 1:2] * buf_ref[1]


def _combine(hs3, wts3, y, slot1, slot2, seq):
    bsz = hs3.shape[0]
    tc = 128
    tiles_per_seq = seq // tc
    off = T_PAD // tc
    kern = functools.partial(_combine_kernel, tc=tc, tiles_per_seq=tiles_per_seq)
    return pl.pallas_call(
        kern,
        out_shape=jax.ShapeDtypeStruct((bsz, seq, D_MODEL), F32),
        grid_spec=pltpu.PrefetchScalarGridSpec(
            num_scalar_prefetch=2,
            grid=(bsz, tiles_per_seq),
            in_specs=[
                pl.BlockSpec((None, tc, D_MODEL), lambda b, i, s1, s2: (b, i + off, 0)),
                pl.BlockSpec((None, tc, 128), lambda b, i, s1, s2: (b, i + off, 0)),
                pl.BlockSpec(memory_space=pl.ANY),
            ],
            out_specs=pl.BlockSpec((None, tc, D_MODEL), lambda b, i, s1, s2: (b, i, 0)),
            scratch_shapes=[pltpu.VMEM((2, tc, D_MODEL), F32), pltpu.SemaphoreType.DMA],
        ),
        compiler_params=_params(("arbitrary", "arbitrary")),
        name="moe_combine",
    )(slot1, slot2, hs3, wts3, y)


def _moe(hs, gain, router_w, wg_bf, wu_bf, wd_bf, bsz, tp):
    seq = tp - T_PAD
    n_tok = bsz * seq
    tg = 512
    idx, wts = _router(hs, gain, router_w)
    idx3 = idx.reshape(bsz, tp, 128)
    e_of = idx3[:, T_PAD:, :2].reshape(n_tok * 2)
    onehot = (e_of[:, None] == jnp.arange(N_EXPERTS)[None, :]).astype(jnp.int32)
    rank = jnp.sum((jnp.cumsum(onehot, axis=0) - onehot) * onehot, axis=1)
    counts = jnp.sum(onehot, axis=0)
    padded = ((counts + tg - 1) // tg) * tg
    ends = jnp.cumsum(padded)
    starts = ends - padded
    slot = (starts[e_of] + rank).astype(jnp.int32)
    n_slots = 2 * n_tok + N_EXPERTS * tg
    tok = jnp.arange(n_tok, dtype=jnp.int32)
    src_row = (tok // seq) * tp + T_PAD + tok % seq
    tok_of_slot = jnp.zeros((n_slots,), jnp.int32).at[slot].set(jnp.repeat(src_row, 2))
    tile_start = jnp.arange(n_slots // tg, dtype=jnp.int32) * tg
    tile_expert = jnp.minimum(jnp.searchsorted(ends, tile_start, side="right"), N_EXPERTS - 1).astype(jnp.int32)
    n_tiles_used = (ends[-1:] // tg).astype(jnp.int32)

    xs = _gather_rows(hs, tok_of_slot, tg)
    y = _experts(xs, gain, tile_expert, n_tiles_used, wg_bf, wu_bf, wd_bf, tg)
    slot2 = slot.reshape(n_tok, 2)
    return _combine(hs.reshape(bsz, tp, D_MODEL), wts.reshape(bsz, tp, 128), y,
                    slot2[:, 0], slot2[:, 1], seq)


def _permute_qk_cols(w):
    return w.reshape(D_MODEL, 2, ATT_HEADS, ATT_QK_DIM).transpose(0, 2, 1, 3).reshape(D_MODEL, 512)


def kernel(x, meta_tokens, norm1_gain, norm2_gain, w_in, q_norm_gain, k_norm_gain, diff_lambda,
           attn_sub_gain, rel_bias, conv_w, hgrn_lb_logits, hgrn_out_gain, w_branch, w_out,
           ffn_w_gate, ffn_w_up, ffn_w_down, router_w, moe_w_gate, moe_w_up, moe_w_down):
    bsz, seq, _ = x.shape
    depth = w_in.shape[0]
    tp = T_PAD + seq
    assert tp % ATT_TILE == 0 and depth == 2

    head = jnp.concatenate([jnp.zeros((PAD0, D_MODEL), x.dtype), meta_tokens.astype(x.dtype)], axis=0)
    hs = jnp.concatenate([jnp.broadcast_to(head[None], (bsz, T_PAD, D_MODEL)), x], axis=1)
    hs = hs.reshape(bsz * tp, D_MODEL)

    lb_all = jnp.cumsum(jax.nn.softmax(hgrn_lb_logits.astype(F32), axis=0), axis=0)
    lb_all = lb_all - lb_all[0]
    btab = _attn_bias_tables(rel_bias, ATT_TILE)

    out = None
    for layer in range(depth):
        w = w_in[layer]
        w_bf = jnp.concatenate([_permute_qk_cols(w[:, :512]), _permute_qk_cols(w[:, 512:1024]), w[:, 1024:]],
                               axis=1).astype(BF16)
        qk_gain = jnp.stack([jnp.tile(q_norm_gain[layer].astype(F32), 8) * (ATT_QK_DIM ** -0.5 * LOG2E),
                             jnp.tile(k_norm_gain[layer].astype(F32), 8)]).reshape(2, 1, COL_TILE)
        proj = _inproj(hs, norm1_gain[layer], w_bf, qk_gain)
        proj3 = proj.reshape(bsz, tp, IN_COLS)

        lam_init = 0.8 - 0.6 * math.exp(-0.3 * layer)
        lp = diff_lambda[layer].astype(F32)
        lam = jnp.exp(jnp.sum(lp[0] * lp[1])) - jnp.exp(jnp.sum(lp[2] * lp[3])) + lam_init
        u_att = _diff_attention(proj3, lam.reshape(1), attn_sub_gain[layer].astype(F32), btab,
                                1.0 - lam_init)

        lb = lb_all[layer]
        u_hgrn = _hgrn(proj3, jnp.log(lb), jnp.log1p(-lb), hgrn_out_gain[layer].astype(F32))

        hs = _merge(hs, u_att.reshape(bsz * tp, 512), u_hgrn.reshape(bsz * tp, 512), proj,
                    conv_w[layer].astype(F32), w_branch[layer].astype(BF16), w_out[layer].astype(BF16), tp)

        j = layer // 2
        if layer % 2 == 0:
            hs = _ffn(hs, norm2_gain[layer], ffn_w_gate[j].astype(BF16), ffn_w_up[j].astype(BF16),
                      ffn_w_down[j].astype(BF16))
        else:
            out = _moe(hs, norm2_gain[layer], router_w[j], moe_w_gate[j].astype(BF16),
                       moe_w_up[j].astype(BF16), moe_w_down[j].astype(BF16), bsz, tp)
    return out
```

```python
import functools
import math

import numpy as np
import jax
import jax.numpy as jnp
from jax import lax
from jax.experimental import pallas as pl
from jax.experimental.pallas import tpu as pltpu

F32 = jnp.float32
BF16 = jnp.bfloat16

D_MODEL = 1024
N_META = 16
EPS = 1e-6
ATT_HEADS = 4
ATT_QK_DIM = 64
ATT_V_DIM = 128
REL_BUCKETS = 32
REL_MAX_DIST = 128
CONV_K = 3
HGRN_HEADS = 4
HGRN_D = 128
N_EXPERTS = 8
IN_COLS = 8192

T_PAD = 128
PAD0 = T_PAD - N_META
ATT_TILE = 384
LOG2E = math.log2(math.e)
HGRN_CHUNK = 128
HGRN_SUB = 16
COL_TILE = 512
GROUP_MEAN_WIDTH = 256
MOE_TILE = 512
MASK = -1e30
VMEM_LIMIT = 56 * 1024 * 1024

BLK_Q, BLK_K, BLK_V = 0, 4, 8
BLK_RQ, BLK_RF, BLK_RI, BLK_RG = 24, 28, 32, 36
BLK_CB, BLK_CC, BLK_CH = 3, 4, 5
BLK_GATE = 5


def _row_tile(rows, target):
    n = rows // 128
    best = 1
    for d in range(1, n + 1):
        if n % d == 0 and d * 128 <= target:
            best = d
    return best * 128


def _params(sem, vmem=VMEM_LIMIT):
    return pltpu.CompilerParams(dimension_semantics=sem, vmem_limit_bytes=vmem)


def _inproj_kernel(x_ref, g_ref, w_ref, qkg_ref, gm_ref, o_ref, xn_ref):
    j = pl.program_id(1)

    @pl.when(j == 0)
    def _():
        x = x_ref[...]
        ms = jnp.mean(x * x, axis=-1, keepdims=True)
        xn_ref[...] = (x * lax.rsqrt(ms + EPS) * g_ref[...]).astype(BF16)

    acc = jnp.dot(xn_ref[...], w_ref[...], preferred_element_type=F32)

    @pl.when(j < 2)
    def _():
        sq = acc * acc
        hi = sq.astype(BF16)
        lo = (sq - hi.astype(F32)).astype(BF16)
        gm = gm_ref[...]
        width = gm.shape[0]
        ms = jnp.concatenate(
            [jnp.dot(hi[:, c:c + width], gm, preferred_element_type=F32)
             + jnp.dot(lo[:, c:c + width], gm, preferred_element_type=F32)
             for c in range(0, COL_TILE, width)], axis=1)
        o_ref[...] = (acc * lax.rsqrt(ms + EPS) * qkg_ref[...]).astype(BF16)

    @pl.when(j >= 2)
    def _():
        o_ref[...] = acc.astype(BF16)


def _inproj(hs, gain, w_bf, qk_gain):
    rows = hs.shape[0]
    tm = _row_tile(rows, 1536)
    n_col = IN_COLS // COL_TILE
    grp = np.arange(GROUP_MEAN_WIDTH) // ATT_QK_DIM
    gm = jnp.asarray((grp[:, None] == grp[None, :]).astype(np.float32) / ATT_QK_DIM, BF16)
    return pl.pallas_call(
        _inproj_kernel,
        out_shape=jax.ShapeDtypeStruct((rows, IN_COLS), BF16),
        grid=(rows // tm, n_col),
        in_specs=[
            pl.BlockSpec((tm, D_MODEL), lambda i, j: (i, 0)),
            pl.BlockSpec((1, D_MODEL), lambda i, j: (0, 0)),
            pl.BlockSpec((D_MODEL, COL_TILE), lambda i, j: (0, j)),
            pl.BlockSpec((None, 1, COL_TILE), lambda i, j: (jnp.minimum(j, 1), 0, 0)),
            pl.BlockSpec((GROUP_MEAN_WIDTH, GROUP_MEAN_WIDTH), lambda i, j: (0, 0)),
        ],
        out_specs=pl.BlockSpec((tm, COL_TILE), lambda i, j: (i, j)),
        scratch_shapes=[pltpu.VMEM((tm, D_MODEL), BF16)],
        compiler_params=_params(("parallel", "arbitrary")),
        name="inproj",
    )(hs, gain.reshape(1, D_MODEL), w_bf, qk_gain, gm)


def _rel_bucket_table(n_max):
    n = np.arange(n_max, dtype=np.int64)
    max_exact = REL_BUCKETS // 2
    nf = np.maximum(n, 1).astype(np.float32)
    large = max_exact + (np.log(nf / np.float32(max_exact)) / np.float32(math.log(REL_MAX_DIST / max_exact))
                         * np.float32(REL_BUCKETS - max_exact)).astype(np.int32)
    large = np.minimum(large, REL_BUCKETS - 1)
    return np.where(n < max_exact, n, large).astype(np.int32)


def _attn_bias_tables(rel_bias, t):
    bucket = _rel_bucket_table(2 * t)
    assert np.all(bucket[t + 1:] == REL_BUCKETS - 1) and np.all(np.diff(bucket) >= 0)
    first_dist = tuple(int(np.searchsorted(bucket, b, side="left")) for b in range(REL_BUCKETS))
    return pl.pallas_call(
        functools.partial(_bias_kernel, t=t, first_dist=first_dist),
        out_shape=jax.ShapeDtypeStruct((ATT_HEADS, 6, t, t), F32),
        grid=(ATT_HEADS,),
        in_specs=[pl.BlockSpec(memory_space=pltpu.SMEM)],
        out_specs=pl.BlockSpec((None, 6, t, t), lambda h: (h, 0, 0, 0)),
        compiler_params=_params(("parallel",)),
        name="attn_bias",
    )(rel_bias.astype(F32))


def _bias_kernel(rb_ref, o_ref, *, t, first_dist):
    h = pl.program_id(0)
    r = lax.broadcasted_iota(jnp.int32, (t, t), 0)
    c = lax.broadcasted_iota(jnp.int32, (t, t), 1)
    far = rb_ref[REL_BUCKETS - 1, h]

    def table(n):
        val = jnp.full((t, t), rb_ref[0, h] - far, F32)
        for b in range(1, REL_BUCKETS):
            val = jnp.where(n >= first_dist[b], rb_ref[b, h] - far, val)
        return val * LOG2E

    n0 = r - c
    diag = jnp.where(n0 >= 0, table(n0), MASK)
    near = table(n0 + t)
    zero = jnp.zeros((t, t), F32)
    for kind, tab in enumerate((diag, near, zero)):
        o_ref[kind] = tab
        o_ref[kind + 3] = jnp.where(c < PAD0, MASK, tab)


def _attn_kernel(lam_ref, q_ref, k_ref, v_ref, bt_ref, sg_ref, o_ref,
                 m_ref, acc_ref, *, t, out_scale):
    i = pl.program_id(2)
    q = q_ref[...]
    lane = lax.broadcasted_iota(jnp.int32, (1, 2 * ATT_QK_DIM), 1)
    zero = jnp.zeros_like(q)
    q_maps = (jnp.where(lane < ATT_QK_DIM, q, zero), jnp.where(lane >= ATT_QK_DIM, q, zero))
    m_ref[...] = jnp.full(m_ref.shape, MASK, F32)
    acc_ref[...] = jnp.zeros(acc_ref.shape, F32)
    nt = (((1,), (1,)), ((), ()))
    ones = jnp.ones((t, ATT_V_DIM), BF16)

    def tile(j, bias_kind):
        r0 = pl.multiple_of(j * t, t)
        ks = k_ref[pl.ds(r0, t), :]
        vs = jnp.concatenate([v_ref[pl.ds(r0, t), :], ones], axis=1)
        for mi in range(2):
            s = lax.dot_general(q_maps[mi], ks, nt, preferred_element_type=F32)
            if bias_kind is not None:
                s = s + bt_ref[bias_kind]
            m_old = m_ref[mi]
            m_new = jnp.maximum(m_old, jnp.max(s, axis=-1, keepdims=True))
            m_ref[mi] = m_new
            p = jnp.exp2(s - m_new).astype(BF16)
            acc_ref[mi] = jnp.exp2(m_old - m_new) * acc_ref[mi] + jnp.dot(p, vs, preferred_element_type=F32)

    tile(0, jnp.minimum(i, 2) + 3)

    def far(j, c):
        tile(j, None)
        return c

    lax.fori_loop(1, i - 1, far, 0)

    @pl.when(i >= 2)
    def _():
        tile(i - 1, 1)

    @pl.when(i >= 1)
    def _():
        tile(i, 0)

    a1, a2 = acc_ref[0], acc_ref[1]
    o = (a1[:, :ATT_V_DIM] / a1[:, ATT_V_DIM:]
         - lam_ref[0] * (a2[:, :ATT_V_DIM] / a2[:, ATT_V_DIM:]))
    ms = jnp.mean(o * o, axis=-1, keepdims=True)
    y = o * lax.rsqrt(ms + EPS) * (sg_ref[...] * out_scale)
    row = i * t + lax.broadcasted_iota(jnp.int32, (t, 1), 0)
    o_ref[...] = jnp.where(row >= PAD0, y, 0.0).astype(BF16)


def _diff_attention(proj3, lam, sub_gain, btab, out_scale):
    bsz, tp, _ = proj3.shape
    t = ATT_TILE
    kern = functools.partial(_attn_kernel, t=t, out_scale=out_scale)
    return pl.pallas_call(
        kern,
        out_shape=jax.ShapeDtypeStruct((bsz, tp, ATT_HEADS * ATT_V_DIM), BF16),
        grid=(bsz, ATT_HEADS, tp // t),
        in_specs=[
            pl.BlockSpec(memory_space=pltpu.SMEM),
            pl.BlockSpec((None, t, 128), lambda b, h, i: (b, i, BLK_Q + h)),
            pl.BlockSpec((None, tp, 128), lambda b, h, i: (b, 0, BLK_K + h)),
            pl.BlockSpec((None, tp, 128), lambda b, h, i: (b, 0, BLK_V + h)),
            pl.BlockSpec((None, 6, t, t), lambda b, h, i: (h, 0, 0, 0)),
            pl.BlockSpec((1, ATT_V_DIM), lambda b, h, i: (0, 0)),
        ],
        out_specs=pl.BlockSpec((None, t, 128), lambda b, h, i: (b, i, h)),
        scratch_shapes=[pltpu.VMEM((2, t, 1), F32), pltpu.VMEM((2, t, 2 * ATT_V_DIM), F32)],
        compiler_params=_params(("parallel", "parallel", "arbitrary")),
        name="diff_attn",
    )(lam, proj3, proj3, proj3, btab, sub_gain.reshape(1, ATT_V_DIM))


def _split3(x):
    h1 = x.astype(BF16)
    r1 = x - h1.astype(F32)
    h2 = r1.astype(BF16)
    h3 = (r1 - h2.astype(F32)).astype(BF16)
    return h1, h2, h3


def _hgrn_kernel(q_ref, f_ref, i_ref, g_ref, la_ref, l1m_ref, og_ref, o_ref):
    c_len, sub = HGRN_CHUNK, HGRN_SUB
    half = sub // 2
    n_chunk = q_ref.shape[0] // c_len
    la, l1m, og = la_ref[...], l1m_ref[...], og_ref[...]
    rr = lax.broadcasted_iota(jnp.int32, (c_len, c_len), 0)
    cc = lax.broadcasted_iota(jnp.int32, (c_len, c_len), 1)
    tri = jnp.where(cc <= rr, 1.0, 0.0).astype(BF16)
    row8 = lax.broadcasted_iota(jnp.int32, (half, 1), 0)
    lane8 = lax.broadcasted_iota(jnp.int32, (half, c_len), 1)
    nt = (((1,), (1,)), ((), ()))

    def chunk(c, st):
        r0 = pl.multiple_of(c * c_len, c_len)
        rows = pl.ds(r0, c_len)
        z = f_ref[rows, :].astype(F32)
        qh = q_ref[rows, :].astype(F32)
        qh = qh * jax.nn.sigmoid(qh)
        v = i_ref[rows, :]
        gate = g_ref[rows, :].astype(F32)
        sp = jnp.log(1.0 + jnp.exp(-jnp.abs(z)))
        bb = l1m + jnp.minimum(z, 0.0) - sp
        log_f = jnp.maximum(la, bb) + jnp.log(1.0 + jnp.exp(-jnp.abs(la - bb)))
        valid = (r0 + lax.broadcasted_iota(jnp.int32, (c_len, 1), 0)) >= PAD0
        log_k = jnp.where(valid, l1m + jnp.minimum(-z, 0.0) - sp, -jnp.inf)
        g = sum(jnp.dot(tri, part, preferred_element_type=F32) for part in _split3(log_f))
        ck = log_k - g
        o_inter = lax.dot_general((qh * jnp.exp(g)).astype(BF16), st.astype(BF16), nt,
                                  preferred_element_type=F32)
        a_rows = []
        for a in range(c_len // sub):
            lo = a * sub
            ga = (g[lo:lo + half, :], g[lo + half:lo + sub, :])
            qa = (qh[lo:lo + half, :], qh[lo + half:lo + sub, :])
            if a == 0:
                blk = [jnp.zeros((half, c_len), F32)] * 2
            else:
                gs = g[lo - 1:lo, :]
                qd = (qh[lo:lo + sub, :] * jnp.exp(g[lo:lo + sub, :] - gs)).astype(BF16)
                kd = jnp.exp(jnp.minimum(gs - g[:lo, :], 0.0) + log_k[:lo, :]).astype(BF16)
                kd = jnp.concatenate([kd, jnp.zeros((c_len - lo, HGRN_D), BF16)], axis=0)
                a_off = lax.dot_general(qd, kd, nt, preferred_element_type=F32)
                blk = [a_off[:half, :], a_off[half:, :]]
            for s in range(sub):
                crow = ck[lo + s:lo + s + 1, :]
                for hh in range(s // half, 2):
                    col = jnp.sum(qa[hh] * jnp.exp(ga[hh] + crow), axis=-1, keepdims=True)
                    blk[hh] = jnp.where(lane8 == lo + s, col, blk[hh])
            for hh in range(2):
                a_rows.append(jnp.where(lane8 <= lo + hh * half + row8, blk[hh], 0.0))
        a_full = jnp.concatenate(a_rows, axis=0).astype(BF16)
        o = o_inter + jnp.dot(a_full, v, preferred_element_type=F32)
        g_last = g[c_len - 1:c_len, :]
        kd = jnp.exp(g_last - g + log_k).astype(BF16)
        st = st * jnp.exp(g_last) + lax.dot_general(v, kd, (((0,), (0,)), ((), ())),
                                                    preferred_element_type=F32)
        ms = jnp.mean(o * o, axis=-1, keepdims=True)
        y = o * lax.rsqrt(ms + EPS) * og * (gate * jax.nn.sigmoid(gate))
        o_ref[rows, :] = y.astype(BF16)
        return st

    lax.fori_loop(0, n_chunk, chunk, jnp.zeros((HGRN_D, HGRN_D), F32), unroll=3)


def _hgrn(proj3, log_lb, log1m_lb, out_gain):
    bsz, tp, _ = proj3.shape
    seq = lambda blk: pl.BlockSpec((None, tp, 128), lambda b, h: (b, 0, blk + h))
    chan = pl.BlockSpec((None, 1, HGRN_D), lambda b, h: (h, 0, 0))
    return pl.pallas_call(
        _hgrn_kernel,
        out_shape=jax.ShapeDtypeStruct((bsz, tp, HGRN_HEADS * HGRN_D), BF16),
        grid=(bsz, HGRN_HEADS),
        in_specs=[seq(BLK_RQ), seq(BLK_RF), seq(BLK_RI), seq(BLK_RG), chan, chan,
                  pl.BlockSpec((1, HGRN_D), lambda b, h: (0, 0))],
        out_specs=pl.BlockSpec((None, tp, 128), lambda b, h: (b, 0, h)),
        compiler_params=_params(("parallel", "parallel")),
        name="hgrn2",
    )(proj3, proj3, proj3, proj3,
      log_lb.reshape(HGRN_HEADS, 1, HGRN_D), log1m_lb.reshape(HGRN_HEADS, 1, HGRN_D),
      out_gain.reshape(1, HGRN_D))


def _merge_kernel(hs_ref, ua_ref, ur_ref, cb_ref, cc_ref, ch_ref, pc_ref, ph_ref,
                  g0_ref, g1_ref, g2_ref, cw_ref, wb_ref, wo_ref, o_ref, *, tm, tiles_per_seq):
    i = pl.program_id(0)
    row = (i % tiles_per_seq) * tm + lax.broadcasted_iota(jnp.int32, (tm, 1), 0)
    valid = row >= PAD0
    z = jnp.where(valid, cc_ref[...].astype(F32) * ch_ref[...].astype(F32), 0.0)
    halo_row = (i % tiles_per_seq) * tm - 8 + lax.broadcasted_iota(jnp.int32, (8, 1), 0)
    zp = jnp.where(halo_row >= PAD0, pc_ref[...].astype(F32) * ph_ref[...].astype(F32), 0.0)
    zz = jnp.concatenate([zp, z], axis=0)
    cw = cw_ref[...]
    y = (cw[2:3, :] * z + cw[1:2, :] * zz[7:7 + tm, :] + cw[0:1, :] * zz[6:6 + tm, :])
    u_conv = jnp.where(valid, cb_ref[...].astype(F32) * y, 0.0).astype(BF16)
    mixed = jnp.zeros((tm, D_MODEL), F32)
    for n, (u, g_ref) in enumerate(((ua_ref[...], g0_ref), (u_conv, g1_ref), (ur_ref[...], g2_ref))):
        up = jnp.dot(u, wb_ref[n], preferred_element_type=F32)
        mixed = mixed + jax.nn.sigmoid(g_ref[...].astype(F32)) * up
    o_ref[...] = hs_ref[...] + jnp.dot(mixed.astype(BF16), wo_ref[...], preferred_element_type=F32)


def _merge(hs, u_att, u_hgrn, proj, conv_w, wb_bf, wo_bf, tp):
    rows = hs.shape[0]
    tm = _row_tile(tp, 384)
    kern = functools.partial(_merge_kernel, tm=tm, tiles_per_seq=tp // tm)
    row_blk = lambda w, blk: pl.BlockSpec((tm, w), lambda i: (i, blk))
    halo = lambda blk: pl.BlockSpec((8, 512), lambda i: (jnp.maximum(i * (tm // 8) - 1, 0), blk))
    const = lambda shape: pl.BlockSpec(shape, lambda i: (0,) * len(shape))
    return pl.pallas_call(
        kern,
        out_shape=jax.ShapeDtypeStruct((rows, D_MODEL), F32),
        grid=(rows // tm,),
        in_specs=[row_blk(D_MODEL, 0), row_blk(512, 0), row_blk(512, 0),
                  row_blk(512, BLK_CB), row_blk(512, BLK_CC), row_blk(512, BLK_CH),
                  halo(BLK_CC), halo(BLK_CH),
                  row_blk(1024, BLK_GATE), row_blk(1024, BLK_GATE + 1), row_blk(1024, BLK_GATE + 2),
                  const((CONV_K, 512)), const((3, 512, D_MODEL)), const((D_MODEL, D_MODEL))],
        out_specs=row_blk(D_MODEL, 0),
        compiler_params=_params(("parallel",)),
        name="merge",
    )(hs, u_att, u_hgrn, proj, proj, proj, proj, proj, proj, proj, proj, conv_w, wb_bf, wo_bf)


def _ffn_kernel(hs_ref, g_ref, wg_ref, wu_ref, wd_ref, o_ref):
    x = hs_ref[...]
    ms = jnp.mean(x * x, axis=-1, keepdims=True)
    hn = (x * lax.rsqrt(ms + EPS) * g_ref[...]).astype(BF16)
    a = jnp.dot(hn, wg_ref[...], preferred_element_type=F32)
    u = jnp.dot(hn, wu_ref[...], preferred_element_type=F32)
    h = (a * jax.nn.sigmoid(a) * u).astype(BF16)
    o_ref[...] = x + jnp.dot(h, wd_ref[...], preferred_element_type=F32)


def _ffn(hs, gain, wg_bf, wu_bf, wd_bf):
    rows = hs.shape[0]
    d_ff = wg_bf.shape[1]
    tm = _row_tile(rows, 384)
    const = lambda shape: pl.BlockSpec(shape, lambda i: (0, 0))
    return pl.pallas_call(
        _ffn_kernel,
        out_shape=jax.ShapeDtypeStruct((rows, D_MODEL), F32),
        grid=(rows // tm,),
        in_specs=[pl.BlockSpec((tm, D_MODEL), lambda i: (i, 0)), const((1, D_MODEL)),
                  const((D_MODEL, d_ff)), const((D_MODEL, d_ff)), const((d_ff, D_MODEL))],
        out_specs=pl.BlockSpec((tm, D_MODEL), lambda i: (i, 0)),
        compiler_params=_params(("parallel",)),
        name="ffn_dense",
    )(hs, gain.reshape(1, D_MODEL), wg_bf, wu_bf, wd_bf)


def _router_kernel(hs_ref, g_ref, rw_ref, idx_ref, wt_ref, cnt_ref, carry_ref, *, tm, tiles_per_seq):
    i = pl.program_id(0)

    @pl.when(i == 0)
    def _():
        carry_ref[...] = jnp.zeros_like(carry_ref)

    x = hs_ref[...]
    ms = jnp.mean(x * x, axis=-1, keepdims=True)
    hn = x * lax.rsqrt(ms + EPS) * g_ref[...]
    logits = jnp.dot(hn, rw_ref[...], preferred_element_type=F32, precision=lax.Precision.HIGHEST)
    lane = lax.broadcasted_iota(jnp.int32, logits.shape, 1)
    lane_f = lane.astype(F32)
    logits = jnp.where(lane < N_EXPERTS, logits, -jnp.inf)
    m1 = jnp.max(logits, axis=-1, keepdims=True)
    i1 = jnp.min(jnp.where(logits == m1, lane_f, 128.0), axis=-1, keepdims=True)
    rest = jnp.where(lane_f == i1, -jnp.inf, logits)
    m2 = jnp.max(rest, axis=-1, keepdims=True)
    i2 = jnp.min(jnp.where(rest == m2, lane_f, 128.0), axis=-1, keepdims=True)
    e2 = jnp.exp(m2 - m1)
    w1 = 1.0 / (1.0 + e2)
    w2 = e2 / (1.0 + e2)
    row = (i % tiles_per_seq) * tm + lax.broadcasted_iota(jnp.int32, (tm, 1), 0)
    real = row >= T_PAD
    hot1 = jnp.where(jnp.logical_and(real, lane_f == i1), 1.0, 0.0)
    hot2 = jnp.where(jnp.logical_and(real, lane_f == i2), 1.0, 0.0)
    both = hot1 + hot2
    rr = lax.broadcasted_iota(jnp.int32, (tm, tm), 0)
    cc = lax.broadcasted_iota(jnp.int32, (tm, tm), 1)
    earlier = jnp.where(cc < rr, 1.0, 0.0).astype(BF16)
    before = carry_ref[...] + jnp.dot(earlier, both.astype(BF16), preferred_element_type=F32)
    r1 = jnp.sum(before * hot1, axis=-1, keepdims=True)
    r2 = jnp.sum(before * hot2, axis=-1, keepdims=True)
    carry_ref[...] += jnp.sum(both, axis=0, keepdims=True)
    cnt_ref[...] = carry_ref[...]
    packed = jnp.where(lane == 0, i1, jnp.where(lane == 1, i2, jnp.where(lane == 2, r1, jnp.where(lane == 3, r2, 0.0))))
    idx_ref[...] = packed.astype(jnp.int32)
    wt_ref[...] = jnp.where(lane == 0, w1, jnp.where(lane == 1, w2, 0.0))


def _router(hs, gain, router_w, tp):
    rows = hs.shape[0]
    tm = _row_tile(tp, 384)
    rw = jnp.zeros((D_MODEL, 128), F32).at[:, :N_EXPERTS].set(router_w.astype(F32))
    blk = pl.BlockSpec((tm, 128), lambda i: (i, 0))
    return pl.pallas_call(
        functools.partial(_router_kernel, tm=tm, tiles_per_seq=tp // tm),
        out_shape=(jax.ShapeDtypeStruct((rows, 128), jnp.int32), jax.ShapeDtypeStruct((rows, 128), F32),
                   jax.ShapeDtypeStruct((1, 128), F32)),
        grid=(rows // tm,),
        in_specs=[pl.BlockSpec((tm, D_MODEL), lambda i: (i, 0)),
                  pl.BlockSpec((1, D_MODEL), lambda i: (0, 0)),
                  pl.BlockSpec((D_MODEL, 128), lambda i: (0, 0))],
        out_specs=(blk, blk, pl.BlockSpec((1, 128), lambda i: (0, 0))),
        scratch_shapes=[pltpu.VMEM((1, 128), F32)],
        compiler_params=_params(("arbitrary",)),
        name="moe_router",
    )(hs, gain.reshape(1, D_MODEL), rw)


def _split_rows(x):
    return [x[:, k * 128:(k + 1) * 128] for k in range(D_MODEL // 128)]


def _dispatch_kernel(s1_ref, s2_ref, zr_ref, hs_ref, g_ref, xs_ref, rows_ref, zero_ref, sem, zsem,
                     *, td, tg, tiles_per_seq, n_steps):
    n = pl.program_id(0) * tiles_per_seq + pl.program_id(1)
    buf = n % 2
    base = n * td

    def row_copy(r, which):
        slot = (s1_ref, s2_ref)[which][base + r]
        return pltpu.make_async_copy(rows_ref.at[buf, r], xs_ref.at[slot], sem.at[buf])

    def wait_rows(b):
        def body(r, c):
            pltpu.make_async_copy(rows_ref.at[b, 0], xs_ref.at[0], sem.at[b]).wait()
            return c
        lax.fori_loop(0, 2 * td, body, 0)

    @pl.when(n == 0)
    def _():
        zero_ref[...] = jnp.zeros_like(zero_ref)
        for e in range(2 * N_EXPERTS):
            fill = pltpu.make_async_copy(zero_ref, xs_ref.at[pl.ds(zr_ref[e], tg)], zsem)
            fill.start()
            fill.wait()

    @pl.when(n >= 2)
    def _():
        wait_rows(buf)

    x = hs_ref[...]
    ms = jnp.mean(x * x, axis=-1, keepdims=True)
    hn = x * lax.rsqrt(ms + EPS) * g_ref[...]
    for k, part in enumerate(_split_rows(hn)):
        rows_ref[buf, :, k, :] = part

    def issue(r, c):
        row_copy(r, 0).start()
        row_copy(r, 1).start()
        return c

    lax.fori_loop(0, td, issue, 0)

    @pl.when(n == n_steps - 1)
    def _():
        wait_rows(buf)
        if n_steps > 1:
            wait_rows(1 - buf)


def _dispatch(hs3, gain, slot1, slot2, zero_rows, n_slots, tg):
    bsz, tp, _ = hs3.shape
    td = T_PAD
    tiles_per_seq = (tp - T_PAD) // td
    kern = functools.partial(_dispatch_kernel, td=td, tg=tg, tiles_per_seq=tiles_per_seq,
                             n_steps=bsz * tiles_per_seq)
    return pl.pallas_call(
        kern,
        out_shape=jax.ShapeDtypeStruct((n_slots, 8, 128), F32),
        grid_spec=pltpu.PrefetchScalarGridSpec(
            num_scalar_prefetch=3,
            grid=(bsz, tiles_per_seq),
            in_specs=[pl.BlockSpec((None, td, D_MODEL), lambda b, i, *_: (b, i + 1, 0)),
                      pl.BlockSpec((1, D_MODEL), lambda b, i, *_: (0, 0))],
            out_specs=pl.BlockSpec(memory_space=pl.ANY),
            scratch_shapes=[pltpu.VMEM((2, td, 8, 128), F32), pltpu.VMEM((tg, 8, 128), F32),
                            pltpu.SemaphoreType.DMA((2,)), pltpu.SemaphoreType.DMA],
        ),
        compiler_params=_params(("arbitrary", "arbitrary")),
        name="moe_dispatch",
    )(slot1, slot2, zero_rows, hs3, gain.reshape(1, D_MODEL))


def _expert_kernel(te_ref, nt_ref, x_ref, wg_ref, wu_ref, wd_ref, o_ref, hn_ref, acc_ref, *, n_f):
    t = pl.program_id(0)
    f = pl.program_id(1)

    @pl.when(t < nt_ref[0])
    def _():
        @pl.when(f == 0)
        def _():
            hn_ref[...] = jnp.concatenate([x_ref[:, k, :] for k in range(8)], axis=1).astype(BF16)

        hn = hn_ref[...]
        a = jnp.dot(hn, wg_ref[...], preferred_element_type=F32)
        u = jnp.dot(hn, wu_ref[...], preferred_element_type=F32)
        h = (a * jax.nn.sigmoid(a) * u).astype(BF16)
        y = jnp.dot(h, wd_ref[...], preferred_element_type=F32)

        @pl.when(f == 0)
        def _():
            acc_ref[...] = y

        @pl.when(f > 0)
        def _():
            acc_ref[...] += y

    @pl.when(jnp.logical_and(t >= nt_ref[0], f == 0))
    def _():
        acc_ref[...] = jnp.zeros_like(acc_ref)

    @pl.when(f == n_f - 1)
    def _():
        for k, part in enumerate(_split_rows(acc_ref[...])):
            o_ref[:, k, :] = part


def _experts(xs, tile_expert, n_tiles_used, wg_bf, wu_bf, wd_bf, tg):
    slots = xs.shape[0]
    d_ff = wg_bf.shape[2]
    n_f = 2
    tf = d_ff // n_f

    def x_map(t, f, te, nt):
        return (jnp.minimum(t, nt[0] - 1), 0, 0)

    def f_of(t, f, nt):
        return jnp.where(t < nt[0], f, n_f - 1)

    return pl.pallas_call(
        functools.partial(_expert_kernel, n_f=n_f),
        out_shape=jax.ShapeDtypeStruct((slots, 8, 128), F32),
        grid_spec=pltpu.PrefetchScalarGridSpec(
            num_scalar_prefetch=2,
            grid=(slots // tg, n_f),
            in_specs=[
                pl.BlockSpec((tg, 8, 128), x_map),
                pl.BlockSpec((None, D_MODEL, tf), lambda t, f, te, nt: (te[t], 0, f_of(t, f, nt))),
                pl.BlockSpec((None, D_MODEL, tf), lambda t, f, te, nt: (te[t], 0, f_of(t, f, nt))),
                pl.BlockSpec((None, tf, D_MODEL), lambda t, f, te, nt: (te[t], f_of(t, f, nt), 0)),
            ],
            out_specs=pl.BlockSpec((tg, 8, 128), lambda t, f, te, nt: (t, 0, 0)),
            scratch_shapes=[pltpu.VMEM((tg, D_MODEL), BF16), pltpu.VMEM((tg, D_MODEL), F32)],
        ),
        compiler_params=_params(("arbitrary", "arbitrary")),
        name="moe_experts",
    )(tile_expert, n_tiles_used, xs, wg_bf, wu_bf, wd_bf)


def _combine_kernel(s1_ref, s2_ref, hs_ref, wt_ref, y_ref, o_ref, buf_ref, sem, *, tc, tiles_per_seq, n_steps):
    n = pl.program_id(0) * tiles_per_seq + pl.program_id(1)
    cur = n % 2

    def row_copy(step, b, r, which):
        slot = (s1_ref, s2_ref)[which][step * tc + r]
        return pltpu.make_async_copy(y_ref.at[slot], buf_ref.at[b, which, r], sem.at[b])

    def fetch(step, b):
        def body(r, c):
            row_copy(step, b, r, 0).start()
            row_copy(step, b, r, 1).start()
            return c
        lax.fori_loop(0, tc, body, 0)

    @pl.when(n == 0)
    def _():
        fetch(0, 0)

    @pl.when(n + 1 < n_steps)
    def _():
        fetch(n + 1, 1 - cur)

    def wait(r, c):
        row_copy(n, cur, r, 0).wait()
        row_copy(n, cur, r, 1).wait()
        return c

    lax.fori_loop(0, tc, wait, 0)
    wt = wt_ref[...]
    w1, w2 = wt[:, 0:1], wt[:, 1:2]
    for k, part in enumerate(_split_rows(hs_ref[...])):
        o_ref[:, k * 128:(k + 1) * 128] = part + w1 * buf_ref[cur, 0, :, k, :] + w2 * buf_ref[cur, 1, :, k, :]


def _combine(hs3, wts3, y, slot1, slot2, seq):
    bsz = hs3.shape[0]
    tc = T_PAD
    tiles_per_seq = seq // tc
    kern = functools.partial(_combine_kernel, tc=tc, tiles_per_seq=tiles_per_seq, n_steps=bsz * tiles_per_seq)
    return pl.pallas_call(
        kern,
        out_shape=jax.ShapeDtypeStruct((bsz, seq, D_MODEL), F32),
        grid_spec=pltpu.PrefetchScalarGridSpec(
            num_scalar_prefetch=2,
            grid=(bsz, tiles_per_seq),
            in_specs=[
                pl.BlockSpec((None, tc, D_MODEL), lambda b, i, s1, s2: (b, i + 1, 0)),
                pl.BlockSpec((None, tc, 128), lambda b, i, s1, s2: (b, i + 1, 0)),
                pl.BlockSpec(memory_space=pl.ANY),
            ],
            out_specs=pl.BlockSpec((None, tc, D_MODEL), lambda b, i, s1, s2: (b, i, 0)),
            scratch_shapes=[pltpu.VMEM((2, 2, tc, 8, 128), F32), pltpu.SemaphoreType.DMA((2,))],
        ),
        compiler_params=_params(("arbitrary", "arbitrary")),
        name="moe_combine",
    )(slot1, slot2, hs3, wts3, y)


def _moe(hs, gain, router_w, wg_bf, wu_bf, wd_bf, bsz, tp):
    seq = tp - T_PAD
    n_tok = bsz * seq
    tg = MOE_TILE
    idx, wts, cnt = _router(hs, gain, router_w, tp)
    sel = idx.reshape(bsz, tp, 128)[:, T_PAD:, :4].reshape(n_tok, 4)
    counts = cnt[0, :N_EXPERTS].astype(jnp.int32)
    padded = ((counts + tg - 1) // tg) * tg
    ends = jnp.cumsum(padded)
    starts = ends - padded
    experts = jnp.arange(N_EXPERTS, dtype=jnp.int32)[None, :]
    slot1 = jnp.sum(jnp.where(sel[:, 0:1] == experts, starts[None, :], 0), axis=1) + sel[:, 2]
    slot2 = jnp.sum(jnp.where(sel[:, 1:2] == experts, starts[None, :], 0), axis=1) + sel[:, 3]
    slot1, slot2 = slot1.astype(jnp.int32), slot2.astype(jnp.int32)
    n_slots = 2 * n_tok + N_EXPERTS * tg
    tile_start = jnp.arange(n_slots // tg, dtype=jnp.int32)[:, None] * tg
    tile_expert = jnp.minimum(jnp.sum((tile_start >= ends[None, :]).astype(jnp.int32), axis=1), N_EXPERTS - 1)
    n_tiles_used = (ends[-1:] // tg).astype(jnp.int32)
    tail = n_slots - tg * (1 + jnp.arange(N_EXPERTS, dtype=jnp.int32))
    zero_rows = jnp.concatenate([jnp.maximum(ends - tg, 0), tail]).astype(jnp.int32)

    hs3 = hs.reshape(bsz, tp, D_MODEL)
    xs = _dispatch(hs3, gain, slot1, slot2, zero_rows, n_slots, tg)
    y = _experts(xs, tile_expert.astype(jnp.int32), n_tiles_used, wg_bf, wu_bf, wd_bf, tg)
    return _combine(hs3, wts.reshape(bsz, tp, 128), y, slot1, slot2, seq)


def _permute_qk_cols(w):
    return w.reshape(D_MODEL, 2, ATT_HEADS, ATT_QK_DIM).transpose(0, 2, 1, 3).reshape(D_MODEL, 512)


def kernel(x, meta_tokens, norm1_gain, norm2_gain, w_in, q_norm_gain, k_norm_gain, diff_lambda,
           attn_sub_gain, rel_bias, conv_w, hgrn_lb_logits, hgrn_out_gain, w_branch, w_out,
           ffn_w_gate, ffn_w_up, ffn_w_down, router_w, moe_w_gate, moe_w_up, moe_w_down):
    bsz, seq, _ = x.shape
    depth = w_in.shape[0]
    tp = T_PAD + seq
    assert tp % ATT_TILE == 0 and depth == 2

    head = jnp.concatenate([jnp.zeros((PAD0, D_MODEL), x.dtype), meta_tokens.astype(x.dtype)], axis=0)
    hs = jnp.concatenate([jnp.broadcast_to(head[None], (bsz, T_PAD, D_MODEL)), x], axis=1)
    hs = hs.reshape(bsz * tp, D_MODEL)

    lb_all = jnp.cumsum(jax.nn.softmax(hgrn_lb_logits.astype(F32), axis=0), axis=0)
    lb_all = lb_all - lb_all[0]
    btab = _attn_bias_tables(rel_bias, ATT_TILE)

    out = None
    for layer in range(depth):
        w = w_in[layer]
        w_bf = jnp.concatenate([_permute_qk_cols(w[:, :512]), _permute_qk_cols(w[:, 512:1024]), w[:, 1024:]],
                               axis=1).astype(BF16)
        qk_gain = jnp.stack([jnp.tile(q_norm_gain[layer].astype(F32), 8) * (ATT_QK_DIM ** -0.5 * LOG2E),
                             jnp.tile(k_norm_gain[layer].astype(F32), 8)]).reshape(2, 1, COL_TILE)
        proj = _inproj(hs, norm1_gain[layer], w_bf, qk_gain)
        proj3 = proj.reshape(bsz, tp, IN_COLS)

        lam_init = 0.8 - 0.6 * math.exp(-0.3 * layer)
        lp = diff_lambda[layer].astype(F32)
        lam = jnp.exp(jnp.sum(lp[0] * lp[1])) - jnp.exp(jnp.sum(lp[2] * lp[3])) + lam_init
        u_att = _diff_attention(proj3, lam.reshape(1), attn_sub_gain[layer].astype(F32), btab,
                                1.0 - lam_init)

        lb = lb_all[layer]
        u_hgrn = _hgrn(proj3, jnp.log(lb), jnp.log1p(-lb), hgrn_out_gain[layer].astype(F32))

        hs = _merge(hs, u_att.reshape(bsz * tp, 512), u_hgrn.reshape(bsz * tp, 512), proj,
                    conv_w[layer].astype(F32), w_branch[layer].astype(BF16), w_out[layer].astype(BF16), tp)

        j = layer // 2
        if layer % 2 == 0:
            hs = _ffn(hs, norm2_gain[layer], ffn_w_gate[j].astype(BF16), ffn_w_up[j].astype(BF16),
                      ffn_w_down[j].astype(BF16))
        else:
            out = _moe(hs, norm2_gain[layer], router_w[j], moe_w_gate[j].astype(BF16),
                       moe_w_up[j].astype(BF16), moe_w_down[j].astype(BF16), bsz, tp)
    return out
```

```python
import functools
import math

import numpy as np
import jax
import jax.numpy as jnp
from jax import lax
from jax.experimental import pallas as pl
from jax.experimental.pallas import tpu as pltpu

F32 = jnp.float32
BF16 = jnp.bfloat16

D_MODEL = 1024
N_META = 16
EPS = 1e-6
ATT_HEADS = 4
ATT_QK_DIM = 64
ATT_V_DIM = 128
REL_BUCKETS = 32
REL_MAX_DIST = 128
CONV_K = 3
HGRN_HEADS = 4
HGRN_D = 128
N_EXPERTS = 8
IN_COLS = 8192

T_PAD = 128
PAD0 = T_PAD - N_META
ATT_TILE = 384
VT_ONES = 16
LOG2E = math.log2(math.e)
HGRN_CHUNK = 128
HGRN_SUB = 16
COL_TILE = 512
GROUP_MEAN_WIDTH = 256
MOE_TILE = 512
MASK = -1e30
VMEM_LIMIT = 56 * 1024 * 1024

BLK_Q, BLK_K, BLK_V = 0, 4, 8
BLK_RQ, BLK_RF, BLK_RI, BLK_RG = 24, 28, 32, 36
BLK_CB, BLK_CC, BLK_CH = 3, 4, 5
BLK_GATE = 5


def _row_tile(rows, target):
    n = rows // 128
    best = 1
    for d in range(1, n + 1):
        if n % d == 0 and d * 128 <= target:
            best = d
    return best * 128


def _params(sem, vmem=VMEM_LIMIT):
    return pltpu.CompilerParams(dimension_semantics=sem, vmem_limit_bytes=vmem)


def _inproj_kernel(x_ref, g_ref, w_ref, qkg_ref, gm_ref, o_ref, xn_ref):
    j = pl.program_id(1)

    @pl.when(j == 0)
    def _():
        x = x_ref[...]
        ms = jnp.mean(x * x, axis=-1, keepdims=True)
        xn_ref[...] = (x * lax.rsqrt(ms + EPS) * g_ref[...]).astype(BF16)

    acc = jnp.dot(xn_ref[...], w_ref[...], preferred_element_type=F32)

    @pl.when(j < 2)
    def _():
        sq = acc * acc
        hi = sq.astype(BF16)
        lo = (sq - hi.astype(F32)).astype(BF16)
        gm = gm_ref[...]
        width = gm.shape[0]
        ms = jnp.concatenate(
            [jnp.dot(hi[:, c:c + width], gm, preferred_element_type=F32)
             + jnp.dot(lo[:, c:c + width], gm, preferred_element_type=F32)
             for c in range(0, COL_TILE, width)], axis=1)
        o_ref[...] = (acc * lax.rsqrt(ms + EPS) * qkg_ref[...]).astype(BF16)

    @pl.when(j >= 2)
    def _():
        o_ref[...] = acc.astype(BF16)


def _inproj(hs, gain, w_bf, qk_gain):
    rows = hs.shape[0]
    tm = _row_tile(rows, 1536)
    n_col = IN_COLS // COL_TILE
    grp = np.arange(GROUP_MEAN_WIDTH) // ATT_QK_DIM
    gm = jnp.asarray((grp[:, None] == grp[None, :]).astype(np.float32) / ATT_QK_DIM, BF16)
    return pl.pallas_call(
        _inproj_kernel,
        out_shape=jax.ShapeDtypeStruct((rows, IN_COLS), BF16),
        grid=(rows // tm, n_col),
        in_specs=[
            pl.BlockSpec((tm, D_MODEL), lambda i, j: (i, 0)),
            pl.BlockSpec((1, D_MODEL), lambda i, j: (0, 0)),
            pl.BlockSpec((D_MODEL, COL_TILE), lambda i, j: (0, j)),
            pl.BlockSpec((None, 1, COL_TILE), lambda i, j: (jnp.minimum(j, 1), 0, 0)),
            pl.BlockSpec((GROUP_MEAN_WIDTH, GROUP_MEAN_WIDTH), lambda i, j: (0, 0)),
        ],
        out_specs=pl.BlockSpec((tm, COL_TILE), lambda i, j: (i, j)),
        scratch_shapes=[pltpu.VMEM((tm, D_MODEL), BF16)],
        compiler_params=_params(("parallel", "arbitrary")),
        name="inproj",
    )(hs, gain.reshape(1, D_MODEL), w_bf, qk_gain, gm)


def _rel_bucket_table(n_max):
    n = np.arange(n_max, dtype=np.int64)
    max_exact = REL_BUCKETS // 2
    nf = np.maximum(n, 1).astype(np.float32)
    large = max_exact + (np.log(nf / np.float32(max_exact)) / np.float32(math.log(REL_MAX_DIST / max_exact))
                         * np.float32(REL_BUCKETS - max_exact)).astype(np.int32)
    large = np.minimum(large, REL_BUCKETS - 1)
    return np.where(n < max_exact, n, large).astype(np.int32)


def _attn_bias_tables(rel_bias, t):
    bucket = _rel_bucket_table(2 * t)
    assert np.all(bucket[t + 1:] == REL_BUCKETS - 1) and np.all(np.diff(bucket) >= 0)
    first_dist = tuple(int(np.searchsorted(bucket, b, side="left")) for b in range(REL_BUCKETS))
    return pl.pallas_call(
        functools.partial(_bias_kernel, t=t, first_dist=first_dist),
        out_shape=jax.ShapeDtypeStruct((ATT_HEADS, 6, t, t), F32),
        grid=(ATT_HEADS,),
        in_specs=[pl.BlockSpec(memory_space=pltpu.SMEM)],
        out_specs=pl.BlockSpec((None, 6, t, t), lambda h: (h, 0, 0, 0)),
        compiler_params=_params(("parallel",)),
        name="attn_bias",
    )(rel_bias.astype(F32))


def _bias_kernel(rb_ref, o_ref, *, t, first_dist):
    h = pl.program_id(0)
    key = lax.broadcasted_iota(jnp.int32, (t, t), 0)
    qry = lax.broadcasted_iota(jnp.int32, (t, t), 1)
    far = rb_ref[REL_BUCKETS - 1, h]

    def table(n):
        val = jnp.full((t, t), rb_ref[0, h] - far, F32)
        for b in range(1, REL_BUCKETS):
            val = jnp.where(n >= first_dist[b], rb_ref[b, h] - far, val)
        return val * LOG2E

    n0 = qry - key
    diag = jnp.where(n0 >= 0, table(n0), MASK)
    near = table(n0 + t)
    zero = jnp.zeros((t, t), F32)
    for kind, tab in enumerate((diag, near, zero)):
        o_ref[kind] = tab
        o_ref[kind + 3] = jnp.where(key < PAD0, MASK, tab)


def _vt_kernel(v_ref, o_ref):
    o_ref[:ATT_V_DIM, :] = v_ref[...].astype(F32).T.astype(BF16)
    o_ref[ATT_V_DIM:, :] = jnp.ones((VT_ONES, o_ref.shape[1]), BF16)


def _v_transposed(proj3):
    bsz, tp, _ = proj3.shape
    t = ATT_TILE
    return pl.pallas_call(
        _vt_kernel,
        out_shape=jax.ShapeDtypeStruct((bsz, ATT_HEADS, tp // t, ATT_V_DIM + VT_ONES, t), BF16),
        grid=(bsz, ATT_HEADS, tp // t),
        in_specs=[pl.BlockSpec((None, t, 128), lambda b, h, j: (b, j, BLK_V + h))],
        out_specs=pl.BlockSpec((None, None, None, ATT_V_DIM + VT_ONES, t), lambda b, h, j: (b, h, j, 0, 0)),
        compiler_params=_params(("parallel", "parallel", "parallel")),
        name="attn_vt",
    )(proj3)


def _attn_kernel(lam_ref, q_ref, k_ref, vt_ref, bt_ref, sg_ref, o_ref, m_ref, acc_ref, *, t, out_scale):
    i = pl.program_id(2)
    q = q_ref[...]
    lane = lax.broadcasted_iota(jnp.int32, (1, 2 * ATT_QK_DIM), 1)
    zero = jnp.zeros_like(q)
    q_maps = (jnp.where(lane < ATT_QK_DIM, q, zero), jnp.where(lane >= ATT_QK_DIM, q, zero))
    m_ref[...] = jnp.full(m_ref.shape, MASK, F32)
    acc_ref[...] = jnp.zeros(acc_ref.shape, F32)
    nt = (((1,), (1,)), ((), ()))

    def scores(j):
        ks = k_ref[pl.ds(pl.multiple_of(j * t, t), t), :]
        bias = bt_ref[jnp.minimum(i - j, 2) + jnp.where(j == 0, 3, 0)]
        return tuple(lax.dot_general(ks, qm, nt, preferred_element_type=F32) + bias for qm in q_maps)

    def consume(j, s):
        vt = vt_ref[j]
        for mi in range(2):
            m_old = m_ref[mi]
            m_new = jnp.maximum(m_old, jnp.max(s[mi], axis=0, keepdims=True))
            m_ref[mi] = m_new
            p = jnp.exp2(s[mi] - m_new).astype(BF16)
            acc_ref[mi] = jnp.exp2(m_old - m_new) * acc_ref[mi] + jnp.dot(vt, p, preferred_element_type=F32)

    def step(j, s):
        nxt = scores(j + 1)
        consume(j, s)
        return nxt

    consume(i, lax.fori_loop(0, i, step, scores(0)))

    a1, a2 = acc_ref[0], acc_ref[1]
    o_t = (a1[:ATT_V_DIM] / a1[ATT_V_DIM:ATT_V_DIM + 1]
           - lam_ref[0] * (a2[:ATT_V_DIM] / a2[ATT_V_DIM:ATT_V_DIM + 1]))
    o = o_t.T
    ms = jnp.mean(o * o, axis=-1, keepdims=True)
    y = o * lax.rsqrt(ms + EPS) * (sg_ref[...] * out_scale)
    row = i * t + lax.broadcasted_iota(jnp.int32, (t, 1), 0)
    o_ref[...] = jnp.where(row >= PAD0, y, 0.0).astype(BF16)


def _diff_attention(proj3, lam, sub_gain, btab, out_scale):
    bsz, tp, _ = proj3.shape
    t = ATT_TILE
    n_t = tp // t
    rows = ATT_V_DIM + VT_ONES
    kern = functools.partial(_attn_kernel, t=t, out_scale=out_scale)
    return pl.pallas_call(
        kern,
        out_shape=jax.ShapeDtypeStruct((bsz, tp, ATT_HEADS * ATT_V_DIM), BF16),
        grid=(bsz, ATT_HEADS, n_t),
        in_specs=[
            pl.BlockSpec(memory_space=pltpu.SMEM),
            pl.BlockSpec((None, t, 128), lambda b, h, i: (b, i, BLK_Q + h)),
            pl.BlockSpec((None, tp, 128), lambda b, h, i: (b, 0, BLK_K + h)),
            pl.BlockSpec((None, None, n_t, rows, t), lambda b, h, i: (b, h, 0, 0, 0)),
            pl.BlockSpec((None, 6, t, t), lambda b, h, i: (h, 0, 0, 0)),
            pl.BlockSpec((1, ATT_V_DIM), lambda b, h, i: (0, 0)),
        ],
        out_specs=pl.BlockSpec((None, t, 128), lambda b, h, i: (b, i, h)),
        scratch_shapes=[pltpu.VMEM((2, 1, t), F32), pltpu.VMEM((2, rows, t), F32)],
        compiler_params=_params(("parallel", "parallel", "arbitrary")),
        name="diff_attn",
    )(lam, proj3, proj3, _v_transposed(proj3), btab, sub_gain.reshape(1, ATT_V_DIM))


def _split3(x):
    h1 = x.astype(BF16)
    r1 = x - h1.astype(F32)
    h2 = r1.astype(BF16)
    h3 = (r1 - h2.astype(F32)).astype(BF16)
    return h1, h2, h3


def _hgrn_kernel(q_ref, f_ref, i_ref, g_ref, la_ref, l1m_ref, og_ref, o_ref):
    c_len, sub = HGRN_CHUNK, HGRN_SUB
    half = sub // 2
    n_chunk = q_ref.shape[0] // c_len
    la, l1m, og = la_ref[...], l1m_ref[...], og_ref[...]
    rr = lax.broadcasted_iota(jnp.int32, (c_len, c_len), 0)
    cc = lax.broadcasted_iota(jnp.int32, (c_len, c_len), 1)
    tri = jnp.where(cc <= rr, 1.0, 0.0).astype(BF16)
    row8 = lax.broadcasted_iota(jnp.int32, (half, 1), 0)
    lane8 = lax.broadcasted_iota(jnp.int32, (half, c_len), 1)
    nt = (((1,), (1,)), ((), ()))

    def chunk(c, st):
        r0 = pl.multiple_of(c * c_len, c_len)
        rows = pl.ds(r0, c_len)
        z = f_ref[rows, :].astype(F32)
        qh = q_ref[rows, :].astype(F32)
        qh = qh * jax.nn.sigmoid(qh)
        v = i_ref[rows, :]
        gate = g_ref[rows, :].astype(F32)
        sp = jnp.log(1.0 + jnp.exp(-jnp.abs(z)))
        bb = l1m + jnp.minimum(z, 0.0) - sp
        log_f = jnp.maximum(la, bb) + jnp.log(1.0 + jnp.exp(-jnp.abs(la - bb)))
        valid = (r0 + lax.broadcasted_iota(jnp.int32, (c_len, 1), 0)) >= PAD0
        log_k = jnp.where(valid, l1m + jnp.minimum(-z, 0.0) - sp, -jnp.inf)
        g = sum(jnp.dot(tri, part, preferred_element_type=F32) for part in _split3(log_f))
        ck = log_k - g
        o_inter = lax.dot_general((qh * jnp.exp(g)).astype(BF16), st.astype(BF16), nt,
                                  preferred_element_type=F32)
        a_rows = []
        for a in range(c_len // sub):
            lo = a * sub
            ga = (g[lo:lo + half, :], g[lo + half:lo + sub, :])
            qa = (qh[lo:lo + half, :], qh[lo + half:lo + sub, :])
            if a == 0:
                blk = [jnp.zeros((half, c_len), F32)] * 2
            else:
                gs = g[lo - 1:lo, :]
                qd = (qh[lo:lo + sub, :] * jnp.exp(g[lo:lo + sub, :] - gs)).astype(BF16)
                kd = jnp.exp(jnp.minimum(gs - g[:lo, :], 0.0) + log_k[:lo, :]).astype(BF16)
                kd = jnp.concatenate([kd, jnp.zeros((c_len - lo, HGRN_D), BF16)], axis=0)
                a_off = lax.dot_general(qd, kd, nt, preferred_element_type=F32)
                blk = [a_off[:half, :], a_off[half:, :]]
            for s in range(sub):
                crow = ck[lo + s:lo + s + 1, :]
                for hh in range(s // half, 2):
                    col = jnp.sum(qa[hh] * jnp.exp(ga[hh] + crow), axis=-1, keepdims=True)
                    blk[hh] = jnp.where(lane8 == lo + s, col, blk[hh])
            for hh in range(2):
                a_rows.append(jnp.where(lane8 <= lo + hh * half + row8, blk[hh], 0.0))
        a_full = jnp.concatenate(a_rows, axis=0).astype(BF16)
        o = o_inter + jnp.dot(a_full, v, preferred_element_type=F32)
        g_last = g[c_len - 1:c_len, :]
        kd = jnp.exp(g_last - g + log_k).astype(BF16)
        st = st * jnp.exp(g_last) + lax.dot_general(v, kd, (((0,), (0,)), ((), ())),
                                                    preferred_element_type=F32)
        ms = jnp.mean(o * o, axis=-1, keepdims=True)
        y = o * lax.rsqrt(ms + EPS) * og * (gate * jax.nn.sigmoid(gate))
        o_ref[rows, :] = y.astype(BF16)
        return st

    lax.fori_loop(0, n_chunk, chunk, jnp.zeros((HGRN_D, HGRN_D), F32), unroll=3)


def _hgrn(proj3, log_lb, log1m_lb, out_gain):
    bsz, tp, _ = proj3.shape
    seq = lambda blk: pl.BlockSpec((None, tp, 128), lambda b, h: (b, 0, blk + h))
    chan = pl.BlockSpec((None, 1, HGRN_D), lambda b, h: (h, 0, 0))
    return pl.pallas_call(
        _hgrn_kernel,
        out_shape=jax.ShapeDtypeStruct((bsz, tp, HGRN_HEADS * HGRN_D), BF16),
        grid=(bsz, HGRN_HEADS),
        in_specs=[seq(BLK_RQ), seq(BLK_RF), seq(BLK_RI), seq(BLK_RG), chan, chan,
                  pl.BlockSpec((1, HGRN_D), lambda b, h: (0, 0))],
        out_specs=pl.BlockSpec((None, tp, 128), lambda b, h: (b, 0, h)),
        compiler_params=_params(("parallel", "parallel")),
        name="hgrn2",
    )(proj3, proj3, proj3, proj3,
      log_lb.reshape(HGRN_HEADS, 1, HGRN_D), log1m_lb.reshape(HGRN_HEADS, 1, HGRN_D),
      out_gain.reshape(1, HGRN_D))


def _merge_kernel(hs_ref, ua_ref, ur_ref, cb_ref, cc_ref, ch_ref, pc_ref, ph_ref,
                  g0_ref, g1_ref, g2_ref, cw_ref, wb_ref, wo_ref, o_ref, *, tm, tiles_per_seq):
    i = pl.program_id(0)
    row = (i % tiles_per_seq) * tm + lax.broadcasted_iota(jnp.int32, (tm, 1), 0)
    valid = row >= PAD0
    z = jnp.where(valid, cc_ref[...].astype(F32) * ch_ref[...].astype(F32), 0.0)
    halo_row = (i % tiles_per_seq) * tm - 8 + lax.broadcasted_iota(jnp.int32, (8, 1), 0)
    zp = jnp.where(halo_row >= PAD0, pc_ref[...].astype(F32) * ph_ref[...].astype(F32), 0.0)
    zz = jnp.concatenate([zp, z], axis=0)
    cw = cw_ref[...]
    y = (cw[2:3, :] * z + cw[1:2, :] * zz[7:7 + tm, :] + cw[0:1, :] * zz[6:6 + tm, :])
    u_conv = jnp.where(valid, cb_ref[...].astype(F32) * y, 0.0).astype(BF16)
    mixed = jnp.zeros((tm, D_MODEL), F32)
    for n, (u, g_ref) in enumerate(((ua_ref[...], g0_ref), (u_conv, g1_ref), (ur_ref[...], g2_ref))):
        up = jnp.dot(u, wb_ref[n], preferred_element_type=F32)
        mixed = mixed + jax.nn.sigmoid(g_ref[...].astype(F32)) * up
    o_ref[...] = hs_ref[...] + jnp.dot(mixed.astype(BF16), wo_ref[...], preferred_element_type=F32)


def _merge(hs, u_att, u_hgrn, proj, conv_w, wb_bf, wo_bf, tp):
    rows = hs.shape[0]
    tm = _row_tile(tp, 384)
    kern = functools.partial(_merge_kernel, tm=tm, tiles_per_seq=tp // tm)
    row_blk = lambda w, blk: pl.BlockSpec((tm, w), lambda i: (i, blk))
    halo = lambda blk: pl.BlockSpec((8, 512), lambda i: (jnp.maximum(i * (tm // 8) - 1, 0), blk))
    const = lambda shape: pl.BlockSpec(shape, lambda i: (0,) * len(shape))
    return pl.pallas_call(
        kern,
        out_shape=jax.ShapeDtypeStruct((rows, D_MODEL), F32),
        grid=(rows // tm,),
        in_specs=[row_blk(D_MODEL, 0), row_blk(512, 0), row_blk(512, 0),
                  row_blk(512, BLK_CB), row_blk(512, BLK_CC), row_blk(512, BLK_CH),
                  halo(BLK_CC), halo(BLK_CH),
                  row_blk(1024, BLK_GATE), row_blk(1024, BLK_GATE + 1), row_blk(1024, BLK_GATE + 2),
                  const((CONV_K, 512)), const((3, 512, D_MODEL)), const((D_MODEL, D_MODEL))],
        out_specs=row_blk(D_MODEL, 0),
        compiler_params=_params(("parallel",)),
        name="merge",
    )(hs, u_att, u_hgrn, proj, proj, proj, proj, proj, proj, proj, proj, conv_w, wb_bf, wo_bf)


def _ffn_kernel(hs_ref, g_ref, wg_ref, wu_ref, wd_ref, o_ref):
    x = hs_ref[...]
    ms = jnp.mean(x * x, axis=-1, keepdims=True)
    hn = (x * lax.rsqrt(ms + EPS) * g_ref[...]).astype(BF16)
    a = jnp.dot(hn, wg_ref[...], preferred_element_type=F32)
    u = jnp.dot(hn, wu_ref[...], preferred_element_type=F32)
    h = (a * jax.nn.sigmoid(a) * u).astype(BF16)
    o_ref[...] = x + jnp.dot(h, wd_ref[...], preferred_element_type=F32)


def _ffn(hs, gain, wg_bf, wu_bf, wd_bf):
    rows = hs.shape[0]
    d_ff = wg_bf.shape[1]
    tm = _row_tile(rows, 384)
    const = lambda shape: pl.BlockSpec(shape, lambda i: (0, 0))
    return pl.pallas_call(
        _ffn_kernel,
        out_shape=jax.ShapeDtypeStruct((rows, D_MODEL), F32),
        grid=(rows // tm,),
        in_specs=[pl.BlockSpec((tm, D_MODEL), lambda i: (i, 0)), const((1, D_MODEL)),
                  const((D_MODEL, d_ff)), const((D_MODEL, d_ff)), const((d_ff, D_MODEL))],
        out_specs=pl.BlockSpec((tm, D_MODEL), lambda i: (i, 0)),
        compiler_params=_params(("parallel",)),
        name="ffn_dense",
    )(hs, gain.reshape(1, D_MODEL), wg_bf, wu_bf, wd_bf)


def _router_kernel(hs_ref, g_ref, rw_ref, idx_ref, wt_ref, cnt_ref, carry_ref, *, tm, tiles_per_seq):
    i = pl.program_id(0)

    @pl.when(i == 0)
    def _():
        carry_ref[...] = jnp.zeros_like(carry_ref)

    x = hs_ref[...]
    ms = jnp.mean(x * x, axis=-1, keepdims=True)
    hn = x * lax.rsqrt(ms + EPS) * g_ref[...]
    logits = jnp.dot(hn, rw_ref[...], preferred_element_type=F32, precision=lax.Precision.HIGHEST)
    lane = lax.broadcasted_iota(jnp.int32, logits.shape, 1)
    lane_f = lane.astype(F32)
    logits = jnp.where(lane < N_EXPERTS, logits, -jnp.inf)
    m1 = jnp.max(logits, axis=-1, keepdims=True)
    i1 = jnp.min(jnp.where(logits == m1, lane_f, 128.0), axis=-1, keepdims=True)
    rest = jnp.where(lane_f == i1, -jnp.inf, logits)
    m2 = jnp.max(rest, axis=-1, keepdims=True)
    i2 = jnp.min(jnp.where(rest == m2, lane_f, 128.0), axis=-1, keepdims=True)
    e2 = jnp.exp(m2 - m1)
    w1 = 1.0 / (1.0 + e2)
    w2 = e2 / (1.0 + e2)
    row = (i % tiles_per_seq) * tm + lax.broadcasted_iota(jnp.int32, (tm, 1), 0)
    real = row >= T_PAD
    hot1 = jnp.where(jnp.logical_and(real, lane_f == i1), 1.0, 0.0)
    hot2 = jnp.where(jnp.logical_and(real, lane_f == i2), 1.0, 0.0)
    both = hot1 + hot2
    rr = lax.broadcasted_iota(jnp.int32, (tm, tm), 0)
    cc = lax.broadcasted_iota(jnp.int32, (tm, tm), 1)
    earlier = jnp.where(cc < rr, 1.0, 0.0).astype(BF16)
    before = carry_ref[...] + jnp.dot(earlier, both.astype(BF16), preferred_element_type=F32)
    r1 = jnp.sum(before * hot1, axis=-1, keepdims=True)
    r2 = jnp.sum(before * hot2, axis=-1, keepdims=True)
    carry_ref[...] += jnp.sum(both, axis=0, keepdims=True)
    cnt_ref[...] = carry_ref[...]
    packed = jnp.where(lane == 0, i1, jnp.where(lane == 1, i2, jnp.where(lane == 2, r1, jnp.where(lane == 3, r2, 0.0))))
    idx_ref[...] = packed.astype(jnp.int32)
    wt_ref[...] = jnp.where(lane == 0, w1, jnp.where(lane == 1, w2, 0.0))


def _router(hs, gain, router_w, tp):
    rows = hs.shape[0]
    tm = _row_tile(tp, 384)
    rw = jnp.zeros((D_MODEL, 128), F32).at[:, :N_EXPERTS].set(router_w.astype(F32))
    blk = pl.BlockSpec((tm, 128), lambda i: (i, 0))
    return pl.pallas_call(
        functools.partial(_router_kernel, tm=tm, tiles_per_seq=tp // tm),
        out_shape=(jax.ShapeDtypeStruct((rows, 128), jnp.int32), jax.ShapeDtypeStruct((rows, 128), F32),
                   jax.ShapeDtypeStruct((1, 128), F32)),
        grid=(rows // tm,),
        in_specs=[pl.BlockSpec((tm, D_MODEL), lambda i: (i, 0)),
                  pl.BlockSpec((1, D_MODEL), lambda i: (0, 0)),
                  pl.BlockSpec((D_MODEL, 128), lambda i: (0, 0))],
        out_specs=(blk, blk, pl.BlockSpec((1, 128), lambda i: (0, 0))),
        scratch_shapes=[pltpu.VMEM((1, 128), F32)],
        compiler_params=_params(("arbitrary",)),
        name="moe_router",
    )(hs, gain.reshape(1, D_MODEL), rw)


def _split_rows(x):
    return [x[:, k * 128:(k + 1) * 128] for k in range(D_MODEL // 128)]


def _dispatch_kernel(s1_ref, s2_ref, zr_ref, hs_ref, g_ref, xs_ref, rows_ref, zero_ref, sem, zsem,
                     *, td, tg, tiles_per_seq, n_steps):
    n = pl.program_id(0) * tiles_per_seq + pl.program_id(1)
    buf = n % 2
    base = n * td

    def row_copy(r, which):
        slot = (s1_ref, s2_ref)[which][base + r]
        return pltpu.make_async_copy(rows_ref.at[buf, r], xs_ref.at[slot], sem.at[buf])

    def wait_rows(b):
        def body(r, c):
            pltpu.make_async_copy(rows_ref.at[b, 0], xs_ref.at[0], sem.at[b]).wait()
            return c
        lax.fori_loop(0, 2 * td, body, 0)

    @pl.when(n == 0)
    def _():
        zero_ref[...] = jnp.zeros_like(zero_ref)
        for e in range(2 * N_EXPERTS):
            fill = pltpu.make_async_copy(zero_ref, xs_ref.at[pl.ds(zr_ref[e], tg)], zsem)
            fill.start()
            fill.wait()

    @pl.when(n >= 2)
    def _():
        wait_rows(buf)

    x = hs_ref[...]
    ms = jnp.mean(x * x, axis=-1, keepdims=True)
    hn = x * lax.rsqrt(ms + EPS) * g_ref[...]
    for k, part in enumerate(_split_rows(hn)):
        rows_ref[buf, :, k, :] = part

    def issue(r, c):
        row_copy(r, 0).start()
        row_copy(r, 1).start()
        return c

    lax.fori_loop(0, td, issue, 0)

    @pl.when(n == n_steps - 1)
    def _():
        wait_rows(buf)
        if n_steps > 1:
            wait_rows(1 - buf)


def _dispatch(hs3, gain, slot1, slot2, zero_rows, n_slots, tg):
    bsz, tp, _ = hs3.shape
    td = T_PAD
    tiles_per_seq = (tp - T_PAD) // td
    kern = functools.partial(_dispatch_kernel, td=td, tg=tg, tiles_per_seq=tiles_per_seq,
                             n_steps=bsz * tiles_per_seq)
    return pl.pallas_call(
        kern,
        out_shape=jax.ShapeDtypeStruct((n_slots, 8, 128), F32),
        grid_spec=pltpu.PrefetchScalarGridSpec(
            num_scalar_prefetch=3,
            grid=(bsz, tiles_per_seq),
            in_specs=[pl.BlockSpec((None, td, D_MODEL), lambda b, i, *_: (b, i + 1, 0)),
                      pl.BlockSpec((1, D_MODEL), lambda b, i, *_: (0, 0))],
            out_specs=pl.BlockSpec(memory_space=pl.ANY),
            scratch_shapes=[pltpu.VMEM((2, td, 8, 128), F32), pltpu.VMEM((tg, 8, 128), F32),
                            pltpu.SemaphoreType.DMA((2,)), pltpu.SemaphoreType.DMA],
        ),
        compiler_params=_params(("arbitrary", "arbitrary")),
        name="moe_dispatch",
    )(slot1, slot2, zero_rows, hs3, gain.reshape(1, D_MODEL))


def _expert_kernel(te_ref, nt_ref, x_ref, wg_ref, wu_ref, wd_ref, o_ref, hn_ref, acc_ref, *, n_f):
    t = pl.program_id(0)
    f = pl.program_id(1)

    @pl.when(t < nt_ref[0])
    def _():
        @pl.when(f == 0)
        def _():
            hn_ref[...] = jnp.concatenate([x_ref[:, k, :] for k in range(8)], axis=1).astype(BF16)

        hn = hn_ref[...]
        a = jnp.dot(hn, wg_ref[...], preferred_element_type=F32)
        u = jnp.dot(hn, wu_ref[...], preferred_element_type=F32)
        h = (a * jax.nn.sigmoid(a) * u).astype(BF16)
        y = jnp.dot(h, wd_ref[...], preferred_element_type=F32)

        @pl.when(f == 0)
        def _():
            acc_ref[...] = y

        @pl.when(f > 0)
        def _():
            acc_ref[...] += y

    @pl.when(jnp.logical_and(t >= nt_ref[0], f == 0))
    def _():
        acc_ref[...] = jnp.zeros_like(acc_ref)

    @pl.when(f == n_f - 1)
    def _():
        for k, part in enumerate(_split_rows(acc_ref[...])):
            o_ref[:, k, :] = part


def _experts(xs, tile_expert, n_tiles_used, wg_bf, wu_bf, wd_bf, tg):
    slots = xs.shape[0]
    d_ff = wg_bf.shape[2]
    n_f = 2
    tf = d_ff // n_f

    def x_map(t, f, te, nt):
        return (jnp.minimum(t, nt[0] - 1), 0, 0)

    def f_of(t, f, nt):
        return jnp.where(t < nt[0], f, n_f - 1)

    return pl.pallas_call(
        functools.partial(_expert_kernel, n_f=n_f),
        out_shape=jax.ShapeDtypeStruct((slots, 8, 128), F32),
        grid_spec=pltpu.PrefetchScalarGridSpec(
            num_scalar_prefetch=2,
            grid=(slots // tg, n_f),
            in_specs=[
                pl.BlockSpec((tg, 8, 128), x_map),
                pl.BlockSpec((None, D_MODEL, tf), lambda t, f, te, nt: (te[t], 0, f_of(t, f, nt))),
                pl.BlockSpec((None, D_MODEL, tf), lambda t, f, te, nt: (te[t], 0, f_of(t, f, nt))),
                pl.BlockSpec((None, tf, D_MODEL), lambda t, f, te, nt: (te[t], f_of(t, f, nt), 0)),
            ],
            out_specs=pl.BlockSpec((tg, 8, 128), lambda t, f, te, nt: (t, 0, 0)),
            scratch_shapes=[pltpu.VMEM((tg, D_MODEL), BF16), pltpu.VMEM((tg, D_MODEL), F32)],
        ),
        compiler_params=_params(("arbitrary", "arbitrary")),
        name="moe_experts",
    )(tile_expert, n_tiles_used, xs, wg_bf, wu_bf, wd_bf)


def _combine_kernel(s1_ref, s2_ref, hs_ref, wt_ref, y_ref, o_ref, buf_ref, sem, *, tc, tiles_per_seq, n_steps):
    n = pl.program_id(0) * tiles_per_seq + pl.program_id(1)
    cur = n % 2

    def row_copy(step, b, r, which):
        slot = (s1_ref, s2_ref)[which][step * tc + r]
        return pltpu.make_async_copy(y_ref.at[slot], buf_ref.at[b, which, r], sem.at[b])

    def fetch(step, b):
        def body(r, c):
            row_copy(step, b, r, 0).start()
            row_copy(step, b, r, 1).start()
            return c
        lax.fori_loop(0, tc, body, 0)

    @pl.when(n == 0)
    def _():
        fetch(0, 0)

    @pl.when(n + 1 < n_steps)
    def _():
        fetch(n + 1, 1 - cur)

    def wait(r, c):
        row_copy(n, cur, r, 0).wait()
        row_copy(n, cur, r, 1).wait()
        return c

    lax.fori_loop(0, tc, wait, 0)
    wt = wt_ref[...]
    w1, w2 = wt[:, 0:1], wt[:, 1:2]
    for k, part in enumerate(_split_rows(hs_ref[...])):
        o_ref[:, k * 128:(k + 1) * 128] = part + w1 * buf_ref[cur, 0, :, k, :] + w2 * buf_ref[cur, 1, :, k, :]


def _combine(hs3, wts3, y, slot1, slot2, seq):
    bsz = hs3.shape[0]
    tc = T_PAD
    tiles_per_seq = seq // tc
    kern = functools.partial(_combine_kernel, tc=tc, tiles_per_seq=tiles_per_seq, n_steps=bsz * tiles_per_seq)
    return pl.pallas_call(
        kern,
        out_shape=jax.ShapeDtypeStruct((bsz, seq, D_MODEL), F32),
        grid_spec=pltpu.PrefetchScalarGridSpec(
            num_scalar_prefetch=2,
            grid=(bsz, tiles_per_seq),
            in_specs=[
                pl.BlockSpec((None, tc, D_MODEL), lambda b, i, s1, s2: (b, i + 1, 0)),
                pl.BlockSpec((None, tc, 128), lambda b, i, s1, s2: (b, i + 1, 0)),
                pl.BlockSpec(memory_space=pl.ANY),
            ],
            out_specs=pl.BlockSpec((None, tc, D_MODEL), lambda b, i, s1, s2: (b, i, 0)),
            scratch_shapes=[pltpu.VMEM((2, 2, tc, 8, 128), F32), pltpu.SemaphoreType.DMA((2,))],
        ),
        compiler_params=_params(("arbitrary", "arbitrary")),
        name="moe_combine",
    )(slot1, slot2, hs3, wts3, y)


def _moe(hs, gain, router_w, wg_bf, wu_bf, wd_bf, bsz, tp):
    seq = tp - T_PAD
    n_tok = bsz * seq
    tg = MOE_TILE
    idx, wts, cnt = _router(hs, gain, router_w, tp)
    sel = idx.reshape(bsz, tp, 128)[:, T_PAD:, :4].reshape(n_tok, 4)
    counts = cnt[0, :N_EXPERTS].astype(jnp.int32)
    padded = ((counts + tg - 1) // tg) * tg
    ends = jnp.cumsum(padded)
    starts = ends - padded
    experts = jnp.arange(N_EXPERTS, dtype=jnp.int32)[None, :]
    slot1 = jnp.sum(jnp.where(sel[:, 0:1] == experts, starts[None, :], 0), axis=1) + sel[:, 2]
    slot2 = jnp.sum(jnp.where(sel[:, 1:2] == experts, starts[None, :], 0), axis=1) + sel[:, 3]
    slot1, slot2 = slot1.astype(jnp.int32), slot2.astype(jnp.int32)
    n_slots = 2 * n_tok + N_EXPERTS * tg
    tile_start = jnp.arange(n_slots // tg, dtype=jnp.int32)[:, None] * tg
    tile_expert = jnp.minimum(jnp.sum((tile_start >= ends[None, :]).astype(jnp.int32), axis=1), N_EXPERTS - 1)
    n_tiles_used = (ends[-1:] // tg).astype(jnp.int32)
    tail = n_slots - tg * (1 + jnp.arange(N_EXPERTS, dtype=jnp.int32))
    zero_rows = jnp.concatenate([jnp.maximum(ends - tg, 0), tail]).astype(jnp.int32)

    hs3 = hs.reshape(bsz, tp, D_MODEL)
    xs = _dispatch(hs3, gain, slot1, slot2, zero_rows, n_slots, tg)
    y = _experts(xs, tile_expert.astype(jnp.int32), n_tiles_used, wg_bf, wu_bf, wd_bf, tg)
    return _combine(hs3, wts.reshape(bsz, tp, 128), y, slot1, slot2, seq)


def _permute_qk_cols(w):
    return w.reshape(D_MODEL, 2, ATT_HEADS, ATT_QK_DIM).transpose(0, 2, 1, 3).reshape(D_MODEL, 512)


def kernel(x, meta_tokens, norm1_gain, norm2_gain, w_in, q_norm_gain, k_norm_gain, diff_lambda,
           attn_sub_gain, rel_bias, conv_w, hgrn_lb_logits, hgrn_out_gain, w_branch, w_out,
           ffn_w_gate, ffn_w_up, ffn_w_down, router_w, moe_w_gate, moe_w_up, moe_w_down):
    bsz, seq, _ = x.shape
    depth = w_in.shape[0]
    tp = T_PAD + seq
    assert tp % ATT_TILE == 0 and depth == 2

    head = jnp.concatenate([jnp.zeros((PAD0, D_MODEL), x.dtype), meta_tokens.astype(x.dtype)], axis=0)
    hs = jnp.concatenate([jnp.broadcast_to(head[None], (bsz, T_PAD, D_MODEL)), x], axis=1)
    hs = hs.reshape(bsz * tp, D_MODEL)

    lb_all = jnp.cumsum(jax.nn.softmax(hgrn_lb_logits.astype(F32), axis=0), axis=0)
    lb_all = lb_all - lb_all[0]
    btab = _attn_bias_tables(rel_bias, ATT_TILE)

    out = None
    for layer in range(depth):
        w = w_in[layer]
        w_bf = jnp.concatenate([_permute_qk_cols(w[:, :512]), _permute_qk_cols(w[:, 512:1024]), w[:, 1024:]],
                               axis=1).astype(BF16)
        qk_gain = jnp.stack([jnp.tile(q_norm_gain[layer].astype(F32), 8) * (ATT_QK_DIM ** -0.5 * LOG2E),
                             jnp.tile(k_norm_gain[layer].astype(F32), 8)]).reshape(2, 1, COL_TILE)
        proj = _inproj(hs, norm1_gain[layer], w_bf, qk_gain)
        proj3 = proj.reshape(bsz, tp, IN_COLS)

        lam_init = 0.8 - 0.6 * math.exp(-0.3 * layer)
        lp = diff_lambda[layer].astype(F32)
        lam = jnp.exp(jnp.sum(lp[0] * lp[1])) - jnp.exp(jnp.sum(lp[2] * lp[3])) + lam_init
        u_att = _diff_attention(proj3, lam.reshape(1), attn_sub_gain[layer].astype(F32), btab,
                                1.0 - lam_init)

        lb = lb_all[layer]
        u_hgrn = _hgrn(proj3, jnp.log(lb), jnp.log1p(-lb), hgrn_out_gain[layer].astype(F32))

        hs = _merge(hs, u_att.reshape(bsz * tp, 512), u_hgrn.reshape(bsz * tp, 512), proj,
                    conv_w[layer].astype(F32), w_branch[layer].astype(BF16), w_out[layer].astype(BF16), tp)

        j = layer // 2
        if layer % 2 == 0:
            hs = _ffn(hs, norm2_gain[layer], ffn_w_gate[j].astype(BF16), ffn_w_up[j].astype(BF16),
                      ffn_w_down[j].astype(BF16))
        else:
            out = _moe(hs, norm2_gain[layer], router_w[j], moe_w_gate[j].astype(BF16),
                       moe_w_up[j].astype(BF16), moe_w_down[j].astype(BF16), bsz, tp)
    return out
```

```python
import functools
import math

import numpy as np
import jax
import jax.numpy as jnp
from jax import lax
from jax.experimental import pallas as pl
from jax.experimental.pallas import tpu as pltpu

F32 = jnp.float32
BF16 = jnp.bfloat16

D_MODEL = 1024
N_META = 16
EPS = 1e-6
ATT_HEADS = 4
ATT_QK_DIM = 64
ATT_V_DIM = 128
REL_BUCKETS = 32
REL_MAX_DIST = 128
CONV_K = 3
HGRN_HEADS = 4
HGRN_D = 128
N_EXPERTS = 8
IN_COLS = 8192

T_PAD = 128
PAD0 = T_PAD - N_META
ATT_TILE = 384
VT_ONES = 16
LOG2E = math.log2(math.e)
HGRN_CHUNK = 128
HGRN_SUB = 16
COL_TILE = 512
GROUP_MEAN_WIDTH = 256
MOE_TILE = 512
MASK = -1e30
VMEM_LIMIT = 56 * 1024 * 1024

BLK_Q, BLK_K, BLK_V = 0, 4, 8
BLK_RQ, BLK_RF, BLK_RI, BLK_RG = 24, 28, 32, 36
BLK_CB, BLK_CC, BLK_CH = 3, 4, 5
BLK_GATE = 5


def _row_tile(rows, target):
    n = rows // 128
    best = 1
    for d in range(1, n + 1):
        if n % d == 0 and d * 128 <= target:
            best = d
    return best * 128


def _params(sem, vmem=VMEM_LIMIT):
    return pltpu.CompilerParams(dimension_semantics=sem, vmem_limit_bytes=vmem)


def _inproj_kernel(x_ref, g_ref, w_ref, qkg_ref, gm_ref, o_ref, xn_ref):
    j = pl.program_id(1)

    @pl.when(j == 0)
    def _():
        x = x_ref[...]
        ms = jnp.mean(x * x, axis=-1, keepdims=True)
        xn_ref[...] = (x * lax.rsqrt(ms + EPS) * g_ref[...]).astype(BF16)

    acc = jnp.dot(xn_ref[...], w_ref[...], preferred_element_type=F32)

    @pl.when(j < 2)
    def _():
        sq = acc * acc
        hi = sq.astype(BF16)
        lo = (sq - hi.astype(F32)).astype(BF16)
        gm = gm_ref[...]
        width = gm.shape[0]
        ms = jnp.concatenate(
            [jnp.dot(hi[:, c:c + width], gm, preferred_element_type=F32)
             + jnp.dot(lo[:, c:c + width], gm, preferred_element_type=F32)
             for c in range(0, COL_TILE, width)], axis=1)
        o_ref[...] = (acc * lax.rsqrt(ms + EPS) * qkg_ref[...]).astype(BF16)

    @pl.when(j >= 2)
    def _():
        o_ref[...] = acc.astype(BF16)


def _inproj(hs, gain, w_bf, qk_gain):
    rows = hs.shape[0]
    tm = _row_tile(rows, 1536)
    n_col = IN_COLS // COL_TILE
    grp = np.arange(GROUP_MEAN_WIDTH) // ATT_QK_DIM
    gm = jnp.asarray((grp[:, None] == grp[None, :]).astype(np.float32) / ATT_QK_DIM, BF16)
    return pl.pallas_call(
        _inproj_kernel,
        out_shape=jax.ShapeDtypeStruct((rows, IN_COLS), BF16),
        grid=(rows // tm, n_col),
        in_specs=[
            pl.BlockSpec((tm, D_MODEL), lambda i, j: (i, 0)),
            pl.BlockSpec((1, D_MODEL), lambda i, j: (0, 0)),
            pl.BlockSpec((D_MODEL, COL_TILE), lambda i, j: (0, j)),
            pl.BlockSpec((None, 1, COL_TILE), lambda i, j: (jnp.minimum(j, 1), 0, 0)),
            pl.BlockSpec((GROUP_MEAN_WIDTH, GROUP_MEAN_WIDTH), lambda i, j: (0, 0)),
        ],
        out_specs=pl.BlockSpec((tm, COL_TILE), lambda i, j: (i, j)),
        scratch_shapes=[pltpu.VMEM((tm, D_MODEL), BF16)],
        compiler_params=_params(("parallel", "arbitrary")),
        name="inproj",
    )(hs, gain.reshape(1, D_MODEL), w_bf, qk_gain, gm)


def _rel_bucket_table(n_max):
    n = np.arange(n_max, dtype=np.int64)
    max_exact = REL_BUCKETS // 2
    nf = np.maximum(n, 1).astype(np.float32)
    large = max_exact + (np.log(nf / np.float32(max_exact)) / np.float32(math.log(REL_MAX_DIST / max_exact))
                         * np.float32(REL_BUCKETS - max_exact)).astype(np.int32)
    large = np.minimum(large, REL_BUCKETS - 1)
    return np.where(n < max_exact, n, large).astype(np.int32)


def _attn_bias_tables(rel_bias, t):
    bucket = _rel_bucket_table(2 * t)
    assert np.all(bucket[t + 1:] == REL_BUCKETS - 1) and np.all(np.diff(bucket) >= 0)
    first_dist = tuple(int(np.searchsorted(bucket, b, side="left")) for b in range(REL_BUCKETS))
    return pl.pallas_call(
        functools.partial(_bias_kernel, t=t, first_dist=first_dist),
        out_shape=jax.ShapeDtypeStruct((ATT_HEADS, 6, t, t), F32),
        grid=(ATT_HEADS,),
        in_specs=[pl.BlockSpec(memory_space=pltpu.SMEM)],
        out_specs=pl.BlockSpec((None, 6, t, t), lambda h: (h, 0, 0, 0)),
        compiler_params=_params(("parallel",)),
        name="attn_bias",
    )(rel_bias.astype(F32))


def _bias_kernel(rb_ref, o_ref, *, t, first_dist):
    h = pl.program_id(0)
    key = lax.broadcasted_iota(jnp.int32, (t, t), 0)
    qry = lax.broadcasted_iota(jnp.int32, (t, t), 1)
    far = rb_ref[REL_BUCKETS - 1, h]

    def table(n):
        val = jnp.full((t, t), rb_ref[0, h] - far, F32)
        for b in range(1, REL_BUCKETS):
            val = jnp.where(n >= first_dist[b], rb_ref[b, h] - far, val)
        return val * LOG2E

    n0 = qry - key
    diag = jnp.where(n0 >= 0, table(n0), MASK)
    near = table(n0 + t)
    zero = jnp.zeros((t, t), F32)
    for kind, tab in enumerate((diag, near, zero)):
        o_ref[kind] = tab
        o_ref[kind + 3] = jnp.where(key < PAD0, MASK, tab)


def _vt_kernel(v_ref, o_ref):
    o_ref[:ATT_V_DIM, :] = v_ref[...].astype(F32).T.astype(BF16)
    o_ref[ATT_V_DIM:, :] = jnp.ones((VT_ONES, o_ref.shape[1]), BF16)


def _v_transposed(proj3):
    bsz, tp, _ = proj3.shape
    t = ATT_TILE
    return pl.pallas_call(
        _vt_kernel,
        out_shape=jax.ShapeDtypeStruct((bsz, ATT_HEADS, tp // t, ATT_V_DIM + VT_ONES, t), BF16),
        grid=(bsz, ATT_HEADS, tp // t),
        in_specs=[pl.BlockSpec((None, t, 128), lambda b, h, j: (b, j, BLK_V + h))],
        out_specs=pl.BlockSpec((None, None, None, ATT_V_DIM + VT_ONES, t), lambda b, h, j: (b, h, j, 0, 0)),
        compiler_params=_params(("parallel", "parallel", "parallel")),
        name="attn_vt",
    )(proj3)


def _attn_kernel(lam_ref, q_ref, k_ref, vt_ref, bt_ref, sg_ref, o_ref, m_ref, acc_ref, *, t, out_scale):
    i = pl.program_id(2)
    q = q_ref[...]
    lane = lax.broadcasted_iota(jnp.int32, (1, 2 * ATT_QK_DIM), 1)
    zero = jnp.zeros_like(q)
    q_maps = (jnp.where(lane < ATT_QK_DIM, q, zero), jnp.where(lane >= ATT_QK_DIM, q, zero))
    m_ref[...] = jnp.full(m_ref.shape, MASK, F32)
    acc_ref[...] = jnp.zeros(acc_ref.shape, F32)
    nt = (((1,), (1,)), ((), ()))

    def scores(j):
        ks = k_ref[pl.ds(pl.multiple_of(j * t, t), t), :]
        bias = bt_ref[jnp.minimum(i - j, 2) + jnp.where(j == 0, 3, 0)]
        return tuple(lax.dot_general(ks, qm, nt, preferred_element_type=F32) + bias for qm in q_maps)

    def consume(j, s):
        vt = vt_ref[j]
        for mi in range(2):
            m_old = m_ref[mi]
            m_new = jnp.maximum(m_old, jnp.max(s[mi], axis=0, keepdims=True))
            m_ref[mi] = m_new
            p = jnp.exp2(s[mi] - m_new).astype(BF16)
            acc_ref[mi] = jnp.exp2(m_old - m_new) * acc_ref[mi] + jnp.dot(vt, p, preferred_element_type=F32)

    def step(j, s):
        nxt = scores(j + 1)
        consume(j, s)
        return nxt

    consume(i, lax.fori_loop(0, i, step, scores(0)))

    a1, a2 = acc_ref[0], acc_ref[1]
    o_t = (a1[:ATT_V_DIM] / a1[ATT_V_DIM:ATT_V_DIM + 1]
           - lam_ref[0] * (a2[:ATT_V_DIM] / a2[ATT_V_DIM:ATT_V_DIM + 1]))
    o = o_t.T
    ms = jnp.mean(o * o, axis=-1, keepdims=True)
    y = o * lax.rsqrt(ms + EPS) * (sg_ref[...] * out_scale)
    row = i * t + lax.broadcasted_iota(jnp.int32, (t, 1), 0)
    o_ref[...] = jnp.where(row >= PAD0, y, 0.0).astype(BF16)


def _diff_attention(proj3, lam, sub_gain, btab, out_scale):
    bsz, tp, _ = proj3.shape
    t = ATT_TILE
    n_t = tp // t
    rows = ATT_V_DIM + VT_ONES
    kern = functools.partial(_attn_kernel, t=t, out_scale=out_scale)
    return pl.pallas_call(
        kern,
        out_shape=jax.ShapeDtypeStruct((bsz, tp, ATT_HEADS * ATT_V_DIM), BF16),
        grid=(bsz, ATT_HEADS, n_t),
        in_specs=[
            pl.BlockSpec(memory_space=pltpu.SMEM),
            pl.BlockSpec((None, t, 128), lambda b, h, i: (b, i, BLK_Q + h)),
            pl.BlockSpec((None, tp, 128), lambda b, h, i: (b, 0, BLK_K + h)),
            pl.BlockSpec((None, None, n_t, rows, t), lambda b, h, i: (b, h, 0, 0, 0)),
            pl.BlockSpec((None, 6, t, t), lambda b, h, i: (h, 0, 0, 0)),
            pl.BlockSpec((1, ATT_V_DIM), lambda b, h, i: (0, 0)),
        ],
        out_specs=pl.BlockSpec((None, t, 128), lambda b, h, i: (b, i, h)),
        scratch_shapes=[pltpu.VMEM((2, 1, t), F32), pltpu.VMEM((2, rows, t), F32)],
        compiler_params=_params(("parallel", "parallel", "arbitrary")),
        name="diff_attn",
    )(lam, proj3, proj3, _v_transposed(proj3), btab, sub_gain.reshape(1, ATT_V_DIM))


def _split3(x):
    h1 = x.astype(BF16)
    r1 = x - h1.astype(F32)
    h2 = r1.astype(BF16)
    h3 = (r1 - h2.astype(F32)).astype(BF16)
    return h1, h2, h3


def _hgrn_kernel(q_ref, f_ref, i_ref, g_ref, la_ref, l1m_ref, og_ref, o_ref):
    c_len, sub = HGRN_CHUNK, HGRN_SUB
    half = sub // 2
    n_chunk = q_ref.shape[0] // c_len
    la, l1m, og = la_ref[...], l1m_ref[...], og_ref[...]
    rr = lax.broadcasted_iota(jnp.int32, (c_len, c_len), 0)
    cc = lax.broadcasted_iota(jnp.int32, (c_len, c_len), 1)
    tri = jnp.where(cc <= rr, 1.0, 0.0).astype(BF16)
    row8 = lax.broadcasted_iota(jnp.int32, (half, 1), 0)
    lane8 = lax.broadcasted_iota(jnp.int32, (half, c_len), 1)
    nt = (((1,), (1,)), ((), ()))

    def chunk(c, st):
        r0 = pl.multiple_of(c * c_len, c_len)
        rows = pl.ds(r0, c_len)
        z = f_ref[rows, :].astype(F32)
        qh = q_ref[rows, :].astype(F32)
        qh = qh * jax.nn.sigmoid(qh)
        v = i_ref[rows, :]
        gate = g_ref[rows, :].astype(F32)
        sp = jnp.log(1.0 + jnp.exp(-jnp.abs(z)))
        bb = l1m + jnp.minimum(z, 0.0) - sp
        log_f = jnp.maximum(la, bb) + jnp.log(1.0 + jnp.exp(-jnp.abs(la - bb)))
        valid = (r0 + lax.broadcasted_iota(jnp.int32, (c_len, 1), 0)) >= PAD0
        log_k = jnp.where(valid, l1m + jnp.minimum(-z, 0.0) - sp, -jnp.inf)
        g = sum(jnp.dot(tri, part, preferred_element_type=F32) for part in _split3(log_f))
        ck = log_k - g
        o_inter = lax.dot_general((qh * jnp.exp(g)).astype(BF16), st.astype(BF16), nt,
                                  preferred_element_type=F32)
        a_rows = []
        for a in range(c_len // sub):
            lo = a * sub
            ga = (g[lo:lo + half, :], g[lo + half:lo + sub, :])
            qa = (qh[lo:lo + half, :], qh[lo + half:lo + sub, :])
            if a == 0:
                blk = [jnp.zeros((half, c_len), F32)] * 2
            else:
                gs = g[lo - 1:lo, :]
                qd = (qh[lo:lo + sub, :] * jnp.exp(g[lo:lo + sub, :] - gs)).astype(BF16)
                kd = jnp.exp(jnp.minimum(gs - g[:lo, :], 0.0) + log_k[:lo, :]).astype(BF16)
                kd = jnp.concatenate([kd, jnp.zeros((c_len - lo, HGRN_D), BF16)], axis=0)
                a_off = lax.dot_general(qd, kd, nt, preferred_element_type=F32)
                blk = [a_off[:half, :], a_off[half:, :]]
            for s in range(sub):
                crow = ck[lo + s:lo + s + 1, :]
                for hh in range(s // half, 2):
                    col = jnp.sum(qa[hh] * jnp.exp(ga[hh] + crow), axis=-1, keepdims=True)
                    blk[hh] = jnp.where(lane8 == lo + s, col, blk[hh])
            for hh in range(2):
                a_rows.append(jnp.where(lane8 <= lo + hh * half + row8, blk[hh], 0.0))
        a_full = jnp.concatenate(a_rows, axis=0).astype(BF16)
        o = o_inter + jnp.dot(a_full, v, preferred_element_type=F32)
        g_last = g[c_len - 1:c_len, :]
        kd = jnp.exp(g_last - g + log_k).astype(BF16)
        st = st * jnp.exp(g_last) + lax.dot_general(v, kd, (((0,), (0,)), ((), ())),
                                                    preferred_element_type=F32)
        ms = jnp.mean(o * o, axis=-1, keepdims=True)
        y = o * lax.rsqrt(ms + EPS) * og * (gate * jax.nn.sigmoid(gate))
        o_ref[rows, :] = y.astype(BF16)
        return st

    lax.fori_loop(0, n_chunk, chunk, jnp.zeros((HGRN_D, HGRN_D), F32), unroll=3)


def _hgrn(proj3, log_lb, log1m_lb, out_gain):
    bsz, tp, _ = proj3.shape
    seq = lambda blk: pl.BlockSpec((None, tp, 128), lambda b, h: (b, 0, blk + h))
    chan = pl.BlockSpec((None, 1, HGRN_D), lambda b, h: (h, 0, 0))
    return pl.pallas_call(
        _hgrn_kernel,
        out_shape=jax.ShapeDtypeStruct((bsz, tp, HGRN_HEADS * HGRN_D), BF16),
        grid=(bsz, HGRN_HEADS),
        in_specs=[seq(BLK_RQ), seq(BLK_RF), seq(BLK_RI), seq(BLK_RG), chan, chan,
                  pl.BlockSpec((1, HGRN_D), lambda b, h: (0, 0))],
        out_specs=pl.BlockSpec((None, tp, 128), lambda b, h: (b, 0, h)),
        compiler_params=_params(("parallel", "parallel")),
        name="hgrn2",
    )(proj3, proj3, proj3, proj3,
      log_lb.reshape(HGRN_HEADS, 1, HGRN_D), log1m_lb.reshape(HGRN_HEADS, 1, HGRN_D),
      out_gain.reshape(1, HGRN_D))


def _merge_kernel(hs_ref, ua_ref, ur_ref, cb_ref, cc_ref, ch_ref, pc_ref, ph_ref,
                  g0_ref, g1_ref, g2_ref, cw_ref, wb_ref, wo_ref, o_ref, *, tm, tiles_per_seq):
    i = pl.program_id(0)
    row = (i % tiles_per_seq) * tm + lax.broadcasted_iota(jnp.int32, (tm, 1), 0)
    valid = row >= PAD0
    z = jnp.where(valid, cc_ref[...].astype(F32) * ch_ref[...].astype(F32), 0.0)
    halo_row = (i % tiles_per_seq) * tm - 8 + lax.broadcasted_iota(jnp.int32, (8, 1), 0)
    zp = jnp.where(halo_row >= PAD0, pc_ref[...].astype(F32) * ph_ref[...].astype(F32), 0.0)
    zz = jnp.concatenate([zp, z], axis=0)
    cw = cw_ref[...]
    y = (cw[2:3, :] * z + cw[1:2, :] * zz[7:7 + tm, :] + cw[0:1, :] * zz[6:6 + tm, :])
    u_conv = jnp.where(valid, cb_ref[...].astype(F32) * y, 0.0).astype(BF16)
    mixed = jnp.zeros((tm, D_MODEL), F32)
    for n, (u, g_ref) in enumerate(((ua_ref[...], g0_ref), (u_conv, g1_ref), (ur_ref[...], g2_ref))):
        up = jnp.dot(u, wb_ref[n], preferred_element_type=F32)
        mixed = mixed + jax.nn.sigmoid(g_ref[...].astype(F32)) * up
    o_ref[...] = hs_ref[...] + jnp.dot(mixed.astype(BF16), wo_ref[...], preferred_element_type=F32)


def _merge(hs, u_att, u_hgrn, proj, conv_w, wb_bf, wo_bf, tp):
    rows = hs.shape[0]
    tm = _row_tile(tp, 384)
    kern = functools.partial(_merge_kernel, tm=tm, tiles_per_seq=tp // tm)
    row_blk = lambda w, blk: pl.BlockSpec((tm, w), lambda i: (i, blk))
    halo = lambda blk: pl.BlockSpec((8, 512), lambda i: (jnp.maximum(i * (tm // 8) - 1, 0), blk))
    const = lambda shape: pl.BlockSpec(shape, lambda i: (0,) * len(shape))
    return pl.pallas_call(
        kern,
        out_shape=jax.ShapeDtypeStruct((rows, D_MODEL), F32),
        grid=(rows // tm,),
        in_specs=[row_blk(D_MODEL, 0), row_blk(512, 0), row_blk(512, 0),
                  row_blk(512, BLK_CB), row_blk(512, BLK_CC), row_blk(512, BLK_CH),
                  halo(BLK_CC), halo(BLK_CH),
                  row_blk(1024, BLK_GATE), row_blk(1024, BLK_GATE + 1), row_blk(1024, BLK_GATE + 2),
                  const((CONV_K, 512)), const((3, 512, D_MODEL)), const((D_MODEL, D_MODEL))],
        out_specs=row_blk(D_MODEL, 0),
        compiler_params=_params(("parallel",)),
        name="merge",
    )(hs, u_att, u_hgrn, proj, proj, proj, proj, proj, proj, proj, proj, conv_w, wb_bf, wo_bf)


def _ffn_kernel(hs_ref, g_ref, wg_ref, wu_ref, wd_ref, o_ref):
    x = hs_ref[...]
    ms = jnp.mean(x * x, axis=-1, keepdims=True)
    hn = (x * lax.rsqrt(ms + EPS) * g_ref[...]).astype(BF16)
    a = jnp.dot(hn, wg_ref[...], preferred_element_type=F32)
    u = jnp.dot(hn, wu_ref[...], preferred_element_type=F32)
    h = (a * jax.nn.sigmoid(a) * u).astype(BF16)
    o_ref[...] = x + jnp.dot(h, wd_ref[...], preferred_element_type=F32)


def _ffn(hs, gain, wg_bf, wu_bf, wd_bf):
    rows = hs.shape[0]
    d_ff = wg_bf.shape[1]
    tm = _row_tile(rows, 384)
    const = lambda shape: pl.BlockSpec(shape, lambda i: (0, 0))
    return pl.pallas_call(
        _ffn_kernel,
        out_shape=jax.ShapeDtypeStruct((rows, D_MODEL), F32),
        grid=(rows // tm,),
        in_specs=[pl.BlockSpec((tm, D_MODEL), lambda i: (i, 0)), const((1, D_MODEL)),
                  const((D_MODEL, d_ff)), const((D_MODEL, d_ff)), const((d_ff, D_MODEL))],
        out_specs=pl.BlockSpec((tm, D_MODEL), lambda i: (i, 0)),
        compiler_params=_params(("parallel",)),
        name="ffn_dense",
    )(hs, gain.reshape(1, D_MODEL), wg_bf, wu_bf, wd_bf)


def _router_kernel(hs_ref, g_ref, rw_ref, idx_ref, wt_ref, cnt_ref, carry_ref, *, tm, tiles_per_seq):
    i = pl.program_id(0)

    @pl.when(i == 0)
    def _():
        carry_ref[...] = jnp.zeros_like(carry_ref)

    x = hs_ref[...]
    ms = jnp.mean(x * x, axis=-1, keepdims=True)
    hn = x * lax.rsqrt(ms + EPS) * g_ref[...]
    logits = jnp.dot(hn, rw_ref[...], preferred_element_type=F32, precision=lax.Precision.HIGHEST)
    lane = lax.broadcasted_iota(jnp.int32, logits.shape, 1)
    lane_f = lane.astype(F32)
    logits = jnp.where(lane < N_EXPERTS, logits, -jnp.inf)
    m1 = jnp.max(logits, axis=-1, keepdims=True)
    i1 = jnp.min(jnp.where(logits == m1, lane_f, 128.0), axis=-1, keepdims=True)
    rest = jnp.where(lane_f == i1, -jnp.inf, logits)
    m2 = jnp.max(rest, axis=-1, keepdims=True)
    i2 = jnp.min(jnp.where(rest == m2, lane_f, 128.0), axis=-1, keepdims=True)
    e2 = jnp.exp(m2 - m1)
    w1 = 1.0 / (1.0 + e2)
    w2 = e2 / (1.0 + e2)
    row = (i % tiles_per_seq) * tm + lax.broadcasted_iota(jnp.int32, (tm, 1), 0)
    real = row >= T_PAD
    hot1 = jnp.where(jnp.logical_and(real, lane_f == i1), 1.0, 0.0)
    hot2 = jnp.where(jnp.logical_and(real, lane_f == i2), 1.0, 0.0)
    both = hot1 + hot2
    rr = lax.broadcasted_iota(jnp.int32, (tm, tm), 0)
    cc = lax.broadcasted_iota(jnp.int32, (tm, tm), 1)
    earlier = jnp.where(cc < rr, 1.0, 0.0).astype(BF16)
    before = carry_ref[...] + jnp.dot(earlier, both.astype(BF16), preferred_element_type=F32)
    r1 = jnp.sum(before * hot1, axis=-1, keepdims=True)
    r2 = jnp.sum(before * hot2, axis=-1, keepdims=True)
    carry_ref[...] += jnp.sum(both, axis=0, keepdims=True)
    cnt_ref[...] = carry_ref[...]
    packed = jnp.where(lane == 0, i1, jnp.where(lane == 1, i2, jnp.where(lane == 2, r1, jnp.where(lane == 3, r2, 0.0))))
    idx_ref[...] = packed.astype(jnp.int32)
    wt_ref[...] = jnp.where(lane == 0, w1, jnp.where(lane == 1, w2, 0.0))


def _router(hs, gain, router_w, tp):
    rows = hs.shape[0]
    tm = _row_tile(tp, 384)
    rw = jnp.zeros((D_MODEL, 128), F32).at[:, :N_EXPERTS].set(router_w.astype(F32))
    blk = pl.BlockSpec((tm, 128), lambda i: (i, 0))
    return pl.pallas_call(
        functools.partial(_router_kernel, tm=tm, tiles_per_seq=tp // tm),
        out_shape=(jax.ShapeDtypeStruct((rows, 128), jnp.int32), jax.ShapeDtypeStruct((rows, 128), F32),
                   jax.ShapeDtypeStruct((1, 128), F32)),
        grid=(rows // tm,),
        in_specs=[pl.BlockSpec((tm, D_MODEL), lambda i: (i, 0)),
                  pl.BlockSpec((1, D_MODEL), lambda i: (0, 0)),
                  pl.BlockSpec((D_MODEL, 128), lambda i: (0, 0))],
        out_specs=(blk, blk, pl.BlockSpec((1, 128), lambda i: (0, 0))),
        scratch_shapes=[pltpu.VMEM((1, 128), F32)],
        compiler_params=_params(("arbitrary",)),
        name="moe_router",
    )(hs, gain.reshape(1, D_MODEL), rw)


def _dispatch_kernel(s1_ref, s2_ref, zr_ref, hs_ref, g_ref, xs_ref, rows_ref, zero_ref, sem, zsem,
                     *, td, tg, tiles_per_seq, n_steps):
    n = pl.program_id(0) * tiles_per_seq + pl.program_id(1)
    buf = n % 2
    base = n * td

    def row_copy(r, which):
        slot = (s1_ref, s2_ref)[which][base + r]
        return pltpu.make_async_copy(rows_ref.at[buf, pl.ds(r, 1)], xs_ref.at[pl.ds(slot, 1)], sem.at[buf])

    def wait_rows(b):
        def body(r, c):
            pltpu.make_async_copy(rows_ref.at[b, pl.ds(0, 1)], xs_ref.at[pl.ds(0, 1)], sem.at[b]).wait()
            return c
        lax.fori_loop(0, 2 * td, body, 0)

    @pl.when(n == 0)
    def _():
        zero_ref[...] = jnp.zeros_like(zero_ref)
        for e in range(2 * N_EXPERTS):
            fill = pltpu.make_async_copy(zero_ref, xs_ref.at[pl.ds(pl.multiple_of(zr_ref[e], tg), tg)], zsem)
            fill.start()
            fill.wait()

    @pl.when(n >= 2)
    def _():
        wait_rows(buf)

    x = hs_ref[...]
    ms = jnp.mean(x * x, axis=-1, keepdims=True)
    rows_ref[buf] = x * lax.rsqrt(ms + EPS) * g_ref[...]

    def issue(r, c):
        row_copy(r, 0).start()
        row_copy(r, 1).start()
        return c

    lax.fori_loop(0, td, issue, 0)

    @pl.when(n == n_steps - 1)
    def _():
        wait_rows(buf)
        if n_steps > 1:
            wait_rows(1 - buf)


def _dispatch(hs3, gain, slot1, slot2, zero_rows, n_slots, tg):
    bsz, tp, _ = hs3.shape
    td = T_PAD
    tiles_per_seq = (tp - T_PAD) // td
    kern = functools.partial(_dispatch_kernel, td=td, tg=tg, tiles_per_seq=tiles_per_seq,
                             n_steps=bsz * tiles_per_seq)
    return pl.pallas_call(
        kern,
        out_shape=jax.ShapeDtypeStruct((n_slots, D_MODEL), F32),
        grid_spec=pltpu.PrefetchScalarGridSpec(
            num_scalar_prefetch=3,
            grid=(bsz, tiles_per_seq),
            in_specs=[pl.BlockSpec((None, td, D_MODEL), lambda b, i, *_: (b, i + 1, 0)),
                      pl.BlockSpec((1, D_MODEL), lambda b, i, *_: (0, 0))],
            out_specs=pl.BlockSpec(memory_space=pl.ANY),
            scratch_shapes=[pltpu.VMEM((2, td, D_MODEL), F32), pltpu.VMEM((tg, D_MODEL), F32),
                            pltpu.SemaphoreType.DMA((2,)), pltpu.SemaphoreType.DMA],
        ),
        compiler_params=_params(("arbitrary", "arbitrary")),
        name="moe_dispatch",
    )(slot1, slot2, zero_rows, hs3, gain.reshape(1, D_MODEL))


def _expert_kernel(te_ref, nt_ref, x_ref, wg_ref, wu_ref, wd_ref, o_ref, hn_ref, acc_ref, *, n_f):
    t = pl.program_id(0)
    f = pl.program_id(1)

    @pl.when(t < nt_ref[0])
    def _():
        @pl.when(f == 0)
        def _():
            hn_ref[...] = x_ref[...].astype(BF16)

        hn = hn_ref[...]
        a = jnp.dot(hn, wg_ref[...], preferred_element_type=F32)
        u = jnp.dot(hn, wu_ref[...], preferred_element_type=F32)
        h = (a * jax.nn.sigmoid(a) * u).astype(BF16)
        y = jnp.dot(h, wd_ref[...], preferred_element_type=F32)

        @pl.when(f == 0)
        def _():
            acc_ref[...] = y

        @pl.when(f > 0)
        def _():
            acc_ref[...] += y

    @pl.when(jnp.logical_and(t >= nt_ref[0], f == 0))
    def _():
        acc_ref[...] = jnp.zeros_like(acc_ref)

    @pl.when(f == n_f - 1)
    def _():
        o_ref[...] = acc_ref[...]


def _experts(xs, tile_expert, n_tiles_used, wg_bf, wu_bf, wd_bf, tg):
    slots = xs.shape[0]
    d_ff = wg_bf.shape[2]
    n_f = 2
    tf = d_ff // n_f

    def x_map(t, f, te, nt):
        return (jnp.minimum(t, nt[0] - 1), 0)

    def f_of(t, f, nt):
        return jnp.where(t < nt[0], f, n_f - 1)

    return pl.pallas_call(
        functools.partial(_expert_kernel, n_f=n_f),
        out_shape=jax.ShapeDtypeStruct((slots, D_MODEL), F32),
        grid_spec=pltpu.PrefetchScalarGridSpec(
            num_scalar_prefetch=2,
            grid=(slots // tg, n_f),
            in_specs=[
                pl.BlockSpec((tg, D_MODEL), x_map),
                pl.BlockSpec((None, D_MODEL, tf), lambda t, f, te, nt: (te[t], 0, f_of(t, f, nt))),
                pl.BlockSpec((None, D_MODEL, tf), lambda t, f, te, nt: (te[t], 0, f_of(t, f, nt))),
                pl.BlockSpec((None, tf, D_MODEL), lambda t, f, te, nt: (te[t], f_of(t, f, nt), 0)),
            ],
            out_specs=pl.BlockSpec((tg, D_MODEL), lambda t, f, te, nt: (t, 0)),
            scratch_shapes=[pltpu.VMEM((tg, D_MODEL), BF16), pltpu.VMEM((tg, D_MODEL), F32)],
        ),
        compiler_params=_params(("arbitrary", "arbitrary")),
        name="moe_experts",
    )(tile_expert, n_tiles_used, xs, wg_bf, wu_bf, wd_bf)


def _combine_kernel(s1_ref, s2_ref, hs_ref, wt_ref, y_ref, o_ref, buf_ref, sem, *, tc, tiles_per_seq, n_steps):
    n = pl.program_id(0) * tiles_per_seq + pl.program_id(1)
    cur = n % 2

    def row_copy(step, b, r, which):
        slot = (s1_ref, s2_ref)[which][step * tc + r]
        return pltpu.make_async_copy(y_ref.at[pl.ds(slot, 1)], buf_ref.at[b, which, pl.ds(r, 1)], sem.at[b])

    def fetch(step, b):
        def body(r, c):
            row_copy(step, b, r, 0).start()
            row_copy(step, b, r, 1).start()
            return c
        lax.fori_loop(0, tc, body, 0)

    @pl.when(n == 0)
    def _():
        fetch(0, 0)

    @pl.when(n + 1 < n_steps)
    def _():
        fetch(n + 1, 1 - cur)

    def wait(r, c):
        row_copy(n, cur, r, 0).wait()
        row_copy(n, cur, r, 1).wait()
        return c

    lax.fori_loop(0, tc, wait, 0)
    wt = wt_ref[...]
    w1, w2 = wt[:, 0:1], wt[:, 1:2]
    o_ref[...] = hs_ref[...] + w1 * buf_ref[cur, 0] + w2 * buf_ref[cur, 1]


def _combine(hs3, wts3, y, slot1, slot2, seq):
    bsz = hs3.shape[0]
    tc = T_PAD
    tiles_per_seq = seq // tc
    kern = functools.partial(_combine_kernel, tc=tc, tiles_per_seq=tiles_per_seq, n_steps=bsz * tiles_per_seq)
    return pl.pallas_call(
        kern,
        out_shape=jax.ShapeDtypeStruct((bsz, seq, D_MODEL), F32),
        grid_spec=pltpu.PrefetchScalarGridSpec(
            num_scalar_prefetch=2,
            grid=(bsz, tiles_per_seq),
            in_specs=[
                pl.BlockSpec((None, tc, D_MODEL), lambda b, i, s1, s2: (b, i + 1, 0)),
                pl.BlockSpec((None, tc, 128), lambda b, i, s1, s2: (b, i + 1, 0)),
                pl.BlockSpec(memory_space=pl.ANY),
            ],
            out_specs=pl.BlockSpec((None, tc, D_MODEL), lambda b, i, s1, s2: (b, i, 0)),
            scratch_shapes=[pltpu.VMEM((2, 2, tc, D_MODEL), F32), pltpu.SemaphoreType.DMA((2,))],
        ),
        compiler_params=_params(("arbitrary", "arbitrary")),
        name="moe_combine",
    )(slot1, slot2, hs3, wts3, y)


def _moe(hs, gain, router_w, wg_bf, wu_bf, wd_bf, bsz, tp):
    seq = tp - T_PAD
    n_tok = bsz * seq
    tg = MOE_TILE
    idx, wts, cnt = _router(hs, gain, router_w, tp)
    sel = idx.reshape(bsz, tp, 128)[:, T_PAD:, :4].reshape(n_tok, 4)
    counts = cnt[0, :N_EXPERTS].astype(jnp.int32)
    padded = ((counts + tg - 1) // tg) * tg
    ends = jnp.cumsum(padded)
    starts = ends - padded
    experts = jnp.arange(N_EXPERTS, dtype=jnp.int32)[None, :]
    slot1 = jnp.sum(jnp.where(sel[:, 0:1] == experts, starts[None, :], 0), axis=1) + sel[:, 2]
    slot2 = jnp.sum(jnp.where(sel[:, 1:2] == experts, starts[None, :], 0), axis=1) + sel[:, 3]
    slot1, slot2 = slot1.astype(jnp.int32), slot2.astype(jnp.int32)
    n_slots = 2 * n_tok + N_EXPERTS * tg
    tile_start = jnp.arange(n_slots // tg, dtype=jnp.int32)[:, None] * tg
    tile_expert = jnp.minimum(jnp.sum((tile_start >= ends[None, :]).astype(jnp.int32), axis=1), N_EXPERTS - 1)
    n_tiles_used = (ends[-1:] // tg).astype(jnp.int32)
    tail = n_slots - tg * (1 + jnp.arange(N_EXPERTS, dtype=jnp.int32))
    zero_rows = jnp.concatenate([jnp.maximum(ends - tg, 0), tail]).astype(jnp.int32)

    hs3 = hs.reshape(bsz, tp, D_MODEL)
    xs = _dispatch(hs3, gain, slot1, slot2, zero_rows, n_slots, tg)
    y = _experts(xs, tile_expert.astype(jnp.int32), n_tiles_used, wg_bf, wu_bf, wd_bf, tg)
    return _combine(hs3, wts.reshape(bsz, tp, 128), y, slot1, slot2, seq)


def _permute_qk_cols(w):
    return w.reshape(D_MODEL, 2, ATT_HEADS, ATT_QK_DIM).transpose(0, 2, 1, 3).reshape(D_MODEL, 512)


def kernel(x, meta_tokens, norm1_gain, norm2_gain, w_in, q_norm_gain, k_norm_gain, diff_lambda,
           attn_sub_gain, rel_bias, conv_w, hgrn_lb_logits, hgrn_out_gain, w_branch, w_out,
           ffn_w_gate, ffn_w_up, ffn_w_down, router_w, moe_w_gate, moe_w_up, moe_w_down):
    bsz, seq, _ = x.shape
    depth = w_in.shape[0]
    tp = T_PAD + seq
    assert tp % ATT_TILE == 0 and depth == 2

    head = jnp.concatenate([jnp.zeros((PAD0, D_MODEL), x.dtype), meta_tokens.astype(x.dtype)], axis=0)
    hs = jnp.concatenate([jnp.broadcast_to(head[None], (bsz, T_PAD, D_MODEL)), x], axis=1)
    hs = hs.reshape(bsz * tp, D_MODEL)

    lb_all = jnp.cumsum(jax.nn.softmax(hgrn_lb_logits.astype(F32), axis=0), axis=0)
    lb_all = lb_all - lb_all[0]
    btab = _attn_bias_tables(rel_bias, ATT_TILE)

    out = None
    for layer in range(depth):
        w = w_in[layer]
        w_bf = jnp.concatenate([_permute_qk_cols(w[:, :512]), _permute_qk_cols(w[:, 512:1024]), w[:, 1024:]],
                               axis=1).astype(BF16)
        qk_gain = jnp.stack([jnp.tile(q_norm_gain[layer].astype(F32), 8) * (ATT_QK_DIM ** -0.5 * LOG2E),
                             jnp.tile(k_norm_gain[layer].astype(F32), 8)]).reshape(2, 1, COL_TILE)
        proj = _inproj(hs, norm1_gain[layer], w_bf, qk_gain)
        proj3 = proj.reshape(bsz, tp, IN_COLS)

        lam_init = 0.8 - 0.6 * math.exp(-0.3 * layer)
        lp = diff_lambda[layer].astype(F32)
        lam = jnp.exp(jnp.sum(lp[0] * lp[1])) - jnp.exp(jnp.sum(lp[2] * lp[3])) + lam_init
        u_att = _diff_attention(proj3, lam.reshape(1), attn_sub_gain[layer].astype(F32), btab,
                                1.0 - lam_init)

        lb = lb_all[layer]
        u_hgrn = _hgrn(proj3, jnp.log(lb), jnp.log1p(-lb), hgrn_out_gain[layer].astype(F32))

        hs = _merge(hs, u_att.reshape(bsz * tp, 512), u_hgrn.reshape(bsz * tp, 512), proj,
                    conv_w[layer].astype(F32), w_branch[layer].astype(BF16), w_out[layer].astype(BF16), tp)

        j = layer // 2
        if layer % 2 == 0:
            hs = _ffn(hs, norm2_gain[layer], ffn_w_gate[j].astype(BF16), ffn_w_up[j].astype(BF16),
                      ffn_w_down[j].astype(BF16))
        else:
            out = _moe(hs, norm2_gain[layer], router_w[j], moe_w_gate[j].astype(BF16),
                       moe_w_up[j].astype(BF16), moe_w_down[j].astype(BF16), bsz, tp)
    return out
```

```python
import functools
import math

import numpy as np
import jax
import jax.numpy as jnp
from jax import lax
from jax.experimental import pallas as pl
from jax.experimental.pallas import tpu as pltpu

F32 = jnp.float32
BF16 = jnp.bfloat16

D_MODEL = 1024
N_META = 16
EPS = 1e-6
ATT_HEADS = 4
ATT_QK_DIM = 64
ATT_V_DIM = 128
REL_BUCKETS = 32
REL_MAX_DIST = 128
CONV_K = 3
HGRN_HEADS = 4
HGRN_D = 128
N_EXPERTS = 8
IN_COLS = 8192

T_PAD = 128
PAD0 = T_PAD - N_META
ATT_TILE = 384
VT_ONES = 16
ATT_GROUP = 1
LOG2E = math.log2(math.e)
HGRN_CHUNK = 128
HGRN_SUB = 16
COL_TILE = 1024
GROUP_MEAN_WIDTH = 256
MOE_TILE = 512
MASK = -1e30
VMEM_LIMIT = 56 * 1024 * 1024

BLK_Q, BLK_K, BLK_V = 0, 4, 8
BLK_RQ, BLK_RF, BLK_RI, BLK_RG = 24, 28, 32, 36
BLK_CB, BLK_CC, BLK_CH = 3, 4, 5
BLK_GATE = 5


def _row_tile(rows, target):
    n = rows // 128
    best = 1
    for d in range(1, n + 1):
        if n % d == 0 and d * 128 <= target:
            best = d
    return best * 128


def _params(sem, vmem=VMEM_LIMIT):
    return pltpu.CompilerParams(dimension_semantics=sem, vmem_limit_bytes=vmem)


def _inproj_kernel(x_ref, g_ref, w_ref, qkg_ref, gm_ref, o_ref, xn_ref):
    j = pl.program_id(1)

    @pl.when(j == 0)
    def _():
        x = x_ref[...]
        ms = jnp.mean(x * x, axis=-1, keepdims=True)
        xn_ref[...] = (x * lax.rsqrt(ms + EPS) * g_ref[...]).astype(BF16)

    acc = jnp.dot(xn_ref[...], w_ref[...], preferred_element_type=F32)

    @pl.when(j == 0)
    def _():
        sq = acc * acc
        hi = sq.astype(BF16)
        lo = (sq - hi.astype(F32)).astype(BF16)
        gm = gm_ref[...]
        width = gm.shape[0]
        ms = jnp.concatenate(
            [jnp.dot(hi[:, c:c + width], gm, preferred_element_type=F32)
             + jnp.dot(lo[:, c:c + width], gm, preferred_element_type=F32)
             for c in range(0, COL_TILE, width)], axis=1)
        o_ref[...] = (acc * lax.rsqrt(ms + EPS) * qkg_ref[...]).astype(BF16)

    @pl.when(j > 0)
    def _():
        o_ref[...] = acc.astype(BF16)


def _inproj(hs, gain, w_bf, qk_gain):
    rows = hs.shape[0]
    tm = _row_tile(rows, 1536)
    n_col = IN_COLS // COL_TILE
    assert COL_TILE == 4 * ATT_HEADS * ATT_QK_DIM
    grp = np.arange(GROUP_MEAN_WIDTH) // ATT_QK_DIM
    gm = jnp.asarray((grp[:, None] == grp[None, :]).astype(np.float32) / ATT_QK_DIM, BF16)
    return pl.pallas_call(
        _inproj_kernel,
        out_shape=jax.ShapeDtypeStruct((rows, IN_COLS), BF16),
        grid=(rows // tm, n_col),
        in_specs=[
            pl.BlockSpec((tm, D_MODEL), lambda i, j: (i, 0)),
            pl.BlockSpec((1, D_MODEL), lambda i, j: (0, 0)),
            pl.BlockSpec((D_MODEL, COL_TILE), lambda i, j: (0, j)),
            pl.BlockSpec((1, COL_TILE), lambda i, j: (0, 0)),
            pl.BlockSpec((GROUP_MEAN_WIDTH, GROUP_MEAN_WIDTH), lambda i, j: (0, 0)),
        ],
        out_specs=pl.BlockSpec((tm, COL_TILE), lambda i, j: (i, j)),
        scratch_shapes=[pltpu.VMEM((tm, D_MODEL), BF16)],
        compiler_params=_params(("parallel", "arbitrary")),
        name="inproj",
    )(hs, gain.reshape(1, D_MODEL), w_bf, qk_gain, gm)


def _rel_bucket_table(n_max):
    n = np.arange(n_max, dtype=np.int64)
    max_exact = REL_BUCKETS // 2
    nf = np.maximum(n, 1).astype(np.float32)
    large = max_exact + (np.log(nf / np.float32(max_exact)) / np.float32(math.log(REL_MAX_DIST / max_exact))
                         * np.float32(REL_BUCKETS - max_exact)).astype(np.int32)
    large = np.minimum(large, REL_BUCKETS - 1)
    return np.where(n < max_exact, n, large).astype(np.int32)


def _attn_bias_tables(rel_bias, t):
    bucket = _rel_bucket_table(2 * t)
    assert np.all(bucket[t + 1:] == REL_BUCKETS - 1) and np.all(np.diff(bucket) >= 0)
    first_dist = tuple(int(np.searchsorted(bucket, b, side="left")) for b in range(REL_BUCKETS))
    return pl.pallas_call(
        functools.partial(_bias_kernel, t=t, first_dist=first_dist),
        out_shape=jax.ShapeDtypeStruct((ATT_HEADS, 6, t, t), F32),
        grid=(ATT_HEADS,),
        in_specs=[pl.BlockSpec(memory_space=pltpu.SMEM)],
        out_specs=pl.BlockSpec((None, 6, t, t), lambda h: (h, 0, 0, 0)),
        compiler_params=_params(("parallel",)),
        name="attn_bias",
    )(rel_bias.astype(F32))


def _bias_kernel(rb_ref, o_ref, *, t, first_dist):
    h = pl.program_id(0)
    key = lax.broadcasted_iota(jnp.int32, (t, t), 0)
    qry = lax.broadcasted_iota(jnp.int32, (t, t), 1)
    far = rb_ref[REL_BUCKETS - 1, h]

    def table(n):
        val = jnp.full((t, t), rb_ref[0, h] - far, F32)
        for b in range(1, REL_BUCKETS):
            val = jnp.where(n >= first_dist[b], rb_ref[b, h] - far, val)
        return val * LOG2E

    n0 = qry - key
    diag = jnp.where(n0 >= 0, table(n0), MASK)
    near = table(n0 + t)
    zero = jnp.zeros((t, t), F32)
    for kind, tab in enumerate((diag, near, zero)):
        o_ref[kind] = tab
        o_ref[kind + 3] = jnp.where(key < PAD0, MASK, tab)


def _attn_kernel(lam_ref, q_ref, k_ref, v_ref, bt_ref, sg_ref, o_ref, m_ref, acc_ref, vt_ref,
                 *, t, out_scale):
    i = pl.program_id(2)
    group = ATT_GROUP
    head_cols = [slice(g * 128, (g + 1) * 128) for g in range(group)]

    @pl.when(i == 0)
    def _():
        for g in range(group):
            for j in range(vt_ref.shape[1]):
                vt_ref[g, j, :ATT_V_DIM, :] = v_ref[j * t:(j + 1) * t, head_cols[g]].astype(F32).T.astype(BF16)
                vt_ref[g, j, ATT_V_DIM:, :] = jnp.ones((VT_ONES, t), BF16)

    lane = lax.broadcasted_iota(jnp.int32, (1, 2 * ATT_QK_DIM), 1)
    q_maps = []
    for g in range(group):
        q = q_ref[:, head_cols[g]]
        zero = jnp.zeros_like(q)
        q_maps.append((jnp.where(lane < ATT_QK_DIM, q, zero), jnp.where(lane >= ATT_QK_DIM, q, zero)))
    m_ref[...] = jnp.full(m_ref.shape, MASK, F32)
    acc_ref[...] = jnp.zeros(acc_ref.shape, F32)
    nt = (((1,), (1,)), ((), ()))

    def scores(j):
        kt = k_ref[pl.ds(pl.multiple_of(j * t, t), t), :]
        kind = jnp.minimum(i - j, 2) + jnp.where(j == 0, 3, 0)
        out = []
        for g in range(group):
            bias = bt_ref[g, kind]
            out.append(tuple(lax.dot_general(kt[:, head_cols[g]], qm, nt, preferred_element_type=F32) + bias
                             for qm in q_maps[g]))
        return tuple(out)

    def consume(j, s):
        for g in range(group):
            vt = vt_ref[g, j]
            for mi in range(2):
                m_old = m_ref[g, mi]
                m_new = jnp.maximum(m_old, jnp.max(s[g][mi], axis=0, keepdims=True))
                m_ref[g, mi] = m_new
                p = jnp.exp2(s[g][mi] - m_new).astype(BF16)
                acc_ref[g, mi] = (jnp.exp2(m_old - m_new) * acc_ref[g, mi]
                                  + jnp.dot(vt, p, preferred_element_type=F32))

    def step(j, s):
        nxt = scores(j + 1)
        consume(j, s)
        return nxt

    consume(i, lax.fori_loop(0, i, step, scores(0)))

    row = i * t + lax.broadcasted_iota(jnp.int32, (t, 1), 0)
    for g in range(group):
        a1, a2 = acc_ref[g, 0], acc_ref[g, 1]
        o_t = (a1[:ATT_V_DIM] / a1[ATT_V_DIM:ATT_V_DIM + 1]
               - lam_ref[0] * (a2[:ATT_V_DIM] / a2[ATT_V_DIM:ATT_V_DIM + 1]))
        o = o_t.T
        ms = jnp.mean(o * o, axis=-1, keepdims=True)
        y = o * lax.rsqrt(ms + EPS) * (sg_ref[...] * out_scale)
        o_ref[:, head_cols[g]] = jnp.where(row >= PAD0, y, 0.0).astype(BF16)


def _diff_attention(proj3, lam, sub_gain, btab, out_scale):
    bsz, tp, _ = proj3.shape
    t = ATT_TILE
    n_t = tp // t
    rows = ATT_V_DIM + VT_ONES
    group = ATT_GROUP
    width = group * 128
    kern = functools.partial(_attn_kernel, t=t, out_scale=out_scale)
    return pl.pallas_call(
        kern,
        out_shape=jax.ShapeDtypeStruct((bsz, tp, ATT_HEADS * ATT_V_DIM), BF16),
        grid=(bsz, ATT_HEADS // group, n_t),
        in_specs=[
            pl.BlockSpec(memory_space=pltpu.SMEM),
            pl.BlockSpec((None, t, width), lambda b, h, i: (b, i, BLK_Q // group + h)),
            pl.BlockSpec((None, tp, width), lambda b, h, i: (b, 0, BLK_K // group + h)),
            pl.BlockSpec((None, tp, width), lambda b, h, i: (b, 0, BLK_V // group + h)),
            pl.BlockSpec((group, 6, t, t), lambda b, h, i: (h, 0, 0, 0)),
            pl.BlockSpec((1, ATT_V_DIM), lambda b, h, i: (0, 0)),
        ],
        out_specs=pl.BlockSpec((None, t, width), lambda b, h, i: (b, i, h)),
        scratch_shapes=[pltpu.VMEM((group, 2, 1, t), F32), pltpu.VMEM((group, 2, rows, t), F32),
                        pltpu.VMEM((group, n_t, rows, t), BF16)],
        compiler_params=_params(("parallel", "parallel", "arbitrary")),
        name="diff_attn",
    )(lam, proj3, proj3, proj3, btab, sub_gain.reshape(1, ATT_V_DIM))


def _split3(x):
    h1 = x.astype(BF16)
    r1 = x - h1.astype(F32)
    h2 = r1.astype(BF16)
    h3 = (r1 - h2.astype(F32)).astype(BF16)
    return h1, h2, h3


def _hgrn_kernel(q_ref, f_ref, i_ref, g_ref, la_ref, l1m_ref, og_ref, o_ref):
    c_len, sub = HGRN_CHUNK, HGRN_SUB
    half = sub // 2
    n_chunk = q_ref.shape[0] // c_len
    la, l1m, og = la_ref[...], l1m_ref[...], og_ref[...]
    rr = lax.broadcasted_iota(jnp.int32, (c_len, c_len), 0)
    cc = lax.broadcasted_iota(jnp.int32, (c_len, c_len), 1)
    tri = jnp.where(cc <= rr, 1.0, 0.0).astype(BF16)
    row8 = lax.broadcasted_iota(jnp.int32, (half, 1), 0)
    lane8 = lax.broadcasted_iota(jnp.int32, (half, c_len), 1)
    nt = (((1,), (1,)), ((), ()))

    def chunk(c, st):
        r0 = pl.multiple_of(c * c_len, c_len)
        rows = pl.ds(r0, c_len)
        z = f_ref[rows, :].astype(F32)
        qh = q_ref[rows, :].astype(F32)
        qh = qh * jax.nn.sigmoid(qh)
        v = i_ref[rows, :]
        gate = g_ref[rows, :].astype(F32)
        sp = jnp.log(1.0 + jnp.exp(-jnp.abs(z)))
        bb = l1m + jnp.minimum(z, 0.0) - sp
        log_f = jnp.maximum(la, bb) + jnp.log(1.0 + jnp.exp(-jnp.abs(la - bb)))
        valid = (r0 + lax.broadcasted_iota(jnp.int32, (c_len, 1), 0)) >= PAD0
        log_k = jnp.where(valid, l1m + jnp.minimum(-z, 0.0) - sp, -jnp.inf)
        g = sum(jnp.dot(tri, part, preferred_element_type=F32) for part in _split3(log_f))
        ck = log_k - g
        o_inter = lax.dot_general((qh * jnp.exp(g)).astype(BF16), st.astype(BF16), nt,
                                  preferred_element_type=F32)
        a_rows = []
        for a in range(c_len // sub):
            lo = a * sub
            ga = (g[lo:lo + half, :], g[lo + half:lo + sub, :])
            qa = (qh[lo:lo + half, :], qh[lo + half:lo + sub, :])
            if a == 0:
                blk = [jnp.zeros((half, c_len), F32)] * 2
            else:
                gs = g[lo - 1:lo, :]
                qd = (qh[lo:lo + sub, :] * jnp.exp(g[lo:lo + sub, :] - gs)).astype(BF16)
                kd = jnp.exp(jnp.minimum(gs - g[:lo, :], 0.0) + log_k[:lo, :]).astype(BF16)
                kd = jnp.concatenate([kd, jnp.zeros((c_len - lo, HGRN_D), BF16)], axis=0)
                a_off = lax.dot_general(qd, kd, nt, preferred_element_type=F32)
                blk = [a_off[:half, :], a_off[half:, :]]
            for s in range(sub):
                crow = ck[lo + s:lo + s + 1, :]
                for hh in range(s // half, 2):
                    col = jnp.sum(qa[hh] * jnp.exp(ga[hh] + crow), axis=-1, keepdims=True)
                    blk[hh] = jnp.where(lane8 == lo + s, col, blk[hh])
            for hh in range(2):
                a_rows.append(jnp.where(lane8 <= lo + hh * half + row8, blk[hh], 0.0))
        a_full = jnp.concatenate(a_rows, axis=0).astype(BF16)
        o = o_inter + jnp.dot(a_full, v, preferred_element_type=F32)
        g_last = g[c_len - 1:c_len, :]
        kd = jnp.exp(g_last - g + log_k).astype(BF16)
        st = st * jnp.exp(g_last) + lax.dot_general(v, kd, (((0,), (0,)), ((), ())),
                                                    preferred_element_type=F32)
        ms = jnp.mean(o * o, axis=-1, keepdims=True)
        y = o * lax.rsqrt(ms + EPS) * og * (gate * jax.nn.sigmoid(gate))
        o_ref[rows, :] = y.astype(BF16)
        return st

    lax.fori_loop(0, n_chunk, chunk, jnp.zeros((HGRN_D, HGRN_D), F32), unroll=3)


def _hgrn(proj3, log_lb, log1m_lb, out_gain):
    bsz, tp, _ = proj3.shape
    seq = lambda blk: pl.BlockSpec((None, tp, 128), lambda b, h: (b, 0, blk + h))
    chan = pl.BlockSpec((None, 1, HGRN_D), lambda b, h: (h, 0, 0))
    return pl.pallas_call(
        _hgrn_kernel,
        out_shape=jax.ShapeDtypeStruct((bsz, tp, HGRN_HEADS * HGRN_D), BF16),
        grid=(bsz, HGRN_HEADS),
        in_specs=[seq(BLK_RQ), seq(BLK_RF), seq(BLK_RI), seq(BLK_RG), chan, chan,
                  pl.BlockSpec((1, HGRN_D), lambda b, h: (0, 0))],
        out_specs=pl.BlockSpec((None, tp, 128), lambda b, h: (b, 0, h)),
        compiler_params=_params(("parallel", "parallel")),
        name="hgrn2",
    )(proj3, proj3, proj3, proj3,
      log_lb.reshape(HGRN_HEADS, 1, HGRN_D), log1m_lb.reshape(HGRN_HEADS, 1, HGRN_D),
      out_gain.reshape(1, HGRN_D))


def _merge_kernel(hs_ref, ua_ref, ur_ref, cb_ref, cc_ref, ch_ref, pc_ref, ph_ref,
                  g0_ref, g1_ref, g2_ref, cw_ref, wb_ref, wo_ref, o_ref, *, tm, tiles_per_seq):
    i = pl.program_id(0)
    row = (i % tiles_per_seq) * tm + lax.broadcasted_iota(jnp.int32, (tm, 1), 0)
    valid = row >= PAD0
    z = jnp.where(valid, cc_ref[...].astype(F32) * ch_ref[...].astype(F32), 0.0)
    halo_row = (i % tiles_per_seq) * tm - 8 + lax.broadcasted_iota(jnp.int32, (8, 1), 0)
    zp = jnp.where(halo_row >= PAD0, pc_ref[...].astype(F32) * ph_ref[...].astype(F32), 0.0)
    zz = jnp.concatenate([zp, z], axis=0)
    cw = cw_ref[...]
    y = (cw[2:3, :] * z + cw[1:2, :] * zz[7:7 + tm, :] + cw[0:1, :] * zz[6:6 + tm, :])
    u_conv = jnp.where(valid, cb_ref[...].astype(F32) * y, 0.0).astype(BF16)
    mixed = jnp.zeros((tm, D_MODEL), F32)
    for n, (u, g_ref) in enumerate(((ua_ref[...], g0_ref), (u_conv, g1_ref), (ur_ref[...], g2_ref))):
        up = jnp.dot(u, wb_ref[n], preferred_element_type=F32)
        mixed = mixed + jax.nn.sigmoid(g_ref[...].astype(F32)) * up
    o_ref[...] = hs_ref[...] + jnp.dot(mixed.astype(BF16), wo_ref[...], preferred_element_type=F32)


def _merge(hs, u_att, u_hgrn, proj, conv_w, wb_bf, wo_bf, tp):
    rows = hs.shape[0]
    tm = _row_tile(tp, 384)
    kern = functools.partial(_merge_kernel, tm=tm, tiles_per_seq=tp // tm)
    row_blk = lambda w, blk: pl.BlockSpec((tm, w), lambda i: (i, blk))
    halo = lambda blk: pl.BlockSpec((8, 512), lambda i: (jnp.maximum(i * (tm // 8) - 1, 0), blk))
    const = lambda shape: pl.BlockSpec(shape, lambda i: (0,) * len(shape))
    return pl.pallas_call(
        kern,
        out_shape=jax.ShapeDtypeStruct((rows, D_MODEL), F32),
        grid=(rows // tm,),
        in_specs=[row_blk(D_MODEL, 0), row_blk(512, 0), row_blk(512, 0),
                  row_blk(512, BLK_CB), row_blk(512, BLK_CC), row_blk(512, BLK_CH),
                  halo(BLK_CC), halo(BLK_CH),
                  row_blk(1024, BLK_GATE), row_blk(1024, BLK_GATE + 1), row_blk(1024, BLK_GATE + 2),
                  const((CONV_K, 512)), const((3, 512, D_MODEL)), const((D_MODEL, D_MODEL))],
        out_specs=row_blk(D_MODEL, 0),
        compiler_params=_params(("parallel",)),
        name="merge",
    )(hs, u_att, u_hgrn, proj, proj, proj, proj, proj, proj, proj, proj, conv_w, wb_bf, wo_bf)


def _ffn_kernel(hs_ref, g_ref, wg_ref, wu_ref, wd_ref, o_ref):
    x = hs_ref[...]
    ms = jnp.mean(x * x, axis=-1, keepdims=True)
    hn = (x * lax.rsqrt(ms + EPS) * g_ref[...]).astype(BF16)
    a = jnp.dot(hn, wg_ref[...], preferred_element_type=F32)
    u = jnp.dot(hn, wu_ref[...], preferred_element_type=F32)
    h = (a * jax.nn.sigmoid(a) * u).astype(BF16)
    o_ref[...] = x + jnp.dot(h, wd_ref[...], preferred_element_type=F32)


def _ffn(hs, gain, wg_bf, wu_bf, wd_bf):
    rows = hs.shape[0]
    d_ff = wg_bf.shape[1]
    tm = _row_tile(rows, 384)
    const = lambda shape: pl.BlockSpec(shape, lambda i: (0, 0))
    return pl.pallas_call(
        _ffn_kernel,
        out_shape=jax.ShapeDtypeStruct((rows, D_MODEL), F32),
        grid=(rows // tm,),
        in_specs=[pl.BlockSpec((tm, D_MODEL), lambda i: (i, 0)), const((1, D_MODEL)),
                  const((D_MODEL, d_ff)), const((D_MODEL, d_ff)), const((d_ff, D_MODEL))],
        out_specs=pl.BlockSpec((tm, D_MODEL), lambda i: (i, 0)),
        compiler_params=_params(("parallel",)),
        name="ffn_dense",
    )(hs, gain.reshape(1, D_MODEL), wg_bf, wu_bf, wd_bf)


def _router_kernel(hs_ref, g_ref, rw_ref, idx_ref, wt_ref, cnt_ref, carry_ref, *, tm, tiles_per_seq):
    i = pl.program_id(0)

    @pl.when(i == 0)
    def _():
        carry_ref[...] = jnp.zeros_like(carry_ref)

    x = hs_ref[...]
    ms = jnp.mean(x * x, axis=-1, keepdims=True)
    hn = x * lax.rsqrt(ms + EPS) * g_ref[...]
    logits = jnp.dot(hn, rw_ref[...], preferred_element_type=F32, precision=lax.Precision.HIGHEST)
    lane = lax.broadcasted_iota(jnp.int32, logits.shape, 1)
    lane_f = lane.astype(F32)
    logits = jnp.where(lane < N_EXPERTS, logits, -jnp.inf)
    m1 = jnp.max(logits, axis=-1, keepdims=True)
    i1 = jnp.min(jnp.where(logits == m1, lane_f, 128.0), axis=-1, keepdims=True)
    rest = jnp.where(lane_f == i1, -jnp.inf, logits)
    m2 = jnp.max(rest, axis=-1, keepdims=True)
    i2 = jnp.min(jnp.where(rest == m2, lane_f, 128.0), axis=-1, keepdims=True)
    e2 = jnp.exp(m2 - m1)
    w1 = 1.0 / (1.0 + e2)
    w2 = e2 / (1.0 + e2)
    row = (i % tiles_per_seq) * tm + lax.broadcasted_iota(jnp.int32, (tm, 1), 0)
    real = row >= T_PAD
    hot1 = jnp.where(jnp.logical_and(real, lane_f == i1), 1.0, 0.0)
    hot2 = jnp.where(jnp.logical_and(real, lane_f == i2), 1.0, 0.0)
    both = hot1 + hot2
    rr = lax.broadcasted_iota(jnp.int32, (tm, tm), 0)
    cc = lax.broadcasted_iota(jnp.int32, (tm, tm), 1)
    earlier = jnp.where(cc < rr, 1.0, 0.0).astype(BF16)
    before = carry_ref[...] + jnp.dot(earlier, both.astype(BF16), preferred_element_type=F32)
    r1 = jnp.sum(before * hot1, axis=-1, keepdims=True)
    r2 = jnp.sum(before * hot2, axis=-1, keepdims=True)
    carry_ref[...] += jnp.sum(both, axis=0, keepdims=True)
    cnt_ref[...] = carry_ref[...]
    packed = jnp.where(lane == 0, i1, jnp.where(lane == 1, i2, jnp.where(lane == 2, r1, jnp.where(lane == 3, r2, 0.0))))
    idx_ref[...] = packed.astype(jnp.int32)
    wt_ref[...] = jnp.where(lane == 0, w1, jnp.where(lane == 1, w2, 0.0))


def _router(hs, gain, router_w, tp):
    rows = hs.shape[0]
    tm = _row_tile(tp, 384)
    rw = jnp.zeros((D_MODEL, 128), F32).at[:, :N_EXPERTS].set(router_w.astype(F32))
    blk = pl.BlockSpec((tm, 128), lambda i: (i, 0))
    return pl.pallas_call(
        functools.partial(_router_kernel, tm=tm, tiles_per_seq=tp // tm),
        out_shape=(jax.ShapeDtypeStruct((rows, 128), jnp.int32), jax.ShapeDtypeStruct((rows, 128), F32),
                   jax.ShapeDtypeStruct((1, 128), F32)),
        grid=(rows // tm,),
        in_specs=[pl.BlockSpec((tm, D_MODEL), lambda i: (i, 0)),
                  pl.BlockSpec((1, D_MODEL), lambda i: (0, 0)),
                  pl.BlockSpec((D_MODEL, 128), lambda i: (0, 0))],
        out_specs=(blk, blk, pl.BlockSpec((1, 128), lambda i: (0, 0))),
        scratch_shapes=[pltpu.VMEM((1, 128), F32)],
        compiler_params=_params(("arbitrary",)),
        name="moe_router",
    )(hs, gain.reshape(1, D_MODEL), rw)


def _dispatch_kernel(s1_ref, s2_ref, zr_ref, hs_ref, g_ref, xs_ref, rows_ref, zero_ref, sem, zsem,
                     *, td, tg, tiles_per_seq, n_steps):
    n = pl.program_id(0) * tiles_per_seq + pl.program_id(1)
    buf = n % 2
    base = n * td

    def row_copy(r, which):
        slot = (s1_ref, s2_ref)[which][base + r]
        return pltpu.make_async_copy(rows_ref.at[buf, pl.ds(r, 1)], xs_ref.at[pl.ds(slot, 1)], sem.at[buf])

    def wait_rows(b):
        def body(r, c):
            pltpu.make_async_copy(rows_ref.at[b, pl.ds(0, 1)], xs_ref.at[pl.ds(0, 1)], sem.at[b]).wait()
            return c
        lax.fori_loop(0, 2 * td, body, 0)

    @pl.when(n == 0)
    def _():
        zero_ref[...] = jnp.zeros_like(zero_ref)
        for e in range(2 * N_EXPERTS):
            fill = pltpu.make_async_copy(zero_ref, xs_ref.at[pl.ds(pl.multiple_of(zr_ref[e], tg), tg)], zsem)
            fill.start()
            fill.wait()

    @pl.when(n >= 2)
    def _():
        wait_rows(buf)

    x = hs_ref[...]
    ms = jnp.mean(x * x, axis=-1, keepdims=True)
    rows_ref[buf] = x * lax.rsqrt(ms + EPS) * g_ref[...]

    def issue(r, c):
        row_copy(r, 0).start()
        row_copy(r, 1).start()
        return c

    lax.fori_loop(0, td, issue, 0)

    @pl.when(n == n_steps - 1)
    def _():
        wait_rows(buf)
        if n_steps > 1:
            wait_rows(1 - buf)


def _dispatch(hs3, gain, slot1, slot2, zero_rows, n_slots, tg):
    bsz, tp, _ = hs3.shape
    td = T_PAD
    tiles_per_seq = (tp - T_PAD) // td
    kern = functools.partial(_dispatch_kernel, td=td, tg=tg, tiles_per_seq=tiles_per_seq,
                             n_steps=bsz * tiles_per_seq)
    return pl.pallas_call(
        kern,
        out_shape=jax.ShapeDtypeStruct((n_slots, D_MODEL), F32),
        grid_spec=pltpu.PrefetchScalarGridSpec(
            num_scalar_prefetch=3,
            grid=(bsz, tiles_per_seq),
            in_specs=[pl.BlockSpec((None, td, D_MODEL), lambda b, i, *_: (b, i + 1, 0)),
                      pl.BlockSpec((1, D_MODEL), lambda b, i, *_: (0, 0))],
            out_specs=pl.BlockSpec(memory_space=pl.ANY),
            scratch_shapes=[pltpu.VMEM((2, td, D_MODEL), F32), pltpu.VMEM((tg, D_MODEL), F32),
                            pltpu.SemaphoreType.DMA((2,)), pltpu.SemaphoreType.DMA],
        ),
        compiler_params=_params(("arbitrary", "arbitrary")),
        name="moe_dispatch",
    )(slot1, slot2, zero_rows, hs3, gain.reshape(1, D_MODEL))


def _expert_kernel(te_ref, nt_ref, x_ref, wg_ref, wu_ref, wd_ref, o_ref, hn_ref, acc_ref, *, n_f):
    t = pl.program_id(0)
    f = pl.program_id(1)

    @pl.when(t < nt_ref[0])
    def _():
        @pl.when(f == 0)
        def _():
            hn_ref[...] = x_ref[...].astype(BF16)

        hn = hn_ref[...]
        a = jnp.dot(hn, wg_ref[...], preferred_element_type=F32)
        u = jnp.dot(hn, wu_ref[...], preferred_element_type=F32)
        h = (a * jax.nn.sigmoid(a) * u).astype(BF16)
        y = jnp.dot(h, wd_ref[...], preferred_element_type=F32)

        @pl.when(f == 0)
        def _():
            acc_ref[...] = y

        @pl.when(f > 0)
        def _():
            acc_ref[...] += y

    @pl.when(jnp.logical_and(t >= nt_ref[0], f == 0))
    def _():
        acc_ref[...] = jnp.zeros_like(acc_ref)

    @pl.when(f == n_f - 1)
    def _():
        o_ref[...] = acc_ref[...]


def _experts(xs, tile_expert, n_tiles_used, wg_bf, wu_bf, wd_bf, tg):
    slots = xs.shape[0]
    d_ff = wg_bf.shape[2]
    n_f = 2
    tf = d_ff // n_f

    def x_map(t, f, te, nt):
        return (jnp.minimum(t, nt[0] - 1), 0)

    def f_of(t, f, nt):
        return jnp.where(t < nt[0], f, n_f - 1)

    return pl.pallas_call(
        functools.partial(_expert_kernel, n_f=n_f),
        out_shape=jax.ShapeDtypeStruct((slots, D_MODEL), F32),
        grid_spec=pltpu.PrefetchScalarGridSpec(
            num_scalar_prefetch=2,
            grid=(slots // tg, n_f),
            in_specs=[
                pl.BlockSpec((tg, D_MODEL), x_map),
                pl.BlockSpec((None, D_MODEL, tf), lambda t, f, te, nt: (te[t], 0, f_of(t, f, nt))),
                pl.BlockSpec((None, D_MODEL, tf), lambda t, f, te, nt: (te[t], 0, f_of(t, f, nt))),
                pl.BlockSpec((None, tf, D_MODEL), lambda t, f, te, nt: (te[t], f_of(t, f, nt), 0)),
            ],
            out_specs=pl.BlockSpec((tg, D_MODEL), lambda t, f, te, nt: (t, 0)),
            scratch_shapes=[pltpu.VMEM((tg, D_MODEL), BF16), pltpu.VMEM((tg, D_MODEL), F32)],
        ),
        compiler_params=_params(("arbitrary", "arbitrary")),
        name="moe_experts",
    )(tile_expert, n_tiles_used, xs, wg_bf, wu_bf, wd_bf)


def _combine_kernel(s1_ref, s2_ref, hs_ref, wt_ref, y_ref, o_ref, buf_ref, sem, *, tc, tiles_per_seq, n_steps):
    n = pl.program_id(0) * tiles_per_seq + pl.program_id(1)
    cur = n % 2

    def row_copy(step, b, r, which):
        slot = (s1_ref, s2_ref)[which][step * tc + r]
        return pltpu.make_async_copy(y_ref.at[pl.ds(slot, 1)], buf_ref.at[b, which, pl.ds(r, 1)], sem.at[b])

    def fetch(step, b):
        def body(r, c):
            row_copy(step, b, r, 0).start()
            row_copy(step, b, r, 1).start()
            return c
        lax.fori_loop(0, tc, body, 0)

    @pl.when(n == 0)
    def _():
        fetch(0, 0)

    @pl.when(n + 1 < n_steps)
    def _():
        fetch(n + 1, 1 - cur)

    def wait(r, c):
        row_copy(n, cur, r, 0).wait()
        row_copy(n, cur, r, 1).wait()
        return c

    lax.fori_loop(0, tc, wait, 0)
    wt = wt_ref[...]
    w1, w2 = wt[:, 0:1], wt[:, 1:2]
    o_ref[...] = hs_ref[...] + w1 * buf_ref[cur, 0] + w2 * buf_ref[cur, 1]


def _combine(hs3, wts3, y, slot1, slot2, seq):
    bsz = hs3.shape[0]
    tc = T_PAD
    tiles_per_seq = seq // tc
    kern = functools.partial(_combine_kernel, tc=tc, tiles_per_seq=tiles_per_seq, n_steps=bsz * tiles_per_seq)
    return pl.pallas_call(
        kern,
        out_shape=jax.ShapeDtypeStruct((bsz, seq, D_MODEL), F32),
        grid_spec=pltpu.PrefetchScalarGridSpec(
            num_scalar_prefetch=2,
            grid=(bsz, tiles_per_seq),
            in_specs=[
                pl.BlockSpec((None, tc, D_MODEL), lambda b, i, s1, s2: (b, i + 1, 0)),
                pl.BlockSpec((None, tc, 128), lambda b, i, s1, s2: (b, i + 1, 0)),
                pl.BlockSpec(memory_space=pl.ANY),
            ],
            out_specs=pl.BlockSpec((None, tc, D_MODEL), lambda b, i, s1, s2: (b, i, 0)),
            scratch_shapes=[pltpu.VMEM((2, 2, tc, D_MODEL), F32), pltpu.SemaphoreType.DMA((2,))],
        ),
        compiler_params=_params(("arbitrary", "arbitrary")),
        name="moe_combine",
    )(slot1, slot2, hs3, wts3, y)


def _moe(hs, gain, router_w, wg_bf, wu_bf, wd_bf, bsz, tp):
    seq = tp - T_PAD
    n_tok = bsz * seq
    tg = MOE_TILE
    idx, wts, cnt = _router(hs, gain, router_w, tp)
    sel = idx.reshape(bsz, tp, 128)[:, T_PAD:, :4].reshape(n_tok, 4)
    counts = cnt[0, :N_EXPERTS].astype(jnp.int32)
    padded = ((counts + tg - 1) // tg) * tg
    ends = jnp.cumsum(padded)
    starts = ends - padded
    experts = jnp.arange(N_EXPERTS, dtype=jnp.int32)[None, :]
    slot1 = jnp.sum(jnp.where(sel[:, 0:1] == experts, starts[None, :], 0), axis=1) + sel[:, 2]
    slot2 = jnp.sum(jnp.where(sel[:, 1:2] == experts, starts[None, :], 0), axis=1) + sel[:, 3]
    slot1, slot2 = slot1.astype(jnp.int32), slot2.astype(jnp.int32)
    n_slots = 2 * n_tok + N_EXPERTS * tg
    tile_start = jnp.arange(n_slots // tg, dtype=jnp.int32)[:, None] * tg
    tile_expert = jnp.minimum(jnp.sum((tile_start >= ends[None, :]).astype(jnp.int32), axis=1), N_EXPERTS - 1)
    n_tiles_used = (ends[-1:] // tg).astype(jnp.int32)
    tail = n_slots - tg * (1 + jnp.arange(N_EXPERTS, dtype=jnp.int32))
    zero_rows = jnp.concatenate([jnp.maximum(ends - tg, 0), tail]).astype(jnp.int32)

    hs3 = hs.reshape(bsz, tp, D_MODEL)
    xs = _dispatch(hs3, gain, slot1, slot2, zero_rows, n_slots, tg)
    y = _experts(xs, tile_expert.astype(jnp.int32), n_tiles_used, wg_bf, wu_bf, wd_bf, tg)
    return _combine(hs3, wts.reshape(bsz, tp, 128), y, slot1, slot2, seq)


def _permute_qk_cols(w):
    return w.reshape(D_MODEL, 2, ATT_HEADS, ATT_QK_DIM).transpose(0, 2, 1, 3).reshape(D_MODEL, 512)


def kernel(x, meta_tokens, norm1_gain, norm2_gain, w_in, q_norm_gain, k_norm_gain, diff_lambda,
           attn_sub_gain, rel_bias, conv_w, hgrn_lb_logits, hgrn_out_gain, w_branch, w_out,
           ffn_w_gate, ffn_w_up, ffn_w_down, router_w, moe_w_gate, moe_w_up, moe_w_down):
    bsz, seq, _ = x.shape
    depth = w_in.shape[0]
    tp = T_PAD + seq
    assert tp % ATT_TILE == 0 and depth == 2

    head = jnp.concatenate([jnp.zeros((PAD0, D_MODEL), x.dtype), meta_tokens.astype(x.dtype)], axis=0)
    hs = jnp.concatenate([jnp.broadcast_to(head[None], (bsz, T_PAD, D_MODEL)), x], axis=1)
    hs = hs.reshape(bsz * tp, D_MODEL)

    lb_all = jnp.cumsum(jax.nn.softmax(hgrn_lb_logits.astype(F32), axis=0), axis=0)
    lb_all = lb_all - lb_all[0]
    btab = _attn_bias_tables(rel_bias, ATT_TILE)

    out = None
    for layer in range(depth):
        w = w_in[layer]
        w_bf = jnp.concatenate([_permute_qk_cols(w[:, :512]), _permute_qk_cols(w[:, 512:1024]), w[:, 1024:]],
                               axis=1).astype(BF16)
        qk_gain = jnp.concatenate([jnp.tile(q_norm_gain[layer].astype(F32), 8) * (ATT_QK_DIM ** -0.5 * LOG2E),
                                   jnp.tile(k_norm_gain[layer].astype(F32), 8)]).reshape(1, COL_TILE)
        proj = _inproj(hs, norm1_gain[layer], w_bf, qk_gain)
        proj3 = proj.reshape(bsz, tp, IN_COLS)

        lam_init = 0.8 - 0.6 * math.exp(-0.3 * layer)
        lp = diff_lambda[layer].astype(F32)
        lam = jnp.exp(jnp.sum(lp[0] * lp[1])) - jnp.exp(jnp.sum(lp[2] * lp[3])) + lam_init
        u_att = _diff_attention(proj3, lam.reshape(1), attn_sub_gain[layer].astype(F32), btab,
                                1.0 - lam_init)

        lb = lb_all[layer]
        u_hgrn = _hgrn(proj3, jnp.log(lb), jnp.log1p(-lb), hgrn_out_gain[layer].astype(F32))

        hs = _merge(hs, u_att.reshape(bsz * tp, 512), u_hgrn.reshape(bsz * tp, 512), proj,
                    conv_w[layer].astype(F32), w_branch[layer].astype(BF16), w_out[layer].astype(BF16), tp)

        j = layer // 2
        if layer % 2 == 0:
            hs = _ffn(hs, norm2_gain[layer], ffn_w_gate[j].astype(BF16), ffn_w_up[j].astype(BF16),
                      ffn_w_down[j].astype(BF16))
        else:
            out = _moe(hs, norm2_gain[layer], router_w[j], moe_w_gate[j].astype(BF16),
                       moe_w_up[j].astype(BF16), moe_w_down[j].astype(BF16), bsz, tp)
    return out
```

```python
import functools
import math

import numpy as np
import jax
import jax.numpy as jnp
from jax import lax
from jax.experimental import pallas as pl
from jax.experimental.pallas import tpu as pltpu

F32 = jnp.float32
BF16 = jnp.bfloat16

D_MODEL = 1024
N_META = 16
EPS = 1e-6
ATT_HEADS = 4
ATT_QK_DIM = 64
ATT_V_DIM = 128
REL_BUCKETS = 32
REL_MAX_DIST = 128
CONV_K = 3
HGRN_HEADS = 4
HGRN_D = 128
N_EXPERTS = 8
IN_COLS = 8192

T_PAD = 128
PAD0 = T_PAD - N_META
ATT_TILE = 384
VT_ONES = 16
LOG2E = math.log2(math.e)
HGRN_CHUNK = 128
HGRN_SUB = 16
COL_TILE = 1024
GROUP_MEAN_WIDTH = 256
MOE_TILE = 512
MASK = -1e30
VMEM_LIMIT = 56 * 1024 * 1024

BLK_Q, BLK_K, BLK_V = 0, 4, 8
BLK_RQ, BLK_RF, BLK_RI, BLK_RG = 24, 28, 32, 36
BLK_CB, BLK_CC, BLK_CH = 3, 4, 5
BLK_GATE = 5


def _row_tile(rows, target):
    n = rows // 128
    best = 1
    for d in range(1, n + 1):
        if n % d == 0 and d * 128 <= target:
            best = d
    return best * 128


def _params(sem, vmem=VMEM_LIMIT):
    return pltpu.CompilerParams(dimension_semantics=sem, vmem_limit_bytes=vmem)


def _inproj_kernel(x_ref, g_ref, w_ref, qkg_ref, gm_ref, o_ref, xn_ref):
    j = pl.program_id(1)

    @pl.when(j == 0)
    def _():
        x = x_ref[...]
        ms = jnp.mean(x * x, axis=-1, keepdims=True)
        xn_ref[...] = (x * lax.rsqrt(ms + EPS) * g_ref[...]).astype(BF16)

    acc = jnp.dot(xn_ref[...], w_ref[...], preferred_element_type=F32)

    @pl.when(j == 0)
    def _():
        sq = acc * acc
        hi = sq.astype(BF16)
        lo = (sq - hi.astype(F32)).astype(BF16)
        gm = gm_ref[...]
        width = gm.shape[0]
        ms = jnp.concatenate(
            [jnp.dot(hi[:, c:c + width], gm, preferred_element_type=F32)
             + jnp.dot(lo[:, c:c + width], gm, preferred_element_type=F32)
             for c in range(0, COL_TILE, width)], axis=1)
        o_ref[...] = (acc * lax.rsqrt(ms + EPS) * qkg_ref[...]).astype(BF16)

    @pl.when(j > 0)
    def _():
        o_ref[...] = acc.astype(BF16)


def _inproj(hs, gain, w_bf, qk_gain):
    rows = hs.shape[0]
    tm = _row_tile(rows, 1536)
    n_col = IN_COLS // COL_TILE
    assert COL_TILE == 4 * ATT_HEADS * ATT_QK_DIM
    grp = np.arange(GROUP_MEAN_WIDTH) // ATT_QK_DIM
    gm = jnp.asarray((grp[:, None] == grp[None, :]).astype(np.float32) / ATT_QK_DIM, BF16)
    return pl.pallas_call(
        _inproj_kernel,
        out_shape=jax.ShapeDtypeStruct((rows, IN_COLS), BF16),
        grid=(rows // tm, n_col),
        in_specs=[
            pl.BlockSpec((tm, D_MODEL), lambda i, j: (i, 0)),
            pl.BlockSpec((1, D_MODEL), lambda i, j: (0, 0)),
            pl.BlockSpec((D_MODEL, COL_TILE), lambda i, j: (0, j)),
            pl.BlockSpec((1, COL_TILE), lambda i, j: (0, 0)),
            pl.BlockSpec((GROUP_MEAN_WIDTH, GROUP_MEAN_WIDTH), lambda i, j: (0, 0)),
        ],
        out_specs=pl.BlockSpec((tm, COL_TILE), lambda i, j: (i, j)),
        scratch_shapes=[pltpu.VMEM((tm, D_MODEL), BF16)],
        compiler_params=_params(("parallel", "arbitrary")),
        name="inproj",
    )(hs, gain.reshape(1, D_MODEL), w_bf, qk_gain, gm)


def _rel_bucket_table(n_max):
    n = np.arange(n_max, dtype=np.int64)
    max_exact = REL_BUCKETS // 2
    nf = np.maximum(n, 1).astype(np.float32)
    large = max_exact + (np.log(nf / np.float32(max_exact)) / np.float32(math.log(REL_MAX_DIST / max_exact))
                         * np.float32(REL_BUCKETS - max_exact)).astype(np.int32)
    large = np.minimum(large, REL_BUCKETS - 1)
    return np.where(n < max_exact, n, large).astype(np.int32)


def _attn_bias_tables(rel_bias, t):
    bucket = _rel_bucket_table(2 * t)
    assert np.all(bucket[t + 1:] == REL_BUCKETS - 1) and np.all(np.diff(bucket) >= 0)
    first_dist = tuple(int(np.searchsorted(bucket, b, side="left")) for b in range(REL_BUCKETS))
    return pl.pallas_call(
        functools.partial(_bias_kernel, t=t, first_dist=first_dist),
        out_shape=jax.ShapeDtypeStruct((ATT_HEADS, 6, t, t), F32),
        grid=(ATT_HEADS,),
        in_specs=[pl.BlockSpec(memory_space=pltpu.SMEM)],
        out_specs=pl.BlockSpec((None, 6, t, t), lambda h: (h, 0, 0, 0)),
        compiler_params=_params(("parallel",)),
        name="attn_bias",
    )(rel_bias.astype(F32))


def _bias_kernel(rb_ref, o_ref, *, t, first_dist):
    h = pl.program_id(0)
    key = lax.broadcasted_iota(jnp.int32, (t, t), 0)
    qry = lax.broadcasted_iota(jnp.int32, (t, t), 1)
    far = rb_ref[REL_BUCKETS - 1, h]

    def table(n):
        val = jnp.full((t, t), rb_ref[0, h] - far, F32)
        for b in range(1, REL_BUCKETS):
            val = jnp.where(n >= first_dist[b], rb_ref[b, h] - far, val)
        return val * LOG2E

    n0 = qry - key
    diag = jnp.where(n0 >= 0, table(n0), MASK)
    near = table(n0 + t)
    zero = jnp.zeros((t, t), F32)
    for kind, tab in enumerate((diag, near, zero)):
        o_ref[kind] = tab
        o_ref[kind + 3] = jnp.where(key < PAD0, MASK, tab)


def _attn_kernel(lam_ref, q_ref, k_ref, v_ref, bt_ref, sg_ref, o_ref, m_ref, acc_ref, vt_ref,
                 *, t, out_scale):
    n_t = vt_ref.shape[0]
    for j in range(n_t):
        vt_ref[j, :ATT_V_DIM, :] = v_ref[j * t:(j + 1) * t, :].astype(F32).T.astype(BF16)
        vt_ref[j, ATT_V_DIM:, :] = jnp.ones((VT_ONES, t), BF16)
    lane = lax.broadcasted_iota(jnp.int32, (1, 2 * ATT_QK_DIM), 1)
    nt = (((1,), (1,)), ((), ()))

    def query_tile(i, carry):
        rows = pl.ds(pl.multiple_of(i * t, t), t)
        q = q_ref[rows, :]
        zero = jnp.zeros_like(q)
        q_maps = (jnp.where(lane < ATT_QK_DIM, q, zero), jnp.where(lane >= ATT_QK_DIM, q, zero))
        m_ref[...] = jnp.full(m_ref.shape, MASK, F32)
        acc_ref[...] = jnp.zeros(acc_ref.shape, F32)

        def scores(j):
            ks = k_ref[pl.ds(pl.multiple_of(j * t, t), t), :]
            bias = bt_ref[jnp.minimum(i - j, 2) + jnp.where(j == 0, 3, 0)]
            return tuple(lax.dot_general(ks, qm, nt, preferred_element_type=F32) + bias for qm in q_maps)

        def consume(j, s):
            vt = vt_ref[j]
            for mi in range(2):
                m_old = m_ref[mi]
                m_new = jnp.maximum(m_old, jnp.max(s[mi], axis=0, keepdims=True))
                m_ref[mi] = m_new
                p = jnp.exp2(s[mi] - m_new).astype(BF16)
                acc_ref[mi] = jnp.exp2(m_old - m_new) * acc_ref[mi] + jnp.dot(vt, p, preferred_element_type=F32)

        def step(j, s):
            nxt = scores(j + 1)
            consume(j, s)
            return nxt

        consume(i, lax.fori_loop(0, i, step, scores(0)))

        a1, a2 = acc_ref[0], acc_ref[1]
        o_t = (a1[:ATT_V_DIM] / a1[ATT_V_DIM:ATT_V_DIM + 1]
               - lam_ref[0] * (a2[:ATT_V_DIM] / a2[ATT_V_DIM:ATT_V_DIM + 1]))
        o = o_t.T
        ms = jnp.mean(o * o, axis=-1, keepdims=True)
        y = o * lax.rsqrt(ms + EPS) * (sg_ref[...] * out_scale)
        row = i * t + lax.broadcasted_iota(jnp.int32, (t, 1), 0)
        o_ref[rows, :] = jnp.where(row >= PAD0, y, 0.0).astype(BF16)
        return carry

    lax.fori_loop(0, n_t, query_tile, 0)


def _diff_attention(proj3, lam, sub_gain, btab, out_scale):
    bsz, tp, _ = proj3.shape
    t = ATT_TILE
    n_t = tp // t
    rows = ATT_V_DIM + VT_ONES
    kern = functools.partial(_attn_kernel, t=t, out_scale=out_scale)
    seq = lambda blk: pl.BlockSpec((None, tp, 128), lambda b, h: (b, 0, blk + h))
    return pl.pallas_call(
        kern,
        out_shape=jax.ShapeDtypeStruct((bsz, tp, ATT_HEADS * ATT_V_DIM), BF16),
        grid=(bsz, ATT_HEADS),
        in_specs=[
            pl.BlockSpec(memory_space=pltpu.SMEM),
            seq(BLK_Q), seq(BLK_K), seq(BLK_V),
            pl.BlockSpec((None, 6, t, t), lambda b, h: (h, 0, 0, 0)),
            pl.BlockSpec((1, ATT_V_DIM), lambda b, h: (0, 0)),
        ],
        out_specs=pl.BlockSpec((None, tp, 128), lambda b, h: (b, 0, h)),
        scratch_shapes=[pltpu.VMEM((2, 1, t), F32), pltpu.VMEM((2, rows, t), F32),
                        pltpu.VMEM((n_t, rows, t), BF16)],
        compiler_params=_params(("parallel", "parallel")),
        name="diff_attn",
    )(lam, proj3, proj3, proj3, btab, sub_gain.reshape(1, ATT_V_DIM))


def _split3(x):
    h1 = x.astype(BF16)
    r1 = x - h1.astype(F32)
    h2 = r1.astype(BF16)
    h3 = (r1 - h2.astype(F32)).astype(BF16)
    return h1, h2, h3


def _hgrn_kernel(q_ref, f_ref, i_ref, g_ref, la_ref, l1m_ref, og_ref, o_ref):
    c_len, sub = HGRN_CHUNK, HGRN_SUB
    half = sub // 2
    n_chunk = q_ref.shape[0] // c_len
    la, l1m, og = la_ref[...], l1m_ref[...], og_ref[...]
    rr = lax.broadcasted_iota(jnp.int32, (c_len, c_len), 0)
    cc = lax.broadcasted_iota(jnp.int32, (c_len, c_len), 1)
    tri = jnp.where(cc <= rr, 1.0, 0.0).astype(BF16)
    row8 = lax.broadcasted_iota(jnp.int32, (half, 1), 0)
    lane8 = lax.broadcasted_iota(jnp.int32, (half, c_len), 1)
    nt = (((1,), (1,)), ((), ()))

    def chunk(c, st):
        r0 = pl.multiple_of(c * c_len, c_len)
        rows = pl.ds(r0, c_len)
        z = f_ref[rows, :].astype(F32)
        qh = q_ref[rows, :].astype(F32)
        qh = qh * jax.nn.sigmoid(qh)
        v = i_ref[rows, :]
        gate = g_ref[rows, :].astype(F32)
        sp = jnp.log(1.0 + jnp.exp(-jnp.abs(z)))
        bb = l1m + jnp.minimum(z, 0.0) - sp
        log_f = jnp.maximum(la, bb) + jnp.log(1.0 + jnp.exp(-jnp.abs(la - bb)))
        valid = (r0 + lax.broadcasted_iota(jnp.int32, (c_len, 1), 0)) >= PAD0
        log_k = jnp.where(valid, l1m + jnp.minimum(-z, 0.0) - sp, -jnp.inf)
        g = sum(jnp.dot(tri, part, preferred_element_type=F32) for part in _split3(log_f))
        ck = log_k - g
        o_inter = lax.dot_general((qh * jnp.exp(g)).astype(BF16), st.astype(BF16), nt,
                                  preferred_element_type=F32)
        a_rows = []
        for a in range(c_len // sub):
            lo = a * sub
            ga = (g[lo:lo + half, :], g[lo + half:lo + sub, :])
            qa = (qh[lo:lo + half, :], qh[lo + half:lo + sub, :])
            if a == 0:
                blk = [jnp.zeros((half, c_len), F32)] * 2
            else:
                gs = g[lo - 1:lo, :]
                qd = (qh[lo:lo + sub, :] * jnp.exp(g[lo:lo + sub, :] - gs)).astype(BF16)
                kd = jnp.exp(jnp.minimum(gs - g[:lo, :], 0.0) + log_k[:lo, :]).astype(BF16)
                kd = jnp.concatenate([kd, jnp.zeros((c_len - lo, HGRN_D), BF16)], axis=0)
                a_off = lax.dot_general(qd, kd, nt, preferred_element_type=F32)
                blk = [a_off[:half, :], a_off[half:, :]]
            for s in range(sub):
                crow = ck[lo + s:lo + s + 1, :]
                for hh in range(s // half, 2):
                    col = jnp.sum(qa[hh] * jnp.exp(ga[hh] + crow), axis=-1, keepdims=True)
                    blk[hh] = jnp.where(lane8 == lo + s, col, blk[hh])
            for hh in range(2):
                a_rows.append(jnp.where(lane8 <= lo + hh * half + row8, blk[hh], 0.0))
        a_full = jnp.concatenate(a_rows, axis=0).astype(BF16)
        o = o_inter + jnp.dot(a_full, v, preferred_element_type=F32)
        g_last = g[c_len - 1:c_len, :]
        kd = jnp.exp(g_last - g + log_k).astype(BF16)
        st = st * jnp.exp(g_last) + lax.dot_general(v, kd, (((0,), (0,)), ((), ())),
                                                    preferred_element_type=F32)
        ms = jnp.mean(o * o, axis=-1, keepdims=True)
        y = o * lax.rsqrt(ms + EPS) * og * (gate * jax.nn.sigmoid(gate))
        o_ref[rows, :] = y.astype(BF16)
        return st

    lax.fori_loop(0, n_chunk, chunk, jnp.zeros((HGRN_D, HGRN_D), F32), unroll=3)


def _hgrn(proj3, log_lb, log1m_lb, out_gain):
    bsz, tp, _ = proj3.shape
    seq = lambda blk: pl.BlockSpec((None, tp, 128), lambda b, h: (b, 0, blk + h))
    chan = pl.BlockSpec((None, 1, HGRN_D), lambda b, h: (h, 0, 0))
    return pl.pallas_call(
        _hgrn_kernel,
        out_shape=jax.ShapeDtypeStruct((bsz, tp, HGRN_HEADS * HGRN_D), BF16),
        grid=(bsz, HGRN_HEADS),
        in_specs=[seq(BLK_RQ), seq(BLK_RF), seq(BLK_RI), seq(BLK_RG), chan, chan,
                  pl.BlockSpec((1, HGRN_D), lambda b, h: (0, 0))],
        out_specs=pl.BlockSpec((None, tp, 128), lambda b, h: (b, 0, h)),
        compiler_params=_params(("parallel", "parallel")),
        name="hgrn2",
    )(proj3, proj3, proj3, proj3,
      log_lb.reshape(HGRN_HEADS, 1, HGRN_D), log1m_lb.reshape(HGRN_HEADS, 1, HGRN_D),
      out_gain.reshape(1, HGRN_D))


def _merge_kernel(hs_ref, ua_ref, ur_ref, cb_ref, cc_ref, ch_ref, pc_ref, ph_ref,
                  g0_ref, g1_ref, g2_ref, cw_ref, wb_ref, wo_ref, o_ref, *, tm, tiles_per_seq):
    i = pl.program_id(0)
    row = (i % tiles_per_seq) * tm + lax.broadcasted_iota(jnp.int32, (tm, 1), 0)
    valid = row >= PAD0
    z = jnp.where(valid, cc_ref[...].astype(F32) * ch_ref[...].astype(F32), 0.0)
    halo_row = (i % tiles_per_seq) * tm - 8 + lax.broadcasted_iota(jnp.int32, (8, 1), 0)
    zp = jnp.where(halo_row >= PAD0, pc_ref[...].astype(F32) * ph_ref[...].astype(F32), 0.0)
    zz = jnp.concatenate([zp, z], axis=0)
    cw = cw_ref[...]
    y = (cw[2:3, :] * z + cw[1:2, :] * zz[7:7 + tm, :] + cw[0:1, :] * zz[6:6 + tm, :])
    u_conv = jnp.where(valid, cb_ref[...].astype(F32) * y, 0.0).astype(BF16)
    mixed = jnp.zeros((tm, D_MODEL), F32)
    for n, (u, g_ref) in enumerate(((ua_ref[...], g0_ref), (u_conv, g1_ref), (ur_ref[...], g2_ref))):
        up = jnp.dot(u, wb_ref[n], preferred_element_type=F32)
        mixed = mixed + jax.nn.sigmoid(g_ref[...].astype(F32)) * up
    o_ref[...] = hs_ref[...] + jnp.dot(mixed.astype(BF16), wo_ref[...], preferred_element_type=F32)


def _merge(hs, u_att, u_hgrn, proj, conv_w, wb_bf, wo_bf, tp):
    rows = hs.shape[0]
    tm = _row_tile(tp, 384)
    kern = functools.partial(_merge_kernel, tm=tm, tiles_per_seq=tp // tm)
    row_blk = lambda w, blk: pl.BlockSpec((tm, w), lambda i: (i, blk))
    halo = lambda blk: pl.BlockSpec((8, 512), lambda i: (jnp.maximum(i * (tm // 8) - 1, 0), blk))
    const = lambda shape: pl.BlockSpec(shape, lambda i: (0,) * len(shape))
    return pl.pallas_call(
        kern,
        out_shape=jax.ShapeDtypeStruct((rows, D_MODEL), F32),
        grid=(rows // tm,),
        in_specs=[row_blk(D_MODEL, 0), row_blk(512, 0), row_blk(512, 0),
                  row_blk(512, BLK_CB), row_blk(512, BLK_CC), row_blk(512, BLK_CH),
                  halo(BLK_CC), halo(BLK_CH),
                  row_blk(1024, BLK_GATE), row_blk(1024, BLK_GATE + 1), row_blk(1024, BLK_GATE + 2),
                  const((CONV_K, 512)), const((3, 512, D_MODEL)), const((D_MODEL, D_MODEL))],
        out_specs=row_blk(D_MODEL, 0),
        compiler_params=_params(("parallel",)),
        name="merge",
    )(hs, u_att, u_hgrn, proj, proj, proj, proj, proj, proj, proj, proj, conv_w, wb_bf, wo_bf)


def _ffn_kernel(hs_ref, g_ref, wg_ref, wu_ref, wd_ref, o_ref):
    x = hs_ref[...]
    ms = jnp.mean(x * x, axis=-1, keepdims=True)
    hn = (x * lax.rsqrt(ms + EPS) * g_ref[...]).astype(BF16)
    a = jnp.dot(hn, wg_ref[...], preferred_element_type=F32)
    u = jnp.dot(hn, wu_ref[...], preferred_element_type=F32)
    h = (a * jax.nn.sigmoid(a) * u).astype(BF16)
    o_ref[...] = x + jnp.dot(h, wd_ref[...], preferred_element_type=F32)


def _ffn(hs, gain, wg_bf, wu_bf, wd_bf):
    rows = hs.shape[0]
    d_ff = wg_bf.shape[1]
    tm = _row_tile(rows, 384)
    const = lambda shape: pl.BlockSpec(shape, lambda i: (0, 0))
    return pl.pallas_call(
        _ffn_kernel,
        out_shape=jax.ShapeDtypeStruct((rows, D_MODEL), F32),
        grid=(rows // tm,),
        in_specs=[pl.BlockSpec((tm, D_MODEL), lambda i: (i, 0)), const((1, D_MODEL)),
                  const((D_MODEL, d_ff)), const((D_MODEL, d_ff)), const((d_ff, D_MODEL))],
        out_specs=pl.BlockSpec((tm, D_MODEL), lambda i: (i, 0)),
        compiler_params=_params(("parallel",)),
        name="ffn_dense",
    )(hs, gain.reshape(1, D_MODEL), wg_bf, wu_bf, wd_bf)


def _router_kernel(hs_ref, g_ref, rw_ref, idx_ref, wt_ref, cnt_ref, carry_ref, *, tm, tiles_per_seq):
    i = pl.program_id(0)

    @pl.when(i == 0)
    def _():
        carry_ref[...] = jnp.zeros_like(carry_ref)

    x = hs_ref[...]
    ms = jnp.mean(x * x, axis=-1, keepdims=True)
    hn = x * lax.rsqrt(ms + EPS) * g_ref[...]
    logits = jnp.dot(hn, rw_ref[...], preferred_element_type=F32, precision=lax.Precision.HIGHEST)
    lane = lax.broadcasted_iota(jnp.int32, logits.shape, 1)
    lane_f = lane.astype(F32)
    logits = jnp.where(lane < N_EXPERTS, logits, -jnp.inf)
    m1 = jnp.max(logits, axis=-1, keepdims=True)
    i1 = jnp.min(jnp.where(logits == m1, lane_f, 128.0), axis=-1, keepdims=True)
    rest = jnp.where(lane_f == i1, -jnp.inf, logits)
    m2 = jnp.max(rest, axis=-1, keepdims=True)
    i2 = jnp.min(jnp.where(rest == m2, lane_f, 128.0), axis=-1, keepdims=True)
    e2 = jnp.exp(m2 - m1)
    w1 = 1.0 / (1.0 + e2)
    w2 = e2 / (1.0 + e2)
    row = (i % tiles_per_seq) * tm + lax.broadcasted_iota(jnp.int32, (tm, 1), 0)
    real = row >= T_PAD
    hot1 = jnp.where(jnp.logical_and(real, lane_f == i1), 1.0, 0.0)
    hot2 = jnp.where(jnp.logical_and(real, lane_f == i2), 1.0, 0.0)
    both = hot1 + hot2
    rr = lax.broadcasted_iota(jnp.int32, (tm, tm), 0)
    cc = lax.broadcasted_iota(jnp.int32, (tm, tm), 1)
    earlier = jnp.where(cc < rr, 1.0, 0.0).astype(BF16)
    before = carry_ref[...] + jnp.dot(earlier, both.astype(BF16), preferred_element_type=F32)
    r1 = jnp.sum(before * hot1, axis=-1, keepdims=True)
    r2 = jnp.sum(before * hot2, axis=-1, keepdims=True)
    carry_ref[...] += jnp.sum(both, axis=0, keepdims=True)
    cnt_ref[...] = carry_ref[...]
    packed = jnp.where(lane == 0, i1, jnp.where(lane == 1, i2, jnp.where(lane == 2, r1, jnp.where(lane == 3, r2, 0.0))))
    idx_ref[...] = packed.astype(jnp.int32)
    wt_ref[...] = jnp.where(lane == 0, w1, jnp.where(lane == 1, w2, 0.0))


def _router(hs, gain, router_w, tp):
    rows = hs.shape[0]
    tm = _row_tile(tp, 384)
    rw = jnp.zeros((D_MODEL, 128), F32).at[:, :N_EXPERTS].set(router_w.astype(F32))
    blk = pl.BlockSpec((tm, 128), lambda i: (i, 0))
    return pl.pallas_call(
        functools.partial(_router_kernel, tm=tm, tiles_per_seq=tp // tm),
        out_shape=(jax.ShapeDtypeStruct((rows, 128), jnp.int32), jax.ShapeDtypeStruct((rows, 128), F32),
                   jax.ShapeDtypeStruct((1, 128), F32)),
        grid=(rows // tm,),
        in_specs=[pl.BlockSpec((tm, D_MODEL), lambda i: (i, 0)),
                  pl.BlockSpec((1, D_MODEL), lambda i: (0, 0)),
                  pl.BlockSpec((D_MODEL, 128), lambda i: (0, 0))],
        out_specs=(blk, blk, pl.BlockSpec((1, 128), lambda i: (0, 0))),
        scratch_shapes=[pltpu.VMEM((1, 128), F32)],
        compiler_params=_params(("arbitrary",)),
        name="moe_router",
    )(hs, gain.reshape(1, D_MODEL), rw)


def _dispatch_kernel(s1_ref, s2_ref, zr_ref, hs_ref, g_ref, xs_ref, rows_ref, zero_ref, sem, zsem,
                     *, td, tg, tiles_per_seq, n_steps):
    n = pl.program_id(0) * tiles_per_seq + pl.program_id(1)
    buf = n % 2
    base = n * td

    def row_copy(r, which):
        slot = (s1_ref, s2_ref)[which][base + r]
        return pltpu.make_async_copy(rows_ref.at[buf, pl.ds(r, 1)], xs_ref.at[pl.ds(slot, 1)], sem.at[buf])

    def wait_rows(b):
        def body(r, c):
            pltpu.make_async_copy(rows_ref.at[b, pl.ds(0, 1)], xs_ref.at[pl.ds(0, 1)], sem.at[b]).wait()
            return c
        lax.fori_loop(0, 2 * td, body, 0)

    @pl.when(n == 0)
    def _():
        zero_ref[...] = jnp.zeros_like(zero_ref)
        for e in range(2 * N_EXPERTS):
            fill = pltpu.make_async_copy(zero_ref, xs_ref.at[pl.ds(pl.multiple_of(zr_ref[e], tg), tg)], zsem)
            fill.start()
            fill.wait()

    @pl.when(n >= 2)
    def _():
        wait_rows(buf)

    x = hs_ref[...]
    ms = jnp.mean(x * x, axis=-1, keepdims=True)
    rows_ref[buf] = x * lax.rsqrt(ms + EPS) * g_ref[...]

    def issue(r, c):
        row_copy(r, 0).start()
        row_copy(r, 1).start()
        return c

    lax.fori_loop(0, td, issue, 0)

    @pl.when(n == n_steps - 1)
    def _():
        wait_rows(buf)
        if n_steps > 1:
            wait_rows(1 - buf)


def _dispatch(hs3, gain, slot1, slot2, zero_rows, n_slots, tg):
    bsz, tp, _ = hs3.shape
    td = T_PAD
    tiles_per_seq = (tp - T_PAD) // td
    kern = functools.partial(_dispatch_kernel, td=td, tg=tg, tiles_per_seq=tiles_per_seq,
                             n_steps=bsz * tiles_per_seq)
    return pl.pallas_call(
        kern,
        out_shape=jax.ShapeDtypeStruct((n_slots, D_MODEL), F32),
        grid_spec=pltpu.PrefetchScalarGridSpec(
            num_scalar_prefetch=3,
            grid=(bsz, tiles_per_seq),
            in_specs=[pl.BlockSpec((None, td, D_MODEL), lambda b, i, *_: (b, i + 1, 0)),
                      pl.BlockSpec((1, D_MODEL), lambda b, i, *_: (0, 0))],
            out_specs=pl.BlockSpec(memory_space=pl.ANY),
            scratch_shapes=[pltpu.VMEM((2, td, D_MODEL), F32), pltpu.VMEM((tg, D_MODEL), F32),
                            pltpu.SemaphoreType.DMA((2,)), pltpu.SemaphoreType.DMA],
        ),
        compiler_params=_params(("arbitrary", "arbitrary")),
        name="moe_dispatch",
    )(slot1, slot2, zero_rows, hs3, gain.reshape(1, D_MODEL))


def _expert_kernel(te_ref, nt_ref, x_ref, wg_ref, wu_ref, wd_ref, o_ref, hn_ref, acc_ref, *, n_f):
    t = pl.program_id(0)
    f = pl.program_id(1)

    @pl.when(t < nt_ref[0])
    def _():
        @pl.when(f == 0)
        def _():
            hn_ref[...] = x_ref[...].astype(BF16)

        hn = hn_ref[...]
        a = jnp.dot(hn, wg_ref[...], preferred_element_type=F32)
        u = jnp.dot(hn, wu_ref[...], preferred_element_type=F32)
        h = (a * jax.nn.sigmoid(a) * u).astype(BF16)
        y = jnp.dot(h, wd_ref[...], preferred_element_type=F32)

        @pl.when(f == 0)
        def _():
            acc_ref[...] = y

        @pl.when(f > 0)
        def _():
            acc_ref[...] += y

    @pl.when(jnp.logical_and(t >= nt_ref[0], f == 0))
    def _():
        acc_ref[...] = jnp.zeros_like(acc_ref)

    @pl.when(f == n_f - 1)
    def _():
        o_ref[...] = acc_ref[...]


def _experts(xs, tile_expert, n_tiles_used, wg_bf, wu_bf, wd_bf, tg):
    slots = xs.shape[0]
    d_ff = wg_bf.shape[2]
    n_f = 2
    tf = d_ff // n_f

    def x_map(t, f, te, nt):
        return (jnp.minimum(t, nt[0] - 1), 0)

    def f_of(t, f, nt):
        return jnp.where(t < nt[0], f, n_f - 1)

    return pl.pallas_call(
        functools.partial(_expert_kernel, n_f=n_f),
        out_shape=jax.ShapeDtypeStruct((slots, D_MODEL), F32),
        grid_spec=pltpu.PrefetchScalarGridSpec(
            num_scalar_prefetch=2,
            grid=(slots // tg, n_f),
            in_specs=[
                pl.BlockSpec((tg, D_MODEL), x_map),
                pl.BlockSpec((None, D_MODEL, tf), lambda t, f, te, nt: (te[t], 0, f_of(t, f, nt))),
                pl.BlockSpec((None, D_MODEL, tf), lambda t, f, te, nt: (te[t], 0, f_of(t, f, nt))),
                pl.BlockSpec((None, tf, D_MODEL), lambda t, f, te, nt: (te[t], f_of(t, f, nt), 0)),
            ],
            out_specs=pl.BlockSpec((tg, D_MODEL), lambda t, f, te, nt: (t, 0)),
            scratch_shapes=[pltpu.VMEM((tg, D_MODEL), BF16), pltpu.VMEM((tg, D_MODEL), F32)],
        ),
        compiler_params=_params(("arbitrary", "arbitrary")),
        name="moe_experts",
    )(tile_expert, n_tiles_used, xs, wg_bf, wu_bf, wd_bf)


def _combine_kernel(s1_ref, s2_ref, hs_ref, wt_ref, y_ref, o_ref, buf_ref, sem, *, tc, tiles_per_seq, n_steps):
    n = pl.program_id(0) * tiles_per_seq + pl.program_id(1)
    cur = n % 2

    def row_copy(step, b, r, which):
        slot = (s1_ref, s2_ref)[which][step * tc + r]
        return pltpu.make_async_copy(y_ref.at[pl.ds(slot, 1)], buf_ref.at[b, which, pl.ds(r, 1)], sem.at[b])

    def fetch(step, b):
        def body(r, c):
            row_copy(step, b, r, 0).start()
            row_copy(step, b, r, 1).start()
            return c
        lax.fori_loop(0, tc, body, 0)

    @pl.when(n == 0)
    def _():
        fetch(0, 0)

    @pl.when(n + 1 < n_steps)
    def _():
        fetch(n + 1, 1 - cur)

    def wait(r, c):
        row_copy(n, cur, r, 0).wait()
        row_copy(n, cur, r, 1).wait()
        return c

    lax.fori_loop(0, tc, wait, 0)
    wt = wt_ref[...]
    w1, w2 = wt[:, 0:1], wt[:, 1:2]
    o_ref[...] = hs_ref[...] + w1 * buf_ref[cur, 0] + w2 * buf_ref[cur, 1]


def _combine(hs3, wts3, y, slot1, slot2, seq):
    bsz = hs3.shape[0]
    tc = T_PAD
    tiles_per_seq = seq // tc
    kern = functools.partial(_combine_kernel, tc=tc, tiles_per_seq=tiles_per_seq, n_steps=bsz * tiles_per_seq)
    return pl.pallas_call(
        kern,
        out_shape=jax.ShapeDtypeStruct((bsz, seq, D_MODEL), F32),
        grid_spec=pltpu.PrefetchScalarGridSpec(
            num_scalar_prefetch=2,
            grid=(bsz, tiles_per_seq),
            in_specs=[
                pl.BlockSpec((None, tc, D_MODEL), lambda b, i, s1, s2: (b, i + 1, 0)),
                pl.BlockSpec((None, tc, 128), lambda b, i, s1, s2: (b, i + 1, 0)),
                pl.BlockSpec(memory_space=pl.ANY),
            ],
            out_specs=pl.BlockSpec((None, tc, D_MODEL), lambda b, i, s1, s2: (b, i, 0)),
            scratch_shapes=[pltpu.VMEM((2, 2, tc, D_MODEL), F32), pltpu.SemaphoreType.DMA((2,))],
        ),
        compiler_params=_params(("arbitrary", "arbitrary")),
        name="moe_combine",
    )(slot1, slot2, hs3, wts3, y)


def _moe(hs, gain, router_w, wg_bf, wu_bf, wd_bf, bsz, tp):
    seq = tp - T_PAD
    n_tok = bsz * seq
    tg = MOE_TILE
    idx, wts, cnt = _router(hs, gain, router_w, tp)
    sel = idx.reshape(bsz, tp, 128)[:, T_PAD:, :4].reshape(n_tok, 4)
    counts = cnt[0, :N_EXPERTS].astype(jnp.int32)
    padded = ((counts + tg - 1) // tg) * tg
    ends = jnp.cumsum(padded)
    starts = ends - padded
    experts = jnp.arange(N_EXPERTS, dtype=jnp.int32)[None, :]
    slot1 = jnp.sum(jnp.where(sel[:, 0:1] == experts, starts[None, :], 0), axis=1) + sel[:, 2]
    slot2 = jnp.sum(jnp.where(sel[:, 1:2] == experts, starts[None, :], 0), axis=1) + sel[:, 3]
    slot1, slot2 = slot1.astype(jnp.int32), slot2.astype(jnp.int32)
    n_slots = 2 * n_tok + N_EXPERTS * tg
    tile_start = jnp.arange(n_slots // tg, dtype=jnp.int32)[:, None] * tg
    tile_expert = jnp.minimum(jnp.sum((tile_start >= ends[None, :]).astype(jnp.int32), axis=1), N_EXPERTS - 1)
    n_tiles_used = (ends[-1:] // tg).astype(jnp.int32)
    tail = n_slots - tg * (1 + jnp.arange(N_EXPERTS, dtype=jnp.int32))
    zero_rows = jnp.concatenate([jnp.maximum(ends - tg, 0), tail]).astype(jnp.int32)

    hs3 = hs.reshape(bsz, tp, D_MODEL)
    xs = _dispatch(hs3, gain, slot1, slot2, zero_rows, n_slots, tg)
    y = _experts(xs, tile_expert.astype(jnp.int32), n_tiles_used, wg_bf, wu_bf, wd_bf, tg)
    return _combine(hs3, wts.reshape(bsz, tp, 128), y, slot1, slot2, seq)


def _permute_qk_cols(w):
    return w.reshape(D_MODEL, 2, ATT_HEADS, ATT_QK_DIM).transpose(0, 2, 1, 3).reshape(D_MODEL, 512)


def kernel(x, meta_tokens, norm1_gain, norm2_gain, w_in, q_norm_gain, k_norm_gain, diff_lambda,
           attn_sub_gain, rel_bias, conv_w, hgrn_lb_logits, hgrn_out_gain, w_branch, w_out,
           ffn_w_gate, ffn_w_up, ffn_w_down, router_w, moe_w_gate, moe_w_up, moe_w_down):
    bsz, seq, _ = x.shape
    depth = w_in.shape[0]
    tp = T_PAD + seq
    assert tp % ATT_TILE == 0 and depth == 2

    head = jnp.concatenate([jnp.zeros((PAD0, D_MODEL), x.dtype), meta_tokens.astype(x.dtype)], axis=0)
    hs = jnp.concatenate([jnp.broadcast_to(head[None], (bsz, T_PAD, D_MODEL)), x], axis=1)
    hs = hs.reshape(bsz * tp, D_MODEL)

    lb_all = jnp.cumsum(jax.nn.softmax(hgrn_lb_logits.astype(F32), axis=0), axis=0)
    lb_all = lb_all - lb_all[0]
    btab = _attn_bias_tables(rel_bias, ATT_TILE)

    out = None
    for layer in range(depth):
        w = w_in[layer]
        w_bf = jnp.concatenate([_permute_qk_cols(w[:, :512]), _permute_qk_cols(w[:, 512:1024]), w[:, 1024:]],
                               axis=1).astype(BF16)
        qk_gain = jnp.concatenate([jnp.tile(q_norm_gain[layer].astype(F32), 8) * (ATT_QK_DIM ** -0.5 * LOG2E),
                                   jnp.tile(k_norm_gain[layer].astype(F32), 8)]).reshape(1, COL_TILE)
        proj = _inproj(hs, norm1_gain[layer], w_bf, qk_gain)
        proj3 = proj.reshape(bsz, tp, IN_COLS)

        lam_init = 0.8 - 0.6 * math.exp(-0.3 * layer)
        lp = diff_lambda[layer].astype(F32)
        lam = jnp.exp(jnp.sum(lp[0] * lp[1])) - jnp.exp(jnp.sum(lp[2] * lp[3])) + lam_init
        u_att = _diff_attention(proj3, lam.reshape(1), attn_sub_gain[layer].astype(F32), btab,
                                1.0 - lam_init)

        lb = lb_all[layer]
        u_hgrn = _hgrn(proj3, jnp.log(lb), jnp.log1p(-lb), hgrn_out_gain[layer].astype(F32))

        hs = _merge(hs, u_att.reshape(bsz * tp, 512), u_hgrn.reshape(bsz * tp, 512), proj,
                    conv_w[layer].astype(F32), w_branch[layer].astype(BF16), w_out[layer].astype(BF16), tp)

        j = layer // 2
        if layer % 2 == 0:
            hs = _ffn(hs, norm2_gain[layer], ffn_w_gate[j].astype(BF16), ffn_w_up[j].astype(BF16),
                      ffn_w_down[j].astype(BF16))
        else:
            out = _moe(hs, norm2_gain[layer], router_w[j], moe_w_gate[j].astype(BF16),
                       moe_w_up[j].astype(BF16), moe_w_down[j].astype(BF16), bsz, tp)
    return out
```

```python
import functools
import math

import numpy as np
import jax
import jax.numpy as jnp
from jax import lax
from jax.experimental import pallas as pl
from jax.experimental.pallas import tpu as pltpu

F32 = jnp.float32
BF16 = jnp.bfloat16

D_MODEL = 1024
N_META = 16
EPS = 1e-6
ATT_HEADS = 4
ATT_QK_DIM = 64
ATT_V_DIM = 128
REL_BUCKETS = 32
REL_MAX_DIST = 128
CONV_K = 3
HGRN_HEADS = 4
HGRN_D = 128
N_EXPERTS = 8
IN_COLS = 8192

T_PAD = 128
PAD0 = T_PAD - N_META
ATT_TILE = 384
VT_ONES = 16
LOG2E = math.log2(math.e)
HGRN_CHUNK = 128
HGRN_SUB = 16
COL_TILE = 1024
GROUP_MEAN_WIDTH = 256
MOE_TILE = 512
COMBINE_TOKENS = 128
WIN_ALIGN = 16
COMBINE_WIN = 256
MASK = -1e30
VMEM_LIMIT = 56 * 1024 * 1024

BLK_Q, BLK_K, BLK_V = 0, 4, 8
BLK_RQ, BLK_RF, BLK_RI, BLK_RG = 24, 28, 32, 36
BLK_CB, BLK_CC, BLK_CH = 3, 4, 5
BLK_GATE = 5


def _row_tile(rows, target):
    n = rows // 128
    best = 1
    for d in range(1, n + 1):
        if n % d == 0 and d * 128 <= target:
            best = d
    return best * 128


def _params(sem, vmem=VMEM_LIMIT):
    return pltpu.CompilerParams(dimension_semantics=sem, vmem_limit_bytes=vmem)


def _inproj_kernel(x_ref, g_ref, w_ref, qkg_ref, gm_ref, o_ref, xn_ref):
    j = pl.program_id(1)

    @pl.when(j == 0)
    def _():
        x = x_ref[...]
        ms = jnp.mean(x * x, axis=-1, keepdims=True)
        xn_ref[...] = (x * lax.rsqrt(ms + EPS) * g_ref[...]).astype(BF16)

    acc = jnp.dot(xn_ref[...], w_ref[...], preferred_element_type=F32)

    @pl.when(j == 0)
    def _():
        sq = acc * acc
        hi = sq.astype(BF16)
        lo = (sq - hi.astype(F32)).astype(BF16)
        gm = gm_ref[...]
        width = gm.shape[0]
        ms = jnp.concatenate(
            [jnp.dot(hi[:, c:c + width], gm, preferred_element_type=F32)
             + jnp.dot(lo[:, c:c + width], gm, preferred_element_type=F32)
             for c in range(0, COL_TILE, width)], axis=1)
        o_ref[...] = (acc * lax.rsqrt(ms + EPS) * qkg_ref[...]).astype(BF16)

    @pl.when(j > 0)
    def _():
        o_ref[...] = acc.astype(BF16)


def _inproj(hs, gain, w_bf, qk_gain):
    rows = hs.shape[0]
    tm = _row_tile(rows, 1536)
    n_col = IN_COLS // COL_TILE
    assert COL_TILE == 4 * ATT_HEADS * ATT_QK_DIM
    grp = np.arange(GROUP_MEAN_WIDTH) // ATT_QK_DIM
    gm = jnp.asarray((grp[:, None] == grp[None, :]).astype(np.float32) / ATT_QK_DIM, BF16)
    return pl.pallas_call(
        _inproj_kernel,
        out_shape=jax.ShapeDtypeStruct((rows, IN_COLS), BF16),
        grid=(rows // tm, n_col),
        in_specs=[
            pl.BlockSpec((tm, D_MODEL), lambda i, j: (i, 0)),
            pl.BlockSpec((1, D_MODEL), lambda i, j: (0, 0)),
            pl.BlockSpec((D_MODEL, COL_TILE), lambda i, j: (0, j)),
            pl.BlockSpec((1, COL_TILE), lambda i, j: (0, 0)),
            pl.BlockSpec((GROUP_MEAN_WIDTH, GROUP_MEAN_WIDTH), lambda i, j: (0, 0)),
        ],
        out_specs=pl.BlockSpec((tm, COL_TILE), lambda i, j: (i, j)),
        scratch_shapes=[pltpu.VMEM((tm, D_MODEL), BF16)],
        compiler_params=_params(("parallel", "arbitrary")),
        name="inproj",
    )(hs, gain.reshape(1, D_MODEL), w_bf, qk_gain, gm)


def _rel_bucket_table(n_max):
    n = np.arange(n_max, dtype=np.int64)
    max_exact = REL_BUCKETS // 2
    nf = np.maximum(n, 1).astype(np.float32)
    large = max_exact + (np.log(nf / np.float32(max_exact)) / np.float32(math.log(REL_MAX_DIST / max_exact))
                         * np.float32(REL_BUCKETS - max_exact)).astype(np.int32)
    large = np.minimum(large, REL_BUCKETS - 1)
    return np.where(n < max_exact, n, large).astype(np.int32)


def _attn_bias_tables(rel_bias, t):
    bucket = _rel_bucket_table(2 * t)
    assert np.all(bucket[t + 1:] == REL_BUCKETS - 1) and np.all(np.diff(bucket) >= 0)
    first_dist = tuple(int(np.searchsorted(bucket, b, side="left")) for b in range(REL_BUCKETS))
    return pl.pallas_call(
        functools.partial(_bias_kernel, t=t, first_dist=first_dist),
        out_shape=jax.ShapeDtypeStruct((ATT_HEADS, 6, t, t), F32),
        grid=(ATT_HEADS,),
        in_specs=[pl.BlockSpec(memory_space=pltpu.SMEM)],
        out_specs=pl.BlockSpec((None, 6, t, t), lambda h: (h, 0, 0, 0)),
        compiler_params=_params(("parallel",)),
        name="attn_bias",
    )(rel_bias.astype(F32))


def _bias_kernel(rb_ref, o_ref, *, t, first_dist):
    h = pl.program_id(0)
    key = lax.broadcasted_iota(jnp.int32, (t, t), 0)
    qry = lax.broadcasted_iota(jnp.int32, (t, t), 1)
    far = rb_ref[REL_BUCKETS - 1, h]

    def table(n):
        val = jnp.full((t, t), rb_ref[0, h] - far, F32)
        for b in range(1, REL_BUCKETS):
            val = jnp.where(n >= first_dist[b], rb_ref[b, h] - far, val)
        return val * LOG2E

    n0 = qry - key
    diag = jnp.where(n0 >= 0, table(n0), MASK)
    near = table(n0 + t)
    zero = jnp.zeros((t, t), F32)
    for kind, tab in enumerate((diag, near, zero)):
        o_ref[kind] = tab
        o_ref[kind + 3] = jnp.where(key < PAD0, MASK, tab)


def _attn_kernel(lam_ref, q_ref, k_ref, v_ref, bt_ref, sg_ref, o_ref, m_ref, acc_ref, vt_ref,
                 *, t, out_scale):
    n_t = vt_ref.shape[0]
    for j in range(n_t):
        vt_ref[j, :ATT_V_DIM, :] = v_ref[j * t:(j + 1) * t, :].astype(F32).T.astype(BF16)
        vt_ref[j, ATT_V_DIM:, :] = jnp.ones((VT_ONES, t), BF16)
    lane = lax.broadcasted_iota(jnp.int32, (1, 2 * ATT_QK_DIM), 1)
    nt = (((1,), (1,)), ((), ()))

    def query_tile(i, carry):
        rows = pl.ds(pl.multiple_of(i * t, t), t)
        q = q_ref[rows, :]
        zero = jnp.zeros_like(q)
        q_maps = (jnp.where(lane < ATT_QK_DIM, q, zero), jnp.where(lane >= ATT_QK_DIM, q, zero))
        m_ref[...] = jnp.full(m_ref.shape, MASK, F32)
        acc_ref[...] = jnp.zeros(acc_ref.shape, F32)

        def scores(j):
            ks = k_ref[pl.ds(pl.multiple_of(j * t, t), t), :]
            bias = bt_ref[jnp.minimum(i - j, 2) + jnp.where(j == 0, 3, 0)]
            return tuple(lax.dot_general(ks, qm, nt, preferred_element_type=F32) + bias for qm in q_maps)

        def consume(j, s):
            vt = vt_ref[j]
            for mi in range(2):
                m_old = m_ref[mi]
                m_new = jnp.maximum(m_old, jnp.max(s[mi], axis=0, keepdims=True))
                m_ref[mi] = m_new
                p = jnp.exp2(s[mi] - m_new).astype(BF16)
                acc_ref[mi] = jnp.exp2(m_old - m_new) * acc_ref[mi] + jnp.dot(vt, p, preferred_element_type=F32)

        def step(j, s):
            nxt = scores(j + 1)
            consume(j, s)
            return nxt

        consume(i, lax.fori_loop(0, i, step, scores(0)))

        a1, a2 = acc_ref[0], acc_ref[1]
        o_t = (a1[:ATT_V_DIM] / a1[ATT_V_DIM:ATT_V_DIM + 1]
               - lam_ref[0] * (a2[:ATT_V_DIM] / a2[ATT_V_DIM:ATT_V_DIM + 1]))
        o = o_t.T
        ms = jnp.mean(o * o, axis=-1, keepdims=True)
        y = o * lax.rsqrt(ms + EPS) * (sg_ref[...] * out_scale)
        row = i * t + lax.broadcasted_iota(jnp.int32, (t, 1), 0)
        o_ref[rows, :] = jnp.where(row >= PAD0, y, 0.0).astype(BF16)
        return carry

    lax.fori_loop(0, n_t, query_tile, 0)


def _diff_attention(proj3, lam, sub_gain, btab, out_scale):
    bsz, tp, _ = proj3.shape
    t = ATT_TILE
    n_t = tp // t
    rows = ATT_V_DIM + VT_ONES
    kern = functools.partial(_attn_kernel, t=t, out_scale=out_scale)
    seq = lambda blk: pl.BlockSpec((None, tp, 128), lambda b, h: (b, 0, blk + h))
    return pl.pallas_call(
        kern,
        out_shape=jax.ShapeDtypeStruct((bsz, tp, ATT_HEADS * ATT_V_DIM), BF16),
        grid=(bsz, ATT_HEADS),
        in_specs=[
            pl.BlockSpec(memory_space=pltpu.SMEM),
            seq(BLK_Q), seq(BLK_K), seq(BLK_V),
            pl.BlockSpec((None, 6, t, t), lambda b, h: (h, 0, 0, 0)),
            pl.BlockSpec((1, ATT_V_DIM), lambda b, h: (0, 0)),
        ],
        out_specs=pl.BlockSpec((None, tp, 128), lambda b, h: (b, 0, h)),
        scratch_shapes=[pltpu.VMEM((2, 1, t), F32), pltpu.VMEM((2, rows, t), F32),
                        pltpu.VMEM((n_t, rows, t), BF16)],
        compiler_params=_params(("parallel", "parallel")),
        name="diff_attn",
    )(lam, proj3, proj3, proj3, btab, sub_gain.reshape(1, ATT_V_DIM))


def _split3(x):
    h1 = x.astype(BF16)
    r1 = x - h1.astype(F32)
    h2 = r1.astype(BF16)
    h3 = (r1 - h2.astype(F32)).astype(BF16)
    return h1, h2, h3


def _hgrn_kernel(q_ref, f_ref, i_ref, g_ref, la_ref, l1m_ref, og_ref, o_ref):
    c_len, sub = HGRN_CHUNK, HGRN_SUB
    half = sub // 2
    n_chunk = q_ref.shape[0] // c_len
    la, l1m, og = la_ref[...], l1m_ref[...], og_ref[...]
    rr = lax.broadcasted_iota(jnp.int32, (c_len, c_len), 0)
    cc = lax.broadcasted_iota(jnp.int32, (c_len, c_len), 1)
    tri = jnp.where(cc <= rr, 1.0, 0.0).astype(BF16)
    row8 = lax.broadcasted_iota(jnp.int32, (half, 1), 0)
    lane8 = lax.broadcasted_iota(jnp.int32, (half, c_len), 1)
    nt = (((1,), (1,)), ((), ()))

    def chunk(c, st):
        r0 = pl.multiple_of(c * c_len, c_len)
        rows = pl.ds(r0, c_len)
        z = f_ref[rows, :].astype(F32)
        qh = q_ref[rows, :].astype(F32)
        qh = qh * jax.nn.sigmoid(qh)
        v = i_ref[rows, :]
        gate = g_ref[rows, :].astype(F32)
        sp = jnp.log(1.0 + jnp.exp(-jnp.abs(z)))
        bb = l1m + jnp.minimum(z, 0.0) - sp
        log_f = jnp.maximum(la, bb) + jnp.log(1.0 + jnp.exp(-jnp.abs(la - bb)))
        valid = (r0 + lax.broadcasted_iota(jnp.int32, (c_len, 1), 0)) >= PAD0
        log_k = jnp.where(valid, l1m + jnp.minimum(-z, 0.0) - sp, -jnp.inf)
        g = sum(jnp.dot(tri, part, preferred_element_type=F32) for part in _split3(log_f))
        ck = log_k - g
        o_inter = lax.dot_general((qh * jnp.exp(g)).astype(BF16), st.astype(BF16), nt,
                                  preferred_element_type=F32)
        a_rows = []
        for a in range(c_len // sub):
            lo = a * sub
            ga = (g[lo:lo + half, :], g[lo + half:lo + sub, :])
            qa = (qh[lo:lo + half, :], qh[lo + half:lo + sub, :])
            if a == 0:
                blk = [jnp.zeros((half, c_len), F32)] * 2
            else:
                gs = g[lo - 1:lo, :]
                qd = (qh[lo:lo + sub, :] * jnp.exp(g[lo:lo + sub, :] - gs)).astype(BF16)
                kd = jnp.exp(jnp.minimum(gs - g[:lo, :], 0.0) + log_k[:lo, :]).astype(BF16)
                kd = jnp.concatenate([kd, jnp.zeros((c_len - lo, HGRN_D), BF16)], axis=0)
                a_off = lax.dot_general(qd, kd, nt, preferred_element_type=F32)
                blk = [a_off[:half, :], a_off[half:, :]]
            for s in range(sub):
                crow = ck[lo + s:lo + s + 1, :]
                for hh in range(s // half, 2):
                    col = jnp.sum(qa[hh] * jnp.exp(ga[hh] + crow), axis=-1, keepdims=True)
                    blk[hh] = jnp.where(lane8 == lo + s, col, blk[hh])
            for hh in range(2):
                a_rows.append(jnp.where(lane8 <= lo + hh * half + row8, blk[hh], 0.0))
        a_full = jnp.concatenate(a_rows, axis=0).astype(BF16)
        o = o_inter + jnp.dot(a_full, v, preferred_element_type=F32)
        g_last = g[c_len - 1:c_len, :]
        kd = jnp.exp(g_last - g + log_k).astype(BF16)
        st = st * jnp.exp(g_last) + lax.dot_general(v, kd, (((0,), (0,)), ((), ())),
                                                    preferred_element_type=F32)
        ms = jnp.mean(o * o, axis=-1, keepdims=True)
        y = o * lax.rsqrt(ms + EPS) * og * (gate * jax.nn.sigmoid(gate))
        o_ref[rows, :] = y.astype(BF16)
        return st

    lax.fori_loop(0, n_chunk, chunk, jnp.zeros((HGRN_D, HGRN_D), F32), unroll=3)


def _hgrn(proj3, log_lb, log1m_lb, out_gain):
    bsz, tp, _ = proj3.shape
    seq = lambda blk: pl.BlockSpec((None, tp, 128), lambda b, h: (b, 0, blk + h))
    chan = pl.BlockSpec((None, 1, HGRN_D), lambda b, h: (h, 0, 0))
    return pl.pallas_call(
        _hgrn_kernel,
        out_shape=jax.ShapeDtypeStruct((bsz, tp, HGRN_HEADS * HGRN_D), BF16),
        grid=(bsz, HGRN_HEADS),
        in_specs=[seq(BLK_RQ), seq(BLK_RF), seq(BLK_RI), seq(BLK_RG), chan, chan,
                  pl.BlockSpec((1, HGRN_D), lambda b, h: (0, 0))],
        out_specs=pl.BlockSpec((None, tp, 128), lambda b, h: (b, 0, h)),
        compiler_params=_params(("parallel", "parallel")),
        name="hgrn2",
    )(proj3, proj3, proj3, proj3,
      log_lb.reshape(HGRN_HEADS, 1, HGRN_D), log1m_lb.reshape(HGRN_HEADS, 1, HGRN_D),
      out_gain.reshape(1, HGRN_D))


def _merge_kernel(hs_ref, ua_ref, ur_ref, cb_ref, cc_ref, ch_ref, pc_ref, ph_ref,
                  g0_ref, g1_ref, g2_ref, cw_ref, wb_ref, wo_ref, o_ref, *, tm, tiles_per_seq):
    i = pl.program_id(0)
    row = (i % tiles_per_seq) * tm + lax.broadcasted_iota(jnp.int32, (tm, 1), 0)
    valid = row >= PAD0
    z = jnp.where(valid, cc_ref[...].astype(F32) * ch_ref[...].astype(F32), 0.0)
    halo_row = (i % tiles_per_seq) * tm - 8 + lax.broadcasted_iota(jnp.int32, (8, 1), 0)
    zp = jnp.where(halo_row >= PAD0, pc_ref[...].astype(F32) * ph_ref[...].astype(F32), 0.0)
    zz = jnp.concatenate([zp, z], axis=0)
    cw = cw_ref[...]
    y = (cw[2:3, :] * z + cw[1:2, :] * zz[7:7 + tm, :] + cw[0:1, :] * zz[6:6 + tm, :])
    u_conv = jnp.where(valid, cb_ref[...].astype(F32) * y, 0.0).astype(BF16)
    mixed = jnp.zeros((tm, D_MODEL), F32)
    for n, (u, g_ref) in enumerate(((ua_ref[...], g0_ref), (u_conv, g1_ref), (ur_ref[...], g2_ref))):
        up = jnp.dot(u, wb_ref[n], preferred_element_type=F32)
        mixed = mixed + jax.nn.sigmoid(g_ref[...].astype(F32)) * up
    o_ref[...] = hs_ref[...] + jnp.dot(mixed.astype(BF16), wo_ref[...], preferred_element_type=F32)


def _merge(hs, u_att, u_hgrn, proj, conv_w, wb_bf, wo_bf, tp):
    rows = hs.shape[0]
    tm = _row_tile(tp, 384)
    kern = functools.partial(_merge_kernel, tm=tm, tiles_per_seq=tp // tm)
    row_blk = lambda w, blk: pl.BlockSpec((tm, w), lambda i: (i, blk))
    halo = lambda blk: pl.BlockSpec((8, 512), lambda i: (jnp.maximum(i * (tm // 8) - 1, 0), blk))
    const = lambda shape: pl.BlockSpec(shape, lambda i: (0,) * len(shape))
    return pl.pallas_call(
        kern,
        out_shape=jax.ShapeDtypeStruct((rows, D_MODEL), F32),
        grid=(rows // tm,),
        in_specs=[row_blk(D_MODEL, 0), row_blk(512, 0), row_blk(512, 0),
                  row_blk(512, BLK_CB), row_blk(512, BLK_CC), row_blk(512, BLK_CH),
                  halo(BLK_CC), halo(BLK_CH),
                  row_blk(1024, BLK_GATE), row_blk(1024, BLK_GATE + 1), row_blk(1024, BLK_GATE + 2),
                  const((CONV_K, 512)), const((3, 512, D_MODEL)), const((D_MODEL, D_MODEL))],
        out_specs=row_blk(D_MODEL, 0),
        compiler_params=_params(("parallel",)),
        name="merge",
    )(hs, u_att, u_hgrn, proj, proj, proj, proj, proj, proj, proj, proj, conv_w, wb_bf, wo_bf)


def _ffn_kernel(hs_ref, g_ref, wg_ref, wu_ref, wd_ref, o_ref):
    x = hs_ref[...]
    ms = jnp.mean(x * x, axis=-1, keepdims=True)
    hn = (x * lax.rsqrt(ms + EPS) * g_ref[...]).astype(BF16)
    a = jnp.dot(hn, wg_ref[...], preferred_element_type=F32)
    u = jnp.dot(hn, wu_ref[...], preferred_element_type=F32)
    h = (a * jax.nn.sigmoid(a) * u).astype(BF16)
    o_ref[...] = x + jnp.dot(h, wd_ref[...], preferred_element_type=F32)


def _ffn(hs, gain, wg_bf, wu_bf, wd_bf):
    rows = hs.shape[0]
    d_ff = wg_bf.shape[1]
    tm = _row_tile(rows, 384)
    const = lambda shape: pl.BlockSpec(shape, lambda i: (0, 0))
    return pl.pallas_call(
        _ffn_kernel,
        out_shape=jax.ShapeDtypeStruct((rows, D_MODEL), F32),
        grid=(rows // tm,),
        in_specs=[pl.BlockSpec((tm, D_MODEL), lambda i: (i, 0)), const((1, D_MODEL)),
                  const((D_MODEL, d_ff)), const((D_MODEL, d_ff)), const((d_ff, D_MODEL))],
        out_specs=pl.BlockSpec((tm, D_MODEL), lambda i: (i, 0)),
        compiler_params=_params(("parallel",)),
        name="ffn_dense",
    )(hs, gain.reshape(1, D_MODEL), wg_bf, wu_bf, wd_bf)


def _router_kernel(hs_ref, g_ref, rw_ref, idx_ref, wt_ref, cnt_ref, carry_ref, *, tm, tiles_per_seq):
    i = pl.program_id(0)

    @pl.when(i == 0)
    def _():
        carry_ref[...] = jnp.zeros_like(carry_ref)

    x = hs_ref[...]
    ms = jnp.mean(x * x, axis=-1, keepdims=True)
    hn = x * lax.rsqrt(ms + EPS) * g_ref[...]
    logits = jnp.dot(hn, rw_ref[...], preferred_element_type=F32, precision=lax.Precision.HIGHEST)
    lane = lax.broadcasted_iota(jnp.int32, logits.shape, 1)
    lane_f = lane.astype(F32)
    logits = jnp.where(lane < N_EXPERTS, logits, -jnp.inf)
    m1 = jnp.max(logits, axis=-1, keepdims=True)
    i1 = jnp.min(jnp.where(logits == m1, lane_f, 128.0), axis=-1, keepdims=True)
    rest = jnp.where(lane_f == i1, -jnp.inf, logits)
    m2 = jnp.max(rest, axis=-1, keepdims=True)
    i2 = jnp.min(jnp.where(rest == m2, lane_f, 128.0), axis=-1, keepdims=True)
    e2 = jnp.exp(m2 - m1)
    w1 = 1.0 / (1.0 + e2)
    w2 = e2 / (1.0 + e2)
    row = (i % tiles_per_seq) * tm + lax.broadcasted_iota(jnp.int32, (tm, 1), 0)
    real = row >= T_PAD
    hot1 = jnp.where(jnp.logical_and(real, lane_f == i1), 1.0, 0.0)
    hot2 = jnp.where(jnp.logical_and(real, lane_f == i2), 1.0, 0.0)
    both = hot1 + hot2
    rr = lax.broadcasted_iota(jnp.int32, (tm, tm), 0)
    cc = lax.broadcasted_iota(jnp.int32, (tm, tm), 1)
    earlier = jnp.where(cc < rr, 1.0, 0.0).astype(BF16)
    before = carry_ref[...] + jnp.dot(earlier, both.astype(BF16), preferred_element_type=F32)
    r1 = jnp.sum(before * hot1, axis=-1, keepdims=True)
    r2 = jnp.sum(before * hot2, axis=-1, keepdims=True)
    carry_ref[...] += jnp.sum(both, axis=0, keepdims=True)
    cnt_ref[...] = carry_ref[...]
    packed = jnp.where(lane == 0, i1, jnp.where(lane == 1, i2, jnp.where(lane == 2, r1, jnp.where(lane == 3, r2, 0.0))))
    idx_ref[...] = packed.astype(jnp.int32)
    wt_ref[...] = jnp.where(lane == 0, w1, jnp.where(lane == 1, w2, 0.0))


def _router(hs, gain, router_w, tp):
    rows = hs.shape[0]
    tm = _row_tile(tp, 384)
    rw = jnp.zeros((D_MODEL, 128), F32).at[:, :N_EXPERTS].set(router_w.astype(F32))
    blk = pl.BlockSpec((tm, 128), lambda i: (i, 0))
    return pl.pallas_call(
        functools.partial(_router_kernel, tm=tm, tiles_per_seq=tp // tm),
        out_shape=(jax.ShapeDtypeStruct((rows, 128), jnp.int32), jax.ShapeDtypeStruct((rows, 128), F32),
                   jax.ShapeDtypeStruct((1, 128), F32)),
        grid=(rows // tm,),
        in_specs=[pl.BlockSpec((tm, D_MODEL), lambda i: (i, 0)),
                  pl.BlockSpec((1, D_MODEL), lambda i: (0, 0)),
                  pl.BlockSpec((D_MODEL, 128), lambda i: (0, 0))],
        out_specs=(blk, blk, pl.BlockSpec((1, 128), lambda i: (0, 0))),
        scratch_shapes=[pltpu.VMEM((1, 128), F32)],
        compiler_params=_params(("arbitrary",)),
        name="moe_router",
    )(hs, gain.reshape(1, D_MODEL), rw)


def _dispatch_kernel(s1_ref, s2_ref, zr_ref, hs_ref, g_ref, xs_ref, rows_ref, zero_ref, sem, zsem,
                     *, td, tg, tiles_per_seq, n_steps):
    n = pl.program_id(0) * tiles_per_seq + pl.program_id(1)
    buf = n % 2
    base = n * td

    def row_copy(r, which):
        slot = (s1_ref, s2_ref)[which][base + r]
        return pltpu.make_async_copy(rows_ref.at[buf, pl.ds(r, 1)], xs_ref.at[pl.ds(slot, 1)], sem.at[buf])

    def wait_rows(b):
        def body(r, c):
            pltpu.make_async_copy(rows_ref.at[b, pl.ds(0, 1)], xs_ref.at[pl.ds(0, 1)], sem.at[b]).wait()
            return c
        lax.fori_loop(0, 2 * td, body, 0)

    @pl.when(n == 0)
    def _():
        zero_ref[...] = jnp.zeros_like(zero_ref)
        for e in range(2 * N_EXPERTS):
            fill = pltpu.make_async_copy(zero_ref, xs_ref.at[pl.ds(pl.multiple_of(zr_ref[e], tg), tg)], zsem)
            fill.start()
            fill.wait()

    @pl.when(n >= 2)
    def _():
        wait_rows(buf)

    x = hs_ref[...]
    ms = jnp.mean(x * x, axis=-1, keepdims=True)
    rows_ref[buf] = x * lax.rsqrt(ms + EPS) * g_ref[...]

    def issue(r, c):
        row_copy(r, 0).start()
        row_copy(r, 1).start()
        return c

    lax.fori_loop(0, td, issue, 0)

    @pl.when(n == n_steps - 1)
    def _():
        wait_rows(buf)
        if n_steps > 1:
            wait_rows(1 - buf)


def _dispatch(hs3, gain, slot1, slot2, zero_rows, n_slots, tg):
    bsz, tp, _ = hs3.shape
    td = T_PAD
    tiles_per_seq = (tp - T_PAD) // td
    kern = functools.partial(_dispatch_kernel, td=td, tg=tg, tiles_per_seq=tiles_per_seq,
                             n_steps=bsz * tiles_per_seq)
    return pl.pallas_call(
        kern,
        out_shape=jax.ShapeDtypeStruct((n_slots, D_MODEL), F32),
        grid_spec=pltpu.PrefetchScalarGridSpec(
            num_scalar_prefetch=3,
            grid=(bsz, tiles_per_seq),
            in_specs=[pl.BlockSpec((None, td, D_MODEL), lambda b, i, *_: (b, i + 1, 0)),
                      pl.BlockSpec((1, D_MODEL), lambda b, i, *_: (0, 0))],
            out_specs=pl.BlockSpec(memory_space=pl.ANY),
            scratch_shapes=[pltpu.VMEM((2, td, D_MODEL), F32), pltpu.VMEM((tg, D_MODEL), F32),
                            pltpu.SemaphoreType.DMA((2,)), pltpu.SemaphoreType.DMA],
        ),
        compiler_params=_params(("arbitrary", "arbitrary")),
        name="moe_dispatch",
    )(slot1, slot2, zero_rows, hs3, gain.reshape(1, D_MODEL))


def _expert_kernel(te_ref, nt_ref, x_ref, wg_ref, wu_ref, wd_ref, o_ref, hn_ref, acc_ref, *, n_f):
    t = pl.program_id(0)
    f = pl.program_id(1)

    @pl.when(t < nt_ref[0])
    def _():
        @pl.when(f == 0)
        def _():
            hn_ref[...] = x_ref[...].astype(BF16)

        hn = hn_ref[...]
        a = jnp.dot(hn, wg_ref[...], preferred_element_type=F32)
        u = jnp.dot(hn, wu_ref[...], preferred_element_type=F32)
        h = (a * jax.nn.sigmoid(a) * u).astype(BF16)
        y = jnp.dot(h, wd_ref[...], preferred_element_type=F32)

        @pl.when(f == 0)
        def _():
            acc_ref[...] = y

        @pl.when(f > 0)
        def _():
            acc_ref[...] += y

    @pl.when(jnp.logical_and(t >= nt_ref[0], f == 0))
    def _():
        acc_ref[...] = jnp.zeros_like(acc_ref)

    @pl.when(f == n_f - 1)
    def _():
        o_ref[...] = acc_ref[...].astype(BF16)


def _experts(xs, tile_expert, n_tiles_used, wg_bf, wu_bf, wd_bf, tg):
    slots = xs.shape[0]
    d_ff = wg_bf.shape[2]
    n_f = 2
    tf = d_ff // n_f

    def x_map(t, f, te, nt):
        return (jnp.minimum(t, nt[0] - 1), 0)

    def f_of(t, f, nt):
        return jnp.where(t < nt[0], f, n_f - 1)

    return pl.pallas_call(
        functools.partial(_expert_kernel, n_f=n_f),
        out_shape=jax.ShapeDtypeStruct((slots, D_MODEL), BF16),
        grid_spec=pltpu.PrefetchScalarGridSpec(
            num_scalar_prefetch=2,
            grid=(slots // tg, n_f),
            in_specs=[
                pl.BlockSpec((tg, D_MODEL), x_map),
                pl.BlockSpec((None, D_MODEL, tf), lambda t, f, te, nt: (te[t], 0, f_of(t, f, nt))),
                pl.BlockSpec((None, D_MODEL, tf), lambda t, f, te, nt: (te[t], 0, f_of(t, f, nt))),
                pl.BlockSpec((None, tf, D_MODEL), lambda t, f, te, nt: (te[t], f_of(t, f, nt), 0)),
            ],
            out_specs=pl.BlockSpec((tg, D_MODEL), lambda t, f, te, nt: (t, 0)),
            scratch_shapes=[pltpu.VMEM((tg, D_MODEL), BF16), pltpu.VMEM((tg, D_MODEL), F32)],
        ),
        compiler_params=_params(("arbitrary", "arbitrary")),
        name="moe_experts",
    )(tile_expert, n_tiles_used, xs, wg_bf, wu_bf, wd_bf)


def _combine_kernel(ws_ref, hs_ref, route_ref, wt_ref, y_ref, o_ref, win_ref, sem, *, tc, tiles_per_seq, n_steps):
    n = pl.program_id(0) * tiles_per_seq + pl.program_id(1)
    cur = n % 2

    def window_copy(step, b, e):
        start = pl.multiple_of(ws_ref[step * N_EXPERTS + e], WIN_ALIGN)
        return pltpu.make_async_copy(y_ref.at[pl.ds(start, COMBINE_WIN)], win_ref.at[b, e], sem.at[b])

    def fetch(step, b):
        for e in range(N_EXPERTS):
            window_copy(step, b, e).start()

    @pl.when(n == 0)
    def _():
        fetch(0, 0)

    @pl.when(n + 1 < n_steps)
    def _():
        fetch(n + 1, 1 - cur)

    for e in range(N_EXPERTS):
        window_copy(n, cur, e).wait()

    route = route_ref[...]
    e1, e2, s1, s2 = (route[:, c:c + 1] for c in range(4))
    wt = wt_ref[...]
    w1, w2 = wt[:, 0:1], wt[:, 1:2]
    pos = lax.broadcasted_iota(jnp.int32, (1, COMBINE_WIN), 1)
    acc = hs_ref[...]
    for e in range(N_EXPERTS):
        start = ws_ref[n * N_EXPERTS + e]
        col = jnp.where(e1 == e, s1 - start, jnp.where(e2 == e, s2 - start, -1))
        pick = jnp.where(col == pos, 1.0, 0.0).astype(BF16)
        w = jnp.where(e1 == e, w1, jnp.where(e2 == e, w2, 0.0))
        acc = acc + w * jnp.dot(pick, win_ref[cur, e], preferred_element_type=F32)
    o_ref[...] = acc


def _combine(hs3, route, wts2, y, win_start, seq):
    bsz = hs3.shape[0]
    tc = COMBINE_TOKENS
    tiles_per_seq = seq // tc
    kern = functools.partial(_combine_kernel, tc=tc, tiles_per_seq=tiles_per_seq, n_steps=bsz * tiles_per_seq)
    tok = lambda w: pl.BlockSpec((tc, w), lambda b, i, ws: (b * tiles_per_seq + i, 0))
    return pl.pallas_call(
        kern,
        out_shape=jax.ShapeDtypeStruct((bsz, seq, D_MODEL), F32),
        grid_spec=pltpu.PrefetchScalarGridSpec(
            num_scalar_prefetch=1,
            grid=(bsz, tiles_per_seq),
            in_specs=[
                pl.BlockSpec((None, tc, D_MODEL), lambda b, i, ws: (b, i + T_PAD // tc, 0)),
                tok(4), tok(2),
                pl.BlockSpec(memory_space=pl.ANY),
            ],
            out_specs=pl.BlockSpec((None, tc, D_MODEL), lambda b, i, ws: (b, i, 0)),
            scratch_shapes=[pltpu.VMEM((2, N_EXPERTS, COMBINE_WIN, D_MODEL), BF16),
                            pltpu.SemaphoreType.DMA((2,))],
        ),
        compiler_params=_params(("arbitrary", "arbitrary")),
        name="moe_combine",
    )(win_start, hs3, route, wts2, y)


def _moe(hs, gain, router_w, wg_bf, wu_bf, wd_bf, bsz, tp):
    seq = tp - T_PAD
    n_tok = bsz * seq
    tg = MOE_TILE
    idx, wts, cnt = _router(hs, gain, router_w, tp)
    sel = idx.reshape(bsz, tp, 128)[:, T_PAD:, :4].reshape(n_tok, 4)
    counts = cnt[0, :N_EXPERTS].astype(jnp.int32)
    padded = ((counts + tg - 1) // tg) * tg
    ends = jnp.cumsum(padded)
    starts = ends - padded
    experts = jnp.arange(N_EXPERTS, dtype=jnp.int32)[None, :]
    slot1 = jnp.sum(jnp.where(sel[:, 0:1] == experts, starts[None, :], 0), axis=1) + sel[:, 2]
    slot2 = jnp.sum(jnp.where(sel[:, 1:2] == experts, starts[None, :], 0), axis=1) + sel[:, 3]
    slot1, slot2 = slot1.astype(jnp.int32), slot2.astype(jnp.int32)
    n_slots = 2 * n_tok + N_EXPERTS * tg
    tile_start = jnp.arange(n_slots // tg, dtype=jnp.int32)[:, None] * tg
    tile_expert = jnp.minimum(jnp.sum((tile_start >= ends[None, :]).astype(jnp.int32), axis=1), N_EXPERTS - 1)
    n_tiles_used = (ends[-1:] // tg).astype(jnp.int32)
    tail = n_slots - tg * (1 + jnp.arange(N_EXPERTS, dtype=jnp.int32))
    zero_rows = jnp.concatenate([jnp.maximum(ends - tg, 0), tail]).astype(jnp.int32)

    tc = COMBINE_TOKENS
    first = jnp.minimum(
        jnp.min(jnp.where(sel[:, 0:1] == experts, slot1[:, None], n_slots).reshape(n_tok // tc, tc, N_EXPERTS), axis=1),
        jnp.min(jnp.where(sel[:, 1:2] == experts, slot2[:, None], n_slots).reshape(n_tok // tc, tc, N_EXPERTS), axis=1))
    win_start = jnp.where(first == n_slots, 0,
                          jnp.minimum(first // WIN_ALIGN * WIN_ALIGN, n_slots - COMBINE_WIN))
    route = jnp.concatenate([sel[:, :2], slot1[:, None], slot2[:, None]], axis=1)
    wts2 = wts.reshape(bsz, tp, 128)[:, T_PAD:, :2].reshape(n_tok, 2)

    hs3 = hs.reshape(bsz, tp, D_MODEL)
    xs = _dispatch(hs3, gain, slot1, slot2, zero_rows, n_slots, tg)
    y = _experts(xs, tile_expert.astype(jnp.int32), n_tiles_used, wg_bf, wu_bf, wd_bf, tg)
    return _combine(hs3, route, wts2, y, win_start.reshape(-1).astype(jnp.int32), seq)


def _permute_qk_cols(w):
    return w.reshape(D_MODEL, 2, ATT_HEADS, ATT_QK_DIM).transpose(0, 2, 1, 3).reshape(D_MODEL, 512)


def kernel(x, meta_tokens, norm1_gain, norm2_gain, w_in, q_norm_gain, k_norm_gain, diff_lambda,
           attn_sub_gain, rel_bias, conv_w, hgrn_lb_logits, hgrn_out_gain, w_branch, w_out,
           ffn_w_gate, ffn_w_up, ffn_w_down, router_w, moe_w_gate, moe_w_up, moe_w_down):
    bsz, seq, _ = x.shape
    depth = w_in.shape[0]
    tp = T_PAD + seq
    assert tp % ATT_TILE == 0 and depth == 2

    head = jnp.concatenate([jnp.zeros((PAD0, D_MODEL), x.dtype), meta_tokens.astype(x.dtype)], axis=0)
    hs = jnp.concatenate([jnp.broadcast_to(head[None], (bsz, T_PAD, D_MODEL)), x], axis=1)
    hs = hs.reshape(bsz * tp, D_MODEL)

    lb_all = jnp.cumsum(jax.nn.softmax(hgrn_lb_logits.astype(F32), axis=0), axis=0)
    lb_all = lb_all - lb_all[0]
    btab = _attn_bias_tables(rel_bias, ATT_TILE)

    out = None
    for layer in range(depth):
        w = w_in[layer]
        w_bf = jnp.concatenate([_permute_qk_cols(w[:, :512]), _permute_qk_cols(w[:, 512:1024]), w[:, 1024:]],
                               axis=1).astype(BF16)
        qk_gain = jnp.concatenate([jnp.tile(q_norm_gain[layer].astype(F32), 8) * (ATT_QK_DIM ** -0.5 * LOG2E),
                                   jnp.tile(k_norm_gain[layer].astype(F32), 8)]).reshape(1, COL_TILE)
        proj = _inproj(hs, norm1_gain[layer], w_bf, qk_gain)
        proj3 = proj.reshape(bsz, tp, IN_COLS)

        lam_init = 0.8 - 0.6 * math.exp(-0.3 * layer)
        lp = diff_lambda[layer].astype(F32)
        lam = jnp.exp(jnp.sum(lp[0] * lp[1])) - jnp.exp(jnp.sum(lp[2] * lp[3])) + lam_init
        u_att = _diff_attention(proj3, lam.reshape(1), attn_sub_gain[layer].astype(F32), btab,
                                1.0 - lam_init)

        lb = lb_all[layer]
        u_hgrn = _hgrn(proj3, jnp.log(lb), jnp.log1p(-lb), hgrn_out_gain[layer].astype(F32))

        hs = _merge(hs, u_att.reshape(bsz * tp, 512), u_hgrn.reshape(bsz * tp, 512), proj,
                    conv_w[layer].astype(F32), w_branch[layer].astype(BF16), w_out[layer].astype(BF16), tp)

        j = layer // 2
        if layer % 2 == 0:
            hs = _ffn(hs, norm2_gain[layer], ffn_w_gate[j].astype(BF16), ffn_w_up[j].astype(BF16),
                      ffn_w_down[j].astype(BF16))
        else:
            out = _moe(hs, norm2_gain[layer], router_w[j], moe_w_gate[j].astype(BF16),
                       moe_w_up[j].astype(BF16), moe_w_down[j].astype(BF16), bsz, tp)
    return out
```

```python
import functools
import math

import numpy as np
import jax
import jax.numpy as jnp
from jax import lax
from jax.experimental import pallas as pl
from jax.experimental.pallas import tpu as pltpu

F32 = jnp.float32
BF16 = jnp.bfloat16

D_MODEL = 1024
N_META = 16
EPS = 1e-6
ATT_HEADS = 4
ATT_QK_DIM = 64
ATT_V_DIM = 128
REL_BUCKETS = 32
REL_MAX_DIST = 128
CONV_K = 3
HGRN_HEADS = 4
HGRN_D = 128
N_EXPERTS = 8
IN_COLS = 8192

T_PAD = 128
PAD0 = T_PAD - N_META
ATT_TILE = 384
VT_ONES = 16
LOG2E = math.log2(math.e)
HGRN_CHUNK = 128
HGRN_SUB = 16
COL_TILE = 1024
GROUP_MEAN_WIDTH = 256
MOE_TILE = 512
XS_EXTRA = 128
COMBINE_TOKENS = 128
WIN_ALIGN = 16
COMBINE_WIN = 256
MASK = -1e30
VMEM_LIMIT = 56 * 1024 * 1024

BLK_Q, BLK_K, BLK_V = 0, 4, 8
BLK_RQ, BLK_RF, BLK_RI, BLK_RG = 24, 28, 32, 36
BLK_CB, BLK_CC, BLK_CH = 3, 4, 5
BLK_GATE = 5


def _row_tile(rows, target):
    n = rows // 128
    best = 1
    for d in range(1, n + 1):
        if n % d == 0 and d * 128 <= target:
            best = d
    return best * 128


def _params(sem, vmem=VMEM_LIMIT):
    return pltpu.CompilerParams(dimension_semantics=sem, vmem_limit_bytes=vmem)


def _inproj_kernel(x_ref, g_ref, w_ref, qkg_ref, gm_ref, o_ref, xn_ref):
    j = pl.program_id(1)

    @pl.when(j == 0)
    def _():
        x = x_ref[...]
        ms = jnp.mean(x * x, axis=-1, keepdims=True)
        xn_ref[...] = (x * lax.rsqrt(ms + EPS) * g_ref[...]).astype(BF16)

    acc = jnp.dot(xn_ref[...], w_ref[...], preferred_element_type=F32)

    @pl.when(j == 0)
    def _():
        sq = acc * acc
        hi = sq.astype(BF16)
        lo = (sq - hi.astype(F32)).astype(BF16)
        gm = gm_ref[...]
        width = gm.shape[0]
        ms = jnp.concatenate(
            [jnp.dot(hi[:, c:c + width], gm, preferred_element_type=F32)
             + jnp.dot(lo[:, c:c + width], gm, preferred_element_type=F32)
             for c in range(0, COL_TILE, width)], axis=1)
        o_ref[...] = (acc * lax.rsqrt(ms + EPS) * qkg_ref[...]).astype(BF16)

    @pl.when(j > 0)
    def _():
        o_ref[...] = acc.astype(BF16)


def _inproj(hs, gain, w_bf, qk_gain):
    rows = hs.shape[0]
    tm = _row_tile(rows, 1536)
    n_col = IN_COLS // COL_TILE
    assert COL_TILE == 4 * ATT_HEADS * ATT_QK_DIM
    grp = np.arange(GROUP_MEAN_WIDTH) // ATT_QK_DIM
    gm = jnp.asarray((grp[:, None] == grp[None, :]).astype(np.float32) / ATT_QK_DIM, BF16)
    return pl.pallas_call(
        _inproj_kernel,
        out_shape=jax.ShapeDtypeStruct((rows, IN_COLS), BF16),
        grid=(rows // tm, n_col),
        in_specs=[
            pl.BlockSpec((tm, D_MODEL), lambda i, j: (i, 0)),
            pl.BlockSpec((1, D_MODEL), lambda i, j: (0, 0)),
            pl.BlockSpec((D_MODEL, COL_TILE), lambda i, j: (0, j)),
            pl.BlockSpec((1, COL_TILE), lambda i, j: (0, 0)),
            pl.BlockSpec((GROUP_MEAN_WIDTH, GROUP_MEAN_WIDTH), lambda i, j: (0, 0)),
        ],
        out_specs=pl.BlockSpec((tm, COL_TILE), lambda i, j: (i, j)),
        scratch_shapes=[pltpu.VMEM((tm, D_MODEL), BF16)],
        compiler_params=_params(("parallel", "arbitrary")),
        name="inproj",
    )(hs, gain.reshape(1, D_MODEL), w_bf, qk_gain, gm)


def _rel_bucket_table(n_max):
    n = np.arange(n_max, dtype=np.int64)
    max_exact = REL_BUCKETS // 2
    nf = np.maximum(n, 1).astype(np.float32)
    large = max_exact + (np.log(nf / np.float32(max_exact)) / np.float32(math.log(REL_MAX_DIST / max_exact))
                         * np.float32(REL_BUCKETS - max_exact)).astype(np.int32)
    large = np.minimum(large, REL_BUCKETS - 1)
    return np.where(n < max_exact, n, large).astype(np.int32)


def _attn_bias_tables(rel_bias, t):
    bucket = _rel_bucket_table(2 * t)
    assert np.all(bucket[t + 1:] == REL_BUCKETS - 1) and np.all(np.diff(bucket) >= 0)
    first_dist = tuple(int(np.searchsorted(bucket, b, side="left")) for b in range(REL_BUCKETS))
    return pl.pallas_call(
        functools.partial(_bias_kernel, t=t, first_dist=first_dist),
        out_shape=jax.ShapeDtypeStruct((ATT_HEADS, 6, t, t), F32),
        grid=(ATT_HEADS,),
        in_specs=[pl.BlockSpec(memory_space=pltpu.SMEM)],
        out_specs=pl.BlockSpec((None, 6, t, t), lambda h: (h, 0, 0, 0)),
        compiler_params=_params(("parallel",)),
        name="attn_bias",
    )(rel_bias.astype(F32))


def _bias_kernel(rb_ref, o_ref, *, t, first_dist):
    h = pl.program_id(0)
    key = lax.broadcasted_iota(jnp.int32, (t, t), 0)
    qry = lax.broadcasted_iota(jnp.int32, (t, t), 1)
    far = rb_ref[REL_BUCKETS - 1, h]

    def table(n):
        val = jnp.full((t, t), rb_ref[0, h] - far, F32)
        for b in range(1, REL_BUCKETS):
            val = jnp.where(n >= first_dist[b], rb_ref[b, h] - far, val)
        return val * LOG2E

    n0 = qry - key
    diag = jnp.where(n0 >= 0, table(n0), MASK)
    near = table(n0 + t)
    zero = jnp.zeros((t, t), F32)
    for kind, tab in enumerate((diag, near, zero)):
        o_ref[kind] = tab
        o_ref[kind + 3] = jnp.where(key < PAD0, MASK, tab)


def _attn_kernel(lam_ref, q_ref, k_ref, v_ref, bt_ref, sg_ref, o_ref, m_ref, acc_ref, vt_ref,
                 *, t, out_scale):
    n_t = vt_ref.shape[0]
    for j in range(n_t):
        vt_ref[j, :ATT_V_DIM, :] = v_ref[j * t:(j + 1) * t, :].astype(F32).T.astype(BF16)
        vt_ref[j, ATT_V_DIM:, :] = jnp.ones((VT_ONES, t), BF16)
    lane = lax.broadcasted_iota(jnp.int32, (1, 2 * ATT_QK_DIM), 1)
    nt = (((1,), (1,)), ((), ()))

    def query_tile(i, carry):
        rows = pl.ds(pl.multiple_of(i * t, t), t)
        q = q_ref[rows, :]
        zero = jnp.zeros_like(q)
        q_maps = (jnp.where(lane < ATT_QK_DIM, q, zero), jnp.where(lane >= ATT_QK_DIM, q, zero))
        m_ref[...] = jnp.full(m_ref.shape, MASK, F32)
        acc_ref[...] = jnp.zeros(acc_ref.shape, F32)

        def scores(j):
            ks = k_ref[pl.ds(pl.multiple_of(j * t, t), t), :]
            bias = bt_ref[jnp.minimum(i - j, 2) + jnp.where(j == 0, 3, 0)]
            return tuple(lax.dot_general(ks, qm, nt, preferred_element_type=F32) + bias for qm in q_maps)

        def consume(j, s):
            vt = vt_ref[j]
            for mi in range(2):
                m_old = m_ref[mi]
                m_new = jnp.maximum(m_old, jnp.max(s[mi], axis=0, keepdims=True))
                m_ref[mi] = m_new
                p = jnp.exp2(s[mi] - m_new).astype(BF16)
                acc_ref[mi] = jnp.exp2(m_old - m_new) * acc_ref[mi] + jnp.dot(vt, p, preferred_element_type=F32)

        def step(j, s):
            nxt = scores(j + 1)
            consume(j, s)
            return nxt

        consume(i, lax.fori_loop(0, i, step, scores(0)))

        a1, a2 = acc_ref[0], acc_ref[1]
        o_t = (a1[:ATT_V_DIM] / a1[ATT_V_DIM:ATT_V_DIM + 1]
               - lam_ref[0] * (a2[:ATT_V_DIM] / a2[ATT_V_DIM:ATT_V_DIM + 1]))
        o = o_t.T
        ms = jnp.mean(o * o, axis=-1, keepdims=True)
        y = o * lax.rsqrt(ms + EPS) * (sg_ref[...] * out_scale)
        row = i * t + lax.broadcasted_iota(jnp.int32, (t, 1), 0)
        o_ref[rows, :] = jnp.where(row >= PAD0, y, 0.0).astype(BF16)
        return carry

    lax.fori_loop(0, n_t, query_tile, 0)


def _diff_attention(proj3, lam, sub_gain, btab, out_scale):
    bsz, tp, _ = proj3.shape
    t = ATT_TILE
    n_t = tp // t
    rows = ATT_V_DIM + VT_ONES
    kern = functools.partial(_attn_kernel, t=t, out_scale=out_scale)
    seq = lambda blk: pl.BlockSpec((None, tp, 128), lambda b, h: (b, 0, blk + h))
    return pl.pallas_call(
        kern,
        out_shape=jax.ShapeDtypeStruct((bsz, tp, ATT_HEADS * ATT_V_DIM), BF16),
        grid=(bsz, ATT_HEADS),
        in_specs=[
            pl.BlockSpec(memory_space=pltpu.SMEM),
            seq(BLK_Q), seq(BLK_K), seq(BLK_V),
            pl.BlockSpec((None, 6, t, t), lambda b, h: (h, 0, 0, 0)),
            pl.BlockSpec((1, ATT_V_DIM), lambda b, h: (0, 0)),
        ],
        out_specs=pl.BlockSpec((None, tp, 128), lambda b, h: (b, 0, h)),
        scratch_shapes=[pltpu.VMEM((2, 1, t), F32), pltpu.VMEM((2, rows, t), F32),
                        pltpu.VMEM((n_t, rows, t), BF16)],
        compiler_params=_params(("parallel", "parallel")),
        name="diff_attn",
    )(lam, proj3, proj3, proj3, btab, sub_gain.reshape(1, ATT_V_DIM))


def _split3(x):
    h1 = x.astype(BF16)
    r1 = x - h1.astype(F32)
    h2 = r1.astype(BF16)
    h3 = (r1 - h2.astype(F32)).astype(BF16)
    return h1, h2, h3


def _hgrn_kernel(q_ref, f_ref, i_ref, g_ref, la_ref, l1m_ref, og_ref, o_ref):
    c_len, sub = HGRN_CHUNK, HGRN_SUB
    half = sub // 2
    n_chunk = q_ref.shape[0] // c_len
    la, l1m, og = la_ref[...], l1m_ref[...], og_ref[...]
    rr = lax.broadcasted_iota(jnp.int32, (c_len, c_len), 0)
    cc = lax.broadcasted_iota(jnp.int32, (c_len, c_len), 1)
    tri = jnp.where(cc <= rr, 1.0, 0.0).astype(BF16)
    row8 = lax.broadcasted_iota(jnp.int32, (half, 1), 0)
    lane8 = lax.broadcasted_iota(jnp.int32, (half, c_len), 1)
    nt = (((1,), (1,)), ((), ()))

    def chunk(c, st):
        r0 = pl.multiple_of(c * c_len, c_len)
        rows = pl.ds(r0, c_len)
        z = f_ref[rows, :].astype(F32)
        qh = q_ref[rows, :].astype(F32)
        qh = qh * jax.nn.sigmoid(qh)
        v = i_ref[rows, :]
        gate = g_ref[rows, :].astype(F32)
        sp = jnp.log(1.0 + jnp.exp(-jnp.abs(z)))
        bb = l1m + jnp.minimum(z, 0.0) - sp
        log_f = jnp.maximum(la, bb) + jnp.log(1.0 + jnp.exp(-jnp.abs(la - bb)))
        valid = (r0 + lax.broadcasted_iota(jnp.int32, (c_len, 1), 0)) >= PAD0
        log_k = jnp.where(valid, l1m + jnp.minimum(-z, 0.0) - sp, -jnp.inf)
        g = sum(jnp.dot(tri, part, preferred_element_type=F32) for part in _split3(log_f))
        ck = log_k - g
        o_inter = lax.dot_general((qh * jnp.exp(g)).astype(BF16), st.astype(BF16), nt,
                                  preferred_element_type=F32)
        a_rows = []
        for a in range(c_len // sub):
            lo = a * sub
            ga = (g[lo:lo + half, :], g[lo + half:lo + sub, :])
            qa = (qh[lo:lo + half, :], qh[lo + half:lo + sub, :])
            if a == 0:
                blk = [jnp.zeros((half, c_len), F32)] * 2
            else:
                gs = g[lo - 1:lo, :]
                qd = (qh[lo:lo + sub, :] * jnp.exp(g[lo:lo + sub, :] - gs)).astype(BF16)
                kd = jnp.exp(jnp.minimum(gs - g[:lo, :], 0.0) + log_k[:lo, :]).astype(BF16)
                kd = jnp.concatenate([kd, jnp.zeros((c_len - lo, HGRN_D), BF16)], axis=0)
                a_off = lax.dot_general(qd, kd, nt, preferred_element_type=F32)
                blk = [a_off[:half, :], a_off[half:, :]]
            for s in range(sub):
                crow = ck[lo + s:lo + s + 1, :]
                for hh in range(s // half, 2):
                    col = jnp.sum(qa[hh] * jnp.exp(ga[hh] + crow), axis=-1, keepdims=True)
                    blk[hh] = jnp.where(lane8 == lo + s, col, blk[hh])
            for hh in range(2):
                a_rows.append(jnp.where(lane8 <= lo + hh * half + row8, blk[hh], 0.0))
        a_full = jnp.concatenate(a_rows, axis=0).astype(BF16)
        o = o_inter + jnp.dot(a_full, v, preferred_element_type=F32)
        g_last = g[c_len - 1:c_len, :]
        kd = jnp.exp(g_last - g + log_k).astype(BF16)
        st = st * jnp.exp(g_last) + lax.dot_general(v, kd, (((0,), (0,)), ((), ())),
                                                    preferred_element_type=F32)
        ms = jnp.mean(o * o, axis=-1, keepdims=True)
        y = o * lax.rsqrt(ms + EPS) * og * (gate * jax.nn.sigmoid(gate))
        o_ref[rows, :] = y.astype(BF16)
        return st

    lax.fori_loop(0, n_chunk, chunk, jnp.zeros((HGRN_D, HGRN_D), F32), unroll=3)


def _hgrn(proj3, log_lb, log1m_lb, out_gain):
    bsz, tp, _ = proj3.shape
    seq = lambda blk: pl.BlockSpec((None, tp, 128), lambda b, h: (b, 0, blk + h))
    chan = pl.BlockSpec((None, 1, HGRN_D), lambda b, h: (h, 0, 0))
    return pl.pallas_call(
        _hgrn_kernel,
        out_shape=jax.ShapeDtypeStruct((bsz, tp, HGRN_HEADS * HGRN_D), BF16),
        grid=(bsz, HGRN_HEADS),
        in_specs=[seq(BLK_RQ), seq(BLK_RF), seq(BLK_RI), seq(BLK_RG), chan, chan,
                  pl.BlockSpec((1, HGRN_D), lambda b, h: (0, 0))],
        out_specs=pl.BlockSpec((None, tp, 128), lambda b, h: (b, 0, h)),
        compiler_params=_params(("parallel", "parallel")),
        name="hgrn2",
    )(proj3, proj3, proj3, proj3,
      log_lb.reshape(HGRN_HEADS, 1, HGRN_D), log1m_lb.reshape(HGRN_HEADS, 1, HGRN_D),
      out_gain.reshape(1, HGRN_D))


def _merge_kernel(hs_ref, ua_ref, ur_ref, cb_ref, cc_ref, ch_ref, pc_ref, ph_ref,
                  g0_ref, g1_ref, g2_ref, cw_ref, wb_ref, wo_ref, o_ref, *, tm, tiles_per_seq):
    i = pl.program_id(0)
    row = (i % tiles_per_seq) * tm + lax.broadcasted_iota(jnp.int32, (tm, 1), 0)
    valid = row >= PAD0
    z = jnp.where(valid, cc_ref[...].astype(F32) * ch_ref[...].astype(F32), 0.0)
    halo_row = (i % tiles_per_seq) * tm - 8 + lax.broadcasted_iota(jnp.int32, (8, 1), 0)
    zp = jnp.where(halo_row >= PAD0, pc_ref[...].astype(F32) * ph_ref[...].astype(F32), 0.0)
    zz = jnp.concatenate([zp, z], axis=0)
    cw = cw_ref[...]
    y = (cw[2:3, :] * z + cw[1:2, :] * zz[7:7 + tm, :] + cw[0:1, :] * zz[6:6 + tm, :])
    u_conv = jnp.where(valid, cb_ref[...].astype(F32) * y, 0.0).astype(BF16)
    mixed = jnp.zeros((tm, D_MODEL), F32)
    for n, (u, g_ref) in enumerate(((ua_ref[...], g0_ref), (u_conv, g1_ref), (ur_ref[...], g2_ref))):
        up = jnp.dot(u, wb_ref[n], preferred_element_type=F32)
        mixed = mixed + jax.nn.sigmoid(g_ref[...].astype(F32)) * up
    o_ref[...] = hs_ref[...] + jnp.dot(mixed.astype(BF16), wo_ref[...], preferred_element_type=F32)


def _merge(hs, u_att, u_hgrn, proj, conv_w, wb_bf, wo_bf, tp):
    rows = hs.shape[0]
    tm = _row_tile(tp, 384)
    kern = functools.partial(_merge_kernel, tm=tm, tiles_per_seq=tp // tm)
    row_blk = lambda w, blk: pl.BlockSpec((tm, w), lambda i: (i, blk))
    halo = lambda blk: pl.BlockSpec((8, 512), lambda i: (jnp.maximum(i * (tm // 8) - 1, 0), blk))
    const = lambda shape: pl.BlockSpec(shape, lambda i: (0,) * len(shape))
    return pl.pallas_call(
        kern,
        out_shape=jax.ShapeDtypeStruct((rows, D_MODEL), F32),
        grid=(rows // tm,),
        in_specs=[row_blk(D_MODEL, 0), row_blk(512, 0), row_blk(512, 0),
                  row_blk(512, BLK_CB), row_blk(512, BLK_CC), row_blk(512, BLK_CH),
                  halo(BLK_CC), halo(BLK_CH),
                  row_blk(1024, BLK_GATE), row_blk(1024, BLK_GATE + 1), row_blk(1024, BLK_GATE + 2),
                  const((CONV_K, 512)), const((3, 512, D_MODEL)), const((D_MODEL, D_MODEL))],
        out_specs=row_blk(D_MODEL, 0),
        compiler_params=_params(("parallel",)),
        name="merge",
    )(hs, u_att, u_hgrn, proj, proj, proj, proj, proj, proj, proj, proj, conv_w, wb_bf, wo_bf)


def _ffn_kernel(hs_ref, g_ref, wg_ref, wu_ref, wd_ref, o_ref):
    x = hs_ref[...]
    ms = jnp.mean(x * x, axis=-1, keepdims=True)
    hn = (x * lax.rsqrt(ms + EPS) * g_ref[...]).astype(BF16)
    a = jnp.dot(hn, wg_ref[...], preferred_element_type=F32)
    u = jnp.dot(hn, wu_ref[...], preferred_element_type=F32)
    h = (a * jax.nn.sigmoid(a) * u).astype(BF16)
    o_ref[...] = x + jnp.dot(h, wd_ref[...], preferred_element_type=F32)


def _ffn(hs, gain, wg_bf, wu_bf, wd_bf):
    rows = hs.shape[0]
    d_ff = wg_bf.shape[1]
    tm = _row_tile(rows, 384)
    const = lambda shape: pl.BlockSpec(shape, lambda i: (0, 0))
    return pl.pallas_call(
        _ffn_kernel,
        out_shape=jax.ShapeDtypeStruct((rows, D_MODEL), F32),
        grid=(rows // tm,),
        in_specs=[pl.BlockSpec((tm, D_MODEL), lambda i: (i, 0)), const((1, D_MODEL)),
                  const((D_MODEL, d_ff)), const((D_MODEL, d_ff)), const((d_ff, D_MODEL))],
        out_specs=pl.BlockSpec((tm, D_MODEL), lambda i: (i, 0)),
        compiler_params=_params(("parallel",)),
        name="ffn_dense",
    )(hs, gain.reshape(1, D_MODEL), wg_bf, wu_bf, wd_bf)


def _router_kernel(hs_ref, g_ref, rw_ref, idx_ref, wt_ref, cnt_ref, carry_ref, *, tm, tiles_per_seq):
    i = pl.program_id(0)

    @pl.when(i == 0)
    def _():
        carry_ref[...] = jnp.zeros_like(carry_ref)

    x = hs_ref[...]
    ms = jnp.mean(x * x, axis=-1, keepdims=True)
    hn = x * lax.rsqrt(ms + EPS) * g_ref[...]
    logits = jnp.dot(hn, rw_ref[...], preferred_element_type=F32, precision=lax.Precision.HIGHEST)
    lane = lax.broadcasted_iota(jnp.int32, logits.shape, 1)
    lane_f = lane.astype(F32)
    logits = jnp.where(lane < N_EXPERTS, logits, -jnp.inf)
    m1 = jnp.max(logits, axis=-1, keepdims=True)
    i1 = jnp.min(jnp.where(logits == m1, lane_f, 128.0), axis=-1, keepdims=True)
    rest = jnp.where(lane_f == i1, -jnp.inf, logits)
    m2 = jnp.max(rest, axis=-1, keepdims=True)
    i2 = jnp.min(jnp.where(rest == m2, lane_f, 128.0), axis=-1, keepdims=True)
    e2 = jnp.exp(m2 - m1)
    w1 = 1.0 / (1.0 + e2)
    w2 = e2 / (1.0 + e2)
    row = (i % tiles_per_seq) * tm + lax.broadcasted_iota(jnp.int32, (tm, 1), 0)
    real = row >= T_PAD
    hot1 = jnp.where(jnp.logical_and(real, lane_f == i1), 1.0, 0.0)
    hot2 = jnp.where(jnp.logical_and(real, lane_f == i2), 1.0, 0.0)
    both = hot1 + hot2
    rr = lax.broadcasted_iota(jnp.int32, (tm, tm), 0)
    cc = lax.broadcasted_iota(jnp.int32, (tm, tm), 1)
    earlier = jnp.where(cc < rr, 1.0, 0.0).astype(BF16)
    before = carry_ref[...] + jnp.dot(earlier, both.astype(BF16), preferred_element_type=F32)
    r1 = jnp.sum(before * hot1, axis=-1, keepdims=True)
    r2 = jnp.sum(before * hot2, axis=-1, keepdims=True)
    carry_ref[...] += jnp.sum(both, axis=0, keepdims=True)
    cnt_ref[...] = carry_ref[...]
    packed = jnp.where(lane == 0, i1, jnp.where(lane == 1, i2, jnp.where(lane == 2, r1, jnp.where(lane == 3, r2, 0.0))))
    idx_ref[...] = packed.astype(jnp.int32)
    wt_ref[...] = jnp.where(lane == 0, w1, jnp.where(lane == 1, w2, 0.0))


def _router(hs, gain, router_w, tp):
    rows = hs.shape[0]
    tm = _row_tile(tp, 384)
    rw = jnp.zeros((D_MODEL, 128), F32).at[:, :N_EXPERTS].set(router_w.astype(F32))
    blk = pl.BlockSpec((tm, 128), lambda i: (i, 0))
    return pl.pallas_call(
        functools.partial(_router_kernel, tm=tm, tiles_per_seq=tp // tm),
        out_shape=(jax.ShapeDtypeStruct((rows, 128), jnp.int32), jax.ShapeDtypeStruct((rows, 128), F32),
                   jax.ShapeDtypeStruct((1, 128), F32)),
        grid=(rows // tm,),
        in_specs=[pl.BlockSpec((tm, D_MODEL), lambda i: (i, 0)),
                  pl.BlockSpec((1, D_MODEL), lambda i: (0, 0)),
                  pl.BlockSpec((D_MODEL, 128), lambda i: (0, 0))],
        out_specs=(blk, blk, pl.BlockSpec((1, 128), lambda i: (0, 0))),
        scratch_shapes=[pltpu.VMEM((1, 128), F32)],
        compiler_params=_params(("arbitrary",)),
        name="moe_router",
    )(hs, gain.reshape(1, D_MODEL), rw)


def _dispatch_kernel(s1_ref, s2_ref, zr_ref, hs_ref, wt_ref, g_ref, xs_ref, rows_ref, zero_ref, sem, zsem,
                     *, td, tg, tiles_per_seq, n_steps):
    n = pl.program_id(0) * tiles_per_seq + pl.program_id(1)
    buf = n % 2
    base = n * td

    def row_copy(r, which):
        slot = (s1_ref, s2_ref)[which][base + r]
        return pltpu.make_async_copy(rows_ref.at[buf, which, pl.ds(r, 1)], xs_ref.at[pl.ds(slot, 1)],
                                     sem.at[buf])

    def wait_rows(b):
        def body(r, c):
            pltpu.make_async_copy(rows_ref.at[b, 0, pl.ds(0, 1)], xs_ref.at[pl.ds(0, 1)], sem.at[b]).wait()
            return c
        lax.fori_loop(0, 2 * td, body, 0)

    @pl.when(n == 0)
    def _():
        zero_ref[...] = jnp.zeros_like(zero_ref)
        for e in range(2 * N_EXPERTS):
            fill = pltpu.make_async_copy(zero_ref, xs_ref.at[pl.ds(pl.multiple_of(zr_ref[e], tg), tg)], zsem)
            fill.start()
            fill.wait()

    @pl.when(n >= 2)
    def _():
        wait_rows(buf)

    x = hs_ref[...]
    ms = jnp.mean(x * x, axis=-1, keepdims=True)
    hn = x * lax.rsqrt(ms + EPS) * g_ref[...]
    wt = wt_ref[...]
    for which in range(2):
        rows_ref[buf, which, :, :D_MODEL] = hn
        rows_ref[buf, which, :, D_MODEL:] = jnp.broadcast_to(wt[:, which:which + 1], (td, XS_EXTRA))

    def issue(r, c):
        row_copy(r, 0).start()
        row_copy(r, 1).start()
        return c

    lax.fori_loop(0, td, issue, 0)

    @pl.when(n == n_steps - 1)
    def _():
        wait_rows(buf)
        if n_steps > 1:
            wait_rows(1 - buf)


def _dispatch(hs3, wts3, gain, slot1, slot2, zero_rows, n_slots, tg):
    bsz, tp, _ = hs3.shape
    td = T_PAD
    tiles_per_seq = (tp - T_PAD) // td
    width = D_MODEL + XS_EXTRA
    kern = functools.partial(_dispatch_kernel, td=td, tg=tg, tiles_per_seq=tiles_per_seq,
                             n_steps=bsz * tiles_per_seq)
    return pl.pallas_call(
        kern,
        out_shape=jax.ShapeDtypeStruct((n_slots, width), F32),
        grid_spec=pltpu.PrefetchScalarGridSpec(
            num_scalar_prefetch=3,
            grid=(bsz, tiles_per_seq),
            in_specs=[pl.BlockSpec((None, td, D_MODEL), lambda b, i, *_: (b, i + 1, 0)),
                      pl.BlockSpec((None, td, 128), lambda b, i, *_: (b, i + 1, 0)),
                      pl.BlockSpec((1, D_MODEL), lambda b, i, *_: (0, 0))],
            out_specs=pl.BlockSpec(memory_space=pl.ANY),
            scratch_shapes=[pltpu.VMEM((2, 2, td, width), F32), pltpu.VMEM((tg, width), F32),
                            pltpu.SemaphoreType.DMA((2,)), pltpu.SemaphoreType.DMA],
        ),
        compiler_params=_params(("arbitrary", "arbitrary")),
        name="moe_dispatch",
    )(slot1, slot2, zero_rows, hs3, wts3, gain.reshape(1, D_MODEL))


def _expert_kernel(te_ref, nt_ref, x_ref, wg_ref, wu_ref, wd_ref, o_ref, hn_ref, acc_ref, *, n_f):
    t = pl.program_id(0)
    f = pl.program_id(1)

    @pl.when(t < nt_ref[0])
    def _():
        @pl.when(f == 0)
        def _():
            hn_ref[...] = x_ref[:, :D_MODEL].astype(BF16)

        hn = hn_ref[...]
        a = jnp.dot(hn, wg_ref[...], preferred_element_type=F32)
        u = jnp.dot(hn, wu_ref[...], preferred_element_type=F32)
        h = (a * jax.nn.sigmoid(a) * u).astype(BF16)
        y = jnp.dot(h, wd_ref[...], preferred_element_type=F32)

        @pl.when(f == 0)
        def _():
            acc_ref[...] = y

        @pl.when(f > 0)
        def _():
            acc_ref[...] += y

    @pl.when(jnp.logical_and(t >= nt_ref[0], f == 0))
    def _():
        acc_ref[...] = jnp.zeros_like(acc_ref)

    @pl.when(f == n_f - 1)
    def _():
        o_ref[...] = (acc_ref[...] * x_ref[:, D_MODEL:D_MODEL + 1]).astype(BF16)


def _experts(xs, tile_expert, n_tiles_used, wg_bf, wu_bf, wd_bf, tg):
    slots = xs.shape[0]
    d_ff = wg_bf.shape[2]
    n_f = 2
    tf = d_ff // n_f

    def x_map(t, f, te, nt):
        return (jnp.minimum(t, nt[0] - 1), 0)

    def f_of(t, f, nt):
        return jnp.where(t < nt[0], f, n_f - 1)

    return pl.pallas_call(
        functools.partial(_expert_kernel, n_f=n_f),
        out_shape=jax.ShapeDtypeStruct((slots, D_MODEL), BF16),
        grid_spec=pltpu.PrefetchScalarGridSpec(
            num_scalar_prefetch=2,
            grid=(slots // tg, n_f),
            in_specs=[
                pl.BlockSpec((tg, D_MODEL + XS_EXTRA), x_map),
                pl.BlockSpec((None, D_MODEL, tf), lambda t, f, te, nt: (te[t], 0, f_of(t, f, nt))),
                pl.BlockSpec((None, D_MODEL, tf), lambda t, f, te, nt: (te[t], 0, f_of(t, f, nt))),
                pl.BlockSpec((None, tf, D_MODEL), lambda t, f, te, nt: (te[t], f_of(t, f, nt), 0)),
            ],
            out_specs=pl.BlockSpec((tg, D_MODEL), lambda t, f, te, nt: (t, 0)),
            scratch_shapes=[pltpu.VMEM((tg, D_MODEL), BF16), pltpu.VMEM((tg, D_MODEL), F32)],
        ),
        compiler_params=_params(("arbitrary", "arbitrary")),
        name="moe_experts",
    )(tile_expert, n_tiles_used, xs, wg_bf, wu_bf, wd_bf)


def _combine_kernel(ws_ref, hs_ref, route_ref, y_ref, o_ref, win_ref, sem, *, tc, tiles_per_seq, n_steps):
    n = pl.program_id(0) * tiles_per_seq + pl.program_id(1)
    cur = n % 2

    def window_copy(step, b, e):
        start = pl.multiple_of(ws_ref[step * N_EXPERTS + e], WIN_ALIGN)
        return pltpu.make_async_copy(y_ref.at[pl.ds(start, COMBINE_WIN)],
                                     win_ref.at[b, pl.ds(e * COMBINE_WIN, COMBINE_WIN)], sem.at[b])

    def fetch(step, b):
        for e in range(N_EXPERTS):
            window_copy(step, b, e).start()

    @pl.when(n == 0)
    def _():
        fetch(0, 0)

    @pl.when(n + 1 < n_steps)
    def _():
        fetch(n + 1, 1 - cur)

    for e in range(N_EXPERTS):
        window_copy(n, cur, e).wait()

    route = route_ref[...]
    e1, e2, s1, s2 = (route[:, c:c + 1] for c in range(4))
    pos = lax.broadcasted_iota(jnp.int32, (1, COMBINE_WIN), 1)
    picks = []
    for e in range(N_EXPERTS):
        start = ws_ref[n * N_EXPERTS + e]
        col = jnp.where(e1 == e, s1 - start, jnp.where(e2 == e, s2 - start, -1))
        picks.append(jnp.where(col == pos, 1.0, 0.0).astype(BF16))
    pick = jnp.concatenate(picks, axis=1)
    o_ref[...] = hs_ref[...] + jnp.dot(pick, win_ref[cur], preferred_element_type=F32)


def _combine(hs3, route, y, win_start, seq):
    bsz = hs3.shape[0]
    tc = COMBINE_TOKENS
    tiles_per_seq = seq // tc
    kern = functools.partial(_combine_kernel, tc=tc, tiles_per_seq=tiles_per_seq, n_steps=bsz * tiles_per_seq)
    tok = lambda w: pl.BlockSpec((tc, w), lambda b, i, ws: (b * tiles_per_seq + i, 0))
    return pl.pallas_call(
        kern,
        out_shape=jax.ShapeDtypeStruct((bsz, seq, D_MODEL), F32),
        grid_spec=pltpu.PrefetchScalarGridSpec(
            num_scalar_prefetch=1,
            grid=(bsz, tiles_per_seq),
            in_specs=[
                pl.BlockSpec((None, tc, D_MODEL), lambda b, i, ws: (b, i + T_PAD // tc, 0)),
                tok(4),
                pl.BlockSpec(memory_space=pl.ANY),
            ],
            out_specs=pl.BlockSpec((None, tc, D_MODEL), lambda b, i, ws: (b, i, 0)),
            scratch_shapes=[pltpu.VMEM((2, N_EXPERTS * COMBINE_WIN, D_MODEL), BF16),
                            pltpu.SemaphoreType.DMA((2,))],
        ),
        compiler_params=_params(("arbitrary", "arbitrary")),
        name="moe_combine",
    )(win_start, hs3, route, y)


def _moe(hs, gain, router_w, wg_bf, wu_bf, wd_bf, bsz, tp):
    seq = tp - T_PAD
    n_tok = bsz * seq
    tg = MOE_TILE
    idx, wts, cnt = _router(hs, gain, router_w, tp)
    sel = idx.reshape(bsz, tp, 128)[:, T_PAD:, :4].reshape(n_tok, 4)
    counts = cnt[0, :N_EXPERTS].astype(jnp.int32)
    padded = ((counts + tg - 1) // tg) * tg
    ends = jnp.cumsum(padded)
    starts = ends - padded
    experts = jnp.arange(N_EXPERTS, dtype=jnp.int32)[None, :]
    slot1 = jnp.sum(jnp.where(sel[:, 0:1] == experts, starts[None, :], 0), axis=1) + sel[:, 2]
    slot2 = jnp.sum(jnp.where(sel[:, 1:2] == experts, starts[None, :], 0), axis=1) + sel[:, 3]
    slot1, slot2 = slot1.astype(jnp.int32), slot2.astype(jnp.int32)
    n_slots = 2 * n_tok + N_EXPERTS * tg
    tile_start = jnp.arange(n_slots // tg, dtype=jnp.int32)[:, None] * tg
    tile_expert = jnp.minimum(jnp.sum((tile_start >= ends[None, :]).astype(jnp.int32), axis=1), N_EXPERTS - 1)
    n_tiles_used = (ends[-1:] // tg).astype(jnp.int32)
    tail = n_slots - tg * (1 + jnp.arange(N_EXPERTS, dtype=jnp.int32))
    zero_rows = jnp.concatenate([jnp.maximum(ends - tg, 0), tail]).astype(jnp.int32)

    tc = COMBINE_TOKENS
    first = jnp.minimum(
        jnp.min(jnp.where(sel[:, 0:1] == experts, slot1[:, None], n_slots).reshape(n_tok // tc, tc, N_EXPERTS), axis=1),
        jnp.min(jnp.where(sel[:, 1:2] == experts, slot2[:, None], n_slots).reshape(n_tok // tc, tc, N_EXPERTS), axis=1))
    win_start = jnp.where(first == n_slots, 0,
                          jnp.minimum(first // WIN_ALIGN * WIN_ALIGN, n_slots - COMBINE_WIN))
    route = jnp.concatenate([sel[:, :2], slot1[:, None], slot2[:, None]], axis=1)

    hs3 = hs.reshape(bsz, tp, D_MODEL)
    xs = _dispatch(hs3, wts.reshape(bsz, tp, 128), gain, slot1, slot2, zero_rows, n_slots, tg)
    y = _experts(xs, tile_expert.astype(jnp.int32), n_tiles_used, wg_bf, wu_bf, wd_bf, tg)
    return _combine(hs3, route, y, win_start.reshape(-1).astype(jnp.int32), seq)


def _permute_qk_cols(w):
    return w.reshape(D_MODEL, 2, ATT_HEADS, ATT_QK_DIM).transpose(0, 2, 1, 3).reshape(D_MODEL, 512)


def kernel(x, meta_tokens, norm1_gain, norm2_gain, w_in, q_norm_gain, k_norm_gain, diff_lambda,
           attn_sub_gain, rel_bias, conv_w, hgrn_lb_logits, hgrn_out_gain, w_branch, w_out,
           ffn_w_gate, ffn_w_up, ffn_w_down, router_w, moe_w_gate, moe_w_up, moe_w_down):
    bsz, seq, _ = x.shape
    depth = w_in.shape[0]
    tp = T_PAD + seq
    assert tp % ATT_TILE == 0 and depth == 2

    head = jnp.concatenate([jnp.zeros((PAD0, D_MODEL), x.dtype), meta_tokens.astype(x.dtype)], axis=0)
    hs = jnp.concatenate([jnp.broadcast_to(head[None], (bsz, T_PAD, D_MODEL)), x], axis=1)
    hs = hs.reshape(bsz * tp, D_MODEL)

    lb_all = jnp.cumsum(jax.nn.softmax(hgrn_lb_logits.astype(F32), axis=0), axis=0)
    lb_all = lb_all - lb_all[0]
    btab = _attn_bias_tables(rel_bias, ATT_TILE)

    out = None
    for layer in range(depth):
        w = w_in[layer]
        w_bf = jnp.concatenate([_permute_qk_cols(w[:, :512]), _permute_qk_cols(w[:, 512:1024]), w[:, 1024:]],
                               axis=1).astype(BF16)
        qk_gain = jnp.concatenate([jnp.tile(q_norm_gain[layer].astype(F32), 8) * (ATT_QK_DIM ** -0.5 * LOG2E),
                                   jnp.tile(k_norm_gain[layer].astype(F32), 8)]).reshape(1, COL_TILE)
        proj = _inproj(hs, norm1_gain[layer], w_bf, qk_gain)
        proj3 = proj.reshape(bsz, tp, IN_COLS)

        lam_init = 0.8 - 0.6 * math.exp(-0.3 * layer)
        lp = diff_lambda[layer].astype(F32)
        lam = jnp.exp(jnp.sum(lp[0] * lp[1])) - jnp.exp(jnp.sum(lp[2] * lp[3])) + lam_init
        u_att = _diff_attention(proj3, lam.reshape(1), attn_sub_gain[layer].astype(F32), btab,
                                1.0 - lam_init)

        lb = lb_all[layer]
        u_hgrn = _hgrn(proj3, jnp.log(lb), jnp.log1p(-lb), hgrn_out_gain[layer].astype(F32))

        hs = _merge(hs, u_att.reshape(bsz * tp, 512), u_hgrn.reshape(bsz * tp, 512), proj,
                    conv_w[layer].astype(F32), w_branch[layer].astype(BF16), w_out[layer].astype(BF16), tp)

        j = layer // 2
        if layer % 2 == 0:
            hs = _ffn(hs, norm2_gain[layer], ffn_w_gate[j].astype(BF16), ffn_w_up[j].astype(BF16),
                      ffn_w_down[j].astype(BF16))
        else:
            out = _moe(hs, norm2_gain[layer], router_w[j], moe_w_gate[j].astype(BF16),
                       moe_w_up[j].astype(BF16), moe_w_down[j].astype(BF16), bsz, tp)
    return out
```

```python
import functools
import math

import numpy as np
import jax
import jax.numpy as jnp
from jax import lax
from jax.experimental import pallas as pl
from jax.experimental.pallas import tpu as pltpu

F32 = jnp.float32
BF16 = jnp.bfloat16

D_MODEL = 1024
N_META = 16
EPS = 1e-6
ATT_HEADS = 4
ATT_QK_DIM = 64
ATT_V_DIM = 128
REL_BUCKETS = 32
REL_MAX_DIST = 128
CONV_K = 3
HGRN_HEADS = 4
HGRN_D = 128
N_EXPERTS = 8
IN_COLS = 8192

T_PAD = 128
PAD0 = T_PAD - N_META
ATT_TILE = 384
VT_ONES = 16
LOG2E = math.log2(math.e)
HGRN_CHUNK = 128
HGRN_SUB = 16
COL_TILE = 1024
GROUP_MEAN_WIDTH = 256
MOE_TILE = 512
XS_EXTRA = 128
COMBINE_TOKENS = 128
WIN_ALIGN = 16
COMBINE_WIN = 256
MASK = -1e30
VMEM_LIMIT = 56 * 1024 * 1024

BLK_Q, BLK_K, BLK_V = 0, 4, 8
BLK_RQ, BLK_RF, BLK_RI, BLK_RG = 24, 28, 32, 36
BLK_CB, BLK_CC, BLK_CH = 3, 4, 5
BLK_GATE = 5


def _row_tile(rows, target):
    n = rows // 128
    best = 1
    for d in range(1, n + 1):
        if n % d == 0 and d * 128 <= target:
            best = d
    return best * 128


def _params(sem, vmem=VMEM_LIMIT):
    return pltpu.CompilerParams(dimension_semantics=sem, vmem_limit_bytes=vmem)


def _inproj_kernel(x_ref, g_ref, w_ref, qkg_ref, gm_ref, o_ref, xn_ref):
    j = pl.program_id(1)

    @pl.when(j == 0)
    def _():
        x = x_ref[...]
        ms = jnp.mean(x * x, axis=-1, keepdims=True)
        xn_ref[...] = (x * lax.rsqrt(ms + EPS) * g_ref[...]).astype(BF16)

    acc = jnp.dot(xn_ref[...], w_ref[...], preferred_element_type=F32)

    @pl.when(j == 0)
    def _():
        sq = acc * acc
        hi = sq.astype(BF16)
        lo = (sq - hi.astype(F32)).astype(BF16)
        gm = gm_ref[...]
        width = gm.shape[0]
        ms = jnp.concatenate(
            [jnp.dot(hi[:, c:c + width], gm, preferred_element_type=F32)
             + jnp.dot(lo[:, c:c + width], gm, preferred_element_type=F32)
             for c in range(0, COL_TILE, width)], axis=1)
        o_ref[...] = (acc * lax.rsqrt(ms + EPS) * qkg_ref[...]).astype(BF16)

    @pl.when(j > 0)
    def _():
        o_ref[...] = acc.astype(BF16)


def _inproj(hs, gain, w_bf, qk_gain):
    rows = hs.shape[0]
    tm = _row_tile(rows, 1536)
    n_col = IN_COLS // COL_TILE
    assert COL_TILE == 4 * ATT_HEADS * ATT_QK_DIM
    grp = np.arange(GROUP_MEAN_WIDTH) // ATT_QK_DIM
    gm = jnp.asarray((grp[:, None] == grp[None, :]).astype(np.float32) / ATT_QK_DIM, BF16)
    return pl.pallas_call(
        _inproj_kernel,
        out_shape=jax.ShapeDtypeStruct((rows, IN_COLS), BF16),
        grid=(rows // tm, n_col),
        in_specs=[
            pl.BlockSpec((tm, D_MODEL), lambda i, j: (i, 0)),
            pl.BlockSpec((1, D_MODEL), lambda i, j: (0, 0)),
            pl.BlockSpec((D_MODEL, COL_TILE), lambda i, j: (0, j)),
            pl.BlockSpec((1, COL_TILE), lambda i, j: (0, 0)),
            pl.BlockSpec((GROUP_MEAN_WIDTH, GROUP_MEAN_WIDTH), lambda i, j: (0, 0)),
        ],
        out_specs=pl.BlockSpec((tm, COL_TILE), lambda i, j: (i, j)),
        scratch_shapes=[pltpu.VMEM((tm, D_MODEL), BF16)],
        compiler_params=_params(("parallel", "arbitrary")),
        name="inproj",
    )(hs, gain.reshape(1, D_MODEL), w_bf, qk_gain, gm)


def _rel_bucket_table(n_max):
    n = np.arange(n_max, dtype=np.int64)
    max_exact = REL_BUCKETS // 2
    nf = np.maximum(n, 1).astype(np.float32)
    large = max_exact + (np.log(nf / np.float32(max_exact)) / np.float32(math.log(REL_MAX_DIST / max_exact))
                         * np.float32(REL_BUCKETS - max_exact)).astype(np.int32)
    large = np.minimum(large, REL_BUCKETS - 1)
    return np.where(n < max_exact, n, large).astype(np.int32)


def _attn_bias_tables(rel_bias, t):
    bucket = _rel_bucket_table(2 * t)
    assert np.all(bucket[t + 1:] == REL_BUCKETS - 1) and np.all(np.diff(bucket) >= 0)
    first_dist = tuple(int(np.searchsorted(bucket, b, side="left")) for b in range(REL_BUCKETS))
    return pl.pallas_call(
        functools.partial(_bias_kernel, t=t, first_dist=first_dist),
        out_shape=jax.ShapeDtypeStruct((ATT_HEADS, 6, t, t), F32),
        grid=(ATT_HEADS,),
        in_specs=[pl.BlockSpec(memory_space=pltpu.SMEM)],
        out_specs=pl.BlockSpec((None, 6, t, t), lambda h: (h, 0, 0, 0)),
        compiler_params=_params(("parallel",)),
        name="attn_bias",
    )(rel_bias.astype(F32))


def _bias_kernel(rb_ref, o_ref, *, t, first_dist):
    h = pl.program_id(0)
    key = lax.broadcasted_iota(jnp.int32, (t, t), 0)
    qry = lax.broadcasted_iota(jnp.int32, (t, t), 1)
    far = rb_ref[REL_BUCKETS - 1, h]

    def table(n):
        val = jnp.full((t, t), rb_ref[0, h] - far, F32)
        for b in range(1, REL_BUCKETS):
            val = jnp.where(n >= first_dist[b], rb_ref[b, h] - far, val)
        return val * LOG2E

    n0 = qry - key
    diag = jnp.where(n0 >= 0, table(n0), MASK)
    near = table(n0 + t)
    zero = jnp.zeros((t, t), F32)
    for kind, tab in enumerate((diag, near, zero)):
        o_ref[kind] = tab
        o_ref[kind + 3] = jnp.where(key < PAD0, MASK, tab)


def _attn_kernel(lam_ref, q_ref, k_ref, v_ref, bt_ref, sg_ref, o_ref, m_ref, acc_ref, vt_ref,
                 *, t, out_scale):
    n_t = vt_ref.shape[0]
    for j in range(n_t):
        vt_ref[j, :ATT_V_DIM, :] = v_ref[j * t:(j + 1) * t, :].astype(F32).T.astype(BF16)
        vt_ref[j, ATT_V_DIM:, :] = jnp.ones((VT_ONES, t), BF16)
    lane = lax.broadcasted_iota(jnp.int32, (1, 2 * ATT_QK_DIM), 1)
    nt = (((1,), (1,)), ((), ()))

    def tile_rows(i):
        return pl.ds(pl.multiple_of(i * t, t), t)

    def scores(i, j):
        q = q_ref[tile_rows(i), :]
        zero = jnp.zeros_like(q)
        q_cat = jnp.concatenate([jnp.where(lane < ATT_QK_DIM, q, zero), jnp.where(lane >= ATT_QK_DIM, q, zero)],
                                axis=0)
        bias = bt_ref[jnp.minimum(i - j, 2) + jnp.where(j == 0, 3, 0)]
        return (lax.dot_general(k_ref[tile_rows(j), :], q_cat, nt, preferred_element_type=F32)
                + jnp.concatenate([bias, bias], axis=1))

    def consume(j, s):
        m_old = m_ref[...]
        m_new = jnp.maximum(m_old, jnp.max(s, axis=0, keepdims=True))
        m_ref[...] = m_new
        p = jnp.exp2(s - m_new).astype(BF16)
        acc_ref[...] = (jnp.exp2(m_old - m_new) * acc_ref[...]
                        + jnp.dot(vt_ref[j], p, preferred_element_type=F32))

    def reset():
        m_ref[...] = jnp.full(m_ref.shape, MASK, F32)
        acc_ref[...] = jnp.zeros(acc_ref.shape, F32)

    def finish(i):
        acc = acc_ref[...]
        a1, a2 = acc[:, :t], acc[:, t:]
        o_t = (a1[:ATT_V_DIM] / a1[ATT_V_DIM:ATT_V_DIM + 1]
               - lam_ref[0] * (a2[:ATT_V_DIM] / a2[ATT_V_DIM:ATT_V_DIM + 1]))
        o = o_t.T
        ms = jnp.mean(o * o, axis=-1, keepdims=True)
        y = o * lax.rsqrt(ms + EPS) * (sg_ref[...] * out_scale)
        row = i * t + lax.broadcasted_iota(jnp.int32, (t, 1), 0)
        o_ref[tile_rows(i), :] = jnp.where(row >= PAD0, y, 0.0).astype(BF16)
        reset()

    def step(_, carry):
        i, j, s = carry
        last = j == i
        ni = jnp.where(last, i + 1, i)
        nj = jnp.where(last, 0, j + 1)
        nxt = scores(jnp.minimum(ni, n_t - 1), nj)
        consume(j, s)

        @pl.when(last)
        def _():
            finish(i)

        return ni, nj, nxt

    reset()
    first = jnp.int32(0)
    lax.fori_loop(0, n_t * (n_t + 1) // 2, step, (first, first, scores(first, first)))


def _diff_attention(proj3, lam, sub_gain, btab, out_scale):
    bsz, tp, _ = proj3.shape
    t = ATT_TILE
    n_t = tp // t
    rows = ATT_V_DIM + VT_ONES
    kern = functools.partial(_attn_kernel, t=t, out_scale=out_scale)
    seq = lambda blk: pl.BlockSpec((None, tp, 128), lambda b, h: (b, 0, blk + h))
    return pl.pallas_call(
        kern,
        out_shape=jax.ShapeDtypeStruct((bsz, tp, ATT_HEADS * ATT_V_DIM), BF16),
        grid=(bsz, ATT_HEADS),
        in_specs=[
            pl.BlockSpec(memory_space=pltpu.SMEM),
            seq(BLK_Q), seq(BLK_K), seq(BLK_V),
            pl.BlockSpec((None, 6, t, t), lambda b, h: (h, 0, 0, 0)),
            pl.BlockSpec((1, ATT_V_DIM), lambda b, h: (0, 0)),
        ],
        out_specs=pl.BlockSpec((None, tp, 128), lambda b, h: (b, 0, h)),
        scratch_shapes=[pltpu.VMEM((1, 2 * t), F32), pltpu.VMEM((rows, 2 * t), F32),
                        pltpu.VMEM((n_t, rows, t), BF16)],
        compiler_params=_params(("parallel", "parallel")),
        name="diff_attn",
    )(lam, proj3, proj3, proj3, btab, sub_gain.reshape(1, ATT_V_DIM))


def _split3(x):
    h1 = x.astype(BF16)
    r1 = x - h1.astype(F32)
    h2 = r1.astype(BF16)
    h3 = (r1 - h2.astype(F32)).astype(BF16)
    return h1, h2, h3


def _hgrn_kernel(q_ref, f_ref, i_ref, g_ref, la_ref, l1m_ref, og_ref, o_ref):
    c_len, sub = HGRN_CHUNK, HGRN_SUB
    half = sub // 2
    n_chunk = q_ref.shape[0] // c_len
    la, l1m, og = la_ref[...], l1m_ref[...], og_ref[...]
    rr = lax.broadcasted_iota(jnp.int32, (c_len, c_len), 0)
    cc = lax.broadcasted_iota(jnp.int32, (c_len, c_len), 1)
    tri = jnp.where(cc <= rr, 1.0, 0.0).astype(BF16)
    row8 = lax.broadcasted_iota(jnp.int32, (half, 1), 0)
    lane8 = lax.broadcasted_iota(jnp.int32, (half, c_len), 1)
    nt = (((1,), (1,)), ((), ()))

    def chunk(c, st):
        r0 = pl.multiple_of(c * c_len, c_len)
        rows = pl.ds(r0, c_len)
        z = f_ref[rows, :].astype(F32)
        qh = q_ref[rows, :].astype(F32)
        qh = qh * jax.nn.sigmoid(qh)
        v = i_ref[rows, :]
        gate = g_ref[rows, :].astype(F32)
        sp = jnp.log(1.0 + jnp.exp(-jnp.abs(z)))
        bb = l1m + jnp.minimum(z, 0.0) - sp
        log_f = jnp.maximum(la, bb) + jnp.log(1.0 + jnp.exp(-jnp.abs(la - bb)))
        valid = (r0 + lax.broadcasted_iota(jnp.int32, (c_len, 1), 0)) >= PAD0
        log_k = jnp.where(valid, l1m + jnp.minimum(-z, 0.0) - sp, -jnp.inf)
        g = sum(jnp.dot(tri, part, preferred_element_type=F32) for part in _split3(log_f))
        ck = log_k - g
        o_inter = lax.dot_general((qh * jnp.exp(g)).astype(BF16), st.astype(BF16), nt,
                                  preferred_element_type=F32)
        a_rows = []
        for a in range(c_len // sub):
            lo = a * sub
            ga = (g[lo:lo + half, :], g[lo + half:lo + sub, :])
            qa = (qh[lo:lo + half, :], qh[lo + half:lo + sub, :])
            if a == 0:
                blk = [jnp.zeros((half, c_len), F32)] * 2
            else:
                gs = g[lo - 1:lo, :]
                qd = (qh[lo:lo + sub, :] * jnp.exp(g[lo:lo + sub, :] - gs)).astype(BF16)
                kd = jnp.exp(jnp.minimum(gs - g[:lo, :], 0.0) + log_k[:lo, :]).astype(BF16)
                kd = jnp.concatenate([kd, jnp.zeros((c_len - lo, HGRN_D), BF16)], axis=0)
                a_off = lax.dot_general(qd, kd, nt, preferred_element_type=F32)
                blk = [a_off[:half, :], a_off[half:, :]]
            for s in range(sub):
                crow = ck[lo + s:lo + s + 1, :]
                for hh in range(s // half, 2):
                    col = jnp.sum(qa[hh] * jnp.exp(ga[hh] + crow), axis=-1, keepdims=True)
                    blk[hh] = jnp.where(lane8 == lo + s, col, blk[hh])
            for hh in range(2):
                a_rows.append(jnp.where(lane8 <= lo + hh * half + row8, blk[hh], 0.0))
        a_full = jnp.concatenate(a_rows, axis=0).astype(BF16)
        o = o_inter + jnp.dot(a_full, v, preferred_element_type=F32)
        g_last = g[c_len - 1:c_len, :]
        kd = jnp.exp(g_last - g + log_k).astype(BF16)
        st = st * jnp.exp(g_last) + lax.dot_general(v, kd, (((0,), (0,)), ((), ())),
                                                    preferred_element_type=F32)
        ms = jnp.mean(o * o, axis=-1, keepdims=True)
        y = o * lax.rsqrt(ms + EPS) * og * (gate * jax.nn.sigmoid(gate))
        o_ref[rows, :] = y.astype(BF16)
        return st

    lax.fori_loop(0, n_chunk, chunk, jnp.zeros((HGRN_D, HGRN_D), F32), unroll=3)


def _hgrn(proj3, log_lb, log1m_lb, out_gain):
    bsz, tp, _ = proj3.shape
    seq = lambda blk: pl.BlockSpec((None, tp, 128), lambda b, h: (b, 0, blk + h))
    chan = pl.BlockSpec((None, 1, HGRN_D), lambda b, h: (h, 0, 0))
    return pl.pallas_call(
        _hgrn_kernel,
        out_shape=jax.ShapeDtypeStruct((bsz, tp, HGRN_HEADS * HGRN_D), BF16),
        grid=(bsz, HGRN_HEADS),
        in_specs=[seq(BLK_RQ), seq(BLK_RF), seq(BLK_RI), seq(BLK_RG), chan, chan,
                  pl.BlockSpec((1, HGRN_D), lambda b, h: (0, 0))],
        out_specs=pl.BlockSpec((None, tp, 128), lambda b, h: (b, 0, h)),
        compiler_params=_params(("parallel", "parallel")),
        name="hgrn2",
    )(proj3, proj3, proj3, proj3,
      log_lb.reshape(HGRN_HEADS, 1, HGRN_D), log1m_lb.reshape(HGRN_HEADS, 1, HGRN_D),
      out_gain.reshape(1, HGRN_D))


def _merge_kernel(hs_ref, ua_ref, ur_ref, cb_ref, cc_ref, ch_ref, pc_ref, ph_ref,
                  g0_ref, g1_ref, g2_ref, cw_ref, wb_ref, wo_ref, o_ref, *, tm, tiles_per_seq):
    i = pl.program_id(0)
    row = (i % tiles_per_seq) * tm + lax.broadcasted_iota(jnp.int32, (tm, 1), 0)
    valid = row >= PAD0
    z = jnp.where(valid, cc_ref[...].astype(F32) * ch_ref[...].astype(F32), 0.0)
    halo_row = (i % tiles_per_seq) * tm - 8 + lax.broadcasted_iota(jnp.int32, (8, 1), 0)
    zp = jnp.where(halo_row >= PAD0, pc_ref[...].astype(F32) * ph_ref[...].astype(F32), 0.0)
    zz = jnp.concatenate([zp, z], axis=0)
    cw = cw_ref[...]
    y = (cw[2:3, :] * z + cw[1:2, :] * zz[7:7 + tm, :] + cw[0:1, :] * zz[6:6 + tm, :])
    u_conv = jnp.where(valid, cb_ref[...].astype(F32) * y, 0.0).astype(BF16)
    mixed = jnp.zeros((tm, D_MODEL), F32)
    for n, (u, g_ref) in enumerate(((ua_ref[...], g0_ref), (u_conv, g1_ref), (ur_ref[...], g2_ref))):
        up = jnp.dot(u, wb_ref[n], preferred_element_type=F32)
        mixed = mixed + jax.nn.sigmoid(g_ref[...].astype(F32)) * up
    o_ref[...] = hs_ref[...] + jnp.dot(mixed.astype(BF16), wo_ref[...], preferred_element_type=F32)


def _merge(hs, u_att, u_hgrn, proj, conv_w, wb_bf, wo_bf, tp):
    rows = hs.shape[0]
    tm = _row_tile(tp, 384)
    kern = functools.partial(_merge_kernel, tm=tm, tiles_per_seq=tp // tm)
    row_blk = lambda w, blk: pl.BlockSpec((tm, w), lambda i: (i, blk))
    halo = lambda blk: pl.BlockSpec((8, 512), lambda i: (jnp.maximum(i * (tm // 8) - 1, 0), blk))
    const = lambda shape: pl.BlockSpec(shape, lambda i: (0,) * len(shape))
    return pl.pallas_call(
        kern,
        out_shape=jax.ShapeDtypeStruct((rows, D_MODEL), F32),
        grid=(rows // tm,),
        in_specs=[row_blk(D_MODEL, 0), row_blk(512, 0), row_blk(512, 0),
                  row_blk(512, BLK_CB), row_blk(512, BLK_CC), row_blk(512, BLK_CH),
                  halo(BLK_CC), halo(BLK_CH),
                  row_blk(1024, BLK_GATE), row_blk(1024, BLK_GATE + 1), row_blk(1024, BLK_GATE + 2),
                  const((CONV_K, 512)), const((3, 512, D_MODEL)), const((D_MODEL, D_MODEL))],
        out_specs=row_blk(D_MODEL, 0),
        compiler_params=_params(("parallel",)),
        name="merge",
    )(hs, u_att, u_hgrn, proj, proj, proj, proj, proj, proj, proj, proj, conv_w, wb_bf, wo_bf)


def _ffn_kernel(hs_ref, g_ref, wg_ref, wu_ref, wd_ref, o_ref):
    x = hs_ref[...]
    ms = jnp.mean(x * x, axis=-1, keepdims=True)
    hn = (x * lax.rsqrt(ms + EPS) * g_ref[...]).astype(BF16)
    a = jnp.dot(hn, wg_ref[...], preferred_element_type=F32)
    u = jnp.dot(hn, wu_ref[...], preferred_element_type=F32)
    h = (a * jax.nn.sigmoid(a) * u).astype(BF16)
    o_ref[...] = x + jnp.dot(h, wd_ref[...], preferred_element_type=F32)


def _ffn(hs, gain, wg_bf, wu_bf, wd_bf):
    rows = hs.shape[0]
    d_ff = wg_bf.shape[1]
    tm = _row_tile(rows, 384)
    const = lambda shape: pl.BlockSpec(shape, lambda i: (0, 0))
    return pl.pallas_call(
        _ffn_kernel,
        out_shape=jax.ShapeDtypeStruct((rows, D_MODEL), F32),
        grid=(rows // tm,),
        in_specs=[pl.BlockSpec((tm, D_MODEL), lambda i: (i, 0)), const((1, D_MODEL)),
                  const((D_MODEL, d_ff)), const((D_MODEL, d_ff)), const((d_ff, D_MODEL))],
        out_specs=pl.BlockSpec((tm, D_MODEL), lambda i: (i, 0)),
        compiler_params=_params(("parallel",)),
        name="ffn_dense",
    )(hs, gain.reshape(1, D_MODEL), wg_bf, wu_bf, wd_bf)


def _router_kernel(hs_ref, g_ref, rw_ref, idx_ref, wt_ref, cnt_ref, carry_ref, *, tm, tiles_per_seq):
    i = pl.program_id(0)

    @pl.when(i == 0)
    def _():
        carry_ref[...] = jnp.zeros_like(carry_ref)

    x = hs_ref[...]
    ms = jnp.mean(x * x, axis=-1, keepdims=True)
    hn = x * lax.rsqrt(ms + EPS) * g_ref[...]
    logits = jnp.dot(hn, rw_ref[...], preferred_element_type=F32, precision=lax.Precision.HIGHEST)
    lane = lax.broadcasted_iota(jnp.int32, logits.shape, 1)
    lane_f = lane.astype(F32)
    logits = jnp.where(lane < N_EXPERTS, logits, -jnp.inf)
    m1 = jnp.max(logits, axis=-1, keepdims=True)
    i1 = jnp.min(jnp.where(logits == m1, lane_f, 128.0), axis=-1, keepdims=True)
    rest = jnp.where(lane_f == i1, -jnp.inf, logits)
    m2 = jnp.max(rest, axis=-1, keepdims=True)
    i2 = jnp.min(jnp.where(rest == m2, lane_f, 128.0), axis=-1, keepdims=True)
    e2 = jnp.exp(m2 - m1)
    w1 = 1.0 / (1.0 + e2)
    w2 = e2 / (1.0 + e2)
    row = (i % tiles_per_seq) * tm + lax.broadcasted_iota(jnp.int32, (tm, 1), 0)
    real = row >= T_PAD
    hot1 = jnp.where(jnp.logical_and(real, lane_f == i1), 1.0, 0.0)
    hot2 = jnp.where(jnp.logical_and(real, lane_f == i2), 1.0, 0.0)
    both = hot1 + hot2
    rr = lax.broadcasted_iota(jnp.int32, (tm, tm), 0)
    cc = lax.broadcasted_iota(jnp.int32, (tm, tm), 1)
    earlier = jnp.where(cc < rr, 1.0, 0.0).astype(BF16)
    before = carry_ref[...] + jnp.dot(earlier, both.astype(BF16), preferred_element_type=F32)
    r1 = jnp.sum(before * hot1, axis=-1, keepdims=True)
    r2 = jnp.sum(before * hot2, axis=-1, keepdims=True)
    carry_ref[...] += jnp.sum(both, axis=0, keepdims=True)
    cnt_ref[...] = carry_ref[...]
    packed = jnp.where(lane == 0, i1, jnp.where(lane == 1, i2, jnp.where(lane == 2, r1, jnp.where(lane == 3, r2, 0.0))))
    idx_ref[...] = packed.astype(jnp.int32)
    wt_ref[...] = jnp.where(lane == 0, w1, jnp.where(lane == 1, w2, 0.0))


def _router(hs, gain, router_w, tp):
    rows = hs.shape[0]
    tm = _row_tile(tp, 384)
    rw = jnp.zeros((D_MODEL, 128), F32).at[:, :N_EXPERTS].set(router_w.astype(F32))
    blk = pl.BlockSpec((tm, 128), lambda i: (i, 0))
    return pl.pallas_call(
        functools.partial(_router_kernel, tm=tm, tiles_per_seq=tp // tm),
        out_shape=(jax.ShapeDtypeStruct((rows, 128), jnp.int32), jax.ShapeDtypeStruct((rows, 128), F32),
                   jax.ShapeDtypeStruct((1, 128), F32)),
        grid=(rows // tm,),
        in_specs=[pl.BlockSpec((tm, D_MODEL), lambda i: (i, 0)),
                  pl.BlockSpec((1, D_MODEL), lambda i: (0, 0)),
                  pl.BlockSpec((D_MODEL, 128), lambda i: (0, 0))],
        out_specs=(blk, blk, pl.BlockSpec((1, 128), lambda i: (0, 0))),
        scratch_shapes=[pltpu.VMEM((1, 128), F32)],
        compiler_params=_params(("arbitrary",)),
        name="moe_router",
    )(hs, gain.reshape(1, D_MODEL), rw)


def _dispatch_kernel(s1_ref, s2_ref, zr_ref, hs_ref, wt_ref, g_ref, xs_ref, rows_ref, zero_ref, sem, zsem,
                     *, td, tg, tiles_per_seq, n_steps):
    n = pl.program_id(0) * tiles_per_seq + pl.program_id(1)
    buf = n % 2
    base = n * td

    def row_copy(r, which):
        slot = (s1_ref, s2_ref)[which][base + r]
        return pltpu.make_async_copy(rows_ref.at[buf, which, pl.ds(r, 1)], xs_ref.at[pl.ds(slot, 1)],
                                     sem.at[buf])

    def wait_rows(b):
        def body(r, c):
            pltpu.make_async_copy(rows_ref.at[b, 0, pl.ds(0, 1)], xs_ref.at[pl.ds(0, 1)], sem.at[b]).wait()
            return c
        lax.fori_loop(0, 2 * td, body, 0)

    @pl.when(n == 0)
    def _():
        zero_ref[...] = jnp.zeros_like(zero_ref)
        for e in range(2 * N_EXPERTS):
            fill = pltpu.make_async_copy(zero_ref, xs_ref.at[pl.ds(pl.multiple_of(zr_ref[e], tg), tg)], zsem)
            fill.start()
            fill.wait()

    @pl.when(n >= 2)
    def _():
        wait_rows(buf)

    x = hs_ref[...]
    ms = jnp.mean(x * x, axis=-1, keepdims=True)
    hn = x * lax.rsqrt(ms + EPS) * g_ref[...]
    wt = wt_ref[...]
    for which in range(2):
        rows_ref[buf, which, :, :D_MODEL] = hn
        rows_ref[buf, which, :, D_MODEL:] = jnp.broadcast_to(wt[:, which:which + 1], (td, XS_EXTRA))

    def issue(r, c):
        row_copy(r, 0).start()
        row_copy(r, 1).start()
        return c

    lax.fori_loop(0, td, issue, 0)

    @pl.when(n == n_steps - 1)
    def _():
        wait_rows(buf)
        if n_steps > 1:
            wait_rows(1 - buf)


def _dispatch(hs3, wts3, gain, slot1, slot2, zero_rows, n_slots, tg):
    bsz, tp, _ = hs3.shape
    td = T_PAD
    tiles_per_seq = (tp - T_PAD) // td
    width = D_MODEL + XS_EXTRA
    kern = functools.partial(_dispatch_kernel, td=td, tg=tg, tiles_per_seq=tiles_per_seq,
                             n_steps=bsz * tiles_per_seq)
    return pl.pallas_call(
        kern,
        out_shape=jax.ShapeDtypeStruct((n_slots, width), F32),
        grid_spec=pltpu.PrefetchScalarGridSpec(
            num_scalar_prefetch=3,
            grid=(bsz, tiles_per_seq),
            in_specs=[pl.BlockSpec((None, td, D_MODEL), lambda b, i, *_: (b, i + 1, 0)),
                      pl.BlockSpec((None, td, 128), lambda b, i, *_: (b, i + 1, 0)),
                      pl.BlockSpec((1, D_MODEL), lambda b, i, *_: (0, 0))],
            out_specs=pl.BlockSpec(memory_space=pl.ANY),
            scratch_shapes=[pltpu.VMEM((2, 2, td, width), F32), pltpu.VMEM((tg, width), F32),
                            pltpu.SemaphoreType.DMA((2,)), pltpu.SemaphoreType.DMA],
        ),
        compiler_params=_params(("arbitrary", "arbitrary")),
        name="moe_dispatch",
    )(slot1, slot2, zero_rows, hs3, wts3, gain.reshape(1, D_MODEL))


def _expert_kernel(te_ref, nt_ref, x_ref, wg_ref, wu_ref, wd_ref, o_ref, hn_ref, acc_ref, *, n_f):
    t = pl.program_id(0)
    f = pl.program_id(1)

    @pl.when(t < nt_ref[0])
    def _():
        @pl.when(f == 0)
        def _():
            hn_ref[...] = x_ref[:, :D_MODEL].astype(BF16)

        hn = hn_ref[...]
        a = jnp.dot(hn, wg_ref[...], preferred_element_type=F32)
        u = jnp.dot(hn, wu_ref[...], preferred_element_type=F32)
        h = (a * jax.nn.sigmoid(a) * u).astype(BF16)
        y = jnp.dot(h, wd_ref[...], preferred_element_type=F32)

        @pl.when(f == 0)
        def _():
            acc_ref[...] = y

        @pl.when(f > 0)
        def _():
            acc_ref[...] += y

    @pl.when(jnp.logical_and(t >= nt_ref[0], f == 0))
    def _():
        acc_ref[...] = jnp.zeros_like(acc_ref)

    @pl.when(f == n_f - 1)
    def _():
        o_ref[...] = (acc_ref[...] * x_ref[:, D_MODEL:D_MODEL + 1]).astype(BF16)


def _experts(xs, tile_expert, n_tiles_used, wg_bf, wu_bf, wd_bf, tg):
    slots = xs.shape[0]
    d_ff = wg_bf.shape[2]
    n_f = 2
    tf = d_ff // n_f

    def x_map(t, f, te, nt):
        return (jnp.minimum(t, nt[0] - 1), 0)

    def f_of(t, f, nt):
        return jnp.where(t < nt[0], f, n_f - 1)

    return pl.pallas_call(
        functools.partial(_expert_kernel, n_f=n_f),
        out_shape=jax.ShapeDtypeStruct((slots, D_MODEL), BF16),
        grid_spec=pltpu.PrefetchScalarGridSpec(
            num_scalar_prefetch=2,
            grid=(slots // tg, n_f),
            in_specs=[
                pl.BlockSpec((tg, D_MODEL + XS_EXTRA), x_map),
                pl.BlockSpec((None, D_MODEL, tf), lambda t, f, te, nt: (te[t], 0, f_of(t, f, nt))),
                pl.BlockSpec((None, D_MODEL, tf), lambda t, f, te, nt: (te[t], 0, f_of(t, f, nt))),
                pl.BlockSpec((None, tf, D_MODEL), lambda t, f, te, nt: (te[t], f_of(t, f, nt), 0)),
            ],
            out_specs=pl.BlockSpec((tg, D_MODEL), lambda t, f, te, nt: (t, 0)),
            scratch_shapes=[pltpu.VMEM((tg, D_MODEL), BF16), pltpu.VMEM((tg, D_MODEL), F32)],
        ),
        compiler_params=_params(("arbitrary", "arbitrary")),
        name="moe_experts",
    )(tile_expert, n_tiles_used, xs, wg_bf, wu_bf, wd_bf)


def _combine_kernel(ws_ref, hs_ref, route_ref, y_ref, o_ref, win_ref, sem, *, tc, tiles_per_seq, n_steps):
    n = pl.program_id(0) * tiles_per_seq + pl.program_id(1)
    cur = n % 2

    def window_copy(step, b, e):
        start = pl.multiple_of(ws_ref[step * N_EXPERTS + e], WIN_ALIGN)
        return pltpu.make_async_copy(y_ref.at[pl.ds(start, COMBINE_WIN)],
                                     win_ref.at[b, pl.ds(e * COMBINE_WIN, COMBINE_WIN)], sem.at[b])

    def fetch(step, b):
        for e in range(N_EXPERTS):
            window_copy(step, b, e).start()

    @pl.when(n == 0)
    def _():
        fetch(0, 0)

    @pl.when(n + 1 < n_steps)
    def _():
        fetch(n + 1, 1 - cur)

    for e in range(N_EXPERTS):
        window_copy(n, cur, e).wait()

    route = route_ref[...]
    e1, e2, s1, s2 = (route[:, c:c + 1] for c in range(4))
    pos = lax.broadcasted_iota(jnp.int32, (1, COMBINE_WIN), 1)
    picks = []
    for e in range(N_EXPERTS):
        start = ws_ref[n * N_EXPERTS + e]
        col = jnp.where(e1 == e, s1 - start, jnp.where(e2 == e, s2 - start, -1))
        picks.append(jnp.where(col == pos, 1.0, 0.0).astype(BF16))
    pick = jnp.concatenate(picks, axis=1)
    o_ref[...] = hs_ref[...] + jnp.dot(pick, win_ref[cur], preferred_element_type=F32)


def _combine(hs3, route, y, win_start, seq):
    bsz = hs3.shape[0]
    tc = COMBINE_TOKENS
    tiles_per_seq = seq // tc
    kern = functools.partial(_combine_kernel, tc=tc, tiles_per_seq=tiles_per_seq, n_steps=bsz * tiles_per_seq)
    tok = lambda w: pl.BlockSpec((tc, w), lambda b, i, ws: (b * tiles_per_seq + i, 0))
    return pl.pallas_call(
        kern,
        out_shape=jax.ShapeDtypeStruct((bsz, seq, D_MODEL), F32),
        grid_spec=pltpu.PrefetchScalarGridSpec(
            num_scalar_prefetch=1,
            grid=(bsz, tiles_per_seq),
            in_specs=[
                pl.BlockSpec((None, tc, D_MODEL), lambda b, i, ws: (b, i + T_PAD // tc, 0)),
                tok(4),
                pl.BlockSpec(memory_space=pl.ANY),
            ],
            out_specs=pl.BlockSpec((None, tc, D_MODEL), lambda b, i, ws: (b, i, 0)),
            scratch_shapes=[pltpu.VMEM((2, N_EXPERTS * COMBINE_WIN, D_MODEL), BF16),
                            pltpu.SemaphoreType.DMA((2,))],
        ),
        compiler_params=_params(("arbitrary", "arbitrary")),
        name="moe_combine",
    )(win_start, hs3, route, y)


def _moe(hs, gain, router_w, wg_bf, wu_bf, wd_bf, bsz, tp):
    seq = tp - T_PAD
    n_tok = bsz * seq
    tg = MOE_TILE
    idx, wts, cnt = _router(hs, gain, router_w, tp)
    sel = idx.reshape(bsz, tp, 128)[:, T_PAD:, :4].reshape(n_tok, 4)
    counts = cnt[0, :N_EXPERTS].astype(jnp.int32)
    padded = ((counts + tg - 1) // tg) * tg
    ends = jnp.cumsum(padded)
    starts = ends - padded
    experts = jnp.arange(N_EXPERTS, dtype=jnp.int32)[None, :]
    slot1 = jnp.sum(jnp.where(sel[:, 0:1] == experts, starts[None, :], 0), axis=1) + sel[:, 2]
    slot2 = jnp.sum(jnp.where(sel[:, 1:2] == experts, starts[None, :], 0), axis=1) + sel[:, 3]
    slot1, slot2 = slot1.astype(jnp.int32), slot2.astype(jnp.int32)
    n_slots = 2 * n_tok + N_EXPERTS * tg
    tile_start = jnp.arange(n_slots // tg, dtype=jnp.int32)[:, None] * tg
    tile_expert = jnp.minimum(jnp.sum((tile_start >= ends[None, :]).astype(jnp.int32), axis=1), N_EXPERTS - 1)
    n_tiles_used = (ends[-1:] // tg).astype(jnp.int32)
    tail = n_slots - tg * (1 + jnp.arange(N_EXPERTS, dtype=jnp.int32))
    zero_rows = jnp.concatenate([jnp.maximum(ends - tg, 0), tail]).astype(jnp.int32)

    tc = COMBINE_TOKENS
    first = jnp.minimum(
        jnp.min(jnp.where(sel[:, 0:1] == experts, slot1[:, None], n_slots).reshape(n_tok // tc, tc, N_EXPERTS), axis=1),
        jnp.min(jnp.where(sel[:, 1:2] == experts, slot2[:, None], n_slots).reshape(n_tok // tc, tc, N_EXPERTS), axis=1))
    win_start = jnp.where(first == n_slots, 0,
                          jnp.minimum(first // WIN_ALIGN * WIN_ALIGN, n_slots - COMBINE_WIN))
    route = jnp.concatenate([sel[:, :2], slot1[:, None], slot2[:, None]], axis=1)

    hs3 = hs.reshape(bsz, tp, D_MODEL)
    xs = _dispatch(hs3, wts.reshape(bsz, tp, 128), gain, slot1, slot2, zero_rows, n_slots, tg)
    y = _experts(xs, tile_expert.astype(jnp.int32), n_tiles_used, wg_bf, wu_bf, wd_bf, tg)
    return _combine(hs3, route, y, win_start.reshape(-1).astype(jnp.int32), seq)


def _permute_qk_cols(w):
    return w.reshape(D_MODEL, 2, ATT_HEADS, ATT_QK_DIM).transpose(0, 2, 1, 3).reshape(D_MODEL, 512)


def kernel(x, meta_tokens, norm1_gain, norm2_gain, w_in, q_norm_gain, k_norm_gain, diff_lambda,
           attn_sub_gain, rel_bias, conv_w, hgrn_lb_logits, hgrn_out_gain, w_branch, w_out,
           ffn_w_gate, ffn_w_up, ffn_w_down, router_w, moe_w_gate, moe_w_up, moe_w_down):
    bsz, seq, _ = x.shape
    depth = w_in.shape[0]
    tp = T_PAD + seq
    assert tp % ATT_TILE == 0 and depth == 2

    head = jnp.concatenate([jnp.zeros((PAD0, D_MODEL), x.dtype), meta_tokens.astype(x.dtype)], axis=0)
    hs = jnp.concatenate([jnp.broadcast_to(head[None], (bsz, T_PAD, D_MODEL)), x], axis=1)
    hs = hs.reshape(bsz * tp, D_MODEL)

    lb_all = jnp.cumsum(jax.nn.softmax(hgrn_lb_logits.astype(F32), axis=0), axis=0)
    lb_all = lb_all - lb_all[0]
    btab = _attn_bias_tables(rel_bias, ATT_TILE)

    out = None
    for layer in range(depth):
        w = w_in[layer]
        w_bf = jnp.concatenate([_permute_qk_cols(w[:, :512]), _permute_qk_cols(w[:, 512:1024]), w[:, 1024:]],
                               axis=1).astype(BF16)
        qk_gain = jnp.concatenate([jnp.tile(q_norm_gain[layer].astype(F32), 8) * (ATT_QK_DIM ** -0.5 * LOG2E),
                                   jnp.tile(k_norm_gain[layer].astype(F32), 8)]).reshape(1, COL_TILE)
        proj = _inproj(hs, norm1_gain[layer], w_bf, qk_gain)
        proj3 = proj.reshape(bsz, tp, IN_COLS)

        lam_init = 0.8 - 0.6 * math.exp(-0.3 * layer)
        lp = diff_lambda[layer].astype(F32)
        lam = jnp.exp(jnp.sum(lp[0] * lp[1])) - jnp.exp(jnp.sum(lp[2] * lp[3])) + lam_init
        u_att = _diff_attention(proj3, lam.reshape(1), attn_sub_gain[layer].astype(F32), btab,
                                1.0 - lam_init)

        lb = lb_all[layer]
        u_hgrn = _hgrn(proj3, jnp.log(lb), jnp.log1p(-lb), hgrn_out_gain[layer].astype(F32))

        hs = _merge(hs, u_att.reshape(bsz * tp, 512), u_hgrn.reshape(bsz * tp, 512), proj,
                    conv_w[layer].astype(F32), w_branch[layer].astype(BF16), w_out[layer].astype(BF16), tp)

        j = layer // 2
        if layer % 2 == 0:
            hs = _ffn(hs, norm2_gain[layer], ffn_w_gate[j].astype(BF16), ffn_w_up[j].astype(BF16),
                      ffn_w_down[j].astype(BF16))
        else:
            out = _moe(hs, norm2_gain[layer], router_w[j], moe_w_gate[j].astype(BF16),
                       moe_w_up[j].astype(BF16), moe_w_down[j].astype(BF16), bsz, tp)
    return out
```

```python
import functools
import math

import numpy as np
import jax
import jax.numpy as jnp
from jax import lax
from jax.experimental import pallas as pl
from jax.experimental.pallas import tpu as pltpu

F32 = jnp.float32
BF16 = jnp.bfloat16

D_MODEL = 1024
N_META = 16
EPS = 1e-6
ATT_HEADS = 4
ATT_QK_DIM = 64
ATT_V_DIM = 128
REL_BUCKETS = 32
REL_MAX_DIST = 128
CONV_K = 3
HGRN_HEADS = 4
HGRN_D = 128
N_EXPERTS = 8
IN_COLS = 8192

T_PAD = 128
PAD0 = T_PAD - N_META
ATT_TILE = 384
VT_ONES = 16
LOG2E = math.log2(math.e)
HGRN_CHUNK = 128
HGRN_SUB = 16
COL_TILE = 1024
GROUP_MEAN_WIDTH = 256
MOE_TILE = 512
XS_EXTRA = 128
COMBINE_TOKENS = 128
WIN_ALIGN = 16
COMBINE_WIN = 256
MASK = -1e30
VMEM_LIMIT = 56 * 1024 * 1024

BLK_Q, BLK_K, BLK_V = 0, 4, 8
BLK_RQ, BLK_RF, BLK_RI, BLK_RG = 24, 28, 32, 36
BLK_CB, BLK_CC, BLK_CH = 3, 4, 5
BLK_GATE = 5


def _row_tile(rows, target):
    n = rows // 128
    best = 1
    for d in range(1, n + 1):
        if n % d == 0 and d * 128 <= target:
            best = d
    return best * 128


def _params(sem, vmem=VMEM_LIMIT):
    return pltpu.CompilerParams(dimension_semantics=sem, vmem_limit_bytes=vmem)


def _inproj_kernel(x_ref, g_ref, w_ref, qkg_ref, gm_ref, o_ref, xn_ref):
    j = pl.program_id(1)

    @pl.when(j == 0)
    def _():
        x = x_ref[...]
        ms = jnp.mean(x * x, axis=-1, keepdims=True)
        xn_ref[...] = (x * lax.rsqrt(ms + EPS) * g_ref[...]).astype(BF16)

    acc = jnp.dot(xn_ref[...], w_ref[...], preferred_element_type=F32)

    @pl.when(j == 0)
    def _():
        sq = acc * acc
        hi = sq.astype(BF16)
        lo = (sq - hi.astype(F32)).astype(BF16)
        gm = gm_ref[...]
        width = gm.shape[0]
        ms = jnp.concatenate(
            [jnp.dot(hi[:, c:c + width], gm, preferred_element_type=F32)
             + jnp.dot(lo[:, c:c + width], gm, preferred_element_type=F32)
             for c in range(0, COL_TILE, width)], axis=1)
        o_ref[...] = (acc * lax.rsqrt(ms + EPS) * qkg_ref[...]).astype(BF16)

    @pl.when(j > 0)
    def _():
        o_ref[...] = acc.astype(BF16)


def _inproj(hs, gain, w_bf, qk_gain):
    rows = hs.shape[0]
    tm = _row_tile(rows, 1536)
    n_col = IN_COLS // COL_TILE
    assert COL_TILE == 4 * ATT_HEADS * ATT_QK_DIM
    grp = np.arange(GROUP_MEAN_WIDTH) // ATT_QK_DIM
    gm = jnp.asarray((grp[:, None] == grp[None, :]).astype(np.float32) / ATT_QK_DIM, BF16)
    return pl.pallas_call(
        _inproj_kernel,
        out_shape=jax.ShapeDtypeStruct((rows, IN_COLS), BF16),
        grid=(rows // tm, n_col),
        in_specs=[
            pl.BlockSpec((tm, D_MODEL), lambda i, j: (i, 0)),
            pl.BlockSpec((1, D_MODEL), lambda i, j: (0, 0)),
            pl.BlockSpec((D_MODEL, COL_TILE), lambda i, j: (0, j)),
            pl.BlockSpec((1, COL_TILE), lambda i, j: (0, 0)),
            pl.BlockSpec((GROUP_MEAN_WIDTH, GROUP_MEAN_WIDTH), lambda i, j: (0, 0)),
        ],
        out_specs=pl.BlockSpec((tm, COL_TILE), lambda i, j: (i, j)),
        scratch_shapes=[pltpu.VMEM((tm, D_MODEL), BF16)],
        compiler_params=_params(("parallel", "arbitrary")),
        name="inproj",
    )(hs, gain.reshape(1, D_MODEL), w_bf, qk_gain, gm)


def _rel_bucket_table(n_max):
    n = np.arange(n_max, dtype=np.int64)
    max_exact = REL_BUCKETS // 2
    nf = np.maximum(n, 1).astype(np.float32)
    large = max_exact + (np.log(nf / np.float32(max_exact)) / np.float32(math.log(REL_MAX_DIST / max_exact))
                         * np.float32(REL_BUCKETS - max_exact)).astype(np.int32)
    large = np.minimum(large, REL_BUCKETS - 1)
    return np.where(n < max_exact, n, large).astype(np.int32)


def _attn_bias_tables(rel_bias, t):
    bucket = _rel_bucket_table(2 * t)
    assert np.all(bucket[t + 1:] == REL_BUCKETS - 1) and np.all(np.diff(bucket) >= 0)
    first_dist = tuple(int(np.searchsorted(bucket, b, side="left")) for b in range(REL_BUCKETS))
    return pl.pallas_call(
        functools.partial(_bias_kernel, t=t, first_dist=first_dist),
        out_shape=jax.ShapeDtypeStruct((ATT_HEADS, 6, t, t), F32),
        grid=(ATT_HEADS,),
        in_specs=[pl.BlockSpec(memory_space=pltpu.SMEM)],
        out_specs=pl.BlockSpec((None, 6, t, t), lambda h: (h, 0, 0, 0)),
        compiler_params=_params(("parallel",)),
        name="attn_bias",
    )(rel_bias.astype(F32))


def _bias_kernel(rb_ref, o_ref, *, t, first_dist):
    h = pl.program_id(0)
    key = lax.broadcasted_iota(jnp.int32, (t, t), 0)
    qry = lax.broadcasted_iota(jnp.int32, (t, t), 1)
    far = rb_ref[REL_BUCKETS - 1, h]

    def table(n):
        val = jnp.full((t, t), rb_ref[0, h] - far, F32)
        for b in range(1, REL_BUCKETS):
            val = jnp.where(n >= first_dist[b], rb_ref[b, h] - far, val)
        return val * LOG2E

    n0 = qry - key
    diag = jnp.where(n0 >= 0, table(n0), MASK)
    near = table(n0 + t)
    zero = jnp.zeros((t, t), F32)
    for kind, tab in enumerate((diag, near, zero)):
        o_ref[kind] = tab
        o_ref[kind + 3] = jnp.where(key < PAD0, MASK, tab)


def _attn_kernel(lam_ref, q_ref, k_ref, v_ref, bt_ref, sg_ref, o_ref, m_ref, acc_ref, vt_ref,
                 *, t, out_scale):
    n_t = vt_ref.shape[0]
    for j in range(n_t):
        vt_ref[j, :ATT_V_DIM, :] = v_ref[j * t:(j + 1) * t, :].astype(F32).T.astype(BF16)
        vt_ref[j, ATT_V_DIM:, :] = jnp.ones((VT_ONES, t), BF16)
    lane = lax.broadcasted_iota(jnp.int32, (1, 2 * ATT_QK_DIM), 1)
    nt = (((1,), (1,)), ((), ()))

    def tile_rows(i):
        return pl.ds(pl.multiple_of(i * t, t), t)

    def scores(i, j):
        q = q_ref[tile_rows(i), :]
        zero = jnp.zeros_like(q)
        q_cat = jnp.concatenate([jnp.where(lane < ATT_QK_DIM, q, zero), jnp.where(lane >= ATT_QK_DIM, q, zero)],
                                axis=0)
        bias = bt_ref[jnp.minimum(i - j, 2) + jnp.where(j == 0, 3, 0)]
        s = (lax.dot_general(k_ref[tile_rows(j), :], q_cat, nt, preferred_element_type=F32)
             + jnp.concatenate([bias, bias], axis=1))
        return s, jnp.max(s, axis=0, keepdims=True)

    def consume(j, scored):
        s, s_max = scored
        m_old = m_ref[...]
        m_new = jnp.maximum(m_old, s_max)
        m_ref[...] = m_new
        p = jnp.exp2(s - m_new).astype(BF16)
        acc_ref[...] = (jnp.exp2(m_old - m_new) * acc_ref[...]
                        + jnp.dot(vt_ref[j], p, preferred_element_type=F32))

    def reset():
        m_ref[...] = jnp.full(m_ref.shape, MASK, F32)
        acc_ref[...] = jnp.zeros(acc_ref.shape, F32)

    def finish(i):
        acc = acc_ref[...]
        a1, a2 = acc[:, :t], acc[:, t:]
        o_t = (a1[:ATT_V_DIM] / a1[ATT_V_DIM:ATT_V_DIM + 1]
               - lam_ref[0] * (a2[:ATT_V_DIM] / a2[ATT_V_DIM:ATT_V_DIM + 1]))
        o = o_t.T
        ms = jnp.mean(o * o, axis=-1, keepdims=True)
        y = o * lax.rsqrt(ms + EPS) * (sg_ref[...] * out_scale)
        row = i * t + lax.broadcasted_iota(jnp.int32, (t, 1), 0)
        o_ref[tile_rows(i), :] = jnp.where(row >= PAD0, y, 0.0).astype(BF16)
        reset()

    def step(_, carry):
        i, j, s = carry
        last = j == i
        ni = jnp.where(last, i + 1, i)
        nj = jnp.where(last, 0, j + 1)
        nxt = scores(jnp.minimum(ni, n_t - 1), nj)
        consume(j, s)

        @pl.when(last)
        def _():
            finish(i)

        return ni, nj, nxt

    reset()
    first = jnp.int32(0)
    lax.fori_loop(0, n_t * (n_t + 1) // 2, step, (first, first, scores(first, first)))


def _diff_attention(proj3, lam, sub_gain, btab, out_scale):
    bsz, tp, _ = proj3.shape
    t = ATT_TILE
    n_t = tp // t
    rows = ATT_V_DIM + VT_ONES
    kern = functools.partial(_attn_kernel, t=t, out_scale=out_scale)
    seq = lambda blk: pl.BlockSpec((None, tp, 128), lambda b, h: (b, 0, blk + h))
    return pl.pallas_call(
        kern,
        out_shape=jax.ShapeDtypeStruct((bsz, tp, ATT_HEADS * ATT_V_DIM), BF16),
        grid=(bsz, ATT_HEADS),
        in_specs=[
            pl.BlockSpec(memory_space=pltpu.SMEM),
            seq(BLK_Q), seq(BLK_K), seq(BLK_V),
            pl.BlockSpec((None, 6, t, t), lambda b, h: (h, 0, 0, 0)),
            pl.BlockSpec((1, ATT_V_DIM), lambda b, h: (0, 0)),
        ],
        out_specs=pl.BlockSpec((None, tp, 128), lambda b, h: (b, 0, h)),
        scratch_shapes=[pltpu.VMEM((1, 2 * t), F32), pltpu.VMEM((rows, 2 * t), F32),
                        pltpu.VMEM((n_t, rows, t), BF16)],
        compiler_params=_params(("parallel", "parallel")),
        name="diff_attn",
    )(lam, proj3, proj3, proj3, btab, sub_gain.reshape(1, ATT_V_DIM))


def _split3(x):
    h1 = x.astype(BF16)
    r1 = x - h1.astype(F32)
    h2 = r1.astype(BF16)
    h3 = (r1 - h2.astype(F32)).astype(BF16)
    return h1, h2, h3


def _hgrn_kernel(q_ref, f_ref, i_ref, g_ref, la_ref, l1m_ref, og_ref, o_ref):
    c_len, sub = HGRN_CHUNK, HGRN_SUB
    half = sub // 2
    n_chunk = q_ref.shape[0] // c_len
    la, l1m, og = la_ref[...], l1m_ref[...], og_ref[...]
    rr = lax.broadcasted_iota(jnp.int32, (c_len, c_len), 0)
    cc = lax.broadcasted_iota(jnp.int32, (c_len, c_len), 1)
    tri = jnp.where(cc <= rr, 1.0, 0.0).astype(BF16)
    row8 = lax.broadcasted_iota(jnp.int32, (half, 1), 0)
    lane8 = lax.broadcasted_iota(jnp.int32, (half, c_len), 1)
    nt = (((1,), (1,)), ((), ()))

    def chunk(c, st):
        r0 = pl.multiple_of(c * c_len, c_len)
        rows = pl.ds(r0, c_len)
        z = f_ref[rows, :].astype(F32)
        qh = q_ref[rows, :].astype(F32)
        qh = qh * jax.nn.sigmoid(qh)
        v = i_ref[rows, :]
        gate = g_ref[rows, :].astype(F32)
        sp = jnp.log(1.0 + jnp.exp(-jnp.abs(z)))
        bb = l1m + jnp.minimum(z, 0.0) - sp
        log_f = jnp.maximum(la, bb) + jnp.log(1.0 + jnp.exp(-jnp.abs(la - bb)))
        valid = (r0 + lax.broadcasted_iota(jnp.int32, (c_len, 1), 0)) >= PAD0
        log_k = jnp.where(valid, l1m + jnp.minimum(-z, 0.0) - sp, -jnp.inf)
        g = sum(jnp.dot(tri, part, preferred_element_type=F32) for part in _split3(log_f))
        ck = log_k - g
        o_inter = lax.dot_general((qh * jnp.exp(g)).astype(BF16), st.astype(BF16), nt,
                                  preferred_element_type=F32)
        a_rows = []
        for a in range(c_len // sub):
            lo = a * sub
            ga = (g[lo:lo + half, :], g[lo + half:lo + sub, :])
            qa = (qh[lo:lo + half, :], qh[lo + half:lo + sub, :])
            if a == 0:
                blk = [jnp.zeros((half, c_len), F32)] * 2
            else:
                gs = g[lo - 1:lo, :]
                qd = (qh[lo:lo + sub, :] * jnp.exp(g[lo:lo + sub, :] - gs)).astype(BF16)
                kd = jnp.exp(jnp.minimum(gs - g[:lo, :], 0.0) + log_k[:lo, :]).astype(BF16)
                kd = jnp.concatenate([kd, jnp.zeros((c_len - lo, HGRN_D), BF16)], axis=0)
                a_off = lax.dot_general(qd, kd, nt, preferred_element_type=F32)
                blk = [a_off[:half, :], a_off[half:, :]]
            for s in range(sub):
                crow = ck[lo + s:lo + s + 1, :]
                for hh in range(s // half, 2):
                    col = jnp.sum(qa[hh] * jnp.exp(ga[hh] + crow), axis=-1, keepdims=True)
                    blk[hh] = jnp.where(lane8 == lo + s, col, blk[hh])
            for hh in range(2):
                a_rows.append(jnp.where(lane8 <= lo + hh * half + row8, blk[hh], 0.0))
        a_full = jnp.concatenate(a_rows, axis=0).astype(BF16)
        o = o_inter + jnp.dot(a_full, v, preferred_element_type=F32)
        g_last = g[c_len - 1:c_len, :]
        kd = jnp.exp(g_last - g + log_k).astype(BF16)
        st = st * jnp.exp(g_last) + lax.dot_general(v, kd, (((0,), (0,)), ((), ())),
                                                    preferred_element_type=F32)
        ms = jnp.mean(o * o, axis=-1, keepdims=True)
        y = o * lax.rsqrt(ms + EPS) * og * (gate * jax.nn.sigmoid(gate))
        o_ref[rows, :] = y.astype(BF16)
        return st

    lax.fori_loop(0, n_chunk, chunk, jnp.zeros((HGRN_D, HGRN_D), F32), unroll=3)


def _hgrn(proj3, log_lb, log1m_lb, out_gain):
    bsz, tp, _ = proj3.shape
    seq = lambda blk: pl.BlockSpec((None, tp, 128), lambda b, h: (b, 0, blk + h))
    chan = pl.BlockSpec((None, 1, HGRN_D), lambda b, h: (h, 0, 0))
    return pl.pallas_call(
        _hgrn_kernel,
        out_shape=jax.ShapeDtypeStruct((bsz, tp, HGRN_HEADS * HGRN_D), BF16),
        grid=(bsz, HGRN_HEADS),
        in_specs=[seq(BLK_RQ), seq(BLK_RF), seq(BLK_RI), seq(BLK_RG), chan, chan,
                  pl.BlockSpec((1, HGRN_D), lambda b, h: (0, 0))],
        out_specs=pl.BlockSpec((None, tp, 128), lambda b, h: (b, 0, h)),
        compiler_params=_params(("parallel", "parallel")),
        name="hgrn2",
    )(proj3, proj3, proj3, proj3,
      log_lb.reshape(HGRN_HEADS, 1, HGRN_D), log1m_lb.reshape(HGRN_HEADS, 1, HGRN_D),
      out_gain.reshape(1, HGRN_D))


def _merge_kernel(hs_ref, ua_ref, ur_ref, cb_ref, cc_ref, ch_ref, pc_ref, ph_ref,
                  g0_ref, g1_ref, g2_ref, cw_ref, wb_ref, wo_ref, o_ref, *, tm, tiles_per_seq):
    i = pl.program_id(0)
    row = (i % tiles_per_seq) * tm + lax.broadcasted_iota(jnp.int32, (tm, 1), 0)
    valid = row >= PAD0
    z = jnp.where(valid, cc_ref[...].astype(F32) * ch_ref[...].astype(F32), 0.0)
    halo_row = (i % tiles_per_seq) * tm - 8 + lax.broadcasted_iota(jnp.int32, (8, 1), 0)
    zp = jnp.where(halo_row >= PAD0, pc_ref[...].astype(F32) * ph_ref[...].astype(F32), 0.0)
    zz = jnp.concatenate([zp, z], axis=0)
    cw = cw_ref[...]
    y = (cw[2:3, :] * z + cw[1:2, :] * zz[7:7 + tm, :] + cw[0:1, :] * zz[6:6 + tm, :])
    u_conv = jnp.where(valid, cb_ref[...].astype(F32) * y, 0.0).astype(BF16)
    mixed = jnp.zeros((tm, D_MODEL), F32)
    for n, (u, g_ref) in enumerate(((ua_ref[...], g0_ref), (u_conv, g1_ref), (ur_ref[...], g2_ref))):
        up = jnp.dot(u, wb_ref[n], preferred_element_type=F32)
        mixed = mixed + jax.nn.sigmoid(g_ref[...].astype(F32)) * up
    o_ref[...] = hs_ref[...] + jnp.dot(mixed.astype(BF16), wo_ref[...], preferred_element_type=F32)


def _merge(hs, u_att, u_hgrn, proj, conv_w, wb_bf, wo_bf, tp):
    rows = hs.shape[0]
    tm = _row_tile(tp, 384)
    kern = functools.partial(_merge_kernel, tm=tm, tiles_per_seq=tp // tm)
    row_blk = lambda w, blk: pl.BlockSpec((tm, w), lambda i: (i, blk))
    halo = lambda blk: pl.BlockSpec((8, 512), lambda i: (jnp.maximum(i * (tm // 8) - 1, 0), blk))
    const = lambda shape: pl.BlockSpec(shape, lambda i: (0,) * len(shape))
    return pl.pallas_call(
        kern,
        out_shape=jax.ShapeDtypeStruct((rows, D_MODEL), F32),
        grid=(rows // tm,),
        in_specs=[row_blk(D_MODEL, 0), row_blk(512, 0), row_blk(512, 0),
                  row_blk(512, BLK_CB), row_blk(512, BLK_CC), row_blk(512, BLK_CH),
                  halo(BLK_CC), halo(BLK_CH),
                  row_blk(1024, BLK_GATE), row_blk(1024, BLK_GATE + 1), row_blk(1024, BLK_GATE + 2),
                  const((CONV_K, 512)), const((3, 512, D_MODEL)), const((D_MODEL, D_MODEL))],
        out_specs=row_blk(D_MODEL, 0),
        compiler_params=_params(("parallel",)),
        name="merge",
    )(hs, u_att, u_hgrn, proj, proj, proj, proj, proj, proj, proj, proj, conv_w, wb_bf, wo_bf)


def _ffn_kernel(hs_ref, g_ref, wg_ref, wu_ref, wd_ref, o_ref):
    x = hs_ref[...]
    ms = jnp.mean(x * x, axis=-1, keepdims=True)
    hn = (x * lax.rsqrt(ms + EPS) * g_ref[...]).astype(BF16)
    a = jnp.dot(hn, wg_ref[...], preferred_element_type=F32)
    u = jnp.dot(hn, wu_ref[...], preferred_element_type=F32)
    h = (a * jax.nn.sigmoid(a) * u).astype(BF16)
    o_ref[...] = x + jnp.dot(h, wd_ref[...], preferred_element_type=F32)


def _ffn(hs, gain, wg_bf, wu_bf, wd_bf):
    rows = hs.shape[0]
    d_ff = wg_bf.shape[1]
    tm = _row_tile(rows, 768)
    const = lambda shape: pl.BlockSpec(shape, lambda i: (0, 0), pipeline_mode=pl.Buffered(1))
    return pl.pallas_call(
        _ffn_kernel,
        out_shape=jax.ShapeDtypeStruct((rows, D_MODEL), F32),
        grid=(rows // tm,),
        in_specs=[pl.BlockSpec((tm, D_MODEL), lambda i: (i, 0)), const((1, D_MODEL)),
                  const((D_MODEL, d_ff)), const((D_MODEL, d_ff)), const((d_ff, D_MODEL))],
        out_specs=pl.BlockSpec((tm, D_MODEL), lambda i: (i, 0)),
        compiler_params=_params(("parallel",)),
        name="ffn_dense",
    )(hs, gain.reshape(1, D_MODEL), wg_bf, wu_bf, wd_bf)


def _router_kernel(hs_ref, g_ref, rw_ref, idx_ref, wt_ref, cnt_ref, carry_ref, *, tm, tiles_per_seq):
    i = pl.program_id(0)

    @pl.when(i == 0)
    def _():
        carry_ref[...] = jnp.zeros_like(carry_ref)

    x = hs_ref[...]
    ms = jnp.mean(x * x, axis=-1, keepdims=True)
    hn = x * lax.rsqrt(ms + EPS) * g_ref[...]
    logits = jnp.dot(hn, rw_ref[...], preferred_element_type=F32, precision=lax.Precision.HIGHEST)
    lane = lax.broadcasted_iota(jnp.int32, logits.shape, 1)
    lane_f = lane.astype(F32)
    logits = jnp.where(lane < N_EXPERTS, logits, -jnp.inf)
    m1 = jnp.max(logits, axis=-1, keepdims=True)
    i1 = jnp.min(jnp.where(logits == m1, lane_f, 128.0), axis=-1, keepdims=True)
    rest = jnp.where(lane_f == i1, -jnp.inf, logits)
    m2 = jnp.max(rest, axis=-1, keepdims=True)
    i2 = jnp.min(jnp.where(rest == m2, lane_f, 128.0), axis=-1, keepdims=True)
    e2 = jnp.exp(m2 - m1)
    w1 = 1.0 / (1.0 + e2)
    w2 = e2 / (1.0 + e2)
    row = (i % tiles_per_seq) * tm + lax.broadcasted_iota(jnp.int32, (tm, 1), 0)
    real = row >= T_PAD
    hot1 = jnp.where(jnp.logical_and(real, lane_f == i1), 1.0, 0.0)
    hot2 = jnp.where(jnp.logical_and(real, lane_f == i2), 1.0, 0.0)
    both = hot1 + hot2
    rr = lax.broadcasted_iota(jnp.int32, (tm, tm), 0)
    cc = lax.broadcasted_iota(jnp.int32, (tm, tm), 1)
    earlier = jnp.where(cc < rr, 1.0, 0.0).astype(BF16)
    before = carry_ref[...] + jnp.dot(earlier, both.astype(BF16), preferred_element_type=F32)
    r1 = jnp.sum(before * hot1, axis=-1, keepdims=True)
    r2 = jnp.sum(before * hot2, axis=-1, keepdims=True)
    carry_ref[...] += jnp.sum(both, axis=0, keepdims=True)
    cnt_ref[...] = carry_ref[...]
    packed = jnp.where(lane == 0, i1, jnp.where(lane == 1, i2, jnp.where(lane == 2, r1, jnp.where(lane == 3, r2, 0.0))))
    idx_ref[...] = packed.astype(jnp.int32)
    wt_ref[...] = jnp.where(lane == 0, w1, jnp.where(lane == 1, w2, 0.0))


def _router(hs, gain, router_w, tp):
    rows = hs.shape[0]
    tm = _row_tile(tp, 384)
    rw = jnp.zeros((D_MODEL, 128), F32).at[:, :N_EXPERTS].set(router_w.astype(F32))
    blk = pl.BlockSpec((tm, 128), lambda i: (i, 0))
    return pl.pallas_call(
        functools.partial(_router_kernel, tm=tm, tiles_per_seq=tp // tm),
        out_shape=(jax.ShapeDtypeStruct((rows, 128), jnp.int32), jax.ShapeDtypeStruct((rows, 128), F32),
                   jax.ShapeDtypeStruct((1, 128), F32)),
        grid=(rows // tm,),
        in_specs=[pl.BlockSpec((tm, D_MODEL), lambda i: (i, 0)),
                  pl.BlockSpec((1, D_MODEL), lambda i: (0, 0)),
                  pl.BlockSpec((D_MODEL, 128), lambda i: (0, 0))],
        out_specs=(blk, blk, pl.BlockSpec((1, 128), lambda i: (0, 0))),
        scratch_shapes=[pltpu.VMEM((1, 128), F32)],
        compiler_params=_params(("arbitrary",)),
        name="moe_router",
    )(hs, gain.reshape(1, D_MODEL), rw)


def _dispatch_kernel(s1_ref, s2_ref, zr_ref, hs_ref, wt_ref, g_ref, xs_ref, rows_ref, zero_ref, sem, zsem,
                     *, td, tg, tiles_per_seq, n_steps):
    n = pl.program_id(0) * tiles_per_seq + pl.program_id(1)
    buf = n % 2
    base = n * td

    def row_copy(r, which):
        slot = (s1_ref, s2_ref)[which][base + r]
        return pltpu.make_async_copy(rows_ref.at[buf, which, pl.ds(r, 1)], xs_ref.at[pl.ds(slot, 1)],
                                     sem.at[buf])

    def wait_rows(b):
        def body(r, c):
            pltpu.make_async_copy(rows_ref.at[b, 0, pl.ds(0, 1)], xs_ref.at[pl.ds(0, 1)], sem.at[b]).wait()
            return c
        lax.fori_loop(0, 2 * td, body, 0)

    @pl.when(n == 0)
    def _():
        zero_ref[...] = jnp.zeros_like(zero_ref)
        for e in range(2 * N_EXPERTS):
            fill = pltpu.make_async_copy(zero_ref, xs_ref.at[pl.ds(pl.multiple_of(zr_ref[e], tg), tg)], zsem)
            fill.start()
            fill.wait()

    @pl.when(n >= 2)
    def _():
        wait_rows(buf)

    x = hs_ref[...]
    ms = jnp.mean(x * x, axis=-1, keepdims=True)
    hn = x * lax.rsqrt(ms + EPS) * g_ref[...]
    wt = wt_ref[...]
    for which in range(2):
        rows_ref[buf, which, :, :D_MODEL] = hn
        rows_ref[buf, which, :, D_MODEL:] = jnp.broadcast_to(wt[:, which:which + 1], (td, XS_EXTRA))

    def issue(r, c):
        row_copy(r, 0).start()
        row_copy(r, 1).start()
        return c

    lax.fori_loop(0, td, issue, 0)

    @pl.when(n == n_steps - 1)
    def _():
        wait_rows(buf)
        if n_steps > 1:
            wait_rows(1 - buf)


def _dispatch(hs3, wts3, gain, slot1, slot2, zero_rows, n_slots, tg):
    bsz, tp, _ = hs3.shape
    td = T_PAD
    tiles_per_seq = (tp - T_PAD) // td
    width = D_MODEL + XS_EXTRA
    kern = functools.partial(_dispatch_kernel, td=td, tg=tg, tiles_per_seq=tiles_per_seq,
                             n_steps=bsz * tiles_per_seq)
    return pl.pallas_call(
        kern,
        out_shape=jax.ShapeDtypeStruct((n_slots, width), F32),
        grid_spec=pltpu.PrefetchScalarGridSpec(
            num_scalar_prefetch=3,
            grid=(bsz, tiles_per_seq),
            in_specs=[pl.BlockSpec((None, td, D_MODEL), lambda b, i, *_: (b, i + 1, 0)),
                      pl.BlockSpec((None, td, 128), lambda b, i, *_: (b, i + 1, 0)),
                      pl.BlockSpec((1, D_MODEL), lambda b, i, *_: (0, 0))],
            out_specs=pl.BlockSpec(memory_space=pl.ANY),
            scratch_shapes=[pltpu.VMEM((2, 2, td, width), F32), pltpu.VMEM((tg, width), F32),
                            pltpu.SemaphoreType.DMA((2,)), pltpu.SemaphoreType.DMA],
        ),
        compiler_params=_params(("arbitrary", "arbitrary")),
        name="moe_dispatch",
    )(slot1, slot2, zero_rows, hs3, wts3, gain.reshape(1, D_MODEL))


def _expert_kernel(te_ref, nt_ref, x_ref, wg_ref, wu_ref, wd_ref, o_ref, hn_ref, acc_ref, *, n_f):
    t = pl.program_id(0)
    f = pl.program_id(1)

    @pl.when(t < nt_ref[0])
    def _():
        @pl.when(f == 0)
        def _():
            hn_ref[...] = x_ref[:, :D_MODEL].astype(BF16)

        hn = hn_ref[...]
        a = jnp.dot(hn, wg_ref[...], preferred_element_type=F32)
        u = jnp.dot(hn, wu_ref[...], preferred_element_type=F32)
        h = (a * jax.nn.sigmoid(a) * u).astype(BF16)
        y = jnp.dot(h, wd_ref[...], preferred_element_type=F32)

        @pl.when(f == 0)
        def _():
            acc_ref[...] = y

        @pl.when(f > 0)
        def _():
            acc_ref[...] += y

    @pl.when(jnp.logical_and(t >= nt_ref[0], f == 0))
    def _():
        acc_ref[...] = jnp.zeros_like(acc_ref)

    @pl.when(f == n_f - 1)
    def _():
        o_ref[...] = (acc_ref[...] * x_ref[:, D_MODEL:D_MODEL + 1]).astype(BF16)


def _experts(xs, tile_expert, n_tiles_used, wg_bf, wu_bf, wd_bf, tg):
    slots = xs.shape[0]
    d_ff = wg_bf.shape[2]
    n_f = 2
    tf = d_ff // n_f

    def x_map(t, f, te, nt):
        return (jnp.minimum(t, nt[0] - 1), 0)

    def f_of(t, f, nt):
        return jnp.where(t < nt[0], f, n_f - 1)

    return pl.pallas_call(
        functools.partial(_expert_kernel, n_f=n_f),
        out_shape=jax.ShapeDtypeStruct((slots, D_MODEL), BF16),
        grid_spec=pltpu.PrefetchScalarGridSpec(
            num_scalar_prefetch=2,
            grid=(slots // tg, n_f),
            in_specs=[
                pl.BlockSpec((tg, D_MODEL + XS_EXTRA), x_map),
                pl.BlockSpec((None, D_MODEL, tf), lambda t, f, te, nt: (te[t], 0, f_of(t, f, nt))),
                pl.BlockSpec((None, D_MODEL, tf), lambda t, f, te, nt: (te[t], 0, f_of(t, f, nt))),
                pl.BlockSpec((None, tf, D_MODEL), lambda t, f, te, nt: (te[t], f_of(t, f, nt), 0)),
            ],
            out_specs=pl.BlockSpec((tg, D_MODEL), lambda t, f, te, nt: (t, 0)),
            scratch_shapes=[pltpu.VMEM((tg, D_MODEL), BF16), pltpu.VMEM((tg, D_MODEL), F32)],
        ),
        compiler_params=_params(("arbitrary", "arbitrary")),
        name="moe_experts",
    )(tile_expert, n_tiles_used, xs, wg_bf, wu_bf, wd_bf)


def _combine_kernel(ws_ref, hs_ref, route_ref, y_ref, o_ref, win_ref, sem, *, tc, tiles_per_seq, n_steps):
    n = pl.program_id(0) * tiles_per_seq + pl.program_id(1)
    cur = n % 2

    def window_copy(step, b, e):
        start = pl.multiple_of(ws_ref[step * N_EXPERTS + e], WIN_ALIGN)
        return pltpu.make_async_copy(y_ref.at[pl.ds(start, COMBINE_WIN)],
                                     win_ref.at[b, pl.ds(e * COMBINE_WIN, COMBINE_WIN)], sem.at[b])

    def fetch(step, b):
        for e in range(N_EXPERTS):
            window_copy(step, b, e).start()

    @pl.when(n == 0)
    def _():
        fetch(0, 0)

    @pl.when(n + 1 < n_steps)
    def _():
        fetch(n + 1, 1 - cur)

    for e in range(N_EXPERTS):
        window_copy(n, cur, e).wait()

    route = route_ref[...]
    e1, e2, s1, s2 = (route[:, c:c + 1] for c in range(4))
    pos = lax.broadcasted_iota(jnp.int32, (1, COMBINE_WIN), 1)
    picks = []
    for e in range(N_EXPERTS):
        start = ws_ref[n * N_EXPERTS + e]
        col = jnp.where(e1 == e, s1 - start, jnp.where(e2 == e, s2 - start, -1))
        picks.append(jnp.where(col == pos, 1.0, 0.0).astype(BF16))
    pick = jnp.concatenate(picks, axis=1)
    o_ref[...] = hs_ref[...] + jnp.dot(pick, win_ref[cur], preferred_element_type=F32)


def _combine(hs3, route, y, win_start, seq):
    bsz = hs3.shape[0]
    tc = COMBINE_TOKENS
    tiles_per_seq = seq // tc
    kern = functools.partial(_combine_kernel, tc=tc, tiles_per_seq=tiles_per_seq, n_steps=bsz * tiles_per_seq)
    tok = lambda w: pl.BlockSpec((tc, w), lambda b, i, ws: (b * tiles_per_seq + i, 0))
    return pl.pallas_call(
        kern,
        out_shape=jax.ShapeDtypeStruct((bsz, seq, D_MODEL), F32),
        grid_spec=pltpu.PrefetchScalarGridSpec(
            num_scalar_prefetch=1,
            grid=(bsz, tiles_per_seq),
            in_specs=[
                pl.BlockSpec((None, tc, D_MODEL), lambda b, i, ws: (b, i + T_PAD // tc, 0)),
                tok(4),
                pl.BlockSpec(memory_space=pl.ANY),
            ],
            out_specs=pl.BlockSpec((None, tc, D_MODEL), lambda b, i, ws: (b, i, 0)),
            scratch_shapes=[pltpu.VMEM((2, N_EXPERTS * COMBINE_WIN, D_MODEL), BF16),
                            pltpu.SemaphoreType.DMA((2,))],
        ),
        compiler_params=_params(("arbitrary", "arbitrary")),
        name="moe_combine",
    )(win_start, hs3, route, y)


def _moe(hs, gain, router_w, wg_bf, wu_bf, wd_bf, bsz, tp):
    seq = tp - T_PAD
    n_tok = bsz * seq
    tg = MOE_TILE
    idx, wts, cnt = _router(hs, gain, router_w, tp)
    sel = idx.reshape(bsz, tp, 128)[:, T_PAD:, :4].reshape(n_tok, 4)
    counts = cnt[0, :N_EXPERTS].astype(jnp.int32)
    padded = ((counts + tg - 1) // tg) * tg
    ends = jnp.cumsum(padded)
    starts = ends - padded
    experts = jnp.arange(N_EXPERTS, dtype=jnp.int32)[None, :]
    slot1 = jnp.sum(jnp.where(sel[:, 0:1] == experts, starts[None, :], 0), axis=1) + sel[:, 2]
    slot2 = jnp.sum(jnp.where(sel[:, 1:2] == experts, starts[None, :], 0), axis=1) + sel[:, 3]
    slot1, slot2 = slot1.astype(jnp.int32), slot2.astype(jnp.int32)
    n_slots = 2 * n_tok + N_EXPERTS * tg
    tile_start = jnp.arange(n_slots // tg, dtype=jnp.int32)[:, None] * tg
    tile_expert = jnp.minimum(jnp.sum((tile_start >= ends[None, :]).astype(jnp.int32), axis=1), N_EXPERTS - 1)
    n_tiles_used = (ends[-1:] // tg).astype(jnp.int32)
    tail = n_slots - tg * (1 + jnp.arange(N_EXPERTS, dtype=jnp.int32))
    zero_rows = jnp.concatenate([jnp.maximum(ends - tg, 0), tail]).astype(jnp.int32)

    tc = COMBINE_TOKENS
    first = jnp.minimum(
        jnp.min(jnp.where(sel[:, 0:1] == experts, slot1[:, None], n_slots).reshape(n_tok // tc, tc, N_EXPERTS), axis=1),
        jnp.min(jnp.where(sel[:, 1:2] == experts, slot2[:, None], n_slots).reshape(n_tok // tc, tc, N_EXPERTS), axis=1))
    win_start = jnp.where(first == n_slots, 0,
                          jnp.minimum(first // WIN_ALIGN * WIN_ALIGN, n_slots - COMBINE_WIN))
    route = jnp.concatenate([sel[:, :2], slot1[:, None], slot2[:, None]], axis=1)

    hs3 = hs.reshape(bsz, tp, D_MODEL)
    xs = _dispatch(hs3, wts.reshape(bsz, tp, 128), gain, slot1, slot2, zero_rows, n_slots, tg)
    y = _experts(xs, tile_expert.astype(jnp.int32), n_tiles_used, wg_bf, wu_bf, wd_bf, tg)
    return _combine(hs3, route, y, win_start.reshape(-1).astype(jnp.int32), seq)


def _permute_qk_cols(w):
    return w.reshape(D_MODEL, 2, ATT_HEADS, ATT_QK_DIM).transpose(0, 2, 1, 3).reshape(D_MODEL, 512)


def kernel(x, meta_tokens, norm1_gain, norm2_gain, w_in, q_norm_gain, k_norm_gain, diff_lambda,
           attn_sub_gain, rel_bias, conv_w, hgrn_lb_logits, hgrn_out_gain, w_branch, w_out,
           ffn_w_gate, ffn_w_up, ffn_w_down, router_w, moe_w_gate, moe_w_up, moe_w_down):
    bsz, seq, _ = x.shape
    depth = w_in.shape[0]
    tp = T_PAD + seq
    assert tp % ATT_TILE == 0 and depth == 2

    head = jnp.concatenate([jnp.zeros((PAD0, D_MODEL), x.dtype), meta_tokens.astype(x.dtype)], axis=0)
    hs = jnp.concatenate([jnp.broadcast_to(head[None], (bsz, T_PAD, D_MODEL)), x], axis=1)
    hs = hs.reshape(bsz * tp, D_MODEL)

    lb_all = jnp.cumsum(jax.nn.softmax(hgrn_lb_logits.astype(F32), axis=0), axis=0)
    lb_all = lb_all - lb_all[0]
    btab = _attn_bias_tables(rel_bias, ATT_TILE)

    out = None
    for layer in range(depth):
        w = w_in[layer]
        w_bf = jnp.concatenate([_permute_qk_cols(w[:, :512]), _permute_qk_cols(w[:, 512:1024]), w[:, 1024:]],
                               axis=1).astype(BF16)
        qk_gain = jnp.concatenate([jnp.tile(q_norm_gain[layer].astype(F32), 8) * (ATT_QK_DIM ** -0.5 * LOG2E),
                                   jnp.tile(k_norm_gain[layer].astype(F32), 8)]).reshape(1, COL_TILE)
        proj = _inproj(hs, norm1_gain[layer], w_bf, qk_gain)
        proj3 = proj.reshape(bsz, tp, IN_COLS)

        lam_init = 0.8 - 0.6 * math.exp(-0.3 * layer)
        lp = diff_lambda[layer].astype(F32)
        lam = jnp.exp(jnp.sum(lp[0] * lp[1])) - jnp.exp(jnp.sum(lp[2] * lp[3])) + lam_init
        u_att = _diff_attention(proj3, lam.reshape(1), attn_sub_gain[layer].astype(F32), btab,
                                1.0 - lam_init)

        lb = lb_all[layer]
        u_hgrn = _hgrn(proj3, jnp.log(lb), jnp.log1p(-lb), hgrn_out_gain[layer].astype(F32))

        hs = _merge(hs, u_att.reshape(bsz * tp, 512), u_hgrn.reshape(bsz * tp, 512), proj,
                    conv_w[layer].astype(F32), w_branch[layer].astype(BF16), w_out[layer].astype(BF16), tp)

        j = layer // 2
        if layer % 2 == 0:
            hs = _ffn(hs, norm2_gain[layer], ffn_w_gate[j].astype(BF16), ffn_w_up[j].astype(BF16),
                      ffn_w_down[j].astype(BF16))
        else:
            out = _moe(hs, norm2_gain[layer], router_w[j], moe_w_gate[j].astype(BF16),
                       moe_w_up[j].astype(BF16), moe_w_down[j].astype(BF16), bsz, tp)
    return out
```

```python
import functools
import math

import numpy as np
import jax
import jax.numpy as jnp
from jax import lax
from jax.experimental import pallas as pl
from jax.experimental.pallas import tpu as pltpu

F32 = jnp.float32
BF16 = jnp.bfloat16

D_MODEL = 1024
N_META = 16
EPS = 1e-6
ATT_HEADS = 4
ATT_QK_DIM = 64
ATT_V_DIM = 128
REL_BUCKETS = 32
REL_MAX_DIST = 128
CONV_K = 3
HGRN_HEADS = 4
HGRN_D = 128
N_EXPERTS = 8
IN_COLS = 8192

T_PAD = 128
PAD0 = T_PAD - N_META
ATT_TILE = 384
VT_ONES = 16
LOG2E = math.log2(math.e)
HGRN_CHUNK = 128
HGRN_SUB = 16
COL_TILE = 1024
GROUP_MEAN_WIDTH = 256
MOE_TILE = 512
MOE_F_BLOCKS = 2
XS_EXTRA = 128
COMBINE_TOKENS = 128
WIN_ALIGN = 16
COMBINE_WIN = 256
MASK = -1e30
VMEM_LIMIT = 56 * 1024 * 1024

BLK_Q, BLK_K, BLK_V = 0, 4, 8
BLK_RQ, BLK_RF, BLK_RI, BLK_RG = 24, 28, 32, 36
BLK_CB, BLK_CC, BLK_CH = 3, 4, 5
BLK_GATE = 5


def _row_tile(rows, target):
    n = rows // 128
    best = 1
    for d in range(1, n + 1):
        if n % d == 0 and d * 128 <= target:
            best = d
    return best * 128


def _params(sem, vmem=VMEM_LIMIT):
    return pltpu.CompilerParams(dimension_semantics=sem, vmem_limit_bytes=vmem)


def _inproj_kernel(x_ref, g_ref, w_ref, qkg_ref, gm_ref, o_ref, xn_ref):
    j = pl.program_id(1)

    @pl.when(j == 0)
    def _():
        x = x_ref[...]
        ms = jnp.mean(x * x, axis=-1, keepdims=True)
        xn_ref[...] = (x * lax.rsqrt(ms + EPS) * g_ref[...]).astype(BF16)

    acc = jnp.dot(xn_ref[...], w_ref[...], preferred_element_type=F32)

    @pl.when(j == 0)
    def _():
        sq = acc * acc
        hi = sq.astype(BF16)
        lo = (sq - hi.astype(F32)).astype(BF16)
        gm = gm_ref[...]
        width = gm.shape[0]
        ms = jnp.concatenate(
            [jnp.dot(hi[:, c:c + width], gm, preferred_element_type=F32)
             + jnp.dot(lo[:, c:c + width], gm, preferred_element_type=F32)
             for c in range(0, COL_TILE, width)], axis=1)
        o_ref[...] = (acc * lax.rsqrt(ms + EPS) * qkg_ref[...]).astype(BF16)

    @pl.when(j > 0)
    def _():
        o_ref[...] = acc.astype(BF16)


def _inproj(hs, gain, w_bf, qk_gain):
    rows = hs.shape[0]
    tm = _row_tile(rows, 1536)
    n_col = IN_COLS // COL_TILE
    assert COL_TILE == 4 * ATT_HEADS * ATT_QK_DIM
    grp = np.arange(GROUP_MEAN_WIDTH) // ATT_QK_DIM
    gm = jnp.asarray((grp[:, None] == grp[None, :]).astype(np.float32) / ATT_QK_DIM, BF16)
    return pl.pallas_call(
        _inproj_kernel,
        out_shape=jax.ShapeDtypeStruct((rows, IN_COLS), BF16),
        grid=(rows // tm, n_col),
        in_specs=[
            pl.BlockSpec((tm, D_MODEL), lambda i, j: (i, 0)),
            pl.BlockSpec((1, D_MODEL), lambda i, j: (0, 0)),
            pl.BlockSpec((None, D_MODEL, COL_TILE), lambda i, j: (j, 0, 0)),
            pl.BlockSpec((1, COL_TILE), lambda i, j: (0, 0)),
            pl.BlockSpec((GROUP_MEAN_WIDTH, GROUP_MEAN_WIDTH), lambda i, j: (0, 0)),
        ],
        out_specs=pl.BlockSpec((tm, COL_TILE), lambda i, j: (i, j)),
        scratch_shapes=[pltpu.VMEM((tm, D_MODEL), BF16)],
        compiler_params=_params(("parallel", "arbitrary")),
        name="inproj",
    )(hs, gain.reshape(1, D_MODEL), w_bf, qk_gain, gm)


def _rel_bucket_table(n_max):
    n = np.arange(n_max, dtype=np.int64)
    max_exact = REL_BUCKETS // 2
    nf = np.maximum(n, 1).astype(np.float32)
    large = max_exact + (np.log(nf / np.float32(max_exact)) / np.float32(math.log(REL_MAX_DIST / max_exact))
                         * np.float32(REL_BUCKETS - max_exact)).astype(np.int32)
    large = np.minimum(large, REL_BUCKETS - 1)
    return np.where(n < max_exact, n, large).astype(np.int32)


def _attn_bias_tables(rel_bias, t):
    bucket = _rel_bucket_table(2 * t)
    assert np.all(bucket[t + 1:] == REL_BUCKETS - 1) and np.all(np.diff(bucket) >= 0)
    first_dist = tuple(int(np.searchsorted(bucket, b, side="left")) for b in range(REL_BUCKETS))
    return pl.pallas_call(
        functools.partial(_bias_kernel, t=t, first_dist=first_dist),
        out_shape=jax.ShapeDtypeStruct((ATT_HEADS, 6, t, t), F32),
        grid=(ATT_HEADS,),
        in_specs=[pl.BlockSpec(memory_space=pltpu.SMEM)],
        out_specs=pl.BlockSpec((None, 6, t, t), lambda h: (h, 0, 0, 0)),
        compiler_params=_params(("parallel",)),
        name="attn_bias",
    )(rel_bias.astype(F32))


def _bias_kernel(rb_ref, o_ref, *, t, first_dist):
    h = pl.program_id(0)
    key = lax.broadcasted_iota(jnp.int32, (t, t), 0)
    qry = lax.broadcasted_iota(jnp.int32, (t, t), 1)
    far = rb_ref[REL_BUCKETS - 1, h]

    def table(n):
        val = jnp.full((t, t), rb_ref[0, h] - far, F32)
        for b in range(1, REL_BUCKETS):
            val = jnp.where(n >= first_dist[b], rb_ref[b, h] - far, val)
        return val * LOG2E

    n0 = qry - key
    diag = jnp.where(n0 >= 0, table(n0), MASK)
    near = table(n0 + t)
    zero = jnp.zeros((t, t), F32)
    for kind, tab in enumerate((diag, near, zero)):
        o_ref[kind] = tab
        o_ref[kind + 3] = jnp.where(key < PAD0, MASK, tab)


def _attn_kernel(lam_ref, q_ref, k_ref, v_ref, bt_ref, sg_ref, o_ref, m_ref, acc_ref, vt_ref,
                 *, t, out_scale):
    n_t = vt_ref.shape[0]
    for j in range(n_t):
        vt_ref[j, :ATT_V_DIM, :] = v_ref[j * t:(j + 1) * t, :].astype(F32).T.astype(BF16)
        vt_ref[j, ATT_V_DIM:, :] = jnp.ones((VT_ONES, t), BF16)
    lane = lax.broadcasted_iota(jnp.int32, (1, 2 * ATT_QK_DIM), 1)
    nt = (((1,), (1,)), ((), ()))

    def tile_rows(i):
        return pl.ds(pl.multiple_of(i * t, t), t)

    def scores(i, j):
        q = q_ref[tile_rows(i), :]
        zero = jnp.zeros_like(q)
        q_cat = jnp.concatenate([jnp.where(lane < ATT_QK_DIM, q, zero), jnp.where(lane >= ATT_QK_DIM, q, zero)],
                                axis=0)
        bias = bt_ref[jnp.minimum(i - j, 2) + jnp.where(j == 0, 3, 0)]
        s = (lax.dot_general(k_ref[tile_rows(j), :], q_cat, nt, preferred_element_type=F32)
             + jnp.concatenate([bias, bias], axis=1))
        return s, jnp.max(s, axis=0, keepdims=True)

    def consume(j, scored):
        s, s_max = scored
        m_old = m_ref[...]
        m_new = jnp.maximum(m_old, s_max)
        m_ref[...] = m_new
        p = jnp.exp2(s - m_new).astype(BF16)
        acc_ref[...] = (jnp.exp2(m_old - m_new) * acc_ref[...]
                        + jnp.dot(vt_ref[j], p, preferred_element_type=F32))

    def reset():
        m_ref[...] = jnp.full(m_ref.shape, MASK, F32)
        acc_ref[...] = jnp.zeros(acc_ref.shape, F32)

    def finish(i):
        acc = acc_ref[...]
        a1, a2 = acc[:, :t], acc[:, t:]
        o_t = (a1[:ATT_V_DIM] / a1[ATT_V_DIM:ATT_V_DIM + 1]
               - lam_ref[0] * (a2[:ATT_V_DIM] / a2[ATT_V_DIM:ATT_V_DIM + 1]))
        o = o_t.T
        ms = jnp.mean(o * o, axis=-1, keepdims=True)
        y = o * lax.rsqrt(ms + EPS) * (sg_ref[...] * out_scale)
        row = i * t + lax.broadcasted_iota(jnp.int32, (t, 1), 0)
        o_ref[tile_rows(i), :] = jnp.where(row >= PAD0, y, 0.0).astype(BF16)
        reset()

    def step(_, carry):
        i, j, s = carry
        last = j == i
        ni = jnp.where(last, i + 1, i)
        nj = jnp.where(last, 0, j + 1)
        nxt = scores(jnp.minimum(ni, n_t - 1), nj)
        consume(j, s)

        @pl.when(last)
        def _():
            finish(i)

        return ni, nj, nxt

    reset()
    first = jnp.int32(0)
    lax.fori_loop(0, n_t * (n_t + 1) // 2, step, (first, first, scores(first, first)))


def _diff_attention(proj3, lam, sub_gain, btab, out_scale):
    bsz, tp, _ = proj3.shape
    t = ATT_TILE
    n_t = tp // t
    rows = ATT_V_DIM + VT_ONES
    kern = functools.partial(_attn_kernel, t=t, out_scale=out_scale)
    seq = lambda blk: pl.BlockSpec((None, tp, 128), lambda b, h: (b, 0, blk + h))
    return pl.pallas_call(
        kern,
        out_shape=jax.ShapeDtypeStruct((bsz, tp, ATT_HEADS * ATT_V_DIM), BF16),
        grid=(bsz, ATT_HEADS),
        in_specs=[
            pl.BlockSpec(memory_space=pltpu.SMEM),
            seq(BLK_Q), seq(BLK_K), seq(BLK_V),
            pl.BlockSpec((None, 6, t, t), lambda b, h: (h, 0, 0, 0)),
            pl.BlockSpec((1, ATT_V_DIM), lambda b, h: (0, 0)),
        ],
        out_specs=pl.BlockSpec((None, tp, 128), lambda b, h: (b, 0, h)),
        scratch_shapes=[pltpu.VMEM((1, 2 * t), F32), pltpu.VMEM((rows, 2 * t), F32),
                        pltpu.VMEM((n_t, rows, t), BF16)],
        compiler_params=_params(("parallel", "parallel")),
        name="diff_attn",
    )(lam, proj3, proj3, proj3, btab, sub_gain.reshape(1, ATT_V_DIM))


def _split3(x):
    h1 = x.astype(BF16)
    r1 = x - h1.astype(F32)
    h2 = r1.astype(BF16)
    h3 = (r1 - h2.astype(F32)).astype(BF16)
    return h1, h2, h3


def _hgrn_kernel(q_ref, f_ref, i_ref, g_ref, la_ref, l1m_ref, og_ref, o_ref):
    c_len, sub = HGRN_CHUNK, HGRN_SUB
    half = sub // 2
    n_chunk = q_ref.shape[0] // c_len
    la, l1m, og = la_ref[...], l1m_ref[...], og_ref[...]
    rr = lax.broadcasted_iota(jnp.int32, (c_len, c_len), 0)
    cc = lax.broadcasted_iota(jnp.int32, (c_len, c_len), 1)
    tri = jnp.where(cc <= rr, 1.0, 0.0).astype(BF16)
    row8 = lax.broadcasted_iota(jnp.int32, (half, 1), 0)
    lane8 = lax.broadcasted_iota(jnp.int32, (half, c_len), 1)
    nt = (((1,), (1,)), ((), ()))

    def chunk(c, st):
        r0 = pl.multiple_of(c * c_len, c_len)
        rows = pl.ds(r0, c_len)
        z = f_ref[rows, :].astype(F32)
        qh = q_ref[rows, :].astype(F32)
        qh = qh * jax.nn.sigmoid(qh)
        v = i_ref[rows, :]
        gate = g_ref[rows, :].astype(F32)
        sp = jnp.log(1.0 + jnp.exp(-jnp.abs(z)))
        bb = l1m + jnp.minimum(z, 0.0) - sp
        log_f = jnp.maximum(la, bb) + jnp.log(1.0 + jnp.exp(-jnp.abs(la - bb)))
        valid = (r0 + lax.broadcasted_iota(jnp.int32, (c_len, 1), 0)) >= PAD0
        log_k = jnp.where(valid, l1m + jnp.minimum(-z, 0.0) - sp, -jnp.inf)
        g = sum(jnp.dot(tri, part, preferred_element_type=F32) for part in _split3(log_f))
        ck = log_k - g
        o_inter = lax.dot_general((qh * jnp.exp(g)).astype(BF16), st.astype(BF16), nt,
                                  preferred_element_type=F32)
        a_rows = []
        for a in range(c_len // sub):
            lo = a * sub
            ga = (g[lo:lo + half, :], g[lo + half:lo + sub, :])
            qa = (qh[lo:lo + half, :], qh[lo + half:lo + sub, :])
            if a == 0:
                blk = [jnp.zeros((half, c_len), F32)] * 2
            else:
                gs = g[lo - 1:lo, :]
                qd = (qh[lo:lo + sub, :] * jnp.exp(g[lo:lo + sub, :] - gs)).astype(BF16)
                kd = jnp.exp(jnp.minimum(gs - g[:lo, :], 0.0) + log_k[:lo, :]).astype(BF16)
                kd = jnp.concatenate([kd, jnp.zeros((c_len - lo, HGRN_D), BF16)], axis=0)
                a_off = lax.dot_general(qd, kd, nt, preferred_element_type=F32)
                blk = [a_off[:half, :], a_off[half:, :]]
            for s in range(sub):
                crow = ck[lo + s:lo + s + 1, :]
                for hh in range(s // half, 2):
                    col = jnp.sum(qa[hh] * jnp.exp(ga[hh] + crow), axis=-1, keepdims=True)
                    blk[hh] = jnp.where(lane8 == lo + s, col, blk[hh])
            for hh in range(2):
                a_rows.append(jnp.where(lane8 <= lo + hh * half + row8, blk[hh], 0.0))
        a_full = jnp.concatenate(a_rows, axis=0).astype(BF16)
        o = o_inter + jnp.dot(a_full, v, preferred_element_type=F32)
        g_last = g[c_len - 1:c_len, :]
        kd = jnp.exp(g_last - g + log_k).astype(BF16)
        st = st * jnp.exp(g_last) + lax.dot_general(v, kd, (((0,), (0,)), ((), ())),
                                                    preferred_element_type=F32)
        ms = jnp.mean(o * o, axis=-1, keepdims=True)
        y = o * lax.rsqrt(ms + EPS) * og * (gate * jax.nn.sigmoid(gate))
        o_ref[rows, :] = y.astype(BF16)
        return st

    lax.fori_loop(0, n_chunk, chunk, jnp.zeros((HGRN_D, HGRN_D), F32), unroll=3)


def _hgrn(proj3, log_lb, log1m_lb, out_gain):
    bsz, tp, _ = proj3.shape
    seq = lambda blk: pl.BlockSpec((None, tp, 128), lambda b, h: (b, 0, blk + h))
    chan = pl.BlockSpec((None, 1, HGRN_D), lambda b, h: (h, 0, 0))
    return pl.pallas_call(
        _hgrn_kernel,
        out_shape=jax.ShapeDtypeStruct((bsz, tp, HGRN_HEADS * HGRN_D), BF16),
        grid=(bsz, HGRN_HEADS),
        in_specs=[seq(BLK_RQ), seq(BLK_RF), seq(BLK_RI), seq(BLK_RG), chan, chan,
                  pl.BlockSpec((1, HGRN_D), lambda b, h: (0, 0))],
        out_specs=pl.BlockSpec((None, tp, 128), lambda b, h: (b, 0, h)),
        compiler_params=_params(("parallel", "parallel")),
        name="hgrn2",
    )(proj3, proj3, proj3, proj3,
      log_lb.reshape(HGRN_HEADS, 1, HGRN_D), log1m_lb.reshape(HGRN_HEADS, 1, HGRN_D),
      out_gain.reshape(1, HGRN_D))


def _merge_kernel(hs_ref, ua_ref, ur_ref, cb_ref, cc_ref, ch_ref, pc_ref, ph_ref,
                  g0_ref, g1_ref, g2_ref, cw_ref, wb_ref, wo_ref, o_ref, *, tm, tiles_per_seq):
    i = pl.program_id(0)
    row = (i % tiles_per_seq) * tm + lax.broadcasted_iota(jnp.int32, (tm, 1), 0)
    valid = row >= PAD0
    z = jnp.where(valid, cc_ref[...].astype(F32) * ch_ref[...].astype(F32), 0.0)
    halo_row = (i % tiles_per_seq) * tm - 8 + lax.broadcasted_iota(jnp.int32, (8, 1), 0)
    zp = jnp.where(halo_row >= PAD0, pc_ref[...].astype(F32) * ph_ref[...].astype(F32), 0.0)
    zz = jnp.concatenate([zp, z], axis=0)
    cw = cw_ref[...]
    y = (cw[2:3, :] * z + cw[1:2, :] * zz[7:7 + tm, :] + cw[0:1, :] * zz[6:6 + tm, :])
    u_conv = jnp.where(valid, cb_ref[...].astype(F32) * y, 0.0).astype(BF16)
    mixed = jnp.zeros((tm, D_MODEL), F32)
    for n, (u, g_ref) in enumerate(((ua_ref[...], g0_ref), (u_conv, g1_ref), (ur_ref[...], g2_ref))):
        up = jnp.dot(u, wb_ref[n], preferred_element_type=F32)
        mixed = mixed + jax.nn.sigmoid(g_ref[...].astype(F32)) * up
    o_ref[...] = hs_ref[...] + jnp.dot(mixed.astype(BF16), wo_ref[...], preferred_element_type=F32)


def _merge(hs, u_att, u_hgrn, proj, conv_w, wb_bf, wo_bf, tp):
    rows = hs.shape[0]
    tm = _row_tile(tp, 384)
    kern = functools.partial(_merge_kernel, tm=tm, tiles_per_seq=tp // tm)
    row_blk = lambda w, blk: pl.BlockSpec((tm, w), lambda i: (i, blk))
    halo = lambda blk: pl.BlockSpec((8, 512), lambda i: (jnp.maximum(i * (tm // 8) - 1, 0), blk))
    const = lambda shape: pl.BlockSpec(shape, lambda i: (0,) * len(shape))
    return pl.pallas_call(
        kern,
        out_shape=jax.ShapeDtypeStruct((rows, D_MODEL), F32),
        grid=(rows // tm,),
        in_specs=[row_blk(D_MODEL, 0), row_blk(512, 0), row_blk(512, 0),
                  row_blk(512, BLK_CB), row_blk(512, BLK_CC), row_blk(512, BLK_CH),
                  halo(BLK_CC), halo(BLK_CH),
                  row_blk(1024, BLK_GATE), row_blk(1024, BLK_GATE + 1), row_blk(1024, BLK_GATE + 2),
                  const((CONV_K, 512)), const((3, 512, D_MODEL)), const((D_MODEL, D_MODEL))],
        out_specs=row_blk(D_MODEL, 0),
        compiler_params=_params(("parallel",)),
        name="merge",
    )(hs, u_att, u_hgrn, proj, proj, proj, proj, proj, proj, proj, proj, conv_w, wb_bf, wo_bf)


def _ffn_kernel(hs_ref, g_ref, wg_ref, wu_ref, wd_ref, o_ref):
    x = hs_ref[...]
    ms = jnp.mean(x * x, axis=-1, keepdims=True)
    hn = (x * lax.rsqrt(ms + EPS) * g_ref[...]).astype(BF16)
    a = jnp.dot(hn, wg_ref[...], preferred_element_type=F32)
    u = jnp.dot(hn, wu_ref[...], preferred_element_type=F32)
    h = (a * jax.nn.sigmoid(a) * u).astype(BF16)
    o_ref[...] = x + jnp.dot(h, wd_ref[...], preferred_element_type=F32)


def _ffn(hs, gain, wg_bf, wu_bf, wd_bf):
    rows = hs.shape[0]
    d_ff = wg_bf.shape[1]
    tm = _row_tile(rows, 768)
    const = lambda shape: pl.BlockSpec(shape, lambda i: (0, 0), pipeline_mode=pl.Buffered(1))
    return pl.pallas_call(
        _ffn_kernel,
        out_shape=jax.ShapeDtypeStruct((rows, D_MODEL), F32),
        grid=(rows // tm,),
        in_specs=[pl.BlockSpec((tm, D_MODEL), lambda i: (i, 0)), const((1, D_MODEL)),
                  const((D_MODEL, d_ff)), const((D_MODEL, d_ff)), const((d_ff, D_MODEL))],
        out_specs=pl.BlockSpec((tm, D_MODEL), lambda i: (i, 0)),
        compiler_params=_params(("parallel",)),
        name="ffn_dense",
    )(hs, gain.reshape(1, D_MODEL), wg_bf, wu_bf, wd_bf)


def _router_kernel(hs_ref, g_ref, rw_ref, idx_ref, wt_ref, cnt_ref, carry_ref, *, tm, tiles_per_seq):
    i = pl.program_id(0)

    @pl.when(i == 0)
    def _():
        carry_ref[...] = jnp.zeros_like(carry_ref)

    x = hs_ref[...]
    ms = jnp.mean(x * x, axis=-1, keepdims=True)
    hn = x * lax.rsqrt(ms + EPS) * g_ref[...]
    logits = jnp.dot(hn, rw_ref[...], preferred_element_type=F32, precision=lax.Precision.HIGHEST)
    lane = lax.broadcasted_iota(jnp.int32, logits.shape, 1)
    lane_f = lane.astype(F32)
    logits = jnp.where(lane < N_EXPERTS, logits, -jnp.inf)
    m1 = jnp.max(logits, axis=-1, keepdims=True)
    i1 = jnp.min(jnp.where(logits == m1, lane_f, 128.0), axis=-1, keepdims=True)
    rest = jnp.where(lane_f == i1, -jnp.inf, logits)
    m2 = jnp.max(rest, axis=-1, keepdims=True)
    i2 = jnp.min(jnp.where(rest == m2, lane_f, 128.0), axis=-1, keepdims=True)
    e2 = jnp.exp(m2 - m1)
    w1 = 1.0 / (1.0 + e2)
    w2 = e2 / (1.0 + e2)
    row = (i % tiles_per_seq) * tm + lax.broadcasted_iota(jnp.int32, (tm, 1), 0)
    real = row >= T_PAD
    hot1 = jnp.where(jnp.logical_and(real, lane_f == i1), 1.0, 0.0)
    hot2 = jnp.where(jnp.logical_and(real, lane_f == i2), 1.0, 0.0)
    both = hot1 + hot2
    rr = lax.broadcasted_iota(jnp.int32, (tm, tm), 0)
    cc = lax.broadcasted_iota(jnp.int32, (tm, tm), 1)
    earlier = jnp.where(cc < rr, 1.0, 0.0).astype(BF16)
    before = carry_ref[...] + jnp.dot(earlier, both.astype(BF16), preferred_element_type=F32)
    r1 = jnp.sum(before * hot1, axis=-1, keepdims=True)
    r2 = jnp.sum(before * hot2, axis=-1, keepdims=True)
    carry_ref[...] += jnp.sum(both, axis=0, keepdims=True)
    cnt_ref[...] = carry_ref[...]
    packed = jnp.where(lane == 0, i1, jnp.where(lane == 1, i2, jnp.where(lane == 2, r1, jnp.where(lane == 3, r2, 0.0))))
    idx_ref[...] = packed.astype(jnp.int32)
    wt_ref[...] = jnp.where(lane == 0, w1, jnp.where(lane == 1, w2, 0.0))


def _router(hs, gain, router_w, tp):
    rows = hs.shape[0]
    tm = _row_tile(tp, 384)
    rw = jnp.zeros((D_MODEL, 128), F32).at[:, :N_EXPERTS].set(router_w.astype(F32))
    blk = pl.BlockSpec((tm, 128), lambda i: (i, 0))
    return pl.pallas_call(
        functools.partial(_router_kernel, tm=tm, tiles_per_seq=tp // tm),
        out_shape=(jax.ShapeDtypeStruct((rows, 128), jnp.int32), jax.ShapeDtypeStruct((rows, 128), F32),
                   jax.ShapeDtypeStruct((1, 128), F32)),
        grid=(rows // tm,),
        in_specs=[pl.BlockSpec((tm, D_MODEL), lambda i: (i, 0)),
                  pl.BlockSpec((1, D_MODEL), lambda i: (0, 0)),
                  pl.BlockSpec((D_MODEL, 128), lambda i: (0, 0))],
        out_specs=(blk, blk, pl.BlockSpec((1, 128), lambda i: (0, 0))),
        scratch_shapes=[pltpu.VMEM((1, 128), F32)],
        compiler_params=_params(("arbitrary",)),
        name="moe_router",
    )(hs, gain.reshape(1, D_MODEL), rw)


def _dispatch_kernel(s1_ref, s2_ref, zr_ref, hs_ref, wt_ref, g_ref, xs_ref, rows_ref, zero_ref, sem, zsem,
                     *, td, tg, tiles_per_seq, n_steps):
    n = pl.program_id(0) * tiles_per_seq + pl.program_id(1)
    buf = n % 2
    base = n * td

    def row_copy(r, which):
        slot = (s1_ref, s2_ref)[which][base + r]
        return pltpu.make_async_copy(rows_ref.at[buf, which, pl.ds(r, 1)], xs_ref.at[pl.ds(slot, 1)],
                                     sem.at[buf])

    def wait_rows(b):
        def body(r, c):
            pltpu.make_async_copy(rows_ref.at[b, 0, pl.ds(0, 1)], xs_ref.at[pl.ds(0, 1)], sem.at[b]).wait()
            return c
        lax.fori_loop(0, 2 * td, body, 0)

    @pl.when(n == 0)
    def _():
        zero_ref[...] = jnp.zeros_like(zero_ref)
        for e in range(2 * N_EXPERTS):
            fill = pltpu.make_async_copy(zero_ref, xs_ref.at[pl.ds(pl.multiple_of(zr_ref[e], tg), tg)], zsem)
            fill.start()
            fill.wait()

    @pl.when(n >= 2)
    def _():
        wait_rows(buf)

    x = hs_ref[...]
    ms = jnp.mean(x * x, axis=-1, keepdims=True)
    hn = x * lax.rsqrt(ms + EPS) * g_ref[...]
    wt = wt_ref[...]
    for which in range(2):
        rows_ref[buf, which, :, :D_MODEL] = hn
        rows_ref[buf, which, :, D_MODEL:] = jnp.broadcast_to(wt[:, which:which + 1], (td, XS_EXTRA))

    def issue(r, c):
        row_copy(r, 0).start()
        row_copy(r, 1).start()
        return c

    lax.fori_loop(0, td, issue, 0)

    @pl.when(n == n_steps - 1)
    def _():
        wait_rows(buf)
        if n_steps > 1:
            wait_rows(1 - buf)


def _dispatch(hs3, wts3, gain, slot1, slot2, zero_rows, n_slots, tg):
    bsz, tp, _ = hs3.shape
    td = T_PAD
    tiles_per_seq = (tp - T_PAD) // td
    width = D_MODEL + XS_EXTRA
    kern = functools.partial(_dispatch_kernel, td=td, tg=tg, tiles_per_seq=tiles_per_seq,
                             n_steps=bsz * tiles_per_seq)
    return pl.pallas_call(
        kern,
        out_shape=jax.ShapeDtypeStruct((n_slots, width), F32),
        grid_spec=pltpu.PrefetchScalarGridSpec(
            num_scalar_prefetch=3,
            grid=(bsz, tiles_per_seq),
            in_specs=[pl.BlockSpec((None, td, D_MODEL), lambda b, i, *_: (b, i + 1, 0)),
                      pl.BlockSpec((None, td, 128), lambda b, i, *_: (b, i + 1, 0)),
                      pl.BlockSpec((1, D_MODEL), lambda b, i, *_: (0, 0))],
            out_specs=pl.BlockSpec(memory_space=pl.ANY),
            scratch_shapes=[pltpu.VMEM((2, 2, td, width), F32), pltpu.VMEM((tg, width), F32),
                            pltpu.SemaphoreType.DMA((2,)), pltpu.SemaphoreType.DMA],
        ),
        compiler_params=_params(("arbitrary", "arbitrary")),
        name="moe_dispatch",
    )(slot1, slot2, zero_rows, hs3, wts3, gain.reshape(1, D_MODEL))


def _expert_kernel(te_ref, nt_ref, x_ref, wg_ref, wu_ref, wd_ref, o_ref, hn_ref, acc_ref, *, n_f):
    t = pl.program_id(0)
    f = pl.program_id(1)

    @pl.when(t < nt_ref[0])
    def _():
        @pl.when(f == 0)
        def _():
            hn_ref[...] = x_ref[:, :D_MODEL].astype(BF16)

        hn = hn_ref[...]
        a = jnp.dot(hn, wg_ref[...], preferred_element_type=F32)
        u = jnp.dot(hn, wu_ref[...], preferred_element_type=F32)
        h = (a * jax.nn.sigmoid(a) * u).astype(BF16)
        y = jnp.dot(h, wd_ref[...], preferred_element_type=F32)

        @pl.when(f == 0)
        def _():
            acc_ref[...] = y

        @pl.when(f > 0)
        def _():
            acc_ref[...] += y

    @pl.when(jnp.logical_and(t >= nt_ref[0], f == 0))
    def _():
        acc_ref[...] = jnp.zeros_like(acc_ref)

    @pl.when(f == n_f - 1)
    def _():
        o_ref[...] = (acc_ref[...] * x_ref[:, D_MODEL:D_MODEL + 1]).astype(BF16)


def _experts(xs, tile_expert, n_tiles_used, wg_bf, wu_bf, wd_bf, tg):
    slots = xs.shape[0]
    n_f, tf = wg_bf.shape[1], wg_bf.shape[3]

    def x_map(t, f, te, nt):
        return (jnp.minimum(t, nt[0] - 1), 0)

    def f_of(t, f, nt):
        return jnp.where(t < nt[0], f, n_f - 1)

    return pl.pallas_call(
        functools.partial(_expert_kernel, n_f=n_f),
        out_shape=jax.ShapeDtypeStruct((slots, D_MODEL), BF16),
        grid_spec=pltpu.PrefetchScalarGridSpec(
            num_scalar_prefetch=2,
            grid=(slots // tg, n_f),
            in_specs=[
                pl.BlockSpec((tg, D_MODEL + XS_EXTRA), x_map),
                pl.BlockSpec((None, None, D_MODEL, tf), lambda t, f, te, nt: (te[t], f_of(t, f, nt), 0, 0)),
                pl.BlockSpec((None, None, D_MODEL, tf), lambda t, f, te, nt: (te[t], f_of(t, f, nt), 0, 0)),
                pl.BlockSpec((None, tf, D_MODEL), lambda t, f, te, nt: (te[t], f_of(t, f, nt), 0)),
            ],
            out_specs=pl.BlockSpec((tg, D_MODEL), lambda t, f, te, nt: (t, 0)),
            scratch_shapes=[pltpu.VMEM((tg, D_MODEL), BF16), pltpu.VMEM((tg, D_MODEL), F32)],
        ),
        compiler_params=_params(("arbitrary", "arbitrary")),
        name="moe_experts",
    )(tile_expert, n_tiles_used, xs, wg_bf, wu_bf, wd_bf)


def _combine_kernel(ws_ref, hs_ref, route_ref, y_ref, o_ref, win_ref, sem, *, tc, tiles_per_seq, n_steps):
    n = pl.program_id(0) * tiles_per_seq + pl.program_id(1)
    cur = n % 2

    def window_copy(step, b, e):
        start = pl.multiple_of(ws_ref[step * N_EXPERTS + e], WIN_ALIGN)
        return pltpu.make_async_copy(y_ref.at[pl.ds(start, COMBINE_WIN)],
                                     win_ref.at[b, pl.ds(e * COMBINE_WIN, COMBINE_WIN)], sem.at[b])

    def fetch(step, b):
        for e in range(N_EXPERTS):
            window_copy(step, b, e).start()

    @pl.when(n == 0)
    def _():
        fetch(0, 0)

    @pl.when(n + 1 < n_steps)
    def _():
        fetch(n + 1, 1 - cur)

    for e in range(N_EXPERTS):
        window_copy(n, cur, e).wait()

    route = route_ref[...]
    e1, e2, s1, s2 = (route[:, c:c + 1] for c in range(4))
    pos = lax.broadcasted_iota(jnp.int32, (1, COMBINE_WIN), 1)
    picks = []
    for e in range(N_EXPERTS):
        start = ws_ref[n * N_EXPERTS + e]
        col = jnp.where(e1 == e, s1 - start, jnp.where(e2 == e, s2 - start, -1))
        picks.append(jnp.where(col == pos, 1.0, 0.0).astype(BF16))
    pick = jnp.concatenate(picks, axis=1)
    o_ref[...] = hs_ref[...] + jnp.dot(pick, win_ref[cur], preferred_element_type=F32)


def _combine(hs3, route, y, win_start, seq):
    bsz = hs3.shape[0]
    tc = COMBINE_TOKENS
    tiles_per_seq = seq // tc
    kern = functools.partial(_combine_kernel, tc=tc, tiles_per_seq=tiles_per_seq, n_steps=bsz * tiles_per_seq)
    tok = lambda w: pl.BlockSpec((tc, w), lambda b, i, ws: (b * tiles_per_seq + i, 0))
    return pl.pallas_call(
        kern,
        out_shape=jax.ShapeDtypeStruct((bsz, seq, D_MODEL), F32),
        grid_spec=pltpu.PrefetchScalarGridSpec(
            num_scalar_prefetch=1,
            grid=(bsz, tiles_per_seq),
            in_specs=[
                pl.BlockSpec((None, tc, D_MODEL), lambda b, i, ws: (b, i + T_PAD // tc, 0)),
                tok(4),
                pl.BlockSpec(memory_space=pl.ANY),
            ],
            out_specs=pl.BlockSpec((None, tc, D_MODEL), lambda b, i, ws: (b, i, 0)),
            scratch_shapes=[pltpu.VMEM((2, N_EXPERTS * COMBINE_WIN, D_MODEL), BF16),
                            pltpu.SemaphoreType.DMA((2,))],
        ),
        compiler_params=_params(("arbitrary", "arbitrary")),
        name="moe_combine",
    )(win_start, hs3, route, y)


def _moe(hs, gain, router_w, wg_bf, wu_bf, wd_bf, bsz, tp):
    seq = tp - T_PAD
    n_tok = bsz * seq
    tg = MOE_TILE
    idx, wts, cnt = _router(hs, gain, router_w, tp)
    sel = idx.reshape(bsz, tp, 128)[:, T_PAD:, :4].reshape(n_tok, 4)
    counts = cnt[0, :N_EXPERTS].astype(jnp.int32)
    padded = ((counts + tg - 1) // tg) * tg
    ends = jnp.cumsum(padded)
    starts = ends - padded
    experts = jnp.arange(N_EXPERTS, dtype=jnp.int32)[None, :]
    slot1 = jnp.sum(jnp.where(sel[:, 0:1] == experts, starts[None, :], 0), axis=1) + sel[:, 2]
    slot2 = jnp.sum(jnp.where(sel[:, 1:2] == experts, starts[None, :], 0), axis=1) + sel[:, 3]
    slot1, slot2 = slot1.astype(jnp.int32), slot2.astype(jnp.int32)
    n_slots = 2 * n_tok + N_EXPERTS * tg
    tile_start = jnp.arange(n_slots // tg, dtype=jnp.int32)[:, None] * tg
    tile_expert = jnp.minimum(jnp.sum((tile_start >= ends[None, :]).astype(jnp.int32), axis=1), N_EXPERTS - 1)
    n_tiles_used = (ends[-1:] // tg).astype(jnp.int32)
    tail = n_slots - tg * (1 + jnp.arange(N_EXPERTS, dtype=jnp.int32))
    zero_rows = jnp.concatenate([jnp.maximum(ends - tg, 0), tail]).astype(jnp.int32)

    tc = COMBINE_TOKENS
    first = jnp.minimum(
        jnp.min(jnp.where(sel[:, 0:1] == experts, slot1[:, None], n_slots).reshape(n_tok // tc, tc, N_EXPERTS), axis=1),
        jnp.min(jnp.where(sel[:, 1:2] == experts, slot2[:, None], n_slots).reshape(n_tok // tc, tc, N_EXPERTS), axis=1))
    win_start = jnp.where(first == n_slots, 0,
                          jnp.minimum(first // WIN_ALIGN * WIN_ALIGN, n_slots - COMBINE_WIN))
    route = jnp.concatenate([sel[:, :2], slot1[:, None], slot2[:, None]], axis=1)

    hs3 = hs.reshape(bsz, tp, D_MODEL)
    xs = _dispatch(hs3, wts.reshape(bsz, tp, 128), gain, slot1, slot2, zero_rows, n_slots, tg)
    y = _experts(xs, tile_expert.astype(jnp.int32), n_tiles_used, wg_bf, wu_bf, wd_bf, tg)
    return _combine(hs3, route, y, win_start.reshape(-1).astype(jnp.int32), seq)


def _ff_blocks(w):
    n_e, d, f = w.shape
    return w.astype(BF16).reshape(n_e, d, MOE_F_BLOCKS, f // MOE_F_BLOCKS).transpose(0, 2, 1, 3)


def _permute_qk_cols(w):
    return w.reshape(D_MODEL, 2, ATT_HEADS, ATT_QK_DIM).transpose(0, 2, 1, 3).reshape(D_MODEL, 512)


def kernel(x, meta_tokens, norm1_gain, norm2_gain, w_in, q_norm_gain, k_norm_gain, diff_lambda,
           attn_sub_gain, rel_bias, conv_w, hgrn_lb_logits, hgrn_out_gain, w_branch, w_out,
           ffn_w_gate, ffn_w_up, ffn_w_down, router_w, moe_w_gate, moe_w_up, moe_w_down):
    bsz, seq, _ = x.shape
    depth = w_in.shape[0]
    tp = T_PAD + seq
    assert tp % ATT_TILE == 0 and depth == 2

    head = jnp.concatenate([jnp.zeros((PAD0, D_MODEL), x.dtype), meta_tokens.astype(x.dtype)], axis=0)
    hs = jnp.concatenate([jnp.broadcast_to(head[None], (bsz, T_PAD, D_MODEL)), x], axis=1)
    hs = hs.reshape(bsz * tp, D_MODEL)

    lb_all = jnp.cumsum(jax.nn.softmax(hgrn_lb_logits.astype(F32), axis=0), axis=0)
    lb_all = lb_all - lb_all[0]
    btab = _attn_bias_tables(rel_bias, ATT_TILE)

    out = None
    for layer in range(depth):
        w = w_in[layer]
        w_bf = jnp.concatenate([_permute_qk_cols(w[:, :512]), _permute_qk_cols(w[:, 512:1024]), w[:, 1024:]],
                               axis=1).astype(BF16)
        w_bf = w_bf.reshape(D_MODEL, IN_COLS // COL_TILE, COL_TILE).transpose(1, 0, 2)
        qk_gain = jnp.concatenate([jnp.tile(q_norm_gain[layer].astype(F32), 8) * (ATT_QK_DIM ** -0.5 * LOG2E),
                                   jnp.tile(k_norm_gain[layer].astype(F32), 8)]).reshape(1, COL_TILE)
        proj = _inproj(hs, norm1_gain[layer], w_bf, qk_gain)
        proj3 = proj.reshape(bsz, tp, IN_COLS)

        lam_init = 0.8 - 0.6 * math.exp(-0.3 * layer)
        lp = diff_lambda[layer].astype(F32)
        lam = jnp.exp(jnp.sum(lp[0] * lp[1])) - jnp.exp(jnp.sum(lp[2] * lp[3])) + lam_init
        u_att = _diff_attention(proj3, lam.reshape(1), attn_sub_gain[layer].astype(F32), btab,
                                1.0 - lam_init)

        lb = lb_all[layer]
        u_hgrn = _hgrn(proj3, jnp.log(lb), jnp.log1p(-lb), hgrn_out_gain[layer].astype(F32))

        hs = _merge(hs, u_att.reshape(bsz * tp, 512), u_hgrn.reshape(bsz * tp, 512), proj,
                    conv_w[layer].astype(F32), w_branch[layer].astype(BF16), w_out[layer].astype(BF16), tp)

        j = layer // 2
        if layer % 2 == 0:
            hs = _ffn(hs, norm2_gain[layer], ffn_w_gate[j].astype(BF16), ffn_w_up[j].astype(BF16),
                      ffn_w_down[j].astype(BF16))
        else:
            out = _moe(hs, norm2_gain[layer], router_w[j], _ff_blocks(moe_w_gate[j]), _ff_blocks(moe_w_up[j]),
                       moe_w_down[j].astype(BF16), bsz, tp)
    return out
```

```python
import functools
import math

import numpy as np
import jax
import jax.numpy as jnp
from jax import lax
from jax.experimental import pallas as pl
from jax.experimental.pallas import tpu as pltpu

F32 = jnp.float32
BF16 = jnp.bfloat16

D_MODEL = 1024
N_META = 16
EPS = 1e-6
ATT_HEADS = 4
ATT_QK_DIM = 64
ATT_V_DIM = 128
REL_BUCKETS = 32
REL_MAX_DIST = 128
CONV_K = 3
HGRN_HEADS = 4
HGRN_D = 128
N_EXPERTS = 8
IN_COLS = 8192

T_PAD = 128
PAD0 = T_PAD - N_META
ATT_TILE = 384
VT_ONES = 16
LOG2E = math.log2(math.e)
HGRN_CHUNK = 128
HGRN_SUB = 16
COL_TILE = 1024
GROUP_MEAN_WIDTH = 256
MOE_TILE = 512
MOE_F_BLOCKS = 2
XS_EXTRA = 128
COMBINE_TOKENS = 128
WIN_ALIGN = 16
COMBINE_WIN = 256
MASK = -1e30
VMEM_LIMIT = 56 * 1024 * 1024

BLK_Q, BLK_K, BLK_V = 0, 4, 8
BLK_RQ, BLK_RF, BLK_RI, BLK_RG = 24, 28, 32, 36
BLK_CB, BLK_CC, BLK_CH = 3, 4, 5
BLK_GATE = 5


def _row_tile(rows, target):
    n = rows // 128
    best = 1
    for d in range(1, n + 1):
        if n % d == 0 and d * 128 <= target:
            best = d
    return best * 128


def _params(sem, vmem=VMEM_LIMIT):
    return pltpu.CompilerParams(dimension_semantics=sem, vmem_limit_bytes=vmem)


def _inproj_kernel(x_ref, g_ref, w_ref, qkg_ref, gm_ref, o_ref, xn_ref):
    j = pl.program_id(1)

    @pl.when(j == 0)
    def _():
        x = x_ref[...]
        ms = jnp.mean(x * x, axis=-1, keepdims=True)
        xn_ref[...] = (x * lax.rsqrt(ms + EPS) * g_ref[...]).astype(BF16)

    acc = jnp.dot(xn_ref[...], w_ref[...], preferred_element_type=F32)

    @pl.when(j == 0)
    def _():
        sq = acc * acc
        hi = sq.astype(BF16)
        lo = (sq - hi.astype(F32)).astype(BF16)
        gm = gm_ref[...]
        width = gm.shape[0]
        ms = jnp.concatenate(
            [jnp.dot(hi[:, c:c + width], gm, preferred_element_type=F32)
             + jnp.dot(lo[:, c:c + width], gm, preferred_element_type=F32)
             for c in range(0, COL_TILE, width)], axis=1)
        o_ref[...] = (acc * lax.rsqrt(ms + EPS) * qkg_ref[...]).astype(BF16)

    @pl.when(j > 0)
    def _():
        o_ref[...] = acc.astype(BF16)


def _inproj(hs, gain, w_bf, qk_gain):
    rows = hs.shape[0]
    tm = _row_tile(rows, 1536)
    n_col = IN_COLS // COL_TILE
    assert COL_TILE == 4 * ATT_HEADS * ATT_QK_DIM
    grp = np.arange(GROUP_MEAN_WIDTH) // ATT_QK_DIM
    gm = jnp.asarray((grp[:, None] == grp[None, :]).astype(np.float32) / ATT_QK_DIM, BF16)
    return pl.pallas_call(
        _inproj_kernel,
        out_shape=jax.ShapeDtypeStruct((rows, IN_COLS), BF16),
        grid=(rows // tm, n_col),
        in_specs=[
            pl.BlockSpec((tm, D_MODEL), lambda i, j: (i, 0)),
            pl.BlockSpec((1, D_MODEL), lambda i, j: (0, 0)),
            pl.BlockSpec((D_MODEL, COL_TILE), lambda i, j: (0, j)),
            pl.BlockSpec((1, COL_TILE), lambda i, j: (0, 0)),
            pl.BlockSpec((GROUP_MEAN_WIDTH, GROUP_MEAN_WIDTH), lambda i, j: (0, 0)),
        ],
        out_specs=pl.BlockSpec((tm, COL_TILE), lambda i, j: (i, j)),
        scratch_shapes=[pltpu.VMEM((tm, D_MODEL), BF16)],
        compiler_params=_params(("parallel", "arbitrary")),
        name="inproj",
    )(hs, gain.reshape(1, D_MODEL), w_bf, qk_gain, gm)


def _rel_bucket_table(n_max):
    n = np.arange(n_max, dtype=np.int64)
    max_exact = REL_BUCKETS // 2
    nf = np.maximum(n, 1).astype(np.float32)
    large = max_exact + (np.log(nf / np.float32(max_exact)) / np.float32(math.log(REL_MAX_DIST / max_exact))
                         * np.float32(REL_BUCKETS - max_exact)).astype(np.int32)
    large = np.minimum(large, REL_BUCKETS - 1)
    return np.where(n < max_exact, n, large).astype(np.int32)


def _attn_bias_tables(rel_bias, t):
    bucket = _rel_bucket_table(2 * t)
    assert np.all(bucket[t + 1:] == REL_BUCKETS - 1) and np.all(np.diff(bucket) >= 0)
    first_dist = tuple(int(np.searchsorted(bucket, b, side="left")) for b in range(REL_BUCKETS))
    return pl.pallas_call(
        functools.partial(_bias_kernel, t=t, first_dist=first_dist),
        out_shape=jax.ShapeDtypeStruct((ATT_HEADS, 6, t, t), F32),
        grid=(ATT_HEADS,),
        in_specs=[pl.BlockSpec(memory_space=pltpu.SMEM)],
        out_specs=pl.BlockSpec((None, 6, t, t), lambda h: (h, 0, 0, 0)),
        compiler_params=_params(("parallel",)),
        name="attn_bias",
    )(rel_bias.astype(F32))


def _bias_kernel(rb_ref, o_ref, *, t, first_dist):
    h = pl.program_id(0)
    key = lax.broadcasted_iota(jnp.int32, (t, t), 0)
    qry = lax.broadcasted_iota(jnp.int32, (t, t), 1)
    far = rb_ref[REL_BUCKETS - 1, h]

    def table(n):
        val = jnp.full((t, t), rb_ref[0, h] - far, F32)
        for b in range(1, REL_BUCKETS):
            val = jnp.where(n >= first_dist[b], rb_ref[b, h] - far, val)
        return val * LOG2E

    n0 = qry - key
    diag = jnp.where(n0 >= 0, table(n0), MASK)
    near = table(n0 + t)
    zero = jnp.zeros((t, t), F32)
    for kind, tab in enumerate((diag, near, zero)):
        o_ref[kind] = tab
        o_ref[kind + 3] = jnp.where(key < PAD0, MASK, tab)


def _attn_kernel(lam_ref, q_ref, k_ref, v_ref, bt_ref, sg_ref, o_ref, m_ref, acc_ref, vt_ref,
                 *, t, out_scale):
    n_t = vt_ref.shape[0]
    for j in range(n_t):
        vt_ref[j, :ATT_V_DIM, :] = v_ref[j * t:(j + 1) * t, :].astype(F32).T.astype(BF16)
        vt_ref[j, ATT_V_DIM:, :] = jnp.ones((VT_ONES, t), BF16)
    lane = lax.broadcasted_iota(jnp.int32, (1, 2 * ATT_QK_DIM), 1)
    nt = (((1,), (1,)), ((), ()))

    def tile_rows(i):
        return pl.ds(pl.multiple_of(i * t, t), t)

    def scores(i, j):
        q = q_ref[tile_rows(i), :]
        zero = jnp.zeros_like(q)
        q_cat = jnp.concatenate([jnp.where(lane < ATT_QK_DIM, q, zero), jnp.where(lane >= ATT_QK_DIM, q, zero)],
                                axis=0)
        bias = bt_ref[jnp.minimum(i - j, 2) + jnp.where(j == 0, 3, 0)]
        s = (lax.dot_general(k_ref[tile_rows(j), :], q_cat, nt, preferred_element_type=F32)
             + jnp.concatenate([bias, bias], axis=1))
        return s, jnp.max(s, axis=0, keepdims=True)

    def consume(j, scored):
        s, s_max = scored
        m_old = m_ref[...]
        m_new = jnp.maximum(m_old, s_max)
        m_ref[...] = m_new
        p = jnp.exp2(s - m_new).astype(BF16)
        acc_ref[...] = (jnp.exp2(m_old - m_new) * acc_ref[...]
                        + jnp.dot(vt_ref[j], p, preferred_element_type=F32))

    def reset():
        m_ref[...] = jnp.full(m_ref.shape, MASK, F32)
        acc_ref[...] = jnp.zeros(acc_ref.shape, F32)

    def finish(i):
        acc = acc_ref[...]
        a1, a2 = acc[:, :t], acc[:, t:]
        o_t = (a1[:ATT_V_DIM] / a1[ATT_V_DIM:ATT_V_DIM + 1]
               - lam_ref[0] * (a2[:ATT_V_DIM] / a2[ATT_V_DIM:ATT_V_DIM + 1]))
        o = o_t.T
        ms = jnp.mean(o * o, axis=-1, keepdims=True)
        y = o * lax.rsqrt(ms + EPS) * (sg_ref[...] * out_scale)
        row = i * t + lax.broadcasted_iota(jnp.int32, (t, 1), 0)
        o_ref[tile_rows(i), :] = jnp.where(row >= PAD0, y, 0.0).astype(BF16)
        reset()

    def step(_, carry):
        i, j, s = carry
        last = j == i
        ni = jnp.where(last, i + 1, i)
        nj = jnp.where(last, 0, j + 1)
        nxt = scores(jnp.minimum(ni, n_t - 1), nj)
        consume(j, s)

        @pl.when(last)
        def _():
            finish(i)

        return ni, nj, nxt

    reset()
    first = jnp.int32(0)
    lax.fori_loop(0, n_t * (n_t + 1) // 2, step, (first, first, scores(first, first)))


def _diff_attention(proj3, lam, sub_gain, btab, out_scale):
    bsz, tp, _ = proj3.shape
    t = ATT_TILE
    n_t = tp // t
    rows = ATT_V_DIM + VT_ONES
    kern = functools.partial(_attn_kernel, t=t, out_scale=out_scale)
    seq = lambda blk: pl.BlockSpec((None, tp, 128), lambda b, h: (b, 0, blk + h))
    return pl.pallas_call(
        kern,
        out_shape=jax.ShapeDtypeStruct((bsz, tp, ATT_HEADS * ATT_V_DIM), BF16),
        grid=(bsz, ATT_HEADS),
        in_specs=[
            pl.BlockSpec(memory_space=pltpu.SMEM),
            seq(BLK_Q), seq(BLK_K), seq(BLK_V),
            pl.BlockSpec((None, 6, t, t), lambda b, h: (h, 0, 0, 0)),
            pl.BlockSpec((1, ATT_V_DIM), lambda b, h: (0, 0)),
        ],
        out_specs=pl.BlockSpec((None, tp, 128), lambda b, h: (b, 0, h)),
        scratch_shapes=[pltpu.VMEM((1, 2 * t), F32), pltpu.VMEM((rows, 2 * t), F32),
                        pltpu.VMEM((n_t, rows, t), BF16)],
        compiler_params=_params(("parallel", "parallel")),
        name="diff_attn",
    )(lam, proj3, proj3, proj3, btab, sub_gain.reshape(1, ATT_V_DIM))


def _split3(x):
    h1 = x.astype(BF16)
    r1 = x - h1.astype(F32)
    h2 = r1.astype(BF16)
    h3 = (r1 - h2.astype(F32)).astype(BF16)
    return h1, h2, h3


def _hgrn_kernel(q_ref, f_ref, i_ref, g_ref, la_ref, l1m_ref, og_ref, o_ref):
    c_len, sub = HGRN_CHUNK, HGRN_SUB
    half = sub // 2
    n_chunk = q_ref.shape[0] // c_len
    la, l1m, og = la_ref[...], l1m_ref[...], og_ref[...]
    rr = lax.broadcasted_iota(jnp.int32, (c_len, c_len), 0)
    cc = lax.broadcasted_iota(jnp.int32, (c_len, c_len), 1)
    tri = jnp.where(cc <= rr, 1.0, 0.0).astype(BF16)
    row8 = lax.broadcasted_iota(jnp.int32, (half, 1), 0)
    lane8 = lax.broadcasted_iota(jnp.int32, (half, c_len), 1)
    nt = (((1,), (1,)), ((), ()))

    def chunk(c, st):
        r0 = pl.multiple_of(c * c_len, c_len)
        rows = pl.ds(r0, c_len)
        z = f_ref[rows, :].astype(F32)
        qh = q_ref[rows, :].astype(F32)
        qh = qh * jax.nn.sigmoid(qh)
        v = i_ref[rows, :]
        gate = g_ref[rows, :].astype(F32)
        sp = jnp.log(1.0 + jnp.exp(-jnp.abs(z)))
        bb = l1m + jnp.minimum(z, 0.0) - sp
        log_f = jnp.maximum(la, bb) + jnp.log(1.0 + jnp.exp(-jnp.abs(la - bb)))
        valid = (r0 + lax.broadcasted_iota(jnp.int32, (c_len, 1), 0)) >= PAD0
        log_k = jnp.where(valid, l1m + jnp.minimum(-z, 0.0) - sp, -jnp.inf)
        g = sum(jnp.dot(tri, part, preferred_element_type=F32) for part in _split3(log_f))
        ck = log_k - g
        o_inter = lax.dot_general((qh * jnp.exp(g)).astype(BF16), st.astype(BF16), nt,
                                  preferred_element_type=F32)
        a_rows = []
        for a in range(c_len // sub):
            lo = a * sub
            ga = (g[lo:lo + half, :], g[lo + half:lo + sub, :])
            qa = (qh[lo:lo + half, :], qh[lo + half:lo + sub, :])
            if a == 0:
                blk = [jnp.zeros((half, c_len), F32)] * 2
            else:
                gs = g[lo - 1:lo, :]
                qd = (qh[lo:lo + sub, :] * jnp.exp(g[lo:lo + sub, :] - gs)).astype(BF16)
                kd = jnp.exp(jnp.minimum(gs - g[:lo, :], 0.0) + log_k[:lo, :]).astype(BF16)
                kd = jnp.concatenate([kd, jnp.zeros((c_len - lo, HGRN_D), BF16)], axis=0)
                a_off = lax.dot_general(qd, kd, nt, preferred_element_type=F32)
                blk = [a_off[:half, :], a_off[half:, :]]
            for s in range(sub):
                crow = ck[lo + s:lo + s + 1, :]
                for hh in range(s // half, 2):
                    col = jnp.sum(qa[hh] * jnp.exp(ga[hh] + crow), axis=-1, keepdims=True)
                    blk[hh] = jnp.where(lane8 == lo + s, col, blk[hh])
            for hh in range(2):
                a_rows.append(jnp.where(lane8 <= lo + hh * half + row8, blk[hh], 0.0))
        a_full = jnp.concatenate(a_rows, axis=0).astype(BF16)
        o = o_inter + jnp.dot(a_full, v, preferred_element_type=F32)
        g_last = g[c_len - 1:c_len, :]
        kd = jnp.exp(g_last - g + log_k).astype(BF16)
        st = st * jnp.exp(g_last) + lax.dot_general(v, kd, (((0,), (0,)), ((), ())),
                                                    preferred_element_type=F32)
        ms = jnp.mean(o * o, axis=-1, keepdims=True)
        y = o * lax.rsqrt(ms + EPS) * og * (gate * jax.nn.sigmoid(gate))
        o_ref[rows, :] = y.astype(BF16)
        return st

    lax.fori_loop(0, n_chunk, chunk, jnp.zeros((HGRN_D, HGRN_D), F32), unroll=3)


def _hgrn(proj3, log_lb, log1m_lb, out_gain):
    bsz, tp, _ = proj3.shape
    seq = lambda blk: pl.BlockSpec((None, tp, 128), lambda b, h: (b, 0, blk + h))
    chan = pl.BlockSpec((None, 1, HGRN_D), lambda b, h: (h, 0, 0))
    return pl.pallas_call(
        _hgrn_kernel,
        out_shape=jax.ShapeDtypeStruct((bsz, tp, HGRN_HEADS * HGRN_D), BF16),
        grid=(bsz, HGRN_HEADS),
        in_specs=[seq(BLK_RQ), seq(BLK_RF), seq(BLK_RI), seq(BLK_RG), chan, chan,
                  pl.BlockSpec((1, HGRN_D), lambda b, h: (0, 0))],
        out_specs=pl.BlockSpec((None, tp, 128), lambda b, h: (b, 0, h)),
        compiler_params=_params(("parallel", "parallel")),
        name="hgrn2",
    )(proj3, proj3, proj3, proj3,
      log_lb.reshape(HGRN_HEADS, 1, HGRN_D), log1m_lb.reshape(HGRN_HEADS, 1, HGRN_D),
      out_gain.reshape(1, HGRN_D))


def _merge_kernel(hs_ref, ua_ref, ur_ref, cb_ref, cc_ref, ch_ref, pc_ref, ph_ref,
                  g0_ref, g1_ref, g2_ref, cw_ref, wb_ref, wo_ref, o_ref, *, tm, tiles_per_seq):
    i = pl.program_id(0)
    row = (i % tiles_per_seq) * tm + lax.broadcasted_iota(jnp.int32, (tm, 1), 0)
    valid = row >= PAD0
    z = jnp.where(valid, cc_ref[...].astype(F32) * ch_ref[...].astype(F32), 0.0)
    halo_row = (i % tiles_per_seq) * tm - 8 + lax.broadcasted_iota(jnp.int32, (8, 1), 0)
    zp = jnp.where(halo_row >= PAD0, pc_ref[...].astype(F32) * ph_ref[...].astype(F32), 0.0)
    zz = jnp.concatenate([zp, z], axis=0)
    cw = cw_ref[...]
    y = (cw[2:3, :] * z + cw[1:2, :] * zz[7:7 + tm, :] + cw[0:1, :] * zz[6:6 + tm, :])
    u_conv = jnp.where(valid, cb_ref[...].astype(F32) * y, 0.0).astype(BF16)
    mixed = jnp.zeros((tm, D_MODEL), F32)
    for n, (u, g_ref) in enumerate(((ua_ref[...], g0_ref), (u_conv, g1_ref), (ur_ref[...], g2_ref))):
        up = jnp.dot(u, wb_ref[n], preferred_element_type=F32)
        mixed = mixed + jax.nn.sigmoid(g_ref[...].astype(F32)) * up
    o_ref[...] = hs_ref[...] + jnp.dot(mixed.astype(BF16), wo_ref[...], preferred_element_type=F32)


def _merge(hs, u_att, u_hgrn, proj, conv_w, wb_bf, wo_bf, tp):
    rows = hs.shape[0]
    tm = _row_tile(tp, 384)
    kern = functools.partial(_merge_kernel, tm=tm, tiles_per_seq=tp // tm)
    row_blk = lambda w, blk: pl.BlockSpec((tm, w), lambda i: (i, blk))
    halo = lambda blk: pl.BlockSpec((8, 512), lambda i: (jnp.maximum(i * (tm // 8) - 1, 0), blk))
    const = lambda shape: pl.BlockSpec(shape, lambda i: (0,) * len(shape))
    return pl.pallas_call(
        kern,
        out_shape=jax.ShapeDtypeStruct((rows, D_MODEL), F32),
        grid=(rows // tm,),
        in_specs=[row_blk(D_MODEL, 0), row_blk(512, 0), row_blk(512, 0),
                  row_blk(512, BLK_CB), row_blk(512, BLK_CC), row_blk(512, BLK_CH),
                  halo(BLK_CC), halo(BLK_CH),
                  row_blk(1024, BLK_GATE), row_blk(1024, BLK_GATE + 1), row_blk(1024, BLK_GATE + 2),
                  const((CONV_K, 512)), const((3, 512, D_MODEL)), const((D_MODEL, D_MODEL))],
        out_specs=row_blk(D_MODEL, 0),
        compiler_params=_params(("parallel",)),
        name="merge",
    )(hs, u_att, u_hgrn, proj, proj, proj, proj, proj, proj, proj, proj, conv_w, wb_bf, wo_bf)


def _ffn_kernel(hs_ref, g_ref, wg_ref, wu_ref, wd_ref, o_ref):
    x = hs_ref[...]
    ms = jnp.mean(x * x, axis=-1, keepdims=True)
    hn = (x * lax.rsqrt(ms + EPS) * g_ref[...]).astype(BF16)
    a = jnp.dot(hn, wg_ref[...], preferred_element_type=F32)
    u = jnp.dot(hn, wu_ref[...], preferred_element_type=F32)
    h = (a * jax.nn.sigmoid(a) * u).astype(BF16)
    o_ref[...] = x + jnp.dot(h, wd_ref[...], preferred_element_type=F32)


def _ffn(hs, gain, wg_bf, wu_bf, wd_bf):
    rows = hs.shape[0]
    d_ff = wg_bf.shape[1]
    tm = _row_tile(rows, 768)
    const = lambda shape: pl.BlockSpec(shape, lambda i: (0, 0), pipeline_mode=pl.Buffered(1))
    return pl.pallas_call(
        _ffn_kernel,
        out_shape=jax.ShapeDtypeStruct((rows, D_MODEL), F32),
        grid=(rows // tm,),
        in_specs=[pl.BlockSpec((tm, D_MODEL), lambda i: (i, 0)), const((1, D_MODEL)),
                  const((D_MODEL, d_ff)), const((D_MODEL, d_ff)), const((d_ff, D_MODEL))],
        out_specs=pl.BlockSpec((tm, D_MODEL), lambda i: (i, 0)),
        compiler_params=_params(("parallel",)),
        name="ffn_dense",
    )(hs, gain.reshape(1, D_MODEL), wg_bf, wu_bf, wd_bf)


def _router_kernel(hs_ref, g_ref, rw_ref, idx_ref, wt_ref, cnt_ref, carry_ref, *, tm, tiles_per_seq):
    i = pl.program_id(0)

    @pl.when(i == 0)
    def _():
        carry_ref[...] = jnp.zeros_like(carry_ref)

    x = hs_ref[...]
    ms = jnp.mean(x * x, axis=-1, keepdims=True)
    hn = x * lax.rsqrt(ms + EPS) * g_ref[...]
    logits = jnp.dot(hn, rw_ref[...], preferred_element_type=F32, precision=lax.Precision.HIGHEST)
    lane = lax.broadcasted_iota(jnp.int32, logits.shape, 1)
    lane_f = lane.astype(F32)
    logits = jnp.where(lane < N_EXPERTS, logits, -jnp.inf)
    m1 = jnp.max(logits, axis=-1, keepdims=True)
    i1 = jnp.min(jnp.where(logits == m1, lane_f, 128.0), axis=-1, keepdims=True)
    rest = jnp.where(lane_f == i1, -jnp.inf, logits)
    m2 = jnp.max(rest, axis=-1, keepdims=True)
    i2 = jnp.min(jnp.where(rest == m2, lane_f, 128.0), axis=-1, keepdims=True)
    e2 = jnp.exp(m2 - m1)
    w1 = 1.0 / (1.0 + e2)
    w2 = e2 / (1.0 + e2)
    row = (i % tiles_per_seq) * tm + lax.broadcasted_iota(jnp.int32, (tm, 1), 0)
    real = row >= T_PAD
    hot1 = jnp.where(jnp.logical_and(real, lane_f == i1), 1.0, 0.0)
    hot2 = jnp.where(jnp.logical_and(real, lane_f == i2), 1.0, 0.0)
    both = hot1 + hot2
    rr = lax.broadcasted_iota(jnp.int32, (tm, tm), 0)
    cc = lax.broadcasted_iota(jnp.int32, (tm, tm), 1)
    earlier = jnp.where(cc < rr, 1.0, 0.0).astype(BF16)
    before = carry_ref[...] + jnp.dot(earlier, both.astype(BF16), preferred_element_type=F32)
    r1 = jnp.sum(before * hot1, axis=-1, keepdims=True)
    r2 = jnp.sum(before * hot2, axis=-1, keepdims=True)
    carry_ref[...] += jnp.sum(both, axis=0, keepdims=True)
    cnt_ref[...] = carry_ref[...]
    packed = jnp.where(lane == 0, i1, jnp.where(lane == 1, i2, jnp.where(lane == 2, r1, jnp.where(lane == 3, r2, 0.0))))
    idx_ref[...] = packed.astype(jnp.int32)
    wt_ref[...] = jnp.where(lane == 0, w1, jnp.where(lane == 1, w2, 0.0))


def _router(hs, gain, router_w, tp):
    rows = hs.shape[0]
    tm = _row_tile(tp, 384)
    rw = jnp.zeros((D_MODEL, 128), F32).at[:, :N_EXPERTS].set(router_w.astype(F32))
    blk = pl.BlockSpec((tm, 128), lambda i: (i, 0))
    return pl.pallas_call(
        functools.partial(_router_kernel, tm=tm, tiles_per_seq=tp // tm),
        out_shape=(jax.ShapeDtypeStruct((rows, 128), jnp.int32), jax.ShapeDtypeStruct((rows, 128), F32),
                   jax.ShapeDtypeStruct((1, 128), F32)),
        grid=(rows // tm,),
        in_specs=[pl.BlockSpec((tm, D_MODEL), lambda i: (i, 0)),
                  pl.BlockSpec((1, D_MODEL), lambda i: (0, 0)),
                  pl.BlockSpec((D_MODEL, 128), lambda i: (0, 0))],
        out_specs=(blk, blk, pl.BlockSpec((1, 128), lambda i: (0, 0))),
        scratch_shapes=[pltpu.VMEM((1, 128), F32)],
        compiler_params=_params(("arbitrary",)),
        name="moe_router",
    )(hs, gain.reshape(1, D_MODEL), rw)


def _dispatch_kernel(s1_ref, s2_ref, zr_ref, hs_ref, wt_ref, g_ref, xs_ref, rows_ref, zero_ref, sem, zsem,
                     *, td, tg, tiles_per_seq, n_steps):
    n = pl.program_id(0) * tiles_per_seq + pl.program_id(1)
    buf = n % 2
    base = n * td

    def row_copy(r, which):
        slot = (s1_ref, s2_ref)[which][base + r]
        return pltpu.make_async_copy(rows_ref.at[buf, which, pl.ds(r, 1)], xs_ref.at[pl.ds(slot, 1)],
                                     sem.at[buf])

    def wait_rows(b):
        def body(r, c):
            pltpu.make_async_copy(rows_ref.at[b, 0, pl.ds(0, 1)], xs_ref.at[pl.ds(0, 1)], sem.at[b]).wait()
            return c
        lax.fori_loop(0, 2 * td, body, 0)

    @pl.when(n == 0)
    def _():
        zero_ref[...] = jnp.zeros_like(zero_ref)
        for e in range(2 * N_EXPERTS):
            fill = pltpu.make_async_copy(zero_ref, xs_ref.at[pl.ds(pl.multiple_of(zr_ref[e], tg), tg)], zsem)
            fill.start()
            fill.wait()

    @pl.when(n >= 2)
    def _():
        wait_rows(buf)

    x = hs_ref[...]
    ms = jnp.mean(x * x, axis=-1, keepdims=True)
    hn = x * lax.rsqrt(ms + EPS) * g_ref[...]
    wt = wt_ref[...]
    for which in range(2):
        rows_ref[buf, which, :, :D_MODEL] = hn
        rows_ref[buf, which, :, D_MODEL:] = jnp.broadcast_to(wt[:, which:which + 1], (td, XS_EXTRA))

    def issue(r, c):
        for which in range(2):
            slot = (s1_ref, s2_ref)[which][base + r]
            pltpu.async_copy(rows_ref.at[buf, which, pl.ds(r, 1)], xs_ref.at[pl.ds(slot, 1)], sem.at[buf],
                             priority=which)
        return c

    lax.fori_loop(0, td, issue, 0)

    @pl.when(n == n_steps - 1)
    def _():
        wait_rows(buf)
        if n_steps > 1:
            wait_rows(1 - buf)


def _dispatch(hs3, wts3, gain, slot1, slot2, zero_rows, n_slots, tg):
    bsz, tp, _ = hs3.shape
    td = T_PAD
    tiles_per_seq = (tp - T_PAD) // td
    width = D_MODEL + XS_EXTRA
    kern = functools.partial(_dispatch_kernel, td=td, tg=tg, tiles_per_seq=tiles_per_seq,
                             n_steps=bsz * tiles_per_seq)
    return pl.pallas_call(
        kern,
        out_shape=jax.ShapeDtypeStruct((n_slots, width), F32),
        grid_spec=pltpu.PrefetchScalarGridSpec(
            num_scalar_prefetch=3,
            grid=(bsz, tiles_per_seq),
            in_specs=[pl.BlockSpec((None, td, D_MODEL), lambda b, i, *_: (b, i + 1, 0)),
                      pl.BlockSpec((None, td, 128), lambda b, i, *_: (b, i + 1, 0)),
                      pl.BlockSpec((1, D_MODEL), lambda b, i, *_: (0, 0))],
            out_specs=pl.BlockSpec(memory_space=pl.ANY),
            scratch_shapes=[pltpu.VMEM((2, 2, td, width), F32), pltpu.VMEM((tg, width), F32),
                            pltpu.SemaphoreType.DMA((2,)), pltpu.SemaphoreType.DMA],
        ),
        compiler_params=_params(("arbitrary", "arbitrary")),
        name="moe_dispatch",
    )(slot1, slot2, zero_rows, hs3, wts3, gain.reshape(1, D_MODEL))


def _expert_kernel(te_ref, nt_ref, x_ref, wg_ref, wu_ref, wd_ref, o_ref, hn_ref, acc_ref, *, n_f):
    t = pl.program_id(0)
    f = pl.program_id(1)

    @pl.when(t < nt_ref[0])
    def _():
        @pl.when(f == 0)
        def _():
            hn_ref[...] = x_ref[:, :D_MODEL].astype(BF16)

        hn = hn_ref[...]
        a = jnp.dot(hn, wg_ref[...], preferred_element_type=F32)
        u = jnp.dot(hn, wu_ref[...], preferred_element_type=F32)
        h = (a * jax.nn.sigmoid(a) * u).astype(BF16)
        y = jnp.dot(h, wd_ref[...], preferred_element_type=F32)

        @pl.when(f == 0)
        def _():
            acc_ref[...] = y

        @pl.when(f > 0)
        def _():
            acc_ref[...] += y

    @pl.when(jnp.logical_and(t >= nt_ref[0], f == 0))
    def _():
        acc_ref[...] = jnp.zeros_like(acc_ref)

    @pl.when(f == n_f - 1)
    def _():
        o_ref[...] = (acc_ref[...] * x_ref[:, D_MODEL:D_MODEL + 1]).astype(BF16)


def _experts(xs, tile_expert, n_tiles_used, wg_bf, wu_bf, wd_bf, tg):
    slots = xs.shape[0]
    n_f = MOE_F_BLOCKS
    tf = wg_bf.shape[2] // n_f

    def x_map(t, f, te, nt):
        return (jnp.minimum(t, nt[0] - 1), 0)

    def f_of(t, f, nt):
        return jnp.where(t < nt[0], f, n_f - 1)

    return pl.pallas_call(
        functools.partial(_expert_kernel, n_f=n_f),
        out_shape=jax.ShapeDtypeStruct((slots, D_MODEL), BF16),
        grid_spec=pltpu.PrefetchScalarGridSpec(
            num_scalar_prefetch=2,
            grid=(slots // tg, n_f),
            in_specs=[
                pl.BlockSpec((tg, D_MODEL + XS_EXTRA), x_map),
                pl.BlockSpec((None, D_MODEL, tf), lambda t, f, te, nt: (te[t], 0, f_of(t, f, nt))),
                pl.BlockSpec((None, D_MODEL, tf), lambda t, f, te, nt: (te[t], 0, f_of(t, f, nt))),
                pl.BlockSpec((None, tf, D_MODEL), lambda t, f, te, nt: (te[t], f_of(t, f, nt), 0)),
            ],
            out_specs=pl.BlockSpec((tg, D_MODEL), lambda t, f, te, nt: (t, 0)),
            scratch_shapes=[pltpu.VMEM((tg, D_MODEL), BF16), pltpu.VMEM((tg, D_MODEL), F32)],
        ),
        compiler_params=_params(("arbitrary", "arbitrary")),
        name="moe_experts",
    )(tile_expert, n_tiles_used, xs, wg_bf, wu_bf, wd_bf)


def _combine_kernel(ws_ref, hs_ref, route_ref, y_ref, o_ref, win_ref, sem, *, tc, tiles_per_seq, n_steps):
    n = pl.program_id(0) * tiles_per_seq + pl.program_id(1)
    cur = n % 2

    def window_copy(step, b, e):
        start = pl.multiple_of(ws_ref[step * N_EXPERTS + e], WIN_ALIGN)
        return pltpu.make_async_copy(y_ref.at[pl.ds(start, COMBINE_WIN)],
                                     win_ref.at[b, pl.ds(e * COMBINE_WIN, COMBINE_WIN)], sem.at[b])

    def fetch(step, b):
        for e in range(N_EXPERTS):
            window_copy(step, b, e).start()

    @pl.when(n == 0)
    def _():
        fetch(0, 0)

    @pl.when(n + 1 < n_steps)
    def _():
        fetch(n + 1, 1 - cur)

    for e in range(N_EXPERTS):
        window_copy(n, cur, e).wait()

    route = route_ref[...]
    e1, e2, s1, s2 = (route[:, c:c + 1] for c in range(4))
    pos = lax.broadcasted_iota(jnp.int32, (1, COMBINE_WIN), 1)
    picks = []
    for e in range(N_EXPERTS):
        start = ws_ref[n * N_EXPERTS + e]
        col = jnp.where(e1 == e, s1 - start, jnp.where(e2 == e, s2 - start, -1))
        picks.append(jnp.where(col == pos, 1.0, 0.0).astype(BF16))
    pick = jnp.concatenate(picks, axis=1)
    o_ref[...] = hs_ref[...] + jnp.dot(pick, win_ref[cur], preferred_element_type=F32)


def _combine(hs3, route, y, win_start, seq):
    bsz = hs3.shape[0]
    tc = COMBINE_TOKENS
    tiles_per_seq = seq // tc
    kern = functools.partial(_combine_kernel, tc=tc, tiles_per_seq=tiles_per_seq, n_steps=bsz * tiles_per_seq)
    tok = lambda w: pl.BlockSpec((tc, w), lambda b, i, ws: (b * tiles_per_seq + i, 0))
    return pl.pallas_call(
        kern,
        out_shape=jax.ShapeDtypeStruct((bsz, seq, D_MODEL), F32),
        grid_spec=pltpu.PrefetchScalarGridSpec(
            num_scalar_prefetch=1,
            grid=(bsz, tiles_per_seq),
            in_specs=[
                pl.BlockSpec((None, tc, D_MODEL), lambda b, i, ws: (b, i + T_PAD // tc, 0)),
                tok(4),
                pl.BlockSpec(memory_space=pl.ANY),
            ],
            out_specs=pl.BlockSpec((None, tc, D_MODEL), lambda b, i, ws: (b, i, 0)),
            scratch_shapes=[pltpu.VMEM((2, N_EXPERTS * COMBINE_WIN, D_MODEL), BF16),
                            pltpu.SemaphoreType.DMA((2,))],
        ),
        compiler_params=_params(("arbitrary", "arbitrary")),
        name="moe_combine",
    )(win_start, hs3, route, y)


def _moe(hs, gain, router_w, wg_bf, wu_bf, wd_bf, bsz, tp):
    seq = tp - T_PAD
    n_tok = bsz * seq
    tg = MOE_TILE
    idx, wts, cnt = _router(hs, gain, router_w, tp)
    sel = idx.reshape(bsz, tp, 128)[:, T_PAD:, :4].reshape(n_tok, 4)
    counts = cnt[0, :N_EXPERTS].astype(jnp.int32)
    padded = ((counts + tg - 1) // tg) * tg
    ends = jnp.cumsum(padded)
    starts = ends - padded
    experts = jnp.arange(N_EXPERTS, dtype=jnp.int32)[None, :]
    slot1 = jnp.sum(jnp.where(sel[:, 0:1] == experts, starts[None, :], 0), axis=1) + sel[:, 2]
    slot2 = jnp.sum(jnp.where(sel[:, 1:2] == experts, starts[None, :], 0), axis=1) + sel[:, 3]
    slot1, slot2 = slot1.astype(jnp.int32), slot2.astype(jnp.int32)
    n_slots = 2 * n_tok + N_EXPERTS * tg
    tile_start = jnp.arange(n_slots // tg, dtype=jnp.int32)[:, None] * tg
    tile_expert = jnp.minimum(jnp.sum((tile_start >= ends[None, :]).astype(jnp.int32), axis=1), N_EXPERTS - 1)
    n_tiles_used = (ends[-1:] // tg).astype(jnp.int32)
    tail = n_slots - tg * (1 + jnp.arange(N_EXPERTS, dtype=jnp.int32))
    zero_rows = jnp.concatenate([jnp.maximum(ends - tg, 0), tail]).astype(jnp.int32)

    tc = COMBINE_TOKENS
    first = jnp.minimum(
        jnp.min(jnp.where(sel[:, 0:1] == experts, slot1[:, None], n_slots).reshape(n_tok // tc, tc, N_EXPERTS), axis=1),
        jnp.min(jnp.where(sel[:, 1:2] == experts, slot2[:, None], n_slots).reshape(n_tok // tc, tc, N_EXPERTS), axis=1))
    win_start = jnp.where(first == n_slots, 0,
                          jnp.minimum(first // WIN_ALIGN * WIN_ALIGN, n_slots - COMBINE_WIN))
    route = jnp.concatenate([sel[:, :2], slot1[:, None], slot2[:, None]], axis=1)

    hs3 = hs.reshape(bsz, tp, D_MODEL)
    xs = _dispatch(hs3, wts.reshape(bsz, tp, 128), gain, slot1, slot2, zero_rows, n_slots, tg)
    y = _experts(xs, tile_expert.astype(jnp.int32), n_tiles_used, wg_bf, wu_bf, wd_bf, tg)
    return _combine(hs3, route, y, win_start.reshape(-1).astype(jnp.int32), seq)


def _permute_qk_cols(w):
    return w.reshape(D_MODEL, 2, ATT_HEADS, ATT_QK_DIM).transpose(0, 2, 1, 3).reshape(D_MODEL, 512)


def kernel(x, meta_tokens, norm1_gain, norm2_gain, w_in, q_norm_gain, k_norm_gain, diff_lambda,
           attn_sub_gain, rel_bias, conv_w, hgrn_lb_logits, hgrn_out_gain, w_branch, w_out,
           ffn_w_gate, ffn_w_up, ffn_w_down, router_w, moe_w_gate, moe_w_up, moe_w_down):
    bsz, seq, _ = x.shape
    depth = w_in.shape[0]
    tp = T_PAD + seq
    assert tp % ATT_TILE == 0 and depth == 2

    head = jnp.concatenate([jnp.zeros((PAD0, D_MODEL), x.dtype), meta_tokens.astype(x.dtype)], axis=0)
    hs = jnp.concatenate([jnp.broadcast_to(head[None], (bsz, T_PAD, D_MODEL)), x], axis=1)
    hs = hs.reshape(bsz * tp, D_MODEL)

    lb_all = jnp.cumsum(jax.nn.softmax(hgrn_lb_logits.astype(F32), axis=0), axis=0)
    lb_all = lb_all - lb_all[0]
    btab = _attn_bias_tables(rel_bias, ATT_TILE)

    out = None
    for layer in range(depth):
        w = w_in[layer]
        w_bf = jnp.concatenate([_permute_qk_cols(w[:, :512]), _permute_qk_cols(w[:, 512:1024]), w[:, 1024:]],
                               axis=1).astype(BF16)
        qk_gain = jnp.concatenate([jnp.tile(q_norm_gain[layer].astype(F32), 8) * (ATT_QK_DIM ** -0.5 * LOG2E),
                                   jnp.tile(k_norm_gain[layer].astype(F32), 8)]).reshape(1, COL_TILE)
        proj = _inproj(hs, norm1_gain[layer], w_bf, qk_gain)
        proj3 = proj.reshape(bsz, tp, IN_COLS)

        lam_init = 0.8 - 0.6 * math.exp(-0.3 * layer)
        lp = diff_lambda[layer].astype(F32)
        lam = jnp.exp(jnp.sum(lp[0] * lp[1])) - jnp.exp(jnp.sum(lp[2] * lp[3])) + lam_init
        u_att = _diff_attention(proj3, lam.reshape(1), attn_sub_gain[layer].astype(F32), btab,
                                1.0 - lam_init)

        lb = lb_all[layer]
        u_hgrn = _hgrn(proj3, jnp.log(lb), jnp.log1p(-lb), hgrn_out_gain[layer].astype(F32))

        hs = _merge(hs, u_att.reshape(bsz * tp, 512), u_hgrn.reshape(bsz * tp, 512), proj,
                    conv_w[layer].astype(F32), w_branch[layer].astype(BF16), w_out[layer].astype(BF16), tp)

        j = layer // 2
        if layer % 2 == 0:
            hs = _ffn(hs, norm2_gain[layer], ffn_w_gate[j].astype(BF16), ffn_w_up[j].astype(BF16),
                      ffn_w_down[j].astype(BF16))
        else:
            out = _moe(hs, norm2_gain[layer], router_w[j], moe_w_gate[j].astype(BF16),
                       moe_w_up[j].astype(BF16), moe_w_down[j].astype(BF16), bsz, tp)
    return out
```

```python
import functools
import math

import numpy as np
import jax
import jax.numpy as jnp
from jax import lax
from jax.experimental import pallas as pl
from jax.experimental.pallas import tpu as pltpu

F32 = jnp.float32
BF16 = jnp.bfloat16

D_MODEL = 1024
N_META = 16
EPS = 1e-6
ATT_HEADS = 4
ATT_QK_DIM = 64
ATT_V_DIM = 128
REL_BUCKETS = 32
REL_MAX_DIST = 128
CONV_K = 3
HGRN_HEADS = 4
HGRN_D = 128
N_EXPERTS = 8
IN_COLS = 8192

T_PAD = 128
PAD0 = T_PAD - N_META
ATT_TILE = 384
VT_ONES = 16
LOG2E = math.log2(math.e)
HGRN_CHUNK = 128
HGRN_SUB = 16
COL_TILE = 1024
GROUP_MEAN_WIDTH = 256
MOE_TILE = 512
MOE_F_BLOCKS = 2
XS_EXTRA = 128
COMBINE_TOKENS = 128
WIN_ALIGN = 16
COMBINE_WIN = 256
MASK = -1e30
VMEM_LIMIT = 56 * 1024 * 1024

BLK_Q, BLK_K, BLK_V = 0, 4, 8
BLK_RQ, BLK_RF, BLK_RI, BLK_RG = 24, 28, 32, 36
BLK_CB, BLK_CC, BLK_CH = 3, 4, 5
BLK_GATE = 5


def _row_tile(rows, target):
    n = rows // 128
    best = 1
    for d in range(1, n + 1):
        if n % d == 0 and d * 128 <= target:
            best = d
    return best * 128


def _params(sem, vmem=VMEM_LIMIT):
    return pltpu.CompilerParams(dimension_semantics=sem, vmem_limit_bytes=vmem)


def _inproj_kernel(x_ref, g_ref, w_ref, qkg_ref, gm_ref, o_ref, xn_ref):
    j = pl.program_id(1)

    @pl.when(j == 0)
    def _():
        x = x_ref[...]
        ms = jnp.mean(x * x, axis=-1, keepdims=True)
        xn_ref[...] = (x * lax.rsqrt(ms + EPS) * g_ref[...]).astype(BF16)

    acc = jnp.dot(xn_ref[...], w_ref[...], preferred_element_type=F32)

    @pl.when(j == 0)
    def _():
        sq = acc * acc
        hi = sq.astype(BF16)
        lo = (sq - hi.astype(F32)).astype(BF16)
        gm = gm_ref[...]
        width = gm.shape[0]
        ms = jnp.concatenate(
            [jnp.dot(hi[:, c:c + width], gm, preferred_element_type=F32)
             + jnp.dot(lo[:, c:c + width], gm, preferred_element_type=F32)
             for c in range(0, COL_TILE, width)], axis=1)
        o_ref[...] = (acc * lax.rsqrt(ms + EPS) * qkg_ref[...]).astype(BF16)

    @pl.when(j > 0)
    def _():
        o_ref[...] = acc.astype(BF16)


def _inproj(hs, gain, w_bf, qk_gain):
    rows = hs.shape[0]
    tm = _row_tile(rows, 1536)
    n_col = IN_COLS // COL_TILE
    assert COL_TILE == 4 * ATT_HEADS * ATT_QK_DIM
    grp = np.arange(GROUP_MEAN_WIDTH) // ATT_QK_DIM
    gm = jnp.asarray((grp[:, None] == grp[None, :]).astype(np.float32) / ATT_QK_DIM, BF16)
    return pl.pallas_call(
        _inproj_kernel,
        out_shape=jax.ShapeDtypeStruct((rows, IN_COLS), BF16),
        grid=(rows // tm, n_col),
        in_specs=[
            pl.BlockSpec((tm, D_MODEL), lambda i, j: (i, 0)),
            pl.BlockSpec((1, D_MODEL), lambda i, j: (0, 0)),
            pl.BlockSpec((D_MODEL, COL_TILE), lambda i, j: (0, j)),
            pl.BlockSpec((1, COL_TILE), lambda i, j: (0, 0)),
            pl.BlockSpec((GROUP_MEAN_WIDTH, GROUP_MEAN_WIDTH), lambda i, j: (0, 0)),
        ],
        out_specs=pl.BlockSpec((tm, COL_TILE), lambda i, j: (i, j)),
        scratch_shapes=[pltpu.VMEM((tm, D_MODEL), BF16)],
        compiler_params=_params(("parallel", "arbitrary")),
        name="inproj",
    )(hs, gain.reshape(1, D_MODEL), w_bf, qk_gain, gm)


def _rel_bucket_table(n_max):
    n = np.arange(n_max, dtype=np.int64)
    max_exact = REL_BUCKETS // 2
    nf = np.maximum(n, 1).astype(np.float32)
    large = max_exact + (np.log(nf / np.float32(max_exact)) / np.float32(math.log(REL_MAX_DIST / max_exact))
                         * np.float32(REL_BUCKETS - max_exact)).astype(np.int32)
    large = np.minimum(large, REL_BUCKETS - 1)
    return np.where(n < max_exact, n, large).astype(np.int32)


def _attn_bias_tables(rel_bias, t):
    bucket = _rel_bucket_table(2 * t)
    assert np.all(bucket[t + 1:] == REL_BUCKETS - 1) and np.all(np.diff(bucket) >= 0)
    first_dist = tuple(int(np.searchsorted(bucket, b, side="left")) for b in range(REL_BUCKETS))
    return pl.pallas_call(
        functools.partial(_bias_kernel, t=t, first_dist=first_dist),
        out_shape=jax.ShapeDtypeStruct((ATT_HEADS, 6, t, t), F32),
        grid=(ATT_HEADS,),
        in_specs=[pl.BlockSpec(memory_space=pltpu.SMEM)],
        out_specs=pl.BlockSpec((None, 6, t, t), lambda h: (h, 0, 0, 0)),
        compiler_params=_params(("parallel",)),
        name="attn_bias",
    )(rel_bias.astype(F32))


def _bias_kernel(rb_ref, o_ref, *, t, first_dist):
    h = pl.program_id(0)
    key = lax.broadcasted_iota(jnp.int32, (t, t), 0)
    qry = lax.broadcasted_iota(jnp.int32, (t, t), 1)
    far = rb_ref[REL_BUCKETS - 1, h]

    def table(n):
        val = jnp.full((t, t), rb_ref[0, h] - far, F32)
        for b in range(1, REL_BUCKETS):
            val = jnp.where(n >= first_dist[b], rb_ref[b, h] - far, val)
        return val * LOG2E

    n0 = qry - key
    diag = jnp.where(n0 >= 0, table(n0), MASK)
    near = table(n0 + t)
    zero = jnp.zeros((t, t), F32)
    for kind, tab in enumerate((diag, near, zero)):
        o_ref[kind] = tab
        o_ref[kind + 3] = jnp.where(key < PAD0, MASK, tab)


def _attn_kernel(lam_ref, q_ref, k_ref, v_ref, bt_ref, sg_ref, o_ref, m_ref, acc_ref, vt_ref,
                 *, t, out_scale):
    n_t = vt_ref.shape[0]
    for j in range(n_t):
        vt_ref[j, :ATT_V_DIM, :] = v_ref[j * t:(j + 1) * t, :].astype(F32).T.astype(BF16)
        vt_ref[j, ATT_V_DIM:, :] = jnp.ones((VT_ONES, t), BF16)
    lane = lax.broadcasted_iota(jnp.int32, (1, 2 * ATT_QK_DIM), 1)
    nt = (((1,), (1,)), ((), ()))

    def tile_rows(i):
        return pl.ds(pl.multiple_of(i * t, t), t)

    def scores(i, j):
        q = q_ref[tile_rows(i), :]
        zero = jnp.zeros_like(q)
        q_cat = jnp.concatenate([jnp.where(lane < ATT_QK_DIM, q, zero), jnp.where(lane >= ATT_QK_DIM, q, zero)],
                                axis=0)
        bias = bt_ref[jnp.minimum(i - j, 2) + jnp.where(j == 0, 3, 0)]
        s = (lax.dot_general(k_ref[tile_rows(j), :], q_cat, nt, preferred_element_type=F32)
             + jnp.concatenate([bias, bias], axis=1))
        return s, jnp.max(s, axis=0, keepdims=True)

    def consume(j, scored):
        s, s_max = scored
        m_old = m_ref[...]
        m_new = jnp.maximum(m_old, s_max)
        m_ref[...] = m_new
        p = jnp.exp2(s - m_new).astype(BF16)
        acc_ref[...] = (jnp.exp2(m_old - m_new) * acc_ref[...]
                        + jnp.dot(vt_ref[j], p, preferred_element_type=F32))

    def reset():
        m_ref[...] = jnp.full(m_ref.shape, MASK, F32)
        acc_ref[...] = jnp.zeros(acc_ref.shape, F32)

    def finish(i):
        acc = acc_ref[...]
        a1, a2 = acc[:, :t], acc[:, t:]
        o_t = (a1[:ATT_V_DIM] / a1[ATT_V_DIM:ATT_V_DIM + 1]
               - lam_ref[0] * (a2[:ATT_V_DIM] / a2[ATT_V_DIM:ATT_V_DIM + 1]))
        o = o_t.T
        ms = jnp.mean(o * o, axis=-1, keepdims=True)
        y = o * lax.rsqrt(ms + EPS) * (sg_ref[...] * out_scale)
        row = i * t + lax.broadcasted_iota(jnp.int32, (t, 1), 0)
        o_ref[tile_rows(i), :] = jnp.where(row >= PAD0, y, 0.0).astype(BF16)
        reset()

    def step(_, carry):
        i, j, s = carry
        last = j == i
        ni = jnp.where(last, i + 1, i)
        nj = jnp.where(last, 0, j + 1)
        nxt = scores(jnp.minimum(ni, n_t - 1), nj)
        consume(j, s)

        @pl.when(last)
        def _():
            finish(i)

        return ni, nj, nxt

    reset()
    first = jnp.int32(0)
    lax.fori_loop(0, n_t * (n_t + 1) // 2, step, (first, first, scores(first, first)))


def _diff_attention(proj3, lam, sub_gain, btab, out_scale):
    bsz, tp, _ = proj3.shape
    t = ATT_TILE
    n_t = tp // t
    rows = ATT_V_DIM + VT_ONES
    kern = functools.partial(_attn_kernel, t=t, out_scale=out_scale)
    seq = lambda blk: pl.BlockSpec((None, tp, 128), lambda b, h: (b, 0, blk + h))
    return pl.pallas_call(
        kern,
        out_shape=jax.ShapeDtypeStruct((bsz, tp, ATT_HEADS * ATT_V_DIM), BF16),
        grid=(bsz, ATT_HEADS),
        in_specs=[
            pl.BlockSpec(memory_space=pltpu.SMEM),
            seq(BLK_Q), seq(BLK_K), seq(BLK_V),
            pl.BlockSpec((None, 6, t, t), lambda b, h: (h, 0, 0, 0)),
            pl.BlockSpec((1, ATT_V_DIM), lambda b, h: (0, 0)),
        ],
        out_specs=pl.BlockSpec((None, tp, 128), lambda b, h: (b, 0, h)),
        scratch_shapes=[pltpu.VMEM((1, 2 * t), F32), pltpu.VMEM((rows, 2 * t), F32),
                        pltpu.VMEM((n_t, rows, t), BF16)],
        compiler_params=_params(("parallel", "parallel")),
        name="diff_attn",
    )(lam, proj3, proj3, proj3, btab, sub_gain.reshape(1, ATT_V_DIM))


def _split3(x):
    h1 = x.astype(BF16)
    r1 = x - h1.astype(F32)
    h2 = r1.astype(BF16)
    h3 = (r1 - h2.astype(F32)).astype(BF16)
    return h1, h2, h3


def _hgrn_kernel(q_ref, f_ref, i_ref, g_ref, la_ref, l1m_ref, og_ref, o_ref):
    c_len, sub = HGRN_CHUNK, HGRN_SUB
    half = sub // 2
    n_chunk = q_ref.shape[0] // c_len
    la, l1m, og = la_ref[...], l1m_ref[...], og_ref[...]
    rr = lax.broadcasted_iota(jnp.int32, (c_len, c_len), 0)
    cc = lax.broadcasted_iota(jnp.int32, (c_len, c_len), 1)
    tri = jnp.where(cc <= rr, 1.0, 0.0).astype(BF16)
    row8 = lax.broadcasted_iota(jnp.int32, (half, 1), 0)
    lane8 = lax.broadcasted_iota(jnp.int32, (half, c_len), 1)
    nt = (((1,), (1,)), ((), ()))

    def chunk(c, st):
        r0 = pl.multiple_of(c * c_len, c_len)
        rows = pl.ds(r0, c_len)
        z = f_ref[rows, :].astype(F32)
        qh = q_ref[rows, :].astype(F32)
        qh = qh * jax.nn.sigmoid(qh)
        v = i_ref[rows, :]
        gate = g_ref[rows, :].astype(F32)
        sp = jnp.log(1.0 + jnp.exp(-jnp.abs(z)))
        bb = l1m + jnp.minimum(z, 0.0) - sp
        log_f = jnp.maximum(la, bb) + jnp.log(1.0 + jnp.exp(-jnp.abs(la - bb)))
        valid = (r0 + lax.broadcasted_iota(jnp.int32, (c_len, 1), 0)) >= PAD0
        log_k = jnp.where(valid, l1m + jnp.minimum(-z, 0.0) - sp, -jnp.inf)
        g = sum(jnp.dot(tri, part, preferred_element_type=F32) for part in _split3(log_f))
        ck = log_k - g
        o_inter = lax.dot_general((qh * jnp.exp(g)).astype(BF16), st.astype(BF16), nt,
                                  preferred_element_type=F32)
        a_rows = []
        for a in range(c_len // sub):
            lo = a * sub
            ga = (g[lo:lo + half, :], g[lo + half:lo + sub, :])
            qa = (qh[lo:lo + half, :], qh[lo + half:lo + sub, :])
            if a == 0:
                blk = [jnp.zeros((half, c_len), F32)] * 2
            else:
                gs = g[lo - 1:lo, :]
                qd = (qh[lo:lo + sub, :] * jnp.exp(g[lo:lo + sub, :] - gs)).astype(BF16)
                kd = jnp.exp(jnp.minimum(gs - g[:lo, :], 0.0) + log_k[:lo, :]).astype(BF16)
                kd = jnp.concatenate([kd, jnp.zeros((c_len - lo, HGRN_D), BF16)], axis=0)
                a_off = lax.dot_general(qd, kd, nt, preferred_element_type=F32)
                blk = [a_off[:half, :], a_off[half:, :]]
            for s in range(sub):
                crow = ck[lo + s:lo + s + 1, :]
                for hh in range(s // half, 2):
                    col = jnp.sum(qa[hh] * jnp.exp(ga[hh] + crow), axis=-1, keepdims=True)
                    blk[hh] = jnp.where(lane8 == lo + s, col, blk[hh])
            for hh in range(2):
                a_rows.append(jnp.where(lane8 <= lo + hh * half + row8, blk[hh], 0.0))
        a_full = jnp.concatenate(a_rows, axis=0).astype(BF16)
        o = o_inter + jnp.dot(a_full, v, preferred_element_type=F32)
        g_last = g[c_len - 1:c_len, :]
        kd = jnp.exp(g_last - g + log_k).astype(BF16)
        st = st * jnp.exp(g_last) + lax.dot_general(v, kd, (((0,), (0,)), ((), ())),
                                                    preferred_element_type=F32)
        ms = jnp.mean(o * o, axis=-1, keepdims=True)
        y = o * lax.rsqrt(ms + EPS) * og * (gate * jax.nn.sigmoid(gate))
        o_ref[rows, :] = y.astype(BF16)
        return st

    lax.fori_loop(0, n_chunk, chunk, jnp.zeros((HGRN_D, HGRN_D), F32), unroll=3)


def _hgrn(proj3, log_lb, log1m_lb, out_gain):
    bsz, tp, _ = proj3.shape
    seq = lambda blk: pl.BlockSpec((None, tp, 128), lambda b, h: (b, 0, blk + h))
    chan = pl.BlockSpec((None, 1, HGRN_D), lambda b, h: (h, 0, 0))
    return pl.pallas_call(
        _hgrn_kernel,
        out_shape=jax.ShapeDtypeStruct((bsz, tp, HGRN_HEADS * HGRN_D), BF16),
        grid=(bsz, HGRN_HEADS),
        in_specs=[seq(BLK_RQ), seq(BLK_RF), seq(BLK_RI), seq(BLK_RG), chan, chan,
                  pl.BlockSpec((1, HGRN_D), lambda b, h: (0, 0))],
        out_specs=pl.BlockSpec((None, tp, 128), lambda b, h: (b, 0, h)),
        compiler_params=_params(("parallel", "parallel")),
        name="hgrn2",
    )(proj3, proj3, proj3, proj3,
      log_lb.reshape(HGRN_HEADS, 1, HGRN_D), log1m_lb.reshape(HGRN_HEADS, 1, HGRN_D),
      out_gain.reshape(1, HGRN_D))


def _merge_kernel(hs_ref, ua_ref, ur_ref, cb_ref, cc_ref, ch_ref, pc_ref, ph_ref,
                  g0_ref, g1_ref, g2_ref, cw_ref, wb_ref, wo_ref, o_ref, *, tm, tiles_per_seq):
    i = pl.program_id(0)
    row = (i % tiles_per_seq) * tm + lax.broadcasted_iota(jnp.int32, (tm, 1), 0)
    valid = row >= PAD0
    z = jnp.where(valid, cc_ref[...].astype(F32) * ch_ref[...].astype(F32), 0.0)
    halo_row = (i % tiles_per_seq) * tm - 8 + lax.broadcasted_iota(jnp.int32, (8, 1), 0)
    zp = jnp.where(halo_row >= PAD0, pc_ref[...].astype(F32) * ph_ref[...].astype(F32), 0.0)
    zz = jnp.concatenate([zp, z], axis=0)
    cw = cw_ref[...]
    y = (cw[2:3, :] * z + cw[1:2, :] * zz[7:7 + tm, :] + cw[0:1, :] * zz[6:6 + tm, :])
    u_conv = jnp.where(valid, cb_ref[...].astype(F32) * y, 0.0).astype(BF16)
    mixed = jnp.zeros((tm, D_MODEL), F32)
    for n, (u, g_ref) in enumerate(((ua_ref[...], g0_ref), (u_conv, g1_ref), (ur_ref[...], g2_ref))):
        up = jnp.dot(u, wb_ref[n], preferred_element_type=F32)
        mixed = mixed + jax.nn.sigmoid(g_ref[...].astype(F32)) * up
    o_ref[...] = hs_ref[...] + jnp.dot(mixed.astype(BF16), wo_ref[...], preferred_element_type=F32)


def _merge(hs, u_att, u_hgrn, proj, conv_w, wb_bf, wo_bf, tp):
    rows = hs.shape[0]
    tm = _row_tile(tp, 384)
    kern = functools.partial(_merge_kernel, tm=tm, tiles_per_seq=tp // tm)
    row_blk = lambda w, blk: pl.BlockSpec((tm, w), lambda i: (i, blk))
    halo = lambda blk: pl.BlockSpec((8, 512), lambda i: (jnp.maximum(i * (tm // 8) - 1, 0), blk))
    const = lambda shape: pl.BlockSpec(shape, lambda i: (0,) * len(shape))
    return pl.pallas_call(
        kern,
        out_shape=jax.ShapeDtypeStruct((rows, D_MODEL), F32),
        grid=(rows // tm,),
        in_specs=[row_blk(D_MODEL, 0), row_blk(512, 0), row_blk(512, 0),
                  row_blk(512, BLK_CB), row_blk(512, BLK_CC), row_blk(512, BLK_CH),
                  halo(BLK_CC), halo(BLK_CH),
                  row_blk(1024, BLK_GATE), row_blk(1024, BLK_GATE + 1), row_blk(1024, BLK_GATE + 2),
                  const((CONV_K, 512)), const((3, 512, D_MODEL)), const((D_MODEL, D_MODEL))],
        out_specs=row_blk(D_MODEL, 0),
        compiler_params=_params(("parallel",)),
        name="merge",
    )(hs, u_att, u_hgrn, proj, proj, proj, proj, proj, proj, proj, proj, conv_w, wb_bf, wo_bf)


def _ffn_kernel(hs_ref, g_ref, wg_ref, wu_ref, wd_ref, o_ref):
    x = hs_ref[...]
    ms = jnp.mean(x * x, axis=-1, keepdims=True)
    hn = (x * lax.rsqrt(ms + EPS) * g_ref[...]).astype(BF16)
    a = jnp.dot(hn, wg_ref[...], preferred_element_type=F32)
    u = jnp.dot(hn, wu_ref[...], preferred_element_type=F32)
    h = (a * jax.nn.sigmoid(a) * u).astype(BF16)
    o_ref[...] = x + jnp.dot(h, wd_ref[...], preferred_element_type=F32)


def _ffn(hs, gain, wg_bf, wu_bf, wd_bf):
    rows = hs.shape[0]
    d_ff = wg_bf.shape[1]
    tm = _row_tile(rows, 768)
    const = lambda shape: pl.BlockSpec(shape, lambda i: (0, 0), pipeline_mode=pl.Buffered(1))
    return pl.pallas_call(
        _ffn_kernel,
        out_shape=jax.ShapeDtypeStruct((rows, D_MODEL), F32),
        grid=(rows // tm,),
        in_specs=[pl.BlockSpec((tm, D_MODEL), lambda i: (i, 0)), const((1, D_MODEL)),
                  const((D_MODEL, d_ff)), const((D_MODEL, d_ff)), const((d_ff, D_MODEL))],
        out_specs=pl.BlockSpec((tm, D_MODEL), lambda i: (i, 0)),
        compiler_params=_params(("parallel",)),
        name="ffn_dense",
    )(hs, gain.reshape(1, D_MODEL), wg_bf, wu_bf, wd_bf)


def _router_kernel(hs_ref, g_ref, rw_ref, idx_ref, wt_ref, cnt_ref, carry_ref, *, tm, tiles_per_seq):
    i = pl.program_id(0)

    @pl.when(i == 0)
    def _():
        carry_ref[...] = jnp.zeros_like(carry_ref)

    x = hs_ref[...]
    ms = jnp.mean(x * x, axis=-1, keepdims=True)
    hn = x * lax.rsqrt(ms + EPS) * g_ref[...]
    logits = jnp.dot(hn, rw_ref[...], preferred_element_type=F32, precision=lax.Precision.HIGHEST)
    lane = lax.broadcasted_iota(jnp.int32, logits.shape, 1)
    lane_f = lane.astype(F32)
    logits = jnp.where(lane < N_EXPERTS, logits, -jnp.inf)
    m1 = jnp.max(logits, axis=-1, keepdims=True)
    i1 = jnp.min(jnp.where(logits == m1, lane_f, 128.0), axis=-1, keepdims=True)
    rest = jnp.where(lane_f == i1, -jnp.inf, logits)
    m2 = jnp.max(rest, axis=-1, keepdims=True)
    i2 = jnp.min(jnp.where(rest == m2, lane_f, 128.0), axis=-1, keepdims=True)
    e2 = jnp.exp(m2 - m1)
    w1 = 1.0 / (1.0 + e2)
    w2 = e2 / (1.0 + e2)
    row = (i % tiles_per_seq) * tm + lax.broadcasted_iota(jnp.int32, (tm, 1), 0)
    real = row >= T_PAD
    hot1 = jnp.where(jnp.logical_and(real, lane_f == i1), 1.0, 0.0)
    hot2 = jnp.where(jnp.logical_and(real, lane_f == i2), 1.0, 0.0)
    both = hot1 + hot2
    rr = lax.broadcasted_iota(jnp.int32, (tm, tm), 0)
    cc = lax.broadcasted_iota(jnp.int32, (tm, tm), 1)
    earlier = jnp.where(cc < rr, 1.0, 0.0).astype(BF16)
    before = carry_ref[...] + jnp.dot(earlier, both.astype(BF16), preferred_element_type=F32)
    r1 = jnp.sum(before * hot1, axis=-1, keepdims=True)
    r2 = jnp.sum(before * hot2, axis=-1, keepdims=True)
    carry_ref[...] += jnp.sum(both, axis=0, keepdims=True)
    cnt_ref[...] = carry_ref[...]
    packed = jnp.where(lane == 0, i1, jnp.where(lane == 1, i2, jnp.where(lane == 2, r1, jnp.where(lane == 3, r2, 0.0))))
    idx_ref[...] = packed.astype(jnp.int32)
    wt_ref[...] = jnp.where(lane == 0, w1, jnp.where(lane == 1, w2, 0.0))


def _router(hs, gain, router_w, tp):
    rows = hs.shape[0]
    tm = _row_tile(tp, 384)
    rw = jnp.zeros((D_MODEL, 128), F32).at[:, :N_EXPERTS].set(router_w.astype(F32))
    blk = pl.BlockSpec((tm, 128), lambda i: (i, 0))
    return pl.pallas_call(
        functools.partial(_router_kernel, tm=tm, tiles_per_seq=tp // tm),
        out_shape=(jax.ShapeDtypeStruct((rows, 128), jnp.int32), jax.ShapeDtypeStruct((rows, 128), F32),
                   jax.ShapeDtypeStruct((1, 128), F32)),
        grid=(rows // tm,),
        in_specs=[pl.BlockSpec((tm, D_MODEL), lambda i: (i, 0)),
                  pl.BlockSpec((1, D_MODEL), lambda i: (0, 0)),
                  pl.BlockSpec((D_MODEL, 128), lambda i: (0, 0))],
        out_specs=(blk, blk, pl.BlockSpec((1, 128), lambda i: (0, 0))),
        scratch_shapes=[pltpu.VMEM((1, 128), F32)],
        compiler_params=_params(("arbitrary",)),
        name="moe_router",
    )(hs, gain.reshape(1, D_MODEL), rw)


def _dispatch_kernel(s1_ref, s2_ref, zr_ref, hs_ref, wt_ref, g_ref, xs_ref, rows_ref, zero_ref, sem, zsem,
                     *, td, tg, tiles_per_seq, n_steps):
    n = pl.program_id(0) * tiles_per_seq + pl.program_id(1)
    buf = n % 2
    base = n * td

    def row_copy(r, which):
        slot = (s1_ref, s2_ref)[which][base + r]
        return pltpu.make_async_copy(rows_ref.at[buf, which, pl.ds(r, 1)], xs_ref.at[pl.ds(slot, 1)],
                                     sem.at[buf])

    def wait_rows(b):
        def body(r, c):
            pltpu.make_async_copy(rows_ref.at[b, 0, pl.ds(0, 1)], xs_ref.at[pl.ds(0, 1)], sem.at[b]).wait()
            return c
        lax.fori_loop(0, 2 * td, body, 0)

    @pl.when(n == 0)
    def _():
        zero_ref[...] = jnp.zeros_like(zero_ref)
        for e in range(2 * N_EXPERTS):
            fill = pltpu.make_async_copy(zero_ref, xs_ref.at[pl.ds(pl.multiple_of(zr_ref[e], tg), tg)], zsem)
            fill.start()
            fill.wait()

    @pl.when(n >= 2)
    def _():
        wait_rows(buf)

    x = hs_ref[...]
    ms = jnp.mean(x * x, axis=-1, keepdims=True)
    hn = x * lax.rsqrt(ms + EPS) * g_ref[...]
    wt = wt_ref[...]
    for which in range(2):
        rows_ref[buf, which, :, :D_MODEL] = hn
        rows_ref[buf, which, :, D_MODEL:] = jnp.broadcast_to(wt[:, which:which + 1], (td, XS_EXTRA))

    def issue(r, c):
        row_copy(r, 0).start()
        row_copy(r, 1).start()
        return c

    lax.fori_loop(0, td, issue, 0)

    @pl.when(n == n_steps - 1)
    def _():
        wait_rows(buf)
        if n_steps > 1:
            wait_rows(1 - buf)


def _dispatch(hs3, wts3, gain, slot1, slot2, zero_rows, n_slots, tg):
    bsz, tp, _ = hs3.shape
    td = T_PAD
    tiles_per_seq = (tp - T_PAD) // td
    width = D_MODEL + XS_EXTRA
    kern = functools.partial(_dispatch_kernel, td=td, tg=tg, tiles_per_seq=tiles_per_seq,
                             n_steps=bsz * tiles_per_seq)
    return pl.pallas_call(
        kern,
        out_shape=jax.ShapeDtypeStruct((n_slots, width), F32),
        grid_spec=pltpu.PrefetchScalarGridSpec(
            num_scalar_prefetch=3,
            grid=(bsz, tiles_per_seq),
            in_specs=[pl.BlockSpec((None, td, D_MODEL), lambda b, i, *_: (b, i + 1, 0)),
                      pl.BlockSpec((None, td, 128), lambda b, i, *_: (b, i + 1, 0)),
                      pl.BlockSpec((1, D_MODEL), lambda b, i, *_: (0, 0))],
            out_specs=pl.BlockSpec(memory_space=pl.ANY),
            scratch_shapes=[pltpu.VMEM((2, 2, td, width), F32), pltpu.VMEM((tg, width), F32),
                            pltpu.SemaphoreType.DMA((2,)), pltpu.SemaphoreType.DMA],
        ),
        compiler_params=_params(("arbitrary", "arbitrary")),
        name="moe_dispatch",
    )(slot1, slot2, zero_rows, hs3, wts3, gain.reshape(1, D_MODEL))


def _expert_kernel(te_ref, nt_ref, x_ref, wg_ref, wu_ref, wd_ref, o_ref, hn_ref, acc_ref, *, n_f):
    t = pl.program_id(0)
    f = pl.program_id(1)

    @pl.when(t < nt_ref[0])
    def _():
        @pl.when(f == 0)
        def _():
            hn_ref[...] = x_ref[:, :D_MODEL].astype(BF16)

        hn = hn_ref[...]
        a = jnp.dot(hn, wg_ref[...], preferred_element_type=F32)
        u = jnp.dot(hn, wu_ref[...], preferred_element_type=F32)
        h = (a * jax.nn.sigmoid(a) * u).astype(BF16)
        y = jnp.dot(h, wd_ref[...], preferred_element_type=F32)

        @pl.when(f == 0)
        def _():
            acc_ref[...] = y

        @pl.when(f > 0)
        def _():
            acc_ref[...] += y

    @pl.when(jnp.logical_and(t >= nt_ref[0], f == 0))
    def _():
        acc_ref[...] = jnp.zeros_like(acc_ref)

    @pl.when(f == n_f - 1)
    def _():
        o_ref[...] = (acc_ref[...] * x_ref[:, D_MODEL:D_MODEL + 1]).astype(BF16)


def _experts(xs, tile_expert, n_tiles_used, wg_bf, wu_bf, wd_bf, tg):
    slots = xs.shape[0]
    n_f = MOE_F_BLOCKS
    tf = wg_bf.shape[2] // n_f

    def x_map(t, f, te, nt):
        return (jnp.minimum(t, nt[0] - 1), 0)

    def f_of(t, f, nt):
        return jnp.where(t < nt[0], f, n_f - 1)

    return pl.pallas_call(
        functools.partial(_expert_kernel, n_f=n_f),
        out_shape=jax.ShapeDtypeStruct((slots, D_MODEL), BF16),
        grid_spec=pltpu.PrefetchScalarGridSpec(
            num_scalar_prefetch=2,
            grid=(slots // tg, n_f),
            in_specs=[
                pl.BlockSpec((tg, D_MODEL + XS_EXTRA), x_map),
                pl.BlockSpec((None, D_MODEL, tf), lambda t, f, te, nt: (te[t], 0, f_of(t, f, nt))),
                pl.BlockSpec((None, D_MODEL, tf), lambda t, f, te, nt: (te[t], 0, f_of(t, f, nt))),
                pl.BlockSpec((None, tf, D_MODEL), lambda t, f, te, nt: (te[t], f_of(t, f, nt), 0)),
            ],
            out_specs=pl.BlockSpec((tg, D_MODEL), lambda t, f, te, nt: (t, 0)),
            scratch_shapes=[pltpu.VMEM((tg, D_MODEL), BF16), pltpu.VMEM((tg, D_MODEL), F32)],
        ),
        compiler_params=_params(("arbitrary", "arbitrary")),
        name="moe_experts",
    )(tile_expert, n_tiles_used, xs, wg_bf, wu_bf, wd_bf)


def _combine_kernel(ws_ref, hs_ref, route_ref, y_ref, o_ref, win_ref, sem, *, tc, tiles_per_seq, n_steps):
    n = pl.program_id(0) * tiles_per_seq + pl.program_id(1)
    cur = n % 2

    def window_copy(step, b, e):
        start = pl.multiple_of(ws_ref[step * N_EXPERTS + e], WIN_ALIGN)
        return pltpu.make_async_copy(y_ref.at[pl.ds(start, COMBINE_WIN)],
                                     win_ref.at[b, pl.ds(e * COMBINE_WIN, COMBINE_WIN)], sem.at[b])

    def fetch(step, b):
        for e in range(N_EXPERTS):
            window_copy(step, b, e).start()

    @pl.when(n == 0)
    def _():
        fetch(0, 0)

    @pl.when(n + 1 < n_steps)
    def _():
        fetch(n + 1, 1 - cur)

    for e in range(N_EXPERTS):
        window_copy(n, cur, e).wait()

    route = route_ref[...]
    e1, e2, s1, s2 = (route[:, c:c + 1] for c in range(4))
    pos = lax.broadcasted_iota(jnp.int32, (1, COMBINE_WIN), 1)
    picks = []
    for e in range(N_EXPERTS):
        start = ws_ref[n * N_EXPERTS + e]
        col = jnp.where(e1 == e, s1 - start, jnp.where(e2 == e, s2 - start, -1))
        picks.append(jnp.where(col == pos, 1.0, 0.0).astype(BF16))
    pick = jnp.concatenate(picks, axis=1)
    o_ref[...] = hs_ref[...] + jnp.dot(pick, win_ref[cur], preferred_element_type=F32)


def _combine(hs3, route, y, win_start, seq):
    bsz = hs3.shape[0]
    tc = COMBINE_TOKENS
    tiles_per_seq = seq // tc
    kern = functools.partial(_combine_kernel, tc=tc, tiles_per_seq=tiles_per_seq, n_steps=bsz * tiles_per_seq)
    tok = lambda w: pl.BlockSpec((tc, w), lambda b, i, ws: (b * tiles_per_seq + i, 0))
    return pl.pallas_call(
        kern,
        out_shape=jax.ShapeDtypeStruct((bsz, seq, D_MODEL), F32),
        grid_spec=pltpu.PrefetchScalarGridSpec(
            num_scalar_prefetch=1,
            grid=(bsz, tiles_per_seq),
            in_specs=[
                pl.BlockSpec((None, tc, D_MODEL), lambda b, i, ws: (b, i + T_PAD // tc, 0)),
                tok(4),
                pl.BlockSpec(memory_space=pl.ANY),
            ],
            out_specs=pl.BlockSpec((None, tc, D_MODEL), lambda b, i, ws: (b, i, 0)),
            scratch_shapes=[pltpu.VMEM((2, N_EXPERTS * COMBINE_WIN, D_MODEL), BF16),
                            pltpu.SemaphoreType.DMA((2,))],
        ),
        compiler_params=_params(("arbitrary", "arbitrary")),
        name="moe_combine",
    )(win_start, hs3, route, y)


def _moe(hs, gain, router_w, wg_bf, wu_bf, wd_bf, bsz, tp):
    seq = tp - T_PAD
    n_tok = bsz * seq
    tg = MOE_TILE
    idx, wts, cnt = _router(hs, gain, router_w, tp)
    sel = idx.reshape(bsz, tp, 128)[:, T_PAD:, :4].reshape(n_tok, 4)
    counts = cnt[0, :N_EXPERTS].astype(jnp.int32)
    padded = ((counts + tg - 1) // tg) * tg
    ends = jnp.cumsum(padded)
    starts = ends - padded
    experts = jnp.arange(N_EXPERTS, dtype=jnp.int32)[None, :]
    slot1 = jnp.sum(jnp.where(sel[:, 0:1] == experts, starts[None, :], 0), axis=1) + sel[:, 2]
    slot2 = jnp.sum(jnp.where(sel[:, 1:2] == experts, starts[None, :], 0), axis=1) + sel[:, 3]
    slot1, slot2 = slot1.astype(jnp.int32), slot2.astype(jnp.int32)
    n_slots = 2 * n_tok + N_EXPERTS * tg
    tile_start = jnp.arange(n_slots // tg, dtype=jnp.int32)[:, None] * tg
    tile_expert = jnp.minimum(jnp.sum((tile_start >= ends[None, :]).astype(jnp.int32), axis=1), N_EXPERTS - 1)
    n_tiles_used = (ends[-1:] // tg).astype(jnp.int32)
    tail = n_slots - tg * (1 + jnp.arange(N_EXPERTS, dtype=jnp.int32))
    zero_rows = jnp.concatenate([jnp.maximum(ends - tg, 0), tail]).astype(jnp.int32)

    tc = COMBINE_TOKENS
    first = jnp.minimum(
        jnp.min(jnp.where(sel[:, 0:1] == experts, slot1[:, None], n_slots).reshape(n_tok // tc, tc, N_EXPERTS), axis=1),
        jnp.min(jnp.where(sel[:, 1:2] == experts, slot2[:, None], n_slots).reshape(n_tok // tc, tc, N_EXPERTS), axis=1))
    win_start = jnp.where(first == n_slots, 0,
                          jnp.minimum(first // WIN_ALIGN * WIN_ALIGN, n_slots - COMBINE_WIN))
    route = jnp.concatenate([sel[:, :2], slot1[:, None], slot2[:, None]], axis=1)

    hs3 = hs.reshape(bsz, tp, D_MODEL)
    xs = _dispatch(hs3, wts.reshape(bsz, tp, 128), gain, slot1, slot2, zero_rows, n_slots, tg)
    y = _experts(xs, tile_expert.astype(jnp.int32), n_tiles_used, wg_bf, wu_bf, wd_bf, tg)
    return _combine(hs3, route, y, win_start.reshape(-1).astype(jnp.int32), seq)


def _permute_qk_cols(w):
    return w.reshape(D_MODEL, 2, ATT_HEADS, ATT_QK_DIM).transpose(0, 2, 1, 3).reshape(D_MODEL, 512)


def kernel(x, meta_tokens, norm1_gain, norm2_gain, w_in, q_norm_gain, k_norm_gain, diff_lambda,
           attn_sub_gain, rel_bias, conv_w, hgrn_lb_logits, hgrn_out_gain, w_branch, w_out,
           ffn_w_gate, ffn_w_up, ffn_w_down, router_w, moe_w_gate, moe_w_up, moe_w_down):
    bsz, seq, _ = x.shape
    depth = w_in.shape[0]
    tp = T_PAD + seq
    assert tp % ATT_TILE == 0 and depth == 2

    head = jnp.concatenate([jnp.zeros((PAD0, D_MODEL), x.dtype), meta_tokens.astype(x.dtype)], axis=0)
    hs = jnp.concatenate([jnp.broadcast_to(head[None], (bsz, T_PAD, D_MODEL)), x], axis=1)
    hs = hs.reshape(bsz * tp, D_MODEL)

    lb_all = jnp.cumsum(jax.nn.softmax(hgrn_lb_logits.astype(F32), axis=0), axis=0)
    lb_all = lb_all - lb_all[0]
    btab = _attn_bias_tables(rel_bias, ATT_TILE)

    out = None
    for layer in range(depth):
        w = w_in[layer]
        w_bf = jnp.concatenate([_permute_qk_cols(w[:, :512]), _permute_qk_cols(w[:, 512:1024]), w[:, 1024:]],
                               axis=1).astype(BF16)
        qk_gain = jnp.concatenate([jnp.tile(q_norm_gain[layer].astype(F32), 8) * (ATT_QK_DIM ** -0.5 * LOG2E),
                                   jnp.tile(k_norm_gain[layer].astype(F32), 8)]).reshape(1, COL_TILE)
        proj = _inproj(hs, norm1_gain[layer], w_bf, qk_gain)
        proj3 = proj.reshape(bsz, tp, IN_COLS)

        lam_init = 0.8 - 0.6 * math.exp(-0.3 * layer)
        lp = diff_lambda[layer].astype(F32)
        lam = jnp.exp(jnp.sum(lp[0] * lp[1])) - jnp.exp(jnp.sum(lp[2] * lp[3])) + lam_init
        u_att = _diff_attention(proj3, lam.reshape(1), attn_sub_gain[layer].astype(F32), btab,
                                1.0 - lam_init)

        lb = lb_all[layer]
        u_hgrn = _hgrn(proj3, jnp.log(lb), jnp.log1p(-lb), hgrn_out_gain[layer].astype(F32))

        hs = _merge(hs, u_att.reshape(bsz * tp, 512), u_hgrn.reshape(bsz * tp, 512), proj,
                    conv_w[layer].astype(F32), w_branch[layer].astype(BF16), w_out[layer].astype(BF16), tp)

        j = layer // 2
        if layer % 2 == 0:
            hs = _ffn(hs, norm2_gain[layer], ffn_w_gate[j].astype(BF16), ffn_w_up[j].astype(BF16),
                      ffn_w_down[j].astype(BF16))
        else:
            out = _moe(hs, norm2_gain[layer], router_w[j], moe_w_gate[j].astype(BF16),
                       moe_w_up[j].astype(BF16), moe_w_down[j].astype(BF16), bsz, tp)
    return out
```

```python
import functools
import math

import numpy as np
import jax
import jax.numpy as jnp
from jax import lax
from jax.experimental import pallas as pl
from jax.experimental.pallas import tpu as pltpu

F32 = jnp.float32
BF16 = jnp.bfloat16

D_MODEL = 1024
N_META = 16
EPS = 1e-6
ATT_HEADS = 4
ATT_QK_DIM = 64
ATT_V_DIM = 128
REL_BUCKETS = 32
REL_MAX_DIST = 128
CONV_K = 3
HGRN_HEADS = 4
HGRN_D = 128
N_EXPERTS = 8
IN_COLS = 8192

T_PAD = 128
PAD0 = T_PAD - N_META
ATT_TILE = 384
VT_ONES = 16
LOG2E = math.log2(math.e)
HGRN_CHUNK = 128
HGRN_SUB = 16
COL_TILE = 1024
GROUP_MEAN_WIDTH = 256
MOE_TILE = 512
MOE_F_BLOCKS = 2
XS_EXTRA = 128
COMBINE_TOKENS = 128
WIN_ALIGN = 16
COMBINE_WIN = 256
MASK = -1e30
VMEM_LIMIT = 56 * 1024 * 1024

BLK_Q, BLK_K, BLK_V = 0, 4, 8
BLK_RQ, BLK_RF, BLK_RI, BLK_RG = 24, 28, 32, 36
BLK_CB, BLK_CC, BLK_CH = 3, 4, 5
BLK_GATE = 5


def _row_tile(rows, target):
    n = rows // 128
    best = 1
    for d in range(1, n + 1):
        if n % d == 0 and d * 128 <= target:
            best = d
    return best * 128


def _params(sem, vmem=VMEM_LIMIT):
    return pltpu.CompilerParams(dimension_semantics=sem, vmem_limit_bytes=vmem)


def _inproj_kernel(x_ref, g_ref, w_ref, qkg_ref, gm_ref, o_ref, xn_ref):
    j = pl.program_id(1)

    @pl.when(j == 0)
    def _():
        x = x_ref[...]
        ms = jnp.mean(x * x, axis=-1, keepdims=True)
        xn_ref[...] = (x * lax.rsqrt(ms + EPS) * g_ref[...]).astype(BF16)

    acc = jnp.dot(xn_ref[...], w_ref[...], preferred_element_type=F32)

    @pl.when(j == 0)
    def _():
        sq = acc * acc
        hi = sq.astype(BF16)
        lo = (sq - hi.astype(F32)).astype(BF16)
        gm = gm_ref[...]
        width = gm.shape[0]
        ms = jnp.concatenate(
            [jnp.dot(hi[:, c:c + width], gm, preferred_element_type=F32)
             + jnp.dot(lo[:, c:c + width], gm, preferred_element_type=F32)
             for c in range(0, COL_TILE, width)], axis=1)
        o_ref[...] = (acc * lax.rsqrt(ms + EPS) * qkg_ref[...]).astype(BF16)

    @pl.when(j > 0)
    def _():
        o_ref[...] = acc.astype(BF16)


def _inproj(hs, gain, w_bf, qk_gain):
    rows = hs.shape[0]
    tm = _row_tile(rows, 1536)
    n_col = IN_COLS // COL_TILE
    assert COL_TILE == 4 * ATT_HEADS * ATT_QK_DIM
    grp = np.arange(GROUP_MEAN_WIDTH) // ATT_QK_DIM
    gm = jnp.asarray((grp[:, None] == grp[None, :]).astype(np.float32) / ATT_QK_DIM, BF16)
    return pl.pallas_call(
        _inproj_kernel,
        out_shape=jax.ShapeDtypeStruct((rows, IN_COLS), BF16),
        grid=(rows // tm, n_col),
        in_specs=[
            pl.BlockSpec((tm, D_MODEL), lambda i, j: (i, 0)),
            pl.BlockSpec((1, D_MODEL), lambda i, j: (0, 0)),
            pl.BlockSpec((D_MODEL, COL_TILE), lambda i, j: (0, j)),
            pl.BlockSpec((1, COL_TILE), lambda i, j: (0, 0)),
            pl.BlockSpec((GROUP_MEAN_WIDTH, GROUP_MEAN_WIDTH), lambda i, j: (0, 0)),
        ],
        out_specs=pl.BlockSpec((tm, COL_TILE), lambda i, j: (i, j)),
        scratch_shapes=[pltpu.VMEM((tm, D_MODEL), BF16)],
        compiler_params=_params(("parallel", "arbitrary")),
        name="inproj",
    )(hs, gain.reshape(1, D_MODEL), w_bf, qk_gain, gm)


def _rel_bucket_table(n_max):
    n = np.arange(n_max, dtype=np.int64)
    max_exact = REL_BUCKETS // 2
    nf = np.maximum(n, 1).astype(np.float32)
    large = max_exact + (np.log(nf / np.float32(max_exact)) / np.float32(math.log(REL_MAX_DIST / max_exact))
                         * np.float32(REL_BUCKETS - max_exact)).astype(np.int32)
    large = np.minimum(large, REL_BUCKETS - 1)
    return np.where(n < max_exact, n, large).astype(np.int32)


def _attn_bias_tables(rel_bias, t):
    bucket = _rel_bucket_table(2 * t)
    assert np.all(bucket[t + 1:] == REL_BUCKETS - 1) and np.all(np.diff(bucket) >= 0)
    first_dist = tuple(int(np.searchsorted(bucket, b, side="left")) for b in range(REL_BUCKETS))
    return pl.pallas_call(
        functools.partial(_bias_kernel, t=t, first_dist=first_dist),
        out_shape=jax.ShapeDtypeStruct((ATT_HEADS, 6, t, t), F32),
        grid=(ATT_HEADS,),
        in_specs=[pl.BlockSpec(memory_space=pltpu.SMEM)],
        out_specs=pl.BlockSpec((None, 6, t, t), lambda h: (h, 0, 0, 0)),
        compiler_params=_params(("parallel",)),
        name="attn_bias",
    )(rel_bias.astype(F32))


def _bias_kernel(rb_ref, o_ref, *, t, first_dist):
    h = pl.program_id(0)
    key = lax.broadcasted_iota(jnp.int32, (t, t), 0)
    qry = lax.broadcasted_iota(jnp.int32, (t, t), 1)
    far = rb_ref[REL_BUCKETS - 1, h]

    def table(n):
        val = jnp.full((t, t), rb_ref[0, h] - far, F32)
        for b in range(1, REL_BUCKETS):
            val = jnp.where(n >= first_dist[b], rb_ref[b, h] - far, val)
        return val * LOG2E

    n0 = qry - key
    diag = jnp.where(n0 >= 0, table(n0), MASK)
    near = table(n0 + t)
    zero = jnp.zeros((t, t), F32)
    for kind, tab in enumerate((diag, near, zero)):
        o_ref[kind] = tab
        o_ref[kind + 3] = jnp.where(key < PAD0, MASK, tab)


def _attn_kernel(lam_ref, q_ref, k_ref, v_ref, bt_ref, sg_ref, o_ref, m_ref, acc_ref, vt_ref,
                 *, t, out_scale):
    n_t = vt_ref.shape[0]
    for j in range(n_t):
        vt_ref[j, :ATT_V_DIM, :] = v_ref[j * t:(j + 1) * t, :].astype(F32).T.astype(BF16)
        vt_ref[j, ATT_V_DIM:, :] = jnp.ones((VT_ONES, t), BF16)
    lane = lax.broadcasted_iota(jnp.int32, (1, 2 * ATT_QK_DIM), 1)
    nt = (((1,), (1,)), ((), ()))

    def tile_rows(i):
        return pl.ds(pl.multiple_of(i * t, t), t)

    def scores(i, j):
        q = q_ref[tile_rows(i), :]
        zero = jnp.zeros_like(q)
        q_cat = jnp.concatenate([jnp.where(lane < ATT_QK_DIM, q, zero), jnp.where(lane >= ATT_QK_DIM, q, zero)],
                                axis=0)
        bias = bt_ref[jnp.minimum(i - j, 2) + jnp.where(j == 0, 3, 0)]
        s = (lax.dot_general(k_ref[tile_rows(j), :], q_cat, nt, preferred_element_type=F32)
             + jnp.concatenate([bias, bias], axis=1))
        return s, jnp.max(s, axis=0, keepdims=True)

    def consume(j, scored):
        s, s_max = scored
        m_old = m_ref[...]
        m_new = jnp.maximum(m_old, s_max)
        m_ref[...] = m_new
        p = jnp.exp2(s - m_new).astype(BF16)
        acc_ref[...] = (jnp.exp2(m_old - m_new) * acc_ref[...]
                        + jnp.dot(vt_ref[j], p, preferred_element_type=F32))

    def reset():
        m_ref[...] = jnp.full(m_ref.shape, MASK, F32)
        acc_ref[...] = jnp.zeros(acc_ref.shape, F32)

    def finish(i):
        acc = acc_ref[...]
        a1, a2 = acc[:, :t], acc[:, t:]
        o_t = (a1[:ATT_V_DIM] / a1[ATT_V_DIM:ATT_V_DIM + 1]
               - lam_ref[0] * (a2[:ATT_V_DIM] / a2[ATT_V_DIM:ATT_V_DIM + 1]))
        o = o_t.T
        ms = jnp.mean(o * o, axis=-1, keepdims=True)
        y = o * lax.rsqrt(ms + EPS) * (sg_ref[...] * out_scale)
        row = i * t + lax.broadcasted_iota(jnp.int32, (t, 1), 0)
        o_ref[tile_rows(i), :] = jnp.where(row >= PAD0, y, 0.0).astype(BF16)
        reset()

    def step(_, carry):
        i, j, s = carry
        last = j == i
        ni = jnp.where(last, i + 1, i)
        nj = jnp.where(last, 0, j + 1)
        nxt = scores(jnp.minimum(ni, n_t - 1), nj)
        consume(j, s)

        @pl.when(last)
        def _():
            finish(i)

        return ni, nj, nxt

    reset()
    first = jnp.int32(0)
    lax.fori_loop(0, n_t * (n_t + 1) // 2, step, (first, first, scores(first, first)))


def _diff_attention(proj3, lam, sub_gain, btab, out_scale):
    bsz, tp, _ = proj3.shape
    t = ATT_TILE
    n_t = tp // t
    rows = ATT_V_DIM + VT_ONES
    kern = functools.partial(_attn_kernel, t=t, out_scale=out_scale)
    seq = lambda blk: pl.BlockSpec((None, tp, 128), lambda b, h: (b, 0, blk + h))
    return pl.pallas_call(
        kern,
        out_shape=jax.ShapeDtypeStruct((bsz, tp, ATT_HEADS * ATT_V_DIM), BF16),
        grid=(bsz, ATT_HEADS),
        in_specs=[
            pl.BlockSpec(memory_space=pltpu.SMEM),
            seq(BLK_Q), seq(BLK_K), seq(BLK_V),
            pl.BlockSpec((None, 6, t, t), lambda b, h: (h, 0, 0, 0)),
            pl.BlockSpec((1, ATT_V_DIM), lambda b, h: (0, 0)),
        ],
        out_specs=pl.BlockSpec((None, tp, 128), lambda b, h: (b, 0, h)),
        scratch_shapes=[pltpu.VMEM((1, 2 * t), F32), pltpu.VMEM((rows, 2 * t), F32),
                        pltpu.VMEM((n_t, rows, t), BF16)],
        compiler_params=_params(("parallel", "parallel")),
        name="diff_attn",
    )(lam, proj3, proj3, proj3, btab, sub_gain.reshape(1, ATT_V_DIM))


def _split3(x):
    h1 = x.astype(BF16)
    r1 = x - h1.astype(F32)
    h2 = r1.astype(BF16)
    h3 = (r1 - h2.astype(F32)).astype(BF16)
    return h1, h2, h3


def _hgrn_kernel(q_ref, f_ref, i_ref, g_ref, la_ref, l1m_ref, og_ref, o_ref):
    c_len, sub = HGRN_CHUNK, HGRN_SUB
    half = sub // 2
    n_chunk = q_ref.shape[0] // c_len
    la, l1m, og = la_ref[...], l1m_ref[...], og_ref[...]
    rr = lax.broadcasted_iota(jnp.int32, (c_len, c_len), 0)
    cc = lax.broadcasted_iota(jnp.int32, (c_len, c_len), 1)
    tri = jnp.where(cc <= rr, 1.0, 0.0).astype(BF16)
    row8 = lax.broadcasted_iota(jnp.int32, (half, 1), 0)
    lane8 = lax.broadcasted_iota(jnp.int32, (half, c_len), 1)
    nt = (((1,), (1,)), ((), ()))

    def chunk(c, st):
        r0 = pl.multiple_of(c * c_len, c_len)
        rows = pl.ds(r0, c_len)
        z = f_ref[rows, :].astype(F32)
        qh = q_ref[rows, :].astype(F32)
        qh = qh * jax.nn.sigmoid(qh)
        v = i_ref[rows, :]
        gate = g_ref[rows, :].astype(F32)
        sp = jnp.log(1.0 + jnp.exp(-jnp.abs(z)))
        bb = l1m + jnp.minimum(z, 0.0) - sp
        log_f = jnp.maximum(la, bb) + jnp.log(1.0 + jnp.exp(-jnp.abs(la - bb)))
        valid = (r0 + lax.broadcasted_iota(jnp.int32, (c_len, 1), 0)) >= PAD0
        log_k = jnp.where(valid, l1m + jnp.minimum(-z, 0.0) - sp, -jnp.inf)
        g = sum(jnp.dot(tri, part, preferred_element_type=F32) for part in _split3(log_f))
        ck = log_k - g
        o_inter = lax.dot_general((qh * jnp.exp(g)).astype(BF16), st.astype(BF16), nt,
                                  preferred_element_type=F32)
        a_rows = []
        for a in range(c_len // sub):
            lo = a * sub
            ga = (g[lo:lo + half, :], g[lo + half:lo + sub, :])
            qa = (qh[lo:lo + half, :], qh[lo + half:lo + sub, :])
            if a == 0:
                blk = [jnp.zeros((half, c_len), F32)] * 2
            else:
                gs = g[lo - 1:lo, :]
                qd = (qh[lo:lo + sub, :] * jnp.exp(g[lo:lo + sub, :] - gs)).astype(BF16)
                kd = jnp.exp(jnp.minimum(gs - g[:lo, :], 0.0) + log_k[:lo, :]).astype(BF16)
                kd = jnp.concatenate([kd, jnp.zeros((c_len - lo, HGRN_D), BF16)], axis=0)
                a_off = lax.dot_general(qd, kd, nt, preferred_element_type=F32)
                blk = [a_off[:half, :], a_off[half:, :]]
            for s in range(sub):
                crow = ck[lo + s:lo + s + 1, :]
                for hh in range(s // half, 2):
                    col = jnp.sum(qa[hh] * jnp.exp(ga[hh] + crow), axis=-1, keepdims=True)
                    blk[hh] = jnp.where(lane8 == lo + s, col, blk[hh])
            for hh in range(2):
                a_rows.append(jnp.where(lane8 <= lo + hh * half + row8, blk[hh], 0.0))
        a_full = jnp.concatenate(a_rows, axis=0).astype(BF16)
        o = o_inter + jnp.dot(a_full, v, preferred_element_type=F32)
        g_last = g[c_len - 1:c_len, :]
        kd = jnp.exp(g_last - g + log_k).astype(BF16)
        st = st * jnp.exp(g_last) + lax.dot_general(v, kd, (((0,), (0,)), ((), ())),
                                                    preferred_element_type=F32)
        ms = jnp.mean(o * o, axis=-1, keepdims=True)
        y = o * lax.rsqrt(ms + EPS) * og * (gate * jax.nn.sigmoid(gate))
        o_ref[rows, :] = y.astype(BF16)
        return st

    lax.fori_loop(0, n_chunk, chunk, jnp.zeros((HGRN_D, HGRN_D), F32), unroll=3)


def _hgrn(proj3, log_lb, log1m_lb, out_gain):
    bsz, tp, _ = proj3.shape
    seq = lambda blk: pl.BlockSpec((None, tp, 128), lambda b, h: (b, 0, blk + h))
    chan = pl.BlockSpec((None, 1, HGRN_D), lambda b, h: (h, 0, 0))
    return pl.pallas_call(
        _hgrn_kernel,
        out_shape=jax.ShapeDtypeStruct((bsz, tp, HGRN_HEADS * HGRN_D), BF16),
        grid=(bsz, HGRN_HEADS),
        in_specs=[seq(BLK_RQ), seq(BLK_RF), seq(BLK_RI), seq(BLK_RG), chan, chan,
                  pl.BlockSpec((1, HGRN_D), lambda b, h: (0, 0))],
        out_specs=pl.BlockSpec((None, tp, 128), lambda b, h: (b, 0, h)),
        compiler_params=_params(("parallel", "parallel")),
        name="hgrn2",
    )(proj3, proj3, proj3, proj3,
      log_lb.reshape(HGRN_HEADS, 1, HGRN_D), log1m_lb.reshape(HGRN_HEADS, 1, HGRN_D),
      out_gain.reshape(1, HGRN_D))


def _merge_kernel(hs_ref, ua_ref, ur_ref, cb_ref, cc_ref, ch_ref, pc_ref, ph_ref,
                  g0_ref, g1_ref, g2_ref, cw_ref, wb_ref, wo_ref, o_ref, *, tm, tiles_per_seq):
    i = pl.program_id(0)
    row = (i % tiles_per_seq) * tm + lax.broadcasted_iota(jnp.int32, (tm, 1), 0)
    valid = row >= PAD0
    z = jnp.where(valid, cc_ref[...].astype(F32) * ch_ref[...].astype(F32), 0.0)
    halo_row = (i % tiles_per_seq) * tm - 8 + lax.broadcasted_iota(jnp.int32, (8, 1), 0)
    zp = jnp.where(halo_row >= PAD0, pc_ref[...].astype(F32) * ph_ref[...].astype(F32), 0.0)
    zz = jnp.concatenate([zp, z], axis=0)
    cw = cw_ref[...]
    y = (cw[2:3, :] * z + cw[1:2, :] * zz[7:7 + tm, :] + cw[0:1, :] * zz[6:6 + tm, :])
    u_conv = jnp.where(valid, cb_ref[...].astype(F32) * y, 0.0).astype(BF16)
    mixed = jnp.zeros((tm, D_MODEL), F32)
    for n, (u, g_ref) in enumerate(((ua_ref[...], g0_ref), (u_conv, g1_ref), (ur_ref[...], g2_ref))):
        up = jnp.dot(u, wb_ref[n], preferred_element_type=F32)
        mixed = mixed + jax.nn.sigmoid(g_ref[...].astype(F32)) * up
    o_ref[...] = hs_ref[...] + jnp.dot(mixed.astype(BF16), wo_ref[...], preferred_element_type=F32)


def _merge(hs, u_att, u_hgrn, proj, conv_w, wb_bf, wo_bf, tp):
    rows = hs.shape[0]
    tm = _row_tile(tp, 384)
    kern = functools.partial(_merge_kernel, tm=tm, tiles_per_seq=tp // tm)
    row_blk = lambda w, blk: pl.BlockSpec((tm, w), lambda i: (i, blk))
    halo = lambda blk: pl.BlockSpec((8, 512), lambda i: (jnp.maximum(i * (tm // 8) - 1, 0), blk))
    const = lambda shape: pl.BlockSpec(shape, lambda i: (0,) * len(shape))
    return pl.pallas_call(
        kern,
        out_shape=jax.ShapeDtypeStruct((rows, D_MODEL), F32),
        grid=(rows // tm,),
        in_specs=[row_blk(D_MODEL, 0), row_blk(512, 0), row_blk(512, 0),
                  row_blk(512, BLK_CB), row_blk(512, BLK_CC), row_blk(512, BLK_CH),
                  halo(BLK_CC), halo(BLK_CH),
                  row_blk(1024, BLK_GATE), row_blk(1024, BLK_GATE + 1), row_blk(1024, BLK_GATE + 2),
                  const((CONV_K, 512)), const((3, 512, D_MODEL)), const((D_MODEL, D_MODEL))],
        out_specs=row_blk(D_MODEL, 0),
        compiler_params=_params(("parallel",)),
        name="merge",
    )(hs, u_att, u_hgrn, proj, proj, proj, proj, proj, proj, proj, proj, conv_w, wb_bf, wo_bf)


def _ffn_kernel(hs_ref, g_ref, wg_ref, wu_ref, wd_ref, o_ref):
    x = hs_ref[...]
    ms = jnp.mean(x * x, axis=-1, keepdims=True)
    hn = (x * lax.rsqrt(ms + EPS) * g_ref[...]).astype(BF16)
    a = jnp.dot(hn, wg_ref[...], preferred_element_type=F32)
    u = jnp.dot(hn, wu_ref[...], preferred_element_type=F32)
    h = (a * jax.nn.sigmoid(a) * u).astype(BF16)
    o_ref[...] = x + jnp.dot(h, wd_ref[...], preferred_element_type=F32)


def _ffn(hs, gain, wg_bf, wu_bf, wd_bf):
    rows = hs.shape[0]
    d_ff = wg_bf.shape[1]
    tm = _row_tile(rows, 768)
    const = lambda shape: pl.BlockSpec(shape, lambda i: (0, 0), pipeline_mode=pl.Buffered(1))
    return pl.pallas_call(
        _ffn_kernel,
        out_shape=jax.ShapeDtypeStruct((rows, D_MODEL), F32),
        grid=(rows // tm,),
        in_specs=[pl.BlockSpec((tm, D_MODEL), lambda i: (i, 0)), const((1, D_MODEL)),
                  const((D_MODEL, d_ff)), const((D_MODEL, d_ff)), const((d_ff, D_MODEL))],
        out_specs=pl.BlockSpec((tm, D_MODEL), lambda i: (i, 0)),
        compiler_params=_params(("parallel",)),
        name="ffn_dense",
    )(hs, gain.reshape(1, D_MODEL), wg_bf, wu_bf, wd_bf)


def _router_kernel(hs_ref, g_ref, rw_ref, idx_ref, wt_ref, cnt_ref, carry_ref, *, tm, tiles_per_seq):
    i = pl.program_id(0)

    @pl.when(i == 0)
    def _():
        carry_ref[...] = jnp.zeros_like(carry_ref)

    x = hs_ref[...]
    ms = jnp.mean(x * x, axis=-1, keepdims=True)
    hn = x * lax.rsqrt(ms + EPS) * g_ref[...]
    logits = jnp.dot(hn, rw_ref[...], preferred_element_type=F32, precision=lax.Precision.HIGHEST)
    lane = lax.broadcasted_iota(jnp.int32, logits.shape, 1)
    lane_f = lane.astype(F32)
    logits = jnp.where(lane < N_EXPERTS, logits, -jnp.inf)
    m1 = jnp.max(logits, axis=-1, keepdims=True)
    i1 = jnp.min(jnp.where(logits == m1, lane_f, 128.0), axis=-1, keepdims=True)
    rest = jnp.where(lane_f == i1, -jnp.inf, logits)
    m2 = jnp.max(rest, axis=-1, keepdims=True)
    i2 = jnp.min(jnp.where(rest == m2, lane_f, 128.0), axis=-1, keepdims=True)
    e2 = jnp.exp(m2 - m1)
    w1 = 1.0 / (1.0 + e2)
    w2 = e2 / (1.0 + e2)
    row = (i % tiles_per_seq) * tm + lax.broadcasted_iota(jnp.int32, (tm, 1), 0)
    real = row >= T_PAD
    hot1 = jnp.where(jnp.logical_and(real, lane_f == i1), 1.0, 0.0)
    hot2 = jnp.where(jnp.logical_and(real, lane_f == i2), 1.0, 0.0)
    both = hot1 + hot2
    rr = lax.broadcasted_iota(jnp.int32, (tm, tm), 0)
    cc = lax.broadcasted_iota(jnp.int32, (tm, tm), 1)
    earlier = jnp.where(cc < rr, 1.0, 0.0).astype(BF16)
    before = carry_ref[...] + jnp.dot(earlier, both.astype(BF16), preferred_element_type=F32)
    r1 = jnp.sum(before * hot1, axis=-1, keepdims=True)
    r2 = jnp.sum(before * hot2, axis=-1, keepdims=True)
    carry_ref[...] += jnp.sum(both, axis=0, keepdims=True)
    cnt_ref[...] = carry_ref[...]
    packed = jnp.where(lane == 0, i1, jnp.where(lane == 1, i2, jnp.where(lane == 2, r1, jnp.where(lane == 3, r2, 0.0))))
    idx_ref[...] = packed.astype(jnp.int32)
    wt_ref[...] = jnp.where(lane == 0, w1, jnp.where(lane == 1, w2, 0.0))


def _router(hs, gain, router_w, tp):
    rows = hs.shape[0]
    tm = _row_tile(tp, 384)
    rw = jnp.zeros((D_MODEL, 128), F32).at[:, :N_EXPERTS].set(router_w.astype(F32))
    blk = pl.BlockSpec((tm, 128), lambda i: (i, 0))
    return pl.pallas_call(
        functools.partial(_router_kernel, tm=tm, tiles_per_seq=tp // tm),
        out_shape=(jax.ShapeDtypeStruct((rows, 128), jnp.int32), jax.ShapeDtypeStruct((rows, 128), F32),
                   jax.ShapeDtypeStruct((1, 128), F32)),
        grid=(rows // tm,),
        in_specs=[pl.BlockSpec((tm, D_MODEL), lambda i: (i, 0)),
                  pl.BlockSpec((1, D_MODEL), lambda i: (0, 0)),
                  pl.BlockSpec((D_MODEL, 128), lambda i: (0, 0))],
        out_specs=(blk, blk, pl.BlockSpec((1, 128), lambda i: (0, 0))),
        scratch_shapes=[pltpu.VMEM((1, 128), F32)],
        compiler_params=_params(("arbitrary",)),
        name="moe_router",
    )(hs, gain.reshape(1, D_MODEL), rw)


def _dispatch_kernel(s1_ref, s2_ref, zr_ref, hs_ref, wt_ref, g_ref, xs_ref, rows_ref, zero_ref, sem, zsem,
                     *, td, tg, tiles_per_seq, n_steps):
    n = pl.program_id(0) * tiles_per_seq + pl.program_id(1)
    buf = n % 2
    base = n * td

    def row_copy(r, which):
        slot = (s1_ref, s2_ref)[which][base + r]
        return pltpu.make_async_copy(rows_ref.at[buf, which, pl.ds(r, 1)], xs_ref.at[pl.ds(slot, 1)],
                                     sem.at[buf])

    def wait_rows(b):
        def body(r, c):
            pltpu.make_async_copy(rows_ref.at[b, 0, pl.ds(0, 1)], xs_ref.at[pl.ds(0, 1)], sem.at[b]).wait()
            return c
        lax.fori_loop(0, 2 * td, body, 0)

    @pl.when(n == 0)
    def _():
        zero_ref[...] = jnp.zeros_like(zero_ref)
        for e in range(2 * N_EXPERTS):
            fill = pltpu.make_async_copy(zero_ref, xs_ref.at[pl.ds(pl.multiple_of(zr_ref[e], tg), tg)], zsem)
            fill.start()
            fill.wait()

    @pl.when(n >= 2)
    def _():
        wait_rows(buf)

    x = hs_ref[...]
    ms = jnp.mean(x * x, axis=-1, keepdims=True)
    hn = x * lax.rsqrt(ms + EPS) * g_ref[...]
    wt = wt_ref[...]
    for which in range(2):
        rows_ref[buf, which, :, :D_MODEL] = hn
        rows_ref[buf, which, :, D_MODEL:] = jnp.broadcast_to(wt[:, which:which + 1], (td, XS_EXTRA))

    def issue(r, c):
        row_copy(r, 0).start()
        row_copy(r, 1).start()
        return c

    lax.fori_loop(0, td, issue, 0)

    @pl.when(n == n_steps - 1)
    def _():
        wait_rows(buf)
        if n_steps > 1:
            wait_rows(1 - buf)


def _dispatch(hs3, wts3, gain, slot1, slot2, zero_rows, n_slots, tg):
    bsz, tp, _ = hs3.shape
    td = T_PAD
    tiles_per_seq = (tp - T_PAD) // td
    width = D_MODEL + XS_EXTRA
    kern = functools.partial(_dispatch_kernel, td=td, tg=tg, tiles_per_seq=tiles_per_seq,
                             n_steps=bsz * tiles_per_seq)
    return pl.pallas_call(
        kern,
        out_shape=jax.ShapeDtypeStruct((n_slots, width), F32),
        grid_spec=pltpu.PrefetchScalarGridSpec(
            num_scalar_prefetch=3,
            grid=(bsz, tiles_per_seq),
            in_specs=[pl.BlockSpec((None, td, D_MODEL), lambda b, i, *_: (b, i + 1, 0)),
                      pl.BlockSpec((None, td, 128), lambda b, i, *_: (b, i + 1, 0)),
                      pl.BlockSpec((1, D_MODEL), lambda b, i, *_: (0, 0))],
            out_specs=pl.BlockSpec(memory_space=pl.ANY),
            scratch_shapes=[pltpu.VMEM((2, 2, td, width), F32), pltpu.VMEM((tg, width), F32),
                            pltpu.SemaphoreType.DMA((2,)), pltpu.SemaphoreType.DMA],
        ),
        compiler_params=_params(("arbitrary", "arbitrary")),
        name="moe_dispatch",
    )(slot1, slot2, zero_rows, hs3, wts3, gain.reshape(1, D_MODEL))


def _expert_kernel(te_ref, nt_ref, x_ref, wg_ref, wu_ref, wd_ref, o_ref, hn_ref, acc_ref, *, n_f):
    t = pl.program_id(0)
    f = pl.program_id(1)

    @pl.when(t < nt_ref[0])
    def _():
        @pl.when(f == 0)
        def _():
            hn_ref[...] = x_ref[:, :D_MODEL].astype(BF16)

        hn = hn_ref[...]
        a = jnp.dot(hn, wg_ref[...], preferred_element_type=F32)
        u = jnp.dot(hn, wu_ref[...], preferred_element_type=F32)
        h = (a * jax.nn.sigmoid(a) * u).astype(BF16)

        @pl.when(f == 0)
        def _():
            acc_ref[...] = jnp.dot(h, wd_ref[...], preferred_element_type=F32)

        @pl.when(jnp.logical_and(f > 0, f < n_f - 1))
        def _():
            acc_ref[...] += jnp.dot(h, wd_ref[...], preferred_element_type=F32)

        @pl.when(f == n_f - 1)
        def _():
            y = acc_ref[...] + jnp.dot(h, wd_ref[...], preferred_element_type=F32)
            o_ref[...] = (y * x_ref[:, D_MODEL:D_MODEL + 1]).astype(BF16)

    @pl.when(jnp.logical_and(t >= nt_ref[0], f == n_f - 1))
    def _():
        o_ref[...] = jnp.zeros_like(o_ref)


def _experts(xs, tile_expert, n_tiles_used, wg_bf, wu_bf, wd_bf, tg):
    slots = xs.shape[0]
    n_f = MOE_F_BLOCKS
    tf = wg_bf.shape[2] // n_f

    def x_map(t, f, te, nt):
        return (jnp.maximum(jnp.minimum(t, nt[0] - 1), 0), 0)

    def f_of(t, f, nt):
        return jnp.where(t < nt[0], f, n_f - 1)

    return pl.pallas_call(
        functools.partial(_expert_kernel, n_f=n_f),
        out_shape=jax.ShapeDtypeStruct((slots, D_MODEL), BF16),
        grid_spec=pltpu.PrefetchScalarGridSpec(
            num_scalar_prefetch=2,
            grid=(slots // tg, n_f),
            in_specs=[
                pl.BlockSpec((tg, D_MODEL + XS_EXTRA), x_map),
                pl.BlockSpec((None, D_MODEL, tf), lambda t, f, te, nt: (te[t], 0, f_of(t, f, nt))),
                pl.BlockSpec((None, D_MODEL, tf), lambda t, f, te, nt: (te[t], 0, f_of(t, f, nt))),
                pl.BlockSpec((None, tf, D_MODEL), lambda t, f, te, nt: (te[t], f_of(t, f, nt), 0)),
            ],
            out_specs=pl.BlockSpec((tg, D_MODEL), lambda t, f, te, nt: (t, 0)),
            scratch_shapes=[pltpu.VMEM((tg, D_MODEL), BF16), pltpu.VMEM((tg, D_MODEL), F32)],
        ),
        compiler_params=_params(("arbitrary", "arbitrary")),
        name="moe_experts",
    )(tile_expert, n_tiles_used, xs, wg_bf, wu_bf, wd_bf)


def _combine_kernel(ws_ref, hs_ref, route_ref, y_ref, o_ref, win_ref, sem, *, tc, tiles_per_seq, n_steps):
    n = pl.program_id(0) * tiles_per_seq + pl.program_id(1)
    cur = n % 2

    def window_copy(step, b, e):
        start = pl.multiple_of(ws_ref[step * N_EXPERTS + e], WIN_ALIGN)
        return pltpu.make_async_copy(y_ref.at[pl.ds(start, COMBINE_WIN)],
                                     win_ref.at[b, pl.ds(e * COMBINE_WIN, COMBINE_WIN)], sem.at[b])

    def fetch(step, b):
        for e in range(N_EXPERTS):
            window_copy(step, b, e).start()

    @pl.when(n == 0)
    def _():
        fetch(0, 0)

    @pl.when(n + 1 < n_steps)
    def _():
        fetch(n + 1, 1 - cur)

    for e in range(N_EXPERTS):
        window_copy(n, cur, e).wait()

    route = route_ref[...]
    e1, e2, s1, s2 = (route[:, c:c + 1] for c in range(4))
    pos = lax.broadcasted_iota(jnp.int32, (1, COMBINE_WIN), 1)
    picks = []
    for e in range(N_EXPERTS):
        start = ws_ref[n * N_EXPERTS + e]
        col = jnp.where(e1 == e, s1 - start, jnp.where(e2 == e, s2 - start, -1))
        picks.append(jnp.where(col == pos, 1.0, 0.0).astype(BF16))
    pick = jnp.concatenate(picks, axis=1)
    o_ref[...] = hs_ref[...] + jnp.dot(pick, win_ref[cur], preferred_element_type=F32)


def _combine(hs3, route, y, win_start, seq):
    bsz = hs3.shape[0]
    tc = COMBINE_TOKENS
    tiles_per_seq = seq // tc
    kern = functools.partial(_combine_kernel, tc=tc, tiles_per_seq=tiles_per_seq, n_steps=bsz * tiles_per_seq)
    tok = lambda w: pl.BlockSpec((tc, w), lambda b, i, ws: (b * tiles_per_seq + i, 0))
    return pl.pallas_call(
        kern,
        out_shape=jax.ShapeDtypeStruct((bsz, seq, D_MODEL), F32),
        grid_spec=pltpu.PrefetchScalarGridSpec(
            num_scalar_prefetch=1,
            grid=(bsz, tiles_per_seq),
            in_specs=[
                pl.BlockSpec((None, tc, D_MODEL), lambda b, i, ws: (b, i + T_PAD // tc, 0)),
                tok(4),
                pl.BlockSpec(memory_space=pl.ANY),
            ],
            out_specs=pl.BlockSpec((None, tc, D_MODEL), lambda b, i, ws: (b, i, 0)),
            scratch_shapes=[pltpu.VMEM((2, N_EXPERTS * COMBINE_WIN, D_MODEL), BF16),
                            pltpu.SemaphoreType.DMA((2,))],
        ),
        compiler_params=_params(("arbitrary", "arbitrary")),
        name="moe_combine",
    )(win_start, hs3, route, y)


def _moe(hs, gain, router_w, wg_bf, wu_bf, wd_bf, bsz, tp):
    seq = tp - T_PAD
    n_tok = bsz * seq
    tg = MOE_TILE
    idx, wts, cnt = _router(hs, gain, router_w, tp)
    sel = idx.reshape(bsz, tp, 128)[:, T_PAD:, :4].reshape(n_tok, 4)
    counts = cnt[0, :N_EXPERTS].astype(jnp.int32)
    padded = ((counts + tg - 1) // tg) * tg
    ends = jnp.cumsum(padded)
    starts = ends - padded
    experts = jnp.arange(N_EXPERTS, dtype=jnp.int32)[None, :]
    slot1 = jnp.sum(jnp.where(sel[:, 0:1] == experts, starts[None, :], 0), axis=1) + sel[:, 2]
    slot2 = jnp.sum(jnp.where(sel[:, 1:2] == experts, starts[None, :], 0), axis=1) + sel[:, 3]
    slot1, slot2 = slot1.astype(jnp.int32), slot2.astype(jnp.int32)
    n_slots = 2 * n_tok + N_EXPERTS * tg
    tile_start = jnp.arange(n_slots // tg, dtype=jnp.int32)[:, None] * tg
    tile_expert = jnp.minimum(jnp.sum((tile_start >= ends[None, :]).astype(jnp.int32), axis=1), N_EXPERTS - 1)
    n_tiles_used = (ends[-1:] // tg).astype(jnp.int32)
    tail = n_slots - tg * (1 + jnp.arange(N_EXPERTS, dtype=jnp.int32))
    zero_rows = jnp.concatenate([jnp.maximum(ends - tg, 0), tail]).astype(jnp.int32)

    tc = COMBINE_TOKENS
    first = jnp.minimum(
        jnp.min(jnp.where(sel[:, 0:1] == experts, slot1[:, None], n_slots).reshape(n_tok // tc, tc, N_EXPERTS), axis=1),
        jnp.min(jnp.where(sel[:, 1:2] == experts, slot2[:, None], n_slots).reshape(n_tok // tc, tc, N_EXPERTS), axis=1))
    win_start = jnp.where(first == n_slots, 0,
                          jnp.minimum(first // WIN_ALIGN * WIN_ALIGN, n_slots - COMBINE_WIN))
    route = jnp.concatenate([sel[:, :2], slot1[:, None], slot2[:, None]], axis=1)

    hs3 = hs.reshape(bsz, tp, D_MODEL)
    xs = _dispatch(hs3, wts.reshape(bsz, tp, 128), gain, slot1, slot2, zero_rows, n_slots, tg)
    y = _experts(xs, tile_expert.astype(jnp.int32), n_tiles_used, wg_bf, wu_bf, wd_bf, tg)
    return _combine(hs3, route, y, win_start.reshape(-1).astype(jnp.int32), seq)


def _permute_qk_cols(w):
    return w.reshape(D_MODEL, 2, ATT_HEADS, ATT_QK_DIM).transpose(0, 2, 1, 3).reshape(D_MODEL, 512)


def kernel(x, meta_tokens, norm1_gain, norm2_gain, w_in, q_norm_gain, k_norm_gain, diff_lambda,
           attn_sub_gain, rel_bias, conv_w, hgrn_lb_logits, hgrn_out_gain, w_branch, w_out,
           ffn_w_gate, ffn_w_up, ffn_w_down, router_w, moe_w_gate, moe_w_up, moe_w_down):
    bsz, seq, _ = x.shape
    depth = w_in.shape[0]
    tp = T_PAD + seq
    assert tp % ATT_TILE == 0 and depth == 2

    head = jnp.concatenate([jnp.zeros((PAD0, D_MODEL), x.dtype), meta_tokens.astype(x.dtype)], axis=0)
    hs = jnp.concatenate([jnp.broadcast_to(head[None], (bsz, T_PAD, D_MODEL)), x], axis=1)
    hs = hs.reshape(bsz * tp, D_MODEL)

    lb_all = jnp.cumsum(jax.nn.softmax(hgrn_lb_logits.astype(F32), axis=0), axis=0)
    lb_all = lb_all - lb_all[0]
    btab = _attn_bias_tables(rel_bias, ATT_TILE)

    out = None
    for layer in range(depth):
        w = w_in[layer]
        w_bf = jnp.concatenate([_permute_qk_cols(w[:, :512]), _permute_qk_cols(w[:, 512:1024]), w[:, 1024:]],
                               axis=1).astype(BF16)
        qk_gain = jnp.concatenate([jnp.tile(q_norm_gain[layer].astype(F32), 8) * (ATT_QK_DIM ** -0.5 * LOG2E),
                                   jnp.tile(k_norm_gain[layer].astype(F32), 8)]).reshape(1, COL_TILE)
        proj = _inproj(hs, norm1_gain[layer], w_bf, qk_gain)
        proj3 = proj.reshape(bsz, tp, IN_COLS)

        lam_init = 0.8 - 0.6 * math.exp(-0.3 * layer)
        lp = diff_lambda[layer].astype(F32)
        lam = jnp.exp(jnp.sum(lp[0] * lp[1])) - jnp.exp(jnp.sum(lp[2] * lp[3])) + lam_init
        u_att = _diff_attention(proj3, lam.reshape(1), attn_sub_gain[layer].astype(F32), btab,
                                1.0 - lam_init)

        lb = lb_all[layer]
        u_hgrn = _hgrn(proj3, jnp.log(lb), jnp.log1p(-lb), hgrn_out_gain[layer].astype(F32))

        hs = _merge(hs, u_att.reshape(bsz * tp, 512), u_hgrn.reshape(bsz * tp, 512), proj,
                    conv_w[layer].astype(F32), w_branch[layer].astype(BF16), w_out[layer].astype(BF16), tp)

        j = layer // 2
        if layer % 2 == 0:
            hs = _ffn(hs, norm2_gain[layer], ffn_w_gate[j].astype(BF16), ffn_w_up[j].astype(BF16),
                      ffn_w_down[j].astype(BF16))
        else:
            out = _moe(hs, norm2_gain[layer], router_w[j], moe_w_gate[j].astype(BF16),
                       moe_w_up[j].astype(BF16), moe_w_down[j].astype(BF16), bsz, tp)
    return out
```

```python
import functools
import math

import numpy as np
import jax
import jax.numpy as jnp
from jax import lax
from jax.experimental import pallas as pl
from jax.experimental.pallas import tpu as pltpu

F32 = jnp.float32
BF16 = jnp.bfloat16

D_MODEL = 1024
N_META = 16
EPS = 1e-6
ATT_HEADS = 4
ATT_QK_DIM = 64
ATT_V_DIM = 128
REL_BUCKETS = 32
REL_MAX_DIST = 128
CONV_K = 3
HGRN_HEADS = 4
HGRN_D = 128
N_EXPERTS = 8
IN_COLS = 8192

T_PAD = 128
PAD0 = T_PAD - N_META
ATT_TILE = 384
VT_ONES = 16
LOG2E = math.log2(math.e)
HGRN_CHUNK = 128
HGRN_SUB = 16
COL_TILE = 1024
GROUP_MEAN_WIDTH = 256
MOE_TILE = 512
MOE_F_BLOCKS = 2
XS_EXTRA = 128
COMBINE_TOKENS = 128
WIN_ALIGN = 16
COMBINE_WIN = COMBINE_TOKENS + WIN_ALIGN
MASK = -1e30
VMEM_LIMIT = 56 * 1024 * 1024

BLK_Q, BLK_K, BLK_V = 0, 4, 8
BLK_RQ, BLK_RF, BLK_RI, BLK_RG = 24, 28, 32, 36
BLK_CB, BLK_CC, BLK_CH = 3, 4, 5
BLK_GATE = 5


def _row_tile(rows, target):
    n = rows // 128
    best = 1
    for d in range(1, n + 1):
        if n % d == 0 and d * 128 <= target:
            best = d
    return best * 128


def _params(sem, vmem=VMEM_LIMIT):
    return pltpu.CompilerParams(dimension_semantics=sem, vmem_limit_bytes=vmem)


def _inproj_kernel(x_ref, g_ref, w_ref, qkg_ref, gm_ref, o_ref, xn_ref):
    j = pl.program_id(1)

    @pl.when(j == 0)
    def _():
        x = x_ref[...]
        ms = jnp.mean(x * x, axis=-1, keepdims=True)
        xn_ref[...] = (x * lax.rsqrt(ms + EPS) * g_ref[...]).astype(BF16)

    acc = jnp.dot(xn_ref[...], w_ref[...], preferred_element_type=F32)

    @pl.when(j == 0)
    def _():
        sq = acc * acc
        hi = sq.astype(BF16)
        lo = (sq - hi.astype(F32)).astype(BF16)
        gm = gm_ref[...]
        width = gm.shape[0]
        ms = jnp.concatenate(
            [jnp.dot(hi[:, c:c + width], gm, preferred_element_type=F32)
             + jnp.dot(lo[:, c:c + width], gm, preferred_element_type=F32)
             for c in range(0, COL_TILE, width)], axis=1)
        o_ref[...] = (acc * lax.rsqrt(ms + EPS) * qkg_ref[...]).astype(BF16)

    @pl.when(j > 0)
    def _():
        o_ref[...] = acc.astype(BF16)


def _inproj(hs, gain, w_bf, qk_gain):
    rows = hs.shape[0]
    tm = _row_tile(rows, 1536)
    n_col = IN_COLS // COL_TILE
    assert COL_TILE == 4 * ATT_HEADS * ATT_QK_DIM
    grp = np.arange(GROUP_MEAN_WIDTH) // ATT_QK_DIM
    gm = jnp.asarray((grp[:, None] == grp[None, :]).astype(np.float32) / ATT_QK_DIM, BF16)
    return pl.pallas_call(
        _inproj_kernel,
        out_shape=jax.ShapeDtypeStruct((rows, IN_COLS), BF16),
        grid=(rows // tm, n_col),
        in_specs=[
            pl.BlockSpec((tm, D_MODEL), lambda i, j: (i, 0)),
            pl.BlockSpec((1, D_MODEL), lambda i, j: (0, 0)),
            pl.BlockSpec((D_MODEL, COL_TILE), lambda i, j: (0, j)),
            pl.BlockSpec((1, COL_TILE), lambda i, j: (0, 0)),
            pl.BlockSpec((GROUP_MEAN_WIDTH, GROUP_MEAN_WIDTH), lambda i, j: (0, 0)),
        ],
        out_specs=pl.BlockSpec((tm, COL_TILE), lambda i, j: (i, j)),
        scratch_shapes=[pltpu.VMEM((tm, D_MODEL), BF16)],
        compiler_params=_params(("parallel", "arbitrary")),
        name="inproj",
    )(hs, gain.reshape(1, D_MODEL), w_bf, qk_gain, gm)


def _rel_bucket_table(n_max):
    n = np.arange(n_max, dtype=np.int64)
    max_exact = REL_BUCKETS // 2
    nf = np.maximum(n, 1).astype(np.float32)
    large = max_exact + (np.log(nf / np.float32(max_exact)) / np.float32(math.log(REL_MAX_DIST / max_exact))
                         * np.float32(REL_BUCKETS - max_exact)).astype(np.int32)
    large = np.minimum(large, REL_BUCKETS - 1)
    return np.where(n < max_exact, n, large).astype(np.int32)


def _attn_bias_tables(rel_bias, t):
    bucket = _rel_bucket_table(2 * t)
    assert np.all(bucket[t + 1:] == REL_BUCKETS - 1) and np.all(np.diff(bucket) >= 0)
    first_dist = tuple(int(np.searchsorted(bucket, b, side="left")) for b in range(REL_BUCKETS))
    return pl.pallas_call(
        functools.partial(_bias_kernel, t=t, first_dist=first_dist),
        out_shape=jax.ShapeDtypeStruct((ATT_HEADS, 6, t, t), F32),
        grid=(ATT_HEADS,),
        in_specs=[pl.BlockSpec(memory_space=pltpu.SMEM)],
        out_specs=pl.BlockSpec((None, 6, t, t), lambda h: (h, 0, 0, 0)),
        compiler_params=_params(("parallel",)),
        name="attn_bias",
    )(rel_bias.astype(F32))


def _bias_kernel(rb_ref, o_ref, *, t, first_dist):
    h = pl.program_id(0)
    key = lax.broadcasted_iota(jnp.int32, (t, t), 0)
    qry = lax.broadcasted_iota(jnp.int32, (t, t), 1)
    far = rb_ref[REL_BUCKETS - 1, h]

    def table(n):
        val = jnp.full((t, t), rb_ref[0, h] - far, F32)
        for b in range(1, REL_BUCKETS):
            val = jnp.where(n >= first_dist[b], rb_ref[b, h] - far, val)
        return val * LOG2E

    n0 = qry - key
    diag = jnp.where(n0 >= 0, table(n0), MASK)
    near = table(n0 + t)
    zero = jnp.zeros((t, t), F32)
    for kind, tab in enumerate((diag, near, zero)):
        o_ref[kind] = tab
        o_ref[kind + 3] = jnp.where(key < PAD0, MASK, tab)


def _attn_kernel(lam_ref, q_ref, k_ref, v_ref, bt_ref, sg_ref, o_ref, m_ref, acc_ref, vt_ref,
                 *, t, out_scale):
    n_t = vt_ref.shape[0]
    for j in range(n_t):
        vt_ref[j, :ATT_V_DIM, :] = v_ref[j * t:(j + 1) * t, :].astype(F32).T.astype(BF16)
        vt_ref[j, ATT_V_DIM:, :] = jnp.ones((VT_ONES, t), BF16)
    lane = lax.broadcasted_iota(jnp.int32, (1, 2 * ATT_QK_DIM), 1)
    nt = (((1,), (1,)), ((), ()))

    def tile_rows(i):
        return pl.ds(pl.multiple_of(i * t, t), t)

    def scores(i, j):
        q = q_ref[tile_rows(i), :]
        zero = jnp.zeros_like(q)
        q_cat = jnp.concatenate([jnp.where(lane < ATT_QK_DIM, q, zero), jnp.where(lane >= ATT_QK_DIM, q, zero)],
                                axis=0)
        bias = bt_ref[jnp.minimum(i - j, 2) + jnp.where(j == 0, 3, 0)]
        s = (lax.dot_general(k_ref[tile_rows(j), :], q_cat, nt, preferred_element_type=F32)
             + jnp.concatenate([bias, bias], axis=1))
        return s, jnp.max(s, axis=0, keepdims=True)

    def consume(j, scored):
        s, s_max = scored
        m_old = m_ref[...]
        m_new = jnp.maximum(m_old, s_max)
        m_ref[...] = m_new
        p = jnp.exp2(s - m_new).astype(BF16)
        acc_ref[...] = (jnp.exp2(m_old - m_new) * acc_ref[...]
                        + jnp.dot(vt_ref[j], p, preferred_element_type=F32))

    def reset():
        m_ref[...] = jnp.full(m_ref.shape, MASK, F32)
        acc_ref[...] = jnp.zeros(acc_ref.shape, F32)

    def finish(i):
        acc = acc_ref[...]
        a1, a2 = acc[:, :t], acc[:, t:]
        o_t = (a1[:ATT_V_DIM] / a1[ATT_V_DIM:ATT_V_DIM + 1]
               - lam_ref[0] * (a2[:ATT_V_DIM] / a2[ATT_V_DIM:ATT_V_DIM + 1]))
        o = o_t.T
        ms = jnp.mean(o * o, axis=-1, keepdims=True)
        y = o * lax.rsqrt(ms + EPS) * (sg_ref[...] * out_scale)
        row = i * t + lax.broadcasted_iota(jnp.int32, (t, 1), 0)
        o_ref[tile_rows(i), :] = jnp.where(row >= PAD0, y, 0.0).astype(BF16)
        reset()

    def step(_, carry):
        i, j, s = carry
        last = j == i
        ni = jnp.where(last, i + 1, i)
        nj = jnp.where(last, 0, j + 1)
        nxt = scores(jnp.minimum(ni, n_t - 1), nj)
        consume(j, s)

        @pl.when(last)
        def _():
            finish(i)

        return ni, nj, nxt

    reset()
    first = jnp.int32(0)
    lax.fori_loop(0, n_t * (n_t + 1) // 2, step, (first, first, scores(first, first)))


def _diff_attention(proj3, lam, sub_gain, btab, out_scale):
    bsz, tp, _ = proj3.shape
    t = ATT_TILE
    n_t = tp // t
    rows = ATT_V_DIM + VT_ONES
    kern = functools.partial(_attn_kernel, t=t, out_scale=out_scale)
    seq = lambda blk: pl.BlockSpec((None, tp, 128), lambda b, h: (b, 0, blk + h))
    return pl.pallas_call(
        kern,
        out_shape=jax.ShapeDtypeStruct((bsz, tp, ATT_HEADS * ATT_V_DIM), BF16),
        grid=(bsz, ATT_HEADS),
        in_specs=[
            pl.BlockSpec(memory_space=pltpu.SMEM),
            seq(BLK_Q), seq(BLK_K), seq(BLK_V),
            pl.BlockSpec((None, 6, t, t), lambda b, h: (h, 0, 0, 0)),
            pl.BlockSpec((1, ATT_V_DIM), lambda b, h: (0, 0)),
        ],
        out_specs=pl.BlockSpec((None, tp, 128), lambda b, h: (b, 0, h)),
        scratch_shapes=[pltpu.VMEM((1, 2 * t), F32), pltpu.VMEM((rows, 2 * t), F32),
                        pltpu.VMEM((n_t, rows, t), BF16)],
        compiler_params=_params(("parallel", "parallel")),
        name="diff_attn",
    )(lam, proj3, proj3, proj3, btab, sub_gain.reshape(1, ATT_V_DIM))


def _split3(x):
    h1 = x.astype(BF16)
    r1 = x - h1.astype(F32)
    h2 = r1.astype(BF16)
    h3 = (r1 - h2.astype(F32)).astype(BF16)
    return h1, h2, h3


def _hgrn_kernel(q_ref, f_ref, i_ref, g_ref, la_ref, l1m_ref, og_ref, o_ref):
    c_len, sub = HGRN_CHUNK, HGRN_SUB
    half = sub // 2
    n_chunk = q_ref.shape[0] // c_len
    la, l1m, og = la_ref[...], l1m_ref[...], og_ref[...]
    rr = lax.broadcasted_iota(jnp.int32, (c_len, c_len), 0)
    cc = lax.broadcasted_iota(jnp.int32, (c_len, c_len), 1)
    tri = jnp.where(cc <= rr, 1.0, 0.0).astype(BF16)
    row8 = lax.broadcasted_iota(jnp.int32, (half, 1), 0)
    lane8 = lax.broadcasted_iota(jnp.int32, (half, c_len), 1)
    nt = (((1,), (1,)), ((), ()))

    def chunk(c, st):
        r0 = pl.multiple_of(c * c_len, c_len)
        rows = pl.ds(r0, c_len)
        z = f_ref[rows, :].astype(F32)
        qh = q_ref[rows, :].astype(F32)
        qh = qh * jax.nn.sigmoid(qh)
        v = i_ref[rows, :]
        gate = g_ref[rows, :].astype(F32)
        sp = jnp.log(1.0 + jnp.exp(-jnp.abs(z)))
        bb = l1m + jnp.minimum(z, 0.0) - sp
        log_f = jnp.maximum(la, bb) + jnp.log(1.0 + jnp.exp(-jnp.abs(la - bb)))
        valid = (r0 + lax.broadcasted_iota(jnp.int32, (c_len, 1), 0)) >= PAD0
        log_k = jnp.where(valid, l1m + jnp.minimum(-z, 0.0) - sp, -jnp.inf)
        g = sum(jnp.dot(tri, part, preferred_element_type=F32) for part in _split3(log_f))
        ck = log_k - g
        o_inter = lax.dot_general((qh * jnp.exp(g)).astype(BF16), st.astype(BF16), nt,
                                  preferred_element_type=F32)
        a_rows = []
        for a in range(c_len // sub):
            lo = a * sub
            ga = (g[lo:lo + half, :], g[lo + half:lo + sub, :])
            qa = (qh[lo:lo + half, :], qh[lo + half:lo + sub, :])
            if a == 0:
                blk = [jnp.zeros((half, c_len), F32)] * 2
            else:
                gs = g[lo - 1:lo, :]
                qd = (qh[lo:lo + sub, :] * jnp.exp(g[lo:lo + sub, :] - gs)).astype(BF16)
                kd = jnp.exp(jnp.minimum(gs - g[:lo, :], 0.0) + log_k[:lo, :]).astype(BF16)
                kd = jnp.concatenate([kd, jnp.zeros((c_len - lo, HGRN_D), BF16)], axis=0)
                a_off = lax.dot_general(qd, kd, nt, preferred_element_type=F32)
                blk = [a_off[:half, :], a_off[half:, :]]
            for s in range(sub):
                crow = ck[lo + s:lo + s + 1, :]
                for hh in range(s // half, 2):
                    col = jnp.sum(qa[hh] * jnp.exp(ga[hh] + crow), axis=-1, keepdims=True)
                    blk[hh] = jnp.where(lane8 == lo + s, col, blk[hh])
            for hh in range(2):
                a_rows.append(jnp.where(lane8 <= lo + hh * half + row8, blk[hh], 0.0))
        a_full = jnp.concatenate(a_rows, axis=0).astype(BF16)
        o = o_inter + jnp.dot(a_full, v, preferred_element_type=F32)
        g_last = g[c_len - 1:c_len, :]
        kd = jnp.exp(g_last - g + log_k).astype(BF16)
        st = st * jnp.exp(g_last) + lax.dot_general(v, kd, (((0,), (0,)), ((), ())),
                                                    preferred_element_type=F32)
        ms = jnp.mean(o * o, axis=-1, keepdims=True)
        y = o * lax.rsqrt(ms + EPS) * og * (gate * jax.nn.sigmoid(gate))
        o_ref[rows, :] = y.astype(BF16)
        return st

    lax.fori_loop(0, n_chunk, chunk, jnp.zeros((HGRN_D, HGRN_D), F32), unroll=3)


def _hgrn(proj3, log_lb, log1m_lb, out_gain):
    bsz, tp, _ = proj3.shape
    seq = lambda blk: pl.BlockSpec((None, tp, 128), lambda b, h: (b, 0, blk + h))
    chan = pl.BlockSpec((None, 1, HGRN_D), lambda b, h: (h, 0, 0))
    return pl.pallas_call(
        _hgrn_kernel,
        out_shape=jax.ShapeDtypeStruct((bsz, tp, HGRN_HEADS * HGRN_D), BF16),
        grid=(bsz, HGRN_HEADS),
        in_specs=[seq(BLK_RQ), seq(BLK_RF), seq(BLK_RI), seq(BLK_RG), chan, chan,
                  pl.BlockSpec((1, HGRN_D), lambda b, h: (0, 0))],
        out_specs=pl.BlockSpec((None, tp, 128), lambda b, h: (b, 0, h)),
        compiler_params=_params(("parallel", "parallel")),
        name="hgrn2",
    )(proj3, proj3, proj3, proj3,
      log_lb.reshape(HGRN_HEADS, 1, HGRN_D), log1m_lb.reshape(HGRN_HEADS, 1, HGRN_D),
      out_gain.reshape(1, HGRN_D))


def _merge_kernel(hs_ref, ua_ref, ur_ref, cb_ref, cc_ref, ch_ref, pc_ref, ph_ref,
                  g0_ref, g1_ref, g2_ref, cw_ref, wb_ref, wo_ref, o_ref, *, tm, tiles_per_seq):
    i = pl.program_id(0)
    row = (i % tiles_per_seq) * tm + lax.broadcasted_iota(jnp.int32, (tm, 1), 0)
    valid = row >= PAD0
    z = jnp.where(valid, cc_ref[...].astype(F32) * ch_ref[...].astype(F32), 0.0)
    halo_row = (i % tiles_per_seq) * tm - 8 + lax.broadcasted_iota(jnp.int32, (8, 1), 0)
    zp = jnp.where(halo_row >= PAD0, pc_ref[...].astype(F32) * ph_ref[...].astype(F32), 0.0)
    zz = jnp.concatenate([zp, z], axis=0)
    cw = cw_ref[...]
    y = (cw[2:3, :] * z + cw[1:2, :] * zz[7:7 + tm, :] + cw[0:1, :] * zz[6:6 + tm, :])
    u_conv = jnp.where(valid, cb_ref[...].astype(F32) * y, 0.0).astype(BF16)
    mixed = jnp.zeros((tm, D_MODEL), F32)
    for n, (u, g_ref) in enumerate(((ua_ref[...], g0_ref), (u_conv, g1_ref), (ur_ref[...], g2_ref))):
        up = jnp.dot(u, wb_ref[n], preferred_element_type=F32)
        mixed = mixed + jax.nn.sigmoid(g_ref[...].astype(F32)) * up
    o_ref[...] = hs_ref[...] + jnp.dot(mixed.astype(BF16), wo_ref[...], preferred_element_type=F32)


def _merge(hs, u_att, u_hgrn, proj, conv_w, wb_bf, wo_bf, tp):
    rows = hs.shape[0]
    tm = _row_tile(tp, 384)
    kern = functools.partial(_merge_kernel, tm=tm, tiles_per_seq=tp // tm)
    row_blk = lambda w, blk: pl.BlockSpec((tm, w), lambda i: (i, blk))
    halo = lambda blk: pl.BlockSpec((8, 512), lambda i: (jnp.maximum(i * (tm // 8) - 1, 0), blk))
    const = lambda shape: pl.BlockSpec(shape, lambda i: (0,) * len(shape))
    return pl.pallas_call(
        kern,
        out_shape=jax.ShapeDtypeStruct((rows, D_MODEL), F32),
        grid=(rows // tm,),
        in_specs=[row_blk(D_MODEL, 0), row_blk(512, 0), row_blk(512, 0),
                  row_blk(512, BLK_CB), row_blk(512, BLK_CC), row_blk(512, BLK_CH),
                  halo(BLK_CC), halo(BLK_CH),
                  row_blk(1024, BLK_GATE), row_blk(1024, BLK_GATE + 1), row_blk(1024, BLK_GATE + 2),
                  const((CONV_K, 512)), const((3, 512, D_MODEL)), const((D_MODEL, D_MODEL))],
        out_specs=row_blk(D_MODEL, 0),
        compiler_params=_params(("parallel",)),
        name="merge",
    )(hs, u_att, u_hgrn, proj, proj, proj, proj, proj, proj, proj, proj, conv_w, wb_bf, wo_bf)


def _ffn_kernel(hs_ref, g_ref, wg_ref, wu_ref, wd_ref, o_ref):
    x = hs_ref[...]
    ms = jnp.mean(x * x, axis=-1, keepdims=True)
    hn = (x * lax.rsqrt(ms + EPS) * g_ref[...]).astype(BF16)
    a = jnp.dot(hn, wg_ref[...], preferred_element_type=F32)
    u = jnp.dot(hn, wu_ref[...], preferred_element_type=F32)
    h = (a * jax.nn.sigmoid(a) * u).astype(BF16)
    o_ref[...] = x + jnp.dot(h, wd_ref[...], preferred_element_type=F32)


def _ffn(hs, gain, wg_bf, wu_bf, wd_bf):
    rows = hs.shape[0]
    d_ff = wg_bf.shape[1]
    tm = _row_tile(rows, 768)
    const = lambda shape: pl.BlockSpec(shape, lambda i: (0, 0), pipeline_mode=pl.Buffered(1))
    return pl.pallas_call(
        _ffn_kernel,
        out_shape=jax.ShapeDtypeStruct((rows, D_MODEL), F32),
        grid=(rows // tm,),
        in_specs=[pl.BlockSpec((tm, D_MODEL), lambda i: (i, 0)), const((1, D_MODEL)),
                  const((D_MODEL, d_ff)), const((D_MODEL, d_ff)), const((d_ff, D_MODEL))],
        out_specs=pl.BlockSpec((tm, D_MODEL), lambda i: (i, 0)),
        compiler_params=_params(("parallel",)),
        name="ffn_dense",
    )(hs, gain.reshape(1, D_MODEL), wg_bf, wu_bf, wd_bf)


def _router_kernel(hs_ref, g_ref, rw_ref, idx_ref, wt_ref, cnt_ref, carry_ref, *, tm, tiles_per_seq):
    i = pl.program_id(0)

    @pl.when(i == 0)
    def _():
        carry_ref[...] = jnp.zeros_like(carry_ref)

    x = hs_ref[...]
    ms = jnp.mean(x * x, axis=-1, keepdims=True)
    hn = x * lax.rsqrt(ms + EPS) * g_ref[...]
    logits = jnp.dot(hn, rw_ref[...], preferred_element_type=F32, precision=lax.Precision.HIGHEST)
    lane = lax.broadcasted_iota(jnp.int32, logits.shape, 1)
    lane_f = lane.astype(F32)
    logits = jnp.where(lane < N_EXPERTS, logits, -jnp.inf)
    m1 = jnp.max(logits, axis=-1, keepdims=True)
    i1 = jnp.min(jnp.where(logits == m1, lane_f, 128.0), axis=-1, keepdims=True)
    rest = jnp.where(lane_f == i1, -jnp.inf, logits)
    m2 = jnp.max(rest, axis=-1, keepdims=True)
    i2 = jnp.min(jnp.where(rest == m2, lane_f, 128.0), axis=-1, keepdims=True)
    e2 = jnp.exp(m2 - m1)
    w1 = 1.0 / (1.0 + e2)
    w2 = e2 / (1.0 + e2)
    row = (i % tiles_per_seq) * tm + lax.broadcasted_iota(jnp.int32, (tm, 1), 0)
    real = row >= T_PAD
    hot1 = jnp.where(jnp.logical_and(real, lane_f == i1), 1.0, 0.0)
    hot2 = jnp.where(jnp.logical_and(real, lane_f == i2), 1.0, 0.0)
    both = hot1 + hot2
    rr = lax.broadcasted_iota(jnp.int32, (tm, tm), 0)
    cc = lax.broadcasted_iota(jnp.int32, (tm, tm), 1)
    earlier = jnp.where(cc < rr, 1.0, 0.0).astype(BF16)
    before = carry_ref[...] + jnp.dot(earlier, both.astype(BF16), preferred_element_type=F32)
    r1 = jnp.sum(before * hot1, axis=-1, keepdims=True)
    r2 = jnp.sum(before * hot2, axis=-1, keepdims=True)
    carry_ref[...] += jnp.sum(both, axis=0, keepdims=True)
    cnt_ref[...] = carry_ref[...]
    packed = jnp.where(lane == 0, i1, jnp.where(lane == 1, i2, jnp.where(lane == 2, r1, jnp.where(lane == 3, r2, 0.0))))
    idx_ref[...] = packed.astype(jnp.int32)
    wt_ref[...] = jnp.where(lane == 0, w1, jnp.where(lane == 1, w2, 0.0))


def _router(hs, gain, router_w, tp):
    rows = hs.shape[0]
    tm = _row_tile(tp, 384)
    rw = jnp.zeros((D_MODEL, 128), F32).at[:, :N_EXPERTS].set(router_w.astype(F32))
    blk = pl.BlockSpec((tm, 128), lambda i: (i, 0))
    return pl.pallas_call(
        functools.partial(_router_kernel, tm=tm, tiles_per_seq=tp // tm),
        out_shape=(jax.ShapeDtypeStruct((rows, 128), jnp.int32), jax.ShapeDtypeStruct((rows, 128), F32),
                   jax.ShapeDtypeStruct((1, 128), F32)),
        grid=(rows // tm,),
        in_specs=[pl.BlockSpec((tm, D_MODEL), lambda i: (i, 0)),
                  pl.BlockSpec((1, D_MODEL), lambda i: (0, 0)),
                  pl.BlockSpec((D_MODEL, 128), lambda i: (0, 0))],
        out_specs=(blk, blk, pl.BlockSpec((1, 128), lambda i: (0, 0))),
        scratch_shapes=[pltpu.VMEM((1, 128), F32)],
        compiler_params=_params(("arbitrary",)),
        name="moe_router",
    )(hs, gain.reshape(1, D_MODEL), rw)


def _dispatch_kernel(s1_ref, s2_ref, zr_ref, hs_ref, wt_ref, g_ref, xs_ref, rows_ref, zero_ref, sem, zsem,
                     *, td, tg, tiles_per_seq, n_steps):
    n = pl.program_id(0) * tiles_per_seq + pl.program_id(1)
    buf = n % 2
    base = n * td

    def row_copy(r, which):
        slot = (s1_ref, s2_ref)[which][base + r]
        return pltpu.make_async_copy(rows_ref.at[buf, which, pl.ds(r, 1)], xs_ref.at[pl.ds(slot, 1)],
                                     sem.at[buf])

    def wait_rows(b):
        def body(r, c):
            pltpu.make_async_copy(rows_ref.at[b, 0, pl.ds(0, 1)], xs_ref.at[pl.ds(0, 1)], sem.at[b]).wait()
            return c
        lax.fori_loop(0, 2 * td, body, 0)

    @pl.when(n == 0)
    def _():
        zero_ref[...] = jnp.zeros_like(zero_ref)
        for e in range(2 * N_EXPERTS):
            fill = pltpu.make_async_copy(zero_ref, xs_ref.at[pl.ds(pl.multiple_of(zr_ref[e], tg), tg)], zsem)
            fill.start()
            fill.wait()

    @pl.when(n >= 2)
    def _():
        wait_rows(buf)

    x = hs_ref[...]
    ms = jnp.mean(x * x, axis=-1, keepdims=True)
    hn = x * lax.rsqrt(ms + EPS) * g_ref[...]
    wt = wt_ref[...]
    for which in range(2):
        rows_ref[buf, which, :, :D_MODEL] = hn
        rows_ref[buf, which, :, D_MODEL:] = jnp.broadcast_to(wt[:, which:which + 1], (td, XS_EXTRA))

    def issue(r, c):
        row_copy(r, 0).start()
        row_copy(r, 1).start()
        return c

    lax.fori_loop(0, td, issue, 0)

    @pl.when(n == n_steps - 1)
    def _():
        wait_rows(buf)
        if n_steps > 1:
            wait_rows(1 - buf)


def _dispatch(hs3, wts3, gain, slot1, slot2, zero_rows, n_slots, tg):
    bsz, tp, _ = hs3.shape
    td = T_PAD
    tiles_per_seq = (tp - T_PAD) // td
    width = D_MODEL + XS_EXTRA
    kern = functools.partial(_dispatch_kernel, td=td, tg=tg, tiles_per_seq=tiles_per_seq,
                             n_steps=bsz * tiles_per_seq)
    return pl.pallas_call(
        kern,
        out_shape=jax.ShapeDtypeStruct((n_slots, width), F32),
        grid_spec=pltpu.PrefetchScalarGridSpec(
            num_scalar_prefetch=3,
            grid=(bsz, tiles_per_seq),
            in_specs=[pl.BlockSpec((None, td, D_MODEL), lambda b, i, *_: (b, i + 1, 0)),
                      pl.BlockSpec((None, td, 128), lambda b, i, *_: (b, i + 1, 0)),
                      pl.BlockSpec((1, D_MODEL), lambda b, i, *_: (0, 0))],
            out_specs=pl.BlockSpec(memory_space=pl.ANY),
            scratch_shapes=[pltpu.VMEM((2, 2, td, width), F32), pltpu.VMEM((tg, width), F32),
                            pltpu.SemaphoreType.DMA((2,)), pltpu.SemaphoreType.DMA],
        ),
        compiler_params=_params(("arbitrary", "arbitrary")),
        name="moe_dispatch",
    )(slot1, slot2, zero_rows, hs3, wts3, gain.reshape(1, D_MODEL))


def _expert_kernel(te_ref, nt_ref, x_ref, wg_ref, wu_ref, wd_ref, o_ref, hn_ref, acc_ref, *, n_f):
    t = pl.program_id(0)
    f = pl.program_id(1)

    @pl.when(t < nt_ref[0])
    def _():
        @pl.when(f == 0)
        def _():
            hn_ref[...] = x_ref[:, :D_MODEL].astype(BF16)

        hn = hn_ref[...]
        a = jnp.dot(hn, wg_ref[...], preferred_element_type=F32)
        u = jnp.dot(hn, wu_ref[...], preferred_element_type=F32)
        h = (a * jax.nn.sigmoid(a) * u).astype(BF16)

        @pl.when(f == 0)
        def _():
            acc_ref[...] = jnp.dot(h, wd_ref[...], preferred_element_type=F32)

        @pl.when(jnp.logical_and(f > 0, f < n_f - 1))
        def _():
            acc_ref[...] += jnp.dot(h, wd_ref[...], preferred_element_type=F32)

        @pl.when(f == n_f - 1)
        def _():
            y = acc_ref[...] + jnp.dot(h, wd_ref[...], preferred_element_type=F32)
            o_ref[...] = (y * x_ref[:, D_MODEL:D_MODEL + 1]).astype(BF16)

    @pl.when(jnp.logical_and(t >= nt_ref[0], f == n_f - 1))
    def _():
        o_ref[...] = jnp.zeros_like(o_ref)


def _experts(xs, tile_expert, n_tiles_used, wg_bf, wu_bf, wd_bf, tg):
    slots = xs.shape[0]
    n_f = MOE_F_BLOCKS
    tf = wg_bf.shape[2] // n_f

    def x_map(t, f, te, nt):
        return (jnp.maximum(jnp.minimum(t, nt[0] - 1), 0), 0)

    def f_of(t, f, nt):
        return jnp.where(t < nt[0], f, n_f - 1)

    return pl.pallas_call(
        functools.partial(_expert_kernel, n_f=n_f),
        out_shape=jax.ShapeDtypeStruct((slots, D_MODEL), BF16),
        grid_spec=pltpu.PrefetchScalarGridSpec(
            num_scalar_prefetch=2,
            grid=(slots // tg, n_f),
            in_specs=[
                pl.BlockSpec((tg, D_MODEL + XS_EXTRA), x_map),
                pl.BlockSpec((None, D_MODEL, tf), lambda t, f, te, nt: (te[t], 0, f_of(t, f, nt))),
                pl.BlockSpec((None, D_MODEL, tf), lambda t, f, te, nt: (te[t], 0, f_of(t, f, nt))),
                pl.BlockSpec((None, tf, D_MODEL), lambda t, f, te, nt: (te[t], f_of(t, f, nt), 0)),
            ],
            out_specs=pl.BlockSpec((tg, D_MODEL), lambda t, f, te, nt: (t, 0)),
            scratch_shapes=[pltpu.VMEM((tg, D_MODEL), BF16), pltpu.VMEM((tg, D_MODEL), F32)],
        ),
        compiler_params=_params(("arbitrary", "arbitrary")),
        name="moe_experts",
    )(tile_expert, n_tiles_used, xs, wg_bf, wu_bf, wd_bf)


def _combine_kernel(ws_ref, hs_ref, route_ref, y_ref, o_ref, win_ref, sem, *, tc, tiles_per_seq, n_steps):
    n = pl.program_id(0) * tiles_per_seq + pl.program_id(1)
    cur = n % 2

    def window_copy(step, b, e):
        start = pl.multiple_of(ws_ref[step * N_EXPERTS + e], WIN_ALIGN)
        return pltpu.make_async_copy(y_ref.at[pl.ds(start, COMBINE_WIN)],
                                     win_ref.at[b, pl.ds(e * COMBINE_WIN, COMBINE_WIN)], sem.at[b])

    def fetch(step, b):
        for e in range(N_EXPERTS):
            window_copy(step, b, e).start()

    @pl.when(n == 0)
    def _():
        fetch(0, 0)

    @pl.when(n + 1 < n_steps)
    def _():
        fetch(n + 1, 1 - cur)

    for e in range(N_EXPERTS):
        window_copy(n, cur, e).wait()

    route = route_ref[...]
    e1, e2, s1, s2 = (route[:, c:c + 1] for c in range(4))
    start1 = jnp.zeros_like(s1)
    start2 = jnp.zeros_like(s2)
    for e in range(N_EXPERTS):
        start = ws_ref[n * N_EXPERTS + e]
        start1 = jnp.where(e1 == e, start, start1)
        start2 = jnp.where(e2 == e, start, start2)
    col1 = e1 * COMBINE_WIN + s1 - start1
    col2 = e2 * COMBINE_WIN + s2 - start2
    pos = lax.broadcasted_iota(jnp.int32, (1, N_EXPERTS * COMBINE_WIN), 1)
    pick = jnp.where(jnp.logical_or(pos == col1, pos == col2), 1.0, 0.0).astype(BF16)
    o_ref[...] = hs_ref[...] + jnp.dot(pick, win_ref[cur], preferred_element_type=F32)


def _combine(hs3, route, y, win_start, seq):
    bsz = hs3.shape[0]
    tc = COMBINE_TOKENS
    tiles_per_seq = seq // tc
    kern = functools.partial(_combine_kernel, tc=tc, tiles_per_seq=tiles_per_seq, n_steps=bsz * tiles_per_seq)
    tok = lambda w: pl.BlockSpec((tc, w), lambda b, i, ws: (b * tiles_per_seq + i, 0))
    return pl.pallas_call(
        kern,
        out_shape=jax.ShapeDtypeStruct((bsz, seq, D_MODEL), F32),
        grid_spec=pltpu.PrefetchScalarGridSpec(
            num_scalar_prefetch=1,
            grid=(bsz, tiles_per_seq),
            in_specs=[
                pl.BlockSpec((None, tc, D_MODEL), lambda b, i, ws: (b, i + T_PAD // tc, 0)),
                tok(4),
                pl.BlockSpec(memory_space=pl.ANY),
            ],
            out_specs=pl.BlockSpec((None, tc, D_MODEL), lambda b, i, ws: (b, i, 0)),
            scratch_shapes=[pltpu.VMEM((2, N_EXPERTS * COMBINE_WIN, D_MODEL), BF16),
                            pltpu.SemaphoreType.DMA((2,))],
        ),
        compiler_params=_params(("arbitrary", "arbitrary")),
        name="moe_combine",
    )(win_start, hs3, route, y)


def _moe(hs, gain, router_w, wg_bf, wu_bf, wd_bf, bsz, tp):
    seq = tp - T_PAD
    n_tok = bsz * seq
    tg = MOE_TILE
    idx, wts, cnt = _router(hs, gain, router_w, tp)
    sel = idx.reshape(bsz, tp, 128)[:, T_PAD:, :4].reshape(n_tok, 4)
    counts = cnt[0, :N_EXPERTS].astype(jnp.int32)
    padded = ((counts + tg - 1) // tg) * tg
    ends = jnp.cumsum(padded)
    starts = ends - padded
    experts = jnp.arange(N_EXPERTS, dtype=jnp.int32)[None, :]
    slot1 = jnp.sum(jnp.where(sel[:, 0:1] == experts, starts[None, :], 0), axis=1) + sel[:, 2]
    slot2 = jnp.sum(jnp.where(sel[:, 1:2] == experts, starts[None, :], 0), axis=1) + sel[:, 3]
    slot1, slot2 = slot1.astype(jnp.int32), slot2.astype(jnp.int32)
    n_slots = 2 * n_tok + N_EXPERTS * tg
    tile_start = jnp.arange(n_slots // tg, dtype=jnp.int32)[:, None] * tg
    tile_expert = jnp.minimum(jnp.sum((tile_start >= ends[None, :]).astype(jnp.int32), axis=1), N_EXPERTS - 1)
    n_tiles_used = (ends[-1:] // tg).astype(jnp.int32)
    tail = n_slots - tg * (1 + jnp.arange(N_EXPERTS, dtype=jnp.int32))
    zero_rows = jnp.concatenate([jnp.maximum(ends - tg, 0), tail]).astype(jnp.int32)

    tc = COMBINE_TOKENS
    first = jnp.minimum(
        jnp.min(jnp.where(sel[:, 0:1] == experts, slot1[:, None], n_slots).reshape(n_tok // tc, tc, N_EXPERTS), axis=1),
        jnp.min(jnp.where(sel[:, 1:2] == experts, slot2[:, None], n_slots).reshape(n_tok // tc, tc, N_EXPERTS), axis=1))
    win_start = jnp.where(first == n_slots, 0,
                          jnp.minimum(first // WIN_ALIGN * WIN_ALIGN, n_slots - COMBINE_WIN))
    route = jnp.concatenate([sel[:, :2], slot1[:, None], slot2[:, None]], axis=1)

    hs3 = hs.reshape(bsz, tp, D_MODEL)
    xs = _dispatch(hs3, wts.reshape(bsz, tp, 128), gain, slot1, slot2, zero_rows, n_slots, tg)
    y = _experts(xs, tile_expert.astype(jnp.int32), n_tiles_used, wg_bf, wu_bf, wd_bf, tg)
    return _combine(hs3, route, y, win_start.reshape(-1).astype(jnp.int32), seq)


def _permute_qk_cols(w):
    return w.reshape(D_MODEL, 2, ATT_HEADS, ATT_QK_DIM).transpose(0, 2, 1, 3).reshape(D_MODEL, 512)


def kernel(x, meta_tokens, norm1_gain, norm2_gain, w_in, q_norm_gain, k_norm_gain, diff_lambda,
           attn_sub_gain, rel_bias, conv_w, hgrn_lb_logits, hgrn_out_gain, w_branch, w_out,
           ffn_w_gate, ffn_w_up, ffn_w_down, router_w, moe_w_gate, moe_w_up, moe_w_down):
    bsz, seq, _ = x.shape
    depth = w_in.shape[0]
    tp = T_PAD + seq
    assert tp % ATT_TILE == 0 and depth == 2

    head = jnp.concatenate([jnp.zeros((PAD0, D_MODEL), x.dtype), meta_tokens.astype(x.dtype)], axis=0)
    hs = jnp.concatenate([jnp.broadcast_to(head[None], (bsz, T_PAD, D_MODEL)), x], axis=1)
    hs = hs.reshape(bsz * tp, D_MODEL)

    lb_all = jnp.cumsum(jax.nn.softmax(hgrn_lb_logits.astype(F32), axis=0), axis=0)
    lb_all = lb_all - lb_all[0]
    btab = _attn_bias_tables(rel_bias, ATT_TILE)

    out = None
    for layer in range(depth):
        w = w_in[layer]
        w_bf = jnp.concatenate([_permute_qk_cols(w[:, :512]), _permute_qk_cols(w[:, 512:1024]), w[:, 1024:]],
                               axis=1).astype(BF16)
        qk_gain = jnp.concatenate([jnp.tile(q_norm_gain[layer].astype(F32), 8) * (ATT_QK_DIM ** -0.5 * LOG2E),
                                   jnp.tile(k_norm_gain[layer].astype(F32), 8)]).reshape(1, COL_TILE)
        proj = _inproj(hs, norm1_gain[layer], w_bf, qk_gain)
        proj3 = proj.reshape(bsz, tp, IN_COLS)

        lam_init = 0.8 - 0.6 * math.exp(-0.3 * layer)
        lp = diff_lambda[layer].astype(F32)
        lam = jnp.exp(jnp.sum(lp[0] * lp[1])) - jnp.exp(jnp.sum(lp[2] * lp[3])) + lam_init
        u_att = _diff_attention(proj3, lam.reshape(1), attn_sub_gain[layer].astype(F32), btab,
                                1.0 - lam_init)

        lb = lb_all[layer]
        u_hgrn = _hgrn(proj3, jnp.log(lb), jnp.log1p(-lb), hgrn_out_gain[layer].astype(F32))

        hs = _merge(hs, u_att.reshape(bsz * tp, 512), u_hgrn.reshape(bsz * tp, 512), proj,
                    conv_w[layer].astype(F32), w_branch[layer].astype(BF16), w_out[layer].astype(BF16), tp)

        j = layer // 2
        if layer % 2 == 0:
            hs = _ffn(hs, norm2_gain[layer], ffn_w_gate[j].astype(BF16), ffn_w_up[j].astype(BF16),
                      ffn_w_down[j].astype(BF16))
        else:
            out = _moe(hs, norm2_gain[layer], router_w[j], moe_w_gate[j].astype(BF16),
                       moe_w_up[j].astype(BF16), moe_w_down[j].astype(BF16), bsz, tp)
    return out
```

```python
import functools
import math

import numpy as np
import jax
import jax.numpy as jnp
from jax import lax
from jax.experimental import pallas as pl
from jax.experimental.pallas import tpu as pltpu

F32 = jnp.float32
BF16 = jnp.bfloat16

D_MODEL = 1024
N_META = 16
EPS = 1e-6
ATT_HEADS = 4
ATT_QK_DIM = 64
ATT_V_DIM = 128
REL_BUCKETS = 32
REL_MAX_DIST = 128
CONV_K = 3
HGRN_HEADS = 4
HGRN_D = 128
N_EXPERTS = 8
IN_COLS = 8192

T_PAD = 128
PAD0 = T_PAD - N_META
ATT_TILE = 384
VT_ONES = 16
LOG2E = math.log2(math.e)
HGRN_CHUNK = 128
HGRN_SUB = 16
COL_TILE = 1024
GROUP_MEAN_WIDTH = 256
MOE_TILE = 512
MOE_F_BLOCKS = 2
XS_EXTRA = 128
COMBINE_TOKENS = 128
WIN_ALIGN = 16
COMBINE_WIN = COMBINE_TOKENS + WIN_ALIGN
MASK = -1e30
VMEM_LIMIT = 56 * 1024 * 1024

BLK_Q, BLK_K, BLK_V = 0, 4, 8
BLK_RQ, BLK_RF, BLK_RI, BLK_RG = 24, 28, 32, 36
BLK_CB, BLK_CC, BLK_CH = 3, 4, 5
BLK_GATE = 5


def _row_tile(rows, target):
    n = rows // 128
    best = 1
    for d in range(1, n + 1):
        if n % d == 0 and d * 128 <= target:
            best = d
    return best * 128


def _params(sem, vmem=VMEM_LIMIT):
    return pltpu.CompilerParams(dimension_semantics=sem, vmem_limit_bytes=vmem)


def _inproj_kernel(x_ref, g_ref, w_ref, qkg_ref, gm_ref, o_ref, xn_ref):
    j = pl.program_id(1)

    @pl.when(j == 0)
    def _():
        x = x_ref[...]
        ms = jnp.mean(x * x, axis=-1, keepdims=True)
        xn_ref[...] = (x * lax.rsqrt(ms + EPS) * g_ref[...]).astype(BF16)

    acc = jnp.dot(xn_ref[...], w_ref[...], preferred_element_type=F32)

    @pl.when(j == 0)
    def _():
        sq = acc * acc
        hi = sq.astype(BF16)
        lo = (sq - hi.astype(F32)).astype(BF16)
        gm = gm_ref[...]
        width = gm.shape[0]
        ms = jnp.concatenate(
            [jnp.dot(hi[:, c:c + width], gm, preferred_element_type=F32)
             + jnp.dot(lo[:, c:c + width], gm, preferred_element_type=F32)
             for c in range(0, COL_TILE, width)], axis=1)
        o_ref[...] = (acc * lax.rsqrt(ms + EPS) * qkg_ref[...]).astype(BF16)

    @pl.when(j > 0)
    def _():
        o_ref[...] = acc.astype(BF16)


def _inproj(hs, gain, w_bf, qk_gain):
    rows = hs.shape[0]
    tm = _row_tile(rows, 1536)
    n_col = IN_COLS // COL_TILE
    assert COL_TILE == 4 * ATT_HEADS * ATT_QK_DIM
    grp = np.arange(GROUP_MEAN_WIDTH) // ATT_QK_DIM
    gm = jnp.asarray((grp[:, None] == grp[None, :]).astype(np.float32) / ATT_QK_DIM, BF16)
    return pl.pallas_call(
        _inproj_kernel,
        out_shape=jax.ShapeDtypeStruct((rows, IN_COLS), BF16),
        grid=(rows // tm, n_col),
        in_specs=[
            pl.BlockSpec((tm, D_MODEL), lambda i, j: (i, 0)),
            pl.BlockSpec((1, D_MODEL), lambda i, j: (0, 0)),
            pl.BlockSpec((D_MODEL, COL_TILE), lambda i, j: (0, j)),
            pl.BlockSpec((1, COL_TILE), lambda i, j: (0, 0)),
            pl.BlockSpec((GROUP_MEAN_WIDTH, GROUP_MEAN_WIDTH), lambda i, j: (0, 0)),
        ],
        out_specs=pl.BlockSpec((tm, COL_TILE), lambda i, j: (i, j)),
        scratch_shapes=[pltpu.VMEM((tm, D_MODEL), BF16)],
        compiler_params=_params(("parallel", "arbitrary")),
        name="inproj",
    )(hs, gain.reshape(1, D_MODEL), w_bf, qk_gain, gm)


def _rel_bucket_table(n_max):
    n = np.arange(n_max, dtype=np.int64)
    max_exact = REL_BUCKETS // 2
    nf = np.maximum(n, 1).astype(np.float32)
    large = max_exact + (np.log(nf / np.float32(max_exact)) / np.float32(math.log(REL_MAX_DIST / max_exact))
                         * np.float32(REL_BUCKETS - max_exact)).astype(np.int32)
    large = np.minimum(large, REL_BUCKETS - 1)
    return np.where(n < max_exact, n, large).astype(np.int32)


def _attn_bias_tables(rel_bias, t):
    bucket = _rel_bucket_table(2 * t)
    assert np.all(bucket[t + 1:] == REL_BUCKETS - 1) and np.all(np.diff(bucket) >= 0)
    first_dist = tuple(int(np.searchsorted(bucket, b, side="left")) for b in range(REL_BUCKETS))
    return pl.pallas_call(
        functools.partial(_bias_kernel, t=t, first_dist=first_dist),
        out_shape=jax.ShapeDtypeStruct((ATT_HEADS, 6, t, t), F32),
        grid=(ATT_HEADS,),
        in_specs=[pl.BlockSpec(memory_space=pltpu.SMEM)],
        out_specs=pl.BlockSpec((None, 6, t, t), lambda h: (h, 0, 0, 0)),
        compiler_params=_params(("parallel",)),
        name="attn_bias",
    )(rel_bias.astype(F32))


def _bias_kernel(rb_ref, o_ref, *, t, first_dist):
    h = pl.program_id(0)
    key = lax.broadcasted_iota(jnp.int32, (t, t), 0)
    qry = lax.broadcasted_iota(jnp.int32, (t, t), 1)
    far = rb_ref[REL_BUCKETS - 1, h]

    def table(n):
        val = jnp.full((t, t), rb_ref[0, h] - far, F32)
        for b in range(1, REL_BUCKETS):
            val = jnp.where(n >= first_dist[b], rb_ref[b, h] - far, val)
        return val * LOG2E

    n0 = qry - key
    diag = jnp.where(n0 >= 0, table(n0), MASK)
    near = table(n0 + t)
    zero = jnp.zeros((t, t), F32)
    for kind, tab in enumerate((diag, near, zero)):
        o_ref[kind] = tab
        o_ref[kind + 3] = jnp.where(key < PAD0, MASK, tab)


def _attn_kernel(lam_ref, q_ref, k_ref, v_ref, bt_ref, sg_ref, o_ref, m_ref, acc_ref, vt_ref,
                 *, t, out_scale):
    n_t = vt_ref.shape[0]
    for j in range(n_t):
        vt_ref[j, :ATT_V_DIM, :] = v_ref[j * t:(j + 1) * t, :].astype(F32).T.astype(BF16)
        vt_ref[j, ATT_V_DIM:, :] = jnp.ones((VT_ONES, t), BF16)
    lane = lax.broadcasted_iota(jnp.int32, (1, 2 * ATT_QK_DIM), 1)
    nt = (((1,), (1,)), ((), ()))

    def tile_rows(i):
        return pl.ds(pl.multiple_of(i * t, t), t)

    def scores(i, j):
        q = q_ref[tile_rows(i), :]
        zero = jnp.zeros_like(q)
        q_cat = jnp.concatenate([jnp.where(lane < ATT_QK_DIM, q, zero), jnp.where(lane >= ATT_QK_DIM, q, zero)],
                                axis=0)
        bias = bt_ref[jnp.minimum(i - j, 2) + jnp.where(j == 0, 3, 0)]
        s = (lax.dot_general(k_ref[tile_rows(j), :], q_cat, nt, preferred_element_type=F32)
             + jnp.concatenate([bias, bias], axis=1))
        return s, jnp.max(s, axis=0, keepdims=True)

    def consume(j, scored):
        s, s_max = scored
        m_old = m_ref[...]
        m_new = jnp.maximum(m_old, s_max)
        m_ref[...] = m_new
        p = jnp.exp2(s - m_new).astype(BF16)
        acc_ref[...] = (jnp.exp2(m_old - m_new) * acc_ref[...]
                        + jnp.dot(vt_ref[j], p, preferred_element_type=F32))

    def reset():
        m_ref[...] = jnp.full(m_ref.shape, MASK, F32)
        acc_ref[...] = jnp.zeros(acc_ref.shape, F32)

    def finish(i):
        acc = acc_ref[...]
        a1, a2 = acc[:, :t], acc[:, t:]
        o_t = (a1[:ATT_V_DIM] / a1[ATT_V_DIM:ATT_V_DIM + 1]
               - lam_ref[0] * (a2[:ATT_V_DIM] / a2[ATT_V_DIM:ATT_V_DIM + 1]))
        o = o_t.T
        ms = jnp.mean(o * o, axis=-1, keepdims=True)
        y = o * lax.rsqrt(ms + EPS) * (sg_ref[...] * out_scale)
        row = i * t + lax.broadcasted_iota(jnp.int32, (t, 1), 0)
        o_ref[tile_rows(i), :] = jnp.where(row >= PAD0, y, 0.0).astype(BF16)
        reset()

    def step(_, carry):
        i, j, s = carry
        last = j == i
        ni = jnp.where(last, i + 1, i)
        nj = jnp.where(last, 0, j + 1)
        nxt = scores(jnp.minimum(ni, n_t - 1), nj)
        consume(j, s)

        @pl.when(last)
        def _():
            finish(i)

        return ni, nj, nxt

    reset()
    first = jnp.int32(0)
    lax.fori_loop(0, n_t * (n_t + 1) // 2, step, (first, first, scores(first, first)))


def _diff_attention(proj3, lam, sub_gain, btab, out_scale):
    bsz, tp, _ = proj3.shape
    t = ATT_TILE
    n_t = tp // t
    rows = ATT_V_DIM + VT_ONES
    kern = functools.partial(_attn_kernel, t=t, out_scale=out_scale)
    seq = lambda blk: pl.BlockSpec((None, tp, 128), lambda b, h: (b, 0, blk + h))
    return pl.pallas_call(
        kern,
        out_shape=jax.ShapeDtypeStruct((bsz, tp, ATT_HEADS * ATT_V_DIM), BF16),
        grid=(bsz, ATT_HEADS),
        in_specs=[
            pl.BlockSpec(memory_space=pltpu.SMEM),
            seq(BLK_Q), seq(BLK_K), seq(BLK_V),
            pl.BlockSpec((None, 6, t, t), lambda b, h: (h, 0, 0, 0)),
            pl.BlockSpec((1, ATT_V_DIM), lambda b, h: (0, 0)),
        ],
        out_specs=pl.BlockSpec((None, tp, 128), lambda b, h: (b, 0, h)),
        scratch_shapes=[pltpu.VMEM((1, 2 * t), F32), pltpu.VMEM((rows, 2 * t), F32),
                        pltpu.VMEM((n_t, rows, t), BF16)],
        compiler_params=_params(("parallel", "parallel")),
        name="diff_attn",
    )(lam, proj3, proj3, proj3, btab, sub_gain.reshape(1, ATT_V_DIM))


def _split3(x):
    h1 = x.astype(BF16)
    r1 = x - h1.astype(F32)
    h2 = r1.astype(BF16)
    h3 = (r1 - h2.astype(F32)).astype(BF16)
    return h1, h2, h3


def _hgrn_kernel(q_ref, f_ref, i_ref, g_ref, la_ref, l1m_ref, og_ref, o_ref):
    c_len, sub = HGRN_CHUNK, HGRN_SUB
    half = sub // 2
    n_chunk = q_ref.shape[0] // c_len
    la, l1m, og = la_ref[...], l1m_ref[...], og_ref[...]
    rr = lax.broadcasted_iota(jnp.int32, (c_len, c_len), 0)
    cc = lax.broadcasted_iota(jnp.int32, (c_len, c_len), 1)
    tri = jnp.where(cc <= rr, 1.0, 0.0).astype(BF16)
    row8 = lax.broadcasted_iota(jnp.int32, (half, 1), 0)
    lane8 = lax.broadcasted_iota(jnp.int32, (half, c_len), 1)
    nt = (((1,), (1,)), ((), ()))

    def chunk(c, st):
        r0 = pl.multiple_of(c * c_len, c_len)
        rows = pl.ds(r0, c_len)
        z = f_ref[rows, :].astype(F32)
        qh = q_ref[rows, :].astype(F32)
        qh = qh * jax.nn.sigmoid(qh)
        v = i_ref[rows, :]
        gate = g_ref[rows, :].astype(F32)
        sp = jnp.log(1.0 + jnp.exp(-jnp.abs(z)))
        bb = l1m + jnp.minimum(z, 0.0) - sp
        log_f = jnp.maximum(la, bb) + jnp.log(1.0 + jnp.exp(-jnp.abs(la - bb)))
        valid = (r0 + lax.broadcasted_iota(jnp.int32, (c_len, 1), 0)) >= PAD0
        log_k = jnp.where(valid, l1m + jnp.minimum(-z, 0.0) - sp, -jnp.inf)
        g = sum(jnp.dot(tri, part, preferred_element_type=F32) for part in _split3(log_f))
        ck = log_k - g
        o_inter = lax.dot_general((qh * jnp.exp(g)).astype(BF16), st.astype(BF16), nt,
                                  preferred_element_type=F32)
        a_rows = []
        for a in range(c_len // sub):
            lo = a * sub
            ga = (g[lo:lo + half, :], g[lo + half:lo + sub, :])
            qa = (qh[lo:lo + half, :], qh[lo + half:lo + sub, :])
            if a == 0:
                blk = [jnp.zeros((half, c_len), F32)] * 2
            else:
                gs = g[lo - 1:lo, :]
                qd = (qh[lo:lo + sub, :] * jnp.exp(g[lo:lo + sub, :] - gs)).astype(BF16)
                kd = jnp.exp(jnp.minimum(gs - g[:lo, :], 0.0) + log_k[:lo, :]).astype(BF16)
                kd = jnp.concatenate([kd, jnp.zeros((c_len - lo, HGRN_D), BF16)], axis=0)
                a_off = lax.dot_general(qd, kd, nt, preferred_element_type=F32)
                blk = [a_off[:half, :], a_off[half:, :]]
            for s in range(sub):
                crow = ck[lo + s:lo + s + 1, :]
                for hh in range(s // half, 2):
                    col = jnp.sum(qa[hh] * jnp.exp(ga[hh] + crow), axis=-1, keepdims=True)
                    blk[hh] = jnp.where(lane8 == lo + s, col, blk[hh])
            for hh in range(2):
                a_rows.append(jnp.where(lane8 <= lo + hh * half + row8, blk[hh], 0.0))
        a_full = jnp.concatenate(a_rows, axis=0).astype(BF16)
        o = o_inter + jnp.dot(a_full, v, preferred_element_type=F32)
        g_last = g[c_len - 1:c_len, :]
        kd = jnp.exp(g_last - g + log_k).astype(BF16)
        st = st * jnp.exp(g_last) + lax.dot_general(v, kd, (((0,), (0,)), ((), ())),
                                                    preferred_element_type=F32)
        ms = jnp.mean(o * o, axis=-1, keepdims=True)
        y = o * lax.rsqrt(ms + EPS) * og * (gate * jax.nn.sigmoid(gate))
        o_ref[rows, :] = y.astype(BF16)
        return st

    lax.fori_loop(0, n_chunk, chunk, jnp.zeros((HGRN_D, HGRN_D), F32), unroll=3)


def _hgrn(proj3, log_lb, log1m_lb, out_gain):
    bsz, tp, _ = proj3.shape
    seq = lambda blk: pl.BlockSpec((None, tp, 128), lambda b, h: (b, 0, blk + h))
    chan = pl.BlockSpec((None, 1, HGRN_D), lambda b, h: (h, 0, 0))
    return pl.pallas_call(
        _hgrn_kernel,
        out_shape=jax.ShapeDtypeStruct((bsz, tp, HGRN_HEADS * HGRN_D), BF16),
        grid=(bsz, HGRN_HEADS),
        in_specs=[seq(BLK_RQ), seq(BLK_RF), seq(BLK_RI), seq(BLK_RG), chan, chan,
                  pl.BlockSpec((1, HGRN_D), lambda b, h: (0, 0))],
        out_specs=pl.BlockSpec((None, tp, 128), lambda b, h: (b, 0, h)),
        compiler_params=_params(("parallel", "parallel")),
        name="hgrn2",
    )(proj3, proj3, proj3, proj3,
      log_lb.reshape(HGRN_HEADS, 1, HGRN_D), log1m_lb.reshape(HGRN_HEADS, 1, HGRN_D),
      out_gain.reshape(1, HGRN_D))


def _merge_kernel(hs_ref, ua_ref, ur_ref, cb_ref, cc_ref, ch_ref, pc_ref, ph_ref,
                  g0_ref, g1_ref, g2_ref, cw_ref, wb_ref, wo_ref, o_ref, *, tm, tiles_per_seq):
    i = pl.program_id(0)
    row = (i % tiles_per_seq) * tm + lax.broadcasted_iota(jnp.int32, (tm, 1), 0)
    valid = row >= PAD0
    z = jnp.where(valid, cc_ref[...].astype(F32) * ch_ref[...].astype(F32), 0.0)
    halo_row = (i % tiles_per_seq) * tm - 8 + lax.broadcasted_iota(jnp.int32, (8, 1), 0)
    zp = jnp.where(halo_row >= PAD0, pc_ref[...].astype(F32) * ph_ref[...].astype(F32), 0.0)
    zz = jnp.concatenate([zp, z], axis=0)
    cw = cw_ref[...]
    y = (cw[2:3, :] * z + cw[1:2, :] * zz[7:7 + tm, :] + cw[0:1, :] * zz[6:6 + tm, :])
    u_conv = jnp.where(valid, cb_ref[...].astype(F32) * y, 0.0).astype(BF16)
    mixed = jnp.zeros((tm, D_MODEL), F32)
    for n, (u, g_ref) in enumerate(((ua_ref[...], g0_ref), (u_conv, g1_ref), (ur_ref[...], g2_ref))):
        up = jnp.dot(u, wb_ref[n], preferred_element_type=F32)
        mixed = mixed + jax.nn.sigmoid(g_ref[...].astype(F32)) * up
    o_ref[...] = hs_ref[...] + jnp.dot(mixed.astype(BF16), wo_ref[...], preferred_element_type=F32)


def _merge(hs, u_att, u_hgrn, proj, conv_w, wb_bf, wo_bf, tp):
    rows = hs.shape[0]
    tm = _row_tile(tp, 384)
    kern = functools.partial(_merge_kernel, tm=tm, tiles_per_seq=tp // tm)
    row_blk = lambda w, blk: pl.BlockSpec((tm, w), lambda i: (i, blk))
    halo = lambda blk: pl.BlockSpec((8, 512), lambda i: (jnp.maximum(i * (tm // 8) - 1, 0), blk))
    const = lambda shape: pl.BlockSpec(shape, lambda i: (0,) * len(shape))
    return pl.pallas_call(
        kern,
        out_shape=jax.ShapeDtypeStruct((rows, D_MODEL), F32),
        grid=(rows // tm,),
        in_specs=[row_blk(D_MODEL, 0), row_blk(512, 0), row_blk(512, 0),
                  row_blk(512, BLK_CB), row_blk(512, BLK_CC), row_blk(512, BLK_CH),
                  halo(BLK_CC), halo(BLK_CH),
                  row_blk(1024, BLK_GATE), row_blk(1024, BLK_GATE + 1), row_blk(1024, BLK_GATE + 2),
                  const((CONV_K, 512)), const((3, 512, D_MODEL)), const((D_MODEL, D_MODEL))],
        out_specs=row_blk(D_MODEL, 0),
        compiler_params=_params(("parallel",)),
        name="merge",
    )(hs, u_att, u_hgrn, proj, proj, proj, proj, proj, proj, proj, proj, conv_w, wb_bf, wo_bf)


def _ffn_kernel(hs_ref, g_ref, wg_ref, wu_ref, wd_ref, o_ref):
    x = hs_ref[...]
    ms = jnp.mean(x * x, axis=-1, keepdims=True)
    hn = (x * lax.rsqrt(ms + EPS) * g_ref[...]).astype(BF16)
    a = jnp.dot(hn, wg_ref[...], preferred_element_type=F32)
    u = jnp.dot(hn, wu_ref[...], preferred_element_type=F32)
    h = (a * jax.nn.sigmoid(a) * u).astype(BF16)
    o_ref[...] = x + jnp.dot(h, wd_ref[...], preferred_element_type=F32)


def _ffn(hs, gain, wg_bf, wu_bf, wd_bf):
    rows = hs.shape[0]
    d_ff = wg_bf.shape[1]
    tm = _row_tile(rows, 768)
    const = lambda shape: pl.BlockSpec(shape, lambda i: (0, 0), pipeline_mode=pl.Buffered(1))
    return pl.pallas_call(
        _ffn_kernel,
        out_shape=jax.ShapeDtypeStruct((rows, D_MODEL), F32),
        grid=(rows // tm,),
        in_specs=[pl.BlockSpec((tm, D_MODEL), lambda i: (i, 0)), const((1, D_MODEL)),
                  const((D_MODEL, d_ff)), const((D_MODEL, d_ff)), const((d_ff, D_MODEL))],
        out_specs=pl.BlockSpec((tm, D_MODEL), lambda i: (i, 0)),
        compiler_params=_params(("parallel",)),
        name="ffn_dense",
    )(hs, gain.reshape(1, D_MODEL), wg_bf, wu_bf, wd_bf)


def _router_kernel(hs_ref, g_ref, rw_ref, idx_ref, wt_ref, cnt_ref, carry_ref, *, tm, tiles_per_seq):
    i = pl.program_id(0)

    @pl.when(i == 0)
    def _():
        carry_ref[...] = jnp.zeros_like(carry_ref)

    x = hs_ref[...]
    ms = jnp.mean(x * x, axis=-1, keepdims=True)
    hn = x * lax.rsqrt(ms + EPS) * g_ref[...]
    hn_hi = hn.astype(BF16)
    hn_lo = (hn - hn_hi.astype(F32)).astype(BF16)
    logits = (jnp.dot(hn_hi, rw_ref[0], preferred_element_type=F32)
              + jnp.dot(hn_lo, rw_ref[0], preferred_element_type=F32)
              + jnp.dot(hn_hi, rw_ref[1], preferred_element_type=F32))
    lane = lax.broadcasted_iota(jnp.int32, logits.shape, 1)
    lane_f = lane.astype(F32)
    logits = jnp.where(lane < N_EXPERTS, logits, -jnp.inf)
    m1 = jnp.max(logits, axis=-1, keepdims=True)
    i1 = jnp.min(jnp.where(logits == m1, lane_f, 128.0), axis=-1, keepdims=True)
    rest = jnp.where(lane_f == i1, -jnp.inf, logits)
    m2 = jnp.max(rest, axis=-1, keepdims=True)
    i2 = jnp.min(jnp.where(rest == m2, lane_f, 128.0), axis=-1, keepdims=True)
    e2 = jnp.exp(m2 - m1)
    w1 = 1.0 / (1.0 + e2)
    w2 = e2 / (1.0 + e2)
    row = (i % tiles_per_seq) * tm + lax.broadcasted_iota(jnp.int32, (tm, 1), 0)
    real = row >= T_PAD
    hot1 = jnp.where(jnp.logical_and(real, lane_f == i1), 1.0, 0.0)
    hot2 = jnp.where(jnp.logical_and(real, lane_f == i2), 1.0, 0.0)
    both = hot1 + hot2
    rr = lax.broadcasted_iota(jnp.int32, (tm, tm), 0)
    cc = lax.broadcasted_iota(jnp.int32, (tm, tm), 1)
    earlier = jnp.where(cc < rr, 1.0, 0.0).astype(BF16)
    before = carry_ref[...] + jnp.dot(earlier, both.astype(BF16), preferred_element_type=F32)
    r1 = jnp.sum(before * hot1, axis=-1, keepdims=True)
    r2 = jnp.sum(before * hot2, axis=-1, keepdims=True)
    carry_ref[...] += jnp.sum(both, axis=0, keepdims=True)
    cnt_ref[...] = carry_ref[...]
    packed = jnp.where(lane == 0, i1, jnp.where(lane == 1, i2, jnp.where(lane == 2, r1, jnp.where(lane == 3, r2, 0.0))))
    idx_ref[...] = packed.astype(jnp.int32)
    wt_ref[...] = jnp.where(lane == 0, w1, jnp.where(lane == 1, w2, 0.0))


def _router(hs, gain, router_w, tp):
    rows = hs.shape[0]
    tm = _row_tile(tp, 384)
    rw = jnp.zeros((D_MODEL, 128), F32).at[:, :N_EXPERTS].set(router_w.astype(F32))
    rw_hi = rw.astype(BF16)
    rw = jnp.stack([rw_hi, (rw - rw_hi.astype(F32)).astype(BF16)])
    blk = pl.BlockSpec((tm, 128), lambda i: (i, 0))
    return pl.pallas_call(
        functools.partial(_router_kernel, tm=tm, tiles_per_seq=tp // tm),
        out_shape=(jax.ShapeDtypeStruct((rows, 128), jnp.int32), jax.ShapeDtypeStruct((rows, 128), F32),
                   jax.ShapeDtypeStruct((1, 128), F32)),
        grid=(rows // tm,),
        in_specs=[pl.BlockSpec((tm, D_MODEL), lambda i: (i, 0)),
                  pl.BlockSpec((1, D_MODEL), lambda i: (0, 0)),
                  pl.BlockSpec((2, D_MODEL, 128), lambda i: (0, 0, 0))],
        out_specs=(blk, blk, pl.BlockSpec((1, 128), lambda i: (0, 0))),
        scratch_shapes=[pltpu.VMEM((1, 128), F32)],
        compiler_params=_params(("arbitrary",)),
        name="moe_router",
    )(hs, gain.reshape(1, D_MODEL), rw)


def _dispatch_kernel(s1_ref, s2_ref, zr_ref, hs_ref, wt_ref, g_ref, xs_ref, rows_ref, zero_ref, sem, zsem,
                     *, td, tg, tiles_per_seq, n_steps):
    n = pl.program_id(0) * tiles_per_seq + pl.program_id(1)
    buf = n % 2
    base = n * td

    def row_copy(r, which):
        slot = (s1_ref, s2_ref)[which][base + r]
        return pltpu.make_async_copy(rows_ref.at[buf, which, pl.ds(r, 1)], xs_ref.at[pl.ds(slot, 1)],
                                     sem.at[buf])

    def wait_rows(b):
        def body(r, c):
            pltpu.make_async_copy(rows_ref.at[b, 0, pl.ds(0, 1)], xs_ref.at[pl.ds(0, 1)], sem.at[b]).wait()
            return c
        lax.fori_loop(0, 2 * td, body, 0)

    @pl.when(n == 0)
    def _():
        zero_ref[...] = jnp.zeros_like(zero_ref)
        for e in range(2 * N_EXPERTS):
            fill = pltpu.make_async_copy(zero_ref, xs_ref.at[pl.ds(pl.multiple_of(zr_ref[e], tg), tg)], zsem)
            fill.start()
            fill.wait()

    @pl.when(n >= 2)
    def _():
        wait_rows(buf)

    x = hs_ref[...]
    ms = jnp.mean(x * x, axis=-1, keepdims=True)
    hn = x * lax.rsqrt(ms + EPS) * g_ref[...]
    wt = wt_ref[...]
    for which in range(2):
        rows_ref[buf, which, :, :D_MODEL] = hn
        rows_ref[buf, which, :, D_MODEL:] = jnp.broadcast_to(wt[:, which:which + 1], (td, XS_EXTRA))

    def issue(r, c):
        row_copy(r, 0).start()
        row_copy(r, 1).start()
        return c

    lax.fori_loop(0, td, issue, 0)

    @pl.when(n == n_steps - 1)
    def _():
        wait_rows(buf)
        if n_steps > 1:
            wait_rows(1 - buf)


def _dispatch(hs3, wts3, gain, slot1, slot2, zero_rows, n_slots, tg):
    bsz, tp, _ = hs3.shape
    td = T_PAD
    tiles_per_seq = (tp - T_PAD) // td
    width = D_MODEL + XS_EXTRA
    kern = functools.partial(_dispatch_kernel, td=td, tg=tg, tiles_per_seq=tiles_per_seq,
                             n_steps=bsz * tiles_per_seq)
    return pl.pallas_call(
        kern,
        out_shape=jax.ShapeDtypeStruct((n_slots, width), F32),
        grid_spec=pltpu.PrefetchScalarGridSpec(
            num_scalar_prefetch=3,
            grid=(bsz, tiles_per_seq),
            in_specs=[pl.BlockSpec((None, td, D_MODEL), lambda b, i, *_: (b, i + 1, 0)),
                      pl.BlockSpec((None, td, 128), lambda b, i, *_: (b, i + 1, 0)),
                      pl.BlockSpec((1, D_MODEL), lambda b, i, *_: (0, 0))],
            out_specs=pl.BlockSpec(memory_space=pl.ANY),
            scratch_shapes=[pltpu.VMEM((2, 2, td, width), F32), pltpu.VMEM((tg, width), F32),
                            pltpu.SemaphoreType.DMA((2,)), pltpu.SemaphoreType.DMA],
        ),
        compiler_params=_params(("arbitrary", "arbitrary")),
        name="moe_dispatch",
    )(slot1, slot2, zero_rows, hs3, wts3, gain.reshape(1, D_MODEL))


def _expert_kernel(te_ref, nt_ref, x_ref, wg_ref, wu_ref, wd_ref, o_ref, hn_ref, acc_ref, *, n_f):
    t = pl.program_id(0)
    f = pl.program_id(1)

    @pl.when(t < nt_ref[0])
    def _():
        @pl.when(f == 0)
        def _():
            hn_ref[...] = x_ref[:, :D_MODEL].astype(BF16)

        hn = hn_ref[...]
        a = jnp.dot(hn, wg_ref[...], preferred_element_type=F32)
        u = jnp.dot(hn, wu_ref[...], preferred_element_type=F32)
        h = (a * jax.nn.sigmoid(a) * u).astype(BF16)

        @pl.when(f == 0)
        def _():
            acc_ref[...] = jnp.dot(h, wd_ref[...], preferred_element_type=F32)

        @pl.when(jnp.logical_and(f > 0, f < n_f - 1))
        def _():
            acc_ref[...] += jnp.dot(h, wd_ref[...], preferred_element_type=F32)

        @pl.when(f == n_f - 1)
        def _():
            y = acc_ref[...] + jnp.dot(h, wd_ref[...], preferred_element_type=F32)
            o_ref[...] = (y * x_ref[:, D_MODEL:D_MODEL + 1]).astype(BF16)

    @pl.when(jnp.logical_and(t >= nt_ref[0], f == n_f - 1))
    def _():
        o_ref[...] = jnp.zeros_like(o_ref)


def _experts(xs, tile_expert, n_tiles_used, wg_bf, wu_bf, wd_bf, tg):
    slots = xs.shape[0]
    n_f = MOE_F_BLOCKS
    tf = wg_bf.shape[2] // n_f

    def x_map(t, f, te, nt):
        return (jnp.maximum(jnp.minimum(t, nt[0] - 1), 0), 0)

    def f_of(t, f, nt):
        return jnp.where(t < nt[0], f, n_f - 1)

    return pl.pallas_call(
        functools.partial(_expert_kernel, n_f=n_f),
        out_shape=jax.ShapeDtypeStruct((slots, D_MODEL), BF16),
        grid_spec=pltpu.PrefetchScalarGridSpec(
            num_scalar_prefetch=2,
            grid=(slots // tg, n_f),
            in_specs=[
                pl.BlockSpec((tg, D_MODEL + XS_EXTRA), x_map),
                pl.BlockSpec((None, D_MODEL, tf), lambda t, f, te, nt: (te[t], 0, f_of(t, f, nt))),
                pl.BlockSpec((None, D_MODEL, tf), lambda t, f, te, nt: (te[t], 0, f_of(t, f, nt))),
                pl.BlockSpec((None, tf, D_MODEL), lambda t, f, te, nt: (te[t], f_of(t, f, nt), 0)),
            ],
            out_specs=pl.BlockSpec((tg, D_MODEL), lambda t, f, te, nt: (t, 0)),
            scratch_shapes=[pltpu.VMEM((tg, D_MODEL), BF16), pltpu.VMEM((tg, D_MODEL), F32)],
        ),
        compiler_params=_params(("arbitrary", "arbitrary")),
        name="moe_experts",
    )(tile_expert, n_tiles_used, xs, wg_bf, wu_bf, wd_bf)


def _combine_kernel(ws_ref, hs_ref, route_ref, y_ref, o_ref, win_ref, sem, *, tc, tiles_per_seq, n_steps):
    n = pl.program_id(0) * tiles_per_seq + pl.program_id(1)
    cur = n % 2

    def window_copy(step, b, e):
        start = pl.multiple_of(ws_ref[step * N_EXPERTS + e], WIN_ALIGN)
        return pltpu.make_async_copy(y_ref.at[pl.ds(start, COMBINE_WIN)],
                                     win_ref.at[b, pl.ds(e * COMBINE_WIN, COMBINE_WIN)], sem.at[b])

    def fetch(step, b):
        for e in range(N_EXPERTS):
            window_copy(step, b, e).start()

    @pl.when(n == 0)
    def _():
        fetch(0, 0)

    @pl.when(n + 1 < n_steps)
    def _():
        fetch(n + 1, 1 - cur)

    for e in range(N_EXPERTS):
        window_copy(n, cur, e).wait()

    route = route_ref[...]
    e1, e2, s1, s2 = (route[:, c:c + 1] for c in range(4))
    start1 = jnp.zeros_like(s1)
    start2 = jnp.zeros_like(s2)
    for e in range(N_EXPERTS):
        start = ws_ref[n * N_EXPERTS + e]
        start1 = jnp.where(e1 == e, start, start1)
        start2 = jnp.where(e2 == e, start, start2)
    col1 = e1 * COMBINE_WIN + s1 - start1
    col2 = e2 * COMBINE_WIN + s2 - start2
    pos = lax.broadcasted_iota(jnp.int32, (1, N_EXPERTS * COMBINE_WIN), 1)
    pick = jnp.where(jnp.logical_or(pos == col1, pos == col2), 1.0, 0.0).astype(BF16)
    o_ref[...] = hs_ref[...] + jnp.dot(pick, win_ref[cur], preferred_element_type=F32)


def _combine(hs3, route, y, win_start, seq):
    bsz = hs3.shape[0]
    tc = COMBINE_TOKENS
    tiles_per_seq = seq // tc
    kern = functools.partial(_combine_kernel, tc=tc, tiles_per_seq=tiles_per_seq, n_steps=bsz * tiles_per_seq)
    tok = lambda w: pl.BlockSpec((tc, w), lambda b, i, ws: (b * tiles_per_seq + i, 0))
    return pl.pallas_call(
        kern,
        out_shape=jax.ShapeDtypeStruct((bsz, seq, D_MODEL), F32),
        grid_spec=pltpu.PrefetchScalarGridSpec(
            num_scalar_prefetch=1,
            grid=(bsz, tiles_per_seq),
            in_specs=[
                pl.BlockSpec((None, tc, D_MODEL), lambda b, i, ws: (b, i + T_PAD // tc, 0)),
                tok(4),
                pl.BlockSpec(memory_space=pl.ANY),
            ],
            out_specs=pl.BlockSpec((None, tc, D_MODEL), lambda b, i, ws: (b, i, 0)),
            scratch_shapes=[pltpu.VMEM((2, N_EXPERTS * COMBINE_WIN, D_MODEL), BF16),
                            pltpu.SemaphoreType.DMA((2,))],
        ),
        compiler_params=_params(("arbitrary", "arbitrary")),
        name="moe_combine",
    )(win_start, hs3, route, y)


def _moe(hs, gain, router_w, wg_bf, wu_bf, wd_bf, bsz, tp):
    seq = tp - T_PAD
    n_tok = bsz * seq
    tg = MOE_TILE
    idx, wts, cnt = _router(hs, gain, router_w, tp)
    sel = idx.reshape(bsz, tp, 128)[:, T_PAD:, :4].reshape(n_tok, 4)
    counts = cnt[0, :N_EXPERTS].astype(jnp.int32)
    padded = ((counts + tg - 1) // tg) * tg
    ends = jnp.cumsum(padded)
    starts = ends - padded
    experts = jnp.arange(N_EXPERTS, dtype=jnp.int32)[None, :]
    slot1 = jnp.sum(jnp.where(sel[:, 0:1] == experts, starts[None, :], 0), axis=1) + sel[:, 2]
    slot2 = jnp.sum(jnp.where(sel[:, 1:2] == experts, starts[None, :], 0), axis=1) + sel[:, 3]
    slot1, slot2 = slot1.astype(jnp.int32), slot2.astype(jnp.int32)
    n_slots = 2 * n_tok + N_EXPERTS * tg
    tile_start = jnp.arange(n_slots // tg, dtype=jnp.int32)[:, None] * tg
    tile_expert = jnp.minimum(jnp.sum((tile_start >= ends[None, :]).astype(jnp.int32), axis=1), N_EXPERTS - 1)
    n_tiles_used = (ends[-1:] // tg).astype(jnp.int32)
    tail = n_slots - tg * (1 + jnp.arange(N_EXPERTS, dtype=jnp.int32))
    zero_rows = jnp.concatenate([jnp.maximum(ends - tg, 0), tail]).astype(jnp.int32)

    tc = COMBINE_TOKENS
    first = jnp.minimum(
        jnp.min(jnp.where(sel[:, 0:1] == experts, slot1[:, None], n_slots).reshape(n_tok // tc, tc, N_EXPERTS), axis=1),
        jnp.min(jnp.where(sel[:, 1:2] == experts, slot2[:, None], n_slots).reshape(n_tok // tc, tc, N_EXPERTS), axis=1))
    win_start = jnp.where(first == n_slots, 0,
                          jnp.minimum(first // WIN_ALIGN * WIN_ALIGN, n_slots - COMBINE_WIN))
    route = jnp.concatenate([sel[:, :2], slot1[:, None], slot2[:, None]], axis=1)

    hs3 = hs.reshape(bsz, tp, D_MODEL)
    xs = _dispatch(hs3, wts.reshape(bsz, tp, 128), gain, slot1, slot2, zero_rows, n_slots, tg)
    y = _experts(xs, tile_expert.astype(jnp.int32), n_tiles_used, wg_bf, wu_bf, wd_bf, tg)
    return _combine(hs3, route, y, win_start.reshape(-1).astype(jnp.int32), seq)


def _permute_qk_cols(w):
    return w.reshape(D_MODEL, 2, ATT_HEADS, ATT_QK_DIM).transpose(0, 2, 1, 3).reshape(D_MODEL, 512)


def kernel(x, meta_tokens, norm1_gain, norm2_gain, w_in, q_norm_gain, k_norm_gain, diff_lambda,
           attn_sub_gain, rel_bias, conv_w, hgrn_lb_logits, hgrn_out_gain, w_branch, w_out,
           ffn_w_gate, ffn_w_up, ffn_w_down, router_w, moe_w_gate, moe_w_up, moe_w_down):
    bsz, seq, _ = x.shape
    depth = w_in.shape[0]
    tp = T_PAD + seq
    assert tp % ATT_TILE == 0 and depth == 2

    head = jnp.concatenate([jnp.zeros((PAD0, D_MODEL), x.dtype), meta_tokens.astype(x.dtype)], axis=0)
    hs = jnp.concatenate([jnp.broadcast_to(head[None], (bsz, T_PAD, D_MODEL)), x], axis=1)
    hs = hs.reshape(bsz * tp, D_MODEL)

    lb_all = jnp.cumsum(jax.nn.softmax(hgrn_lb_logits.astype(F32), axis=0), axis=0)
    lb_all = lb_all - lb_all[0]
    btab = _attn_bias_tables(rel_bias, ATT_TILE)

    out = None
    for layer in range(depth):
        w = w_in[layer]
        w_bf = jnp.concatenate([_permute_qk_cols(w[:, :512]), _permute_qk_cols(w[:, 512:1024]), w[:, 1024:]],
                               axis=1).astype(BF16)
        qk_gain = jnp.concatenate([jnp.tile(q_norm_gain[layer].astype(F32), 8) * (ATT_QK_DIM ** -0.5 * LOG2E),
                                   jnp.tile(k_norm_gain[layer].astype(F32), 8)]).reshape(1, COL_TILE)
        proj = _inproj(hs, norm1_gain[layer], w_bf, qk_gain)
        proj3 = proj.reshape(bsz, tp, IN_COLS)

        lam_init = 0.8 - 0.6 * math.exp(-0.3 * layer)
        lp = diff_lambda[layer].astype(F32)
        lam = jnp.exp(jnp.sum(lp[0] * lp[1])) - jnp.exp(jnp.sum(lp[2] * lp[3])) + lam_init
        u_att = _diff_attention(proj3, lam.reshape(1), attn_sub_gain[layer].astype(F32), btab,
                                1.0 - lam_init)

        lb = lb_all[layer]
        u_hgrn = _hgrn(proj3, jnp.log(lb), jnp.log1p(-lb), hgrn_out_gain[layer].astype(F32))

        hs = _merge(hs, u_att.reshape(bsz * tp, 512), u_hgrn.reshape(bsz * tp, 512), proj,
                    conv_w[layer].astype(F32), w_branch[layer].astype(BF16), w_out[layer].astype(BF16), tp)

        j = layer // 2
        if layer % 2 == 0:
            hs = _ffn(hs, norm2_gain[layer], ffn_w_gate[j].astype(BF16), ffn_w_up[j].astype(BF16),
                      ffn_w_down[j].astype(BF16))
        else:
            out = _moe(hs, norm2_gain[layer], router_w[j], moe_w_gate[j].astype(BF16),
                       moe_w_up[j].astype(BF16), moe_w_down[j].astype(BF16), bsz, tp)
    return out
```

```python
import functools
import math

import numpy as np
import jax
import jax.numpy as jnp
from jax import lax
from jax.experimental import pallas as pl
from jax.experimental.pallas import tpu as pltpu

F32 = jnp.float32
BF16 = jnp.bfloat16

D_MODEL = 1024
N_META = 16
EPS = 1e-6
ATT_HEADS = 4
ATT_QK_DIM = 64
ATT_V_DIM = 128
REL_BUCKETS = 32
REL_MAX_DIST = 128
CONV_K = 3
HGRN_HEADS = 4
HGRN_D = 128
N_EXPERTS = 8
IN_COLS = 8192

T_PAD = 128
PAD0 = T_PAD - N_META
ATT_TILE = 384
VT_ONES = 16
LOG2E = math.log2(math.e)
HGRN_CHUNK = 128
HGRN_SUB = 16
COL_TILE = 1024
GROUP_MEAN_WIDTH = 256
MOE_TILE = 512
MOE_F_BLOCKS = 2
XS_EXTRA = 128
COMBINE_TOKENS = 128
WIN_ALIGN = 16
COMBINE_WIN = COMBINE_TOKENS + WIN_ALIGN
MASK = -1e30
VMEM_LIMIT = 56 * 1024 * 1024

BLK_Q, BLK_K, BLK_V = 0, 4, 8
BLK_RQ, BLK_RF, BLK_RI, BLK_RG = 24, 28, 32, 36
BLK_CB, BLK_CC, BLK_CH = 3, 4, 5
BLK_GATE = 5


def _row_tile(rows, target):
    n = rows // 128
    best = 1
    for d in range(1, n + 1):
        if n % d == 0 and d * 128 <= target:
            best = d
    return best * 128


def _params(sem, vmem=VMEM_LIMIT):
    return pltpu.CompilerParams(dimension_semantics=sem, vmem_limit_bytes=vmem)


def _inproj_kernel(x_ref, g_ref, w_ref, qkg_ref, gm_ref, o_ref, xn_ref):
    j = pl.program_id(1)

    @pl.when(j == 0)
    def _():
        x = x_ref[...]
        ms = jnp.mean(x * x, axis=-1, keepdims=True)
        xn_ref[...] = (x * lax.rsqrt(ms + EPS) * g_ref[...]).astype(BF16)

    acc = jnp.dot(xn_ref[...], w_ref[...], preferred_element_type=F32)

    @pl.when(j == 0)
    def _():
        sq = acc * acc
        hi = sq.astype(BF16)
        lo = (sq - hi.astype(F32)).astype(BF16)
        gm = gm_ref[...]
        width = gm.shape[0]
        ms = jnp.concatenate(
            [jnp.dot(hi[:, c:c + width], gm, preferred_element_type=F32)
             + jnp.dot(lo[:, c:c + width], gm, preferred_element_type=F32)
             for c in range(0, COL_TILE, width)], axis=1)
        o_ref[...] = (acc * lax.rsqrt(ms + EPS) * qkg_ref[...]).astype(BF16)

    @pl.when(j > 0)
    def _():
        o_ref[...] = acc.astype(BF16)


def _inproj(hs, gain, w_bf, qk_gain):
    rows = hs.shape[0]
    tm = _row_tile(rows, 1536)
    n_col = IN_COLS // COL_TILE
    assert COL_TILE == 4 * ATT_HEADS * ATT_QK_DIM
    grp = np.arange(GROUP_MEAN_WIDTH) // ATT_QK_DIM
    gm = jnp.asarray((grp[:, None] == grp[None, :]).astype(np.float32) / ATT_QK_DIM, BF16)
    return pl.pallas_call(
        _inproj_kernel,
        out_shape=jax.ShapeDtypeStruct((rows, IN_COLS), BF16),
        grid=(rows // tm, n_col),
        in_specs=[
            pl.BlockSpec((tm, D_MODEL), lambda i, j: (i, 0)),
            pl.BlockSpec((1, D_MODEL), lambda i, j: (0, 0)),
            pl.BlockSpec((D_MODEL, COL_TILE), lambda i, j: (0, j)),
            pl.BlockSpec((1, COL_TILE), lambda i, j: (0, 0)),
            pl.BlockSpec((GROUP_MEAN_WIDTH, GROUP_MEAN_WIDTH), lambda i, j: (0, 0)),
        ],
        out_specs=pl.BlockSpec((tm, COL_TILE), lambda i, j: (i, j)),
        scratch_shapes=[pltpu.VMEM((tm, D_MODEL), BF16)],
        compiler_params=_params(("parallel", "arbitrary")),
        name="inproj",
    )(hs, gain.reshape(1, D_MODEL), w_bf, qk_gain, gm)


def _rel_bucket_table(n_max):
    n = np.arange(n_max, dtype=np.int64)
    max_exact = REL_BUCKETS // 2
    nf = np.maximum(n, 1).astype(np.float32)
    large = max_exact + (np.log(nf / np.float32(max_exact)) / np.float32(math.log(REL_MAX_DIST / max_exact))
                         * np.float32(REL_BUCKETS - max_exact)).astype(np.int32)
    large = np.minimum(large, REL_BUCKETS - 1)
    return np.where(n < max_exact, n, large).astype(np.int32)


def _attn_bias_tables(rel_bias, t):
    bucket = _rel_bucket_table(2 * t)
    assert np.all(bucket[t + 1:] == REL_BUCKETS - 1) and np.all(np.diff(bucket) >= 0)
    first_dist = tuple(int(np.searchsorted(bucket, b, side="left")) for b in range(REL_BUCKETS))
    return pl.pallas_call(
        functools.partial(_bias_kernel, t=t, first_dist=first_dist),
        out_shape=jax.ShapeDtypeStruct((ATT_HEADS, 6, t, t), F32),
        grid=(ATT_HEADS,),
        in_specs=[pl.BlockSpec(memory_space=pltpu.SMEM)],
        out_specs=pl.BlockSpec((None, 6, t, t), lambda h: (h, 0, 0, 0)),
        compiler_params=_params(("parallel",)),
        name="attn_bias",
    )(rel_bias.astype(F32))


def _bias_kernel(rb_ref, o_ref, *, t, first_dist):
    h = pl.program_id(0)
    key = lax.broadcasted_iota(jnp.int32, (t, t), 0)
    qry = lax.broadcasted_iota(jnp.int32, (t, t), 1)
    far = rb_ref[REL_BUCKETS - 1, h]

    def table(n):
        val = jnp.full((t, t), rb_ref[0, h] - far, F32)
        for b in range(1, REL_BUCKETS):
            val = jnp.where(n >= first_dist[b], rb_ref[b, h] - far, val)
        return val * LOG2E

    n0 = qry - key
    diag = jnp.where(n0 >= 0, table(n0), MASK)
    near = table(n0 + t)
    zero = jnp.zeros((t, t), F32)
    for kind, tab in enumerate((diag, near, zero)):
        o_ref[kind] = tab
        o_ref[kind + 3] = jnp.where(key < PAD0, MASK, tab)


def _attn_kernel(lam_ref, q_ref, k_ref, v_ref, bt_ref, sg_ref, o_ref, m_ref, acc_ref, vt_ref,
                 *, t, out_scale):
    n_t = vt_ref.shape[0]
    for j in range(n_t):
        vt_ref[j, :ATT_V_DIM, :] = v_ref[j * t:(j + 1) * t, :].astype(F32).T.astype(BF16)
        vt_ref[j, ATT_V_DIM:, :] = jnp.ones((VT_ONES, t), BF16)
    lane = lax.broadcasted_iota(jnp.int32, (1, 2 * ATT_QK_DIM), 1)
    nt = (((1,), (1,)), ((), ()))

    def tile_rows(i):
        return pl.ds(pl.multiple_of(i * t, t), t)

    def scores(i, j):
        q = q_ref[tile_rows(i), :]
        zero = jnp.zeros_like(q)
        q_cat = jnp.concatenate([jnp.where(lane < ATT_QK_DIM, q, zero), jnp.where(lane >= ATT_QK_DIM, q, zero)],
                                axis=0)
        bias = bt_ref[jnp.minimum(i - j, 2) + jnp.where(j == 0, 3, 0)]
        s = (lax.dot_general(k_ref[tile_rows(j), :], q_cat, nt, preferred_element_type=F32)
             + jnp.concatenate([bias, bias], axis=1))
        return s, jnp.max(s, axis=0, keepdims=True)

    def consume(j, scored):
        s, s_max = scored
        m_old = m_ref[...]
        m_new = jnp.maximum(m_old, s_max)
        m_ref[...] = m_new
        p = jnp.exp2(s - m_new).astype(BF16)
        acc_ref[...] = (jnp.exp2(m_old - m_new) * acc_ref[...]
                        + jnp.dot(vt_ref[j], p, preferred_element_type=F32))

    def reset():
        m_ref[...] = jnp.full(m_ref.shape, MASK, F32)
        acc_ref[...] = jnp.zeros(acc_ref.shape, F32)

    def finish(i):
        acc = acc_ref[...]
        a1, a2 = acc[:, :t], acc[:, t:]
        o_t = (a1[:ATT_V_DIM] / a1[ATT_V_DIM:ATT_V_DIM + 1]
               - lam_ref[0] * (a2[:ATT_V_DIM] / a2[ATT_V_DIM:ATT_V_DIM + 1]))
        o = o_t.T
        ms = jnp.mean(o * o, axis=-1, keepdims=True)
        y = o * lax.rsqrt(ms + EPS) * (sg_ref[...] * out_scale)
        row = i * t + lax.broadcasted_iota(jnp.int32, (t, 1), 0)
        o_ref[tile_rows(i), :] = jnp.where(row >= PAD0, y, 0.0).astype(BF16)
        reset()

    def step(_, carry):
        i, j, s = carry
        last = j == i
        ni = jnp.where(last, i + 1, i)
        nj = jnp.where(last, 0, j + 1)
        nxt = scores(jnp.minimum(ni, n_t - 1), nj)
        consume(j, s)

        @pl.when(last)
        def _():
            finish(i)

        return ni, nj, nxt

    reset()
    first = jnp.int32(0)
    lax.fori_loop(0, n_t * (n_t + 1) // 2, step, (first, first, scores(first, first)))


def _diff_attention(proj3, lam, sub_gain, btab, out_scale):
    bsz, tp, _ = proj3.shape
    t = ATT_TILE
    n_t = tp // t
    rows = ATT_V_DIM + VT_ONES
    kern = functools.partial(_attn_kernel, t=t, out_scale=out_scale)
    seq = lambda blk: pl.BlockSpec((None, tp, 128), lambda b, h: (b, 0, blk + h))
    return pl.pallas_call(
        kern,
        out_shape=jax.ShapeDtypeStruct((bsz, tp, ATT_HEADS * ATT_V_DIM), BF16),
        grid=(bsz, ATT_HEADS),
        in_specs=[
            pl.BlockSpec(memory_space=pltpu.SMEM),
            seq(BLK_Q), seq(BLK_K), seq(BLK_V),
            pl.BlockSpec((None, 6, t, t), lambda b, h: (h, 0, 0, 0)),
            pl.BlockSpec((1, ATT_V_DIM), lambda b, h: (0, 0)),
        ],
        out_specs=pl.BlockSpec((None, tp, 128), lambda b, h: (b, 0, h)),
        scratch_shapes=[pltpu.VMEM((1, 2 * t), F32), pltpu.VMEM((rows, 2 * t), F32),
                        pltpu.VMEM((n_t, rows, t), BF16)],
        compiler_params=_params(("parallel", "parallel")),
        name="diff_attn",
    )(lam, proj3, proj3, proj3, btab, sub_gain.reshape(1, ATT_V_DIM))


def _split3(x):
    h1 = x.astype(BF16)
    r1 = x - h1.astype(F32)
    h2 = r1.astype(BF16)
    h3 = (r1 - h2.astype(F32)).astype(BF16)
    return h1, h2, h3


def _hgrn_kernel(q_ref, f_ref, i_ref, g_ref, la_ref, l1m_ref, og_ref, o_ref):
    c_len, sub = HGRN_CHUNK, HGRN_SUB
    half = sub // 2
    n_chunk = q_ref.shape[0] // c_len
    la, l1m, og = la_ref[...], l1m_ref[...], og_ref[...]
    rr = lax.broadcasted_iota(jnp.int32, (c_len, c_len), 0)
    cc = lax.broadcasted_iota(jnp.int32, (c_len, c_len), 1)
    tri = jnp.where(cc <= rr, 1.0, 0.0).astype(BF16)
    row8 = lax.broadcasted_iota(jnp.int32, (half, 1), 0)
    lane8 = lax.broadcasted_iota(jnp.int32, (half, c_len), 1)
    nt = (((1,), (1,)), ((), ()))

    def chunk(c, st):
        r0 = pl.multiple_of(c * c_len, c_len)
        rows = pl.ds(r0, c_len)
        z = f_ref[rows, :].astype(F32)
        qh = q_ref[rows, :].astype(F32)
        qh = qh * jax.nn.sigmoid(qh)
        v = i_ref[rows, :]
        gate = g_ref[rows, :].astype(F32)
        sp = jnp.log(1.0 + jnp.exp(-jnp.abs(z)))
        bb = l1m + jnp.minimum(z, 0.0) - sp
        log_f = jnp.maximum(la, bb) + jnp.log(1.0 + jnp.exp(-jnp.abs(la - bb)))
        valid = (r0 + lax.broadcasted_iota(jnp.int32, (c_len, 1), 0)) >= PAD0
        log_k = jnp.where(valid, l1m + jnp.minimum(-z, 0.0) - sp, -jnp.inf)
        g = sum(jnp.dot(tri, part, preferred_element_type=F32) for part in _split3(log_f))
        ck = log_k - g
        o_inter = lax.dot_general((qh * jnp.exp(g)).astype(BF16), st.astype(BF16), nt,
                                  preferred_element_type=F32)
        a_rows = []
        for a in range(c_len // sub):
            lo = a * sub
            ga = (g[lo:lo + half, :], g[lo + half:lo + sub, :])
            qa = (qh[lo:lo + half, :], qh[lo + half:lo + sub, :])
            if a == 0:
                blk = [jnp.zeros((half, c_len), F32)] * 2
            else:
                gs = g[lo - 1:lo, :]
                qd = (qh[lo:lo + sub, :] * jnp.exp(g[lo:lo + sub, :] - gs)).astype(BF16)
                kd = jnp.exp(jnp.minimum(gs - g[:lo, :], 0.0) + log_k[:lo, :]).astype(BF16)
                kd = jnp.concatenate([kd, jnp.zeros((c_len - lo, HGRN_D), BF16)], axis=0)
                a_off = lax.dot_general(qd, kd, nt, preferred_element_type=F32)
                blk = [a_off[:half, :], a_off[half:, :]]
            for s in range(sub):
                crow = ck[lo + s:lo + s + 1, :]
                for hh in range(s // half, 2):
                    col = jnp.sum(qa[hh] * jnp.exp(ga[hh] + crow), axis=-1, keepdims=True)
                    blk[hh] = jnp.where(lane8 == lo + s, col, blk[hh])
            for hh in range(2):
                a_rows.append(jnp.where(lane8 <= lo + hh * half + row8, blk[hh], 0.0))
        a_full = jnp.concatenate(a_rows, axis=0).astype(BF16)
        o = o_inter + jnp.dot(a_full, v, preferred_element_type=F32)
        g_last = g[c_len - 1:c_len, :]
        kd = jnp.exp(g_last - g + log_k).astype(BF16)
        st = st * jnp.exp(g_last) + lax.dot_general(v, kd, (((0,), (0,)), ((), ())),
                                                    preferred_element_type=F32)
        ms = jnp.mean(o * o, axis=-1, keepdims=True)
        y = o * lax.rsqrt(ms + EPS) * og * (gate * jax.nn.sigmoid(gate))
        o_ref[rows, :] = y.astype(BF16)
        return st

    lax.fori_loop(0, n_chunk, chunk, jnp.zeros((HGRN_D, HGRN_D), F32), unroll=3)


def _hgrn(proj3, log_lb, log1m_lb, out_gain):
    bsz, tp, _ = proj3.shape
    seq = lambda blk: pl.BlockSpec((None, tp, 128), lambda b, h: (b, 0, blk + h))
    chan = pl.BlockSpec((None, 1, HGRN_D), lambda b, h: (h, 0, 0))
    return pl.pallas_call(
        _hgrn_kernel,
        out_shape=jax.ShapeDtypeStruct((bsz, tp, HGRN_HEADS * HGRN_D), BF16),
        grid=(bsz, HGRN_HEADS),
        in_specs=[seq(BLK_RQ), seq(BLK_RF), seq(BLK_RI), seq(BLK_RG), chan, chan,
                  pl.BlockSpec((1, HGRN_D), lambda b, h: (0, 0))],
        out_specs=pl.BlockSpec((None, tp, 128), lambda b, h: (b, 0, h)),
        compiler_params=_params(("parallel", "parallel")),
        name="hgrn2",
    )(proj3, proj3, proj3, proj3,
      log_lb.reshape(HGRN_HEADS, 1, HGRN_D), log1m_lb.reshape(HGRN_HEADS, 1, HGRN_D),
      out_gain.reshape(1, HGRN_D))


def _merge_kernel(hs_ref, ua_ref, ur_ref, cb_ref, cc_ref, ch_ref, pc_ref, ph_ref,
                  g0_ref, g1_ref, g2_ref, cw_ref, wb_ref, wo_ref, o_ref, *, tm, tiles_per_seq):
    i = pl.program_id(0)
    row = (i % tiles_per_seq) * tm + lax.broadcasted_iota(jnp.int32, (tm, 1), 0)
    valid = row >= PAD0
    z = jnp.where(valid, cc_ref[...].astype(F32) * ch_ref[...].astype(F32), 0.0)
    halo_row = (i % tiles_per_seq) * tm - 8 + lax.broadcasted_iota(jnp.int32, (8, 1), 0)
    zp = jnp.where(halo_row >= PAD0, pc_ref[...].astype(F32) * ph_ref[...].astype(F32), 0.0)
    zz = jnp.concatenate([zp, z], axis=0)
    cw = cw_ref[...]
    y = (cw[2:3, :] * z + cw[1:2, :] * zz[7:7 + tm, :] + cw[0:1, :] * zz[6:6 + tm, :])
    u_conv = jnp.where(valid, cb_ref[...].astype(F32) * y, 0.0).astype(BF16)
    mixed = jnp.zeros((tm, D_MODEL), F32)
    for n, (u, g_ref) in enumerate(((ua_ref[...], g0_ref), (u_conv, g1_ref), (ur_ref[...], g2_ref))):
        up = jnp.dot(u, wb_ref[n], preferred_element_type=F32)
        mixed = mixed + jax.nn.sigmoid(g_ref[...].astype(F32)) * up
    o_ref[...] = hs_ref[...] + jnp.dot(mixed.astype(BF16), wo_ref[...], preferred_element_type=F32)


def _merge(hs, u_att, u_hgrn, proj, conv_w, wb_bf, wo_bf, tp):
    rows = hs.shape[0]
    tm = _row_tile(tp, 384)
    kern = functools.partial(_merge_kernel, tm=tm, tiles_per_seq=tp // tm)
    row_blk = lambda w, blk: pl.BlockSpec((tm, w), lambda i: (i, blk))
    halo = lambda blk: pl.BlockSpec((8, 512), lambda i: (jnp.maximum(i * (tm // 8) - 1, 0), blk))
    const = lambda shape: pl.BlockSpec(shape, lambda i: (0,) * len(shape))
    return pl.pallas_call(
        kern,
        out_shape=jax.ShapeDtypeStruct((rows, D_MODEL), F32),
        grid=(rows // tm,),
        in_specs=[row_blk(D_MODEL, 0), row_blk(512, 0), row_blk(512, 0),
                  row_blk(512, BLK_CB), row_blk(512, BLK_CC), row_blk(512, BLK_CH),
                  halo(BLK_CC), halo(BLK_CH),
                  row_blk(1024, BLK_GATE), row_blk(1024, BLK_GATE + 1), row_blk(1024, BLK_GATE + 2),
                  const((CONV_K, 512)), const((3, 512, D_MODEL)), const((D_MODEL, D_MODEL))],
        out_specs=row_blk(D_MODEL, 0),
        compiler_params=_params(("parallel",)),
        name="merge",
    )(hs, u_att, u_hgrn, proj, proj, proj, proj, proj, proj, proj, proj, conv_w, wb_bf, wo_bf)


def _ffn_kernel(hs_ref, g_ref, wg_ref, wu_ref, wd_ref, o_ref):
    x = hs_ref[...]
    ms = jnp.mean(x * x, axis=-1, keepdims=True)
    hn = (x * lax.rsqrt(ms + EPS) * g_ref[...]).astype(BF16)
    a = jnp.dot(hn, wg_ref[...], preferred_element_type=F32)
    u = jnp.dot(hn, wu_ref[...], preferred_element_type=F32)
    h = (a * jax.nn.sigmoid(a) * u).astype(BF16)
    o_ref[...] = x + jnp.dot(h, wd_ref[...], preferred_element_type=F32)


def _ffn(hs, gain, wg_bf, wu_bf, wd_bf):
    rows = hs.shape[0]
    d_ff = wg_bf.shape[1]
    tm = _row_tile(rows, 768)
    const = lambda shape: pl.BlockSpec(shape, lambda i: (0, 0), pipeline_mode=pl.Buffered(1))
    return pl.pallas_call(
        _ffn_kernel,
        out_shape=jax.ShapeDtypeStruct((rows, D_MODEL), F32),
        grid=(rows // tm,),
        in_specs=[pl.BlockSpec((tm, D_MODEL), lambda i: (i, 0)), const((1, D_MODEL)),
                  const((D_MODEL, d_ff)), const((D_MODEL, d_ff)), const((d_ff, D_MODEL))],
        out_specs=pl.BlockSpec((tm, D_MODEL), lambda i: (i, 0)),
        compiler_params=_params(("parallel",)),
        name="ffn_dense",
    )(hs, gain.reshape(1, D_MODEL), wg_bf, wu_bf, wd_bf)


def _router_kernel(hs_ref, g_ref, rw_ref, idx_ref, wt_ref, cnt_ref, carry_ref, *, tm, tiles_per_seq):
    i = pl.program_id(0)

    @pl.when(i == 0)
    def _():
        carry_ref[...] = jnp.zeros_like(carry_ref)

    x = hs_ref[...]
    ms = jnp.mean(x * x, axis=-1, keepdims=True)
    hn = x * lax.rsqrt(ms + EPS) * g_ref[...]
    hn_hi = hn.astype(BF16)
    hn_lo = (hn - hn_hi.astype(F32)).astype(BF16)
    logits = (jnp.dot(hn_hi, rw_ref[0], preferred_element_type=F32)
              + jnp.dot(hn_lo, rw_ref[0], preferred_element_type=F32)
              + jnp.dot(hn_hi, rw_ref[1], preferred_element_type=F32))
    lane = lax.broadcasted_iota(jnp.int32, logits.shape, 1)
    lane_f = lane.astype(F32)
    logits = jnp.where(lane < N_EXPERTS, logits, -jnp.inf)
    m1 = jnp.max(logits, axis=-1, keepdims=True)
    i1 = jnp.min(jnp.where(logits == m1, lane_f, 128.0), axis=-1, keepdims=True)
    rest = jnp.where(lane_f == i1, -jnp.inf, logits)
    m2 = jnp.max(rest, axis=-1, keepdims=True)
    i2 = jnp.min(jnp.where(rest == m2, lane_f, 128.0), axis=-1, keepdims=True)
    e2 = jnp.exp(m2 - m1)
    w1 = 1.0 / (1.0 + e2)
    w2 = e2 / (1.0 + e2)
    row = (i % tiles_per_seq) * tm + lax.broadcasted_iota(jnp.int32, (tm, 1), 0)
    real = row >= T_PAD
    hot1 = jnp.where(jnp.logical_and(real, lane_f == i1), 1.0, 0.0)
    hot2 = jnp.where(jnp.logical_and(real, lane_f == i2), 1.0, 0.0)
    both = hot1 + hot2
    rr = lax.broadcasted_iota(jnp.int32, (tm, tm), 0)
    cc = lax.broadcasted_iota(jnp.int32, (tm, tm), 1)
    earlier = jnp.where(cc < rr, 1.0, 0.0).astype(BF16)
    before = carry_ref[...] + jnp.dot(earlier, both.astype(BF16), preferred_element_type=F32)
    r1 = jnp.sum(before * hot1, axis=-1, keepdims=True)
    r2 = jnp.sum(before * hot2, axis=-1, keepdims=True)
    carry_ref[...] += jnp.sum(both, axis=0, keepdims=True)
    cnt_ref[...] = carry_ref[...]
    packed = jnp.where(lane == 0, i1, jnp.where(lane == 1, i2, jnp.where(lane == 2, r1, jnp.where(lane == 3, r2, 0.0))))
    idx_ref[...] = packed.astype(jnp.int32)
    wt_ref[...] = jnp.where(lane == 0, w1, jnp.where(lane == 1, w2, 0.0))


def _router(hs, gain, router_w, tp):
    rows = hs.shape[0]
    tm = _row_tile(tp, 384)
    rw = jnp.zeros((D_MODEL, 128), F32).at[:, :N_EXPERTS].set(router_w.astype(F32))
    rw_hi = rw.astype(BF16)
    rw = jnp.stack([rw_hi, (rw - rw_hi.astype(F32)).astype(BF16)])
    blk = pl.BlockSpec((tm, 128), lambda i: (i, 0))
    return pl.pallas_call(
        functools.partial(_router_kernel, tm=tm, tiles_per_seq=tp // tm),
        out_shape=(jax.ShapeDtypeStruct((rows, 128), jnp.int32), jax.ShapeDtypeStruct((rows, 128), F32),
                   jax.ShapeDtypeStruct((1, 128), F32)),
        grid=(rows // tm,),
        in_specs=[pl.BlockSpec((tm, D_MODEL), lambda i: (i, 0)),
                  pl.BlockSpec((1, D_MODEL), lambda i: (0, 0)),
                  pl.BlockSpec((2, D_MODEL, 128), lambda i: (0, 0, 0))],
        out_specs=(blk, blk, pl.BlockSpec((1, 128), lambda i: (0, 0))),
        scratch_shapes=[pltpu.VMEM((1, 128), F32)],
        compiler_params=_params(("arbitrary",)),
        name="moe_router",
    )(hs, gain.reshape(1, D_MODEL), rw)


def _dispatch_kernel(s1_ref, s2_ref, zr_ref, hs_ref, wt_ref, g_ref, wg_ref, wu_ref, wd_ref,
                     xs_ref, wg_out, wu_out, wd_out, rows_ref, zero_ref, sem, zsem,
                     *, td, tg, tiles_per_seq, n_steps):
    for src, dst in ((wg_ref, wg_out), (wu_ref, wu_out), (wd_ref, wd_out)):
        dst[...] = src[...].astype(BF16)
    n = pl.program_id(0) * tiles_per_seq + pl.program_id(1)
    buf = n % 2
    base = n * td

    def row_copy(r, which):
        slot = (s1_ref, s2_ref)[which][base + r]
        return pltpu.make_async_copy(rows_ref.at[buf, which, pl.ds(r, 1)], xs_ref.at[pl.ds(slot, 1)],
                                     sem.at[buf])

    def wait_rows(b):
        def body(r, c):
            pltpu.make_async_copy(rows_ref.at[b, 0, pl.ds(0, 1)], xs_ref.at[pl.ds(0, 1)], sem.at[b]).wait()
            return c
        lax.fori_loop(0, 2 * td, body, 0)

    @pl.when(n == 0)
    def _():
        zero_ref[...] = jnp.zeros_like(zero_ref)
        for e in range(2 * N_EXPERTS):
            fill = pltpu.make_async_copy(zero_ref, xs_ref.at[pl.ds(pl.multiple_of(zr_ref[e], tg), tg)], zsem)
            fill.start()
            fill.wait()

    @pl.when(n >= 2)
    def _():
        wait_rows(buf)

    x = hs_ref[...]
    ms = jnp.mean(x * x, axis=-1, keepdims=True)
    hn = x * lax.rsqrt(ms + EPS) * g_ref[...]
    wt = wt_ref[...]
    for which in range(2):
        rows_ref[buf, which, :, :D_MODEL] = hn
        rows_ref[buf, which, :, D_MODEL:] = jnp.broadcast_to(wt[:, which:which + 1], (td, XS_EXTRA))

    def issue(r, c):
        row_copy(r, 0).start()
        row_copy(r, 1).start()
        return c

    lax.fori_loop(0, td, issue, 0)

    @pl.when(n == n_steps - 1)
    def _():
        wait_rows(buf)
        if n_steps > 1:
            wait_rows(1 - buf)


def _dispatch(hs3, wts3, gain, slot1, slot2, zero_rows, n_slots, tg, expert_weights):
    bsz, tp, _ = hs3.shape
    td = T_PAD
    tiles_per_seq = (tp - T_PAD) // td
    n_steps = bsz * tiles_per_seq
    width = D_MODEL + XS_EXTRA
    kern = functools.partial(_dispatch_kernel, td=td, tg=tg, tiles_per_seq=tiles_per_seq, n_steps=n_steps)
    flat = [w.reshape(-1, w.shape[-1]) for w in expert_weights]
    assert all(w.shape[0] % (n_steps * WIN_ALIGN) == 0 for w in flat)
    w_spec = lambda w: pl.BlockSpec((w.shape[0] // n_steps, w.shape[1]),
                                    lambda b, i, *_: (b * tiles_per_seq + i, 0))
    out = pl.pallas_call(
        kern,
        out_shape=[jax.ShapeDtypeStruct((n_slots, width), F32)]
        + [jax.ShapeDtypeStruct(w.shape, BF16) for w in flat],
        grid_spec=pltpu.PrefetchScalarGridSpec(
            num_scalar_prefetch=3,
            grid=(bsz, tiles_per_seq),
            in_specs=[pl.BlockSpec((None, td, D_MODEL), lambda b, i, *_: (b, i + 1, 0)),
                      pl.BlockSpec((None, td, 128), lambda b, i, *_: (b, i + 1, 0)),
                      pl.BlockSpec((1, D_MODEL), lambda b, i, *_: (0, 0))] + [w_spec(w) for w in flat],
            out_specs=[pl.BlockSpec(memory_space=pl.ANY)] + [w_spec(w) for w in flat],
            scratch_shapes=[pltpu.VMEM((2, 2, td, width), F32), pltpu.VMEM((tg, width), F32),
                            pltpu.SemaphoreType.DMA((2,)), pltpu.SemaphoreType.DMA],
        ),
        compiler_params=_params(("arbitrary", "arbitrary")),
        name="moe_dispatch",
    )(slot1, slot2, zero_rows, hs3, wts3, gain.reshape(1, D_MODEL), *flat)
    return out[0], [o.reshape(w.shape) for o, w in zip(out[1:], expert_weights)]


def _expert_kernel(te_ref, nt_ref, x_ref, wg_ref, wu_ref, wd_ref, o_ref, hn_ref, acc_ref, *, n_f):
    t = pl.program_id(0)
    f = pl.program_id(1)

    @pl.when(t < nt_ref[0])
    def _():
        @pl.when(f == 0)
        def _():
            hn_ref[...] = x_ref[:, :D_MODEL].astype(BF16)

        hn = hn_ref[...]
        a = jnp.dot(hn, wg_ref[...], preferred_element_type=F32)
        u = jnp.dot(hn, wu_ref[...], preferred_element_type=F32)
        h = (a * jax.nn.sigmoid(a) * u).astype(BF16)

        @pl.when(f == 0)
        def _():
            acc_ref[...] = jnp.dot(h, wd_ref[...], preferred_element_type=F32)

        @pl.when(jnp.logical_and(f > 0, f < n_f - 1))
        def _():
            acc_ref[...] += jnp.dot(h, wd_ref[...], preferred_element_type=F32)

        @pl.when(f == n_f - 1)
        def _():
            y = acc_ref[...] + jnp.dot(h, wd_ref[...], preferred_element_type=F32)
            o_ref[...] = (y * x_ref[:, D_MODEL:D_MODEL + 1]).astype(BF16)

    @pl.when(jnp.logical_and(t >= nt_ref[0], f == n_f - 1))
    def _():
        o_ref[...] = jnp.zeros_like(o_ref)


def _experts(xs, tile_expert, n_tiles_used, wg_bf, wu_bf, wd_bf, tg):
    slots = xs.shape[0]
    n_f = MOE_F_BLOCKS
    tf = wg_bf.shape[2] // n_f

    def x_map(t, f, te, nt):
        return (jnp.maximum(jnp.minimum(t, nt[0] - 1), 0), 0)

    def f_of(t, f, nt):
        return jnp.where(t < nt[0], f, n_f - 1)

    return pl.pallas_call(
        functools.partial(_expert_kernel, n_f=n_f),
        out_shape=jax.ShapeDtypeStruct((slots, D_MODEL), BF16),
        grid_spec=pltpu.PrefetchScalarGridSpec(
            num_scalar_prefetch=2,
            grid=(slots // tg, n_f),
            in_specs=[
                pl.BlockSpec((tg, D_MODEL + XS_EXTRA), x_map),
                pl.BlockSpec((None, D_MODEL, tf), lambda t, f, te, nt: (te[t], 0, f_of(t, f, nt))),
                pl.BlockSpec((None, D_MODEL, tf), lambda t, f, te, nt: (te[t], 0, f_of(t, f, nt))),
                pl.BlockSpec((None, tf, D_MODEL), lambda t, f, te, nt: (te[t], f_of(t, f, nt), 0)),
            ],
            out_specs=pl.BlockSpec((tg, D_MODEL), lambda t, f, te, nt: (t, 0)),
            scratch_shapes=[pltpu.VMEM((tg, D_MODEL), BF16), pltpu.VMEM((tg, D_MODEL), F32)],
        ),
        compiler_params=_params(("arbitrary", "arbitrary")),
        name="moe_experts",
    )(tile_expert, n_tiles_used, xs, wg_bf, wu_bf, wd_bf)


def _combine_kernel(ws_ref, hs_ref, route_ref, y_ref, o_ref, win_ref, sem, *, tc, tiles_per_seq, n_steps):
    n = pl.program_id(0) * tiles_per_seq + pl.program_id(1)
    cur = n % 2

    def window_copy(step, b, e):
        start = pl.multiple_of(ws_ref[step * N_EXPERTS + e], WIN_ALIGN)
        return pltpu.make_async_copy(y_ref.at[pl.ds(start, COMBINE_WIN)],
                                     win_ref.at[b, pl.ds(e * COMBINE_WIN, COMBINE_WIN)], sem.at[b])

    def fetch(step, b):
        for e in range(N_EXPERTS):
            window_copy(step, b, e).start()

    @pl.when(n == 0)
    def _():
        fetch(0, 0)

    @pl.when(n + 1 < n_steps)
    def _():
        fetch(n + 1, 1 - cur)

    for e in range(N_EXPERTS):
        window_copy(n, cur, e).wait()

    route = route_ref[...]
    e1, e2, s1, s2 = (route[:, c:c + 1] for c in range(4))
    start1 = jnp.zeros_like(s1)
    start2 = jnp.zeros_like(s2)
    for e in range(N_EXPERTS):
        start = ws_ref[n * N_EXPERTS + e]
        start1 = jnp.where(e1 == e, start, start1)
        start2 = jnp.where(e2 == e, start, start2)
    col1 = e1 * COMBINE_WIN + s1 - start1
    col2 = e2 * COMBINE_WIN + s2 - start2
    pos = lax.broadcasted_iota(jnp.int32, (1, N_EXPERTS * COMBINE_WIN), 1)
    pick = jnp.where(jnp.logical_or(pos == col1, pos == col2), 1.0, 0.0).astype(BF16)
    o_ref[...] = hs_ref[...] + jnp.dot(pick, win_ref[cur], preferred_element_type=F32)


def _combine(hs3, route, y, win_start, seq):
    bsz = hs3.shape[0]
    tc = COMBINE_TOKENS
    tiles_per_seq = seq // tc
    kern = functools.partial(_combine_kernel, tc=tc, tiles_per_seq=tiles_per_seq, n_steps=bsz * tiles_per_seq)
    tok = lambda w: pl.BlockSpec((tc, w), lambda b, i, ws: (b * tiles_per_seq + i, 0))
    return pl.pallas_call(
        kern,
        out_shape=jax.ShapeDtypeStruct((bsz, seq, D_MODEL), F32),
        grid_spec=pltpu.PrefetchScalarGridSpec(
            num_scalar_prefetch=1,
            grid=(bsz, tiles_per_seq),
            in_specs=[
                pl.BlockSpec((None, tc, D_MODEL), lambda b, i, ws: (b, i + T_PAD // tc, 0)),
                tok(4),
                pl.BlockSpec(memory_space=pl.ANY),
            ],
            out_specs=pl.BlockSpec((None, tc, D_MODEL), lambda b, i, ws: (b, i, 0)),
            scratch_shapes=[pltpu.VMEM((2, N_EXPERTS * COMBINE_WIN, D_MODEL), BF16),
                            pltpu.SemaphoreType.DMA((2,))],
        ),
        compiler_params=_params(("arbitrary", "arbitrary")),
        name="moe_combine",
    )(win_start, hs3, route, y)


def _moe(hs, gain, router_w, w_gate, w_up, w_down, bsz, tp):
    seq = tp - T_PAD
    n_tok = bsz * seq
    tg = MOE_TILE
    idx, wts, cnt = _router(hs, gain, router_w, tp)
    sel = idx.reshape(bsz, tp, 128)[:, T_PAD:, :4].reshape(n_tok, 4)
    counts = cnt[0, :N_EXPERTS].astype(jnp.int32)
    padded = ((counts + tg - 1) // tg) * tg
    ends = jnp.cumsum(padded)
    starts = ends - padded
    experts = jnp.arange(N_EXPERTS, dtype=jnp.int32)[None, :]
    slot1 = jnp.sum(jnp.where(sel[:, 0:1] == experts, starts[None, :], 0), axis=1) + sel[:, 2]
    slot2 = jnp.sum(jnp.where(sel[:, 1:2] == experts, starts[None, :], 0), axis=1) + sel[:, 3]
    slot1, slot2 = slot1.astype(jnp.int32), slot2.astype(jnp.int32)
    n_slots = 2 * n_tok + N_EXPERTS * tg
    tile_start = jnp.arange(n_slots // tg, dtype=jnp.int32)[:, None] * tg
    tile_expert = jnp.minimum(jnp.sum((tile_start >= ends[None, :]).astype(jnp.int32), axis=1), N_EXPERTS - 1)
    n_tiles_used = (ends[-1:] // tg).astype(jnp.int32)
    tail = n_slots - tg * (1 + jnp.arange(N_EXPERTS, dtype=jnp.int32))
    zero_rows = jnp.concatenate([jnp.maximum(ends - tg, 0), tail]).astype(jnp.int32)

    tc = COMBINE_TOKENS
    first = jnp.minimum(
        jnp.min(jnp.where(sel[:, 0:1] == experts, slot1[:, None], n_slots).reshape(n_tok // tc, tc, N_EXPERTS), axis=1),
        jnp.min(jnp.where(sel[:, 1:2] == experts, slot2[:, None], n_slots).reshape(n_tok // tc, tc, N_EXPERTS), axis=1))
    win_start = jnp.where(first == n_slots, 0,
                          jnp.minimum(first // WIN_ALIGN * WIN_ALIGN, n_slots - COMBINE_WIN))
    route = jnp.concatenate([sel[:, :2], slot1[:, None], slot2[:, None]], axis=1)

    hs3 = hs.reshape(bsz, tp, D_MODEL)
    xs, (wg_bf, wu_bf, wd_bf) = _dispatch(hs3, wts.reshape(bsz, tp, 128), gain, slot1, slot2, zero_rows,
                                          n_slots, tg, (w_gate, w_up, w_down))
    y = _experts(xs, tile_expert.astype(jnp.int32), n_tiles_used, wg_bf, wu_bf, wd_bf, tg)
    return _combine(hs3, route, y, win_start.reshape(-1).astype(jnp.int32), seq)


def _permute_qk_cols(w):
    return w.reshape(D_MODEL, 2, ATT_HEADS, ATT_QK_DIM).transpose(0, 2, 1, 3).reshape(D_MODEL, 512)


def kernel(x, meta_tokens, norm1_gain, norm2_gain, w_in, q_norm_gain, k_norm_gain, diff_lambda,
           attn_sub_gain, rel_bias, conv_w, hgrn_lb_logits, hgrn_out_gain, w_branch, w_out,
           ffn_w_gate, ffn_w_up, ffn_w_down, router_w, moe_w_gate, moe_w_up, moe_w_down):
    bsz, seq, _ = x.shape
    depth = w_in.shape[0]
    tp = T_PAD + seq
    assert tp % ATT_TILE == 0 and depth == 2

    head = jnp.concatenate([jnp.zeros((PAD0, D_MODEL), x.dtype), meta_tokens.astype(x.dtype)], axis=0)
    hs = jnp.concatenate([jnp.broadcast_to(head[None], (bsz, T_PAD, D_MODEL)), x], axis=1)
    hs = hs.reshape(bsz * tp, D_MODEL)

    lb_all = jnp.cumsum(jax.nn.softmax(hgrn_lb_logits.astype(F32), axis=0), axis=0)
    lb_all = lb_all - lb_all[0]
    btab = _attn_bias_tables(rel_bias, ATT_TILE)

    out = None
    for layer in range(depth):
        w = w_in[layer]
        w_bf = jnp.concatenate([_permute_qk_cols(w[:, :512]), _permute_qk_cols(w[:, 512:1024]), w[:, 1024:]],
                               axis=1).astype(BF16)
        qk_gain = jnp.concatenate([jnp.tile(q_norm_gain[layer].astype(F32), 8) * (ATT_QK_DIM ** -0.5 * LOG2E),
                                   jnp.tile(k_norm_gain[layer].astype(F32), 8)]).reshape(1, COL_TILE)
        proj = _inproj(hs, norm1_gain[layer], w_bf, qk_gain)
        proj3 = proj.reshape(bsz, tp, IN_COLS)

        lam_init = 0.8 - 0.6 * math.exp(-0.3 * layer)
        lp = diff_lambda[layer].astype(F32)
        lam = jnp.exp(jnp.sum(lp[0] * lp[1])) - jnp.exp(jnp.sum(lp[2] * lp[3])) + lam_init
        u_att = _diff_attention(proj3, lam.reshape(1), attn_sub_gain[layer].astype(F32), btab,
                                1.0 - lam_init)

        lb = lb_all[layer]
        u_hgrn = _hgrn(proj3, jnp.log(lb), jnp.log1p(-lb), hgrn_out_gain[layer].astype(F32))

        hs = _merge(hs, u_att.reshape(bsz * tp, 512), u_hgrn.reshape(bsz * tp, 512), proj,
                    conv_w[layer].astype(F32), w_branch[layer].astype(BF16), w_out[layer].astype(BF16), tp)

        j = layer // 2
        if layer % 2 == 0:
            hs = _ffn(hs, norm2_gain[layer], ffn_w_gate[j].astype(BF16), ffn_w_up[j].astype(BF16),
                      ffn_w_down[j].astype(BF16))
        else:
            out = _moe(hs, norm2_gain[layer], router_w[j], moe_w_gate[j], moe_w_up[j], moe_w_down[j], bsz, tp)
    return out
```

```python
import functools
import math

import numpy as np
import jax
import jax.numpy as jnp
from jax import lax
from jax.experimental import pallas as pl
from jax.experimental.pallas import tpu as pltpu

F32 = jnp.float32
BF16 = jnp.bfloat16

D_MODEL = 1024
N_META = 16
EPS = 1e-6
ATT_HEADS = 4
ATT_QK_DIM = 64
ATT_V_DIM = 128
REL_BUCKETS = 32
REL_MAX_DIST = 128
CONV_K = 3
HGRN_HEADS = 4
HGRN_D = 128
N_EXPERTS = 8
IN_COLS = 8192

T_PAD = 128
PAD0 = T_PAD - N_META
ATT_TILE = 384
VT_ONES = 16
LOG2E = math.log2(math.e)
HGRN_CHUNK = 128
HGRN_SUB = 16
COL_TILE = 1024
GROUP_MEAN_WIDTH = 256
MOE_TILE = 512
MOE_F_BLOCKS = 2
XS_EXTRA = 128
COMBINE_TOKENS = 128
WIN_ALIGN = 16
COMBINE_WIN = COMBINE_TOKENS + WIN_ALIGN
MASK = -1e30
VMEM_LIMIT = 56 * 1024 * 1024

BLK_Q, BLK_K, BLK_V = 0, 4, 8
BLK_RQ, BLK_RF, BLK_RI, BLK_RG = 24, 28, 32, 36
BLK_CB, BLK_CC, BLK_CH = 3, 4, 5
BLK_GATE = 5


def _row_tile(rows, target):
    n = rows // 128
    best = 1
    for d in range(1, n + 1):
        if n % d == 0 and d * 128 <= target:
            best = d
    return best * 128


def _params(sem, vmem=VMEM_LIMIT):
    return pltpu.CompilerParams(dimension_semantics=sem, vmem_limit_bytes=vmem)


def _inproj_kernel(x_ref, g_ref, w_ref, qkg_ref, gm_ref, o_ref, xn_ref):
    j = pl.program_id(1)

    @pl.when(j == 0)
    def _():
        x = x_ref[...]
        ms = jnp.mean(x * x, axis=-1, keepdims=True)
        xn_ref[...] = (x * lax.rsqrt(ms + EPS) * g_ref[...]).astype(BF16)

    acc = jnp.dot(xn_ref[...], w_ref[...], preferred_element_type=F32)

    @pl.when(j == 0)
    def _():
        sq = acc * acc
        hi = sq.astype(BF16)
        lo = (sq - hi.astype(F32)).astype(BF16)
        gm = gm_ref[...]
        width = gm.shape[0]
        ms = jnp.concatenate(
            [jnp.dot(hi[:, c:c + width], gm, preferred_element_type=F32)
             + jnp.dot(lo[:, c:c + width], gm, preferred_element_type=F32)
             for c in range(0, COL_TILE, width)], axis=1)
        o_ref[...] = (acc * lax.rsqrt(ms + EPS) * qkg_ref[...]).astype(BF16)

    @pl.when(j > 0)
    def _():
        o_ref[...] = acc.astype(BF16)


def _inproj(hs, gain, w_bf, qk_gain):
    rows = hs.shape[0]
    tm = _row_tile(rows, 1536)
    n_col = IN_COLS // COL_TILE
    assert COL_TILE == 4 * ATT_HEADS * ATT_QK_DIM
    grp = np.arange(GROUP_MEAN_WIDTH) // ATT_QK_DIM
    gm = jnp.asarray((grp[:, None] == grp[None, :]).astype(np.float32) / ATT_QK_DIM, BF16)
    return pl.pallas_call(
        _inproj_kernel,
        out_shape=jax.ShapeDtypeStruct((rows, IN_COLS), BF16),
        grid=(rows // tm, n_col),
        in_specs=[
            pl.BlockSpec((tm, D_MODEL), lambda i, j: (i, 0)),
            pl.BlockSpec((1, D_MODEL), lambda i, j: (0, 0)),
            pl.BlockSpec((D_MODEL, COL_TILE), lambda i, j: (0, j)),
            pl.BlockSpec((1, COL_TILE), lambda i, j: (0, 0)),
            pl.BlockSpec((GROUP_MEAN_WIDTH, GROUP_MEAN_WIDTH), lambda i, j: (0, 0)),
        ],
        out_specs=pl.BlockSpec((tm, COL_TILE), lambda i, j: (i, j)),
        scratch_shapes=[pltpu.VMEM((tm, D_MODEL), BF16)],
        compiler_params=_params(("parallel", "arbitrary")),
        name="inproj",
    )(hs, gain.reshape(1, D_MODEL), w_bf, qk_gain, gm)


def _rel_bucket_table(n_max):
    n = np.arange(n_max, dtype=np.int64)
    max_exact = REL_BUCKETS // 2
    nf = np.maximum(n, 1).astype(np.float32)
    large = max_exact + (np.log(nf / np.float32(max_exact)) / np.float32(math.log(REL_MAX_DIST / max_exact))
                         * np.float32(REL_BUCKETS - max_exact)).astype(np.int32)
    large = np.minimum(large, REL_BUCKETS - 1)
    return np.where(n < max_exact, n, large).astype(np.int32)


def _attn_bias_tables(rel_bias, t):
    bucket = _rel_bucket_table(2 * t)
    assert np.all(bucket[t + 1:] == REL_BUCKETS - 1) and np.all(np.diff(bucket) >= 0)
    first_dist = tuple(int(np.searchsorted(bucket, b, side="left")) for b in range(REL_BUCKETS))
    return pl.pallas_call(
        functools.partial(_bias_kernel, t=t, first_dist=first_dist),
        out_shape=jax.ShapeDtypeStruct((ATT_HEADS, 6, t, t), F32),
        grid=(ATT_HEADS,),
        in_specs=[pl.BlockSpec(memory_space=pltpu.SMEM)],
        out_specs=pl.BlockSpec((None, 6, t, t), lambda h: (h, 0, 0, 0)),
        compiler_params=_params(("parallel",)),
        name="attn_bias",
    )(rel_bias.astype(F32))


def _bias_kernel(rb_ref, o_ref, *, t, first_dist):
    h = pl.program_id(0)
    key = lax.broadcasted_iota(jnp.int32, (t, t), 0)
    qry = lax.broadcasted_iota(jnp.int32, (t, t), 1)
    far = rb_ref[REL_BUCKETS - 1, h]

    def table(n):
        val = jnp.full((t, t), rb_ref[0, h] - far, F32)
        for b in range(1, REL_BUCKETS):
            val = jnp.where(n >= first_dist[b], rb_ref[b, h] - far, val)
        return val * LOG2E

    n0 = qry - key
    diag = jnp.where(n0 >= 0, table(n0), MASK)
    near = table(n0 + t)
    zero = jnp.zeros((t, t), F32)
    for kind, tab in enumerate((diag, near, zero)):
        o_ref[kind] = tab
        o_ref[kind + 3] = jnp.where(key < PAD0, MASK, tab)


def _cast_specs(arrays, n_steps, step_of):
    flat = [w.reshape(-1, w.shape[-1]) for w in arrays]
    assert all(w.shape[0] % (n_steps * WIN_ALIGN) == 0 for w in flat)
    specs = [pl.BlockSpec((w.shape[0] // n_steps, w.shape[1]), lambda *idx: (step_of(*idx), 0)) for w in flat]
    return flat, specs, [jax.ShapeDtypeStruct(w.shape, BF16) for w in flat]


def _attn_kernel(lam_ref, q_ref, k_ref, v_ref, bt_ref, sg_ref, *refs, t, out_scale, n_cast):
    cast_in, (o_ref, *cast_out), (m_ref, acc_ref, vt_ref) = refs[:n_cast], refs[n_cast:2 * n_cast + 1], refs[-3:]
    for src, dst in zip(cast_in, cast_out):
        dst[...] = src[...].astype(BF16)
    n_t = vt_ref.shape[0]
    for j in range(n_t):
        vt_ref[j, :ATT_V_DIM, :] = v_ref[j * t:(j + 1) * t, :].astype(F32).T.astype(BF16)
        vt_ref[j, ATT_V_DIM:, :] = jnp.ones((VT_ONES, t), BF16)
    lane = lax.broadcasted_iota(jnp.int32, (1, 2 * ATT_QK_DIM), 1)
    nt = (((1,), (1,)), ((), ()))

    def tile_rows(i):
        return pl.ds(pl.multiple_of(i * t, t), t)

    def scores(i, j):
        q = q_ref[tile_rows(i), :]
        zero = jnp.zeros_like(q)
        q_cat = jnp.concatenate([jnp.where(lane < ATT_QK_DIM, q, zero), jnp.where(lane >= ATT_QK_DIM, q, zero)],
                                axis=0)
        bias = bt_ref[jnp.minimum(i - j, 2) + jnp.where(j == 0, 3, 0)]
        s = (lax.dot_general(k_ref[tile_rows(j), :], q_cat, nt, preferred_element_type=F32)
             + jnp.concatenate([bias, bias], axis=1))
        return s, jnp.max(s, axis=0, keepdims=True)

    def consume(j, scored):
        s, s_max = scored
        m_old = m_ref[...]
        m_new = jnp.maximum(m_old, s_max)
        m_ref[...] = m_new
        p = jnp.exp2(s - m_new).astype(BF16)
        acc_ref[...] = (jnp.exp2(m_old - m_new) * acc_ref[...]
                        + jnp.dot(vt_ref[j], p, preferred_element_type=F32))

    def reset():
        m_ref[...] = jnp.full(m_ref.shape, MASK, F32)
        acc_ref[...] = jnp.zeros(acc_ref.shape, F32)

    def finish(i):
        acc = acc_ref[...]
        a1, a2 = acc[:, :t], acc[:, t:]
        o_t = (a1[:ATT_V_DIM] / a1[ATT_V_DIM:ATT_V_DIM + 1]
               - lam_ref[0] * (a2[:ATT_V_DIM] / a2[ATT_V_DIM:ATT_V_DIM + 1]))
        o = o_t.T
        ms = jnp.mean(o * o, axis=-1, keepdims=True)
        y = o * lax.rsqrt(ms + EPS) * (sg_ref[...] * out_scale)
        row = i * t + lax.broadcasted_iota(jnp.int32, (t, 1), 0)
        o_ref[tile_rows(i), :] = jnp.where(row >= PAD0, y, 0.0).astype(BF16)
        reset()

    def step(_, carry):
        i, j, s = carry
        last = j == i
        ni = jnp.where(last, i + 1, i)
        nj = jnp.where(last, 0, j + 1)
        nxt = scores(jnp.minimum(ni, n_t - 1), nj)
        consume(j, s)

        @pl.when(last)
        def _():
            finish(i)

        return ni, nj, nxt

    reset()
    first = jnp.int32(0)
    lax.fori_loop(0, n_t * (n_t + 1) // 2, step, (first, first, scores(first, first)))


def _diff_attention(proj3, lam, sub_gain, btab, out_scale, casts=()):
    bsz, tp, _ = proj3.shape
    t = ATT_TILE
    n_t = tp // t
    rows = ATT_V_DIM + VT_ONES
    kern = functools.partial(_attn_kernel, t=t, out_scale=out_scale, n_cast=len(casts))
    seq = lambda blk: pl.BlockSpec((None, tp, 128), lambda b, h: (b, 0, blk + h))
    flat, cast_specs, cast_shapes = _cast_specs(casts, bsz * ATT_HEADS, lambda b, h: b * ATT_HEADS + h)
    out = pl.pallas_call(
        kern,
        out_shape=[jax.ShapeDtypeStruct((bsz, tp, ATT_HEADS * ATT_V_DIM), BF16)] + cast_shapes,
        grid=(bsz, ATT_HEADS),
        in_specs=[
            pl.BlockSpec(memory_space=pltpu.SMEM),
            seq(BLK_Q), seq(BLK_K), seq(BLK_V),
            pl.BlockSpec((None, 6, t, t), lambda b, h: (h, 0, 0, 0)),
            pl.BlockSpec((1, ATT_V_DIM), lambda b, h: (0, 0)),
        ] + cast_specs,
        out_specs=[pl.BlockSpec((None, tp, 128), lambda b, h: (b, 0, h))] + cast_specs,
        scratch_shapes=[pltpu.VMEM((1, 2 * t), F32), pltpu.VMEM((rows, 2 * t), F32),
                        pltpu.VMEM((n_t, rows, t), BF16)],
        compiler_params=_params(("parallel", "parallel")),
        name="diff_attn",
    )(lam, proj3, proj3, proj3, btab, sub_gain.reshape(1, ATT_V_DIM), *flat)
    return out[0], [o.reshape(w.shape) for o, w in zip(out[1:], casts)]


def _split3(x):
    h1 = x.astype(BF16)
    r1 = x - h1.astype(F32)
    h2 = r1.astype(BF16)
    h3 = (r1 - h2.astype(F32)).astype(BF16)
    return h1, h2, h3


def _hgrn_kernel(q_ref, f_ref, i_ref, g_ref, la_ref, l1m_ref, og_ref, o_ref):
    c_len, sub = HGRN_CHUNK, HGRN_SUB
    half = sub // 2
    n_chunk = q_ref.shape[0] // c_len
    la, l1m, og = la_ref[...], l1m_ref[...], og_ref[...]
    rr = lax.broadcasted_iota(jnp.int32, (c_len, c_len), 0)
    cc = lax.broadcasted_iota(jnp.int32, (c_len, c_len), 1)
    tri = jnp.where(cc <= rr, 1.0, 0.0).astype(BF16)
    row8 = lax.broadcasted_iota(jnp.int32, (half, 1), 0)
    lane8 = lax.broadcasted_iota(jnp.int32, (half, c_len), 1)
    nt = (((1,), (1,)), ((), ()))

    def chunk(c, st):
        r0 = pl.multiple_of(c * c_len, c_len)
        rows = pl.ds(r0, c_len)
        z = f_ref[rows, :].astype(F32)
        qh = q_ref[rows, :].astype(F32)
        qh = qh * jax.nn.sigmoid(qh)
        v = i_ref[rows, :]
        gate = g_ref[rows, :].astype(F32)
        sp = jnp.log(1.0 + jnp.exp(-jnp.abs(z)))
        bb = l1m + jnp.minimum(z, 0.0) - sp
        log_f = jnp.maximum(la, bb) + jnp.log(1.0 + jnp.exp(-jnp.abs(la - bb)))
        valid = (r0 + lax.broadcasted_iota(jnp.int32, (c_len, 1), 0)) >= PAD0
        log_k = jnp.where(valid, l1m + jnp.minimum(-z, 0.0) - sp, -jnp.inf)
        g = sum(jnp.dot(tri, part, preferred_element_type=F32) for part in _split3(log_f))
        ck = log_k - g
        o_inter = lax.dot_general((qh * jnp.exp(g)).astype(BF16), st.astype(BF16), nt,
                                  preferred_element_type=F32)
        a_rows = []
        for a in range(c_len // sub):
            lo = a * sub
            ga = (g[lo:lo + half, :], g[lo + half:lo + sub, :])
            qa = (qh[lo:lo + half, :], qh[lo + half:lo + sub, :])
            if a == 0:
                blk = [jnp.zeros((half, c_len), F32)] * 2
            else:
                gs = g[lo - 1:lo, :]
                qd = (qh[lo:lo + sub, :] * jnp.exp(g[lo:lo + sub, :] - gs)).astype(BF16)
                kd = jnp.exp(jnp.minimum(gs - g[:lo, :], 0.0) + log_k[:lo, :]).astype(BF16)
                kd = jnp.concatenate([kd, jnp.zeros((c_len - lo, HGRN_D), BF16)], axis=0)
                a_off = lax.dot_general(qd, kd, nt, preferred_element_type=F32)
                blk = [a_off[:half, :], a_off[half:, :]]
            for s in range(sub):
                crow = ck[lo + s:lo + s + 1, :]
                for hh in range(s // half, 2):
                    col = jnp.sum(qa[hh] * jnp.exp(ga[hh] + crow), axis=-1, keepdims=True)
                    blk[hh] = jnp.where(lane8 == lo + s, col, blk[hh])
            for hh in range(2):
                a_rows.append(jnp.where(lane8 <= lo + hh * half + row8, blk[hh], 0.0))
        a_full = jnp.concatenate(a_rows, axis=0).astype(BF16)
        o = o_inter + jnp.dot(a_full, v, preferred_element_type=F32)
        g_last = g[c_len - 1:c_len, :]
        kd = jnp.exp(g_last - g + log_k).astype(BF16)
        st = st * jnp.exp(g_last) + lax.dot_general(v, kd, (((0,), (0,)), ((), ())),
                                                    preferred_element_type=F32)
        ms = jnp.mean(o * o, axis=-1, keepdims=True)
        y = o * lax.rsqrt(ms + EPS) * og * (gate * jax.nn.sigmoid(gate))
        o_ref[rows, :] = y.astype(BF16)
        return st

    lax.fori_loop(0, n_chunk, chunk, jnp.zeros((HGRN_D, HGRN_D), F32), unroll=3)


def _hgrn(proj3, log_lb, log1m_lb, out_gain):
    bsz, tp, _ = proj3.shape
    seq = lambda blk: pl.BlockSpec((None, tp, 128), lambda b, h: (b, 0, blk + h))
    chan = pl.BlockSpec((None, 1, HGRN_D), lambda b, h: (h, 0, 0))
    return pl.pallas_call(
        _hgrn_kernel,
        out_shape=jax.ShapeDtypeStruct((bsz, tp, HGRN_HEADS * HGRN_D), BF16),
        grid=(bsz, HGRN_HEADS),
        in_specs=[seq(BLK_RQ), seq(BLK_RF), seq(BLK_RI), seq(BLK_RG), chan, chan,
                  pl.BlockSpec((1, HGRN_D), lambda b, h: (0, 0))],
        out_specs=pl.BlockSpec((None, tp, 128), lambda b, h: (b, 0, h)),
        compiler_params=_params(("parallel", "parallel")),
        name="hgrn2",
    )(proj3, proj3, proj3, proj3,
      log_lb.reshape(HGRN_HEADS, 1, HGRN_D), log1m_lb.reshape(HGRN_HEADS, 1, HGRN_D),
      out_gain.reshape(1, HGRN_D))


def _merge_kernel(hs_ref, ua_ref, ur_ref, cb_ref, cc_ref, ch_ref, pc_ref, ph_ref,
                  g0_ref, g1_ref, g2_ref, cw_ref, wb_ref, wo_ref, o_ref, *, tm, tiles_per_seq):
    i = pl.program_id(0)
    row = (i % tiles_per_seq) * tm + lax.broadcasted_iota(jnp.int32, (tm, 1), 0)
    valid = row >= PAD0
    z = jnp.where(valid, cc_ref[...].astype(F32) * ch_ref[...].astype(F32), 0.0)
    halo_row = (i % tiles_per_seq) * tm - 8 + lax.broadcasted_iota(jnp.int32, (8, 1), 0)
    zp = jnp.where(halo_row >= PAD0, pc_ref[...].astype(F32) * ph_ref[...].astype(F32), 0.0)
    zz = jnp.concatenate([zp, z], axis=0)
    cw = cw_ref[...]
    y = (cw[2:3, :] * z + cw[1:2, :] * zz[7:7 + tm, :] + cw[0:1, :] * zz[6:6 + tm, :])
    u_conv = jnp.where(valid, cb_ref[...].astype(F32) * y, 0.0).astype(BF16)
    mixed = jnp.zeros((tm, D_MODEL), F32)
    for n, (u, g_ref) in enumerate(((ua_ref[...], g0_ref), (u_conv, g1_ref), (ur_ref[...], g2_ref))):
        up = jnp.dot(u, wb_ref[n], preferred_element_type=F32)
        mixed = mixed + jax.nn.sigmoid(g_ref[...].astype(F32)) * up
    o_ref[...] = hs_ref[...] + jnp.dot(mixed.astype(BF16), wo_ref[...], preferred_element_type=F32)


def _merge(hs, u_att, u_hgrn, proj, conv_w, wb_bf, wo_bf, tp):
    rows = hs.shape[0]
    tm = _row_tile(tp, 384)
    kern = functools.partial(_merge_kernel, tm=tm, tiles_per_seq=tp // tm)
    row_blk = lambda w, blk: pl.BlockSpec((tm, w), lambda i: (i, blk))
    halo = lambda blk: pl.BlockSpec((8, 512), lambda i: (jnp.maximum(i * (tm // 8) - 1, 0), blk))
    const = lambda shape: pl.BlockSpec(shape, lambda i: (0,) * len(shape))
    return pl.pallas_call(
        kern,
        out_shape=jax.ShapeDtypeStruct((rows, D_MODEL), F32),
        grid=(rows // tm,),
        in_specs=[row_blk(D_MODEL, 0), row_blk(512, 0), row_blk(512, 0),
                  row_blk(512, BLK_CB), row_blk(512, BLK_CC), row_blk(512, BLK_CH),
                  halo(BLK_CC), halo(BLK_CH),
                  row_blk(1024, BLK_GATE), row_blk(1024, BLK_GATE + 1), row_blk(1024, BLK_GATE + 2),
                  const((CONV_K, 512)), const((3, 512, D_MODEL)), const((D_MODEL, D_MODEL))],
        out_specs=row_blk(D_MODEL, 0),
        compiler_params=_params(("parallel",)),
        name="merge",
    )(hs, u_att, u_hgrn, proj, proj, proj, proj, proj, proj, proj, proj, conv_w, wb_bf, wo_bf)


def _ffn_kernel(hs_ref, g_ref, wg_ref, wu_ref, wd_ref, o_ref):
    x = hs_ref[...]
    ms = jnp.mean(x * x, axis=-1, keepdims=True)
    hn = (x * lax.rsqrt(ms + EPS) * g_ref[...]).astype(BF16)
    a = jnp.dot(hn, wg_ref[...], preferred_element_type=F32)
    u = jnp.dot(hn, wu_ref[...], preferred_element_type=F32)
    h = (a * jax.nn.sigmoid(a) * u).astype(BF16)
    o_ref[...] = x + jnp.dot(h, wd_ref[...], preferred_element_type=F32)


def _ffn(hs, gain, wg_bf, wu_bf, wd_bf):
    rows = hs.shape[0]
    d_ff = wg_bf.shape[1]
    tm = _row_tile(rows, 768)
    const = lambda shape: pl.BlockSpec(shape, lambda i: (0, 0), pipeline_mode=pl.Buffered(1))
    return pl.pallas_call(
        _ffn_kernel,
        out_shape=jax.ShapeDtypeStruct((rows, D_MODEL), F32),
        grid=(rows // tm,),
        in_specs=[pl.BlockSpec((tm, D_MODEL), lambda i: (i, 0)), const((1, D_MODEL)),
                  const((D_MODEL, d_ff)), const((D_MODEL, d_ff)), const((d_ff, D_MODEL))],
        out_specs=pl.BlockSpec((tm, D_MODEL), lambda i: (i, 0)),
        compiler_params=_params(("parallel",)),
        name="ffn_dense",
    )(hs, gain.reshape(1, D_MODEL), wg_bf, wu_bf, wd_bf)


def _router_kernel(hs_ref, g_ref, rw_ref, idx_ref, wt_ref, cnt_ref, carry_ref, *, tm, tiles_per_seq):
    i = pl.program_id(0)

    @pl.when(i == 0)
    def _():
        carry_ref[...] = jnp.zeros_like(carry_ref)

    x = hs_ref[...]
    ms = jnp.mean(x * x, axis=-1, keepdims=True)
    hn = x * lax.rsqrt(ms + EPS) * g_ref[...]
    hn_hi = hn.astype(BF16)
    hn_lo = (hn - hn_hi.astype(F32)).astype(BF16)
    logits = (jnp.dot(hn_hi, rw_ref[0], preferred_element_type=F32)
              + jnp.dot(hn_lo, rw_ref[0], preferred_element_type=F32)
              + jnp.dot(hn_hi, rw_ref[1], preferred_element_type=F32))
    lane = lax.broadcasted_iota(jnp.int32, logits.shape, 1)
    lane_f = lane.astype(F32)
    logits = jnp.where(lane < N_EXPERTS, logits, -jnp.inf)
    m1 = jnp.max(logits, axis=-1, keepdims=True)
    i1 = jnp.min(jnp.where(logits == m1, lane_f, 128.0), axis=-1, keepdims=True)
    rest = jnp.where(lane_f == i1, -jnp.inf, logits)
    m2 = jnp.max(rest, axis=-1, keepdims=True)
    i2 = jnp.min(jnp.where(rest == m2, lane_f, 128.0), axis=-1, keepdims=True)
    e2 = jnp.exp(m2 - m1)
    w1 = 1.0 / (1.0 + e2)
    w2 = e2 / (1.0 + e2)
    row = (i % tiles_per_seq) * tm + lax.broadcasted_iota(jnp.int32, (tm, 1), 0)
    real = row >= T_PAD
    hot1 = jnp.where(jnp.logical_and(real, lane_f == i1), 1.0, 0.0)
    hot2 = jnp.where(jnp.logical_and(real, lane_f == i2), 1.0, 0.0)
    both = hot1 + hot2
    rr = lax.broadcasted_iota(jnp.int32, (tm, tm), 0)
    cc = lax.broadcasted_iota(jnp.int32, (tm, tm), 1)
    earlier = jnp.where(cc < rr, 1.0, 0.0).astype(BF16)
    before = carry_ref[...] + jnp.dot(earlier, both.astype(BF16), preferred_element_type=F32)
    r1 = jnp.sum(before * hot1, axis=-1, keepdims=True)
    r2 = jnp.sum(before * hot2, axis=-1, keepdims=True)
    carry_ref[...] += jnp.sum(both, axis=0, keepdims=True)
    cnt_ref[...] = carry_ref[...]
    packed = jnp.where(lane == 0, i1, jnp.where(lane == 1, i2, jnp.where(lane == 2, r1, jnp.where(lane == 3, r2, 0.0))))
    idx_ref[...] = packed.astype(jnp.int32)
    wt_ref[...] = jnp.where(lane == 0, w1, jnp.where(lane == 1, w2, 0.0))


def _router(hs, gain, router_w, tp):
    rows = hs.shape[0]
    tm = _row_tile(tp, 384)
    rw = jnp.zeros((D_MODEL, 128), F32).at[:, :N_EXPERTS].set(router_w.astype(F32))
    rw_hi = rw.astype(BF16)
    rw = jnp.stack([rw_hi, (rw - rw_hi.astype(F32)).astype(BF16)])
    blk = pl.BlockSpec((tm, 128), lambda i: (i, 0))
    return pl.pallas_call(
        functools.partial(_router_kernel, tm=tm, tiles_per_seq=tp // tm),
        out_shape=(jax.ShapeDtypeStruct((rows, 128), jnp.int32), jax.ShapeDtypeStruct((rows, 128), F32),
                   jax.ShapeDtypeStruct((1, 128), F32)),
        grid=(rows // tm,),
        in_specs=[pl.BlockSpec((tm, D_MODEL), lambda i: (i, 0)),
                  pl.BlockSpec((1, D_MODEL), lambda i: (0, 0)),
                  pl.BlockSpec((2, D_MODEL, 128), lambda i: (0, 0, 0))],
        out_specs=(blk, blk, pl.BlockSpec((1, 128), lambda i: (0, 0))),
        scratch_shapes=[pltpu.VMEM((1, 128), F32)],
        compiler_params=_params(("arbitrary",)),
        name="moe_router",
    )(hs, gain.reshape(1, D_MODEL), rw)


def _dispatch_kernel(s1_ref, s2_ref, zr_ref, hs_ref, wt_ref, g_ref, wg_ref, wu_ref, wd_ref,
                     xs_ref, wg_out, wu_out, wd_out, rows_ref, zero_ref, sem, zsem,
                     *, td, tg, tiles_per_seq, n_steps):
    for src, dst in ((wg_ref, wg_out), (wu_ref, wu_out), (wd_ref, wd_out)):
        dst[...] = src[...].astype(BF16)
    n = pl.program_id(0) * tiles_per_seq + pl.program_id(1)
    buf = n % 2
    base = n * td

    def row_copy(r, which):
        slot = (s1_ref, s2_ref)[which][base + r]
        return pltpu.make_async_copy(rows_ref.at[buf, which, pl.ds(r, 1)], xs_ref.at[pl.ds(slot, 1)],
                                     sem.at[buf])

    def wait_rows(b):
        def body(r, c):
            pltpu.make_async_copy(rows_ref.at[b, 0, pl.ds(0, 1)], xs_ref.at[pl.ds(0, 1)], sem.at[b]).wait()
            return c
        lax.fori_loop(0, 2 * td, body, 0)

    @pl.when(n == 0)
    def _():
        zero_ref[...] = jnp.zeros_like(zero_ref)
        for e in range(2 * N_EXPERTS):
            fill = pltpu.make_async_copy(zero_ref, xs_ref.at[pl.ds(pl.multiple_of(zr_ref[e], tg), tg)], zsem)
            fill.start()
            fill.wait()

    @pl.when(n >= 2)
    def _():
        wait_rows(buf)

    x = hs_ref[...]
    ms = jnp.mean(x * x, axis=-1, keepdims=True)
    hn = x * lax.rsqrt(ms + EPS) * g_ref[...]
    wt = wt_ref[...]
    for which in range(2):
        rows_ref[buf, which, :, :D_MODEL] = hn
        rows_ref[buf, which, :, D_MODEL:] = jnp.broadcast_to(wt[:, which:which + 1], (td, XS_EXTRA))

    def issue(r, c):
        row_copy(r, 0).start()
        row_copy(r, 1).start()
        return c

    lax.fori_loop(0, td, issue, 0)

    @pl.when(n == n_steps - 1)
    def _():
        wait_rows(buf)
        if n_steps > 1:
            wait_rows(1 - buf)


def _dispatch(hs3, wts3, gain, slot1, slot2, zero_rows, n_slots, tg, expert_weights):
    bsz, tp, _ = hs3.shape
    td = T_PAD
    tiles_per_seq = (tp - T_PAD) // td
    n_steps = bsz * tiles_per_seq
    width = D_MODEL + XS_EXTRA
    kern = functools.partial(_dispatch_kernel, td=td, tg=tg, tiles_per_seq=tiles_per_seq, n_steps=n_steps)
    flat = [w.reshape(-1, w.shape[-1]) for w in expert_weights]
    assert all(w.shape[0] % (n_steps * WIN_ALIGN) == 0 for w in flat)
    w_spec = lambda w: pl.BlockSpec((w.shape[0] // n_steps, w.shape[1]),
                                    lambda b, i, *_: (b * tiles_per_seq + i, 0))
    out = pl.pallas_call(
        kern,
        out_shape=[jax.ShapeDtypeStruct((n_slots, width), F32)]
        + [jax.ShapeDtypeStruct(w.shape, BF16) for w in flat],
        grid_spec=pltpu.PrefetchScalarGridSpec(
            num_scalar_prefetch=3,
            grid=(bsz, tiles_per_seq),
            in_specs=[pl.BlockSpec((None, td, D_MODEL), lambda b, i, *_: (b, i + 1, 0)),
                      pl.BlockSpec((None, td, 128), lambda b, i, *_: (b, i + 1, 0)),
                      pl.BlockSpec((1, D_MODEL), lambda b, i, *_: (0, 0))] + [w_spec(w) for w in flat],
            out_specs=[pl.BlockSpec(memory_space=pl.ANY)] + [w_spec(w) for w in flat],
            scratch_shapes=[pltpu.VMEM((2, 2, td, width), F32), pltpu.VMEM((tg, width), F32),
                            pltpu.SemaphoreType.DMA((2,)), pltpu.SemaphoreType.DMA],
        ),
        compiler_params=_params(("arbitrary", "arbitrary")),
        name="moe_dispatch",
    )(slot1, slot2, zero_rows, hs3, wts3, gain.reshape(1, D_MODEL), *flat)
    return out[0], [o.reshape(w.shape) for o, w in zip(out[1:], expert_weights)]


def _expert_kernel(te_ref, nt_ref, x_ref, wg_ref, wu_ref, wd_ref, o_ref, hn_ref, acc_ref, *, n_f):
    t = pl.program_id(0)
    f = pl.program_id(1)

    @pl.when(t < nt_ref[0])
    def _():
        @pl.when(f == 0)
        def _():
            hn_ref[...] = x_ref[:, :D_MODEL].astype(BF16)

        hn = hn_ref[...]
        a = jnp.dot(hn, wg_ref[...], preferred_element_type=F32)
        u = jnp.dot(hn, wu_ref[...], preferred_element_type=F32)
        h = (a * jax.nn.sigmoid(a) * u).astype(BF16)

        @pl.when(f == 0)
        def _():
            acc_ref[...] = jnp.dot(h, wd_ref[...], preferred_element_type=F32)

        @pl.when(jnp.logical_and(f > 0, f < n_f - 1))
        def _():
            acc_ref[...] += jnp.dot(h, wd_ref[...], preferred_element_type=F32)

        @pl.when(f == n_f - 1)
        def _():
            y = acc_ref[...] + jnp.dot(h, wd_ref[...], preferred_element_type=F32)
            o_ref[...] = (y * x_ref[:, D_MODEL:D_MODEL + 1]).astype(BF16)

    @pl.when(jnp.logical_and(t >= nt_ref[0], f == n_f - 1))
    def _():
        o_ref[...] = jnp.zeros_like(o_ref)


def _experts(xs, tile_expert, n_tiles_used, wg_bf, wu_bf, wd_bf, tg):
    slots = xs.shape[0]
    n_f = MOE_F_BLOCKS
    tf = wg_bf.shape[2] // n_f

    def x_map(t, f, te, nt):
        return (jnp.maximum(jnp.minimum(t, nt[0] - 1), 0), 0)

    def f_of(t, f, nt):
        return jnp.where(t < nt[0], f, n_f - 1)

    return pl.pallas_call(
        functools.partial(_expert_kernel, n_f=n_f),
        out_shape=jax.ShapeDtypeStruct((slots, D_MODEL), BF16),
        grid_spec=pltpu.PrefetchScalarGridSpec(
            num_scalar_prefetch=2,
            grid=(slots // tg, n_f),
            in_specs=[
                pl.BlockSpec((tg, D_MODEL + XS_EXTRA), x_map),
                pl.BlockSpec((None, D_MODEL, tf), lambda t, f, te, nt: (te[t], 0, f_of(t, f, nt))),
                pl.BlockSpec((None, D_MODEL, tf), lambda t, f, te, nt: (te[t], 0, f_of(t, f, nt))),
                pl.BlockSpec((None, tf, D_MODEL), lambda t, f, te, nt: (te[t], f_of(t, f, nt), 0)),
            ],
            out_specs=pl.BlockSpec((tg, D_MODEL), lambda t, f, te, nt: (t, 0)),
            scratch_shapes=[pltpu.VMEM((tg, D_MODEL), BF16), pltpu.VMEM((tg, D_MODEL), F32)],
        ),
        compiler_params=_params(("arbitrary", "arbitrary")),
        name="moe_experts",
    )(tile_expert, n_tiles_used, xs, wg_bf, wu_bf, wd_bf)


def _combine_kernel(ws_ref, hs_ref, route_ref, y_ref, o_ref, win_ref, sem, *, tc, tiles_per_seq, n_steps):
    n = pl.program_id(0) * tiles_per_seq + pl.program_id(1)
    cur = n % 2

    def window_copy(step, b, e):
        start = pl.multiple_of(ws_ref[step * N_EXPERTS + e], WIN_ALIGN)
        return pltpu.make_async_copy(y_ref.at[pl.ds(start, COMBINE_WIN)],
                                     win_ref.at[b, pl.ds(e * COMBINE_WIN, COMBINE_WIN)], sem.at[b])

    def fetch(step, b):
        for e in range(N_EXPERTS):
            window_copy(step, b, e).start()

    @pl.when(n == 0)
    def _():
        fetch(0, 0)

    @pl.when(n + 1 < n_steps)
    def _():
        fetch(n + 1, 1 - cur)

    for e in range(N_EXPERTS):
        window_copy(n, cur, e).wait()

    route = route_ref[...]
    e1, e2, s1, s2 = (route[:, c:c + 1] for c in range(4))
    start1 = jnp.zeros_like(s1)
    start2 = jnp.zeros_like(s2)
    for e in range(N_EXPERTS):
        start = ws_ref[n * N_EXPERTS + e]
        start1 = jnp.where(e1 == e, start, start1)
        start2 = jnp.where(e2 == e, start, start2)
    col1 = e1 * COMBINE_WIN + s1 - start1
    col2 = e2 * COMBINE_WIN + s2 - start2
    pos = lax.broadcasted_iota(jnp.int32, (1, N_EXPERTS * COMBINE_WIN), 1)
    pick = jnp.where(jnp.logical_or(pos == col1, pos == col2), 1.0, 0.0).astype(BF16)
    o_ref[...] = hs_ref[...] + jnp.dot(pick, win_ref[cur], preferred_element_type=F32)


def _combine(hs3, route, y, win_start, seq):
    bsz = hs3.shape[0]
    tc = COMBINE_TOKENS
    tiles_per_seq = seq // tc
    kern = functools.partial(_combine_kernel, tc=tc, tiles_per_seq=tiles_per_seq, n_steps=bsz * tiles_per_seq)
    tok = lambda w: pl.BlockSpec((tc, w), lambda b, i, ws: (b * tiles_per_seq + i, 0))
    return pl.pallas_call(
        kern,
        out_shape=jax.ShapeDtypeStruct((bsz, seq, D_MODEL), F32),
        grid_spec=pltpu.PrefetchScalarGridSpec(
            num_scalar_prefetch=1,
            grid=(bsz, tiles_per_seq),
            in_specs=[
                pl.BlockSpec((None, tc, D_MODEL), lambda b, i, ws: (b, i + T_PAD // tc, 0)),
                tok(4),
                pl.BlockSpec(memory_space=pl.ANY),
            ],
            out_specs=pl.BlockSpec((None, tc, D_MODEL), lambda b, i, ws: (b, i, 0)),
            scratch_shapes=[pltpu.VMEM((2, N_EXPERTS * COMBINE_WIN, D_MODEL), BF16),
                            pltpu.SemaphoreType.DMA((2,))],
        ),
        compiler_params=_params(("arbitrary", "arbitrary")),
        name="moe_combine",
    )(win_start, hs3, route, y)


def _moe(hs, gain, router_w, w_gate, w_up, w_down, bsz, tp):
    seq = tp - T_PAD
    n_tok = bsz * seq
    tg = MOE_TILE
    idx, wts, cnt = _router(hs, gain, router_w, tp)
    sel = idx.reshape(bsz, tp, 128)[:, T_PAD:, :4].reshape(n_tok, 4)
    counts = cnt[0, :N_EXPERTS].astype(jnp.int32)
    padded = ((counts + tg - 1) // tg) * tg
    ends = jnp.cumsum(padded)
    starts = ends - padded
    experts = jnp.arange(N_EXPERTS, dtype=jnp.int32)[None, :]
    slot1 = jnp.sum(jnp.where(sel[:, 0:1] == experts, starts[None, :], 0), axis=1) + sel[:, 2]
    slot2 = jnp.sum(jnp.where(sel[:, 1:2] == experts, starts[None, :], 0), axis=1) + sel[:, 3]
    slot1, slot2 = slot1.astype(jnp.int32), slot2.astype(jnp.int32)
    n_slots = 2 * n_tok + N_EXPERTS * tg
    tile_start = jnp.arange(n_slots // tg, dtype=jnp.int32)[:, None] * tg
    tile_expert = jnp.minimum(jnp.sum((tile_start >= ends[None, :]).astype(jnp.int32), axis=1), N_EXPERTS - 1)
    n_tiles_used = (ends[-1:] // tg).astype(jnp.int32)
    tail = n_slots - tg * (1 + jnp.arange(N_EXPERTS, dtype=jnp.int32))
    zero_rows = jnp.concatenate([jnp.maximum(ends - tg, 0), tail]).astype(jnp.int32)

    tc = COMBINE_TOKENS
    first = jnp.minimum(
        jnp.min(jnp.where(sel[:, 0:1] == experts, slot1[:, None], n_slots).reshape(n_tok // tc, tc, N_EXPERTS), axis=1),
        jnp.min(jnp.where(sel[:, 1:2] == experts, slot2[:, None], n_slots).reshape(n_tok // tc, tc, N_EXPERTS), axis=1))
    win_start = jnp.where(first == n_slots, 0,
                          jnp.minimum(first // WIN_ALIGN * WIN_ALIGN, n_slots - COMBINE_WIN))
    route = jnp.concatenate([sel[:, :2], slot1[:, None], slot2[:, None]], axis=1)

    hs3 = hs.reshape(bsz, tp, D_MODEL)
    xs, (wg_bf, wu_bf, wd_bf) = _dispatch(hs3, wts.reshape(bsz, tp, 128), gain, slot1, slot2, zero_rows,
                                          n_slots, tg, (w_gate, w_up, w_down))
    y = _experts(xs, tile_expert.astype(jnp.int32), n_tiles_used, wg_bf, wu_bf, wd_bf, tg)
    return _combine(hs3, route, y, win_start.reshape(-1).astype(jnp.int32), seq)


def _permute_qk_cols(w):
    return w.reshape(D_MODEL, 2, ATT_HEADS, ATT_QK_DIM).transpose(0, 2, 1, 3).reshape(D_MODEL, 512)


def kernel(x, meta_tokens, norm1_gain, norm2_gain, w_in, q_norm_gain, k_norm_gain, diff_lambda,
           attn_sub_gain, rel_bias, conv_w, hgrn_lb_logits, hgrn_out_gain, w_branch, w_out,
           ffn_w_gate, ffn_w_up, ffn_w_down, router_w, moe_w_gate, moe_w_up, moe_w_down):
    bsz, seq, _ = x.shape
    depth = w_in.shape[0]
    tp = T_PAD + seq
    assert tp % ATT_TILE == 0 and depth == 2

    head = jnp.concatenate([jnp.zeros((PAD0, D_MODEL), x.dtype), meta_tokens.astype(x.dtype)], axis=0)
    hs = jnp.concatenate([jnp.broadcast_to(head[None], (bsz, T_PAD, D_MODEL)), x], axis=1)
    hs = hs.reshape(bsz * tp, D_MODEL)

    lb_all = jnp.cumsum(jax.nn.softmax(hgrn_lb_logits.astype(F32), axis=0), axis=0)
    lb_all = lb_all - lb_all[0]
    btab = _attn_bias_tables(rel_bias, ATT_TILE)

    def permuted(w):
        return jnp.concatenate([_permute_qk_cols(w[:, :512]), _permute_qk_cols(w[:, 512:1024])], axis=1)

    later = dict(branch0=w_branch[0], out0=w_out[0], gate=ffn_w_gate[0], up=ffn_w_up[0], down=ffn_w_down[0],
                 w_in1=w_in[1], branch1=w_branch[1], out1=w_out[1])
    bf = {}
    out = None
    for layer in range(depth):
        if layer == 0:
            w = w_in[0]
            w_bf = jnp.concatenate([permuted(w), w[:, 1024:]], axis=1).astype(BF16)
        else:
            w_bf = bf["w_in1"].at[:, :1024].set(permuted(bf["w_in1"]))
        qk_gain = jnp.concatenate([jnp.tile(q_norm_gain[layer].astype(F32), 8) * (ATT_QK_DIM ** -0.5 * LOG2E),
                                   jnp.tile(k_norm_gain[layer].astype(F32), 8)]).reshape(1, COL_TILE)
        proj = _inproj(hs, norm1_gain[layer], w_bf, qk_gain)
        proj3 = proj.reshape(bsz, tp, IN_COLS)

        lam_init = 0.8 - 0.6 * math.exp(-0.3 * layer)
        lp = diff_lambda[layer].astype(F32)
        lam = jnp.exp(jnp.sum(lp[0] * lp[1])) - jnp.exp(jnp.sum(lp[2] * lp[3])) + lam_init
        u_att, cast = _diff_attention(proj3, lam.reshape(1), attn_sub_gain[layer].astype(F32), btab,
                                      1.0 - lam_init, tuple(later.values()) if layer == 0 else ())
        if layer == 0:
            bf = dict(zip(later.keys(), cast))

        lb = lb_all[layer]
        u_hgrn = _hgrn(proj3, jnp.log(lb), jnp.log1p(-lb), hgrn_out_gain[layer].astype(F32))

        hs = _merge(hs, u_att.reshape(bsz * tp, 512), u_hgrn.reshape(bsz * tp, 512), proj,
                    conv_w[layer].astype(F32), bf["branch%d" % layer], bf["out%d" % layer], tp)

        j = layer // 2
        if layer % 2 == 0:
            hs = _ffn(hs, norm2_gain[layer], bf["gate"], bf["up"], bf["down"])
        else:
            out = _moe(hs, norm2_gain[layer], router_w[j], moe_w_gate[j], moe_w_up[j], moe_w_down[j], bsz, tp)
    return out
```

```python
import functools
import math

import numpy as np
import jax
import jax.numpy as jnp
from jax import lax
from jax.experimental import pallas as pl
from jax.experimental.pallas import tpu as pltpu

F32 = jnp.float32
BF16 = jnp.bfloat16

D_MODEL = 1024
N_META = 16
EPS = 1e-6
ATT_HEADS = 4
ATT_QK_DIM = 64
ATT_V_DIM = 128
REL_BUCKETS = 32
REL_MAX_DIST = 128
CONV_K = 3
HGRN_HEADS = 4
HGRN_D = 128
N_EXPERTS = 8
IN_COLS = 8192

T_PAD = 128
PAD0 = T_PAD - N_META
ATT_TILE = 384
VT_ONES = 16
LOG2E = math.log2(math.e)
HGRN_CHUNK = 128
HGRN_SUB = 16
COL_TILE = 1024
GROUP_MEAN_WIDTH = 256
MOE_TILE = 512
MOE_F_BLOCKS = 2
XS_EXTRA = 128
COMBINE_TOKENS = 128
WIN_ALIGN = 16
COMBINE_WIN = COMBINE_TOKENS + WIN_ALIGN
MASK = -1e30
VMEM_LIMIT = 56 * 1024 * 1024

BLK_Q, BLK_K, BLK_V = 0, 4, 8
BLK_RQ, BLK_RF, BLK_RI, BLK_RG = 24, 28, 32, 36
BLK_CB, BLK_CC, BLK_CH = 3, 4, 5
BLK_GATE = 5


def _row_tile(rows, target):
    n = rows // 128
    best = 1
    for d in range(1, n + 1):
        if n % d == 0 and d * 128 <= target:
            best = d
    return best * 128


def _params(sem, vmem=VMEM_LIMIT):
    return pltpu.CompilerParams(dimension_semantics=sem, vmem_limit_bytes=vmem)


def _inproj_kernel(x_ref, g_ref, w_ref, qkg_ref, gm_ref, o_ref, xn_ref):
    j = pl.program_id(1)

    @pl.when(j == 0)
    def _():
        x = x_ref[...]
        ms = jnp.mean(x * x, axis=-1, keepdims=True)
        xn_ref[...] = (x * lax.rsqrt(ms + EPS) * g_ref[...]).astype(BF16)

    acc = jnp.dot(xn_ref[...], w_ref[...], preferred_element_type=F32)

    @pl.when(j == 0)
    def _():
        sq = acc * acc
        hi = sq.astype(BF16)
        lo = (sq - hi.astype(F32)).astype(BF16)
        gm = gm_ref[...]
        width = gm.shape[0]
        ms = jnp.concatenate(
            [jnp.dot(hi[:, c:c + width], gm, preferred_element_type=F32)
             + jnp.dot(lo[:, c:c + width], gm, preferred_element_type=F32)
             for c in range(0, COL_TILE, width)], axis=1)
        o_ref[...] = (acc * lax.rsqrt(ms + EPS) * qkg_ref[...]).astype(BF16)

    @pl.when(j > 0)
    def _():
        o_ref[...] = acc.astype(BF16)


def _inproj(hs, gain, w_bf, qk_gain):
    rows = hs.shape[0]
    tm = _row_tile(rows, 1536)
    n_col = IN_COLS // COL_TILE
    assert COL_TILE == 4 * ATT_HEADS * ATT_QK_DIM
    grp = np.arange(GROUP_MEAN_WIDTH) // ATT_QK_DIM
    gm = jnp.asarray((grp[:, None] == grp[None, :]).astype(np.float32) / ATT_QK_DIM, BF16)
    return pl.pallas_call(
        _inproj_kernel,
        out_shape=jax.ShapeDtypeStruct((rows, IN_COLS), BF16),
        grid=(rows // tm, n_col),
        in_specs=[
            pl.BlockSpec((tm, D_MODEL), lambda i, j: (i, 0)),
            pl.BlockSpec((1, D_MODEL), lambda i, j: (0, 0)),
            pl.BlockSpec((D_MODEL, COL_TILE), lambda i, j: (0, j)),
            pl.BlockSpec((1, COL_TILE), lambda i, j: (0, 0)),
            pl.BlockSpec((GROUP_MEAN_WIDTH, GROUP_MEAN_WIDTH), lambda i, j: (0, 0)),
        ],
        out_specs=pl.BlockSpec((tm, COL_TILE), lambda i, j: (i, j)),
        scratch_shapes=[pltpu.VMEM((tm, D_MODEL), BF16)],
        compiler_params=_params(("parallel", "arbitrary")),
        name="inproj",
    )(hs, gain.reshape(1, D_MODEL), w_bf, qk_gain, gm)


def _rel_bucket_table(n_max):
    n = np.arange(n_max, dtype=np.int64)
    max_exact = REL_BUCKETS // 2
    nf = np.maximum(n, 1).astype(np.float32)
    large = max_exact + (np.log(nf / np.float32(max_exact)) / np.float32(math.log(REL_MAX_DIST / max_exact))
                         * np.float32(REL_BUCKETS - max_exact)).astype(np.int32)
    large = np.minimum(large, REL_BUCKETS - 1)
    return np.where(n < max_exact, n, large).astype(np.int32)


def _attn_bias_tables(rel_bias, t):
    bucket = _rel_bucket_table(2 * t)
    assert np.all(bucket[t + 1:] == REL_BUCKETS - 1) and np.all(np.diff(bucket) >= 0)
    first_dist = tuple(int(np.searchsorted(bucket, b, side="left")) for b in range(REL_BUCKETS))
    return pl.pallas_call(
        functools.partial(_bias_kernel, t=t, first_dist=first_dist),
        out_shape=jax.ShapeDtypeStruct((ATT_HEADS, 6, t, t), F32),
        grid=(ATT_HEADS,),
        in_specs=[pl.BlockSpec(memory_space=pltpu.SMEM)],
        out_specs=pl.BlockSpec((None, 6, t, t), lambda h: (h, 0, 0, 0)),
        compiler_params=_params(("parallel",)),
        name="attn_bias",
    )(rel_bias.astype(F32))


def _bias_kernel(rb_ref, o_ref, *, t, first_dist):
    h = pl.program_id(0)
    key = lax.broadcasted_iota(jnp.int32, (t, t), 0)
    qry = lax.broadcasted_iota(jnp.int32, (t, t), 1)
    far = rb_ref[REL_BUCKETS - 1, h]

    def table(n):
        val = jnp.full((t, t), rb_ref[0, h] - far, F32)
        for b in range(1, REL_BUCKETS):
            val = jnp.where(n >= first_dist[b], rb_ref[b, h] - far, val)
        return val * LOG2E

    n0 = qry - key
    diag = jnp.where(n0 >= 0, table(n0), MASK)
    near = table(n0 + t)
    zero = jnp.zeros((t, t), F32)
    for kind, tab in enumerate((diag, near, zero)):
        o_ref[kind] = tab
        o_ref[kind + 3] = jnp.where(key < PAD0, MASK, tab)


def _cast_specs(arrays, n_steps, step_of):
    flat = [w.reshape(-1, w.shape[-1]) for w in arrays]
    assert all(w.shape[0] % (n_steps * WIN_ALIGN) == 0 for w in flat)
    specs = [pl.BlockSpec((w.shape[0] // n_steps, w.shape[1]), lambda *idx: (step_of(*idx), 0)) for w in flat]
    return flat, specs, [jax.ShapeDtypeStruct(w.shape, BF16) for w in flat]


def _attn_kernel(lam_ref, q_ref, k_ref, v_ref, bt_ref, sg_ref, *refs, t, out_scale, n_cast):
    cast_in, (o_ref, *cast_out), (m_ref, acc_ref, vt_ref) = refs[:n_cast], refs[n_cast:2 * n_cast + 1], refs[-3:]
    for src, dst in zip(cast_in, cast_out):
        dst[...] = src[...].astype(BF16)
    n_t = vt_ref.shape[0]
    for j in range(n_t):
        vt_ref[j, :ATT_V_DIM, :] = v_ref[j * t:(j + 1) * t, :].astype(F32).T.astype(BF16)
        vt_ref[j, ATT_V_DIM:, :] = jnp.ones((VT_ONES, t), BF16)
    lane = lax.broadcasted_iota(jnp.int32, (1, 2 * ATT_QK_DIM), 1)
    nt = (((1,), (1,)), ((), ()))

    def tile_rows(i):
        return pl.ds(pl.multiple_of(i * t, t), t)

    def scores(i, j):
        q = q_ref[tile_rows(i), :]
        zero = jnp.zeros_like(q)
        q_cat = jnp.concatenate([jnp.where(lane < ATT_QK_DIM, q, zero), jnp.where(lane >= ATT_QK_DIM, q, zero)],
                                axis=0)
        bias = bt_ref[jnp.minimum(i - j, 2) + jnp.where(j == 0, 3, 0)]
        s = (lax.dot_general(k_ref[tile_rows(j), :], q_cat, nt, preferred_element_type=F32)
             + jnp.concatenate([bias, bias], axis=1))
        return s, jnp.max(s, axis=0, keepdims=True)

    def consume(j, scored):
        s, s_max = scored
        m_old = m_ref[...]
        m_new = jnp.maximum(m_old, s_max)
        m_ref[...] = m_new
        p = jnp.exp2(s - m_new).astype(BF16)
        acc_ref[...] = (jnp.exp2(m_old - m_new) * acc_ref[...]
                        + jnp.dot(vt_ref[j], p, preferred_element_type=F32))

    def reset():
        m_ref[...] = jnp.full(m_ref.shape, MASK, F32)
        acc_ref[...] = jnp.zeros(acc_ref.shape, F32)

    def finish(i):
        acc = acc_ref[...]
        a1, a2 = acc[:, :t], acc[:, t:]
        o_t = (a1[:ATT_V_DIM] / a1[ATT_V_DIM:ATT_V_DIM + 1]
               - lam_ref[0] * (a2[:ATT_V_DIM] / a2[ATT_V_DIM:ATT_V_DIM + 1]))
        o = o_t.T
        ms = jnp.mean(o * o, axis=-1, keepdims=True)
        y = o * lax.rsqrt(ms + EPS) * (sg_ref[...] * out_scale)
        row = i * t + lax.broadcasted_iota(jnp.int32, (t, 1), 0)
        o_ref[tile_rows(i), :] = jnp.where(row >= PAD0, y, 0.0).astype(BF16)
        reset()

    def step(_, carry):
        i, j, s = carry
        last = j == i
        ni = jnp.where(last, i + 1, i)
        nj = jnp.where(last, 0, j + 1)
        nxt = scores(jnp.minimum(ni, n_t - 1), nj)
        consume(j, s)

        @pl.when(last)
        def _():
            finish(i)

        return ni, nj, nxt

    reset()
    first = jnp.int32(0)
    lax.fori_loop(0, n_t * (n_t + 1) // 2, step, (first, first, scores(first, first)))


def _diff_attention(proj3, lam, sub_gain, btab, out_scale, casts=()):
    bsz, tp, _ = proj3.shape
    t = ATT_TILE
    n_t = tp // t
    rows = ATT_V_DIM + VT_ONES
    kern = functools.partial(_attn_kernel, t=t, out_scale=out_scale, n_cast=len(casts))
    seq = lambda blk: pl.BlockSpec((None, tp, 128), lambda b, h: (b, 0, blk + h))
    flat, cast_specs, cast_shapes = _cast_specs(casts, bsz * ATT_HEADS, lambda b, h: b * ATT_HEADS + h)
    out = pl.pallas_call(
        kern,
        out_shape=[jax.ShapeDtypeStruct((bsz, tp, ATT_HEADS * ATT_V_DIM), BF16)] + cast_shapes,
        grid=(bsz, ATT_HEADS),
        in_specs=[
            pl.BlockSpec(memory_space=pltpu.SMEM),
            seq(BLK_Q), seq(BLK_K), seq(BLK_V),
            pl.BlockSpec((None, 6, t, t), lambda b, h: (h, 0, 0, 0)),
            pl.BlockSpec((1, ATT_V_DIM), lambda b, h: (0, 0)),
        ] + cast_specs,
        out_specs=[pl.BlockSpec((None, tp, 128), lambda b, h: (b, 0, h))] + cast_specs,
        scratch_shapes=[pltpu.VMEM((1, 2 * t), F32), pltpu.VMEM((rows, 2 * t), F32),
                        pltpu.VMEM((n_t, rows, t), BF16)],
        compiler_params=_params(("parallel", "parallel")),
        name="diff_attn",
    )(lam, proj3, proj3, proj3, btab, sub_gain.reshape(1, ATT_V_DIM), *flat)
    return out[0], [o.reshape(w.shape) for o, w in zip(out[1:], casts)]


def _split3(x):
    h1 = x.astype(BF16)
    r1 = x - h1.astype(F32)
    h2 = r1.astype(BF16)
    h3 = (r1 - h2.astype(F32)).astype(BF16)
    return h1, h2, h3


def _hgrn_kernel(q_ref, f_ref, i_ref, g_ref, la_ref, l1m_ref, og_ref, o_ref):
    c_len, sub = HGRN_CHUNK, HGRN_SUB
    half = sub // 2
    n_chunk = q_ref.shape[0] // c_len
    la, l1m, og = la_ref[...], l1m_ref[...], og_ref[...]
    rr = lax.broadcasted_iota(jnp.int32, (c_len, c_len), 0)
    cc = lax.broadcasted_iota(jnp.int32, (c_len, c_len), 1)
    tri = jnp.where(cc <= rr, 1.0, 0.0).astype(BF16)
    row8 = lax.broadcasted_iota(jnp.int32, (half, 1), 0)
    lane8 = lax.broadcasted_iota(jnp.int32, (half, c_len), 1)
    nt = (((1,), (1,)), ((), ()))

    def chunk(c, st):
        r0 = pl.multiple_of(c * c_len, c_len)
        rows = pl.ds(r0, c_len)
        z = f_ref[rows, :].astype(F32)
        qh = q_ref[rows, :].astype(F32)
        qh = qh * jax.nn.sigmoid(qh)
        v = i_ref[rows, :]
        gate = g_ref[rows, :].astype(F32)
        sp = jnp.log(1.0 + jnp.exp(-jnp.abs(z)))
        bb = l1m + jnp.minimum(z, 0.0) - sp
        log_f = jnp.maximum(la, bb) + jnp.log(1.0 + jnp.exp(-jnp.abs(la - bb)))
        valid = (r0 + lax.broadcasted_iota(jnp.int32, (c_len, 1), 0)) >= PAD0
        log_k = jnp.where(valid, l1m + jnp.minimum(-z, 0.0) - sp, -jnp.inf)
        g = sum(jnp.dot(tri, part, preferred_element_type=F32) for part in _split3(log_f))
        ck = log_k - g
        o_inter = lax.dot_general((qh * jnp.exp(g)).astype(BF16), st.astype(BF16), nt,
                                  preferred_element_type=F32)
        a_rows = []
        for a in range(c_len // sub):
            lo = a * sub
            ga = (g[lo:lo + half, :], g[lo + half:lo + sub, :])
            qa = (qh[lo:lo + half, :], qh[lo + half:lo + sub, :])
            if a == 0:
                blk = [jnp.zeros((half, c_len), F32)] * 2
            else:
                gs = g[lo - 1:lo, :]
                qd = (qh[lo:lo + sub, :] * jnp.exp(g[lo:lo + sub, :] - gs)).astype(BF16)
                kd = jnp.exp(jnp.minimum(gs - g[:lo, :], 0.0) + log_k[:lo, :]).astype(BF16)
                kd = jnp.concatenate([kd, jnp.zeros((c_len - lo, HGRN_D), BF16)], axis=0)
                a_off = lax.dot_general(qd, kd, nt, preferred_element_type=F32)
                blk = [a_off[:half, :], a_off[half:, :]]
            for s in range(sub):
                crow = ck[lo + s:lo + s + 1, :]
                for hh in range(s // half, 2):
                    col = jnp.sum(qa[hh] * jnp.exp(ga[hh] + crow), axis=-1, keepdims=True)
                    blk[hh] = jnp.where(lane8 == lo + s, col, blk[hh])
            for hh in range(2):
                a_rows.append(jnp.where(lane8 <= lo + hh * half + row8, blk[hh], 0.0))
        a_full = jnp.concatenate(a_rows, axis=0).astype(BF16)
        o = o_inter + jnp.dot(a_full, v, preferred_element_type=F32)
        g_last = g[c_len - 1:c_len, :]
        kd = jnp.exp(g_last - g + log_k).astype(BF16)
        st = st * jnp.exp(g_last) + lax.dot_general(v, kd, (((0,), (0,)), ((), ())),
                                                    preferred_element_type=F32)
        ms = jnp.mean(o * o, axis=-1, keepdims=True)
        y = o * lax.rsqrt(ms + EPS) * og * (gate * jax.nn.sigmoid(gate))
        o_ref[rows, :] = y.astype(BF16)
        return st

    lax.fori_loop(0, n_chunk, chunk, jnp.zeros((HGRN_D, HGRN_D), F32), unroll=3)


def _hgrn(proj3, log_lb, log1m_lb, out_gain):
    bsz, tp, _ = proj3.shape
    seq = lambda blk: pl.BlockSpec((None, tp, 128), lambda b, h: (b, 0, blk + h))
    chan = pl.BlockSpec((None, 1, HGRN_D), lambda b, h: (h, 0, 0))
    return pl.pallas_call(
        _hgrn_kernel,
        out_shape=jax.ShapeDtypeStruct((bsz, tp, HGRN_HEADS * HGRN_D), BF16),
        grid=(bsz, HGRN_HEADS),
        in_specs=[seq(BLK_RQ), seq(BLK_RF), seq(BLK_RI), seq(BLK_RG), chan, chan,
                  pl.BlockSpec((1, HGRN_D), lambda b, h: (0, 0))],
        out_specs=pl.BlockSpec((None, tp, 128), lambda b, h: (b, 0, h)),
        compiler_params=_params(("parallel", "parallel")),
        name="hgrn2",
    )(proj3, proj3, proj3, proj3,
      log_lb.reshape(HGRN_HEADS, 1, HGRN_D), log1m_lb.reshape(HGRN_HEADS, 1, HGRN_D),
      out_gain.reshape(1, HGRN_D))


def _merge_kernel(hs_ref, ua_ref, ur_ref, cb_ref, cc_ref, ch_ref, pc_ref, ph_ref,
                  g0_ref, g1_ref, g2_ref, cw_ref, wb_ref, wo_ref, o_ref, *, tm, tiles_per_seq):
    i = pl.program_id(0)
    row = (i % tiles_per_seq) * tm + lax.broadcasted_iota(jnp.int32, (tm, 1), 0)
    valid = row >= PAD0
    z = jnp.where(valid, cc_ref[...].astype(F32) * ch_ref[...].astype(F32), 0.0)
    halo_row = (i % tiles_per_seq) * tm - 8 + lax.broadcasted_iota(jnp.int32, (8, 1), 0)
    zp = jnp.where(halo_row >= PAD0, pc_ref[...].astype(F32) * ph_ref[...].astype(F32), 0.0)
    zz = jnp.concatenate([zp, z], axis=0)
    cw = cw_ref[...]
    y = (cw[2:3, :] * z + cw[1:2, :] * zz[7:7 + tm, :] + cw[0:1, :] * zz[6:6 + tm, :])
    u_conv = jnp.where(valid, cb_ref[...].astype(F32) * y, 0.0).astype(BF16)
    mixed = jnp.zeros((tm, D_MODEL), F32)
    for n, (u, g_ref) in enumerate(((ua_ref[...], g0_ref), (u_conv, g1_ref), (ur_ref[...], g2_ref))):
        up = jnp.dot(u, wb_ref[n], preferred_element_type=F32)
        mixed = mixed + jax.nn.sigmoid(g_ref[...].astype(F32)) * up
    o_ref[...] = hs_ref[...] + jnp.dot(mixed.astype(BF16), wo_ref[...], preferred_element_type=F32)


def _merge(hs, u_att, u_hgrn, proj, conv_w, wb_bf, wo_bf, tp):
    rows = hs.shape[0]
    tm = _row_tile(tp, 384)
    kern = functools.partial(_merge_kernel, tm=tm, tiles_per_seq=tp // tm)
    row_blk = lambda w, blk: pl.BlockSpec((tm, w), lambda i: (i, blk))
    halo = lambda blk: pl.BlockSpec((8, 512), lambda i: (jnp.maximum(i * (tm // 8) - 1, 0), blk))
    const = lambda shape: pl.BlockSpec(shape, lambda i: (0,) * len(shape))
    return pl.pallas_call(
        kern,
        out_shape=jax.ShapeDtypeStruct((rows, D_MODEL), F32),
        grid=(rows // tm,),
        in_specs=[row_blk(D_MODEL, 0), row_blk(512, 0), row_blk(512, 0),
                  row_blk(512, BLK_CB), row_blk(512, BLK_CC), row_blk(512, BLK_CH),
                  halo(BLK_CC), halo(BLK_CH),
                  row_blk(1024, BLK_GATE), row_blk(1024, BLK_GATE + 1), row_blk(1024, BLK_GATE + 2),
                  const((CONV_K, 512)), const((3, 512, D_MODEL)), const((D_MODEL, D_MODEL))],
        out_specs=row_blk(D_MODEL, 0),
        compiler_params=_params(("parallel",)),
        name="merge",
    )(hs, u_att, u_hgrn, proj, proj, proj, proj, proj, proj, proj, proj, conv_w, wb_bf, wo_bf)


def _ffn_kernel(hs_ref, g_ref, wg_ref, wu_ref, wd_ref, o_ref):
    x = hs_ref[...]
    ms = jnp.mean(x * x, axis=-1, keepdims=True)
    hn = (x * lax.rsqrt(ms + EPS) * g_ref[...]).astype(BF16)
    a = jnp.dot(hn, wg_ref[...], preferred_element_type=F32)
    u = jnp.dot(hn, wu_ref[...], preferred_element_type=F32)
    h = (a * jax.nn.sigmoid(a) * u).astype(BF16)
    o_ref[...] = x + jnp.dot(h, wd_ref[...], preferred_element_type=F32)


def _ffn(hs, gain, wg_bf, wu_bf, wd_bf):
    rows = hs.shape[0]
    d_ff = wg_bf.shape[1]
    tm = _row_tile(rows, 768)
    const = lambda shape: pl.BlockSpec(shape, lambda i: (0, 0), pipeline_mode=pl.Buffered(1))
    return pl.pallas_call(
        _ffn_kernel,
        out_shape=jax.ShapeDtypeStruct((rows, D_MODEL), F32),
        grid=(rows // tm,),
        in_specs=[pl.BlockSpec((tm, D_MODEL), lambda i: (i, 0)), const((1, D_MODEL)),
                  const((D_MODEL, d_ff)), const((D_MODEL, d_ff)), const((d_ff, D_MODEL))],
        out_specs=pl.BlockSpec((tm, D_MODEL), lambda i: (i, 0)),
        compiler_params=_params(("parallel",)),
        name="ffn_dense",
    )(hs, gain.reshape(1, D_MODEL), wg_bf, wu_bf, wd_bf)


def _router_kernel(hs_ref, g_ref, rw_ref, idx_ref, wt_ref, cnt_ref, carry_ref, *, tm, tiles_per_seq):
    i = pl.program_id(0)

    @pl.when(i == 0)
    def _():
        carry_ref[...] = jnp.zeros_like(carry_ref)

    x = hs_ref[...]
    ms = jnp.mean(x * x, axis=-1, keepdims=True)
    hn = x * lax.rsqrt(ms + EPS) * g_ref[...]
    hn_hi = hn.astype(BF16)
    hn_lo = (hn - hn_hi.astype(F32)).astype(BF16)
    logits = (jnp.dot(hn_hi, rw_ref[0], preferred_element_type=F32)
              + jnp.dot(hn_lo, rw_ref[0], preferred_element_type=F32)
              + jnp.dot(hn_hi, rw_ref[1], preferred_element_type=F32))
    lane = lax.broadcasted_iota(jnp.int32, logits.shape, 1)
    lane_f = lane.astype(F32)
    logits = jnp.where(lane < N_EXPERTS, logits, -jnp.inf)
    m1 = jnp.max(logits, axis=-1, keepdims=True)
    i1 = jnp.min(jnp.where(logits == m1, lane_f, 128.0), axis=-1, keepdims=True)
    rest = jnp.where(lane_f == i1, -jnp.inf, logits)
    m2 = jnp.max(rest, axis=-1, keepdims=True)
    i2 = jnp.min(jnp.where(rest == m2, lane_f, 128.0), axis=-1, keepdims=True)
    e2 = jnp.exp(m2 - m1)
    w1 = 1.0 / (1.0 + e2)
    w2 = e2 / (1.0 + e2)
    row = (i % tiles_per_seq) * tm + lax.broadcasted_iota(jnp.int32, (tm, 1), 0)
    real = row >= T_PAD
    hot1 = jnp.where(jnp.logical_and(real, lane_f == i1), 1.0, 0.0)
    hot2 = jnp.where(jnp.logical_and(real, lane_f == i2), 1.0, 0.0)
    both = hot1 + hot2
    rr = lax.broadcasted_iota(jnp.int32, (tm, tm), 0)
    cc = lax.broadcasted_iota(jnp.int32, (tm, tm), 1)
    earlier = jnp.where(cc < rr, 1.0, 0.0).astype(BF16)
    before = carry_ref[...] + jnp.dot(earlier, both.astype(BF16), preferred_element_type=F32)
    r1 = jnp.sum(before * hot1, axis=-1, keepdims=True)
    r2 = jnp.sum(before * hot2, axis=-1, keepdims=True)
    carry_ref[...] += jnp.sum(both, axis=0, keepdims=True)
    cnt_ref[...] = carry_ref[...]
    packed = jnp.where(lane == 0, i1, jnp.where(lane == 1, i2, jnp.where(lane == 2, r1, jnp.where(lane == 3, r2, 0.0))))
    idx_ref[...] = packed.astype(jnp.int32)
    wt_ref[...] = jnp.where(lane == 0, w1, jnp.where(lane == 1, w2, 0.0))


def _router(hs, gain, router_w, tp):
    rows = hs.shape[0]
    tm = _row_tile(tp, 384)
    rw = jnp.zeros((D_MODEL, 128), F32).at[:, :N_EXPERTS].set(router_w.astype(F32))
    rw_hi = rw.astype(BF16)
    rw = jnp.stack([rw_hi, (rw - rw_hi.astype(F32)).astype(BF16)])
    blk = pl.BlockSpec((tm, 128), lambda i: (i, 0))
    return pl.pallas_call(
        functools.partial(_router_kernel, tm=tm, tiles_per_seq=tp // tm),
        out_shape=(jax.ShapeDtypeStruct((rows, 128), jnp.int32), jax.ShapeDtypeStruct((rows, 128), F32),
                   jax.ShapeDtypeStruct((1, 128), F32)),
        grid=(rows // tm,),
        in_specs=[pl.BlockSpec((tm, D_MODEL), lambda i: (i, 0)),
                  pl.BlockSpec((1, D_MODEL), lambda i: (0, 0)),
                  pl.BlockSpec((2, D_MODEL, 128), lambda i: (0, 0, 0))],
        out_specs=(blk, blk, pl.BlockSpec((1, 128), lambda i: (0, 0))),
        scratch_shapes=[pltpu.VMEM((1, 128), F32)],
        compiler_params=_params(("arbitrary",)),
        name="moe_router",
    )(hs, gain.reshape(1, D_MODEL), rw)


def _dispatch_kernel(s1_ref, s2_ref, zr_ref, hs_ref, wt_ref, g_ref, wg_ref, wu_ref, wd_ref,
                     xs_ref, wg_out, wu_out, wd_out, rows_ref, zero_ref, sem, zsem,
                     *, td, tg, tiles_per_seq, n_steps):
    for src, dst in ((wg_ref, wg_out), (wu_ref, wu_out), (wd_ref, wd_out)):
        dst[...] = src[...].astype(BF16)
    n = pl.program_id(0) * tiles_per_seq + pl.program_id(1)
    buf = n % 2
    base = n * td

    def row_copy(r, which):
        slot = (s1_ref, s2_ref)[which][base + r]
        return pltpu.make_async_copy(rows_ref.at[buf, which, pl.ds(r, 1)], xs_ref.at[pl.ds(slot, 1)],
                                     sem.at[buf])

    def wait_rows(b):
        def body(r, c):
            pltpu.make_async_copy(rows_ref.at[b, 0, pl.ds(0, 1)], xs_ref.at[pl.ds(0, 1)], sem.at[b]).wait()
            return c
        lax.fori_loop(0, 2 * td, body, 0)

    @pl.when(n == 0)
    def _():
        zero_ref[...] = jnp.zeros_like(zero_ref)

        def fill(e):
            return pltpu.make_async_copy(zero_ref, xs_ref.at[pl.ds(pl.multiple_of(zr_ref[e], tg), tg)], zsem)

        for e in range(N_EXPERTS, 2 * N_EXPERTS):
            fill(e).start()
        for e in range(N_EXPERTS, 2 * N_EXPERTS):
            fill(e).wait()
        for action in ("start", "wait"):
            getattr(fill(0), action)()
            for e in range(1, N_EXPERTS):
                @pl.when(zr_ref[e] != zr_ref[e - 1])
                def _():
                    getattr(fill(e), action)()

    @pl.when(n >= 2)
    def _():
        wait_rows(buf)

    x = hs_ref[...]
    ms = jnp.mean(x * x, axis=-1, keepdims=True)
    hn = x * lax.rsqrt(ms + EPS) * g_ref[...]
    wt = wt_ref[...]
    for which in range(2):
        rows_ref[buf, which, :, :D_MODEL] = hn
        rows_ref[buf, which, :, D_MODEL:] = jnp.broadcast_to(wt[:, which:which + 1], (td, XS_EXTRA))

    def issue(r, c):
        row_copy(r, 0).start()
        row_copy(r, 1).start()
        return c

    lax.fori_loop(0, td, issue, 0)

    @pl.when(n == n_steps - 1)
    def _():
        wait_rows(buf)
        if n_steps > 1:
            wait_rows(1 - buf)


def _dispatch(hs3, wts3, gain, slot1, slot2, zero_rows, n_slots, tg, expert_weights):
    bsz, tp, _ = hs3.shape
    td = T_PAD
    tiles_per_seq = (tp - T_PAD) // td
    n_steps = bsz * tiles_per_seq
    width = D_MODEL + XS_EXTRA
    kern = functools.partial(_dispatch_kernel, td=td, tg=tg, tiles_per_seq=tiles_per_seq, n_steps=n_steps)
    flat = [w.reshape(-1, w.shape[-1]) for w in expert_weights]
    assert all(w.shape[0] % (n_steps * WIN_ALIGN) == 0 for w in flat)
    w_spec = lambda w: pl.BlockSpec((w.shape[0] // n_steps, w.shape[1]),
                                    lambda b, i, *_: (b * tiles_per_seq + i, 0))
    out = pl.pallas_call(
        kern,
        out_shape=[jax.ShapeDtypeStruct((n_slots, width), F32)]
        + [jax.ShapeDtypeStruct(w.shape, BF16) for w in flat],
        grid_spec=pltpu.PrefetchScalarGridSpec(
            num_scalar_prefetch=3,
            grid=(bsz, tiles_per_seq),
            in_specs=[pl.BlockSpec((None, td, D_MODEL), lambda b, i, *_: (b, i + 1, 0)),
                      pl.BlockSpec((None, td, 128), lambda b, i, *_: (b, i + 1, 0)),
                      pl.BlockSpec((1, D_MODEL), lambda b, i, *_: (0, 0))] + [w_spec(w) for w in flat],
            out_specs=[pl.BlockSpec(memory_space=pl.ANY)] + [w_spec(w) for w in flat],
            scratch_shapes=[pltpu.VMEM((2, 2, td, width), F32), pltpu.VMEM((tg, width), F32),
                            pltpu.SemaphoreType.DMA((2,)), pltpu.SemaphoreType.DMA],
        ),
        compiler_params=_params(("arbitrary", "arbitrary")),
        name="moe_dispatch",
    )(slot1, slot2, zero_rows, hs3, wts3, gain.reshape(1, D_MODEL), *flat)
    return out[0], [o.reshape(w.shape) for o, w in zip(out[1:], expert_weights)]


def _expert_kernel(te_ref, nt_ref, x_ref, wg_ref, wu_ref, wd_ref, o_ref, hn_ref, acc_ref, *, n_f):
    t = pl.program_id(0)
    f = pl.program_id(1)

    @pl.when(t < nt_ref[0])
    def _():
        @pl.when(f == 0)
        def _():
            hn_ref[...] = x_ref[:, :D_MODEL].astype(BF16)

        hn = hn_ref[...]
        a = jnp.dot(hn, wg_ref[...], preferred_element_type=F32)
        u = jnp.dot(hn, wu_ref[...], preferred_element_type=F32)
        h = (a * jax.nn.sigmoid(a) * u).astype(BF16)

        @pl.when(f == 0)
        def _():
            acc_ref[...] = jnp.dot(h, wd_ref[...], preferred_element_type=F32)

        @pl.when(jnp.logical_and(f > 0, f < n_f - 1))
        def _():
            acc_ref[...] += jnp.dot(h, wd_ref[...], preferred_element_type=F32)

        @pl.when(f == n_f - 1)
        def _():
            y = acc_ref[...] + jnp.dot(h, wd_ref[...], preferred_element_type=F32)
            o_ref[...] = (y * x_ref[:, D_MODEL:D_MODEL + 1]).astype(BF16)

    @pl.when(jnp.logical_and(t >= nt_ref[0], f == n_f - 1))
    def _():
        o_ref[...] = jnp.zeros_like(o_ref)


def _experts(xs, tile_expert, n_tiles_used, wg_bf, wu_bf, wd_bf, tg):
    slots = xs.shape[0]
    n_f = MOE_F_BLOCKS
    tf = wg_bf.shape[2] // n_f

    def x_map(t, f, te, nt):
        return (jnp.maximum(jnp.minimum(t, nt[0] - 1), 0), 0)

    def f_of(t, f, nt):
        return jnp.where(t < nt[0], f, n_f - 1)

    return pl.pallas_call(
        functools.partial(_expert_kernel, n_f=n_f),
        out_shape=jax.ShapeDtypeStruct((slots, D_MODEL), BF16),
        grid_spec=pltpu.PrefetchScalarGridSpec(
            num_scalar_prefetch=2,
            grid=(slots // tg, n_f),
            in_specs=[
                pl.BlockSpec((tg, D_MODEL + XS_EXTRA), x_map),
                pl.BlockSpec((None, D_MODEL, tf), lambda t, f, te, nt: (te[t], 0, f_of(t, f, nt))),
                pl.BlockSpec((None, D_MODEL, tf), lambda t, f, te, nt: (te[t], 0, f_of(t, f, nt))),
                pl.BlockSpec((None, tf, D_MODEL), lambda t, f, te, nt: (te[t], f_of(t, f, nt), 0)),
            ],
            out_specs=pl.BlockSpec((tg, D_MODEL), lambda t, f, te, nt: (t, 0)),
            scratch_shapes=[pltpu.VMEM((tg, D_MODEL), BF16), pltpu.VMEM((tg, D_MODEL), F32)],
        ),
        compiler_params=_params(("arbitrary", "arbitrary")),
        name="moe_experts",
    )(tile_expert, n_tiles_used, xs, wg_bf, wu_bf, wd_bf)


def _combine_kernel(ws_ref, hs_ref, route_ref, y_ref, o_ref, win_ref, sem, *, tc, tiles_per_seq, n_steps):
    n = pl.program_id(0) * tiles_per_seq + pl.program_id(1)
    cur = n % 2

    def window_copy(step, b, e):
        start = pl.multiple_of(ws_ref[step * N_EXPERTS + e], WIN_ALIGN)
        return pltpu.make_async_copy(y_ref.at[pl.ds(start, COMBINE_WIN)],
                                     win_ref.at[b, pl.ds(e * COMBINE_WIN, COMBINE_WIN)], sem.at[b])

    def fetch(step, b):
        for e in range(N_EXPERTS):
            window_copy(step, b, e).start()

    @pl.when(n == 0)
    def _():
        fetch(0, 0)

    @pl.when(n + 1 < n_steps)
    def _():
        fetch(n + 1, 1 - cur)

    for e in range(N_EXPERTS):
        window_copy(n, cur, e).wait()

    route = route_ref[...]
    e1, e2, s1, s2 = (route[:, c:c + 1] for c in range(4))
    start1 = jnp.zeros_like(s1)
    start2 = jnp.zeros_like(s2)
    for e in range(N_EXPERTS):
        start = ws_ref[n * N_EXPERTS + e]
        start1 = jnp.where(e1 == e, start, start1)
        start2 = jnp.where(e2 == e, start, start2)
    col1 = e1 * COMBINE_WIN + s1 - start1
    col2 = e2 * COMBINE_WIN + s2 - start2
    pos = lax.broadcasted_iota(jnp.int32, (1, N_EXPERTS * COMBINE_WIN), 1)
    pick = jnp.where(jnp.logical_or(pos == col1, pos == col2), 1.0, 0.0).astype(BF16)
    o_ref[...] = hs_ref[...] + jnp.dot(pick, win_ref[cur], preferred_element_type=F32)


def _combine(hs3, route, y, win_start, seq):
    bsz = hs3.shape[0]
    tc = COMBINE_TOKENS
    tiles_per_seq = seq // tc
    kern = functools.partial(_combine_kernel, tc=tc, tiles_per_seq=tiles_per_seq, n_steps=bsz * tiles_per_seq)
    tok = lambda w: pl.BlockSpec((tc, w), lambda b, i, ws: (b * tiles_per_seq + i, 0))
    return pl.pallas_call(
        kern,
        out_shape=jax.ShapeDtypeStruct((bsz, seq, D_MODEL), F32),
        grid_spec=pltpu.PrefetchScalarGridSpec(
            num_scalar_prefetch=1,
            grid=(bsz, tiles_per_seq),
            in_specs=[
                pl.BlockSpec((None, tc, D_MODEL), lambda b, i, ws: (b, i + T_PAD // tc, 0)),
                tok(4),
                pl.BlockSpec(memory_space=pl.ANY),
            ],
            out_specs=pl.BlockSpec((None, tc, D_MODEL), lambda b, i, ws: (b, i, 0)),
            scratch_shapes=[pltpu.VMEM((2, N_EXPERTS * COMBINE_WIN, D_MODEL), BF16),
                            pltpu.SemaphoreType.DMA((2,))],
        ),
        compiler_params=_params(("arbitrary", "arbitrary")),
        name="moe_combine",
    )(win_start, hs3, route, y)


def _moe(hs, gain, router_w, w_gate, w_up, w_down, bsz, tp):
    seq = tp - T_PAD
    n_tok = bsz * seq
    tg = MOE_TILE
    idx, wts, cnt = _router(hs, gain, router_w, tp)
    sel = idx.reshape(bsz, tp, 128)[:, T_PAD:, :4].reshape(n_tok, 4)
    counts = cnt[0, :N_EXPERTS].astype(jnp.int32)
    padded = ((counts + tg - 1) // tg) * tg
    ends = jnp.cumsum(padded)
    starts = ends - padded
    experts = jnp.arange(N_EXPERTS, dtype=jnp.int32)[None, :]
    slot1 = jnp.sum(jnp.where(sel[:, 0:1] == experts, starts[None, :], 0), axis=1) + sel[:, 2]
    slot2 = jnp.sum(jnp.where(sel[:, 1:2] == experts, starts[None, :], 0), axis=1) + sel[:, 3]
    slot1, slot2 = slot1.astype(jnp.int32), slot2.astype(jnp.int32)
    n_slots = 2 * n_tok + N_EXPERTS * tg
    tile_start = jnp.arange(n_slots // tg, dtype=jnp.int32)[:, None] * tg
    tile_expert = jnp.minimum(jnp.sum((tile_start >= ends[None, :]).astype(jnp.int32), axis=1), N_EXPERTS - 1)
    n_tiles_used = (ends[-1:] // tg).astype(jnp.int32)
    tail = n_slots - tg * (1 + jnp.arange(N_EXPERTS, dtype=jnp.int32))
    zero_rows = jnp.concatenate([jnp.maximum(ends - tg, 0), tail]).astype(jnp.int32)

    tc = COMBINE_TOKENS
    first = jnp.minimum(
        jnp.min(jnp.where(sel[:, 0:1] == experts, slot1[:, None], n_slots).reshape(n_tok // tc, tc, N_EXPERTS), axis=1),
        jnp.min(jnp.where(sel[:, 1:2] == experts, slot2[:, None], n_slots).reshape(n_tok // tc, tc, N_EXPERTS), axis=1))
    win_start = jnp.where(first == n_slots, 0,
                          jnp.minimum(first // WIN_ALIGN * WIN_ALIGN, n_slots - COMBINE_WIN))
    route = jnp.concatenate([sel[:, :2], slot1[:, None], slot2[:, None]], axis=1)

    hs3 = hs.reshape(bsz, tp, D_MODEL)
    xs, (wg_bf, wu_bf, wd_bf) = _dispatch(hs3, wts.reshape(bsz, tp, 128), gain, slot1, slot2, zero_rows,
                                          n_slots, tg, (w_gate, w_up, w_down))
    y = _experts(xs, tile_expert.astype(jnp.int32), n_tiles_used, wg_bf, wu_bf, wd_bf, tg)
    return _combine(hs3, route, y, win_start.reshape(-1).astype(jnp.int32), seq)


def _permute_qk_cols(w):
    return w.reshape(D_MODEL, 2, ATT_HEADS, ATT_QK_DIM).transpose(0, 2, 1, 3).reshape(D_MODEL, 512)


def kernel(x, meta_tokens, norm1_gain, norm2_gain, w_in, q_norm_gain, k_norm_gain, diff_lambda,
           attn_sub_gain, rel_bias, conv_w, hgrn_lb_logits, hgrn_out_gain, w_branch, w_out,
           ffn_w_gate, ffn_w_up, ffn_w_down, router_w, moe_w_gate, moe_w_up, moe_w_down):
    bsz, seq, _ = x.shape
    depth = w_in.shape[0]
    tp = T_PAD + seq
    assert tp % ATT_TILE == 0 and depth == 2

    head = jnp.concatenate([jnp.zeros((PAD0, D_MODEL), x.dtype), meta_tokens.astype(x.dtype)], axis=0)
    hs = jnp.concatenate([jnp.broadcast_to(head[None], (bsz, T_PAD, D_MODEL)), x], axis=1)
    hs = hs.reshape(bsz * tp, D_MODEL)

    lb_all = jnp.cumsum(jax.nn.softmax(hgrn_lb_logits.astype(F32), axis=0), axis=0)
    lb_all = lb_all - lb_all[0]
    btab = _attn_bias_tables(rel_bias, ATT_TILE)

    def permuted(w):
        return jnp.concatenate([_permute_qk_cols(w[:, :512]), _permute_qk_cols(w[:, 512:1024])], axis=1)

    later = dict(branch0=w_branch[0], out0=w_out[0], gate=ffn_w_gate[0], up=ffn_w_up[0], down=ffn_w_down[0],
                 w_in1=w_in[1], branch1=w_branch[1], out1=w_out[1])
    bf = {}
    out = None
    for layer in range(depth):
        if layer == 0:
            w = w_in[0]
            w_bf = jnp.concatenate([permuted(w), w[:, 1024:]], axis=1).astype(BF16)
        else:
            w_bf = bf["w_in1"].at[:, :1024].set(permuted(bf["w_in1"]))
        qk_gain = jnp.concatenate([jnp.tile(q_norm_gain[layer].astype(F32), 8) * (ATT_QK_DIM ** -0.5 * LOG2E),
                                   jnp.tile(k_norm_gain[layer].astype(F32), 8)]).reshape(1, COL_TILE)
        proj = _inproj(hs, norm1_gain[layer], w_bf, qk_gain)
        proj3 = proj.reshape(bsz, tp, IN_COLS)

        lam_init = 0.8 - 0.6 * math.exp(-0.3 * layer)
        lp = diff_lambda[layer].astype(F32)
        lam = jnp.exp(jnp.sum(lp[0] * lp[1])) - jnp.exp(jnp.sum(lp[2] * lp[3])) + lam_init
        u_att, cast = _diff_attention(proj3, lam.reshape(1), attn_sub_gain[layer].astype(F32), btab,
                                      1.0 - lam_init, tuple(later.values()) if layer == 0 else ())
        if layer == 0:
            bf = dict(zip(later.keys(), cast))

        lb = lb_all[layer]
        u_hgrn = _hgrn(proj3, jnp.log(lb), jnp.log1p(-lb), hgrn_out_gain[layer].astype(F32))

        hs = _merge(hs, u_att.reshape(bsz * tp, 512), u_hgrn.reshape(bsz * tp, 512), proj,
                    conv_w[layer].astype(F32), bf["branch%d" % layer], bf["out%d" % layer], tp)

        j = layer // 2
        if layer % 2 == 0:
            hs = _ffn(hs, norm2_gain[layer], bf["gate"], bf["up"], bf["down"])
        else:
            out = _moe(hs, norm2_gain[layer], router_w[j], moe_w_gate[j], moe_w_up[j], moe_w_down[j], bsz, tp)
    return out
```

```python
import functools
import math

import numpy as np
import jax
import jax.numpy as jnp
from jax import lax
from jax.experimental import pallas as pl
from jax.experimental.pallas import tpu as pltpu

F32 = jnp.float32
BF16 = jnp.bfloat16

D_MODEL = 1024
N_META = 16
EPS = 1e-6
ATT_HEADS = 4
ATT_QK_DIM = 64
ATT_V_DIM = 128
REL_BUCKETS = 32
REL_MAX_DIST = 128
CONV_K = 3
HGRN_HEADS = 4
HGRN_D = 128
N_EXPERTS = 8
IN_COLS = 8192

T_PAD = 128
PAD0 = T_PAD - N_META
ATT_TILE = 384
VT_ONES = 16
LOG2E = math.log2(math.e)
HGRN_CHUNK = 128
HGRN_SUB = 16
COL_TILE = 1024
GROUP_MEAN_WIDTH = 256
MOE_TILE = 512
MOE_F_BLOCKS = 2
XS_EXTRA = 128
COMBINE_TOKENS = 128
WIN_ALIGN = 16
COMBINE_WIN = COMBINE_TOKENS + WIN_ALIGN
MASK = -1e30
VMEM_LIMIT = 56 * 1024 * 1024

BLK_Q, BLK_K, BLK_V = 0, 4, 8
BLK_RQ, BLK_RF, BLK_RI, BLK_RG = 24, 28, 32, 36
BLK_CB, BLK_CC, BLK_CH = 3, 4, 5
BLK_GATE = 5


def _row_tile(rows, target):
    n = rows // 128
    best = 1
    for d in range(1, n + 1):
        if n % d == 0 and d * 128 <= target:
            best = d
    return best * 128


def _params(sem, vmem=VMEM_LIMIT):
    return pltpu.CompilerParams(dimension_semantics=sem, vmem_limit_bytes=vmem)


def _inproj_kernel(x_ref, g_ref, w_ref, qkg_ref, gm_ref, o_ref, xn_ref):
    j = pl.program_id(1)

    @pl.when(j == 0)
    def _():
        x = x_ref[...]
        ms = jnp.mean(x * x, axis=-1, keepdims=True)
        xn_ref[...] = (x * lax.rsqrt(ms + EPS) * g_ref[...]).astype(BF16)

    acc = jnp.dot(xn_ref[...], w_ref[...], preferred_element_type=F32)

    @pl.when(j == 0)
    def _():
        sq = acc * acc
        hi = sq.astype(BF16)
        lo = (sq - hi.astype(F32)).astype(BF16)
        gm = gm_ref[...]
        width = gm.shape[0]
        ms = jnp.concatenate(
            [jnp.dot(hi[:, c:c + width], gm, preferred_element_type=F32)
             + jnp.dot(lo[:, c:c + width], gm, preferred_element_type=F32)
             for c in range(0, COL_TILE, width)], axis=1)
        o_ref[...] = (acc * lax.rsqrt(ms + EPS) * qkg_ref[...]).astype(BF16)

    @pl.when(j > 0)
    def _():
        o_ref[...] = acc.astype(BF16)


def _inproj(hs, gain, w_bf, qk_gain):
    rows = hs.shape[0]
    tm = _row_tile(rows, 1536)
    n_col = IN_COLS // COL_TILE
    assert COL_TILE == 4 * ATT_HEADS * ATT_QK_DIM
    grp = np.arange(GROUP_MEAN_WIDTH) // ATT_QK_DIM
    gm = jnp.asarray((grp[:, None] == grp[None, :]).astype(np.float32) / ATT_QK_DIM, BF16)
    return pl.pallas_call(
        _inproj_kernel,
        out_shape=jax.ShapeDtypeStruct((rows, IN_COLS), BF16),
        grid=(rows // tm, n_col),
        in_specs=[
            pl.BlockSpec((tm, D_MODEL), lambda i, j: (i, 0)),
            pl.BlockSpec((1, D_MODEL), lambda i, j: (0, 0)),
            pl.BlockSpec((D_MODEL, COL_TILE), lambda i, j: (0, j)),
            pl.BlockSpec((1, COL_TILE), lambda i, j: (0, 0)),
            pl.BlockSpec((GROUP_MEAN_WIDTH, GROUP_MEAN_WIDTH), lambda i, j: (0, 0)),
        ],
        out_specs=pl.BlockSpec((tm, COL_TILE), lambda i, j: (i, j)),
        scratch_shapes=[pltpu.VMEM((tm, D_MODEL), BF16)],
        compiler_params=_params(("parallel", "arbitrary")),
        name="inproj",
    )(hs, gain.reshape(1, D_MODEL), w_bf, qk_gain, gm)


def _rel_bucket_table(n_max):
    n = np.arange(n_max, dtype=np.int64)
    max_exact = REL_BUCKETS // 2
    nf = np.maximum(n, 1).astype(np.float32)
    large = max_exact + (np.log(nf / np.float32(max_exact)) / np.float32(math.log(REL_MAX_DIST / max_exact))
                         * np.float32(REL_BUCKETS - max_exact)).astype(np.int32)
    large = np.minimum(large, REL_BUCKETS - 1)
    return np.where(n < max_exact, n, large).astype(np.int32)


def _attn_bias_tables(rel_bias, t):
    bucket = _rel_bucket_table(2 * t)
    assert np.all(bucket[t + 1:] == REL_BUCKETS - 1) and np.all(np.diff(bucket) >= 0)
    first_dist = tuple(int(np.searchsorted(bucket, b, side="left")) for b in range(REL_BUCKETS))
    return pl.pallas_call(
        functools.partial(_bias_kernel, t=t, first_dist=first_dist),
        out_shape=jax.ShapeDtypeStruct((ATT_HEADS, 6, t, t), F32),
        grid=(ATT_HEADS,),
        in_specs=[pl.BlockSpec(memory_space=pltpu.SMEM)],
        out_specs=pl.BlockSpec((None, 6, t, t), lambda h: (h, 0, 0, 0)),
        compiler_params=_params(("parallel",)),
        name="attn_bias",
    )(rel_bias.astype(F32))


def _bias_kernel(rb_ref, o_ref, *, t, first_dist):
    h = pl.program_id(0)
    key = lax.broadcasted_iota(jnp.int32, (t, t), 0)
    qry = lax.broadcasted_iota(jnp.int32, (t, t), 1)
    far = rb_ref[REL_BUCKETS - 1, h]

    def table(n):
        val = jnp.full((t, t), rb_ref[0, h] - far, F32)
        for b in range(1, REL_BUCKETS):
            val = jnp.where(n >= first_dist[b], rb_ref[b, h] - far, val)
        return val * LOG2E

    n0 = qry - key
    diag = jnp.where(n0 >= 0, table(n0), MASK)
    near = table(n0 + t)
    zero = jnp.zeros((t, t), F32)
    for kind, tab in enumerate((diag, near, zero)):
        o_ref[kind] = tab
        o_ref[kind + 3] = jnp.where(key < PAD0, MASK, tab)


def _cast_specs(arrays, n_steps, step_of):
    flat = [w.reshape(-1, w.shape[-1]) for w in arrays]
    assert all(w.shape[0] % (n_steps * WIN_ALIGN) == 0 for w in flat)
    specs = [pl.BlockSpec((w.shape[0] // n_steps, w.shape[1]), lambda *idx: (step_of(*idx), 0)) for w in flat]
    return flat, specs, [jax.ShapeDtypeStruct(w.shape, BF16) for w in flat]


def _attn_kernel(lam_ref, q_ref, k_ref, v_ref, bt_ref, sg_ref, *refs, t, out_scale, n_cast):
    cast_in, (o_ref, *cast_out), (m_ref, acc_ref, vt_ref) = refs[:n_cast], refs[n_cast:2 * n_cast + 1], refs[-3:]
    for src, dst in zip(cast_in, cast_out):
        dst[...] = src[...].astype(BF16)
    n_t = vt_ref.shape[0]
    for j in range(n_t):
        vt_ref[j, :ATT_V_DIM, :] = v_ref[j * t:(j + 1) * t, :].astype(F32).T.astype(BF16)
        vt_ref[j, ATT_V_DIM:, :] = jnp.ones((VT_ONES, t), BF16)
    lane = lax.broadcasted_iota(jnp.int32, (1, 2 * ATT_QK_DIM), 1)
    nt = (((1,), (1,)), ((), ()))

    def tile_rows(i):
        return pl.ds(pl.multiple_of(i * t, t), t)

    def scores(i, j):
        q = q_ref[tile_rows(i), :]
        zero = jnp.zeros_like(q)
        q_cat = jnp.concatenate([jnp.where(lane < ATT_QK_DIM, q, zero), jnp.where(lane >= ATT_QK_DIM, q, zero)],
                                axis=0)
        bias = bt_ref[jnp.minimum(i - j, 2) + jnp.where(j == 0, 3, 0)]
        s = (lax.dot_general(k_ref[tile_rows(j), :], q_cat, nt, preferred_element_type=F32)
             + jnp.concatenate([bias, bias], axis=1))
        return s, jnp.max(s, axis=0, keepdims=True)

    def consume(j, scored):
        s, s_max = scored
        m_old = m_ref[...]
        m_new = jnp.maximum(m_old, s_max)
        m_ref[...] = m_new
        p = jnp.exp2(s - m_new).astype(BF16)
        acc_ref[...] = (jnp.exp2(m_old - m_new) * acc_ref[...]
                        + jnp.dot(vt_ref[j], p, preferred_element_type=F32))

    def reset():
        m_ref[...] = jnp.full(m_ref.shape, MASK, F32)
        acc_ref[...] = jnp.zeros(acc_ref.shape, F32)

    def finish(i):
        acc = acc_ref[...]
        a1, a2 = acc[:, :t], acc[:, t:]
        o_t = (a1[:ATT_V_DIM] / a1[ATT_V_DIM:ATT_V_DIM + 1]
               - lam_ref[0] * (a2[:ATT_V_DIM] / a2[ATT_V_DIM:ATT_V_DIM + 1]))
        o = o_t.T
        ms = jnp.mean(o * o, axis=-1, keepdims=True)
        y = o * lax.rsqrt(ms + EPS) * (sg_ref[...] * out_scale)
        row = i * t + lax.broadcasted_iota(jnp.int32, (t, 1), 0)
        o_ref[tile_rows(i), :] = jnp.where(row >= PAD0, y, 0.0).astype(BF16)
        reset()

    def step(_, carry):
        i, j, s = carry
        last = j == i
        ni = jnp.where(last, i + 1, i)
        nj = jnp.where(last, 0, j + 1)
        nxt = scores(jnp.minimum(ni, n_t - 1), nj)
        consume(j, s)

        @pl.when(last)
        def _():
            finish(i)

        return ni, nj, nxt

    reset()
    first = jnp.int32(0)
    lax.fori_loop(0, n_t * (n_t + 1) // 2, step, (first, first, scores(first, first)))


def _diff_attention(proj3, lam, sub_gain, btab, out_scale, casts=()):
    bsz, tp, _ = proj3.shape
    t = ATT_TILE
    n_t = tp // t
    rows = ATT_V_DIM + VT_ONES
    kern = functools.partial(_attn_kernel, t=t, out_scale=out_scale, n_cast=len(casts))
    seq = lambda blk: pl.BlockSpec((None, tp, 128), lambda b, h: (b, 0, blk + h))
    flat, cast_specs, cast_shapes = _cast_specs(casts, bsz * ATT_HEADS, lambda b, h: b * ATT_HEADS + h)
    out = pl.pallas_call(
        kern,
        out_shape=[jax.ShapeDtypeStruct((bsz, tp, ATT_HEADS * ATT_V_DIM), BF16)] + cast_shapes,
        grid=(bsz, ATT_HEADS),
        in_specs=[
            pl.BlockSpec(memory_space=pltpu.SMEM),
            seq(BLK_Q), seq(BLK_K), seq(BLK_V),
            pl.BlockSpec((None, 6, t, t), lambda b, h: (h, 0, 0, 0)),
            pl.BlockSpec((1, ATT_V_DIM), lambda b, h: (0, 0)),
        ] + cast_specs,
        out_specs=[pl.BlockSpec((None, tp, 128), lambda b, h: (b, 0, h))] + cast_specs,
        scratch_shapes=[pltpu.VMEM((1, 2 * t), F32), pltpu.VMEM((rows, 2 * t), F32),
                        pltpu.VMEM((n_t, rows, t), BF16)],
        compiler_params=_params(("parallel", "parallel")),
        name="diff_attn",
    )(lam, proj3, proj3, proj3, btab, sub_gain.reshape(1, ATT_V_DIM), *flat)
    return out[0], [o.reshape(w.shape) for o, w in zip(out[1:], casts)]


def _split3(x):
    h1 = x.astype(BF16)
    r1 = x - h1.astype(F32)
    h2 = r1.astype(BF16)
    h3 = (r1 - h2.astype(F32)).astype(BF16)
    return h1, h2, h3


def _hgrn_kernel(q_ref, f_ref, i_ref, g_ref, la_ref, l1m_ref, og_ref, o_ref):
    c_len, sub = HGRN_CHUNK, HGRN_SUB
    half = sub // 2
    n_chunk = q_ref.shape[0] // c_len
    la, l1m, og = la_ref[...], l1m_ref[...], og_ref[...]
    rr = lax.broadcasted_iota(jnp.int32, (c_len, c_len), 0)
    cc = lax.broadcasted_iota(jnp.int32, (c_len, c_len), 1)
    tri = jnp.where(cc <= rr, 1.0, 0.0).astype(BF16)
    row8 = lax.broadcasted_iota(jnp.int32, (half, 1), 0)
    lane8 = lax.broadcasted_iota(jnp.int32, (half, c_len), 1)
    nt = (((1,), (1,)), ((), ()))

    def chunk(c, st):
        r0 = pl.multiple_of(c * c_len, c_len)
        rows = pl.ds(r0, c_len)
        z = f_ref[rows, :].astype(F32)
        qh = q_ref[rows, :].astype(F32)
        qh = qh * jax.nn.sigmoid(qh)
        v = i_ref[rows, :]
        gate = g_ref[rows, :].astype(F32)
        sp = jnp.log(1.0 + jnp.exp(-jnp.abs(z)))
        bb = l1m + jnp.minimum(z, 0.0) - sp
        log_f = jnp.maximum(la, bb) + jnp.log(1.0 + jnp.exp(-jnp.abs(la - bb)))
        valid = (r0 + lax.broadcasted_iota(jnp.int32, (c_len, 1), 0)) >= PAD0
        log_k = jnp.where(valid, l1m + jnp.minimum(-z, 0.0) - sp, -jnp.inf)
        g = sum(jnp.dot(tri, part, preferred_element_type=F32) for part in _split3(log_f))
        ck = log_k - g
        o_inter = lax.dot_general((qh * jnp.exp(g)).astype(BF16), st.astype(BF16), nt,
                                  preferred_element_type=F32)
        a_rows = []
        for a in range(c_len // sub):
            lo = a * sub
            ga = (g[lo:lo + half, :], g[lo + half:lo + sub, :])
            qa = (qh[lo:lo + half, :], qh[lo + half:lo + sub, :])
            if a == 0:
                blk = [jnp.zeros((half, c_len), F32)] * 2
            else:
                gs = g[lo - 1:lo, :]
                qd = (qh[lo:lo + sub, :] * jnp.exp(g[lo:lo + sub, :] - gs)).astype(BF16)
                kd = jnp.exp(jnp.minimum(gs - g[:lo, :], 0.0) + log_k[:lo, :]).astype(BF16)
                kd = jnp.concatenate([kd, jnp.zeros((c_len - lo, HGRN_D), BF16)], axis=0)
                a_off = lax.dot_general(qd, kd, nt, preferred_element_type=F32)
                blk = [a_off[:half, :], a_off[half:, :]]
            for s in range(sub):
                crow = ck[lo + s:lo + s + 1, :]
                for hh in range(s // half, 2):
                    col = jnp.sum(qa[hh] * jnp.exp(ga[hh] + crow), axis=-1, keepdims=True)
                    blk[hh] = jnp.where(lane8 == lo + s, col, blk[hh])
            for hh in range(2):
                a_rows.append(jnp.where(lane8 <= lo + hh * half + row8, blk[hh], 0.0))
        a_full = jnp.concatenate(a_rows, axis=0).astype(BF16)
        o = o_inter + jnp.dot(a_full, v, preferred_element_type=F32)
        g_last = g[c_len - 1:c_len, :]
        kd = jnp.exp(g_last - g + log_k).astype(BF16)
        st = st * jnp.exp(g_last) + lax.dot_general(v, kd, (((0,), (0,)), ((), ())),
                                                    preferred_element_type=F32)
        ms = jnp.mean(o * o, axis=-1, keepdims=True)
        y = o * lax.rsqrt(ms + EPS) * og * (gate * jax.nn.sigmoid(gate))
        o_ref[rows, :] = y.astype(BF16)
        return st

    lax.fori_loop(0, n_chunk, chunk, jnp.zeros((HGRN_D, HGRN_D), F32), unroll=11)


def _hgrn(proj3, log_lb, log1m_lb, out_gain):
    bsz, tp, _ = proj3.shape
    seq = lambda blk: pl.BlockSpec((None, tp, 128), lambda b, h: (b, 0, blk + h))
    chan = pl.BlockSpec((None, 1, HGRN_D), lambda b, h: (h, 0, 0))
    return pl.pallas_call(
        _hgrn_kernel,
        out_shape=jax.ShapeDtypeStruct((bsz, tp, HGRN_HEADS * HGRN_D), BF16),
        grid=(bsz, HGRN_HEADS),
        in_specs=[seq(BLK_RQ), seq(BLK_RF), seq(BLK_RI), seq(BLK_RG), chan, chan,
                  pl.BlockSpec((1, HGRN_D), lambda b, h: (0, 0))],
        out_specs=pl.BlockSpec((None, tp, 128), lambda b, h: (b, 0, h)),
        compiler_params=_params(("parallel", "parallel")),
        name="hgrn2",
    )(proj3, proj3, proj3, proj3,
      log_lb.reshape(HGRN_HEADS, 1, HGRN_D), log1m_lb.reshape(HGRN_HEADS, 1, HGRN_D),
      out_gain.reshape(1, HGRN_D))


def _merge_kernel(hs_ref, ua_ref, ur_ref, cb_ref, cc_ref, ch_ref, pc_ref, ph_ref,
                  g0_ref, g1_ref, g2_ref, cw_ref, wb_ref, wo_ref, o_ref, *, tm, tiles_per_seq):
    i = pl.program_id(0)
    row = (i % tiles_per_seq) * tm + lax.broadcasted_iota(jnp.int32, (tm, 1), 0)
    valid = row >= PAD0
    z = jnp.where(valid, cc_ref[...].astype(F32) * ch_ref[...].astype(F32), 0.0)
    halo_row = (i % tiles_per_seq) * tm - 8 + lax.broadcasted_iota(jnp.int32, (8, 1), 0)
    zp = jnp.where(halo_row >= PAD0, pc_ref[...].astype(F32) * ph_ref[...].astype(F32), 0.0)
    zz = jnp.concatenate([zp, z], axis=0)
    cw = cw_ref[...]
    y = (cw[2:3, :] * z + cw[1:2, :] * zz[7:7 + tm, :] + cw[0:1, :] * zz[6:6 + tm, :])
    u_conv = jnp.where(valid, cb_ref[...].astype(F32) * y, 0.0).astype(BF16)
    mixed = jnp.zeros((tm, D_MODEL), F32)
    for n, (u, g_ref) in enumerate(((ua_ref[...], g0_ref), (u_conv, g1_ref), (ur_ref[...], g2_ref))):
        up = jnp.dot(u, wb_ref[n], preferred_element_type=F32)
        mixed = mixed + jax.nn.sigmoid(g_ref[...].astype(F32)) * up
    o_ref[...] = hs_ref[...] + jnp.dot(mixed.astype(BF16), wo_ref[...], preferred_element_type=F32)


def _merge(hs, u_att, u_hgrn, proj, conv_w, wb_bf, wo_bf, tp):
    rows = hs.shape[0]
    tm = _row_tile(tp, 384)
    kern = functools.partial(_merge_kernel, tm=tm, tiles_per_seq=tp // tm)
    row_blk = lambda w, blk: pl.BlockSpec((tm, w), lambda i: (i, blk))
    halo = lambda blk: pl.BlockSpec((8, 512), lambda i: (jnp.maximum(i * (tm // 8) - 1, 0), blk))
    const = lambda shape: pl.BlockSpec(shape, lambda i: (0,) * len(shape))
    return pl.pallas_call(
        kern,
        out_shape=jax.ShapeDtypeStruct((rows, D_MODEL), F32),
        grid=(rows // tm,),
        in_specs=[row_blk(D_MODEL, 0), row_blk(512, 0), row_blk(512, 0),
                  row_blk(512, BLK_CB), row_blk(512, BLK_CC), row_blk(512, BLK_CH),
                  halo(BLK_CC), halo(BLK_CH),
                  row_blk(1024, BLK_GATE), row_blk(1024, BLK_GATE + 1), row_blk(1024, BLK_GATE + 2),
                  const((CONV_K, 512)), const((3, 512, D_MODEL)), const((D_MODEL, D_MODEL))],
        out_specs=row_blk(D_MODEL, 0),
        compiler_params=_params(("parallel",)),
        name="merge",
    )(hs, u_att, u_hgrn, proj, proj, proj, proj, proj, proj, proj, proj, conv_w, wb_bf, wo_bf)


def _ffn_kernel(hs_ref, g_ref, wg_ref, wu_ref, wd_ref, o_ref):
    x = hs_ref[...]
    ms = jnp.mean(x * x, axis=-1, keepdims=True)
    hn = (x * lax.rsqrt(ms + EPS) * g_ref[...]).astype(BF16)
    a = jnp.dot(hn, wg_ref[...], preferred_element_type=F32)
    u = jnp.dot(hn, wu_ref[...], preferred_element_type=F32)
    h = (a * jax.nn.sigmoid(a) * u).astype(BF16)
    o_ref[...] = x + jnp.dot(h, wd_ref[...], preferred_element_type=F32)


def _ffn(hs, gain, wg_bf, wu_bf, wd_bf):
    rows = hs.shape[0]
    d_ff = wg_bf.shape[1]
    tm = _row_tile(rows, 768)
    const = lambda shape: pl.BlockSpec(shape, lambda i: (0, 0), pipeline_mode=pl.Buffered(1))
    return pl.pallas_call(
        _ffn_kernel,
        out_shape=jax.ShapeDtypeStruct((rows, D_MODEL), F32),
        grid=(rows // tm,),
        in_specs=[pl.BlockSpec((tm, D_MODEL), lambda i: (i, 0)), const((1, D_MODEL)),
                  const((D_MODEL, d_ff)), const((D_MODEL, d_ff)), const((d_ff, D_MODEL))],
        out_specs=pl.BlockSpec((tm, D_MODEL), lambda i: (i, 0)),
        compiler_params=_params(("parallel",)),
        name="ffn_dense",
    )(hs, gain.reshape(1, D_MODEL), wg_bf, wu_bf, wd_bf)


def _router_kernel(hs_ref, g_ref, rw_ref, idx_ref, wt_ref, cnt_ref, carry_ref, *, tm, tiles_per_seq):
    i = pl.program_id(0)

    @pl.when(i == 0)
    def _():
        carry_ref[...] = jnp.zeros_like(carry_ref)

    x = hs_ref[...]
    ms = jnp.mean(x * x, axis=-1, keepdims=True)
    hn = x * lax.rsqrt(ms + EPS) * g_ref[...]
    hn_hi = hn.astype(BF16)
    hn_lo = (hn - hn_hi.astype(F32)).astype(BF16)
    logits = (jnp.dot(hn_hi, rw_ref[0], preferred_element_type=F32)
              + jnp.dot(hn_lo, rw_ref[0], preferred_element_type=F32)
              + jnp.dot(hn_hi, rw_ref[1], preferred_element_type=F32))
    lane = lax.broadcasted_iota(jnp.int32, logits.shape, 1)
    lane_f = lane.astype(F32)
    logits = jnp.where(lane < N_EXPERTS, logits, -jnp.inf)
    m1 = jnp.max(logits, axis=-1, keepdims=True)
    i1 = jnp.min(jnp.where(logits == m1, lane_f, 128.0), axis=-1, keepdims=True)
    rest = jnp.where(lane_f == i1, -jnp.inf, logits)
    m2 = jnp.max(rest, axis=-1, keepdims=True)
    i2 = jnp.min(jnp.where(rest == m2, lane_f, 128.0), axis=-1, keepdims=True)
    e2 = jnp.exp(m2 - m1)
    w1 = 1.0 / (1.0 + e2)
    w2 = e2 / (1.0 + e2)
    row = (i % tiles_per_seq) * tm + lax.broadcasted_iota(jnp.int32, (tm, 1), 0)
    real = row >= T_PAD
    hot1 = jnp.where(jnp.logical_and(real, lane_f == i1), 1.0, 0.0)
    hot2 = jnp.where(jnp.logical_and(real, lane_f == i2), 1.0, 0.0)
    both = hot1 + hot2
    rr = lax.broadcasted_iota(jnp.int32, (tm, tm), 0)
    cc = lax.broadcasted_iota(jnp.int32, (tm, tm), 1)
    earlier = jnp.where(cc < rr, 1.0, 0.0).astype(BF16)
    before = carry_ref[...] + jnp.dot(earlier, both.astype(BF16), preferred_element_type=F32)
    r1 = jnp.sum(before * hot1, axis=-1, keepdims=True)
    r2 = jnp.sum(before * hot2, axis=-1, keepdims=True)
    carry_ref[...] += jnp.sum(both, axis=0, keepdims=True)
    cnt_ref[...] = carry_ref[...]
    packed = jnp.where(lane == 0, i1, jnp.where(lane == 1, i2, jnp.where(lane == 2, r1, jnp.where(lane == 3, r2, 0.0))))
    idx_ref[...] = packed.astype(jnp.int32)
    wt_ref[...] = jnp.where(lane == 0, w1, jnp.where(lane == 1, w2, 0.0))


def _router(hs, gain, router_w, tp):
    rows = hs.shape[0]
    tm = _row_tile(tp, 384)
    rw = jnp.zeros((D_MODEL, 128), F32).at[:, :N_EXPERTS].set(router_w.astype(F32))
    rw_hi = rw.astype(BF16)
    rw = jnp.stack([rw_hi, (rw - rw_hi.astype(F32)).astype(BF16)])
    blk = pl.BlockSpec((tm, 128), lambda i: (i, 0))
    return pl.pallas_call(
        functools.partial(_router_kernel, tm=tm, tiles_per_seq=tp // tm),
        out_shape=(jax.ShapeDtypeStruct((rows, 128), jnp.int32), jax.ShapeDtypeStruct((rows, 128), F32),
                   jax.ShapeDtypeStruct((1, 128), F32)),
        grid=(rows // tm,),
        in_specs=[pl.BlockSpec((tm, D_MODEL), lambda i: (i, 0)),
                  pl.BlockSpec((1, D_MODEL), lambda i: (0, 0)),
                  pl.BlockSpec((2, D_MODEL, 128), lambda i: (0, 0, 0))],
        out_specs=(blk, blk, pl.BlockSpec((1, 128), lambda i: (0, 0))),
        scratch_shapes=[pltpu.VMEM((1, 128), F32)],
        compiler_params=_params(("arbitrary",)),
        name="moe_router",
    )(hs, gain.reshape(1, D_MODEL), rw)


def _dispatch_kernel(s1_ref, s2_ref, zr_ref, hs_ref, wt_ref, g_ref, wg_ref, wu_ref, wd_ref,
                     xs_ref, wg_out, wu_out, wd_out, rows_ref, zero_ref, sem, zsem,
                     *, td, tg, tiles_per_seq, n_steps):
    for src, dst in ((wg_ref, wg_out), (wu_ref, wu_out), (wd_ref, wd_out)):
        dst[...] = src[...].astype(BF16)
    n = pl.program_id(0) * tiles_per_seq + pl.program_id(1)
    buf = n % 2
    base = n * td

    def row_copy(r, which):
        slot = (s1_ref, s2_ref)[which][base + r]
        return pltpu.make_async_copy(rows_ref.at[buf, which, pl.ds(r, 1)], xs_ref.at[pl.ds(slot, 1)],
                                     sem.at[buf])

    def wait_rows(b):
        def body(r, c):
            pltpu.make_async_copy(rows_ref.at[b, 0, pl.ds(0, 1)], xs_ref.at[pl.ds(0, 1)], sem.at[b]).wait()
            return c
        lax.fori_loop(0, 2 * td, body, 0)

    @pl.when(n == 0)
    def _():
        zero_ref[...] = jnp.zeros_like(zero_ref)

        def fill(e):
            return pltpu.make_async_copy(zero_ref, xs_ref.at[pl.ds(pl.multiple_of(zr_ref[e], tg), tg)], zsem)

        for e in range(N_EXPERTS, 2 * N_EXPERTS):
            fill(e).start()
        for e in range(N_EXPERTS, 2 * N_EXPERTS):
            fill(e).wait()
        for action in ("start", "wait"):
            getattr(fill(0), action)()
            for e in range(1, N_EXPERTS):
                @pl.when(zr_ref[e] != zr_ref[e - 1])
                def _():
                    getattr(fill(e), action)()

    @pl.when(n >= 2)
    def _():
        wait_rows(buf)

    x = hs_ref[...]
    ms = jnp.mean(x * x, axis=-1, keepdims=True)
    hn = x * lax.rsqrt(ms + EPS) * g_ref[...]
    wt = wt_ref[...]
    for which in range(2):
        rows_ref[buf, which, :, :D_MODEL] = hn
        rows_ref[buf, which, :, D_MODEL:] = jnp.broadcast_to(wt[:, which:which + 1], (td, XS_EXTRA))

    def issue(r, c):
        row_copy(r, 0).start()
        row_copy(r, 1).start()
        return c

    lax.fori_loop(0, td, issue, 0)

    @pl.when(n == n_steps - 1)
    def _():
        wait_rows(buf)
        if n_steps > 1:
            wait_rows(1 - buf)


def _dispatch(hs3, wts3, gain, slot1, slot2, zero_rows, n_slots, tg, expert_weights):
    bsz, tp, _ = hs3.shape
    td = T_PAD
    tiles_per_seq = (tp - T_PAD) // td
    n_steps = bsz * tiles_per_seq
    width = D_MODEL + XS_EXTRA
    kern = functools.partial(_dispatch_kernel, td=td, tg=tg, tiles_per_seq=tiles_per_seq, n_steps=n_steps)
    flat = [w.reshape(-1, w.shape[-1]) for w in expert_weights]
    assert all(w.shape[0] % (n_steps * WIN_ALIGN) == 0 for w in flat)
    w_spec = lambda w: pl.BlockSpec((w.shape[0] // n_steps, w.shape[1]),
                                    lambda b, i, *_: (b * tiles_per_seq + i, 0))
    out = pl.pallas_call(
        kern,
        out_shape=[jax.ShapeDtypeStruct((n_slots, width), F32)]
        + [jax.ShapeDtypeStruct(w.shape, BF16) for w in flat],
        grid_spec=pltpu.PrefetchScalarGridSpec(
            num_scalar_prefetch=3,
            grid=(bsz, tiles_per_seq),
            in_specs=[pl.BlockSpec((None, td, D_MODEL), lambda b, i, *_: (b, i + 1, 0)),
                      pl.BlockSpec((None, td, 128), lambda b, i, *_: (b, i + 1, 0)),
                      pl.BlockSpec((1, D_MODEL), lambda b, i, *_: (0, 0))] + [w_spec(w) for w in flat],
            out_specs=[pl.BlockSpec(memory_space=pl.ANY)] + [w_spec(w) for w in flat],
            scratch_shapes=[pltpu.VMEM((2, 2, td, width), F32), pltpu.VMEM((tg, width), F32),
                            pltpu.SemaphoreType.DMA((2,)), pltpu.SemaphoreType.DMA],
        ),
        compiler_params=_params(("arbitrary", "arbitrary")),
        name="moe_dispatch",
    )(slot1, slot2, zero_rows, hs3, wts3, gain.reshape(1, D_MODEL), *flat)
    return out[0], [o.reshape(w.shape) for o, w in zip(out[1:], expert_weights)]


def _expert_kernel(te_ref, nt_ref, x_ref, wg_ref, wu_ref, wd_ref, o_ref, hn_ref, acc_ref, *, n_f):
    t = pl.program_id(0)
    f = pl.program_id(1)

    @pl.when(t < nt_ref[0])
    def _():
        @pl.when(f == 0)
        def _():
            hn_ref[...] = x_ref[:, :D_MODEL].astype(BF16)

        hn = hn_ref[...]
        a = jnp.dot(hn, wg_ref[...], preferred_element_type=F32)
        u = jnp.dot(hn, wu_ref[...], preferred_element_type=F32)
        h = (a * jax.nn.sigmoid(a) * u).astype(BF16)

        @pl.when(f == 0)
        def _():
            acc_ref[...] = jnp.dot(h, wd_ref[...], preferred_element_type=F32)

        @pl.when(jnp.logical_and(f > 0, f < n_f - 1))
        def _():
            acc_ref[...] += jnp.dot(h, wd_ref[...], preferred_element_type=F32)

        @pl.when(f == n_f - 1)
        def _():
            y = acc_ref[...] + jnp.dot(h, wd_ref[...], preferred_element_type=F32)
            o_ref[...] = (y * x_ref[:, D_MODEL:D_MODEL + 1]).astype(BF16)

    @pl.when(jnp.logical_and(t >= nt_ref[0], f == n_f - 1))
    def _():
        o_ref[...] = jnp.zeros_like(o_ref)


def _experts(xs, tile_expert, n_tiles_used, wg_bf, wu_bf, wd_bf, tg):
    slots = xs.shape[0]
    n_f = MOE_F_BLOCKS
    tf = wg_bf.shape[2] // n_f

    def x_map(t, f, te, nt):
        return (jnp.maximum(jnp.minimum(t, nt[0] - 1), 0), 0)

    def f_of(t, f, nt):
        return jnp.where(t < nt[0], f, n_f - 1)

    return pl.pallas_call(
        functools.partial(_expert_kernel, n_f=n_f),
        out_shape=jax.ShapeDtypeStruct((slots, D_MODEL), BF16),
        grid_spec=pltpu.PrefetchScalarGridSpec(
            num_scalar_prefetch=2,
            grid=(slots // tg, n_f),
            in_specs=[
                pl.BlockSpec((tg, D_MODEL + XS_EXTRA), x_map),
                pl.BlockSpec((None, D_MODEL, tf), lambda t, f, te, nt: (te[t], 0, f_of(t, f, nt))),
                pl.BlockSpec((None, D_MODEL, tf), lambda t, f, te, nt: (te[t], 0, f_of(t, f, nt))),
                pl.BlockSpec((None, tf, D_MODEL), lambda t, f, te, nt: (te[t], f_of(t, f, nt), 0)),
            ],
            out_specs=pl.BlockSpec((tg, D_MODEL), lambda t, f, te, nt: (t, 0)),
            scratch_shapes=[pltpu.VMEM((tg, D_MODEL), BF16), pltpu.VMEM((tg, D_MODEL), F32)],
        ),
        compiler_params=_params(("arbitrary", "arbitrary")),
        name="moe_experts",
    )(tile_expert, n_tiles_used, xs, wg_bf, wu_bf, wd_bf)


def _combine_kernel(ws_ref, hs_ref, route_ref, y_ref, o_ref, win_ref, sem, *, tc, tiles_per_seq, n_steps):
    n = pl.program_id(0) * tiles_per_seq + pl.program_id(1)
    cur = n % 2

    def window_copy(step, b, e):
        start = pl.multiple_of(ws_ref[step * N_EXPERTS + e], WIN_ALIGN)
        return pltpu.make_async_copy(y_ref.at[pl.ds(start, COMBINE_WIN)],
                                     win_ref.at[b, pl.ds(e * COMBINE_WIN, COMBINE_WIN)], sem.at[b])

    def fetch(step, b):
        for e in range(N_EXPERTS):
            window_copy(step, b, e).start()

    @pl.when(n == 0)
    def _():
        fetch(0, 0)

    @pl.when(n + 1 < n_steps)
    def _():
        fetch(n + 1, 1 - cur)

    for e in range(N_EXPERTS):
        window_copy(n, cur, e).wait()

    route = route_ref[...]
    e1, e2, s1, s2 = (route[:, c:c + 1] for c in range(4))
    start1 = jnp.zeros_like(s1)
    start2 = jnp.zeros_like(s2)
    for e in range(N_EXPERTS):
        start = ws_ref[n * N_EXPERTS + e]
        start1 = jnp.where(e1 == e, start, start1)
        start2 = jnp.where(e2 == e, start, start2)
    col1 = e1 * COMBINE_WIN + s1 - start1
    col2 = e2 * COMBINE_WIN + s2 - start2
    pos = lax.broadcasted_iota(jnp.int32, (1, N_EXPERTS * COMBINE_WIN), 1)
    pick = jnp.where(jnp.logical_or(pos == col1, pos == col2), 1.0, 0.0).astype(BF16)
    o_ref[...] = hs_ref[...] + jnp.dot(pick, win_ref[cur], preferred_element_type=F32)


def _combine(hs3, route, y, win_start, seq):
    bsz = hs3.shape[0]
    tc = COMBINE_TOKENS
    tiles_per_seq = seq // tc
    kern = functools.partial(_combine_kernel, tc=tc, tiles_per_seq=tiles_per_seq, n_steps=bsz * tiles_per_seq)
    tok = lambda w: pl.BlockSpec((tc, w), lambda b, i, ws: (b * tiles_per_seq + i, 0))
    return pl.pallas_call(
        kern,
        out_shape=jax.ShapeDtypeStruct((bsz, seq, D_MODEL), F32),
        grid_spec=pltpu.PrefetchScalarGridSpec(
            num_scalar_prefetch=1,
            grid=(bsz, tiles_per_seq),
            in_specs=[
                pl.BlockSpec((None, tc, D_MODEL), lambda b, i, ws: (b, i + T_PAD // tc, 0)),
                tok(4),
                pl.BlockSpec(memory_space=pl.ANY),
            ],
            out_specs=pl.BlockSpec((None, tc, D_MODEL), lambda b, i, ws: (b, i, 0)),
            scratch_shapes=[pltpu.VMEM((2, N_EXPERTS * COMBINE_WIN, D_MODEL), BF16),
                            pltpu.SemaphoreType.DMA((2,))],
        ),
        compiler_params=_params(("arbitrary", "arbitrary")),
        name="moe_combine",
    )(win_start, hs3, route, y)


def _moe(hs, gain, router_w, w_gate, w_up, w_down, bsz, tp):
    seq = tp - T_PAD
    n_tok = bsz * seq
    tg = MOE_TILE
    idx, wts, cnt = _router(hs, gain, router_w, tp)
    sel = idx.reshape(bsz, tp, 128)[:, T_PAD:, :4].reshape(n_tok, 4)
    counts = cnt[0, :N_EXPERTS].astype(jnp.int32)
    padded = ((counts + tg - 1) // tg) * tg
    ends = jnp.cumsum(padded)
    starts = ends - padded
    experts = jnp.arange(N_EXPERTS, dtype=jnp.int32)[None, :]
    slot1 = jnp.sum(jnp.where(sel[:, 0:1] == experts, starts[None, :], 0), axis=1) + sel[:, 2]
    slot2 = jnp.sum(jnp.where(sel[:, 1:2] == experts, starts[None, :], 0), axis=1) + sel[:, 3]
    slot1, slot2 = slot1.astype(jnp.int32), slot2.astype(jnp.int32)
    n_slots = 2 * n_tok + N_EXPERTS * tg
    tile_start = jnp.arange(n_slots // tg, dtype=jnp.int32)[:, None] * tg
    tile_expert = jnp.minimum(jnp.sum((tile_start >= ends[None, :]).astype(jnp.int32), axis=1), N_EXPERTS - 1)
    n_tiles_used = (ends[-1:] // tg).astype(jnp.int32)
    tail = n_slots - tg * (1 + jnp.arange(N_EXPERTS, dtype=jnp.int32))
    zero_rows = jnp.concatenate([jnp.maximum(ends - tg, 0), tail]).astype(jnp.int32)

    tc = COMBINE_TOKENS
    first = jnp.minimum(
        jnp.min(jnp.where(sel[:, 0:1] == experts, slot1[:, None], n_slots).reshape(n_tok // tc, tc, N_EXPERTS), axis=1),
        jnp.min(jnp.where(sel[:, 1:2] == experts, slot2[:, None], n_slots).reshape(n_tok // tc, tc, N_EXPERTS), axis=1))
    win_start = jnp.where(first == n_slots, 0,
                          jnp.minimum(first // WIN_ALIGN * WIN_ALIGN, n_slots - COMBINE_WIN))
    route = jnp.concatenate([sel[:, :2], slot1[:, None], slot2[:, None]], axis=1)

    hs3 = hs.reshape(bsz, tp, D_MODEL)
    xs, (wg_bf, wu_bf, wd_bf) = _dispatch(hs3, wts.reshape(bsz, tp, 128), gain, slot1, slot2, zero_rows,
                                          n_slots, tg, (w_gate, w_up, w_down))
    y = _experts(xs, tile_expert.astype(jnp.int32), n_tiles_used, wg_bf, wu_bf, wd_bf, tg)
    return _combine(hs3, route, y, win_start.reshape(-1).astype(jnp.int32), seq)


def _permute_qk_cols(w):
    return w.reshape(D_MODEL, 2, ATT_HEADS, ATT_QK_DIM).transpose(0, 2, 1, 3).reshape(D_MODEL, 512)


def kernel(x, meta_tokens, norm1_gain, norm2_gain, w_in, q_norm_gain, k_norm_gain, diff_lambda,
           attn_sub_gain, rel_bias, conv_w, hgrn_lb_logits, hgrn_out_gain, w_branch, w_out,
           ffn_w_gate, ffn_w_up, ffn_w_down, router_w, moe_w_gate, moe_w_up, moe_w_down):
    bsz, seq, _ = x.shape
    depth = w_in.shape[0]
    tp = T_PAD + seq
    assert tp % ATT_TILE == 0 and depth == 2

    head = jnp.concatenate([jnp.zeros((PAD0, D_MODEL), x.dtype), meta_tokens.astype(x.dtype)], axis=0)
    hs = jnp.concatenate([jnp.broadcast_to(head[None], (bsz, T_PAD, D_MODEL)), x], axis=1)
    hs = hs.reshape(bsz * tp, D_MODEL)

    lb_all = jnp.cumsum(jax.nn.softmax(hgrn_lb_logits.astype(F32), axis=0), axis=0)
    lb_all = lb_all - lb_all[0]
    btab = _attn_bias_tables(rel_bias, ATT_TILE)

    def permuted(w):
        return jnp.concatenate([_permute_qk_cols(w[:, :512]), _permute_qk_cols(w[:, 512:1024])], axis=1)

    later = dict(branch0=w_branch[0], out0=w_out[0], gate=ffn_w_gate[0], up=ffn_w_up[0], down=ffn_w_down[0],
                 w_in1=w_in[1], branch1=w_branch[1], out1=w_out[1])
    bf = {}
    out = None
    for layer in range(depth):
        if layer == 0:
            w = w_in[0]
            w_bf = jnp.concatenate([permuted(w), w[:, 1024:]], axis=1).astype(BF16)
        else:
            w_bf = bf["w_in1"].at[:, :1024].set(permuted(bf["w_in1"]))
        qk_gain = jnp.concatenate([jnp.tile(q_norm_gain[layer].astype(F32), 8) * (ATT_QK_DIM ** -0.5 * LOG2E),
                                   jnp.tile(k_norm_gain[layer].astype(F32), 8)]).reshape(1, COL_TILE)
        proj = _inproj(hs, norm1_gain[layer], w_bf, qk_gain)
        proj3 = proj.reshape(bsz, tp, IN_COLS)

        lam_init = 0.8 - 0.6 * math.exp(-0.3 * layer)
        lp = diff_lambda[layer].astype(F32)
        lam = jnp.exp(jnp.sum(lp[0] * lp[1])) - jnp.exp(jnp.sum(lp[2] * lp[3])) + lam_init
        u_att, cast = _diff_attention(proj3, lam.reshape(1), attn_sub_gain[layer].astype(F32), btab,
                                      1.0 - lam_init, tuple(later.values()) if layer == 0 else ())
        if layer == 0:
            bf = dict(zip(later.keys(), cast))

        lb = lb_all[layer]
        u_hgrn = _hgrn(proj3, jnp.log(lb), jnp.log1p(-lb), hgrn_out_gain[layer].astype(F32))

        hs = _merge(hs, u_att.reshape(bsz * tp, 512), u_hgrn.reshape(bsz * tp, 512), proj,
                    conv_w[layer].astype(F32), bf["branch%d" % layer], bf["out%d" % layer], tp)

        j = layer // 2
        if layer % 2 == 0:
            hs = _ffn(hs, norm2_gain[layer], bf["gate"], bf["up"], bf["down"])
        else:
            out = _moe(hs, norm2_gain[layer], router_w[j], moe_w_gate[j], moe_w_up[j], moe_w_down[j], bsz, tp)
    return out
```

```python
import functools
import math

import numpy as np
import jax
import jax.numpy as jnp
from jax import lax
from jax.experimental import pallas as pl
from jax.experimental.pallas import tpu as pltpu

F32 = jnp.float32
BF16 = jnp.bfloat16

D_MODEL = 1024
N_META = 16
EPS = 1e-6
ATT_HEADS = 4
ATT_QK_DIM = 64
ATT_V_DIM = 128
REL_BUCKETS = 32
REL_MAX_DIST = 128
CONV_K = 3
HGRN_HEADS = 4
HGRN_D = 128
N_EXPERTS = 8
IN_COLS = 8192

T_PAD = 128
PAD0 = T_PAD - N_META
ATT_TILE = 384
VT_ONES = 16
LOG2E = math.log2(math.e)
HGRN_CHUNK = 128
HGRN_SUB = 16
COL_TILE = 1024
GROUP_MEAN_WIDTH = 256
MOE_TILE = 512
MOE_F_BLOCKS = 2
XS_EXTRA = 128
COMBINE_TOKENS = 128
WIN_ALIGN = 16
COMBINE_WIN = COMBINE_TOKENS + WIN_ALIGN
MASK = -1e30
VMEM_LIMIT = 56 * 1024 * 1024

BLK_Q, BLK_K, BLK_V = 0, 4, 8
BLK_RQ, BLK_RF, BLK_RI, BLK_RG = 24, 28, 32, 36
BLK_CB, BLK_CC, BLK_CH = 3, 4, 5
BLK_GATE = 5


def _row_tile(rows, target):
    n = rows // 128
    best = 1
    for d in range(1, n + 1):
        if n % d == 0 and d * 128 <= target:
            best = d
    return best * 128


def _params(sem, vmem=VMEM_LIMIT):
    return pltpu.CompilerParams(dimension_semantics=sem, vmem_limit_bytes=vmem)


def _inproj_kernel(x_ref, g_ref, w_ref, qkg_ref, gm_ref, o_ref, xn_ref):
    j = pl.program_id(1)

    @pl.when(j == 0)
    def _():
        x = x_ref[...]
        ms = jnp.mean(x * x, axis=-1, keepdims=True)
        xn_ref[...] = (x * lax.rsqrt(ms + EPS) * g_ref[...]).astype(BF16)

    acc = jnp.dot(xn_ref[...], w_ref[...], preferred_element_type=F32)

    @pl.when(j == 0)
    def _():
        sq = acc * acc
        hi = sq.astype(BF16)
        lo = (sq - hi.astype(F32)).astype(BF16)
        gm = gm_ref[...]
        width = gm.shape[0]
        ms = jnp.concatenate(
            [jnp.dot(hi[:, c:c + width], gm, preferred_element_type=F32)
             + jnp.dot(lo[:, c:c + width], gm, preferred_element_type=F32)
             for c in range(0, COL_TILE, width)], axis=1)
        o_ref[...] = (acc * lax.rsqrt(ms + EPS) * qkg_ref[...]).astype(BF16)

    @pl.when(j > 0)
    def _():
        o_ref[...] = acc.astype(BF16)


def _inproj(hs, gain, w_bf, qk_gain):
    rows = hs.shape[0]
    tm = _row_tile(rows, 1536)
    n_col = IN_COLS // COL_TILE
    assert COL_TILE == 4 * ATT_HEADS * ATT_QK_DIM
    grp = np.arange(GROUP_MEAN_WIDTH) // ATT_QK_DIM
    gm = jnp.asarray((grp[:, None] == grp[None, :]).astype(np.float32) / ATT_QK_DIM, BF16)
    return pl.pallas_call(
        _inproj_kernel,
        out_shape=jax.ShapeDtypeStruct((rows, IN_COLS), BF16),
        grid=(rows // tm, n_col),
        in_specs=[
            pl.BlockSpec((tm, D_MODEL), lambda i, j: (i, 0)),
            pl.BlockSpec((1, D_MODEL), lambda i, j: (0, 0)),
            pl.BlockSpec((D_MODEL, COL_TILE), lambda i, j: (0, j)),
            pl.BlockSpec((1, COL_TILE), lambda i, j: (0, 0)),
            pl.BlockSpec((GROUP_MEAN_WIDTH, GROUP_MEAN_WIDTH), lambda i, j: (0, 0)),
        ],
        out_specs=pl.BlockSpec((tm, COL_TILE), lambda i, j: (i, j)),
        scratch_shapes=[pltpu.VMEM((tm, D_MODEL), BF16)],
        compiler_params=_params(("parallel", "arbitrary")),
        name="inproj",
    )(hs, gain.reshape(1, D_MODEL), w_bf, qk_gain, gm)


def _rel_bucket_table(n_max):
    n = np.arange(n_max, dtype=np.int64)
    max_exact = REL_BUCKETS // 2
    nf = np.maximum(n, 1).astype(np.float32)
    large = max_exact + (np.log(nf / np.float32(max_exact)) / np.float32(math.log(REL_MAX_DIST / max_exact))
                         * np.float32(REL_BUCKETS - max_exact)).astype(np.int32)
    large = np.minimum(large, REL_BUCKETS - 1)
    return np.where(n < max_exact, n, large).astype(np.int32)


def _attn_bias_tables(rel_bias, t):
    bucket = _rel_bucket_table(2 * t)
    assert np.all(bucket[t + 1:] == REL_BUCKETS - 1) and np.all(np.diff(bucket) >= 0)
    first_dist = tuple(int(np.searchsorted(bucket, b, side="left")) for b in range(REL_BUCKETS))
    return pl.pallas_call(
        functools.partial(_bias_kernel, t=t, first_dist=first_dist),
        out_shape=jax.ShapeDtypeStruct((ATT_HEADS, 6, t, t), F32),
        grid=(ATT_HEADS,),
        in_specs=[pl.BlockSpec(memory_space=pltpu.SMEM)],
        out_specs=pl.BlockSpec((None, 6, t, t), lambda h: (h, 0, 0, 0)),
        compiler_params=_params(("parallel",)),
        name="attn_bias",
    )(rel_bias.astype(F32))


def _bias_kernel(rb_ref, o_ref, *, t, first_dist):
    h = pl.program_id(0)
    key = lax.broadcasted_iota(jnp.int32, (t, t), 0)
    qry = lax.broadcasted_iota(jnp.int32, (t, t), 1)
    far = rb_ref[REL_BUCKETS - 1, h]

    def table(n):
        val = jnp.full((t, t), rb_ref[0, h] - far, F32)
        for b in range(1, REL_BUCKETS):
            val = jnp.where(n >= first_dist[b], rb_ref[b, h] - far, val)
        return val * LOG2E

    n0 = qry - key
    diag = jnp.where(n0 >= 0, table(n0), MASK)
    near = table(n0 + t)
    zero = jnp.zeros((t, t), F32)
    for kind, tab in enumerate((diag, near, zero)):
        o_ref[kind] = tab
        o_ref[kind + 3] = jnp.where(key < PAD0, MASK, tab)


def _cast_specs(arrays, n_steps, step_of):
    flat = [w.reshape(-1, w.shape[-1]) for w in arrays]
    assert all(w.shape[0] % (n_steps * WIN_ALIGN) == 0 for w in flat)
    specs = [pl.BlockSpec((w.shape[0] // n_steps, w.shape[1]), lambda *idx: (step_of(*idx), 0)) for w in flat]
    return flat, specs, [jax.ShapeDtypeStruct(w.shape, BF16) for w in flat]


def _attn_kernel(lam_ref, q_ref, k_ref, v_ref, bt_ref, sg_ref, *refs, t, out_scale, n_cast):
    cast_in, (o_ref, *cast_out), (m_ref, acc_ref, vt_ref) = refs[:n_cast], refs[n_cast:2 * n_cast + 1], refs[-3:]
    for src, dst in zip(cast_in, cast_out):
        dst[...] = src[...].astype(BF16)
    n_t = vt_ref.shape[0]
    for j in range(n_t):
        vt_ref[j, :ATT_V_DIM, :] = v_ref[j * t:(j + 1) * t, :].astype(F32).T.astype(BF16)
        vt_ref[j, ATT_V_DIM:, :] = jnp.ones((VT_ONES, t), BF16)
    lane = lax.broadcasted_iota(jnp.int32, (1, 2 * ATT_QK_DIM), 1)
    nt = (((1,), (1,)), ((), ()))

    def tile_rows(i):
        return pl.ds(pl.multiple_of(i * t, t), t)

    def scores(i, j):
        q = q_ref[tile_rows(i), :]
        zero = jnp.zeros_like(q)
        q_cat = jnp.concatenate([jnp.where(lane < ATT_QK_DIM, q, zero), jnp.where(lane >= ATT_QK_DIM, q, zero)],
                                axis=0)
        bias = bt_ref[jnp.minimum(i - j, 2) + jnp.where(j == 0, 3, 0)]
        s = (lax.dot_general(k_ref[tile_rows(j), :], q_cat, nt, preferred_element_type=F32)
             + jnp.concatenate([bias, bias], axis=1))
        return s, jnp.max(s, axis=0, keepdims=True)

    def consume(j, scored):
        s, s_max = scored
        m_old = m_ref[...]
        m_new = jnp.maximum(m_old, s_max)
        m_ref[...] = m_new
        p = jnp.exp2(s - m_new).astype(BF16)
        acc_ref[...] = (jnp.exp2(m_old - m_new) * acc_ref[...]
                        + jnp.dot(vt_ref[j], p, preferred_element_type=F32))

    def reset():
        m_ref[...] = jnp.full(m_ref.shape, MASK, F32)
        acc_ref[...] = jnp.zeros(acc_ref.shape, F32)

    def finish(i):
        acc = acc_ref[...]
        a1, a2 = acc[:, :t], acc[:, t:]
        o_t = (a1[:ATT_V_DIM] / a1[ATT_V_DIM:ATT_V_DIM + 1]
               - lam_ref[0] * (a2[:ATT_V_DIM] / a2[ATT_V_DIM:ATT_V_DIM + 1]))
        o = o_t.T
        ms = jnp.mean(o * o, axis=-1, keepdims=True)
        y = o * lax.rsqrt(ms + EPS) * (sg_ref[...] * out_scale)
        row = i * t + lax.broadcasted_iota(jnp.int32, (t, 1), 0)
        o_ref[tile_rows(i), :] = jnp.where(row >= PAD0, y, 0.0).astype(BF16)
        reset()

    def step(_, carry):
        i, j, s = carry
        last = j == i
        ni = jnp.where(last, i + 1, i)
        nj = jnp.where(last, 0, j + 1)
        nxt = scores(jnp.minimum(ni, n_t - 1), nj)
        consume(j, s)

        @pl.when(last)
        def _():
            finish(i)

        return ni, nj, nxt

    reset()
    first = jnp.int32(0)
    lax.fori_loop(0, n_t * (n_t + 1) // 2, step, (first, first, scores(first, first)), unroll=2)


def _diff_attention(proj3, lam, sub_gain, btab, out_scale, casts=()):
    bsz, tp, _ = proj3.shape
    t = ATT_TILE
    n_t = tp // t
    rows = ATT_V_DIM + VT_ONES
    kern = functools.partial(_attn_kernel, t=t, out_scale=out_scale, n_cast=len(casts))
    seq = lambda blk: pl.BlockSpec((None, tp, 128), lambda b, h: (b, 0, blk + h))
    flat, cast_specs, cast_shapes = _cast_specs(casts, bsz * ATT_HEADS, lambda b, h: b * ATT_HEADS + h)
    out = pl.pallas_call(
        kern,
        out_shape=[jax.ShapeDtypeStruct((bsz, tp, ATT_HEADS * ATT_V_DIM), BF16)] + cast_shapes,
        grid=(bsz, ATT_HEADS),
        in_specs=[
            pl.BlockSpec(memory_space=pltpu.SMEM),
            seq(BLK_Q), seq(BLK_K), seq(BLK_V),
            pl.BlockSpec((None, 6, t, t), lambda b, h: (h, 0, 0, 0)),
            pl.BlockSpec((1, ATT_V_DIM), lambda b, h: (0, 0)),
        ] + cast_specs,
        out_specs=[pl.BlockSpec((None, tp, 128), lambda b, h: (b, 0, h))] + cast_specs,
        scratch_shapes=[pltpu.VMEM((1, 2 * t), F32), pltpu.VMEM((rows, 2 * t), F32),
                        pltpu.VMEM((n_t, rows, t), BF16)],
        compiler_params=_params(("parallel", "parallel")),
        name="diff_attn",
    )(lam, proj3, proj3, proj3, btab, sub_gain.reshape(1, ATT_V_DIM), *flat)
    return out[0], [o.reshape(w.shape) for o, w in zip(out[1:], casts)]


def _split3(x):
    h1 = x.astype(BF16)
    r1 = x - h1.astype(F32)
    h2 = r1.astype(BF16)
    h3 = (r1 - h2.astype(F32)).astype(BF16)
    return h1, h2, h3


def _hgrn_kernel(q_ref, f_ref, i_ref, g_ref, la_ref, l1m_ref, og_ref, o_ref):
    c_len, sub = HGRN_CHUNK, HGRN_SUB
    half = sub // 2
    n_chunk = q_ref.shape[0] // c_len
    la, l1m, og = la_ref[...], l1m_ref[...], og_ref[...]
    rr = lax.broadcasted_iota(jnp.int32, (c_len, c_len), 0)
    cc = lax.broadcasted_iota(jnp.int32, (c_len, c_len), 1)
    tri = jnp.where(cc <= rr, 1.0, 0.0).astype(BF16)
    row8 = lax.broadcasted_iota(jnp.int32, (half, 1), 0)
    lane8 = lax.broadcasted_iota(jnp.int32, (half, c_len), 1)
    nt = (((1,), (1,)), ((), ()))

    def chunk(c, st):
        r0 = pl.multiple_of(c * c_len, c_len)
        rows = pl.ds(r0, c_len)
        z = f_ref[rows, :].astype(F32)
        qh = q_ref[rows, :].astype(F32)
        qh = qh * jax.nn.sigmoid(qh)
        v = i_ref[rows, :]
        gate = g_ref[rows, :].astype(F32)
        sp = jnp.log(1.0 + jnp.exp(-jnp.abs(z)))
        bb = l1m + jnp.minimum(z, 0.0) - sp
        log_f = jnp.maximum(la, bb) + jnp.log(1.0 + jnp.exp(-jnp.abs(la - bb)))
        valid = (r0 + lax.broadcasted_iota(jnp.int32, (c_len, 1), 0)) >= PAD0
        log_k = jnp.where(valid, l1m + jnp.minimum(-z, 0.0) - sp, -jnp.inf)
        g = sum(jnp.dot(tri, part, preferred_element_type=F32) for part in _split3(log_f))
        ck = log_k - g
        o_inter = lax.dot_general((qh * jnp.exp(g)).astype(BF16), st.astype(BF16), nt,
                                  preferred_element_type=F32)
        a_rows = []
        for a in range(c_len // sub):
            lo = a * sub
            ga = (g[lo:lo + half, :], g[lo + half:lo + sub, :])
            qa = (qh[lo:lo + half, :], qh[lo + half:lo + sub, :])
            if a == 0:
                blk = [jnp.zeros((half, c_len), F32)] * 2
            else:
                gs = g[lo - 1:lo, :]
                qd = (qh[lo:lo + sub, :] * jnp.exp(g[lo:lo + sub, :] - gs)).astype(BF16)
                kd = jnp.exp(jnp.minimum(gs - g[:lo, :], 0.0) + log_k[:lo, :]).astype(BF16)
                kd = jnp.concatenate([kd, jnp.zeros((c_len - lo, HGRN_D), BF16)], axis=0)
                a_off = lax.dot_general(qd, kd, nt, preferred_element_type=F32)
                blk = [a_off[:half, :], a_off[half:, :]]
            for s in range(sub):
                crow = ck[lo + s:lo + s + 1, :]
                for hh in range(s // half, 2):
                    col = jnp.sum(qa[hh] * jnp.exp(ga[hh] + crow), axis=-1, keepdims=True)
                    blk[hh] = jnp.where(lane8 == lo + s, col, blk[hh])
            for hh in range(2):
                a_rows.append(jnp.where(lane8 <= lo + hh * half + row8, blk[hh], 0.0))
        a_full = jnp.concatenate(a_rows, axis=0).astype(BF16)
        o = o_inter + jnp.dot(a_full, v, preferred_element_type=F32)
        g_last = g[c_len - 1:c_len, :]
        kd = jnp.exp(g_last - g + log_k).astype(BF16)
        st = st * jnp.exp(g_last) + lax.dot_general(v, kd, (((0,), (0,)), ((), ())),
                                                    preferred_element_type=F32)
        ms = jnp.mean(o * o, axis=-1, keepdims=True)
        y = o * lax.rsqrt(ms + EPS) * og * (gate * jax.nn.sigmoid(gate))
        o_ref[rows, :] = y.astype(BF16)
        return st

    lax.fori_loop(0, n_chunk, chunk, jnp.zeros((HGRN_D, HGRN_D), F32), unroll=11)


def _hgrn(proj3, log_lb, log1m_lb, out_gain):
    bsz, tp, _ = proj3.shape
    seq = lambda blk: pl.BlockSpec((None, tp, 128), lambda b, h: (b, 0, blk + h))
    chan = pl.BlockSpec((None, 1, HGRN_D), lambda b, h: (h, 0, 0))
    return pl.pallas_call(
        _hgrn_kernel,
        out_shape=jax.ShapeDtypeStruct((bsz, tp, HGRN_HEADS * HGRN_D), BF16),
        grid=(bsz, HGRN_HEADS),
        in_specs=[seq(BLK_RQ), seq(BLK_RF), seq(BLK_RI), seq(BLK_RG), chan, chan,
                  pl.BlockSpec((1, HGRN_D), lambda b, h: (0, 0))],
        out_specs=pl.BlockSpec((None, tp, 128), lambda b, h: (b, 0, h)),
        compiler_params=_params(("parallel", "parallel")),
        name="hgrn2",
    )(proj3, proj3, proj3, proj3,
      log_lb.reshape(HGRN_HEADS, 1, HGRN_D), log1m_lb.reshape(HGRN_HEADS, 1, HGRN_D),
      out_gain.reshape(1, HGRN_D))


def _merge_kernel(hs_ref, ua_ref, ur_ref, cb_ref, cc_ref, ch_ref, pc_ref, ph_ref,
                  g0_ref, g1_ref, g2_ref, cw_ref, wb_ref, wo_ref, o_ref, *, tm, tiles_per_seq):
    i = pl.program_id(0)
    row = (i % tiles_per_seq) * tm + lax.broadcasted_iota(jnp.int32, (tm, 1), 0)
    valid = row >= PAD0
    z = jnp.where(valid, cc_ref[...].astype(F32) * ch_ref[...].astype(F32), 0.0)
    halo_row = (i % tiles_per_seq) * tm - 8 + lax.broadcasted_iota(jnp.int32, (8, 1), 0)
    zp = jnp.where(halo_row >= PAD0, pc_ref[...].astype(F32) * ph_ref[...].astype(F32), 0.0)
    zz = jnp.concatenate([zp, z], axis=0)
    cw = cw_ref[...]
    y = (cw[2:3, :] * z + cw[1:2, :] * zz[7:7 + tm, :] + cw[0:1, :] * zz[6:6 + tm, :])
    u_conv = jnp.where(valid, cb_ref[...].astype(F32) * y, 0.0).astype(BF16)
    mixed = jnp.zeros((tm, D_MODEL), F32)
    for n, (u, g_ref) in enumerate(((ua_ref[...], g0_ref), (u_conv, g1_ref), (ur_ref[...], g2_ref))):
        up = jnp.dot(u, wb_ref[n], preferred_element_type=F32)
        mixed = mixed + jax.nn.sigmoid(g_ref[...].astype(F32)) * up
    o_ref[...] = hs_ref[...] + jnp.dot(mixed.astype(BF16), wo_ref[...], preferred_element_type=F32)


def _merge(hs, u_att, u_hgrn, proj, conv_w, wb_bf, wo_bf, tp):
    rows = hs.shape[0]
    tm = _row_tile(tp, 384)
    kern = functools.partial(_merge_kernel, tm=tm, tiles_per_seq=tp // tm)
    row_blk = lambda w, blk: pl.BlockSpec((tm, w), lambda i: (i, blk))
    halo = lambda blk: pl.BlockSpec((8, 512), lambda i: (jnp.maximum(i * (tm // 8) - 1, 0), blk))
    const = lambda shape: pl.BlockSpec(shape, lambda i: (0,) * len(shape))
    return pl.pallas_call(
        kern,
        out_shape=jax.ShapeDtypeStruct((rows, D_MODEL), F32),
        grid=(rows // tm,),
        in_specs=[row_blk(D_MODEL, 0), row_blk(512, 0), row_blk(512, 0),
                  row_blk(512, BLK_CB), row_blk(512, BLK_CC), row_blk(512, BLK_CH),
                  halo(BLK_CC), halo(BLK_CH),
                  row_blk(1024, BLK_GATE), row_blk(1024, BLK_GATE + 1), row_blk(1024, BLK_GATE + 2),
                  const((CONV_K, 512)), const((3, 512, D_MODEL)), const((D_MODEL, D_MODEL))],
        out_specs=row_blk(D_MODEL, 0),
        compiler_params=_params(("parallel",)),
        name="merge",
    )(hs, u_att, u_hgrn, proj, proj, proj, proj, proj, proj, proj, proj, conv_w, wb_bf, wo_bf)


def _ffn_kernel(hs_ref, g_ref, wg_ref, wu_ref, wd_ref, o_ref):
    x = hs_ref[...]
    ms = jnp.mean(x * x, axis=-1, keepdims=True)
    hn = (x * lax.rsqrt(ms + EPS) * g_ref[...]).astype(BF16)
    a = jnp.dot(hn, wg_ref[...], preferred_element_type=F32)
    u = jnp.dot(hn, wu_ref[...], preferred_element_type=F32)
    h = (a * jax.nn.sigmoid(a) * u).astype(BF16)
    o_ref[...] = x + jnp.dot(h, wd_ref[...], preferred_element_type=F32)


def _ffn(hs, gain, wg_bf, wu_bf, wd_bf):
    rows = hs.shape[0]
    d_ff = wg_bf.shape[1]
    tm = _row_tile(rows, 768)
    const = lambda shape: pl.BlockSpec(shape, lambda i: (0, 0), pipeline_mode=pl.Buffered(1))
    return pl.pallas_call(
        _ffn_kernel,
        out_shape=jax.ShapeDtypeStruct((rows, D_MODEL), F32),
        grid=(rows // tm,),
        in_specs=[pl.BlockSpec((tm, D_MODEL), lambda i: (i, 0)), const((1, D_MODEL)),
                  const((D_MODEL, d_ff)), const((D_MODEL, d_ff)), const((d_ff, D_MODEL))],
        out_specs=pl.BlockSpec((tm, D_MODEL), lambda i: (i, 0)),
        compiler_params=_params(("parallel",)),
        name="ffn_dense",
    )(hs, gain.reshape(1, D_MODEL), wg_bf, wu_bf, wd_bf)


def _router_kernel(hs_ref, g_ref, rw_ref, idx_ref, wt_ref, cnt_ref, carry_ref, *, tm, tiles_per_seq):
    i = pl.program_id(0)

    @pl.when(i == 0)
    def _():
        carry_ref[...] = jnp.zeros_like(carry_ref)

    x = hs_ref[...]
    ms = jnp.mean(x * x, axis=-1, keepdims=True)
    hn = x * lax.rsqrt(ms + EPS) * g_ref[...]
    hn_hi = hn.astype(BF16)
    hn_lo = (hn - hn_hi.astype(F32)).astype(BF16)
    logits = (jnp.dot(hn_hi, rw_ref[0], preferred_element_type=F32)
              + jnp.dot(hn_lo, rw_ref[0], preferred_element_type=F32)
              + jnp.dot(hn_hi, rw_ref[1], preferred_element_type=F32))
    lane = lax.broadcasted_iota(jnp.int32, logits.shape, 1)
    lane_f = lane.astype(F32)
    logits = jnp.where(lane < N_EXPERTS, logits, -jnp.inf)
    m1 = jnp.max(logits, axis=-1, keepdims=True)
    i1 = jnp.min(jnp.where(logits == m1, lane_f, 128.0), axis=-1, keepdims=True)
    rest = jnp.where(lane_f == i1, -jnp.inf, logits)
    m2 = jnp.max(rest, axis=-1, keepdims=True)
    i2 = jnp.min(jnp.where(rest == m2, lane_f, 128.0), axis=-1, keepdims=True)
    e2 = jnp.exp(m2 - m1)
    w1 = 1.0 / (1.0 + e2)
    w2 = e2 / (1.0 + e2)
    row = (i % tiles_per_seq) * tm + lax.broadcasted_iota(jnp.int32, (tm, 1), 0)
    real = row >= T_PAD
    hot1 = jnp.where(jnp.logical_and(real, lane_f == i1), 1.0, 0.0)
    hot2 = jnp.where(jnp.logical_and(real, lane_f == i2), 1.0, 0.0)
    both = hot1 + hot2
    rr = lax.broadcasted_iota(jnp.int32, (tm, tm), 0)
    cc = lax.broadcasted_iota(jnp.int32, (tm, tm), 1)
    earlier = jnp.where(cc < rr, 1.0, 0.0).astype(BF16)
    before = carry_ref[...] + jnp.dot(earlier, both.astype(BF16), preferred_element_type=F32)
    r1 = jnp.sum(before * hot1, axis=-1, keepdims=True)
    r2 = jnp.sum(before * hot2, axis=-1, keepdims=True)
    carry_ref[...] += jnp.sum(both, axis=0, keepdims=True)
    cnt_ref[...] = carry_ref[...]
    packed = jnp.where(lane == 0, i1, jnp.where(lane == 1, i2, jnp.where(lane == 2, r1, jnp.where(lane == 3, r2, 0.0))))
    idx_ref[...] = packed.astype(jnp.int32)
    wt_ref[...] = jnp.where(lane == 0, w1, jnp.where(lane == 1, w2, 0.0))


def _router(hs, gain, router_w, tp):
    rows = hs.shape[0]
    tm = _row_tile(tp, 384)
    rw = jnp.zeros((D_MODEL, 128), F32).at[:, :N_EXPERTS].set(router_w.astype(F32))
    rw_hi = rw.astype(BF16)
    rw = jnp.stack([rw_hi, (rw - rw_hi.astype(F32)).astype(BF16)])
    blk = pl.BlockSpec((tm, 128), lambda i: (i, 0))
    return pl.pallas_call(
        functools.partial(_router_kernel, tm=tm, tiles_per_seq=tp // tm),
        out_shape=(jax.ShapeDtypeStruct((rows, 128), jnp.int32), jax.ShapeDtypeStruct((rows, 128), F32),
                   jax.ShapeDtypeStruct((1, 128), F32)),
        grid=(rows // tm,),
        in_specs=[pl.BlockSpec((tm, D_MODEL), lambda i: (i, 0)),
                  pl.BlockSpec((1, D_MODEL), lambda i: (0, 0)),
                  pl.BlockSpec((2, D_MODEL, 128), lambda i: (0, 0, 0))],
        out_specs=(blk, blk, pl.BlockSpec((1, 128), lambda i: (0, 0))),
        scratch_shapes=[pltpu.VMEM((1, 128), F32)],
        compiler_params=_params(("arbitrary",)),
        name="moe_router",
    )(hs, gain.reshape(1, D_MODEL), rw)


def _dispatch_kernel(s1_ref, s2_ref, zr_ref, hs_ref, wt_ref, g_ref, wg_ref, wu_ref, wd_ref,
                     xs_ref, wg_out, wu_out, wd_out, rows_ref, zero_ref, sem, zsem,
                     *, td, tg, tiles_per_seq, n_steps):
    for src, dst in ((wg_ref, wg_out), (wu_ref, wu_out), (wd_ref, wd_out)):
        dst[...] = src[...].astype(BF16)
    n = pl.program_id(0) * tiles_per_seq + pl.program_id(1)
    buf = n % 2
    base = n * td

    def row_copy(r, which):
        slot = (s1_ref, s2_ref)[which][base + r]
        return pltpu.make_async_copy(rows_ref.at[buf, which, pl.ds(r, 1)], xs_ref.at[pl.ds(slot, 1)],
                                     sem.at[buf])

    def wait_rows(b):
        def body(r, c):
            pltpu.make_async_copy(rows_ref.at[b, 0, pl.ds(0, 1)], xs_ref.at[pl.ds(0, 1)], sem.at[b]).wait()
            return c
        lax.fori_loop(0, 2 * td, body, 0)

    @pl.when(n == 0)
    def _():
        zero_ref[...] = jnp.zeros_like(zero_ref)

        def fill(e):
            return pltpu.make_async_copy(zero_ref, xs_ref.at[pl.ds(pl.multiple_of(zr_ref[e], tg), tg)], zsem)

        for e in range(N_EXPERTS, 2 * N_EXPERTS):
            fill(e).start()
        for e in range(N_EXPERTS, 2 * N_EXPERTS):
            fill(e).wait()
        for action in ("start", "wait"):
            getattr(fill(0), action)()
            for e in range(1, N_EXPERTS):
                @pl.when(zr_ref[e] != zr_ref[e - 1])
                def _():
                    getattr(fill(e), action)()

    @pl.when(n >= 2)
    def _():
        wait_rows(buf)

    x = hs_ref[...]
    ms = jnp.mean(x * x, axis=-1, keepdims=True)
    hn = x * lax.rsqrt(ms + EPS) * g_ref[...]
    wt = wt_ref[...]
    for which in range(2):
        rows_ref[buf, which, :, :D_MODEL] = hn
        rows_ref[buf, which, :, D_MODEL:] = jnp.broadcast_to(wt[:, which:which + 1], (td, XS_EXTRA))

    def issue(r, c):
        row_copy(r, 0).start()
        row_copy(r, 1).start()
        return c

    lax.fori_loop(0, td, issue, 0)

    @pl.when(n == n_steps - 1)
    def _():
        wait_rows(buf)
        if n_steps > 1:
            wait_rows(1 - buf)


def _dispatch(hs3, wts3, gain, slot1, slot2, zero_rows, n_slots, tg, expert_weights):
    bsz, tp, _ = hs3.shape
    td = T_PAD
    tiles_per_seq = (tp - T_PAD) // td
    n_steps = bsz * tiles_per_seq
    width = D_MODEL + XS_EXTRA
    kern = functools.partial(_dispatch_kernel, td=td, tg=tg, tiles_per_seq=tiles_per_seq, n_steps=n_steps)
    flat = [w.reshape(-1, w.shape[-1]) for w in expert_weights]
    assert all(w.shape[0] % (n_steps * WIN_ALIGN) == 0 for w in flat)
    w_spec = lambda w: pl.BlockSpec((w.shape[0] // n_steps, w.shape[1]),
                                    lambda b, i, *_: (b * tiles_per_seq + i, 0))
    out = pl.pallas_call(
        kern,
        out_shape=[jax.ShapeDtypeStruct((n_slots, width), F32)]
        + [jax.ShapeDtypeStruct(w.shape, BF16) for w in flat],
        grid_spec=pltpu.PrefetchScalarGridSpec(
            num_scalar_prefetch=3,
            grid=(bsz, tiles_per_seq),
            in_specs=[pl.BlockSpec((None, td, D_MODEL), lambda b, i, *_: (b, i + 1, 0)),
                      pl.BlockSpec((None, td, 128), lambda b, i, *_: (b, i + 1, 0)),
                      pl.BlockSpec((1, D_MODEL), lambda b, i, *_: (0, 0))] + [w_spec(w) for w in flat],
            out_specs=[pl.BlockSpec(memory_space=pl.ANY)] + [w_spec(w) for w in flat],
            scratch_shapes=[pltpu.VMEM((2, 2, td, width), F32), pltpu.VMEM((tg, width), F32),
                            pltpu.SemaphoreType.DMA((2,)), pltpu.SemaphoreType.DMA],
        ),
        compiler_params=_params(("arbitrary", "arbitrary")),
        name="moe_dispatch",
    )(slot1, slot2, zero_rows, hs3, wts3, gain.reshape(1, D_MODEL), *flat)
    return out[0], [o.reshape(w.shape) for o, w in zip(out[1:], expert_weights)]


def _expert_kernel(te_ref, nt_ref, x_ref, wg_ref, wu_ref, wd_ref, o_ref, hn_ref, acc_ref, *, n_f):
    t = pl.program_id(0)
    f = pl.program_id(1)

    @pl.when(t < nt_ref[0])
    def _():
        @pl.when(f == 0)
        def _():
            hn_ref[...] = x_ref[:, :D_MODEL].astype(BF16)

        hn = hn_ref[...]
        a = jnp.dot(hn, wg_ref[...], preferred_element_type=F32)
        u = jnp.dot(hn, wu_ref[...], preferred_element_type=F32)
        h = (a * jax.nn.sigmoid(a) * u).astype(BF16)

        @pl.when(f == 0)
        def _():
            acc_ref[...] = jnp.dot(h, wd_ref[...], preferred_element_type=F32)

        @pl.when(jnp.logical_and(f > 0, f < n_f - 1))
        def _():
            acc_ref[...] += jnp.dot(h, wd_ref[...], preferred_element_type=F32)

        @pl.when(f == n_f - 1)
        def _():
            y = acc_ref[...] + jnp.dot(h, wd_ref[...], preferred_element_type=F32)
            o_ref[...] = (y * x_ref[:, D_MODEL:D_MODEL + 1]).astype(BF16)

    @pl.when(jnp.logical_and(t >= nt_ref[0], f == n_f - 1))
    def _():
        o_ref[...] = jnp.zeros_like(o_ref)


def _experts(xs, tile_expert, n_tiles_used, wg_bf, wu_bf, wd_bf, tg):
    slots = xs.shape[0]
    n_f = MOE_F_BLOCKS
    tf = wg_bf.shape[2] // n_f

    def x_map(t, f, te, nt):
        return (jnp.maximum(jnp.minimum(t, nt[0] - 1), 0), 0)

    def f_of(t, f, nt):
        return jnp.where(t < nt[0], f, n_f - 1)

    return pl.pallas_call(
        functools.partial(_expert_kernel, n_f=n_f),
        out_shape=jax.ShapeDtypeStruct((slots, D_MODEL), BF16),
        grid_spec=pltpu.PrefetchScalarGridSpec(
            num_scalar_prefetch=2,
            grid=(slots // tg, n_f),
            in_specs=[
                pl.BlockSpec((tg, D_MODEL + XS_EXTRA), x_map),
                pl.BlockSpec((None, D_MODEL, tf), lambda t, f, te, nt: (te[t], 0, f_of(t, f, nt))),
                pl.BlockSpec((None, D_MODEL, tf), lambda t, f, te, nt: (te[t], 0, f_of(t, f, nt))),
                pl.BlockSpec((None, tf, D_MODEL), lambda t, f, te, nt: (te[t], f_of(t, f, nt), 0)),
            ],
            out_specs=pl.BlockSpec((tg, D_MODEL), lambda t, f, te, nt: (t, 0)),
            scratch_shapes=[pltpu.VMEM((tg, D_MODEL), BF16), pltpu.VMEM((tg, D_MODEL), F32)],
        ),
        compiler_params=_params(("arbitrary", "arbitrary")),
        name="moe_experts",
    )(tile_expert, n_tiles_used, xs, wg_bf, wu_bf, wd_bf)


def _combine_kernel(ws_ref, hs_ref, route_ref, y_ref, o_ref, win_ref, sem, *, tc, tiles_per_seq, n_steps):
    n = pl.program_id(0) * tiles_per_seq + pl.program_id(1)
    cur = n % 2

    def window_copy(step, b, e):
        start = pl.multiple_of(ws_ref[step * N_EXPERTS + e], WIN_ALIGN)
        return pltpu.make_async_copy(y_ref.at[pl.ds(start, COMBINE_WIN)],
                                     win_ref.at[b, pl.ds(e * COMBINE_WIN, COMBINE_WIN)], sem.at[b])

    def fetch(step, b):
        for e in range(N_EXPERTS):
            window_copy(step, b, e).start()

    @pl.when(n == 0)
    def _():
        fetch(0, 0)

    @pl.when(n + 1 < n_steps)
    def _():
        fetch(n + 1, 1 - cur)

    for e in range(N_EXPERTS):
        window_copy(n, cur, e).wait()

    route = route_ref[...]
    e1, e2, s1, s2 = (route[:, c:c + 1] for c in range(4))
    start1 = jnp.zeros_like(s1)
    start2 = jnp.zeros_like(s2)
    for e in range(N_EXPERTS):
        start = ws_ref[n * N_EXPERTS + e]
        start1 = jnp.where(e1 == e, start, start1)
        start2 = jnp.where(e2 == e, start, start2)
    col1 = e1 * COMBINE_WIN + s1 - start1
    col2 = e2 * COMBINE_WIN + s2 - start2
    pos = lax.broadcasted_iota(jnp.int32, (1, N_EXPERTS * COMBINE_WIN), 1)
    pick = jnp.where(jnp.logical_or(pos == col1, pos == col2), 1.0, 0.0).astype(BF16)
    o_ref[...] = hs_ref[...] + jnp.dot(pick, win_ref[cur], preferred_element_type=F32)


def _combine(hs3, route, y, win_start, seq):
    bsz = hs3.shape[0]
    tc = COMBINE_TOKENS
    tiles_per_seq = seq // tc
    kern = functools.partial(_combine_kernel, tc=tc, tiles_per_seq=tiles_per_seq, n_steps=bsz * tiles_per_seq)
    tok = lambda w: pl.BlockSpec((tc, w), lambda b, i, ws: (b * tiles_per_seq + i, 0))
    return pl.pallas_call(
        kern,
        out_shape=jax.ShapeDtypeStruct((bsz, seq, D_MODEL), F32),
        grid_spec=pltpu.PrefetchScalarGridSpec(
            num_scalar_prefetch=1,
            grid=(bsz, tiles_per_seq),
            in_specs=[
                pl.BlockSpec((None, tc, D_MODEL), lambda b, i, ws: (b, i + T_PAD // tc, 0)),
                tok(4),
                pl.BlockSpec(memory_space=pl.ANY),
            ],
            out_specs=pl.BlockSpec((None, tc, D_MODEL), lambda b, i, ws: (b, i, 0)),
            scratch_shapes=[pltpu.VMEM((2, N_EXPERTS * COMBINE_WIN, D_MODEL), BF16),
                            pltpu.SemaphoreType.DMA((2,))],
        ),
        compiler_params=_params(("arbitrary", "arbitrary")),
        name="moe_combine",
    )(win_start, hs3, route, y)


def _moe(hs, gain, router_w, w_gate, w_up, w_down, bsz, tp):
    seq = tp - T_PAD
    n_tok = bsz * seq
    tg = MOE_TILE
    idx, wts, cnt = _router(hs, gain, router_w, tp)
    sel = idx.reshape(bsz, tp, 128)[:, T_PAD:, :4].reshape(n_tok, 4)
    counts = cnt[0, :N_EXPERTS].astype(jnp.int32)
    padded = ((counts + tg - 1) // tg) * tg
    ends = jnp.cumsum(padded)
    starts = ends - padded
    experts = jnp.arange(N_EXPERTS, dtype=jnp.int32)[None, :]
    slot1 = jnp.sum(jnp.where(sel[:, 0:1] == experts, starts[None, :], 0), axis=1) + sel[:, 2]
    slot2 = jnp.sum(jnp.where(sel[:, 1:2] == experts, starts[None, :], 0), axis=1) + sel[:, 3]
    slot1, slot2 = slot1.astype(jnp.int32), slot2.astype(jnp.int32)
    n_slots = 2 * n_tok + N_EXPERTS * tg
    tile_start = jnp.arange(n_slots // tg, dtype=jnp.int32)[:, None] * tg
    tile_expert = jnp.minimum(jnp.sum((tile_start >= ends[None, :]).astype(jnp.int32), axis=1), N_EXPERTS - 1)
    n_tiles_used = (ends[-1:] // tg).astype(jnp.int32)
    tail = n_slots - tg * (1 + jnp.arange(N_EXPERTS, dtype=jnp.int32))
    zero_rows = jnp.concatenate([jnp.maximum(ends - tg, 0), tail]).astype(jnp.int32)

    tc = COMBINE_TOKENS
    first = jnp.minimum(
        jnp.min(jnp.where(sel[:, 0:1] == experts, slot1[:, None], n_slots).reshape(n_tok // tc, tc, N_EXPERTS), axis=1),
        jnp.min(jnp.where(sel[:, 1:2] == experts, slot2[:, None], n_slots).reshape(n_tok // tc, tc, N_EXPERTS), axis=1))
    win_start = jnp.where(first == n_slots, 0,
                          jnp.minimum(first // WIN_ALIGN * WIN_ALIGN, n_slots - COMBINE_WIN))
    route = jnp.concatenate([sel[:, :2], slot1[:, None], slot2[:, None]], axis=1)

    hs3 = hs.reshape(bsz, tp, D_MODEL)
    xs, (wg_bf, wu_bf, wd_bf) = _dispatch(hs3, wts.reshape(bsz, tp, 128), gain, slot1, slot2, zero_rows,
                                          n_slots, tg, (w_gate, w_up, w_down))
    y = _experts(xs, tile_expert.astype(jnp.int32), n_tiles_used, wg_bf, wu_bf, wd_bf, tg)
    return _combine(hs3, route, y, win_start.reshape(-1).astype(jnp.int32), seq)


def _permute_qk_cols(w):
    return w.reshape(D_MODEL, 2, ATT_HEADS, ATT_QK_DIM).transpose(0, 2, 1, 3).reshape(D_MODEL, 512)


def kernel(x, meta_tokens, norm1_gain, norm2_gain, w_in, q_norm_gain, k_norm_gain, diff_lambda,
           attn_sub_gain, rel_bias, conv_w, hgrn_lb_logits, hgrn_out_gain, w_branch, w_out,
           ffn_w_gate, ffn_w_up, ffn_w_down, router_w, moe_w_gate, moe_w_up, moe_w_down):
    bsz, seq, _ = x.shape
    depth = w_in.shape[0]
    tp = T_PAD + seq
    assert tp % ATT_TILE == 0 and depth == 2

    head = jnp.concatenate([jnp.zeros((PAD0, D_MODEL), x.dtype), meta_tokens.astype(x.dtype)], axis=0)
    hs = jnp.concatenate([jnp.broadcast_to(head[None], (bsz, T_PAD, D_MODEL)), x], axis=1)
    hs = hs.reshape(bsz * tp, D_MODEL)

    lb_all = jnp.cumsum(jax.nn.softmax(hgrn_lb_logits.astype(F32), axis=0), axis=0)
    lb_all = lb_all - lb_all[0]
    btab = _attn_bias_tables(rel_bias, ATT_TILE)

    def permuted(w):
        return jnp.concatenate([_permute_qk_cols(w[:, :512]), _permute_qk_cols(w[:, 512:1024])], axis=1)

    later = dict(branch0=w_branch[0], out0=w_out[0], gate=ffn_w_gate[0], up=ffn_w_up[0], down=ffn_w_down[0],
                 w_in1=w_in[1], branch1=w_branch[1], out1=w_out[1])
    bf = {}
    out = None
    for layer in range(depth):
        if layer == 0:
            w = w_in[0]
            w_bf = jnp.concatenate([permuted(w), w[:, 1024:]], axis=1).astype(BF16)
        else:
            w_bf = bf["w_in1"].at[:, :1024].set(permuted(bf["w_in1"]))
        qk_gain = jnp.concatenate([jnp.tile(q_norm_gain[layer].astype(F32), 8) * (ATT_QK_DIM ** -0.5 * LOG2E),
                                   jnp.tile(k_norm_gain[layer].astype(F32), 8)]).reshape(1, COL_TILE)
        proj = _inproj(hs, norm1_gain[layer], w_bf, qk_gain)
        proj3 = proj.reshape(bsz, tp, IN_COLS)

        lam_init = 0.8 - 0.6 * math.exp(-0.3 * layer)
        lp = diff_lambda[layer].astype(F32)
        lam = jnp.exp(jnp.sum(lp[0] * lp[1])) - jnp.exp(jnp.sum(lp[2] * lp[3])) + lam_init
        u_att, cast = _diff_attention(proj3, lam.reshape(1), attn_sub_gain[layer].astype(F32), btab,
                                      1.0 - lam_init, tuple(later.values()) if layer == 0 else ())
        if layer == 0:
            bf = dict(zip(later.keys(), cast))

        lb = lb_all[layer]
        u_hgrn = _hgrn(proj3, jnp.log(lb), jnp.log1p(-lb), hgrn_out_gain[layer].astype(F32))

        hs = _merge(hs, u_att.reshape(bsz * tp, 512), u_hgrn.reshape(bsz * tp, 512), proj,
                    conv_w[layer].astype(F32), bf["branch%d" % layer], bf["out%d" % layer], tp)

        j = layer // 2
        if layer % 2 == 0:
            hs = _ffn(hs, norm2_gain[layer], bf["gate"], bf["up"], bf["down"])
        else:
            out = _moe(hs, norm2_gain[layer], router_w[j], moe_w_gate[j], moe_w_up[j], moe_w_down[j], bsz, tp)
    return out
```

```python
import functools
import math

import numpy as np
import jax
import jax.numpy as jnp
from jax import lax
from jax.experimental import pallas as pl
from jax.experimental.pallas import tpu as pltpu

F32 = jnp.float32
BF16 = jnp.bfloat16

D_MODEL = 1024
N_META = 16
EPS = 1e-6
ATT_HEADS = 4
ATT_QK_DIM = 64
ATT_V_DIM = 128
REL_BUCKETS = 32
REL_MAX_DIST = 128
CONV_K = 3
HGRN_HEADS = 4
HGRN_D = 128
N_EXPERTS = 8
IN_COLS = 8192

T_PAD = 128
PAD0 = T_PAD - N_META
ATT_TILE = 384
VT_ONES = 16
LOG2E = math.log2(math.e)
HGRN_CHUNK = 128
HGRN_SUB = 16
COL_TILE = 1024
GROUP_MEAN_WIDTH = 256
MOE_TILE = 512
MOE_F_BLOCKS = 2
XS_EXTRA = 128
COMBINE_TOKENS = 128
WIN_ALIGN = 16
COMBINE_WIN = COMBINE_TOKENS + WIN_ALIGN
MASK = -1e30
VMEM_LIMIT = 56 * 1024 * 1024

BLK_Q, BLK_K, BLK_V = 0, 4, 8
BLK_RQ, BLK_RF, BLK_RI, BLK_RG = 24, 28, 32, 36
BLK_CB, BLK_CC, BLK_CH = 3, 4, 5
BLK_GATE = 5


def _row_tile(rows, target):
    n = rows // 128
    best = 1
    for d in range(1, n + 1):
        if n % d == 0 and d * 128 <= target:
            best = d
    return best * 128


def _params(sem, vmem=VMEM_LIMIT):
    return pltpu.CompilerParams(dimension_semantics=sem, vmem_limit_bytes=vmem)


def _inproj_kernel(x_ref, g_ref, w_ref, qkg_ref, gm_ref, o_ref, xn_ref):
    j = pl.program_id(1)

    @pl.when(j == 0)
    def _():
        x = x_ref[...]
        ms = jnp.mean(x * x, axis=-1, keepdims=True)
        xn_ref[...] = (x * lax.rsqrt(ms + EPS) * g_ref[...]).astype(BF16)

    acc = jnp.dot(xn_ref[...], w_ref[...], preferred_element_type=F32)

    @pl.when(j == 0)
    def _():
        sq = acc * acc
        hi = sq.astype(BF16)
        lo = (sq - hi.astype(F32)).astype(BF16)
        gm = gm_ref[...]
        width = gm.shape[0]
        ms = jnp.concatenate(
            [jnp.dot(hi[:, c:c + width], gm, preferred_element_type=F32)
             + jnp.dot(lo[:, c:c + width], gm, preferred_element_type=F32)
             for c in range(0, COL_TILE, width)], axis=1)
        o_ref[...] = (acc * lax.rsqrt(ms + EPS) * qkg_ref[...]).astype(BF16)

    @pl.when(j > 0)
    def _():
        o_ref[...] = acc.astype(BF16)


def _inproj(hs, gain, w_bf, qk_gain):
    rows = hs.shape[0]
    tm = _row_tile(rows, 1536)
    n_col = IN_COLS // COL_TILE
    assert COL_TILE == 4 * ATT_HEADS * ATT_QK_DIM
    grp = np.arange(GROUP_MEAN_WIDTH) // ATT_QK_DIM
    gm = jnp.asarray((grp[:, None] == grp[None, :]).astype(np.float32) / ATT_QK_DIM, BF16)
    return pl.pallas_call(
        _inproj_kernel,
        out_shape=jax.ShapeDtypeStruct((rows, IN_COLS), BF16),
        grid=(rows // tm, n_col),
        in_specs=[
            pl.BlockSpec((tm, D_MODEL), lambda i, j: (i, 0)),
            pl.BlockSpec((1, D_MODEL), lambda i, j: (0, 0)),
            pl.BlockSpec((D_MODEL, COL_TILE), lambda i, j: (0, j)),
            pl.BlockSpec((1, COL_TILE), lambda i, j: (0, 0)),
            pl.BlockSpec((GROUP_MEAN_WIDTH, GROUP_MEAN_WIDTH), lambda i, j: (0, 0)),
        ],
        out_specs=pl.BlockSpec((tm, COL_TILE), lambda i, j: (i, j)),
        scratch_shapes=[pltpu.VMEM((tm, D_MODEL), BF16)],
        compiler_params=_params(("parallel", "arbitrary")),
        name="inproj",
    )(hs, gain.reshape(1, D_MODEL), w_bf, qk_gain, gm)


def _rel_bucket_table(n_max):
    n = np.arange(n_max, dtype=np.int64)
    max_exact = REL_BUCKETS // 2
    nf = np.maximum(n, 1).astype(np.float32)
    large = max_exact + (np.log(nf / np.float32(max_exact)) / np.float32(math.log(REL_MAX_DIST / max_exact))
                         * np.float32(REL_BUCKETS - max_exact)).astype(np.int32)
    large = np.minimum(large, REL_BUCKETS - 1)
    return np.where(n < max_exact, n, large).astype(np.int32)


def _attn_bias_tables(rel_bias, t):
    bucket = _rel_bucket_table(2 * t)
    assert np.all(bucket[t + 1:] == REL_BUCKETS - 1) and np.all(np.diff(bucket) >= 0)
    first_dist = tuple(int(np.searchsorted(bucket, b, side="left")) for b in range(REL_BUCKETS))
    return pl.pallas_call(
        functools.partial(_bias_kernel, t=t, first_dist=first_dist),
        out_shape=jax.ShapeDtypeStruct((ATT_HEADS, 6, t, t), F32),
        grid=(ATT_HEADS,),
        in_specs=[pl.BlockSpec(memory_space=pltpu.SMEM)],
        out_specs=pl.BlockSpec((None, 6, t, t), lambda h: (h, 0, 0, 0)),
        compiler_params=_params(("parallel",)),
        name="attn_bias",
    )(rel_bias.astype(F32))


def _bias_kernel(rb_ref, o_ref, *, t, first_dist):
    h = pl.program_id(0)
    key = lax.broadcasted_iota(jnp.int32, (t, t), 0)
    qry = lax.broadcasted_iota(jnp.int32, (t, t), 1)
    far = rb_ref[REL_BUCKETS - 1, h]

    def table(n):
        val = jnp.full((t, t), rb_ref[0, h] - far, F32)
        for b in range(1, REL_BUCKETS):
            val = jnp.where(n >= first_dist[b], rb_ref[b, h] - far, val)
        return val * LOG2E

    n0 = qry - key
    diag = jnp.where(n0 >= 0, table(n0), MASK)
    near = table(n0 + t)
    zero = jnp.zeros((t, t), F32)
    for kind, tab in enumerate((diag, near, zero)):
        o_ref[kind] = tab
        o_ref[kind + 3] = jnp.where(key < PAD0, MASK, tab)


def _cast_specs(arrays, n_steps, step_of):
    flat = [w.reshape(-1, w.shape[-1]) for w in arrays]
    assert all(w.shape[0] % (n_steps * WIN_ALIGN) == 0 for w in flat)
    specs = [pl.BlockSpec((w.shape[0] // n_steps, w.shape[1]), lambda *idx: (step_of(*idx), 0)) for w in flat]
    return flat, specs, [jax.ShapeDtypeStruct(w.shape, BF16) for w in flat]


def _attn_kernel(lam_ref, q_ref, k_ref, v_ref, bt_ref, sg_ref, *refs, t, out_scale, n_cast):
    cast_in, (o_ref, *cast_out), (m_ref, acc_ref, vt_ref) = refs[:n_cast], refs[n_cast:2 * n_cast + 1], refs[-3:]
    for src, dst in zip(cast_in, cast_out):
        dst[...] = src[...].astype(BF16)
    n_t = vt_ref.shape[0]
    for j in range(n_t):
        vt_ref[j, :ATT_V_DIM, :] = v_ref[j * t:(j + 1) * t, :].astype(F32).T.astype(BF16)
        vt_ref[j, ATT_V_DIM:, :] = jnp.ones((VT_ONES, t), BF16)
    lane = lax.broadcasted_iota(jnp.int32, (1, 2 * ATT_QK_DIM), 1)
    nt = (((1,), (1,)), ((), ()))

    def tile_rows(i):
        return pl.ds(pl.multiple_of(i * t, t), t)

    def scores(i, j):
        q = q_ref[tile_rows(i), :]
        zero = jnp.zeros_like(q)
        q_cat = jnp.concatenate([jnp.where(lane < ATT_QK_DIM, q, zero), jnp.where(lane >= ATT_QK_DIM, q, zero)],
                                axis=0)
        bias = bt_ref[jnp.minimum(i - j, 2) + jnp.where(j == 0, 3, 0)]
        s = (lax.dot_general(k_ref[tile_rows(j), :], q_cat, nt, preferred_element_type=F32)
             + jnp.concatenate([bias, bias], axis=1))
        return s, jnp.max(s, axis=0, keepdims=True)

    def consume(j, scored):
        s, s_max = scored
        m_old = m_ref[...]
        m_new = jnp.maximum(m_old, s_max)
        m_ref[...] = m_new
        p = jnp.exp2(s - m_new).astype(BF16)
        acc_ref[...] = (jnp.exp2(m_old - m_new) * acc_ref[...]
                        + jnp.dot(vt_ref[j], p, preferred_element_type=F32))

    def reset():
        m_ref[...] = jnp.full(m_ref.shape, MASK, F32)
        acc_ref[...] = jnp.zeros(acc_ref.shape, F32)

    def finish(i):
        acc = acc_ref[...]
        a1, a2 = acc[:, :t], acc[:, t:]
        o_t = (a1[:ATT_V_DIM] / a1[ATT_V_DIM:ATT_V_DIM + 1]
               - lam_ref[0] * (a2[:ATT_V_DIM] / a2[ATT_V_DIM:ATT_V_DIM + 1]))
        o = o_t.T
        ms = jnp.mean(o * o, axis=-1, keepdims=True)
        y = o * lax.rsqrt(ms + EPS) * (sg_ref[...] * out_scale)
        row = i * t + lax.broadcasted_iota(jnp.int32, (t, 1), 0)
        o_ref[tile_rows(i), :] = jnp.where(row >= PAD0, y, 0.0).astype(BF16)
        reset()

    def step(_, carry):
        i, j, s = carry
        last = j == i
        ni = jnp.where(last, i + 1, i)
        nj = jnp.where(last, 0, j + 1)
        nxt = scores(jnp.minimum(ni, n_t - 1), nj)
        consume(j, s)

        @pl.when(last)
        def _():
            finish(i)

        return ni, nj, nxt

    reset()
    first = jnp.int32(0)
    lax.fori_loop(0, n_t * (n_t + 1) // 2, step, (first, first, scores(first, first)), unroll=3)


def _diff_attention(proj3, lam, sub_gain, btab, out_scale, casts=()):
    bsz, tp, _ = proj3.shape
    t = ATT_TILE
    n_t = tp // t
    rows = ATT_V_DIM + VT_ONES
    kern = functools.partial(_attn_kernel, t=t, out_scale=out_scale, n_cast=len(casts))
    seq = lambda blk: pl.BlockSpec((None, tp, 128), lambda b, h: (b, 0, blk + h))
    flat, cast_specs, cast_shapes = _cast_specs(casts, bsz * ATT_HEADS, lambda b, h: b * ATT_HEADS + h)
    out = pl.pallas_call(
        kern,
        out_shape=[jax.ShapeDtypeStruct((bsz, tp, ATT_HEADS * ATT_V_DIM), BF16)] + cast_shapes,
        grid=(bsz, ATT_HEADS),
        in_specs=[
            pl.BlockSpec(memory_space=pltpu.SMEM),
            seq(BLK_Q), seq(BLK_K), seq(BLK_V),
            pl.BlockSpec((None, 6, t, t), lambda b, h: (h, 0, 0, 0)),
            pl.BlockSpec((1, ATT_V_DIM), lambda b, h: (0, 0)),
        ] + cast_specs,
        out_specs=[pl.BlockSpec((None, tp, 128), lambda b, h: (b, 0, h))] + cast_specs,
        scratch_shapes=[pltpu.VMEM((1, 2 * t), F32), pltpu.VMEM((rows, 2 * t), F32),
                        pltpu.VMEM((n_t, rows, t), BF16)],
        compiler_params=_params(("parallel", "parallel")),
        name="diff_attn",
    )(lam, proj3, proj3, proj3, btab, sub_gain.reshape(1, ATT_V_DIM), *flat)
    return out[0], [o.reshape(w.shape) for o, w in zip(out[1:], casts)]


def _split3(x):
    h1 = x.astype(BF16)
    r1 = x - h1.astype(F32)
    h2 = r1.astype(BF16)
    h3 = (r1 - h2.astype(F32)).astype(BF16)
    return h1, h2, h3


def _hgrn_kernel(q_ref, f_ref, i_ref, g_ref, la_ref, l1m_ref, og_ref, o_ref):
    c_len, sub = HGRN_CHUNK, HGRN_SUB
    half = sub // 2
    n_chunk = q_ref.shape[0] // c_len
    la, l1m, og = la_ref[...], l1m_ref[...], og_ref[...]
    rr = lax.broadcasted_iota(jnp.int32, (c_len, c_len), 0)
    cc = lax.broadcasted_iota(jnp.int32, (c_len, c_len), 1)
    tri = jnp.where(cc <= rr, 1.0, 0.0).astype(BF16)
    row8 = lax.broadcasted_iota(jnp.int32, (half, 1), 0)
    lane8 = lax.broadcasted_iota(jnp.int32, (half, c_len), 1)
    nt = (((1,), (1,)), ((), ()))

    def chunk(c, st):
        r0 = pl.multiple_of(c * c_len, c_len)
        rows = pl.ds(r0, c_len)
        z = f_ref[rows, :].astype(F32)
        qh = q_ref[rows, :].astype(F32)
        qh = qh * jax.nn.sigmoid(qh)
        v = i_ref[rows, :]
        gate = g_ref[rows, :].astype(F32)
        sp = jnp.log(1.0 + jnp.exp(-jnp.abs(z)))
        bb = l1m + jnp.minimum(z, 0.0) - sp
        log_f = jnp.maximum(la, bb) + jnp.log(1.0 + jnp.exp(-jnp.abs(la - bb)))
        valid = (r0 + lax.broadcasted_iota(jnp.int32, (c_len, 1), 0)) >= PAD0
        log_k = jnp.where(valid, l1m + jnp.minimum(-z, 0.0) - sp, -jnp.inf)
        g = sum(jnp.dot(tri, part, preferred_element_type=F32) for part in _split3(log_f))
        ck = log_k - g
        o_inter = lax.dot_general((qh * jnp.exp(g)).astype(BF16), st.astype(BF16), nt,
                                  preferred_element_type=F32)
        a_rows = []
        for a in range(c_len // sub):
            lo = a * sub
            ga = (g[lo:lo + half, :], g[lo + half:lo + sub, :])
            qa = (qh[lo:lo + half, :], qh[lo + half:lo + sub, :])
            if a == 0:
                blk = [jnp.zeros((half, c_len), F32)] * 2
            else:
                gs = g[lo - 1:lo, :]
                qd = (qh[lo:lo + sub, :] * jnp.exp(g[lo:lo + sub, :] - gs)).astype(BF16)
                kd = jnp.exp(jnp.minimum(gs - g[:lo, :], 0.0) + log_k[:lo, :]).astype(BF16)
                kd = jnp.concatenate([kd, jnp.zeros((c_len - lo, HGRN_D), BF16)], axis=0)
                a_off = lax.dot_general(qd, kd, nt, preferred_element_type=F32)
                blk = [a_off[:half, :], a_off[half:, :]]
            for s in range(sub):
                crow = ck[lo + s:lo + s + 1, :]
                for hh in range(s // half, 2):
                    col = jnp.sum(qa[hh] * jnp.exp(ga[hh] + crow), axis=-1, keepdims=True)
                    blk[hh] = jnp.where(lane8 == lo + s, col, blk[hh])
            for hh in range(2):
                a_rows.append(jnp.where(lane8 <= lo + hh * half + row8, blk[hh], 0.0))
        a_full = jnp.concatenate(a_rows, axis=0).astype(BF16)
        o = o_inter + jnp.dot(a_full, v, preferred_element_type=F32)
        g_last = g[c_len - 1:c_len, :]
        kd = jnp.exp(g_last - g + log_k).astype(BF16)
        st = st * jnp.exp(g_last) + lax.dot_general(v, kd, (((0,), (0,)), ((), ())),
                                                    preferred_element_type=F32)
        ms = jnp.mean(o * o, axis=-1, keepdims=True)
        y = o * lax.rsqrt(ms + EPS) * og * (gate * jax.nn.sigmoid(gate))
        o_ref[rows, :] = y.astype(BF16)
        return st

    lax.fori_loop(0, n_chunk, chunk, jnp.zeros((HGRN_D, HGRN_D), F32), unroll=11)


def _hgrn(proj3, log_lb, log1m_lb, out_gain):
    bsz, tp, _ = proj3.shape
    seq = lambda blk: pl.BlockSpec((None, tp, 128), lambda b, h: (b, 0, blk + h))
    chan = pl.BlockSpec((None, 1, HGRN_D), lambda b, h: (h, 0, 0))
    return pl.pallas_call(
        _hgrn_kernel,
        out_shape=jax.ShapeDtypeStruct((bsz, tp, HGRN_HEADS * HGRN_D), BF16),
        grid=(bsz, HGRN_HEADS),
        in_specs=[seq(BLK_RQ), seq(BLK_RF), seq(BLK_RI), seq(BLK_RG), chan, chan,
                  pl.BlockSpec((1, HGRN_D), lambda b, h: (0, 0))],
        out_specs=pl.BlockSpec((None, tp, 128), lambda b, h: (b, 0, h)),
        compiler_params=_params(("parallel", "parallel")),
        name="hgrn2",
    )(proj3, proj3, proj3, proj3,
      log_lb.reshape(HGRN_HEADS, 1, HGRN_D), log1m_lb.reshape(HGRN_HEADS, 1, HGRN_D),
      out_gain.reshape(1, HGRN_D))


def _merge_kernel(hs_ref, ua_ref, ur_ref, cb_ref, cc_ref, ch_ref, pc_ref, ph_ref,
                  g0_ref, g1_ref, g2_ref, cw_ref, wb_ref, wo_ref, o_ref, *, tm, tiles_per_seq):
    i = pl.program_id(0)
    row = (i % tiles_per_seq) * tm + lax.broadcasted_iota(jnp.int32, (tm, 1), 0)
    valid = row >= PAD0
    z = jnp.where(valid, cc_ref[...].astype(F32) * ch_ref[...].astype(F32), 0.0)
    halo_row = (i % tiles_per_seq) * tm - 8 + lax.broadcasted_iota(jnp.int32, (8, 1), 0)
    zp = jnp.where(halo_row >= PAD0, pc_ref[...].astype(F32) * ph_ref[...].astype(F32), 0.0)
    zz = jnp.concatenate([zp, z], axis=0)
    cw = cw_ref[...]
    y = (cw[2:3, :] * z + cw[1:2, :] * zz[7:7 + tm, :] + cw[0:1, :] * zz[6:6 + tm, :])
    u_conv = jnp.where(valid, cb_ref[...].astype(F32) * y, 0.0).astype(BF16)
    mixed = jnp.zeros((tm, D_MODEL), F32)
    for n, (u, g_ref) in enumerate(((ua_ref[...], g0_ref), (u_conv, g1_ref), (ur_ref[...], g2_ref))):
        up = jnp.dot(u, wb_ref[n], preferred_element_type=F32)
        mixed = mixed + jax.nn.sigmoid(g_ref[...].astype(F32)) * up
    o_ref[...] = hs_ref[...] + jnp.dot(mixed.astype(BF16), wo_ref[...], preferred_element_type=F32)


def _merge(hs, u_att, u_hgrn, proj, conv_w, wb_bf, wo_bf, tp):
    rows = hs.shape[0]
    tm = _row_tile(tp, 384)
    kern = functools.partial(_merge_kernel, tm=tm, tiles_per_seq=tp // tm)
    row_blk = lambda w, blk: pl.BlockSpec((tm, w), lambda i: (i, blk))
    halo = lambda blk: pl.BlockSpec((8, 512), lambda i: (jnp.maximum(i * (tm // 8) - 1, 0), blk))
    const = lambda shape: pl.BlockSpec(shape, lambda i: (0,) * len(shape))
    return pl.pallas_call(
        kern,
        out_shape=jax.ShapeDtypeStruct((rows, D_MODEL), F32),
        grid=(rows // tm,),
        in_specs=[row_blk(D_MODEL, 0), row_blk(512, 0), row_blk(512, 0),
                  row_blk(512, BLK_CB), row_blk(512, BLK_CC), row_blk(512, BLK_CH),
                  halo(BLK_CC), halo(BLK_CH),
                  row_blk(1024, BLK_GATE), row_blk(1024, BLK_GATE + 1), row_blk(1024, BLK_GATE + 2),
                  const((CONV_K, 512)), const((3, 512, D_MODEL)), const((D_MODEL, D_MODEL))],
        out_specs=row_blk(D_MODEL, 0),
        compiler_params=_params(("parallel",)),
        name="merge",
    )(hs, u_att, u_hgrn, proj, proj, proj, proj, proj, proj, proj, proj, conv_w, wb_bf, wo_bf)


def _ffn_kernel(hs_ref, g_ref, wg_ref, wu_ref, wd_ref, o_ref):
    x = hs_ref[...]
    ms = jnp.mean(x * x, axis=-1, keepdims=True)
    hn = (x * lax.rsqrt(ms + EPS) * g_ref[...]).astype(BF16)
    a = jnp.dot(hn, wg_ref[...], preferred_element_type=F32)
    u = jnp.dot(hn, wu_ref[...], preferred_element_type=F32)
    h = (a * jax.nn.sigmoid(a) * u).astype(BF16)
    o_ref[...] = x + jnp.dot(h, wd_ref[...], preferred_element_type=F32)


def _ffn(hs, gain, wg_bf, wu_bf, wd_bf):
    rows = hs.shape[0]
    d_ff = wg_bf.shape[1]
    tm = _row_tile(rows, 768)
    const = lambda shape: pl.BlockSpec(shape, lambda i: (0, 0), pipeline_mode=pl.Buffered(1))
    return pl.pallas_call(
        _ffn_kernel,
        out_shape=jax.ShapeDtypeStruct((rows, D_MODEL), F32),
        grid=(rows // tm,),
        in_specs=[pl.BlockSpec((tm, D_MODEL), lambda i: (i, 0)), const((1, D_MODEL)),
                  const((D_MODEL, d_ff)), const((D_MODEL, d_ff)), const((d_ff, D_MODEL))],
        out_specs=pl.BlockSpec((tm, D_MODEL), lambda i: (i, 0)),
        compiler_params=_params(("parallel",)),
        name="ffn_dense",
    )(hs, gain.reshape(1, D_MODEL), wg_bf, wu_bf, wd_bf)


def _router_kernel(hs_ref, g_ref, rw_ref, idx_ref, wt_ref, cnt_ref, carry_ref, *, tm, tiles_per_seq):
    i = pl.program_id(0)

    @pl.when(i == 0)
    def _():
        carry_ref[...] = jnp.zeros_like(carry_ref)

    x = hs_ref[...]
    ms = jnp.mean(x * x, axis=-1, keepdims=True)
    hn = x * lax.rsqrt(ms + EPS) * g_ref[...]
    hn_hi = hn.astype(BF16)
    hn_lo = (hn - hn_hi.astype(F32)).astype(BF16)
    logits = (jnp.dot(hn_hi, rw_ref[0], preferred_element_type=F32)
              + jnp.dot(hn_lo, rw_ref[0], preferred_element_type=F32)
              + jnp.dot(hn_hi, rw_ref[1], preferred_element_type=F32))
    lane = lax.broadcasted_iota(jnp.int32, logits.shape, 1)
    lane_f = lane.astype(F32)
    logits = jnp.where(lane < N_EXPERTS, logits, -jnp.inf)
    m1 = jnp.max(logits, axis=-1, keepdims=True)
    i1 = jnp.min(jnp.where(logits == m1, lane_f, 128.0), axis=-1, keepdims=True)
    rest = jnp.where(lane_f == i1, -jnp.inf, logits)
    m2 = jnp.max(rest, axis=-1, keepdims=True)
    i2 = jnp.min(jnp.where(rest == m2, lane_f, 128.0), axis=-1, keepdims=True)
    e2 = jnp.exp(m2 - m1)
    w1 = 1.0 / (1.0 + e2)
    w2 = e2 / (1.0 + e2)
    row = (i % tiles_per_seq) * tm + lax.broadcasted_iota(jnp.int32, (tm, 1), 0)
    real = row >= T_PAD
    hot1 = jnp.where(jnp.logical_and(real, lane_f == i1), 1.0, 0.0)
    hot2 = jnp.where(jnp.logical_and(real, lane_f == i2), 1.0, 0.0)
    both = hot1 + hot2
    rr = lax.broadcasted_iota(jnp.int32, (tm, tm), 0)
    cc = lax.broadcasted_iota(jnp.int32, (tm, tm), 1)
    earlier = jnp.where(cc < rr, 1.0, 0.0).astype(BF16)
    before = carry_ref[...] + jnp.dot(earlier, both.astype(BF16), preferred_element_type=F32)
    r1 = jnp.sum(before * hot1, axis=-1, keepdims=True)
    r2 = jnp.sum(before * hot2, axis=-1, keepdims=True)
    carry_ref[...] += jnp.sum(both, axis=0, keepdims=True)
    cnt_ref[...] = carry_ref[...]
    packed = jnp.where(lane == 0, i1, jnp.where(lane == 1, i2, jnp.where(lane == 2, r1, jnp.where(lane == 3, r2, 0.0))))
    idx_ref[...] = packed.astype(jnp.int32)
    wt_ref[...] = jnp.where(lane == 0, w1, jnp.where(lane == 1, w2, 0.0))


def _router(hs, gain, router_w, tp):
    rows = hs.shape[0]
    tm = _row_tile(tp, 384)
    rw = jnp.zeros((D_MODEL, 128), F32).at[:, :N_EXPERTS].set(router_w.astype(F32))
    rw_hi = rw.astype(BF16)
    rw = jnp.stack([rw_hi, (rw - rw_hi.astype(F32)).astype(BF16)])
    blk = pl.BlockSpec((tm, 128), lambda i: (i, 0))
    return pl.pallas_call(
        functools.partial(_router_kernel, tm=tm, tiles_per_seq=tp // tm),
        out_shape=(jax.ShapeDtypeStruct((rows, 128), jnp.int32), jax.ShapeDtypeStruct((rows, 128), F32),
                   jax.ShapeDtypeStruct((1, 128), F32)),
        grid=(rows // tm,),
        in_specs=[pl.BlockSpec((tm, D_MODEL), lambda i: (i, 0)),
                  pl.BlockSpec((1, D_MODEL), lambda i: (0, 0)),
                  pl.BlockSpec((2, D_MODEL, 128), lambda i: (0, 0, 0))],
        out_specs=(blk, blk, pl.BlockSpec((1, 128), lambda i: (0, 0))),
        scratch_shapes=[pltpu.VMEM((1, 128), F32)],
        compiler_params=_params(("arbitrary",)),
        name="moe_router",
    )(hs, gain.reshape(1, D_MODEL), rw)


def _dispatch_kernel(s1_ref, s2_ref, zr_ref, hs_ref, wt_ref, g_ref, wg_ref, wu_ref, wd_ref,
                     xs_ref, wg_out, wu_out, wd_out, rows_ref, zero_ref, sem, zsem,
                     *, td, tg, tiles_per_seq, n_steps):
    for src, dst in ((wg_ref, wg_out), (wu_ref, wu_out), (wd_ref, wd_out)):
        dst[...] = src[...].astype(BF16)
    n = pl.program_id(0) * tiles_per_seq + pl.program_id(1)
    buf = n % 2
    base = n * td

    def row_copy(r, which):
        slot = (s1_ref, s2_ref)[which][base + r]
        return pltpu.make_async_copy(rows_ref.at[buf, which, pl.ds(r, 1)], xs_ref.at[pl.ds(slot, 1)],
                                     sem.at[buf])

    def wait_rows(b):
        def body(r, c):
            pltpu.make_async_copy(rows_ref.at[b, 0, pl.ds(0, 1)], xs_ref.at[pl.ds(0, 1)], sem.at[b]).wait()
            return c
        lax.fori_loop(0, 2 * td, body, 0)

    @pl.when(n == 0)
    def _():
        zero_ref[...] = jnp.zeros_like(zero_ref)

        def fill(e):
            return pltpu.make_async_copy(zero_ref, xs_ref.at[pl.ds(pl.multiple_of(zr_ref[e], tg), tg)], zsem)

        for e in range(N_EXPERTS, 2 * N_EXPERTS):
            fill(e).start()
        for e in range(N_EXPERTS, 2 * N_EXPERTS):
            fill(e).wait()
        for action in ("start", "wait"):
            getattr(fill(0), action)()
            for e in range(1, N_EXPERTS):
                @pl.when(zr_ref[e] != zr_ref[e - 1])
                def _():
                    getattr(fill(e), action)()

    @pl.when(n >= 2)
    def _():
        wait_rows(buf)

    x = hs_ref[...]
    ms = jnp.mean(x * x, axis=-1, keepdims=True)
    hn = x * lax.rsqrt(ms + EPS) * g_ref[...]
    wt = wt_ref[...]
    for which in range(2):
        rows_ref[buf, which, :, :D_MODEL] = hn
        rows_ref[buf, which, :, D_MODEL:] = jnp.broadcast_to(wt[:, which:which + 1], (td, XS_EXTRA))

    def issue(r, c):
        row_copy(r, 0).start()
        row_copy(r, 1).start()
        return c

    lax.fori_loop(0, td, issue, 0)

    @pl.when(n == n_steps - 1)
    def _():
        wait_rows(buf)
        if n_steps > 1:
            wait_rows(1 - buf)


def _dispatch(hs3, wts3, gain, slot1, slot2, zero_rows, n_slots, tg, expert_weights):
    bsz, tp, _ = hs3.shape
    td = T_PAD
    tiles_per_seq = (tp - T_PAD) // td
    n_steps = bsz * tiles_per_seq
    width = D_MODEL + XS_EXTRA
    kern = functools.partial(_dispatch_kernel, td=td, tg=tg, tiles_per_seq=tiles_per_seq, n_steps=n_steps)
    flat = [w.reshape(-1, w.shape[-1]) for w in expert_weights]
    assert all(w.shape[0] % (n_steps * WIN_ALIGN) == 0 for w in flat)
    w_spec = lambda w: pl.BlockSpec((w.shape[0] // n_steps, w.shape[1]),
                                    lambda b, i, *_: (b * tiles_per_seq + i, 0))
    out = pl.pallas_call(
        kern,
        out_shape=[jax.ShapeDtypeStruct((n_slots, width), F32)]
        + [jax.ShapeDtypeStruct(w.shape, BF16) for w in flat],
        grid_spec=pltpu.PrefetchScalarGridSpec(
            num_scalar_prefetch=3,
            grid=(bsz, tiles_per_seq),
            in_specs=[pl.BlockSpec((None, td, D_MODEL), lambda b, i, *_: (b, i + 1, 0)),
                      pl.BlockSpec((None, td, 128), lambda b, i, *_: (b, i + 1, 0)),
                      pl.BlockSpec((1, D_MODEL), lambda b, i, *_: (0, 0))] + [w_spec(w) for w in flat],
            out_specs=[pl.BlockSpec(memory_space=pl.ANY)] + [w_spec(w) for w in flat],
            scratch_shapes=[pltpu.VMEM((2, 2, td, width), F32), pltpu.VMEM((tg, width), F32),
                            pltpu.SemaphoreType.DMA((2,)), pltpu.SemaphoreType.DMA],
        ),
        compiler_params=_params(("arbitrary", "arbitrary")),
        name="moe_dispatch",
    )(slot1, slot2, zero_rows, hs3, wts3, gain.reshape(1, D_MODEL), *flat)
    return out[0], [o.reshape(w.shape) for o, w in zip(out[1:], expert_weights)]


def _expert_kernel(te_ref, nt_ref, x_ref, wg_ref, wu_ref, wd_ref, o_ref, hn_ref, acc_ref, *, n_f):
    t = pl.program_id(0)
    f = pl.program_id(1)

    @pl.when(t < nt_ref[0])
    def _():
        @pl.when(f == 0)
        def _():
            hn_ref[...] = x_ref[:, :D_MODEL].astype(BF16)

        hn = hn_ref[...]
        a = jnp.dot(hn, wg_ref[...], preferred_element_type=F32)
        u = jnp.dot(hn, wu_ref[...], preferred_element_type=F32)
        h = (a * jax.nn.sigmoid(a) * u).astype(BF16)

        @pl.when(f == 0)
        def _():
            acc_ref[...] = jnp.dot(h, wd_ref[...], preferred_element_type=F32)

        @pl.when(jnp.logical_and(f > 0, f < n_f - 1))
        def _():
            acc_ref[...] += jnp.dot(h, wd_ref[...], preferred_element_type=F32)

        @pl.when(f == n_f - 1)
        def _():
            y = acc_ref[...] + jnp.dot(h, wd_ref[...], preferred_element_type=F32)
            o_ref[...] = (y * x_ref[:, D_MODEL:D_MODEL + 1]).astype(BF16)

    @pl.when(jnp.logical_and(t >= nt_ref[0], f == n_f - 1))
    def _():
        o_ref[...] = jnp.zeros_like(o_ref)


def _experts(xs, tile_expert, n_tiles_used, wg_bf, wu_bf, wd_bf, tg):
    slots = xs.shape[0]
    n_f = MOE_F_BLOCKS
    tf = wg_bf.shape[2] // n_f

    def x_map(t, f, te, nt):
        return (jnp.maximum(jnp.minimum(t, nt[0] - 1), 0), 0)

    def f_of(t, f, nt):
        return jnp.where(t < nt[0], f, n_f - 1)

    return pl.pallas_call(
        functools.partial(_expert_kernel, n_f=n_f),
        out_shape=jax.ShapeDtypeStruct((slots, D_MODEL), BF16),
        grid_spec=pltpu.PrefetchScalarGridSpec(
            num_scalar_prefetch=2,
            grid=(slots // tg, n_f),
            in_specs=[
                pl.BlockSpec((tg, D_MODEL + XS_EXTRA), x_map),
                pl.BlockSpec((None, D_MODEL, tf), lambda t, f, te, nt: (te[t], 0, f_of(t, f, nt))),
                pl.BlockSpec((None, D_MODEL, tf), lambda t, f, te, nt: (te[t], 0, f_of(t, f, nt))),
                pl.BlockSpec((None, tf, D_MODEL), lambda t, f, te, nt: (te[t], f_of(t, f, nt), 0)),
            ],
            out_specs=pl.BlockSpec((tg, D_MODEL), lambda t, f, te, nt: (t, 0)),
            scratch_shapes=[pltpu.VMEM((tg, D_MODEL), BF16), pltpu.VMEM((tg, D_MODEL), F32)],
        ),
        compiler_params=_params(("arbitrary", "arbitrary")),
        name="moe_experts",
    )(tile_expert, n_tiles_used, xs, wg_bf, wu_bf, wd_bf)


def _combine_kernel(ws_ref, hs_ref, route_ref, y_ref, o_ref, win_ref, sem, *, tc, tiles_per_seq, n_steps):
    n = pl.program_id(0) * tiles_per_seq + pl.program_id(1)
    cur = n % 2

    def window_copy(step, b, e):
        start = pl.multiple_of(ws_ref[step * N_EXPERTS + e], WIN_ALIGN)
        return pltpu.make_async_copy(y_ref.at[pl.ds(start, COMBINE_WIN)],
                                     win_ref.at[b, pl.ds(e * COMBINE_WIN, COMBINE_WIN)], sem.at[b])

    def fetch(step, b):
        for e in range(N_EXPERTS):
            window_copy(step, b, e).start()

    @pl.when(n == 0)
    def _():
        fetch(0, 0)

    @pl.when(n + 1 < n_steps)
    def _():
        fetch(n + 1, 1 - cur)

    for e in range(N_EXPERTS):
        window_copy(n, cur, e).wait()

    route = route_ref[...]
    e1, e2, s1, s2 = (route[:, c:c + 1] for c in range(4))
    start1 = jnp.zeros_like(s1)
    start2 = jnp.zeros_like(s2)
    for e in range(N_EXPERTS):
        start = ws_ref[n * N_EXPERTS + e]
        start1 = jnp.where(e1 == e, start, start1)
        start2 = jnp.where(e2 == e, start, start2)
    col1 = e1 * COMBINE_WIN + s1 - start1
    col2 = e2 * COMBINE_WIN + s2 - start2
    pos = lax.broadcasted_iota(jnp.int32, (1, N_EXPERTS * COMBINE_WIN), 1)
    pick = jnp.where(jnp.logical_or(pos == col1, pos == col2), 1.0, 0.0).astype(BF16)
    o_ref[...] = hs_ref[...] + jnp.dot(pick, win_ref[cur], preferred_element_type=F32)


def _combine(hs3, route, y, win_start, seq):
    bsz = hs3.shape[0]
    tc = COMBINE_TOKENS
    tiles_per_seq = seq // tc
    kern = functools.partial(_combine_kernel, tc=tc, tiles_per_seq=tiles_per_seq, n_steps=bsz * tiles_per_seq)
    tok = lambda w: pl.BlockSpec((tc, w), lambda b, i, ws: (b * tiles_per_seq + i, 0))
    return pl.pallas_call(
        kern,
        out_shape=jax.ShapeDtypeStruct((bsz, seq, D_MODEL), F32),
        grid_spec=pltpu.PrefetchScalarGridSpec(
            num_scalar_prefetch=1,
            grid=(bsz, tiles_per_seq),
            in_specs=[
                pl.BlockSpec((None, tc, D_MODEL), lambda b, i, ws: (b, i + T_PAD // tc, 0)),
                tok(4),
                pl.BlockSpec(memory_space=pl.ANY),
            ],
            out_specs=pl.BlockSpec((None, tc, D_MODEL), lambda b, i, ws: (b, i, 0)),
            scratch_shapes=[pltpu.VMEM((2, N_EXPERTS * COMBINE_WIN, D_MODEL), BF16),
                            pltpu.SemaphoreType.DMA((2,))],
        ),
        compiler_params=_params(("arbitrary", "arbitrary")),
        name="moe_combine",
    )(win_start, hs3, route, y)


def _moe(hs, gain, router_w, w_gate, w_up, w_down, bsz, tp):
    seq = tp - T_PAD
    n_tok = bsz * seq
    tg = MOE_TILE
    idx, wts, cnt = _router(hs, gain, router_w, tp)
    sel = idx.reshape(bsz, tp, 128)[:, T_PAD:, :4].reshape(n_tok, 4)
    counts = cnt[0, :N_EXPERTS].astype(jnp.int32)
    padded = ((counts + tg - 1) // tg) * tg
    ends = jnp.cumsum(padded)
    starts = ends - padded
    experts = jnp.arange(N_EXPERTS, dtype=jnp.int32)[None, :]
    slot1 = jnp.sum(jnp.where(sel[:, 0:1] == experts, starts[None, :], 0), axis=1) + sel[:, 2]
    slot2 = jnp.sum(jnp.where(sel[:, 1:2] == experts, starts[None, :], 0), axis=1) + sel[:, 3]
    slot1, slot2 = slot1.astype(jnp.int32), slot2.astype(jnp.int32)
    n_slots = 2 * n_tok + N_EXPERTS * tg
    tile_start = jnp.arange(n_slots // tg, dtype=jnp.int32)[:, None] * tg
    tile_expert = jnp.minimum(jnp.sum((tile_start >= ends[None, :]).astype(jnp.int32), axis=1), N_EXPERTS - 1)
    n_tiles_used = (ends[-1:] // tg).astype(jnp.int32)
    tail = n_slots - tg * (1 + jnp.arange(N_EXPERTS, dtype=jnp.int32))
    zero_rows = jnp.concatenate([jnp.maximum(ends - tg, 0), tail]).astype(jnp.int32)

    tc = COMBINE_TOKENS
    first = jnp.minimum(
        jnp.min(jnp.where(sel[:, 0:1] == experts, slot1[:, None], n_slots).reshape(n_tok // tc, tc, N_EXPERTS), axis=1),
        jnp.min(jnp.where(sel[:, 1:2] == experts, slot2[:, None], n_slots).reshape(n_tok // tc, tc, N_EXPERTS), axis=1))
    win_start = jnp.where(first == n_slots, 0,
                          jnp.minimum(first // WIN_ALIGN * WIN_ALIGN, n_slots - COMBINE_WIN))
    route = jnp.concatenate([sel[:, :2], slot1[:, None], slot2[:, None]], axis=1)

    hs3 = hs.reshape(bsz, tp, D_MODEL)
    xs, (wg_bf, wu_bf, wd_bf) = _dispatch(hs3, wts.reshape(bsz, tp, 128), gain, slot1, slot2, zero_rows,
                                          n_slots, tg, (w_gate, w_up, w_down))
    y = _experts(xs, tile_expert.astype(jnp.int32), n_tiles_used, wg_bf, wu_bf, wd_bf, tg)
    return _combine(hs3, route, y, win_start.reshape(-1).astype(jnp.int32), seq)


def _permute_qk_cols(w):
    return w.reshape(D_MODEL, 2, ATT_HEADS, ATT_QK_DIM).transpose(0, 2, 1, 3).reshape(D_MODEL, 512)


def kernel(x, meta_tokens, norm1_gain, norm2_gain, w_in, q_norm_gain, k_norm_gain, diff_lambda,
           attn_sub_gain, rel_bias, conv_w, hgrn_lb_logits, hgrn_out_gain, w_branch, w_out,
           ffn_w_gate, ffn_w_up, ffn_w_down, router_w, moe_w_gate, moe_w_up, moe_w_down):
    bsz, seq, _ = x.shape
    depth = w_in.shape[0]
    tp = T_PAD + seq
    assert tp % ATT_TILE == 0 and depth == 2

    head = jnp.concatenate([jnp.zeros((PAD0, D_MODEL), x.dtype), meta_tokens.astype(x.dtype)], axis=0)
    hs = jnp.concatenate([jnp.broadcast_to(head[None], (bsz, T_PAD, D_MODEL)), x], axis=1)
    hs = hs.reshape(bsz * tp, D_MODEL)

    lb_all = jnp.cumsum(jax.nn.softmax(hgrn_lb_logits.astype(F32), axis=0), axis=0)
    lb_all = lb_all - lb_all[0]
    btab = _attn_bias_tables(rel_bias, ATT_TILE)

    def permuted(w):
        return jnp.concatenate([_permute_qk_cols(w[:, :512]), _permute_qk_cols(w[:, 512:1024])], axis=1)

    later = dict(branch0=w_branch[0], out0=w_out[0], gate=ffn_w_gate[0], up=ffn_w_up[0], down=ffn_w_down[0],
                 w_in1=w_in[1], branch1=w_branch[1], out1=w_out[1])
    bf = {}
    out = None
    for layer in range(depth):
        if layer == 0:
            w = w_in[0]
            w_bf = jnp.concatenate([permuted(w), w[:, 1024:]], axis=1).astype(BF16)
        else:
            w_bf = bf["w_in1"].at[:, :1024].set(permuted(bf["w_in1"]))
        qk_gain = jnp.concatenate([jnp.tile(q_norm_gain[layer].astype(F32), 8) * (ATT_QK_DIM ** -0.5 * LOG2E),
                                   jnp.tile(k_norm_gain[layer].astype(F32), 8)]).reshape(1, COL_TILE)
        proj = _inproj(hs, norm1_gain[layer], w_bf, qk_gain)
        proj3 = proj.reshape(bsz, tp, IN_COLS)

        lam_init = 0.8 - 0.6 * math.exp(-0.3 * layer)
        lp = diff_lambda[layer].astype(F32)
        lam = jnp.exp(jnp.sum(lp[0] * lp[1])) - jnp.exp(jnp.sum(lp[2] * lp[3])) + lam_init
        u_att, cast = _diff_attention(proj3, lam.reshape(1), attn_sub_gain[layer].astype(F32), btab,
                                      1.0 - lam_init, tuple(later.values()) if layer == 0 else ())
        if layer == 0:
            bf = dict(zip(later.keys(), cast))

        lb = lb_all[layer]
        u_hgrn = _hgrn(proj3, jnp.log(lb), jnp.log1p(-lb), hgrn_out_gain[layer].astype(F32))

        hs = _merge(hs, u_att.reshape(bsz * tp, 512), u_hgrn.reshape(bsz * tp, 512), proj,
                    conv_w[layer].astype(F32), bf["branch%d" % layer], bf["out%d" % layer], tp)

        j = layer // 2
        if layer % 2 == 0:
            hs = _ffn(hs, norm2_gain[layer], bf["gate"], bf["up"], bf["down"])
        else:
            out = _moe(hs, norm2_gain[layer], router_w[j], moe_w_gate[j], moe_w_up[j], moe_w_down[j], bsz, tp)
    return out
```

```python
import functools
import math

import numpy as np
import jax
import jax.numpy as jnp
from jax import lax
from jax.experimental import pallas as pl
from jax.experimental.pallas import tpu as pltpu

F32 = jnp.float32
BF16 = jnp.bfloat16

D_MODEL = 1024
N_META = 16
EPS = 1e-6
ATT_HEADS = 4
ATT_QK_DIM = 64
ATT_V_DIM = 128
REL_BUCKETS = 32
REL_MAX_DIST = 128
CONV_K = 3
HGRN_HEADS = 4
HGRN_D = 128
N_EXPERTS = 8
IN_COLS = 8192

T_PAD = 128
PAD0 = T_PAD - N_META
ATT_TILE = 384
VT_ONES = 16
LOG2E = math.log2(math.e)
HGRN_CHUNK = 128
HGRN_SUB = 16
COL_TILE = 1024
GROUP_MEAN_WIDTH = 256
MOE_TILE = 512
MOE_F_BLOCKS = 2
XS_EXTRA = 128
COMBINE_TOKENS = 128
WIN_ALIGN = 16
COMBINE_WIN = COMBINE_TOKENS + WIN_ALIGN
MASK = -1e30
VMEM_LIMIT = 56 * 1024 * 1024

BLK_Q, BLK_K, BLK_V = 0, 4, 8
BLK_RQ, BLK_RF, BLK_RI, BLK_RG = 24, 28, 32, 36
BLK_CB, BLK_CC, BLK_CH = 3, 4, 5
BLK_GATE = 5


def _row_tile(rows, target):
    n = rows // 128
    best = 1
    for d in range(1, n + 1):
        if n % d == 0 and d * 128 <= target:
            best = d
    return best * 128


def _params(sem, vmem=VMEM_LIMIT):
    return pltpu.CompilerParams(dimension_semantics=sem, vmem_limit_bytes=vmem)


def _inproj_kernel(x_ref, g_ref, w_ref, qkg_ref, gm_ref, o_ref, xn_ref):
    j = pl.program_id(1)

    @pl.when(j == 0)
    def _():
        x = x_ref[...]
        ms = jnp.mean(x * x, axis=-1, keepdims=True)
        xn_ref[...] = (x * lax.rsqrt(ms + EPS) * g_ref[...]).astype(BF16)

    acc = jnp.dot(xn_ref[...], w_ref[...], preferred_element_type=F32)

    @pl.when(j == 0)
    def _():
        sq = acc * acc
        hi = sq.astype(BF16)
        lo = (sq - hi.astype(F32)).astype(BF16)
        gm = gm_ref[...]
        width = gm.shape[0]
        ms = jnp.concatenate(
            [jnp.dot(hi[:, c:c + width], gm, preferred_element_type=F32)
             + jnp.dot(lo[:, c:c + width], gm, preferred_element_type=F32)
             for c in range(0, COL_TILE, width)], axis=1)
        o_ref[...] = (acc * lax.rsqrt(ms + EPS) * qkg_ref[...]).astype(BF16)

    @pl.when(j > 0)
    def _():
        o_ref[...] = acc.astype(BF16)


def _inproj(hs, gain, w_bf, qk_gain):
    rows = hs.shape[0]
    tm = _row_tile(rows, 1536)
    n_col = IN_COLS // COL_TILE
    assert COL_TILE == 4 * ATT_HEADS * ATT_QK_DIM
    grp = np.arange(GROUP_MEAN_WIDTH) // ATT_QK_DIM
    gm = jnp.asarray((grp[:, None] == grp[None, :]).astype(np.float32) / ATT_QK_DIM, BF16)
    return pl.pallas_call(
        _inproj_kernel,
        out_shape=jax.ShapeDtypeStruct((rows, IN_COLS), BF16),
        grid=(rows // tm, n_col),
        in_specs=[
            pl.BlockSpec((tm, D_MODEL), lambda i, j: (i, 0)),
            pl.BlockSpec((1, D_MODEL), lambda i, j: (0, 0)),
            pl.BlockSpec((D_MODEL, COL_TILE), lambda i, j: (0, j)),
            pl.BlockSpec((1, COL_TILE), lambda i, j: (0, 0)),
            pl.BlockSpec((GROUP_MEAN_WIDTH, GROUP_MEAN_WIDTH), lambda i, j: (0, 0)),
        ],
        out_specs=pl.BlockSpec((tm, COL_TILE), lambda i, j: (i, j)),
        scratch_shapes=[pltpu.VMEM((tm, D_MODEL), BF16)],
        compiler_params=_params(("parallel", "arbitrary")),
        name="inproj",
    )(hs, gain.reshape(1, D_MODEL), w_bf, qk_gain, gm)


def _rel_bucket_table(n_max):
    n = np.arange(n_max, dtype=np.int64)
    max_exact = REL_BUCKETS // 2
    nf = np.maximum(n, 1).astype(np.float32)
    large = max_exact + (np.log(nf / np.float32(max_exact)) / np.float32(math.log(REL_MAX_DIST / max_exact))
                         * np.float32(REL_BUCKETS - max_exact)).astype(np.int32)
    large = np.minimum(large, REL_BUCKETS - 1)
    return np.where(n < max_exact, n, large).astype(np.int32)


def _attn_bias_tables(rel_bias, t):
    bucket = _rel_bucket_table(2 * t)
    assert np.all(bucket[t + 1:] == REL_BUCKETS - 1) and np.all(np.diff(bucket) >= 0)
    first_dist = tuple(int(np.searchsorted(bucket, b, side="left")) for b in range(REL_BUCKETS))
    return pl.pallas_call(
        functools.partial(_bias_kernel, t=t, first_dist=first_dist),
        out_shape=jax.ShapeDtypeStruct((ATT_HEADS, 6, t, t), F32),
        grid=(ATT_HEADS,),
        in_specs=[pl.BlockSpec(memory_space=pltpu.SMEM)],
        out_specs=pl.BlockSpec((None, 6, t, t), lambda h: (h, 0, 0, 0)),
        compiler_params=_params(("parallel",)),
        name="attn_bias",
    )(rel_bias.astype(F32))


def _bias_kernel(rb_ref, o_ref, *, t, first_dist):
    h = pl.program_id(0)
    key = lax.broadcasted_iota(jnp.int32, (t, t), 0)
    qry = lax.broadcasted_iota(jnp.int32, (t, t), 1)
    far = rb_ref[REL_BUCKETS - 1, h]

    def table(n):
        val = jnp.full((t, t), rb_ref[0, h] - far, F32)
        for b in range(1, REL_BUCKETS):
            val = jnp.where(n >= first_dist[b], rb_ref[b, h] - far, val)
        return val * LOG2E

    n0 = qry - key
    diag = jnp.where(n0 >= 0, table(n0), MASK)
    near = table(n0 + t)
    zero = jnp.zeros((t, t), F32)
    for kind, tab in enumerate((diag, near, zero)):
        o_ref[kind] = tab
        o_ref[kind + 3] = jnp.where(key < PAD0, MASK, tab)


def _cast_specs(arrays, n_steps, step_of):
    flat = [w.reshape(-1, w.shape[-1]) for w in arrays]
    assert all(w.shape[0] % (n_steps * WIN_ALIGN) == 0 for w in flat)
    specs = [pl.BlockSpec((w.shape[0] // n_steps, w.shape[1]), lambda *idx: (step_of(*idx), 0)) for w in flat]
    return flat, specs, [jax.ShapeDtypeStruct(w.shape, BF16) for w in flat]


def _attn_kernel(lam_ref, q_ref, k_ref, v_ref, bt_ref, sg_ref, *refs, t, out_scale, n_cast):
    cast_in, (o_ref, *cast_out), (m_ref, acc_ref, vt_ref) = refs[:n_cast], refs[n_cast:2 * n_cast + 1], refs[-3:]
    for src, dst in zip(cast_in, cast_out):
        dst[...] = src[...].astype(BF16)
    n_t = vt_ref.shape[0]
    for j in range(n_t):
        vt_ref[j, :ATT_V_DIM, :] = v_ref[j * t:(j + 1) * t, :].astype(F32).T.astype(BF16)
        vt_ref[j, ATT_V_DIM:, :] = jnp.ones((VT_ONES, t), BF16)
    lane = lax.broadcasted_iota(jnp.int32, (1, 2 * ATT_QK_DIM), 1)
    nt = (((1,), (1,)), ((), ()))

    def tile_rows(i):
        return pl.ds(pl.multiple_of(i * t, t), t)

    def scores(i, j):
        q = q_ref[tile_rows(i), :]
        zero = jnp.zeros_like(q)
        q_cat = jnp.concatenate([jnp.where(lane < ATT_QK_DIM, q, zero), jnp.where(lane >= ATT_QK_DIM, q, zero)],
                                axis=0)
        bias = bt_ref[jnp.minimum(i - j, 2) + jnp.where(j == 0, 3, 0)]
        s = (lax.dot_general(k_ref[tile_rows(j), :], q_cat, nt, preferred_element_type=F32)
             + jnp.concatenate([bias, bias], axis=1))
        return s, jnp.max(s, axis=0, keepdims=True)

    def consume(j, scored):
        s, s_max = scored
        m_old = m_ref[...]
        m_new = jnp.maximum(m_old, s_max)
        m_ref[...] = m_new
        p = jnp.exp2(s - m_new).astype(BF16)
        acc_ref[...] = (jnp.exp2(m_old - m_new) * acc_ref[...]
                        + jnp.dot(vt_ref[j], p, preferred_element_type=F32))

    def reset():
        m_ref[...] = jnp.full(m_ref.shape, MASK, F32)
        acc_ref[...] = jnp.zeros(acc_ref.shape, F32)

    def finish(i):
        acc = acc_ref[...]
        a1, a2 = acc[:, :t], acc[:, t:]
        o_t = (a1[:ATT_V_DIM] / a1[ATT_V_DIM:ATT_V_DIM + 1]
               - lam_ref[0] * (a2[:ATT_V_DIM] / a2[ATT_V_DIM:ATT_V_DIM + 1]))
        o = o_t.T
        ms = jnp.mean(o * o, axis=-1, keepdims=True)
        y = o * lax.rsqrt(ms + EPS) * (sg_ref[...] * out_scale)
        row = i * t + lax.broadcasted_iota(jnp.int32, (t, 1), 0)
        o_ref[tile_rows(i), :] = jnp.where(row >= PAD0, y, 0.0).astype(BF16)
        reset()

    def step(_, carry):
        i, j, s = carry
        last = j == i
        ni = jnp.where(last, i + 1, i)
        nj = jnp.where(last, 0, j + 1)
        nxt = scores(jnp.minimum(ni, n_t - 1), nj)
        consume(j, s)

        @pl.when(last)
        def _():
            finish(i)

        return ni, nj, nxt

    reset()
    first = jnp.int32(0)
    lax.fori_loop(0, n_t * (n_t + 1) // 2, step, (first, first, scores(first, first)), unroll=6)


def _diff_attention(proj3, lam, sub_gain, btab, out_scale, casts=()):
    bsz, tp, _ = proj3.shape
    t = ATT_TILE
    n_t = tp // t
    rows = ATT_V_DIM + VT_ONES
    kern = functools.partial(_attn_kernel, t=t, out_scale=out_scale, n_cast=len(casts))
    seq = lambda blk: pl.BlockSpec((None, tp, 128), lambda b, h: (b, 0, blk + h))
    flat, cast_specs, cast_shapes = _cast_specs(casts, bsz * ATT_HEADS, lambda b, h: b * ATT_HEADS + h)
    out = pl.pallas_call(
        kern,
        out_shape=[jax.ShapeDtypeStruct((bsz, tp, ATT_HEADS * ATT_V_DIM), BF16)] + cast_shapes,
        grid=(bsz, ATT_HEADS),
        in_specs=[
            pl.BlockSpec(memory_space=pltpu.SMEM),
            seq(BLK_Q), seq(BLK_K), seq(BLK_V),
            pl.BlockSpec((None, 6, t, t), lambda b, h: (h, 0, 0, 0)),
            pl.BlockSpec((1, ATT_V_DIM), lambda b, h: (0, 0)),
        ] + cast_specs,
        out_specs=[pl.BlockSpec((None, tp, 128), lambda b, h: (b, 0, h))] + cast_specs,
        scratch_shapes=[pltpu.VMEM((1, 2 * t), F32), pltpu.VMEM((rows, 2 * t), F32),
                        pltpu.VMEM((n_t, rows, t), BF16)],
        compiler_params=_params(("parallel", "parallel")),
        name="diff_attn",
    )(lam, proj3, proj3, proj3, btab, sub_gain.reshape(1, ATT_V_DIM), *flat)
    return out[0], [o.reshape(w.shape) for o, w in zip(out[1:], casts)]


def _split3(x):
    h1 = x.astype(BF16)
    r1 = x - h1.astype(F32)
    h2 = r1.astype(BF16)
    h3 = (r1 - h2.astype(F32)).astype(BF16)
    return h1, h2, h3


def _hgrn_kernel(q_ref, f_ref, i_ref, g_ref, la_ref, l1m_ref, og_ref, o_ref):
    c_len, sub = HGRN_CHUNK, HGRN_SUB
    half = sub // 2
    n_chunk = q_ref.shape[0] // c_len
    la, l1m, og = la_ref[...], l1m_ref[...], og_ref[...]
    rr = lax.broadcasted_iota(jnp.int32, (c_len, c_len), 0)
    cc = lax.broadcasted_iota(jnp.int32, (c_len, c_len), 1)
    tri = jnp.where(cc <= rr, 1.0, 0.0).astype(BF16)
    row8 = lax.broadcasted_iota(jnp.int32, (half, 1), 0)
    lane8 = lax.broadcasted_iota(jnp.int32, (half, c_len), 1)
    nt = (((1,), (1,)), ((), ()))

    def chunk(c, st):
        r0 = pl.multiple_of(c * c_len, c_len)
        rows = pl.ds(r0, c_len)
        z = f_ref[rows, :].astype(F32)
        qh = q_ref[rows, :].astype(F32)
        qh = qh * jax.nn.sigmoid(qh)
        v = i_ref[rows, :]
        gate = g_ref[rows, :].astype(F32)
        sp = jnp.log(1.0 + jnp.exp(-jnp.abs(z)))
        bb = l1m + jnp.minimum(z, 0.0) - sp
        log_f = jnp.maximum(la, bb) + jnp.log(1.0 + jnp.exp(-jnp.abs(la - bb)))
        valid = (r0 + lax.broadcasted_iota(jnp.int32, (c_len, 1), 0)) >= PAD0
        log_k = jnp.where(valid, l1m + jnp.minimum(-z, 0.0) - sp, -jnp.inf)
        g = sum(jnp.dot(tri, part, preferred_element_type=F32) for part in _split3(log_f))
        ck = log_k - g
        o_inter = lax.dot_general((qh * jnp.exp(g)).astype(BF16), st.astype(BF16), nt,
                                  preferred_element_type=F32)
        a_rows = []
        for a in range(c_len // sub):
            lo = a * sub
            ga = (g[lo:lo + half, :], g[lo + half:lo + sub, :])
            qa = (qh[lo:lo + half, :], qh[lo + half:lo + sub, :])
            if a == 0:
                blk = [jnp.zeros((half, c_len), F32)] * 2
            else:
                gs = g[lo - 1:lo, :]
                qd = (qh[lo:lo + sub, :] * jnp.exp(g[lo:lo + sub, :] - gs)).astype(BF16)
                kd = jnp.exp(jnp.minimum(gs - g[:lo, :], 0.0) + log_k[:lo, :]).astype(BF16)
                kd = jnp.concatenate([kd, jnp.zeros((c_len - lo, HGRN_D), BF16)], axis=0)
                a_off = lax.dot_general(qd, kd, nt, preferred_element_type=F32)
                blk = [a_off[:half, :], a_off[half:, :]]
            for s in range(sub):
                crow = ck[lo + s:lo + s + 1, :]
                for hh in range(s // half, 2):
                    col = jnp.sum(qa[hh] * jnp.exp(ga[hh] + crow), axis=-1, keepdims=True)
                    blk[hh] = jnp.where(lane8 == lo + s, col, blk[hh])
            for hh in range(2):
                a_rows.append(jnp.where(lane8 <= lo + hh * half + row8, blk[hh], 0.0))
        a_full = jnp.concatenate(a_rows, axis=0).astype(BF16)
        o = o_inter + jnp.dot(a_full, v, preferred_element_type=F32)
        g_last = g[c_len - 1:c_len, :]
        kd = jnp.exp(g_last - g + log_k).astype(BF16)
        st = st * jnp.exp(g_last) + lax.dot_general(v, kd, (((0,), (0,)), ((), ())),
                                                    preferred_element_type=F32)
        ms = jnp.mean(o * o, axis=-1, keepdims=True)
        y = o * lax.rsqrt(ms + EPS) * og * (gate * jax.nn.sigmoid(gate))
        o_ref[rows, :] = y.astype(BF16)
        return st

    lax.fori_loop(0, n_chunk, chunk, jnp.zeros((HGRN_D, HGRN_D), F32), unroll=11)


def _hgrn(proj3, log_lb, log1m_lb, out_gain):
    bsz, tp, _ = proj3.shape
    seq = lambda blk: pl.BlockSpec((None, tp, 128), lambda b, h: (b, 0, blk + h))
    chan = pl.BlockSpec((None, 1, HGRN_D), lambda b, h: (h, 0, 0))
    return pl.pallas_call(
        _hgrn_kernel,
        out_shape=jax.ShapeDtypeStruct((bsz, tp, HGRN_HEADS * HGRN_D), BF16),
        grid=(bsz, HGRN_HEADS),
        in_specs=[seq(BLK_RQ), seq(BLK_RF), seq(BLK_RI), seq(BLK_RG), chan, chan,
                  pl.BlockSpec((1, HGRN_D), lambda b, h: (0, 0))],
        out_specs=pl.BlockSpec((None, tp, 128), lambda b, h: (b, 0, h)),
        compiler_params=_params(("parallel", "parallel")),
        name="hgrn2",
    )(proj3, proj3, proj3, proj3,
      log_lb.reshape(HGRN_HEADS, 1, HGRN_D), log1m_lb.reshape(HGRN_HEADS, 1, HGRN_D),
      out_gain.reshape(1, HGRN_D))


def _merge_kernel(hs_ref, ua_ref, ur_ref, cb_ref, cc_ref, ch_ref, pc_ref, ph_ref,
                  g0_ref, g1_ref, g2_ref, cw_ref, wb_ref, wo_ref, o_ref, *, tm, tiles_per_seq):
    i = pl.program_id(0)
    row = (i % tiles_per_seq) * tm + lax.broadcasted_iota(jnp.int32, (tm, 1), 0)
    valid = row >= PAD0
    z = jnp.where(valid, cc_ref[...].astype(F32) * ch_ref[...].astype(F32), 0.0)
    halo_row = (i % tiles_per_seq) * tm - 8 + lax.broadcasted_iota(jnp.int32, (8, 1), 0)
    zp = jnp.where(halo_row >= PAD0, pc_ref[...].astype(F32) * ph_ref[...].astype(F32), 0.0)
    zz = jnp.concatenate([zp, z], axis=0)
    cw = cw_ref[...]
    y = (cw[2:3, :] * z + cw[1:2, :] * zz[7:7 + tm, :] + cw[0:1, :] * zz[6:6 + tm, :])
    u_conv = jnp.where(valid, cb_ref[...].astype(F32) * y, 0.0).astype(BF16)
    mixed = jnp.zeros((tm, D_MODEL), F32)
    for n, (u, g_ref) in enumerate(((ua_ref[...], g0_ref), (u_conv, g1_ref), (ur_ref[...], g2_ref))):
        up = jnp.dot(u, wb_ref[n], preferred_element_type=F32)
        mixed = mixed + jax.nn.sigmoid(g_ref[...].astype(F32)) * up
    o_ref[...] = hs_ref[...] + jnp.dot(mixed.astype(BF16), wo_ref[...], preferred_element_type=F32)


def _merge(hs, u_att, u_hgrn, proj, conv_w, wb_bf, wo_bf, tp):
    rows = hs.shape[0]
    tm = _row_tile(tp, 384)
    kern = functools.partial(_merge_kernel, tm=tm, tiles_per_seq=tp // tm)
    row_blk = lambda w, blk: pl.BlockSpec((tm, w), lambda i: (i, blk))
    halo = lambda blk: pl.BlockSpec((8, 512), lambda i: (jnp.maximum(i * (tm // 8) - 1, 0), blk))
    const = lambda shape: pl.BlockSpec(shape, lambda i: (0,) * len(shape))
    return pl.pallas_call(
        kern,
        out_shape=jax.ShapeDtypeStruct((rows, D_MODEL), F32),
        grid=(rows // tm,),
        in_specs=[row_blk(D_MODEL, 0), row_blk(512, 0), row_blk(512, 0),
                  row_blk(512, BLK_CB), row_blk(512, BLK_CC), row_blk(512, BLK_CH),
                  halo(BLK_CC), halo(BLK_CH),
                  row_blk(1024, BLK_GATE), row_blk(1024, BLK_GATE + 1), row_blk(1024, BLK_GATE + 2),
                  const((CONV_K, 512)), const((3, 512, D_MODEL)), const((D_MODEL, D_MODEL))],
        out_specs=row_blk(D_MODEL, 0),
        compiler_params=_params(("parallel",)),
        name="merge",
    )(hs, u_att, u_hgrn, proj, proj, proj, proj, proj, proj, proj, proj, conv_w, wb_bf, wo_bf)


def _ffn_kernel(hs_ref, g_ref, wg_ref, wu_ref, wd_ref, o_ref):
    x = hs_ref[...]
    ms = jnp.mean(x * x, axis=-1, keepdims=True)
    hn = (x * lax.rsqrt(ms + EPS) * g_ref[...]).astype(BF16)
    a = jnp.dot(hn, wg_ref[...], preferred_element_type=F32)
    u = jnp.dot(hn, wu_ref[...], preferred_element_type=F32)
    h = (a * jax.nn.sigmoid(a) * u).astype(BF16)
    o_ref[...] = x + jnp.dot(h, wd_ref[...], preferred_element_type=F32)


def _ffn(hs, gain, wg_bf, wu_bf, wd_bf):
    rows = hs.shape[0]
    d_ff = wg_bf.shape[1]
    tm = _row_tile(rows, 768)
    const = lambda shape: pl.BlockSpec(shape, lambda i: (0, 0), pipeline_mode=pl.Buffered(1))
    return pl.pallas_call(
        _ffn_kernel,
        out_shape=jax.ShapeDtypeStruct((rows, D_MODEL), F32),
        grid=(rows // tm,),
        in_specs=[pl.BlockSpec((tm, D_MODEL), lambda i: (i, 0)), const((1, D_MODEL)),
                  const((D_MODEL, d_ff)), const((D_MODEL, d_ff)), const((d_ff, D_MODEL))],
        out_specs=pl.BlockSpec((tm, D_MODEL), lambda i: (i, 0)),
        compiler_params=_params(("parallel",)),
        name="ffn_dense",
    )(hs, gain.reshape(1, D_MODEL), wg_bf, wu_bf, wd_bf)


def _router_kernel(hs_ref, g_ref, rw_ref, idx_ref, wt_ref, cnt_ref, carry_ref, *, tm, tiles_per_seq):
    i = pl.program_id(0)

    @pl.when(i == 0)
    def _():
        carry_ref[...] = jnp.zeros_like(carry_ref)

    x = hs_ref[...]
    ms = jnp.mean(x * x, axis=-1, keepdims=True)
    hn = x * lax.rsqrt(ms + EPS) * g_ref[...]
    hn_hi = hn.astype(BF16)
    hn_lo = (hn - hn_hi.astype(F32)).astype(BF16)
    logits = (jnp.dot(hn_hi, rw_ref[0], preferred_element_type=F32)
              + jnp.dot(hn_lo, rw_ref[0], preferred_element_type=F32)
              + jnp.dot(hn_hi, rw_ref[1], preferred_element_type=F32))
    lane = lax.broadcasted_iota(jnp.int32, logits.shape, 1)
    lane_f = lane.astype(F32)
    logits = jnp.where(lane < N_EXPERTS, logits, -jnp.inf)
    m1 = jnp.max(logits, axis=-1, keepdims=True)
    i1 = jnp.min(jnp.where(logits == m1, lane_f, 128.0), axis=-1, keepdims=True)
    rest = jnp.where(lane_f == i1, -jnp.inf, logits)
    m2 = jnp.max(rest, axis=-1, keepdims=True)
    i2 = jnp.min(jnp.where(rest == m2, lane_f, 128.0), axis=-1, keepdims=True)
    e2 = jnp.exp(m2 - m1)
    w1 = 1.0 / (1.0 + e2)
    w2 = e2 / (1.0 + e2)
    row = (i % tiles_per_seq) * tm + lax.broadcasted_iota(jnp.int32, (tm, 1), 0)
    real = row >= T_PAD
    hot1 = jnp.where(jnp.logical_and(real, lane_f == i1), 1.0, 0.0)
    hot2 = jnp.where(jnp.logical_and(real, lane_f == i2), 1.0, 0.0)
    both = hot1 + hot2
    rr = lax.broadcasted_iota(jnp.int32, (tm, tm), 0)
    cc = lax.broadcasted_iota(jnp.int32, (tm, tm), 1)
    earlier = jnp.where(cc < rr, 1.0, 0.0).astype(BF16)
    before = carry_ref[...] + jnp.dot(earlier, both.astype(BF16), preferred_element_type=F32)
    r1 = jnp.sum(before * hot1, axis=-1, keepdims=True)
    r2 = jnp.sum(before * hot2, axis=-1, keepdims=True)
    carry_ref[...] += jnp.sum(both, axis=0, keepdims=True)
    cnt_ref[...] = carry_ref[...]
    packed = jnp.where(lane == 0, i1, jnp.where(lane == 1, i2, jnp.where(lane == 2, r1, jnp.where(lane == 3, r2, 0.0))))
    idx_ref[...] = packed.astype(jnp.int32)
    wt_ref[...] = jnp.where(lane == 0, w1, jnp.where(lane == 1, w2, 0.0))


def _router(hs, gain, router_w, tp):
    rows = hs.shape[0]
    tm = _row_tile(tp, 384)
    rw = jnp.zeros((D_MODEL, 128), F32).at[:, :N_EXPERTS].set(router_w.astype(F32))
    rw_hi = rw.astype(BF16)
    rw = jnp.stack([rw_hi, (rw - rw_hi.astype(F32)).astype(BF16)])
    blk = pl.BlockSpec((tm, 128), lambda i: (i, 0))
    return pl.pallas_call(
        functools.partial(_router_kernel, tm=tm, tiles_per_seq=tp // tm),
        out_shape=(jax.ShapeDtypeStruct((rows, 128), jnp.int32), jax.ShapeDtypeStruct((rows, 128), F32),
                   jax.ShapeDtypeStruct((1, 128), F32)),
        grid=(rows // tm,),
        in_specs=[pl.BlockSpec((tm, D_MODEL), lambda i: (i, 0)),
                  pl.BlockSpec((1, D_MODEL), lambda i: (0, 0)),
                  pl.BlockSpec((2, D_MODEL, 128), lambda i: (0, 0, 0))],
        out_specs=(blk, blk, pl.BlockSpec((1, 128), lambda i: (0, 0))),
        scratch_shapes=[pltpu.VMEM((1, 128), F32)],
        compiler_params=_params(("arbitrary",)),
        name="moe_router",
    )(hs, gain.reshape(1, D_MODEL), rw)


def _dispatch_kernel(s1_ref, s2_ref, zr_ref, hs_ref, wt_ref, g_ref, wg_ref, wu_ref, wd_ref,
                     xs_ref, wg_out, wu_out, wd_out, rows_ref, zero_ref, sem, zsem,
                     *, td, tg, tiles_per_seq, n_steps):
    for src, dst in ((wg_ref, wg_out), (wu_ref, wu_out), (wd_ref, wd_out)):
        dst[...] = src[...].astype(BF16)
    n = pl.program_id(0) * tiles_per_seq + pl.program_id(1)
    buf = n % 2
    base = n * td

    def row_copy(r, which):
        slot = (s1_ref, s2_ref)[which][base + r]
        return pltpu.make_async_copy(rows_ref.at[buf, which, pl.ds(r, 1)], xs_ref.at[pl.ds(slot, 1)],
                                     sem.at[buf])

    def wait_rows(b):
        def body(r, c):
            pltpu.make_async_copy(rows_ref.at[b, 0, pl.ds(0, 1)], xs_ref.at[pl.ds(0, 1)], sem.at[b]).wait()
            return c
        lax.fori_loop(0, 2 * td, body, 0)

    @pl.when(n == 0)
    def _():
        zero_ref[...] = jnp.zeros_like(zero_ref)

        def fill(e):
            return pltpu.make_async_copy(zero_ref, xs_ref.at[pl.ds(pl.multiple_of(zr_ref[e], tg), tg)], zsem)

        for e in range(N_EXPERTS, 2 * N_EXPERTS):
            fill(e).start()
        for e in range(N_EXPERTS, 2 * N_EXPERTS):
            fill(e).wait()
        for action in ("start", "wait"):
            getattr(fill(0), action)()
            for e in range(1, N_EXPERTS):
                @pl.when(zr_ref[e] != zr_ref[e - 1])
                def _():
                    getattr(fill(e), action)()

    @pl.when(n >= 2)
    def _():
        wait_rows(buf)

    x = hs_ref[...]
    ms = jnp.mean(x * x, axis=-1, keepdims=True)
    hn = x * lax.rsqrt(ms + EPS) * g_ref[...]
    wt = wt_ref[...]
    for which in range(2):
        rows_ref[buf, which, :, :D_MODEL] = hn
        rows_ref[buf, which, :, D_MODEL:] = jnp.broadcast_to(wt[:, which:which + 1], (td, XS_EXTRA))

    def issue(r, c):
        row_copy(r, 0).start()
        row_copy(r, 1).start()
        return c

    lax.fori_loop(0, td, issue, 0)

    @pl.when(n == n_steps - 1)
    def _():
        wait_rows(buf)
        if n_steps > 1:
            wait_rows(1 - buf)


def _dispatch(hs3, wts3, gain, slot1, slot2, zero_rows, n_slots, tg, expert_weights):
    bsz, tp, _ = hs3.shape
    td = T_PAD
    tiles_per_seq = (tp - T_PAD) // td
    n_steps = bsz * tiles_per_seq
    width = D_MODEL + XS_EXTRA
    kern = functools.partial(_dispatch_kernel, td=td, tg=tg, tiles_per_seq=tiles_per_seq, n_steps=n_steps)
    flat = [w.reshape(-1, w.shape[-1]) for w in expert_weights]
    assert all(w.shape[0] % (n_steps * WIN_ALIGN) == 0 for w in flat)
    w_spec = lambda w: pl.BlockSpec((w.shape[0] // n_steps, w.shape[1]),
                                    lambda b, i, *_: (b * tiles_per_seq + i, 0))
    out = pl.pallas_call(
        kern,
        out_shape=[jax.ShapeDtypeStruct((n_slots, width), F32)]
        + [jax.ShapeDtypeStruct(w.shape, BF16) for w in flat],
        grid_spec=pltpu.PrefetchScalarGridSpec(
            num_scalar_prefetch=3,
            grid=(bsz, tiles_per_seq),
            in_specs=[pl.BlockSpec((None, td, D_MODEL), lambda b, i, *_: (b, i + 1, 0)),
                      pl.BlockSpec((None, td, 128), lambda b, i, *_: (b, i + 1, 0)),
                      pl.BlockSpec((1, D_MODEL), lambda b, i, *_: (0, 0))] + [w_spec(w) for w in flat],
            out_specs=[pl.BlockSpec(memory_space=pl.ANY)] + [w_spec(w) for w in flat],
            scratch_shapes=[pltpu.VMEM((2, 2, td, width), F32), pltpu.VMEM((tg, width), F32),
                            pltpu.SemaphoreType.DMA((2,)), pltpu.SemaphoreType.DMA],
        ),
        compiler_params=_params(("arbitrary", "arbitrary")),
        name="moe_dispatch",
    )(slot1, slot2, zero_rows, hs3, wts3, gain.reshape(1, D_MODEL), *flat)
    return out[0], [o.reshape(w.shape) for o, w in zip(out[1:], expert_weights)]


def _expert_kernel(te_ref, nt_ref, x_ref, wg_ref, wu_ref, wd_ref, o_ref, hn_ref, acc_ref, *, n_f):
    t = pl.program_id(0)
    f = pl.program_id(1)

    @pl.when(t < nt_ref[0])
    def _():
        @pl.when(f == 0)
        def _():
            hn_ref[...] = x_ref[:, :D_MODEL].astype(BF16)

        hn = hn_ref[...]
        a = jnp.dot(hn, wg_ref[...], preferred_element_type=F32)
        u = jnp.dot(hn, wu_ref[...], preferred_element_type=F32)
        h = (a * jax.nn.sigmoid(a) * u).astype(BF16)

        @pl.when(f == 0)
        def _():
            acc_ref[...] = jnp.dot(h, wd_ref[...], preferred_element_type=F32)

        @pl.when(jnp.logical_and(f > 0, f < n_f - 1))
        def _():
            acc_ref[...] += jnp.dot(h, wd_ref[...], preferred_element_type=F32)

        @pl.when(f == n_f - 1)
        def _():
            y = acc_ref[...] + jnp.dot(h, wd_ref[...], preferred_element_type=F32)
            o_ref[...] = (y * x_ref[:, D_MODEL:D_MODEL + 1]).astype(BF16)

    @pl.when(jnp.logical_and(t >= nt_ref[0], f == n_f - 1))
    def _():
        o_ref[...] = jnp.zeros_like(o_ref)


def _experts(xs, tile_expert, n_tiles_used, wg_bf, wu_bf, wd_bf, tg):
    slots = xs.shape[0]
    n_f = MOE_F_BLOCKS
    tf = wg_bf.shape[2] // n_f

    def x_map(t, f, te, nt):
        return (jnp.maximum(jnp.minimum(t, nt[0] - 1), 0), 0)

    def f_of(t, f, nt):
        return jnp.where(t < nt[0], f, n_f - 1)

    return pl.pallas_call(
        functools.partial(_expert_kernel, n_f=n_f),
        out_shape=jax.ShapeDtypeStruct((slots, D_MODEL), BF16),
        grid_spec=pltpu.PrefetchScalarGridSpec(
            num_scalar_prefetch=2,
            grid=(slots // tg, n_f),
            in_specs=[
                pl.BlockSpec((tg, D_MODEL + XS_EXTRA), x_map),
                pl.BlockSpec((None, D_MODEL, tf), lambda t, f, te, nt: (te[t], 0, f_of(t, f, nt))),
                pl.BlockSpec((None, D_MODEL, tf), lambda t, f, te, nt: (te[t], 0, f_of(t, f, nt))),
                pl.BlockSpec((None, tf, D_MODEL), lambda t, f, te, nt: (te[t], f_of(t, f, nt), 0)),
            ],
            out_specs=pl.BlockSpec((tg, D_MODEL), lambda t, f, te, nt: (t, 0)),
            scratch_shapes=[pltpu.VMEM((tg, D_MODEL), BF16), pltpu.VMEM((tg, D_MODEL), F32)],
        ),
        compiler_params=_params(("arbitrary", "arbitrary")),
        name="moe_experts",
    )(tile_expert, n_tiles_used, xs, wg_bf, wu_bf, wd_bf)


def _combine_kernel(ws_ref, hs_ref, route_ref, y_ref, o_ref, win_ref, sem, *, tc, tiles_per_seq, n_steps):
    n = pl.program_id(0) * tiles_per_seq + pl.program_id(1)
    cur = n % 2

    def window_copy(step, b, e):
        start = pl.multiple_of(ws_ref[step * N_EXPERTS + e], WIN_ALIGN)
        return pltpu.make_async_copy(y_ref.at[pl.ds(start, COMBINE_WIN)],
                                     win_ref.at[b, pl.ds(e * COMBINE_WIN, COMBINE_WIN)], sem.at[b])

    def fetch(step, b):
        for e in range(N_EXPERTS):
            window_copy(step, b, e).start()

    @pl.when(n == 0)
    def _():
        fetch(0, 0)

    @pl.when(n + 1 < n_steps)
    def _():
        fetch(n + 1, 1 - cur)

    for e in range(N_EXPERTS):
        window_copy(n, cur, e).wait()

    route = route_ref[...]
    e1, e2, s1, s2 = (route[:, c:c + 1] for c in range(4))
    start1 = jnp.zeros_like(s1)
    start2 = jnp.zeros_like(s2)
    for e in range(N_EXPERTS):
        start = ws_ref[n * N_EXPERTS + e]
        start1 = jnp.where(e1 == e, start, start1)
        start2 = jnp.where(e2 == e, start, start2)
    col1 = e1 * COMBINE_WIN + s1 - start1
    col2 = e2 * COMBINE_WIN + s2 - start2
    pos = lax.broadcasted_iota(jnp.int32, (1, N_EXPERTS * COMBINE_WIN), 1)
    pick = jnp.where(jnp.logical_or(pos == col1, pos == col2), 1.0, 0.0).astype(BF16)
    o_ref[...] = hs_ref[...] + jnp.dot(pick, win_ref[cur], preferred_element_type=F32)


def _combine(hs3, route, y, win_start, seq):
    bsz = hs3.shape[0]
    tc = COMBINE_TOKENS
    tiles_per_seq = seq // tc
    kern = functools.partial(_combine_kernel, tc=tc, tiles_per_seq=tiles_per_seq, n_steps=bsz * tiles_per_seq)
    tok = lambda w: pl.BlockSpec((tc, w), lambda b, i, ws: (b * tiles_per_seq + i, 0))
    return pl.pallas_call(
        kern,
        out_shape=jax.ShapeDtypeStruct((bsz, seq, D_MODEL), F32),
        grid_spec=pltpu.PrefetchScalarGridSpec(
            num_scalar_prefetch=1,
            grid=(bsz, tiles_per_seq),
            in_specs=[
                pl.BlockSpec((None, tc, D_MODEL), lambda b, i, ws: (b, i + T_PAD // tc, 0)),
                tok(4),
                pl.BlockSpec(memory_space=pl.ANY),
            ],
            out_specs=pl.BlockSpec((None, tc, D_MODEL), lambda b, i, ws: (b, i, 0)),
            scratch_shapes=[pltpu.VMEM((2, N_EXPERTS * COMBINE_WIN, D_MODEL), BF16),
                            pltpu.SemaphoreType.DMA((2,))],
        ),
        compiler_params=_params(("arbitrary", "arbitrary")),
        name="moe_combine",
    )(win_start, hs3, route, y)


def _moe(hs, gain, router_w, w_gate, w_up, w_down, bsz, tp):
    seq = tp - T_PAD
    n_tok = bsz * seq
    tg = MOE_TILE
    idx, wts, cnt = _router(hs, gain, router_w, tp)
    sel = idx.reshape(bsz, tp, 128)[:, T_PAD:, :4].reshape(n_tok, 4)
    counts = cnt[0, :N_EXPERTS].astype(jnp.int32)
    padded = ((counts + tg - 1) // tg) * tg
    ends = jnp.cumsum(padded)
    starts = ends - padded
    experts = jnp.arange(N_EXPERTS, dtype=jnp.int32)[None, :]
    slot1 = jnp.sum(jnp.where(sel[:, 0:1] == experts, starts[None, :], 0), axis=1) + sel[:, 2]
    slot2 = jnp.sum(jnp.where(sel[:, 1:2] == experts, starts[None, :], 0), axis=1) + sel[:, 3]
    slot1, slot2 = slot1.astype(jnp.int32), slot2.astype(jnp.int32)
    n_slots = 2 * n_tok + N_EXPERTS * tg
    tile_start = jnp.arange(n_slots // tg, dtype=jnp.int32)[:, None] * tg
    tile_expert = jnp.minimum(jnp.sum((tile_start >= ends[None, :]).astype(jnp.int32), axis=1), N_EXPERTS - 1)
    n_tiles_used = (ends[-1:] // tg).astype(jnp.int32)
    tail = n_slots - tg * (1 + jnp.arange(N_EXPERTS, dtype=jnp.int32))
    zero_rows = jnp.concatenate([jnp.maximum(ends - tg, 0), tail]).astype(jnp.int32)

    tc = COMBINE_TOKENS
    first = jnp.minimum(
        jnp.min(jnp.where(sel[:, 0:1] == experts, slot1[:, None], n_slots).reshape(n_tok // tc, tc, N_EXPERTS), axis=1),
        jnp.min(jnp.where(sel[:, 1:2] == experts, slot2[:, None], n_slots).reshape(n_tok // tc, tc, N_EXPERTS), axis=1))
    win_start = jnp.where(first == n_slots, 0,
                          jnp.minimum(first // WIN_ALIGN * WIN_ALIGN, n_slots - COMBINE_WIN))
    route = jnp.concatenate([sel[:, :2], slot1[:, None], slot2[:, None]], axis=1)

    hs3 = hs.reshape(bsz, tp, D_MODEL)
    xs, (wg_bf, wu_bf, wd_bf) = _dispatch(hs3, wts.reshape(bsz, tp, 128), gain, slot1, slot2, zero_rows,
                                          n_slots, tg, (w_gate, w_up, w_down))
    y = _experts(xs, tile_expert.astype(jnp.int32), n_tiles_used, wg_bf, wu_bf, wd_bf, tg)
    return _combine(hs3, route, y, win_start.reshape(-1).astype(jnp.int32), seq)


def _permute_qk_cols(w):
    return w.reshape(D_MODEL, 2, ATT_HEADS, ATT_QK_DIM).transpose(0, 2, 1, 3).reshape(D_MODEL, 512)


def kernel(x, meta_tokens, norm1_gain, norm2_gain, w_in, q_norm_gain, k_norm_gain, diff_lambda,
           attn_sub_gain, rel_bias, conv_w, hgrn_lb_logits, hgrn_out_gain, w_branch, w_out,
           ffn_w_gate, ffn_w_up, ffn_w_down, router_w, moe_w_gate, moe_w_up, moe_w_down):
    bsz, seq, _ = x.shape
    depth = w_in.shape[0]
    tp = T_PAD + seq
    assert tp % ATT_TILE == 0 and depth == 2

    head = jnp.concatenate([jnp.zeros((PAD0, D_MODEL), x.dtype), meta_tokens.astype(x.dtype)], axis=0)
    hs = jnp.concatenate([jnp.broadcast_to(head[None], (bsz, T_PAD, D_MODEL)), x], axis=1)
    hs = hs.reshape(bsz * tp, D_MODEL)

    lb_all = jnp.cumsum(jax.nn.softmax(hgrn_lb_logits.astype(F32), axis=0), axis=0)
    lb_all = lb_all - lb_all[0]
    btab = _attn_bias_tables(rel_bias, ATT_TILE)

    def permuted(w):
        return jnp.concatenate([_permute_qk_cols(w[:, :512]), _permute_qk_cols(w[:, 512:1024])], axis=1)

    later = dict(branch0=w_branch[0], out0=w_out[0], gate=ffn_w_gate[0], up=ffn_w_up[0], down=ffn_w_down[0],
                 w_in1=w_in[1], branch1=w_branch[1], out1=w_out[1])
    bf = {}
    out = None
    for layer in range(depth):
        if layer == 0:
            w = w_in[0]
            w_bf = jnp.concatenate([permuted(w), w[:, 1024:]], axis=1).astype(BF16)
        else:
            w_bf = bf["w_in1"].at[:, :1024].set(permuted(bf["w_in1"]))
        qk_gain = jnp.concatenate([jnp.tile(q_norm_gain[layer].astype(F32), 8) * (ATT_QK_DIM ** -0.5 * LOG2E),
                                   jnp.tile(k_norm_gain[layer].astype(F32), 8)]).reshape(1, COL_TILE)
        proj = _inproj(hs, norm1_gain[layer], w_bf, qk_gain)
        proj3 = proj.reshape(bsz, tp, IN_COLS)

        lam_init = 0.8 - 0.6 * math.exp(-0.3 * layer)
        lp = diff_lambda[layer].astype(F32)
        lam = jnp.exp(jnp.sum(lp[0] * lp[1])) - jnp.exp(jnp.sum(lp[2] * lp[3])) + lam_init
        u_att, cast = _diff_attention(proj3, lam.reshape(1), attn_sub_gain[layer].astype(F32), btab,
                                      1.0 - lam_init, tuple(later.values()) if layer == 0 else ())
        if layer == 0:
            bf = dict(zip(later.keys(), cast))

        lb = lb_all[layer]
        u_hgrn = _hgrn(proj3, jnp.log(lb), jnp.log1p(-lb), hgrn_out_gain[layer].astype(F32))

        hs = _merge(hs, u_att.reshape(bsz * tp, 512), u_hgrn.reshape(bsz * tp, 512), proj,
                    conv_w[layer].astype(F32), bf["branch%d" % layer], bf["out%d" % layer], tp)

        j = layer // 2
        if layer % 2 == 0:
            hs = _ffn(hs, norm2_gain[layer], bf["gate"], bf["up"], bf["down"])
        else:
            out = _moe(hs, norm2_gain[layer], router_w[j], moe_w_gate[j], moe_w_up[j], moe_w_down[j], bsz, tp)
    return out
```
